```python
import jax, jax.numpy as jnp
from jax import lax
import numpy as np

D_MODEL = 2048
BATCH = 8
SEQ = 2048
DEPTH = 1

CHUNK = 64
Q_BLOCK = 128
CONV_WIDTH = D_MODEL // 2
CONV_GROUPS = 8
CONV_K = 3
V_DIM = 128
N_HEADS = (D_MODEL - CONV_WIDTH) // V_DIM
QK_NOPE = 128
QK_ROPE = 64
Q_RANK = 768
KV_RANK = 512
ATTN_WIDTH = N_HEADS * V_DIM
MIX_WIDTH = CONV_WIDTH + ATTN_WIDTH
IN_WIDTH = 3 * CONV_WIDTH + Q_RANK + KV_RANK + QK_ROPE
D_FF = 4 * D_MODEL
ROPE_THETA = 10000.0
EPS = 1e-6
NEG_INF = -1e30

kernel_name = "hybrid_conv_mla_sandwich_block"


def rms_norm(x, g):
    x32 = x.astype(jnp.float32)
    y = x32 * lax.rsqrt(jnp.mean(x32 * x32, axis=-1, keepdims=True) + EPS)
    return (y * g.astype(jnp.float32)).astype(x.dtype)


def group_rms_norm(x, g, n_groups):
    b, s, w = x.shape
    y = rms_norm(x.reshape(b, s, n_groups, w // n_groups), g.reshape(n_groups, w // n_groups))
    return y.reshape(b, s, w)


def apply_rope(x, cos, sin):
    x32 = x.astype(jnp.float32)
    x1, x2 = jnp.split(x32, 2, axis=-1)
    return jnp.concatenate([x1 * cos - x2 * sin, x2 * cos + x1 * sin], axis=-1).astype(x.dtype)


def short_conv_causal(u, w):
    rhs = w[:, None, :].astype(u.dtype)
    return lax.conv_general_dilated(
        u, rhs, window_strides=(1,), padding=[(CONV_K - 1, 0)],
        dimension_numbers=("NWC", "WIO", "NWC"), feature_group_count=u.shape[-1])


def chunk_causal_mla(q_nope, q_rope, k_nope, k_rope, v):
    s_len = q_nope.shape[1]
    scale = (QK_NOPE + QK_ROPE) ** -0.5
    chunk_id = jnp.arange(s_len) // CHUNK
    outs = []
    for start in range(0, s_len, Q_BLOCK):
        end = start + Q_BLOCK
        s = (jnp.einsum("bqhd,bkhd->bhqk", q_nope[:, start:end], k_nope[:, :end],
                        preferred_element_type=jnp.float32)
             + jnp.einsum("bqhr,bkr->bhqk", q_rope[:, start:end], k_rope[:, :end],
                          preferred_element_type=jnp.float32)) * scale
        visible = chunk_id[None, :end] <= chunk_id[start:end, None]
        s = jnp.where(visible, s, NEG_INF)
        p = jax.nn.softmax(s, axis=-1).astype(v.dtype)
        outs.append(jnp.einsum("bhqk,bkhd->bqhd", p, v[:, :end]))
    return jnp.concatenate(outs, axis=1)


def hybrid_mixer(h, w_in, conv_w, q_norm_g, w_uq, kv_norm_g, w_ukv, conv_out_g, attn_out_g, w_o):
    b, s, _ = h.shape
    proj = h @ w_in
    cuts = np.cumsum([CONV_WIDTH, CONV_WIDTH, CONV_WIDTH, Q_RANK, KV_RANK]).tolist()
    u, gate_b, gate_c, c_q, c_kv, k_rope_raw = jnp.split(proj, cuts, axis=-1)

    y_conv = gate_b * short_conv_causal(gate_c * u, conv_w)
    y_conv = group_rms_norm(y_conv, conv_out_g, CONV_GROUPS)

    q = (rms_norm(c_q, q_norm_g) @ w_uq).reshape(b, s, N_HEADS, QK_NOPE + QK_ROPE)
    q_nope, q_rope = q[..., :QK_NOPE], q[..., QK_NOPE:]
    kv = (rms_norm(c_kv, kv_norm_g) @ w_ukv).reshape(b, s, N_HEADS, QK_NOPE + V_DIM)
    k_nope, v = kv[..., :QK_NOPE], kv[..., QK_NOPE:]
    pos = jnp.arange(s, dtype=jnp.float32)
    inv_freq = jnp.power(ROPE_THETA, -jnp.arange(0, QK_ROPE, 2, dtype=jnp.float32) / QK_ROPE)
    ang = pos[:, None] * inv_freq[None, :]
    cos, sin = jnp.cos(ang), jnp.sin(ang)
    q_rope = apply_rope(q_rope, cos[None, :, None, :], sin[None, :, None, :])
    k_rope = apply_rope(k_rope_raw, cos[None], sin[None])
    o = chunk_causal_mla(q_nope, q_rope, k_nope, k_rope, v)
    y_attn = group_rms_norm(o.reshape(b, s, ATTN_WIDTH), attn_out_g, N_HEADS)

    return jnp.concatenate([y_conv, y_attn], axis=-1) @ w_o


def _fwd_setup_inputs(seed: int = 0) -> dict:
    key = jax.random.key(seed)
    ks = jax.random.split(key, 17)

    def w(k, shape, fan_in):
        return jax.random.normal(k, shape, jnp.float32) * (fan_in ** -0.5)

    def gain(k, n):
        return 1.0 + 0.05 * jax.random.normal(k, (DEPTH, n), jnp.float32)

    return {
        "x": jax.random.normal(ks[0], (BATCH, SEQ, D_MODEL), jnp.float32),
        "pre_mix_g": gain(ks[1], D_MODEL),
        "w_in": w(ks[2], (DEPTH, D_MODEL, IN_WIDTH), D_MODEL),
        "conv_w": w(ks[3], (DEPTH, CONV_K, CONV_WIDTH), CONV_K),
        "q_norm_g": gain(ks[4], Q_RANK),
        "w_uq": w(ks[5], (DEPTH, Q_RANK, N_HEADS * (QK_NOPE + QK_ROPE)), Q_RANK),
        "kv_norm_g": gain(ks[6], KV_RANK),
        "w_ukv": w(ks[7], (DEPTH, KV_RANK, N_HEADS * (QK_NOPE + V_DIM)), KV_RANK),
        "conv_out_g": gain(ks[8], CONV_WIDTH),
        "attn_out_g": gain(ks[9], ATTN_WIDTH),
        "w_o": w(ks[10], (DEPTH, MIX_WIDTH, D_MODEL), MIX_WIDTH),
        "post_mix_g": gain(ks[11], D_MODEL),
        "pre_mlp_g": gain(ks[12], D_MODEL),
        "w_up": w(ks[13], (DEPTH, D_MODEL, D_FF), D_MODEL),
        "w_down": w(ks[14], (DEPTH, D_FF, D_MODEL), D_FF),
        "post_mlp_g": gain(ks[15], D_MODEL),
    }


def _fwd_reference(x, pre_mix_g, w_in, conv_w, q_norm_g, w_uq, kv_norm_g, w_ukv, conv_out_g,
              attn_out_g, w_o, post_mix_g, pre_mlp_g, w_up, w_down, post_mlp_g):
    for l in range(DEPTH):
        h = rms_norm(x, pre_mix_g[l])
        y = hybrid_mixer(h, w_in[l], conv_w[l], q_norm_g[l], w_uq[l], kv_norm_g[l], w_ukv[l],
                         conv_out_g[l], attn_out_g[l], w_o[l])
        x = x + rms_norm(y, post_mix_g[l])
        h = rms_norm(x, pre_mlp_g[l])
        m = jnp.square(jax.nn.relu(h @ w_up[l])) @ w_down[l]
        x = x + rms_norm(m, post_mlp_g[l])
    return x


import jax as _jax
import jax.numpy as _jnp

TWIN_FORMAT = 'train_step'
FWD_PARAMS = ['x', 'pre_mix_g', 'w_in', 'conv_w', 'q_norm_g', 'w_uq', 'kv_norm_g', 'w_ukv', 'conv_out_g', 'attn_out_g', 'w_o', 'post_mix_g', 'pre_mlp_g', 'w_up', 'w_down', 'post_mlp_g']
TWIN_WEIGHTS = ['pre_mix_g', 'w_in', 'conv_w', 'q_norm_g', 'w_uq', 'kv_norm_g', 'w_ukv', 'conv_out_g', 'attn_out_g', 'w_o', 'post_mix_g', 'pre_mlp_g', 'w_up', 'w_down', 'post_mlp_g']
TWIN_DIFF_INPUT = 'x'
TWIN_INPUTS = ['x', 'pre_mix_g', 'w_in', 'conv_w', 'q_norm_g', 'w_uq', 'kv_norm_g', 'w_ukv', 'conv_out_g', 'attn_out_g', 'w_o', 'post_mix_g', 'pre_mlp_g', 'w_up', 'w_down', 'post_mlp_g', 'loss_target', 'm_pre_mix_g', 'm_w_in', 'm_conv_w', 'm_q_norm_g', 'm_w_uq', 'm_kv_norm_g', 'm_w_ukv', 'm_conv_out_g', 'm_attn_out_g', 'm_w_o', 'm_post_mix_g', 'm_pre_mlp_g', 'm_w_up', 'm_w_down', 'm_post_mlp_g', 'v_pre_mix_g', 'v_w_in', 'v_conv_w', 'v_q_norm_g', 'v_w_uq', 'v_kv_norm_g', 'v_w_ukv', 'v_conv_out_g', 'v_attn_out_g', 'v_w_o', 'v_post_mix_g', 'v_pre_mlp_g', 'v_w_up', 'v_w_down', 'v_post_mlp_g']
TWIN_OUTPUTS = ['loss', 'grad_x', 'grad_pre_mix_g', 'grad_w_in', 'grad_conv_w', 'grad_q_norm_g', 'grad_w_uq', 'grad_kv_norm_g', 'grad_w_ukv', 'grad_conv_out_g', 'grad_attn_out_g', 'grad_w_o', 'grad_post_mix_g', 'grad_pre_mlp_g', 'grad_w_up', 'grad_w_down', 'grad_post_mlp_g', 'delta_pre_mix_g', 'delta_w_in', 'delta_conv_w', 'delta_q_norm_g', 'delta_w_uq', 'delta_kv_norm_g', 'delta_w_ukv', 'delta_conv_out_g', 'delta_attn_out_g', 'delta_w_o', 'delta_post_mix_g', 'delta_pre_mlp_g', 'delta_w_up', 'delta_w_down', 'delta_post_mlp_g', 'new_m_pre_mix_g', 'new_m_w_in', 'new_m_conv_w', 'new_m_q_norm_g', 'new_m_w_uq', 'new_m_kv_norm_g', 'new_m_w_ukv', 'new_m_conv_out_g', 'new_m_attn_out_g', 'new_m_w_o', 'new_m_post_mix_g', 'new_m_pre_mlp_g', 'new_m_w_up', 'new_m_w_down', 'new_m_post_mlp_g', 'new_v_pre_mix_g', 'new_v_w_in', 'new_v_conv_w', 'new_v_q_norm_g', 'new_v_w_uq', 'new_v_kv_norm_g', 'new_v_w_ukv', 'new_v_conv_out_g', 'new_v_attn_out_g', 'new_v_w_o', 'new_v_post_mix_g', 'new_v_pre_mlp_g', 'new_v_w_up', 'new_v_w_down', 'new_v_post_mlp_g']
TWIN_LEAF_KINDS = {'loss': 'loss', 'grad_x': 'grad_x', 'grad_pre_mix_g': 'grad_w', 'grad_w_in': 'grad_w', 'grad_conv_w': 'grad_w', 'grad_q_norm_g': 'grad_w', 'grad_w_uq': 'grad_w', 'grad_kv_norm_g': 'grad_w', 'grad_w_ukv': 'grad_w', 'grad_conv_out_g': 'grad_w', 'grad_attn_out_g': 'grad_w', 'grad_w_o': 'grad_w', 'grad_post_mix_g': 'grad_w', 'grad_pre_mlp_g': 'grad_w', 'grad_w_up': 'grad_w', 'grad_w_down': 'grad_w', 'grad_post_mlp_g': 'grad_w', 'delta_pre_mix_g': 'delta_w', 'delta_w_in': 'delta_w', 'delta_conv_w': 'delta_w', 'delta_q_norm_g': 'delta_w', 'delta_w_uq': 'delta_w', 'delta_kv_norm_g': 'delta_w', 'delta_w_ukv': 'delta_w', 'delta_conv_out_g': 'delta_w', 'delta_attn_out_g': 'delta_w', 'delta_w_o': 'delta_w', 'delta_post_mix_g': 'delta_w', 'delta_pre_mlp_g': 'delta_w', 'delta_w_up': 'delta_w', 'delta_w_down': 'delta_w', 'delta_post_mlp_g': 'delta_w', 'new_m_pre_mix_g': 'new_m', 'new_m_w_in': 'new_m', 'new_m_conv_w': 'new_m', 'new_m_q_norm_g': 'new_m', 'new_m_w_uq': 'new_m', 'new_m_kv_norm_g': 'new_m', 'new_m_w_ukv': 'new_m', 'new_m_conv_out_g': 'new_m', 'new_m_attn_out_g': 'new_m', 'new_m_w_o': 'new_m', 'new_m_post_mix_g': 'new_m', 'new_m_pre_mlp_g': 'new_m', 'new_m_w_up': 'new_m', 'new_m_w_down': 'new_m', 'new_m_post_mlp_g': 'new_m', 'new_v_pre_mix_g': 'new_v', 'new_v_w_in': 'new_v', 'new_v_conv_w': 'new_v', 'new_v_q_norm_g': 'new_v', 'new_v_w_uq': 'new_v', 'new_v_kv_norm_g': 'new_v', 'new_v_w_ukv': 'new_v', 'new_v_conv_out_g': 'new_v', 'new_v_attn_out_g': 'new_v', 'new_v_w_o': 'new_v', 'new_v_post_mix_g': 'new_v', 'new_v_pre_mlp_g': 'new_v', 'new_v_w_up': 'new_v', 'new_v_w_down': 'new_v', 'new_v_post_mlp_g': 'new_v'}


def _forward(args):
    return _fwd_reference(*[args[k] for k in FWD_PARAMS])


def _output_shape():
    out = _jax.eval_shape(lambda: _forward(_fwd_setup_inputs(0)))
    return out.shape, out.dtype

N_MICROBATCH = 1
ADAM_LR = 0.001
ADAM_B1 = 0.9
ADAM_B2 = 0.999
ADAM_EPS = 1e-08
ADAM_WD = 0.01
ADAM_STEP = 10
PER_EXAMPLE_BATCH_AXIS = {'x': 0, 'loss_target': 0}
SHARED_INPUTS = []
_WEIGHT_DTYPES = {'pre_mix_g': _jnp.float32, 'w_in': _jnp.float32, 'conv_w': _jnp.float32, 'q_norm_g': _jnp.float32, 'w_uq': _jnp.float32, 'kv_norm_g': _jnp.float32, 'w_ukv': _jnp.float32, 'conv_out_g': _jnp.float32, 'attn_out_g': _jnp.float32, 'w_o': _jnp.float32, 'post_mix_g': _jnp.float32, 'pre_mlp_g': _jnp.float32, 'w_up': _jnp.float32, 'w_down': _jnp.float32, 'post_mlp_g': _jnp.float32}
MOMENT_SCALE = {'pre_mix_g': 4.834588e-01, 'w_in': 3.360638e-01, 'conv_w': 1.630010e-01, 'q_norm_g': 1.864690e-01, 'w_uq': 1.389881e-01, 'kv_norm_g': 1.032318e+00, 'w_ukv': 4.604703e-01, 'conv_out_g': 2.050223e-01, 'attn_out_g': 6.267255e-01, 'w_o': 4.532320e-01, 'post_mix_g': 7.987011e+00, 'pre_mlp_g': 2.370537e-01, 'w_up': 1.181386e-01, 'w_down': 4.561821e-01, 'post_mlp_g': 8.248557e+00}


def _to_microbatches(a, axis):
    t = _jnp.moveaxis(a, axis, 0)
    t = t.reshape((N_MICROBATCH, t.shape[0] // N_MICROBATCH) + t.shape[1:])
    return _jnp.moveaxis(t, 1, axis + 1)


def setup_inputs(seed: int = 0) -> dict:
    inp = _fwd_setup_inputs(seed)
    key = _jax.random.fold_in(_jax.random.key(seed), 7919)
    shape, _ = _output_shape()
    out = dict(inp)
    out["loss_target"] = _jax.random.normal(_jax.random.fold_in(key, 0), shape, _jnp.float32)
    for i, name in enumerate(TWIN_WEIGHTS):
        w = inp[name].astype(_jnp.float32)
        if MOMENT_SCALE is None:
            s = _jnp.sqrt(_jnp.mean(_jnp.square(w)) + 1e-30)
        else:
            s = MOMENT_SCALE[name]
        km, kv = _jax.random.split(_jax.random.fold_in(key, i + 1))
        out[name] = w
        out["m_" + name] = s * _jax.random.normal(km, w.shape, _jnp.float32)
        out["v_" + name] = (s * s) * _jax.random.uniform(kv, w.shape, _jnp.float32, 0.5, 1.5)
    if N_MICROBATCH > 1:
        for name, axis in PER_EXAMPLE_BATCH_AXIS.items():
            out[name] = _to_microbatches(out[name], axis)
    return {'x': out['x'], 'pre_mix_g': out['pre_mix_g'], 'w_in': out['w_in'], 'conv_w': out['conv_w'], 'q_norm_g': out['q_norm_g'], 'w_uq': out['w_uq'], 'kv_norm_g': out['kv_norm_g'], 'w_ukv': out['w_ukv'], 'conv_out_g': out['conv_out_g'], 'attn_out_g': out['attn_out_g'], 'w_o': out['w_o'], 'post_mix_g': out['post_mix_g'], 'pre_mlp_g': out['pre_mlp_g'], 'w_up': out['w_up'], 'w_down': out['w_down'], 'post_mlp_g': out['post_mlp_g'], 'loss_target': out['loss_target'], 'm_pre_mix_g': out['m_pre_mix_g'], 'm_w_in': out['m_w_in'], 'm_conv_w': out['m_conv_w'], 'm_q_norm_g': out['m_q_norm_g'], 'm_w_uq': out['m_w_uq'], 'm_kv_norm_g': out['m_kv_norm_g'], 'm_w_ukv': out['m_w_ukv'], 'm_conv_out_g': out['m_conv_out_g'], 'm_attn_out_g': out['m_attn_out_g'], 'm_w_o': out['m_w_o'], 'm_post_mix_g': out['m_post_mix_g'], 'm_pre_mlp_g': out['m_pre_mlp_g'], 'm_w_up': out['m_w_up'], 'm_w_down': out['m_w_down'], 'm_post_mlp_g': out['m_post_mlp_g'], 'v_pre_mix_g': out['v_pre_mix_g'], 'v_w_in': out['v_w_in'], 'v_conv_w': out['v_conv_w'], 'v_q_norm_g': out['v_q_norm_g'], 'v_w_uq': out['v_w_uq'], 'v_kv_norm_g': out['v_kv_norm_g'], 'v_w_ukv': out['v_w_ukv'], 'v_conv_out_g': out['v_conv_out_g'], 'v_attn_out_g': out['v_attn_out_g'], 'v_w_o': out['v_w_o'], 'v_post_mix_g': out['v_post_mix_g'], 'v_pre_mlp_g': out['v_pre_mlp_g'], 'v_w_up': out['v_w_up'], 'v_w_down': out['v_w_down'], 'v_post_mlp_g': out['v_post_mlp_g']}


def _loss(weights, diff, rest, loss_target):
    with _jax.named_scope("forward"):
        args = {**rest, TWIN_DIFF_INPUT: diff, **{k: w.astype(_WEIGHT_DTYPES[k]) for k, w in weights.items()}}
        y = _forward(args)
    with _jax.named_scope("loss_head"):
        err = _jnp.square(y.astype(_jnp.float32) - loss_target)
        return 0.5 * _jnp.sum(_jnp.mean(err, axis=-1)) if err.ndim else 0.5 * err


def _adamw(w, g, m, v):
    m = ADAM_B1 * m + (1.0 - ADAM_B1) * g
    v = ADAM_B2 * v + (1.0 - ADAM_B2) * _jnp.square(g)
    m_hat = m / (1.0 - ADAM_B1 ** ADAM_STEP)
    v_hat = v / (1.0 - ADAM_B2 ** ADAM_STEP)
    delta = -ADAM_LR * (m_hat / (_jnp.sqrt(v_hat) + ADAM_EPS) + ADAM_WD * w)
    return delta, m, v


def reference(x, pre_mix_g, w_in, conv_w, q_norm_g, w_uq, kv_norm_g, w_ukv, conv_out_g, attn_out_g, w_o, post_mix_g, pre_mlp_g, w_up, w_down, post_mlp_g, loss_target, m_pre_mix_g, m_w_in, m_conv_w, m_q_norm_g, m_w_uq, m_kv_norm_g, m_w_ukv, m_conv_out_g, m_attn_out_g, m_w_o, m_post_mix_g, m_pre_mlp_g, m_w_up, m_w_down, m_post_mlp_g, v_pre_mix_g, v_w_in, v_conv_w, v_q_norm_g, v_w_uq, v_kv_norm_g, v_w_ukv, v_conv_out_g, v_attn_out_g, v_w_o, v_post_mix_g, v_pre_mlp_g, v_w_up, v_w_down, v_post_mlp_g):
    given = dict(x=x, pre_mix_g=pre_mix_g, w_in=w_in, conv_w=conv_w, q_norm_g=q_norm_g, w_uq=w_uq, kv_norm_g=kv_norm_g, w_ukv=w_ukv, conv_out_g=conv_out_g, attn_out_g=attn_out_g, w_o=w_o, post_mix_g=post_mix_g, pre_mlp_g=pre_mlp_g, w_up=w_up, w_down=w_down, post_mlp_g=post_mlp_g, loss_target=loss_target, m_pre_mix_g=m_pre_mix_g, m_w_in=m_w_in, m_conv_w=m_conv_w, m_q_norm_g=m_q_norm_g, m_w_uq=m_w_uq, m_kv_norm_g=m_kv_norm_g, m_w_ukv=m_w_ukv, m_conv_out_g=m_conv_out_g, m_attn_out_g=m_attn_out_g, m_w_o=m_w_o, m_post_mix_g=m_post_mix_g, m_pre_mlp_g=m_pre_mlp_g, m_w_up=m_w_up, m_w_down=m_w_down, m_post_mlp_g=m_post_mlp_g, v_pre_mix_g=v_pre_mix_g, v_w_in=v_w_in, v_conv_w=v_conv_w, v_q_norm_g=v_q_norm_g, v_w_uq=v_w_uq, v_kv_norm_g=v_kv_norm_g, v_w_ukv=v_w_ukv, v_conv_out_g=v_conv_out_g, v_attn_out_g=v_attn_out_g, v_w_o=v_w_o, v_post_mix_g=v_post_mix_g, v_pre_mlp_g=v_pre_mlp_g, v_w_up=v_w_up, v_w_down=v_w_down, v_post_mlp_g=v_post_mlp_g)
    weights = {n: given[n] for n in TWIN_WEIGHTS}
    shared = {n: given[n] for n in SHARED_INPUTS}
    per_example = {n: given[n] for n in ['x']}
    grad_fn = _jax.value_and_grad(_loss, argnums=(0, 1))

    def one_microbatch(ex, loss_target):
        ex = dict(ex)
        diff = ex.pop(TWIN_DIFF_INPUT)
        return grad_fn(weights, diff, {**shared, **ex}, loss_target)

    if N_MICROBATCH == 1:
        loss, (grad_w, grad_x) = one_microbatch(per_example, given["loss_target"])
    else:
        def body(carry, xs):
            loss_sum, grad_sum = carry
            l_k, (gw_k, gx_k) = one_microbatch(xs[0], xs[1])
            with _jax.named_scope("update"):
                return (loss_sum + l_k, _jax.tree.map(_jnp.add, grad_sum, gw_k)), gx_k

        init = (_jnp.zeros((), _jnp.float32), _jax.tree.map(_jnp.zeros_like, weights))
        (loss, grad_w), grad_x = _jax.lax.scan(body, init, (per_example, given["loss_target"]))
    with _jax.named_scope("update"):
        delta_w, new_m, new_v = {}, {}, {}
        for n in TWIN_WEIGHTS:
            delta_w[n], new_m[n], new_v[n] = _adamw(weights[n], grad_w[n], given["m_" + n], given["v_" + n])
    return (loss, grad_x, *[grad_w[n] for n in TWIN_WEIGHTS], *[delta_w[n] for n in TWIN_WEIGHTS],
            *[new_m[n] for n in TWIN_WEIGHTS], *[new_v[n] for n in TWIN_WEIGHTS])
```

```python
import functools

import jax
import jax.numpy as jnp
from jax import lax
from jax.experimental import pallas as pl
from jax.experimental.pallas import tpu as pltpu

EPS = 1e-6
NEG_INF = -1e30
CHUNK_SHIFT = 6
N_HEADS = 8
HEAD_PAD = 256
QK_NOPE = 128
QK_ROPE = 64
V_DIM = 128
CONV_WIDTH = 1024
Q_RANK = 768
KV_RANK = 512
ROPE_THETA = 10000.0
ATTN_SCALE = (QK_NOPE + QK_ROPE) ** -0.5
ADAM_LR, ADAM_B1, ADAM_B2, ADAM_EPS, ADAM_WD, ADAM_STEP = 0.001, 0.9, 0.999, 1e-08, 0.01, 10

COL_CQ = 3 * CONV_WIDTH
COL_KR = COL_CQ + Q_RANK
COL_CKV = 4096
IN_PAD = COL_CKV + KV_RANK
IN_WIDTH = 3 * CONV_WIDTH + Q_RANK + KV_RANK + QK_ROPE

VMEM_LIMIT_BYTES = 56 * 1024 * 1024
MESH = pl.DeviceIdType.MESH
ANY = pl.BlockSpec(memory_space=pl.ANY)

_NN = (((1,), (0,)), ((), ()))
_NT = (((1,), (1,)), ((), ()))
_TN = (((0,), (0,)), ((), ()))


def _params(sem):
    return pltpu.CompilerParams(dimension_semantics=sem, vmem_limit_bytes=VMEM_LIMIT_BYTES)


def _matmul(a, b, *, dims, mnk, tiles, name, out_dtype=jnp.float32, a_spec=None, b_spec=None,
            out_shape=None, o_spec=None, epilogue=None, extra=(), extra_specs=()):
    m, n, k = mnk
    tm, tn, tk = tiles
    assert m % tm == 0 and n % tn == 0 and k % tk == 0, (name, mnk, tiles)
    gm, gn, gk = m // tm, n // tn, k // tk
    if a_spec is None:
        a_spec = (pl.BlockSpec((tk, tm), lambda i, j, l: (l, i)) if dims is _TN
                  else pl.BlockSpec((tm, tk), lambda i, j, l: (i, l)))
    if b_spec is None:
        b_spec = (pl.BlockSpec((tn, tk), lambda i, j, l: (j, l)) if dims is _NT
                  else pl.BlockSpec((tk, tn), lambda i, j, l: (l, j)))
    if out_shape is None:
        out_shape = jax.ShapeDtypeStruct((m, n), out_dtype)
    if o_spec is None:
        o_spec = pl.BlockSpec((tm, tn), lambda i, j, l: (i, j))
    single = not isinstance(out_shape, (tuple, list))
    n_extra = len(extra)

    def body(*refs):
        a_ref, b_ref = refs[0], refs[1]
        extra_refs = refs[2:2 + n_extra]
        out_refs = refs[2 + n_extra:-1]
        acc_ref = refs[-1]
        step = pl.program_id(2)

        @pl.when(step == 0)
        def _():
            acc_ref[...] = jnp.zeros_like(acc_ref)

        acc_ref[...] += lax.dot_general(a_ref[...], b_ref[...], dims, preferred_element_type=jnp.float32)

        @pl.when(step == gk - 1)
        def _():
            if epilogue is None:
                out_refs[0][...] = acc_ref[...].astype(out_refs[0].dtype)
            else:
                epilogue(acc_ref[...], extra_refs, out_refs)

    return pl.pallas_call(
        body, name=name, grid=(gm, gn, gk),
        in_specs=[a_spec, b_spec, *extra_specs],
        out_specs=o_spec if single else list(o_spec),
        out_shape=out_shape if single else list(out_shape),
        scratch_shapes=[pltpu.VMEM((tm, tn), jnp.float32)],
        compiler_params=_params(("parallel", "parallel", "arbitrary")),
    )(a, b, *extra)


def _rstd(x):
    return lax.rsqrt(jnp.mean(x * x, axis=-1, keepdims=True) + EPS)


def _rms_bwd_rows(x, g, dy):
    r = _rstd(x)
    xn = x * r
    dyg = dy * g
    dx = r * (dyg - xn * jnp.mean(xn * dyg, axis=-1, keepdims=True))
    return dx, dy * xn


def _acc_rows(ref, rows, first):
    part = jnp.sum(rows, axis=0, keepdims=True)

    @pl.when(first)
    def _():
        ref[...] = part

    @pl.when(jnp.logical_not(first))
    def _():
        ref[...] += part


def _rms_fwd(x, g, *, width, col, tm, name):
    s = x.shape[0]

    def body(x_ref, g_ref, o_ref):
        v = x_ref[...]
        o_ref[...] = (v * _rstd(v) * g_ref[...]).astype(o_ref.dtype)

    return pl.pallas_call(
        body, name=name, grid=(s // tm,),
        in_specs=[pl.BlockSpec((tm, width), lambda i: (i, col)), pl.BlockSpec((1, width), lambda i: (0, 0))],
        out_specs=pl.BlockSpec((tm, width), lambda i: (i, 0)),
        out_shape=jax.ShapeDtypeStruct((s, width), jnp.bfloat16),
        compiler_params=_params(("parallel",)),
    )(x, g)


def _rms_bwd(x, g, dy, *, width, col, tm, name):
    s = x.shape[0]

    def body(x_ref, g_ref, dy_ref, dx_ref, dg_ref):
        dx, dgr = _rms_bwd_rows(x_ref[...], g_ref[...], dy_ref[...])
        dx_ref[...] = dx.astype(dx_ref.dtype)
        _acc_rows(dg_ref, dgr, pl.program_id(0) == 0)

    return pl.pallas_call(
        body, name=name, grid=(s // tm,),
        in_specs=[pl.BlockSpec((tm, width), lambda i: (i, col)), pl.BlockSpec((1, width), lambda i: (0, 0)),
                  pl.BlockSpec((tm, width), lambda i: (i, 0))],
        out_specs=[pl.BlockSpec((tm, width), lambda i: (i, 0)), pl.BlockSpec((1, width), lambda i: (0, 0))],
        out_shape=[jax.ShapeDtypeStruct((s, width), jnp.bfloat16), jax.ShapeDtypeStruct((1, width), jnp.float32)],
        compiler_params=_params(("arbitrary",)),
    )(x, g, dy)


def _row_specs(tm, d, n):
    return [pl.BlockSpec((tm, d), lambda i: (i, 0)) for _ in range(n)]


def _gain_specs(d, n):
    return [pl.BlockSpec((1, d), lambda i: (0, 0)) for _ in range(n)]


def _mix_residual_fwd(x, mix, g_post_mix, g_pre_mlp, *, tm):
    s, d = x.shape

    def body(x_ref, mix_ref, g1_ref, g2_ref, x2_ref, h2_ref):
        mixv = mix_ref[...]
        x2 = x_ref[...] + mixv * _rstd(mixv) * g1_ref[...]
        x2_ref[...] = x2
        h2_ref[...] = (x2 * _rstd(x2) * g2_ref[...]).astype(h2_ref.dtype)

    return pl.pallas_call(
        body, name="mix_residual_fwd", grid=(s // tm,),
        in_specs=_row_specs(tm, d, 2) + _gain_specs(d, 2),
        out_specs=_row_specs(tm, d, 2),
        out_shape=[jax.ShapeDtypeStruct((s, d), jnp.float32), jax.ShapeDtypeStruct((s, d), jnp.bfloat16)],
        compiler_params=_params(("parallel",)),
    )(x, mix, g_post_mix, g_pre_mlp)


def _loss_head(x2, mlp, target, g_post_mlp, *, tm):
    s, d = x2.shape

    def body(x2_ref, m_ref, t_ref, g_ref, dx3_ref, dm_ref, dg_ref, loss_ref):
        first = pl.program_id(0) == 0
        mv = m_ref[...]
        g = g_ref[...]
        diff = x2_ref[...] + mv * _rstd(mv) * g - t_ref[...]
        dx3 = diff * (1.0 / d)
        dx3_ref[...] = dx3
        dm, dgr = _rms_bwd_rows(mv, g, dx3)
        dm_ref[...] = dm.astype(dm_ref.dtype)
        _acc_rows(dg_ref, dgr, first)
        part = 0.5 * jnp.sum(jnp.mean(diff * diff, axis=-1, keepdims=True), axis=0, keepdims=True)
        _acc_rows(loss_ref, jnp.broadcast_to(part, (1, 128)), first)

    return pl.pallas_call(
        body, name="loss_head", grid=(s // tm,),
        in_specs=_row_specs(tm, d, 3) + _gain_specs(d, 1),
        out_specs=_row_specs(tm, d, 2) + _gain_specs(d, 1) + [pl.BlockSpec((1, 128), lambda i: (0, 0))],
        out_shape=[jax.ShapeDtypeStruct((s, d), jnp.float32), jax.ShapeDtypeStruct((s, d), jnp.bfloat16),
                   jax.ShapeDtypeStruct((1, d), jnp.float32), jax.ShapeDtypeStruct((1, 128), jnp.float32)],
        compiler_params=_params(("arbitrary",)),
    )(x2, mlp, target, g_post_mlp)


def _mix_residual_bwd(dx3, dh2, x2, mix, g_pre_mlp, g_post_mix, *, tm):
    s, d = x2.shape

    def body(dx3_ref, dh2_ref, x2_ref, mix_ref, g2_ref, g1_ref, dx2_ref, dmix_ref, dg2_ref, dg1_ref):
        first = pl.program_id(0) == 0
        d_in, dgr2 = _rms_bwd_rows(x2_ref[...], g2_ref[...], dh2_ref[...])
        dx2 = dx3_ref[...] + d_in
        dx2_ref[...] = dx2
        dmix, dgr1 = _rms_bwd_rows(mix_ref[...], g1_ref[...], dx2)
        dmix_ref[...] = dmix.astype(dmix_ref.dtype)
        _acc_rows(dg2_ref, dgr2, first)
        _acc_rows(dg1_ref, dgr1, first)

    return pl.pallas_call(
        body, name="mix_residual_bwd", grid=(s // tm,),
        in_specs=_row_specs(tm, d, 4) + _gain_specs(d, 2),
        out_specs=_row_specs(tm, d, 2) + _gain_specs(d, 2),
        out_shape=[jax.ShapeDtypeStruct((s, d), jnp.float32), jax.ShapeDtypeStruct((s, d), jnp.bfloat16),
                   jax.ShapeDtypeStruct((1, d), jnp.float32), jax.ShapeDtypeStruct((1, d), jnp.float32)],
        compiler_params=_params(("arbitrary",)),
    )(dx3, dh2, x2, mix, g_pre_mlp, g_post_mix)


def _input_bwd(dx2, dh1, x, g_pre_mix, *, tm):
    s, d = x.shape

    def body(dx2_ref, dh1_ref, x_ref, g_ref, dx_ref, dg_ref):
        d_in, dgr = _rms_bwd_rows(x_ref[...], g_ref[...], dh1_ref[...])
        dx_ref[...] = dx2_ref[...] + d_in
        _acc_rows(dg_ref, dgr, pl.program_id(0) == 0)

    return pl.pallas_call(
        body, name="input_bwd", grid=(s // tm,),
        in_specs=_row_specs(tm, d, 3) + _gain_specs(d, 1),
        out_specs=_row_specs(tm, d, 1) + _gain_specs(d, 1),
        out_shape=[jax.ShapeDtypeStruct((s, d), jnp.float32), jax.ShapeDtypeStruct((1, d), jnp.float32)],
        compiler_params=_params(("arbitrary",)),
    )(dx2, dh1, x, g_pre_mix)


def _shift_rows(z, by):
    s = z.shape[0]
    rows = lax.broadcasted_iota(jnp.int32, z.shape, 0)
    rolled = pltpu.roll(z, by % s, axis=0)
    keep = rows >= by if by > 0 else rows < s + by
    return jnp.where(keep, rolled, 0.0)


def _conv_fwd(proj, conv_w, conv_out_g):
    s = proj.shape[0]
    groups = CONV_WIDTH // 128

    def body(u_ref, gb_ref, gc_ref, w_ref, g_ref, y_ref):
        z = gc_ref[...] * u_ref[...]
        w = w_ref[...]
        conv = w[0:1, :] * _shift_rows(z, 2) + w[1:2, :] * _shift_rows(z, 1) + w[2:3, :] * z
        y = gb_ref[...] * conv
        y_ref[...] = (y * _rstd(y) * g_ref[...]).astype(y_ref.dtype)

    col = lambda base: pl.BlockSpec((s, 128), lambda j: (0, base + j))
    return pl.pallas_call(
        body, name="conv_fwd", grid=(groups,),
        in_specs=[col(0), col(groups), col(2 * groups), pl.BlockSpec((3, 128), lambda j: (0, j)),
                  pl.BlockSpec((1, 128), lambda j: (0, j))],
        out_specs=pl.BlockSpec((s, 128), lambda j: (0, j)),
        out_shape=jax.ShapeDtypeStruct((s, CONV_WIDTH), jnp.bfloat16),
        compiler_params=_params(("parallel",)),
    )(proj, proj, proj, conv_w, conv_out_g)


def _conv_bwd(proj, conv_w, conv_out_g, dycat):
    s = proj.shape[0]
    groups = CONV_WIDTH // 128

    def body(u_ref, gb_ref, gc_ref, w_ref, g_ref, dy_ref, du_ref, dgb_ref, dgc_ref, dw_ref, dg_ref):
        u, gb, gc = u_ref[...], gb_ref[...], gc_ref[...]
        w = w_ref[...]
        z = gc * u
        z1, z2 = _shift_rows(z, 1), _shift_rows(z, 2)
        conv = w[0:1, :] * z2 + w[1:2, :] * z1 + w[2:3, :] * z
        dyr, dgr = _rms_bwd_rows(gb * conv, g_ref[...], dy_ref[...])
        dg_ref[...] = jnp.sum(dgr, axis=0, keepdims=True)
        dgb_ref[...] = (dyr * conv).astype(dgb_ref.dtype)
        dconv = dyr * gb
        dw_ref[0:1, :] = jnp.sum(dconv * z2, axis=0, keepdims=True)
        dw_ref[1:2, :] = jnp.sum(dconv * z1, axis=0, keepdims=True)
        dw_ref[2:3, :] = jnp.sum(dconv * z, axis=0, keepdims=True)
        dz = w[2:3, :] * dconv + w[1:2, :] * _shift_rows(dconv, -1) + w[0:1, :] * _shift_rows(dconv, -2)
        dgc_ref[...] = (dz * u).astype(dgc_ref.dtype)
        du_ref[...] = (dz * gc).astype(du_ref.dtype)

    col = lambda base: pl.BlockSpec((s, 128), lambda j: (0, base + j))
    act = jax.ShapeDtypeStruct((s, CONV_WIDTH), jnp.bfloat16)
    return pl.pallas_call(
        body, name="conv_bwd", grid=(groups,),
        in_specs=[col(0), col(groups), col(2 * groups), pl.BlockSpec((3, 128), lambda j: (0, j)),
                  pl.BlockSpec((1, 128), lambda j: (0, j)), col(0)],
        out_specs=[col(0), col(0), col(0), pl.BlockSpec((3, 128), lambda j: (0, j)),
                   pl.BlockSpec((1, 128), lambda j: (0, j))],
        out_shape=[act, act, act, jax.ShapeDtypeStruct((3, CONV_WIDTH), jnp.float32),
                   jax.ShapeDtypeStruct((1, CONV_WIDTH), jnp.float32)],
        compiler_params=_params(("parallel",)),
    )(proj, proj, proj, conv_w, conv_out_g, dycat)


def _rope_tables(s):
    pos = jnp.arange(s, dtype=jnp.float32)
    inv_freq = jnp.power(ROPE_THETA, -jnp.arange(0, QK_ROPE, 2, dtype=jnp.float32) / QK_ROPE)
    ang = pos[:, None] * inv_freq[None, :]
    cos, sin = jnp.cos(ang), jnp.sin(ang)
    zeros = jnp.zeros((s, 128 - QK_ROPE), jnp.float32)
    return (jnp.concatenate([cos, cos, zeros], axis=1), jnp.concatenate([-sin, sin, zeros], axis=1))


def _swap_halves(x):
    lane = lax.broadcasted_iota(jnp.int32, x.shape, 1)
    swapped = jnp.where(lane < QK_ROPE // 2, pltpu.roll(x, 128 - QK_ROPE // 2, axis=1),
                        pltpu.roll(x, QK_ROPE // 2, axis=1))
    return jnp.where(lane < QK_ROPE, swapped, 0.0)


def _rope(x, cos, sin):
    return x * cos + _swap_halves(x) * sin


def _rope_transposed(d, cos, sin):
    return d * cos + _swap_halves(d * sin)


def _qk_rope_fwd(q_pad, proj, cos, sin, *, tm):
    s = q_pad.shape[0]
    wq = N_HEADS * HEAD_PAD

    def body(q_ref, kr_ref, cos_ref, sin_ref, qo_ref, kro_ref):
        c, sn = cos_ref[...], sin_ref[...]
        for h in range(N_HEADS):
            lo = h * HEAD_PAD
            qo_ref[:, lo:lo + 128] = q_ref[:, lo:lo + 128].astype(qo_ref.dtype)
            qo_ref[:, lo + 128:lo + 256] = _rope(q_ref[:, lo + 128:lo + 256], c, sn).astype(qo_ref.dtype)
        kro_ref[...] = _rope(kr_ref[...], c, sn).astype(kro_ref.dtype)

    return pl.pallas_call(
        body, name="qk_rope_fwd", grid=(s // tm,),
        in_specs=[pl.BlockSpec((tm, wq), lambda i: (i, 0)), pl.BlockSpec((tm, 128), lambda i: (i, COL_KR // 128)),
                  pl.BlockSpec((tm, 128), lambda i: (i, 0)), pl.BlockSpec((tm, 128), lambda i: (i, 0))],
        out_specs=[pl.BlockSpec((tm, wq), lambda i: (i, 0)), pl.BlockSpec((tm, 128), lambda i: (i, 0))],
        out_shape=[jax.ShapeDtypeStruct((s, wq), jnp.bfloat16), jax.ShapeDtypeStruct((s, 128), jnp.bfloat16)],
        compiler_params=_params(("parallel",)),
    )(q_pad, proj, cos, sin)


def _qk_rope_bwd(dq_pad, dk_pad, dv, cos, sin, *, tm):
    s = dq_pad.shape[0]
    wq = N_HEADS * HEAD_PAD

    def body(dq_ref, dk_ref, dv_ref, cos_ref, sin_ref, dqo_ref, dkv_ref, dkr_ref):
        c, sn = cos_ref[...], sin_ref[...]
        dkr = jnp.zeros((tm, 128), jnp.float32)
        for h in range(N_HEADS):
            lo = h * HEAD_PAD
            dqo_ref[:, lo:lo + 128] = dq_ref[:, lo:lo + 128].astype(dqo_ref.dtype)
            dqo_ref[:, lo + 128:lo + 256] = _rope_transposed(dq_ref[:, lo + 128:lo + 256], c, sn).astype(dqo_ref.dtype)
            dkv_ref[:, lo:lo + 128] = dk_ref[:, lo:lo + 128].astype(dkv_ref.dtype)
            dkv_ref[:, lo + 128:lo + 256] = dv_ref[:, h * V_DIM:(h + 1) * V_DIM].astype(dkv_ref.dtype)
            dkr = dkr + dk_ref[:, lo + 128:lo + 256]
        dkr_ref[...] = _rope_transposed(dkr, c, sn).astype(dkr_ref.dtype)

    return pl.pallas_call(
        body, name="qk_rope_bwd", grid=(s // tm,),
        in_specs=[pl.BlockSpec((tm, wq), lambda i: (i, 0)), pl.BlockSpec((tm, wq), lambda i: (i, 0)),
                  pl.BlockSpec((tm, N_HEADS * V_DIM), lambda i: (i, 0)),
                  pl.BlockSpec((tm, 128), lambda i: (i, 0)), pl.BlockSpec((tm, 128), lambda i: (i, 0))],
        out_specs=[pl.BlockSpec((tm, wq), lambda i: (i, 0)), pl.BlockSpec((tm, wq), lambda i: (i, 0)),
                   pl.BlockSpec((tm, 128), lambda i: (i, 0))],
        out_shape=[jax.ShapeDtypeStruct((s, wq), jnp.bfloat16), jax.ShapeDtypeStruct((s, wq), jnp.bfloat16),
                   jax.ShapeDtypeStruct((s, 128), jnp.bfloat16)],
        compiler_params=_params(("parallel",)),
    )(dq_pad, dk_pad, dv, cos, sin)


def _visible(q0, k0, t):
    qpos = q0 + lax.broadcasted_iota(jnp.int32, (t, t), 0)
    kpos = k0 + lax.broadcasted_iota(jnp.int32, (t, t), 1)
    return lax.shift_right_logical(kpos, CHUNK_SHIFT) <= lax.shift_right_logical(qpos, CHUNK_SHIFT)


def _attn_fwd(q, kv, kr, attn_out_g, *, t):
    s = q.shape[0]
    nq = s // t

    def body(q_ref, kn_ref, v_ref, kr_ref, g_ref, o_ref, lse_ref, y_ref, kcat_ref):
        i = pl.program_id(1)

        @pl.when(i == 0)
        def _():
            kcat_ref[:, 0:128] = kn_ref[...]
            kcat_ref[:, 128:256] = kr_ref[...]

        qv = q_ref[...]

        def step(j, carry):
            m, l, acc = carry
            k = kcat_ref[pl.ds(pl.multiple_of(j * t, t), t), :]
            v = v_ref[pl.ds(pl.multiple_of(j * t, t), t), :]
            sc = lax.dot_general(qv, k, _NT, preferred_element_type=jnp.float32) * ATTN_SCALE
            sc = jnp.where(_visible(i * t, j * t, t), sc, NEG_INF)
            m_new = jnp.maximum(m, jnp.max(sc, axis=-1, keepdims=True))
            p = jnp.exp(sc - m_new)
            alpha = jnp.exp(m - m_new)
            l = alpha * l + jnp.sum(p, axis=-1, keepdims=True)
            acc = alpha * acc + lax.dot_general(p.astype(jnp.bfloat16), v, _NN, preferred_element_type=jnp.float32)
            return m_new, l, acc

        init = (jnp.full((t, 1), NEG_INF, jnp.float32), jnp.zeros((t, 1), jnp.float32),
                jnp.zeros((t, V_DIM), jnp.float32))
        m, l, acc = lax.fori_loop(0, i + 1, step, init)
        o = acc / l
        o_ref[...] = o
        lse_ref[...] = jnp.broadcast_to(m + jnp.log(l), (t, 128))
        y_ref[...] = (o * _rstd(o) * g_ref[...]).astype(y_ref.dtype)

    head_rows = lambda w, f: pl.BlockSpec((s, w), lambda h, i: (0, f(h)))
    blk = pl.BlockSpec((t, 128), lambda h, i: (i, h))
    full = jax.ShapeDtypeStruct((s, N_HEADS * V_DIM), jnp.float32)
    return pl.pallas_call(
        body, name="attn_fwd", grid=(N_HEADS, nq),
        in_specs=[pl.BlockSpec((t, HEAD_PAD), lambda h, i: (i, h)), head_rows(128, lambda h: 2 * h),
                  head_rows(128, lambda h: 2 * h + 1), head_rows(128, lambda h: 0),
                  pl.BlockSpec((1, 128), lambda h, i: (0, h))],
        out_specs=[blk, blk, blk],
        out_shape=[full, full, jax.ShapeDtypeStruct((s, N_HEADS * V_DIM), jnp.bfloat16)],
        scratch_shapes=[pltpu.VMEM((s, HEAD_PAD), jnp.bfloat16)],
        compiler_params=_params(("arbitrary", "arbitrary")),
    )(q, kv, kv, kr, attn_out_g)


def _attn_norm_bwd(o, attn_out_g, dycat):
    s = o.shape[0]

    def body(o_ref, g_ref, dy_ref, do_ref, delta_ref, dg_ref):
        ov = o_ref[...]
        do, dgr = _rms_bwd_rows(ov, g_ref[...], dy_ref[...])
        do_ref[...] = do.astype(do_ref.dtype)
        delta_ref[...] = jnp.broadcast_to(jnp.sum(do * ov, axis=-1, keepdims=True), (s, 128))
        dg_ref[...] = jnp.sum(dgr, axis=0, keepdims=True)

    col = lambda base: pl.BlockSpec((s, 128), lambda h: (0, base + h))
    return pl.pallas_call(
        body, name="attn_norm_bwd", grid=(N_HEADS,),
        in_specs=[col(0), pl.BlockSpec((1, 128), lambda h: (0, h)), col(CONV_WIDTH // 128)],
        out_specs=[col(0), col(0), pl.BlockSpec((1, 128), lambda h: (0, h))],
        out_shape=[jax.ShapeDtypeStruct((s, N_HEADS * V_DIM), jnp.bfloat16),
                   jax.ShapeDtypeStruct((s, N_HEADS * V_DIM), jnp.float32),
                   jax.ShapeDtypeStruct((1, N_HEADS * V_DIM), jnp.float32)],
        compiler_params=_params(("parallel",)),
    )(o, attn_out_g, dycat)


def _attn_bwd(q, kv, kr, do, lse, delta, *, t):
    s = q.shape[0]
    nq = s // t

    def body(q_ref, kn_ref, v_ref, kr_ref, do_ref, lse_ref, delta_ref, dq_ref, dk_ref, dv_ref, kcat_ref):
        kcat_ref[:, 0:128] = kn_ref[...]
        kcat_ref[:, 128:256] = kr_ref[...]
        dq_ref[...] = jnp.zeros_like(dq_ref)
        dk_ref[...] = jnp.zeros_like(dk_ref)
        dv_ref[...] = jnp.zeros_like(dv_ref)

        def kv_step(j, _):
            krows = pl.ds(pl.multiple_of(j * t, t), t)
            k = kcat_ref[krows, :]
            v = v_ref[krows, :]

            def q_step(i, _):
                qrows = pl.ds(pl.multiple_of(i * t, t), t)
                qv = q_ref[qrows, :]
                dov = do_ref[qrows, :]
                sc = lax.dot_general(qv, k, _NT, preferred_element_type=jnp.float32) * ATTN_SCALE
                sc = jnp.where(_visible(i * t, j * t, t), sc, NEG_INF)
                p = jnp.exp(sc - lse_ref[qrows, :][:, 0:1])
                dp = lax.dot_general(dov, v, _NT, preferred_element_type=jnp.float32)
                ds = (p * (dp - delta_ref[qrows, :][:, 0:1]) * ATTN_SCALE).astype(jnp.bfloat16)
                dv_ref[krows, :] += lax.dot_general(p.astype(jnp.bfloat16), dov, _TN,
                                                    preferred_element_type=jnp.float32)
                dk_ref[krows, :] += lax.dot_general(ds, qv, _TN, preferred_element_type=jnp.float32)
                dq_ref[qrows, :] += lax.dot_general(ds, k, _NN, preferred_element_type=jnp.float32)
                return 0

            lax.fori_loop(j, nq, q_step, 0)
            return 0

        lax.fori_loop(0, nq, kv_step, 0)

    col = lambda w, f: pl.BlockSpec((s, w), lambda h: (0, f(h)))
    return pl.pallas_call(
        body, name="attn_bwd", grid=(N_HEADS,),
        in_specs=[col(HEAD_PAD, lambda h: h), col(128, lambda h: 2 * h), col(128, lambda h: 2 * h + 1),
                  col(128, lambda h: 0), col(128, lambda h: h), col(128, lambda h: h), col(128, lambda h: h)],
        out_specs=[col(HEAD_PAD, lambda h: h), col(HEAD_PAD, lambda h: h), col(128, lambda h: h)],
        out_shape=[jax.ShapeDtypeStruct((s, N_HEADS * HEAD_PAD), jnp.float32),
                   jax.ShapeDtypeStruct((s, N_HEADS * HEAD_PAD), jnp.float32),
                   jax.ShapeDtypeStruct((s, N_HEADS * V_DIM), jnp.float32)],
        scratch_shapes=[pltpu.VMEM((s, HEAD_PAD), jnp.bfloat16)],
        compiler_params=_params(("parallel",)),
    )(q, kv, kv, kr, do, lse, delta)


def _row_tile(rows):
    for cand in (256, 128, 64, 32, 16, 8):
        if rows % cand == 0:
            return cand
    return rows


def _pair_add(a, b, *, name):
    n, r, c = a.shape
    tr = _row_tile(r)

    def body(a_ref, b_ref, o_ref):
        o_ref[...] = (a_ref[...].astype(jnp.float32) + b_ref[...].astype(jnp.float32)).astype(o_ref.dtype)

    spec = pl.BlockSpec((None, tr, c), lambda j, i: (j, i, 0))
    return pl.pallas_call(
        body, name=name, grid=(n, r // tr), in_specs=[spec, spec], out_specs=spec,
        out_shape=jax.ShapeDtypeStruct(a.shape, jnp.bfloat16),
        compiler_params=_params(("parallel", "parallel")),
    )(a, b)


def _chip_sum(parts, *, name):
    n, r, c = parts.shape
    tr = _row_tile(r)

    def body(p0, p1, p2, p3, o_ref):
        f = lambda ref: ref[...].astype(jnp.float32)
        o_ref[...] = ((f(p0) + f(p1)) + f(p2)) + f(p3)

    specs = [pl.BlockSpec((None, tr, c), functools.partial(lambda i, k: (k, i, 0), k=k)) for k in range(4)]
    return pl.pallas_call(
        body, name=name, grid=(r // tr,), in_specs=specs,
        out_specs=pl.BlockSpec((tr, c), lambda i: (i, 0)),
        out_shape=jax.ShapeDtypeStruct((r, c), jnp.float32),
        compiler_params=_params(("parallel",)),
    )(parts, parts, parts, parts)


def _adamw(w, g, m, v, *, name):
    r, c = w.shape
    tr = _row_tile(r)

    def body(w_ref, g_ref, m_ref, v_ref, d_ref, mo_ref, vo_ref):
        gv = g_ref[...]
        mn = ADAM_B1 * m_ref[...] + (1.0 - ADAM_B1) * gv
        vn = ADAM_B2 * v_ref[...] + (1.0 - ADAM_B2) * (gv * gv)
        m_hat = mn / (1.0 - ADAM_B1 ** ADAM_STEP)
        v_hat = vn / (1.0 - ADAM_B2 ** ADAM_STEP)
        d_ref[...] = -ADAM_LR * (m_hat / (jnp.sqrt(v_hat) + ADAM_EPS) + ADAM_WD * w_ref[...])
        mo_ref[...] = mn
        vo_ref[...] = vn

    spec = pl.BlockSpec((tr, c), lambda i: (i, 0))
    out = jax.ShapeDtypeStruct((r, c), jnp.float32)
    return pl.pallas_call(
        body, name=name, grid=(r // tr,), in_specs=[spec] * 4, out_specs=[spec] * 3, out_shape=[out] * 3,
        compiler_params=_params(("parallel",)),
    )(w, g, m, v)


def _position():
    return lax.axis_index("x"), lax.axis_index("y"), lax.axis_index("c")


def _other_chips(x, y):
    return [(2 * (1 - x) + y, (1 - x, y)), (2 * x + (1 - y), (x, 1 - y)), (2 * (1 - x) + (1 - y), (1 - x, 1 - y))]


def _gather_weights(shards):
    n = len(shards)

    def body(*refs):
        src, out = refs[:n], refs[n:2 * n]
        send_sems, recv_sems, local_sems = refs[2 * n:]
        x, y, c = _position()
        me = 2 * x + y
        sibling = (x, y, 1 - c)
        chips = _other_chips(x, y)

        def half(ref, chip, which):
            r = ref.shape[1]
            return ref.at[chip, pl.ds(which * (r // 2), r // 2), :]

        local = [pltpu.make_async_copy(src[w], out[w].at[me], local_sems.at[w]) for w in range(n)]
        for cp in local:
            cp.start()

        def over_ici(w, k):
            r = src[w].shape[0]
            return pltpu.make_async_remote_copy(
                src_ref=src[w].at[pl.ds(c * (r // 2), r // 2), :], dst_ref=half(out[w], me, c),
                send_sem=send_sems.at[6 * w + k], recv_sem=recv_sems.at[6 * w + k],
                device_id=(*chips[k][1], c), device_id_type=MESH)

        def landed(w, k):
            return pltpu.make_async_remote_copy(
                src_ref=half(out[w], chips[k][0], c), dst_ref=half(out[w], chips[k][0], c),
                send_sem=send_sems.at[6 * w + k], recv_sem=recv_sems.at[6 * w + k],
                device_id=(*chips[k][1], c), device_id_type=MESH)

        def over_d2d(w, k, which):
            return pltpu.make_async_remote_copy(
                src_ref=half(out[w], chips[k][0], which), dst_ref=half(out[w], chips[k][0], which),
                send_sem=send_sems.at[6 * w + 3 + k], recv_sem=recv_sems.at[6 * w + 3 + k],
                device_id=sibling, device_id_type=MESH)

        for w in range(n):
            for k in range(3):
                over_ici(w, k).start()
        for w in range(n):
            for k in range(3):
                landed(w, k).wait_recv()
                over_d2d(w, k, c).start()
        for w in range(n):
            for k in range(3):
                over_d2d(w, k, 1 - c).wait_recv()
        for w in range(n):
            for k in range(3):
                over_ici(w, k).wait_send()
                over_d2d(w, k, c).wait_send()
            local[w].wait()

    return pl.pallas_call(
        body, name="gather_weights",
        in_specs=[ANY] * n, out_specs=[ANY] * n,
        out_shape=[jax.ShapeDtypeStruct((4, *s.shape), s.dtype) for s in shards],
        scratch_shapes=[pltpu.SemaphoreType.DMA((6 * n,)), pltpu.SemaphoreType.DMA((6 * n,)),
                        pltpu.SemaphoreType.DMA((n,))],
        compiler_params=pltpu.CompilerParams(has_side_effects=True),
    )(*shards)


def _pair_exchange(grads):
    n = len(grads)

    def body(*refs):
        src, mine, theirs = refs[:n], refs[n:2 * n], refs[2 * n:3 * n]
        send_sems, recv_sems, local_sems = refs[3 * n:]
        x, y, c = _position()

        def rows(ref, which):
            r = ref.shape[1]
            return ref.at[:, pl.ds(which * (r // 2), r // 2), :]

        remote = [pltpu.make_async_remote_copy(
            src_ref=rows(src[w], 1 - c), dst_ref=theirs[w], send_sem=send_sems.at[w], recv_sem=recv_sems.at[w],
            device_id=(x, y, 1 - c), device_id_type=MESH) for w in range(n)]
        local = [pltpu.make_async_copy(rows(src[w], c), mine[w], local_sems.at[w]) for w in range(n)]
        for w in range(n):
            remote[w].start()
            local[w].start()
        for w in range(n):
            remote[w].wait()
            local[w].wait()

    halves = [jax.ShapeDtypeStruct((4, g.shape[1] // 2, g.shape[2]), g.dtype) for g in grads]
    outs = pl.pallas_call(
        body, name="grad_pair_exchange",
        in_specs=[ANY] * n, out_specs=[ANY] * (2 * n), out_shape=halves + halves,
        scratch_shapes=[pltpu.SemaphoreType.DMA((n,)), pltpu.SemaphoreType.DMA((n,)), pltpu.SemaphoreType.DMA((n,))],
        compiler_params=pltpu.CompilerParams(has_side_effects=True),
    )(*grads)
    return outs[:n], outs[n:]


def _chip_scatter(pair_sums):
    n = len(pair_sums)

    def body(*refs):
        src, out = refs[:n], refs[n:2 * n]
        send_sems, recv_sems, local_sems = refs[2 * n:]
        x, y, c = _position()
        me = 2 * x + y
        chips = _other_chips(x, y)

        local = [pltpu.make_async_copy(src[w].at[me], out[w].at[me], local_sems.at[w]) for w in range(n)]
        remote = [[pltpu.make_async_remote_copy(
            src_ref=src[w].at[chips[k][0]], dst_ref=out[w].at[me],
            send_sem=send_sems.at[3 * w + k], recv_sem=recv_sems.at[3 * w + k],
            device_id=(*chips[k][1], c), device_id_type=MESH) for k in range(3)] for w in range(n)]
        for w in range(n):
            local[w].start()
            for k in range(3):
                remote[w][k].start()
        for w in range(n):
            for k in range(3):
                pltpu.make_async_remote_copy(
                    src_ref=src[w].at[chips[k][0]], dst_ref=out[w].at[chips[k][0]],
                    send_sem=send_sems.at[3 * w + k], recv_sem=recv_sems.at[3 * w + k],
                    device_id=(*chips[k][1], c), device_id_type=MESH).wait_recv()
        for w in range(n):
            for k in range(3):
                remote[w][k].wait_send()
            local[w].wait()

    return pl.pallas_call(
        body, name="grad_chip_scatter",
        in_specs=[ANY] * n, out_specs=[ANY] * n,
        out_shape=[jax.ShapeDtypeStruct(p.shape, p.dtype) for p in pair_sums],
        scratch_shapes=[pltpu.SemaphoreType.DMA((3 * n,)), pltpu.SemaphoreType.DMA((3 * n,)),
                        pltpu.SemaphoreType.DMA((n,))],
        compiler_params=pltpu.CompilerParams(has_side_effects=True),
    )(*pair_sums)


def _pair_share(halves):
    n = len(halves)

    def body(*refs):
        src, out = refs[:n], refs[n:2 * n]
        send_sems, recv_sems, local_sems = refs[2 * n:]
        x, y, c = _position()

        def rows(ref, which):
            r = ref.shape[0]
            return ref.at[pl.ds(which * (r // 2), r // 2), :]

        remote = [pltpu.make_async_remote_copy(
            src_ref=src[w], dst_ref=rows(out[w], c), send_sem=send_sems.at[w], recv_sem=recv_sems.at[w],
            device_id=(x, y, 1 - c), device_id_type=MESH) for w in range(n)]
        local = [pltpu.make_async_copy(src[w], rows(out[w], c), local_sems.at[w]) for w in range(n)]
        for w in range(n):
            remote[w].start()
            local[w].start()
        for w in range(n):
            remote[w].wait_send()
            pltpu.make_async_remote_copy(
                src_ref=src[w], dst_ref=rows(out[w], 1 - c), send_sem=send_sems.at[w], recv_sem=recv_sems.at[w],
                device_id=(x, y, 1 - c), device_id_type=MESH).wait_recv()
            local[w].wait()

    return pl.pallas_call(
        body, name="grad_pair_share",
        in_specs=[ANY] * n, out_specs=[ANY] * n,
        out_shape=[jax.ShapeDtypeStruct((2 * h.shape[0], h.shape[1]), h.dtype) for h in halves],
        scratch_shapes=[pltpu.SemaphoreType.DMA((n,)), pltpu.SemaphoreType.DMA((n,)), pltpu.SemaphoreType.DMA((n,))],
        compiler_params=pltpu.CompilerParams(has_side_effects=True),
    )(*halves)


def _all_reduce_small(block):
    r, c = block.shape

    def body(src_ref, out_ref, stage_ref, send_sems, recv_sems):
        x, y, cc = _position()
        me = 4 * x + 2 * y + cc
        stage_ref[me] = src_ref[...]
        flip = lambda v, on: 1 - v if on else v
        peers = [(flip(x, dx), flip(y, dy), flip(cc, dc)) for dx in (0, 1) for dy in (0, 1) for dc in (0, 1)][1:]
        copies = [pltpu.make_async_remote_copy(
            src_ref=stage_ref.at[me], dst_ref=stage_ref.at[me],
            send_sem=send_sems.at[k], recv_sem=recv_sems.at[k], device_id=peer, device_id_type=MESH)
            for k, peer in enumerate(peers)]
        for cp in copies:
            cp.start()
        for k, (px, py, pc) in enumerate(peers):
            them = 4 * px + 2 * py + pc
            pltpu.make_async_remote_copy(
                src_ref=stage_ref.at[them], dst_ref=stage_ref.at[them],
                send_sem=send_sems.at[k], recv_sem=recv_sems.at[k], device_id=(px, py, pc),
                device_id_type=MESH).wait_recv()
        for cp in copies:
            cp.wait_send()
        total = stage_ref[0]
        for d in range(1, 8):
            total = total + stage_ref[d]
        out_ref[...] = total

    return pl.pallas_call(
        body, name="all_reduce_small",
        in_specs=[pl.BlockSpec(memory_space=pltpu.VMEM)], out_specs=pl.BlockSpec(memory_space=pltpu.VMEM),
        out_shape=jax.ShapeDtypeStruct((r, c), jnp.float32),
        scratch_shapes=[pltpu.VMEM((8, r, c), jnp.float32), pltpu.SemaphoreType.DMA((7,)),
                        pltpu.SemaphoreType.DMA((7,))],
        compiler_params=pltpu.CompilerParams(has_side_effects=True),
    )(block)


def _cols_from_shards(g):
    n, r, c = g.shape
    return jnp.transpose(g, (1, 0, 2)).reshape(r, n * c)


def _cols_to_shards(w, n=4):
    r, c = w.shape
    return jnp.transpose(w.reshape(r, n, c // n), (1, 0, 2))


def _pad_w_in(full):
    d = full.shape[0]
    zeros = jnp.zeros((d, COL_CKV - COL_KR - QK_ROPE), full.dtype)
    return jnp.concatenate([full[:, :COL_KR], full[:, IN_WIDTH - QK_ROPE:], zeros,
                            full[:, COL_KR:COL_KR + KV_RANK]], axis=1)


def _unpad_w_in(padded):
    return jnp.concatenate([padded[:, :COL_KR], padded[:, COL_CKV:COL_CKV + KV_RANK],
                            padded[:, COL_KR:COL_KR + QK_ROPE]], axis=1)


def _pad_w_uq(full):
    r = full.shape[0]
    per_head = full.reshape(r, N_HEADS, QK_NOPE + QK_ROPE)
    return jnp.pad(per_head, ((0, 0), (0, 0), (0, HEAD_PAD - QK_NOPE - QK_ROPE))).reshape(r, N_HEADS * HEAD_PAD)


def _unpad_w_uq(padded):
    r = padded.shape[0]
    return padded.reshape(r, N_HEADS, HEAD_PAD)[:, :, :QK_NOPE + QK_ROPE].reshape(r, N_HEADS * (QK_NOPE + QK_ROPE))


_SMALL = ["pre_mix_g", "post_mix_g", "pre_mlp_g", "post_mlp_g", "conv_attn_g", "qkv_g", "conv_w0", "conv_w1", "conv_w2"]
SMALL_ROWS = 16


def _pack_small(d, pre_mix, post_mix, pre_mlp, post_mlp, conv_out, attn_out, q_norm, kv_norm, conv_w):
    row = lambda *parts: jnp.pad(jnp.concatenate(parts, axis=1), ((0, 0), (0, d - sum(p.shape[1] for p in parts))))
    rows = [row(pre_mix), row(post_mix), row(pre_mlp), row(post_mlp), row(conv_out, attn_out), row(q_norm, kv_norm),
            row(conv_w[0:1]), row(conv_w[1:2]), row(conv_w[2:3])]
    return jnp.pad(jnp.concatenate(rows, axis=0), ((0, SMALL_ROWS - len(rows)), (0, 0)))


def _unpack_small(p, chip):
    cw = CONV_WIDTH // 4
    conv_w = lax.dynamic_slice(p[6:9, :CONV_WIDTH], (0, chip * cw), (3, cw))
    return dict(pre_mix_g=p[0:1], post_mix_g=p[1:2], pre_mlp_g=p[2:3], post_mlp_g=p[3:4],
                conv_out_g=p[4:5, :CONV_WIDTH], attn_out_g=p[4:5, CONV_WIDTH:2 * CONV_WIDTH],
                q_norm_g=p[5:6, :Q_RANK], kv_norm_g=p[5:6, Q_RANK:Q_RANK + KV_RANK], conv_w=conv_w[None])


def kernel(x, pre_mix_g, w_in, conv_w, q_norm_g, w_uq, kv_norm_g, w_ukv, conv_out_g, attn_out_g, w_o, post_mix_g, pre_mlp_g, w_up, w_down, post_mlp_g, loss_target, m_pre_mix_g, m_w_in, m_conv_w, m_q_norm_g, m_w_uq, m_kv_norm_g, m_w_ukv, m_conv_out_g, m_attn_out_g, m_w_o, m_post_mix_g, m_pre_mlp_g, m_w_up, m_w_down, m_post_mlp_g, v_pre_mix_g, v_w_in, v_conv_w, v_q_norm_g, v_w_uq, v_kv_norm_g, v_w_ukv, v_conv_out_g, v_attn_out_g, v_w_o, v_post_mix_g, v_pre_mlp_g, v_w_up, v_w_down, v_post_mlp_g):
    bf16 = jnp.bfloat16
    s, d = x.shape[1], x.shape[2]
    d_ff = 4 * d
    chip = 2 * lax.axis_index("x") + lax.axis_index("y")
    xs = x.reshape(s, d)
    target = loss_target.reshape(s, d)
    tm = min(256, s)
    t_attn = min(256, s)
    mt = min(1024, s)

    big = dict(w_in=w_in[0], w_uq=w_uq[0], w_ukv=w_ukv[0], w_o=w_o[0], w_up=w_up[0], w_down=w_down[0])
    names = list(big)
    gathered = dict(zip(names, _gather_weights([big[k].astype(bf16) for k in names])))
    win = _pad_w_in(_cols_from_shards(gathered["w_in"]))
    wuq = _pad_w_uq(_cols_from_shards(gathered["w_uq"]))
    wukv = _cols_from_shards(gathered["w_ukv"])
    wo = gathered["w_o"].reshape(-1, d)
    wup = gathered["w_up"]
    wdown = gathered["w_down"].reshape(d_ff, d)
    ff4 = d_ff // 4

    spread = lambda a: lax.dynamic_update_slice(jnp.zeros((3, CONV_WIDTH), jnp.float32), a[0],
                                                (0, chip * (CONV_WIDTH // 4)))
    conv_w_mine = jnp.where(lax.axis_index("c") == 0, spread(conv_w), 0.0)
    conv_w_full = _all_reduce_small(jnp.pad(conv_w_mine, ((0, 5), (0, 0))))[0:3]

    h1 = _rms_fwd(xs, pre_mix_g, width=d, col=0, tm=tm, name="rms_pre_mix")
    proj = _matmul(h1, win, dims=_NN, mnk=(s, IN_PAD, d), tiles=(mt, 512, 512), name="mm_proj")
    y_conv = _conv_fwd(proj, conv_w_full, conv_out_g)
    cqn = _rms_fwd(proj, q_norm_g, width=Q_RANK, col=COL_CQ // Q_RANK, tm=tm, name="rms_q")
    ckvn = _rms_fwd(proj, kv_norm_g, width=KV_RANK, col=COL_CKV // KV_RANK, tm=tm, name="rms_kv")
    q_pad = _matmul(cqn, wuq, dims=_NN, mnk=(s, N_HEADS * HEAD_PAD, Q_RANK), tiles=(mt, 1024, Q_RANK), name="mm_q")
    kv = _matmul(ckvn, wukv, dims=_NN, mnk=(s, N_HEADS * HEAD_PAD, KV_RANK), tiles=(mt, 1024, KV_RANK),
                 name="mm_kv", out_dtype=bf16)
    cos, sin = _rope_tables(s)
    q_rot, kr_rot = _qk_rope_fwd(q_pad, proj, cos, sin, tm=tm)
    o, lse, y_attn = _attn_fwd(q_rot, kv, kr_rot, attn_out_g, t=t_attn)
    ycat = jnp.concatenate([y_conv, y_attn], axis=1)
    mix = _matmul(ycat, wo, dims=_NN, mnk=(s, d, 2 * CONV_WIDTH), tiles=(mt, 1024, 512), name="mm_out")
    x2, h2 = _mix_residual_fwd(xs, mix, post_mix_g, pre_mlp_g, tm=tm)

    def up_epilogue(acc, extra_refs, out_refs):
        r = jnp.maximum(acc, 0.0)
        out_refs[0][...] = acc.astype(bf16)
        out_refs[1][...] = (r * r).astype(bf16)

    n_ff = ff4 // 1024
    act = jax.ShapeDtypeStruct((s, d_ff), bf16)
    up, act_sq = _matmul(
        h2, wup, dims=_NN, mnk=(s, d_ff, d), tiles=(mt, 1024, 512), name="mm_up",
        b_spec=pl.BlockSpec((None, 512, 1024), lambda i, j, l: (j // n_ff, l, j % n_ff)),
        out_shape=(act, act), o_spec=(pl.BlockSpec((mt, 1024), lambda i, j, l: (i, j)),) * 2, epilogue=up_epilogue)
    mlp = _matmul(act_sq, wdown, dims=_NN, mnk=(s, d, d_ff), tiles=(mt, 1024, 512), name="mm_down")
    dx3, dmlp, dg_post_mlp, loss_part = _loss_head(x2, mlp, target, post_mlp_g, tm=tm)

    def dup_epilogue(acc, extra_refs, out_refs):
        out_refs[0][...] = (acc * (2.0 * jnp.maximum(extra_refs[0][...].astype(jnp.float32), 0.0))).astype(bf16)

    dup = _matmul(dmlp, wdown, dims=_NT, mnk=(s, d_ff, d), tiles=(mt, 1024, 512), name="mm_dact",
                  out_dtype=bf16, epilogue=dup_epilogue, extra=(up,),
                  extra_specs=(pl.BlockSpec((mt, 1024), lambda i, j, l: (i, j)),))
    g_wdown = _matmul(act_sq, dmlp, dims=_TN, mnk=(d_ff, d, s), tiles=(1024, 1024, min(512, s)), name="mm_gw_down",
                      out_dtype=bf16).reshape(4, ff4, d)
    n_ffk = ff4 // 512
    dh2 = _matmul(dup, wup, dims=_NT, mnk=(s, d, d_ff), tiles=(mt, 1024, 512), name="mm_dh2",
                  b_spec=pl.BlockSpec((None, 1024, 512), lambda i, j, l: (l // n_ffk, j, l % n_ffk)))
    g_wup = _matmul(h2, dup, dims=_TN, mnk=(d, d_ff, s), tiles=(1024, 1024, min(512, s)), name="mm_gw_up",
                    out_shape=jax.ShapeDtypeStruct((4, d, ff4), bf16),
                    o_spec=pl.BlockSpec((None, 1024, 1024), lambda i, j, l: (j // n_ff, i, j % n_ff)))
    dx2, dmix, dg_pre_mlp, dg_post_mix = _mix_residual_bwd(dx3, dh2, x2, mix, pre_mlp_g, post_mix_g, tm=tm)

    dycat = _matmul(dmix, wo, dims=_NT, mnk=(s, 2 * CONV_WIDTH, d), tiles=(mt, 1024, 512), name="mm_dycat")
    g_wo = _matmul(ycat, dmix, dims=_TN, mnk=(2 * CONV_WIDTH, d, s), tiles=(1024, 1024, min(512, s)),
                   name="mm_gw_out", out_dtype=bf16).reshape(4, CONV_WIDTH // 2, d)
    du, dgb, dgc, dg_conv_w, dg_conv_out = _conv_bwd(proj, conv_w_full, conv_out_g, dycat)
    do, delta, dg_attn_out = _attn_norm_bwd(o, attn_out_g, dycat)
    dq_pad, dk_pad, dv = _attn_bwd(q_rot, kv, kr_rot, do, lse, delta, t=t_attn)
    dq_raw, dkv, dkr = _qk_rope_bwd(dq_pad, dk_pad, dv, cos, sin, tm=tm)
    wq_cols = N_HEADS * HEAD_PAD
    g_wuq = _matmul(cqn, dq_raw, dims=_TN, mnk=(Q_RANK, wq_cols, s), tiles=(Q_RANK, 1024, min(512, s)),
                    name="mm_gw_uq", out_dtype=bf16)
    dcqn = _matmul(dq_raw, wuq, dims=_NT, mnk=(s, Q_RANK, wq_cols), tiles=(mt, Q_RANK, 512), name="mm_dcq")
    g_wukv = _matmul(ckvn, dkv, dims=_TN, mnk=(KV_RANK, wq_cols, s), tiles=(KV_RANK, 1024, min(512, s)),
                     name="mm_gw_ukv", out_dtype=bf16)
    dckvn = _matmul(dkv, wukv, dims=_NT, mnk=(s, KV_RANK, wq_cols), tiles=(mt, KV_RANK, 512), name="mm_dckv")
    dcq, dg_q_norm = _rms_bwd(proj, q_norm_g, dcqn, width=Q_RANK, col=COL_CQ // Q_RANK, tm=tm, name="rms_q_bwd")
    dckv, dg_kv_norm = _rms_bwd(proj, kv_norm_g, dckvn, width=KV_RANK, col=COL_CKV // KV_RANK, tm=tm,
                                name="rms_kv_bwd")
    dproj = jnp.concatenate([du, dgb, dgc, dcq, dkr, jnp.zeros((s, COL_CKV - COL_KR - 128), bf16), dckv], axis=1)
    dh1 = _matmul(dproj, win, dims=_NT, mnk=(s, d, IN_PAD), tiles=(mt, 1024, 512), name="mm_dh1")
    g_win = _matmul(h1, dproj, dims=_TN, mnk=(d, IN_PAD, s), tiles=(1024, 512, min(512, s)), name="mm_gw_in",
                    out_dtype=bf16)
    grad_x, dg_pre_mix = _input_bwd(dx2, dh1, xs, pre_mix_g, tm=tm)

    grads = [_cols_to_shards(_unpad_w_in(g_win)), _cols_to_shards(_unpad_w_uq(g_wuq)), _cols_to_shards(g_wukv),
             g_wo, g_wup, g_wdown]
    mine, theirs = _pair_exchange(grads)
    pair_sums = [_pair_add(a, b, name="pair_add_" + k) for k, a, b in zip(names, mine, theirs)]
    by_source = _chip_scatter(pair_sums)
    halves = [_chip_sum(p, name="chip_sum_" + k) for k, p in zip(names, by_source)]
    full_grads = dict(zip(names, _pair_share(halves)))

    small_g = _all_reduce_small(_pack_small(d, dg_pre_mix, dg_post_mix, dg_pre_mlp, dg_post_mlp, dg_conv_out,
                                            dg_attn_out, dg_q_norm, dg_kv_norm, dg_conv_w))
    pack_w = lambda cw, pre_mix, post_mix, pre_mlp, post_mlp, conv_out, attn_out, q_norm, kv_norm: _pack_small(
        d, pre_mix, post_mix, pre_mlp, post_mlp, conv_out, attn_out, q_norm, kv_norm, cw)
    small_w = pack_w(conv_w_full, pre_mix_g, post_mix_g, pre_mlp_g, post_mlp_g, conv_out_g, attn_out_g, q_norm_g,
                     kv_norm_g)
    small_m = pack_w(spread(m_conv_w), m_pre_mix_g, m_post_mix_g, m_pre_mlp_g, m_post_mlp_g, m_conv_out_g,
                     m_attn_out_g, m_q_norm_g, m_kv_norm_g)
    small_v = pack_w(spread(v_conv_w), v_pre_mix_g, v_post_mix_g, v_pre_mlp_g, v_post_mlp_g, v_conv_out_g,
                     v_attn_out_g, v_q_norm_g, v_kv_norm_g)
    small_d, small_nm, small_nv = _adamw(small_w, small_g, small_m, small_v, name="adamw_small")
    sg, sd, snm, snv = (_unpack_small(p, chip) for p in (small_g, small_d, small_nm, small_nv))

    moments = dict(w_in=(m_w_in, v_w_in), w_uq=(m_w_uq, v_w_uq), w_ukv=(m_w_ukv, v_w_ukv), w_o=(m_w_o, v_w_o),
                   w_up=(m_w_up, v_w_up), w_down=(m_w_down, v_w_down))
    gw, dw, nm, nv = {}, {}, {}, {}
    for k in names:
        g = full_grads[k]
        delta_k, nm_k, nv_k = _adamw(big[k], g, moments[k][0][0], moments[k][1][0], name="adamw_" + k)
        gw[k], dw[k], nm[k], nv[k] = g[None], delta_k[None], nm_k[None], nv_k[None]
    for src, dst in ((sg, gw), (sd, dw), (snm, nm), (snv, nv)):
        dst.update(src)

    loss = lax.psum(loss_part[0, 0], ("x", "y", "c"))
    order = ["pre_mix_g", "w_in", "conv_w", "q_norm_g", "w_uq", "kv_norm_g", "w_ukv", "conv_out_g", "attn_out_g",
             "w_o", "post_mix_g", "pre_mlp_g", "w_up", "w_down", "post_mlp_g"]
    return (loss, grad_x.reshape(1, s, d), *[gw[k] for k in order], *[dw[k] for k in order],
            *[nm[k] for k in order], *[nv[k] for k in order])
```

```python
import functools

import jax
import jax.numpy as jnp
from jax import lax
from jax.experimental import pallas as pl
from jax.experimental.pallas import tpu as pltpu

EPS = 1e-6
NEG_INF = -1e30
CHUNK_SHIFT = 6
N_HEADS = 8
HEAD_PAD = 256
QK_NOPE = 128
QK_ROPE = 64
V_DIM = 128
CONV_WIDTH = 1024
Q_RANK = 768
KV_RANK = 512
ROPE_THETA = 10000.0
ATTN_SCALE = (QK_NOPE + QK_ROPE) ** -0.5
ADAM_LR, ADAM_B1, ADAM_B2, ADAM_EPS, ADAM_WD, ADAM_STEP = 0.001, 0.9, 0.999, 1e-08, 0.01, 10

COL_CQ = 3 * CONV_WIDTH
COL_KR = COL_CQ + Q_RANK
COL_CKV = 4096
IN_PAD = COL_CKV + KV_RANK
IN_WIDTH = 3 * CONV_WIDTH + Q_RANK + KV_RANK + QK_ROPE

VMEM_LIMIT_BYTES = 56 * 1024 * 1024
MESH = pl.DeviceIdType.MESH
ANY = pl.BlockSpec(memory_space=pl.ANY)

_NN = (((1,), (0,)), ((), ()))
_NT = (((1,), (1,)), ((), ()))
_TN = (((0,), (0,)), ((), ()))


def _params(sem):
    return pltpu.CompilerParams(dimension_semantics=sem, vmem_limit_bytes=VMEM_LIMIT_BYTES)


def _matmul(a, b, *, dims, mnk, tiles, name, out_dtype=jnp.float32, a_spec=None, b_spec=None,
            out_shape=None, o_spec=None, epilogue=None, extra=(), extra_specs=()):
    m, n, k = mnk
    tm, tn, tk = tiles
    assert m % tm == 0 and n % tn == 0 and k % tk == 0, (name, mnk, tiles)
    gm, gn, gk = m // tm, n // tn, k // tk
    if a_spec is None:
        a_spec = (pl.BlockSpec((tk, tm), lambda i, j, l: (l, i)) if dims is _TN
                  else pl.BlockSpec((tm, tk), lambda i, j, l: (i, l)))
    if b_spec is None:
        b_spec = (pl.BlockSpec((tn, tk), lambda i, j, l: (j, l)) if dims is _NT
                  else pl.BlockSpec((tk, tn), lambda i, j, l: (l, j)))
    if out_shape is None:
        out_shape = jax.ShapeDtypeStruct((m, n), out_dtype)
    if o_spec is None:
        o_spec = pl.BlockSpec((tm, tn), lambda i, j, l: (i, j))
    single = not isinstance(out_shape, (tuple, list))
    n_extra = len(extra)

    def body(*refs):
        a_ref, b_ref = refs[0], refs[1]
        extra_refs = refs[2:2 + n_extra]
        out_refs = refs[2 + n_extra:-1]
        acc_ref = refs[-1]
        step = pl.program_id(2)

        @pl.when(step == 0)
        def _():
            acc_ref[...] = jnp.zeros_like(acc_ref)

        acc_ref[...] += lax.dot_general(a_ref[...], b_ref[...], dims, preferred_element_type=jnp.float32)

        @pl.when(step == gk - 1)
        def _():
            if epilogue is None:
                out_refs[0][...] = acc_ref[...].astype(out_refs[0].dtype)
            else:
                epilogue(acc_ref[...], extra_refs, out_refs)

    return pl.pallas_call(
        body, name=name, grid=(gm, gn, gk),
        in_specs=[a_spec, b_spec, *extra_specs],
        out_specs=o_spec if single else list(o_spec),
        out_shape=out_shape if single else list(out_shape),
        scratch_shapes=[pltpu.VMEM((tm, tn), jnp.float32)],
        compiler_params=_params(("parallel", "parallel", "arbitrary")),
    )(a, b, *extra)


def _rstd(x):
    return lax.rsqrt(jnp.mean(x * x, axis=-1, keepdims=True) + EPS)


def _rms_bwd_rows(x, g, dy):
    r = _rstd(x)
    xn = x * r
    dyg = dy * g
    dx = r * (dyg - xn * jnp.mean(xn * dyg, axis=-1, keepdims=True))
    return dx, dy * xn


def _acc_rows(ref, rows, first):
    part = jnp.sum(rows, axis=0, keepdims=True)

    @pl.when(first)
    def _():
        ref[...] = part

    @pl.when(jnp.logical_not(first))
    def _():
        ref[...] += part


def _rms_fwd(x, g, *, width, col, tm, name):
    s = x.shape[0]

    def body(x_ref, g_ref, o_ref):
        v = x_ref[...]
        o_ref[...] = (v * _rstd(v) * g_ref[...]).astype(o_ref.dtype)

    return pl.pallas_call(
        body, name=name, grid=(s // tm,),
        in_specs=[pl.BlockSpec((tm, width), lambda i: (i, col)), pl.BlockSpec((1, width), lambda i: (0, 0))],
        out_specs=pl.BlockSpec((tm, width), lambda i: (i, 0)),
        out_shape=jax.ShapeDtypeStruct((s, width), jnp.bfloat16),
        compiler_params=_params(("parallel",)),
    )(x, g)


def _rms_bwd(x, g, dy, *, width, col, tm, name):
    s = x.shape[0]

    def body(x_ref, g_ref, dy_ref, dx_ref, dg_ref):
        dx, dgr = _rms_bwd_rows(x_ref[...], g_ref[...], dy_ref[...])
        dx_ref[...] = dx.astype(dx_ref.dtype)
        _acc_rows(dg_ref, dgr, pl.program_id(0) == 0)

    return pl.pallas_call(
        body, name=name, grid=(s // tm,),
        in_specs=[pl.BlockSpec((tm, width), lambda i: (i, col)), pl.BlockSpec((1, width), lambda i: (0, 0)),
                  pl.BlockSpec((tm, width), lambda i: (i, 0))],
        out_specs=[pl.BlockSpec((tm, width), lambda i: (i, 0)), pl.BlockSpec((1, width), lambda i: (0, 0))],
        out_shape=[jax.ShapeDtypeStruct((s, width), jnp.bfloat16), jax.ShapeDtypeStruct((1, width), jnp.float32)],
        compiler_params=_params(("arbitrary",)),
    )(x, g, dy)


def _row_specs(tm, d, n):
    return [pl.BlockSpec((tm, d), lambda i: (i, 0)) for _ in range(n)]


def _gain_specs(d, n):
    return [pl.BlockSpec((1, d), lambda i: (0, 0)) for _ in range(n)]


def _mix_residual_fwd(x, mix, g_post_mix, g_pre_mlp, *, tm):
    s, d = x.shape

    def body(x_ref, mix_ref, g1_ref, g2_ref, x2_ref, h2_ref):
        mixv = mix_ref[...]
        x2 = x_ref[...] + mixv * _rstd(mixv) * g1_ref[...]
        x2_ref[...] = x2
        h2_ref[...] = (x2 * _rstd(x2) * g2_ref[...]).astype(h2_ref.dtype)

    return pl.pallas_call(
        body, name="mix_residual_fwd", grid=(s // tm,),
        in_specs=_row_specs(tm, d, 2) + _gain_specs(d, 2),
        out_specs=_row_specs(tm, d, 2),
        out_shape=[jax.ShapeDtypeStruct((s, d), jnp.float32), jax.ShapeDtypeStruct((s, d), jnp.bfloat16)],
        compiler_params=_params(("parallel",)),
    )(x, mix, g_post_mix, g_pre_mlp)


def _loss_head(x2, mlp, target, g_post_mlp, *, tm):
    s, d = x2.shape

    def body(x2_ref, m_ref, t_ref, g_ref, dx3_ref, dm_ref, dg_ref, loss_ref):
        first = pl.program_id(0) == 0
        mv = m_ref[...]
        g = g_ref[...]
        diff = x2_ref[...] + mv * _rstd(mv) * g - t_ref[...]
        dx3 = diff * (1.0 / d)
        dx3_ref[...] = dx3
        dm, dgr = _rms_bwd_rows(mv, g, dx3)
        dm_ref[...] = dm.astype(dm_ref.dtype)
        _acc_rows(dg_ref, dgr, first)
        part = 0.5 * jnp.sum(jnp.mean(diff * diff, axis=-1, keepdims=True), axis=0, keepdims=True)
        _acc_rows(loss_ref, jnp.broadcast_to(part, (1, 128)), first)

    return pl.pallas_call(
        body, name="loss_head", grid=(s // tm,),
        in_specs=_row_specs(tm, d, 3) + _gain_specs(d, 1),
        out_specs=_row_specs(tm, d, 2) + _gain_specs(d, 1) + [pl.BlockSpec((1, 128), lambda i: (0, 0))],
        out_shape=[jax.ShapeDtypeStruct((s, d), jnp.float32), jax.ShapeDtypeStruct((s, d), jnp.bfloat16),
                   jax.ShapeDtypeStruct((1, d), jnp.float32), jax.ShapeDtypeStruct((1, 128), jnp.float32)],
        compiler_params=_params(("arbitrary",)),
    )(x2, mlp, target, g_post_mlp)


def _mix_residual_bwd(dx3, dh2, x2, mix, g_pre_mlp, g_post_mix, *, tm):
    s, d = x2.shape

    def body(dx3_ref, dh2_ref, x2_ref, mix_ref, g2_ref, g1_ref, dx2_ref, dmix_ref, dg2_ref, dg1_ref):
        first = pl.program_id(0) == 0
        d_in, dgr2 = _rms_bwd_rows(x2_ref[...], g2_ref[...], dh2_ref[...])
        dx2 = dx3_ref[...] + d_in
        dx2_ref[...] = dx2
        dmix, dgr1 = _rms_bwd_rows(mix_ref[...], g1_ref[...], dx2)
        dmix_ref[...] = dmix.astype(dmix_ref.dtype)
        _acc_rows(dg2_ref, dgr2, first)
        _acc_rows(dg1_ref, dgr1, first)

    return pl.pallas_call(
        body, name="mix_residual_bwd", grid=(s // tm,),
        in_specs=_row_specs(tm, d, 4) + _gain_specs(d, 2),
        out_specs=_row_specs(tm, d, 2) + _gain_specs(d, 2),
        out_shape=[jax.ShapeDtypeStruct((s, d), jnp.float32), jax.ShapeDtypeStruct((s, d), jnp.bfloat16),
                   jax.ShapeDtypeStruct((1, d), jnp.float32), jax.ShapeDtypeStruct((1, d), jnp.float32)],
        compiler_params=_params(("arbitrary",)),
    )(dx3, dh2, x2, mix, g_pre_mlp, g_post_mix)


def _input_bwd(dx2, dh1, x, g_pre_mix, *, tm):
    s, d = x.shape

    def body(dx2_ref, dh1_ref, x_ref, g_ref, dx_ref, dg_ref):
        d_in, dgr = _rms_bwd_rows(x_ref[...], g_ref[...], dh1_ref[...])
        dx_ref[...] = dx2_ref[...] + d_in
        _acc_rows(dg_ref, dgr, pl.program_id(0) == 0)

    return pl.pallas_call(
        body, name="input_bwd", grid=(s // tm,),
        in_specs=_row_specs(tm, d, 3) + _gain_specs(d, 1),
        out_specs=_row_specs(tm, d, 1) + _gain_specs(d, 1),
        out_shape=[jax.ShapeDtypeStruct((s, d), jnp.float32), jax.ShapeDtypeStruct((1, d), jnp.float32)],
        compiler_params=_params(("arbitrary",)),
    )(dx2, dh1, x, g_pre_mix)


def _shift_rows(z, by):
    s = z.shape[0]
    rows = lax.broadcasted_iota(jnp.int32, z.shape, 0)
    rolled = pltpu.roll(z, by % s, axis=0)
    keep = rows >= by if by > 0 else rows < s + by
    return jnp.where(keep, rolled, 0.0)


def _conv_fwd(proj, conv_w, conv_out_g):
    s = proj.shape[0]
    groups = CONV_WIDTH // 128

    def body(u_ref, gb_ref, gc_ref, w_ref, g_ref, y_ref):
        z = gc_ref[...] * u_ref[...]
        w = w_ref[...]
        conv = w[0:1, :] * _shift_rows(z, 2) + w[1:2, :] * _shift_rows(z, 1) + w[2:3, :] * z
        y = gb_ref[...] * conv
        y_ref[...] = (y * _rstd(y) * g_ref[...]).astype(y_ref.dtype)

    col = lambda base: pl.BlockSpec((s, 128), lambda j: (0, base + j))
    return pl.pallas_call(
        body, name="conv_fwd", grid=(groups,),
        in_specs=[col(0), col(groups), col(2 * groups), pl.BlockSpec((3, 128), lambda j: (0, j)),
                  pl.BlockSpec((1, 128), lambda j: (0, j))],
        out_specs=pl.BlockSpec((s, 128), lambda j: (0, j)),
        out_shape=jax.ShapeDtypeStruct((s, CONV_WIDTH), jnp.bfloat16),
        compiler_params=_params(("parallel",)),
    )(proj, proj, proj, conv_w, conv_out_g)


def _conv_bwd(proj, conv_w, conv_out_g, dycat):
    s = proj.shape[0]
    groups = CONV_WIDTH // 128

    def body(u_ref, gb_ref, gc_ref, w_ref, g_ref, dy_ref, du_ref, dgb_ref, dgc_ref, dw_ref, dg_ref):
        u, gb, gc = u_ref[...], gb_ref[...], gc_ref[...]
        w = w_ref[...]
        z = gc * u
        z1, z2 = _shift_rows(z, 1), _shift_rows(z, 2)
        conv = w[0:1, :] * z2 + w[1:2, :] * z1 + w[2:3, :] * z
        dyr, dgr = _rms_bwd_rows(gb * conv, g_ref[...], dy_ref[...])
        dg_ref[...] = jnp.sum(dgr, axis=0, keepdims=True)
        dgb_ref[...] = (dyr * conv).astype(dgb_ref.dtype)
        dconv = dyr * gb
        dw_ref[0:1, :] = jnp.sum(dconv * z2, axis=0, keepdims=True)
        dw_ref[1:2, :] = jnp.sum(dconv * z1, axis=0, keepdims=True)
        dw_ref[2:3, :] = jnp.sum(dconv * z, axis=0, keepdims=True)
        dz = w[2:3, :] * dconv + w[1:2, :] * _shift_rows(dconv, -1) + w[0:1, :] * _shift_rows(dconv, -2)
        dgc_ref[...] = (dz * u).astype(dgc_ref.dtype)
        du_ref[...] = (dz * gc).astype(du_ref.dtype)

    col = lambda base: pl.BlockSpec((s, 128), lambda j: (0, base + j))
    act = jax.ShapeDtypeStruct((s, CONV_WIDTH), jnp.bfloat16)
    return pl.pallas_call(
        body, name="conv_bwd", grid=(groups,),
        in_specs=[col(0), col(groups), col(2 * groups), pl.BlockSpec((3, 128), lambda j: (0, j)),
                  pl.BlockSpec((1, 128), lambda j: (0, j)), col(0)],
        out_specs=[col(0), col(0), col(0), pl.BlockSpec((3, 128), lambda j: (0, j)),
                   pl.BlockSpec((1, 128), lambda j: (0, j))],
        out_shape=[act, act, act, jax.ShapeDtypeStruct((3, CONV_WIDTH), jnp.float32),
                   jax.ShapeDtypeStruct((1, CONV_WIDTH), jnp.float32)],
        compiler_params=_params(("parallel",)),
    )(proj, proj, proj, conv_w, conv_out_g, dycat)


def _rope_tables(s):
    pos = jnp.arange(s, dtype=jnp.float32)
    inv_freq = jnp.power(ROPE_THETA, -jnp.arange(0, QK_ROPE, 2, dtype=jnp.float32) / QK_ROPE)
    ang = pos[:, None] * inv_freq[None, :]
    cos, sin = jnp.cos(ang), jnp.sin(ang)
    zeros = jnp.zeros((s, 128 - QK_ROPE), jnp.float32)
    return (jnp.concatenate([cos, cos, zeros], axis=1), jnp.concatenate([-sin, sin, zeros], axis=1))


def _swap_halves(x):
    lane = lax.broadcasted_iota(jnp.int32, x.shape, 1)
    swapped = jnp.where(lane < QK_ROPE // 2, pltpu.roll(x, 128 - QK_ROPE // 2, axis=1),
                        pltpu.roll(x, QK_ROPE // 2, axis=1))
    return jnp.where(lane < QK_ROPE, swapped, 0.0)


def _rope(x, cos, sin):
    return x * cos + _swap_halves(x) * sin


def _rope_transposed(d, cos, sin):
    return d * cos + _swap_halves(d * sin)


def _qk_rope_fwd(q_pad, proj, cos, sin, *, tm):
    s = q_pad.shape[0]
    wq = N_HEADS * HEAD_PAD

    def body(q_ref, kr_ref, cos_ref, sin_ref, qo_ref, kro_ref):
        c, sn = cos_ref[...], sin_ref[...]
        for h in range(N_HEADS):
            lo = h * HEAD_PAD
            qo_ref[:, lo:lo + 128] = q_ref[:, lo:lo + 128].astype(qo_ref.dtype)
            qo_ref[:, lo + 128:lo + 256] = _rope(q_ref[:, lo + 128:lo + 256], c, sn).astype(qo_ref.dtype)
        kro_ref[...] = _rope(kr_ref[...], c, sn).astype(kro_ref.dtype)

    return pl.pallas_call(
        body, name="qk_rope_fwd", grid=(s // tm,),
        in_specs=[pl.BlockSpec((tm, wq), lambda i: (i, 0)), pl.BlockSpec((tm, 128), lambda i: (i, COL_KR // 128)),
                  pl.BlockSpec((tm, 128), lambda i: (i, 0)), pl.BlockSpec((tm, 128), lambda i: (i, 0))],
        out_specs=[pl.BlockSpec((tm, wq), lambda i: (i, 0)), pl.BlockSpec((tm, 128), lambda i: (i, 0))],
        out_shape=[jax.ShapeDtypeStruct((s, wq), jnp.bfloat16), jax.ShapeDtypeStruct((s, 128), jnp.bfloat16)],
        compiler_params=_params(("parallel",)),
    )(q_pad, proj, cos, sin)


def _qk_rope_bwd(dq_pad, dk_pad, dv, cos, sin, *, tm):
    s = dq_pad.shape[0]
    wq = N_HEADS * HEAD_PAD

    def body(dq_ref, dk_ref, dv_ref, cos_ref, sin_ref, dqo_ref, dkv_ref, dkr_ref):
        c, sn = cos_ref[...], sin_ref[...]
        dkr = jnp.zeros((tm, 128), jnp.float32)
        for h in range(N_HEADS):
            lo = h * HEAD_PAD
            dqo_ref[:, lo:lo + 128] = dq_ref[:, lo:lo + 128].astype(dqo_ref.dtype)
            dqo_ref[:, lo + 128:lo + 256] = _rope_transposed(dq_ref[:, lo + 128:lo + 256], c, sn).astype(dqo_ref.dtype)
            dkv_ref[:, lo:lo + 128] = dk_ref[:, lo:lo + 128].astype(dkv_ref.dtype)
            dkv_ref[:, lo + 128:lo + 256] = dv_ref[:, h * V_DIM:(h + 1) * V_DIM].astype(dkv_ref.dtype)
            dkr = dkr + dk_ref[:, lo + 128:lo + 256]
        dkr_ref[...] = _rope_transposed(dkr, c, sn).astype(dkr_ref.dtype)

    return pl.pallas_call(
        body, name="qk_rope_bwd", grid=(s // tm,),
        in_specs=[pl.BlockSpec((tm, wq), lambda i: (i, 0)), pl.BlockSpec((tm, wq), lambda i: (i, 0)),
                  pl.BlockSpec((tm, N_HEADS * V_DIM), lambda i: (i, 0)),
                  pl.BlockSpec((tm, 128), lambda i: (i, 0)), pl.BlockSpec((tm, 128), lambda i: (i, 0))],
        out_specs=[pl.BlockSpec((tm, wq), lambda i: (i, 0)), pl.BlockSpec((tm, wq), lambda i: (i, 0)),
                   pl.BlockSpec((tm, 128), lambda i: (i, 0))],
        out_shape=[jax.ShapeDtypeStruct((s, wq), jnp.bfloat16), jax.ShapeDtypeStruct((s, wq), jnp.bfloat16),
                   jax.ShapeDtypeStruct((s, 128), jnp.bfloat16)],
        compiler_params=_params(("parallel",)),
    )(dq_pad, dk_pad, dv, cos, sin)


def _visible(q0, k0, t):
    qpos = q0 + lax.broadcasted_iota(jnp.int32, (t, t), 0)
    kpos = k0 + lax.broadcasted_iota(jnp.int32, (t, t), 1)
    return lax.shift_right_logical(kpos, CHUNK_SHIFT) <= lax.shift_right_logical(qpos, CHUNK_SHIFT)


def _attn_fwd(q, kv, kr, attn_out_g, *, t):
    s = q.shape[0]
    nq = s // t

    def body(q_ref, kn_ref, v_ref, kr_ref, g_ref, o_ref, lse_ref, y_ref, kcat_ref):
        i = pl.program_id(1)

        @pl.when(i == 0)
        def _():
            kcat_ref[:, 0:128] = kn_ref[...]
            kcat_ref[:, 128:256] = kr_ref[...]

        qv = q_ref[...]

        def step(j, carry):
            m, l, acc = carry
            k = kcat_ref[pl.ds(pl.multiple_of(j * t, t), t), :]
            v = v_ref[pl.ds(pl.multiple_of(j * t, t), t), :]
            sc = lax.dot_general(qv, k, _NT, preferred_element_type=jnp.float32) * ATTN_SCALE
            sc = jnp.where(_visible(i * t, j * t, t), sc, NEG_INF)
            m_new = jnp.maximum(m, jnp.max(sc, axis=-1, keepdims=True))
            p = jnp.exp(sc - m_new)
            alpha = jnp.exp(m - m_new)
            l = alpha * l + jnp.sum(p, axis=-1, keepdims=True)
            acc = alpha * acc + lax.dot_general(p.astype(jnp.bfloat16), v, _NN, preferred_element_type=jnp.float32)
            return m_new, l, acc

        init = (jnp.full((t, 1), NEG_INF, jnp.float32), jnp.zeros((t, 1), jnp.float32),
                jnp.zeros((t, V_DIM), jnp.float32))
        m, l, acc = lax.fori_loop(0, i + 1, step, init)
        o = acc / l
        o_ref[...] = o
        lse_ref[...] = jnp.broadcast_to(m + jnp.log(l), (t, 128))
        y_ref[...] = (o * _rstd(o) * g_ref[...]).astype(y_ref.dtype)

    head_rows = lambda w, f: pl.BlockSpec((s, w), lambda h, i: (0, f(h)))
    blk = pl.BlockSpec((t, 128), lambda h, i: (i, h))
    full = jax.ShapeDtypeStruct((s, N_HEADS * V_DIM), jnp.float32)
    return pl.pallas_call(
        body, name="attn_fwd", grid=(N_HEADS, nq),
        in_specs=[pl.BlockSpec((t, HEAD_PAD), lambda h, i: (i, h)), head_rows(128, lambda h: 2 * h),
                  head_rows(128, lambda h: 2 * h + 1), head_rows(128, lambda h: 0),
                  pl.BlockSpec((1, 128), lambda h, i: (0, h))],
        out_specs=[blk, blk, blk],
        out_shape=[full, full, jax.ShapeDtypeStruct((s, N_HEADS * V_DIM), jnp.bfloat16)],
        scratch_shapes=[pltpu.VMEM((s, HEAD_PAD), jnp.bfloat16)],
        compiler_params=_params(("arbitrary", "arbitrary")),
    )(q, kv, kv, kr, attn_out_g)


def _attn_norm_bwd(o, attn_out_g, dycat):
    s = o.shape[0]

    def body(o_ref, g_ref, dy_ref, do_ref, delta_ref, dg_ref):
        ov = o_ref[...]
        do, dgr = _rms_bwd_rows(ov, g_ref[...], dy_ref[...])
        do_ref[...] = do.astype(do_ref.dtype)
        delta_ref[...] = jnp.broadcast_to(jnp.sum(do * ov, axis=-1, keepdims=True), (s, 128))
        dg_ref[...] = jnp.sum(dgr, axis=0, keepdims=True)

    col = lambda base: pl.BlockSpec((s, 128), lambda h: (0, base + h))
    return pl.pallas_call(
        body, name="attn_norm_bwd", grid=(N_HEADS,),
        in_specs=[col(0), pl.BlockSpec((1, 128), lambda h: (0, h)), col(CONV_WIDTH // 128)],
        out_specs=[col(0), col(0), pl.BlockSpec((1, 128), lambda h: (0, h))],
        out_shape=[jax.ShapeDtypeStruct((s, N_HEADS * V_DIM), jnp.bfloat16),
                   jax.ShapeDtypeStruct((s, N_HEADS * V_DIM), jnp.float32),
                   jax.ShapeDtypeStruct((1, N_HEADS * V_DIM), jnp.float32)],
        compiler_params=_params(("parallel",)),
    )(o, attn_out_g, dycat)


def _attn_bwd(q, kv, kr, do, lse, delta, *, t):
    s = q.shape[0]
    nq = s // t

    def body(q_ref, kn_ref, v_ref, kr_ref, do_ref, lse_ref, delta_ref, dq_ref, dk_ref, dv_ref, kcat_ref):
        kcat_ref[:, 0:128] = kn_ref[...]
        kcat_ref[:, 128:256] = kr_ref[...]
        dq_ref[...] = jnp.zeros_like(dq_ref)
        dk_ref[...] = jnp.zeros_like(dk_ref)
        dv_ref[...] = jnp.zeros_like(dv_ref)

        def kv_step(j, _):
            krows = pl.ds(pl.multiple_of(j * t, t), t)
            k = kcat_ref[krows, :]
            v = v_ref[krows, :]

            def q_step(i, _):
                qrows = pl.ds(pl.multiple_of(i * t, t), t)
                qv = q_ref[qrows, :]
                dov = do_ref[qrows, :]
                sc = lax.dot_general(qv, k, _NT, preferred_element_type=jnp.float32) * ATTN_SCALE
                sc = jnp.where(_visible(i * t, j * t, t), sc, NEG_INF)
                p = jnp.exp(sc - lse_ref[qrows, :][:, 0:1])
                dp = lax.dot_general(dov, v, _NT, preferred_element_type=jnp.float32)
                ds = (p * (dp - delta_ref[qrows, :][:, 0:1]) * ATTN_SCALE).astype(jnp.bfloat16)
                dv_ref[krows, :] += lax.dot_general(p.astype(jnp.bfloat16), dov, _TN,
                                                    preferred_element_type=jnp.float32)
                dk_ref[krows, :] += lax.dot_general(ds, qv, _TN, preferred_element_type=jnp.float32)
                dq_ref[qrows, :] += lax.dot_general(ds, k, _NN, preferred_element_type=jnp.float32)
                return 0

            lax.fori_loop(j, nq, q_step, 0)
            return 0

        lax.fori_loop(0, nq, kv_step, 0)

    col = lambda w, f: pl.BlockSpec((s, w), lambda h: (0, f(h)))
    return pl.pallas_call(
        body, name="attn_bwd", grid=(N_HEADS,),
        in_specs=[col(HEAD_PAD, lambda h: h), col(128, lambda h: 2 * h), col(128, lambda h: 2 * h + 1),
                  col(128, lambda h: 0), col(128, lambda h: h), col(128, lambda h: h), col(128, lambda h: h)],
        out_specs=[col(HEAD_PAD, lambda h: h), col(HEAD_PAD, lambda h: h), col(128, lambda h: h)],
        out_shape=[jax.ShapeDtypeStruct((s, N_HEADS * HEAD_PAD), jnp.float32),
                   jax.ShapeDtypeStruct((s, N_HEADS * HEAD_PAD), jnp.float32),
                   jax.ShapeDtypeStruct((s, N_HEADS * V_DIM), jnp.float32)],
        scratch_shapes=[pltpu.VMEM((s, HEAD_PAD), jnp.bfloat16)],
        compiler_params=_params(("parallel",)),
    )(q, kv, kv, kr, do, lse, delta)


def _row_tile(rows):
    for cand in (256, 128, 64, 32, 16, 8):
        if rows % cand == 0:
            return cand
    return rows


def _cast_into_slot(w, pos, *, name):
    r, c = w.shape
    tr = _row_tile(r)

    def body(pos_ref, w_ref, o_ref):
        o_ref[...] = w_ref[...].astype(o_ref.dtype)

    return pl.pallas_call(
        body, name=name, out_shape=jax.ShapeDtypeStruct((4, r, c), jnp.bfloat16),
        grid_spec=pltpu.PrefetchScalarGridSpec(
            num_scalar_prefetch=1, grid=(r // tr,),
            in_specs=[pl.BlockSpec((tr, c), lambda i, p: (i, 0))],
            out_specs=pl.BlockSpec((None, tr, c), lambda i, p: (p[1], i, 0))),
        compiler_params=_params(("parallel",)),
    )(pos, w)


def _pair_add(g, theirs, pos, *, name):
    n, h, c = theirs.shape
    tr = _row_tile(h)
    nb = h // tr

    def body(pos_ref, a_ref, b_ref, o_ref):
        o_ref[...] = (a_ref[...].astype(jnp.float32) + b_ref[...].astype(jnp.float32)).astype(o_ref.dtype)

    spec = pl.BlockSpec((None, tr, c), lambda j, i, p: (j, i, 0))
    return pl.pallas_call(
        body, name=name, out_shape=jax.ShapeDtypeStruct(theirs.shape, jnp.bfloat16),
        grid_spec=pltpu.PrefetchScalarGridSpec(
            num_scalar_prefetch=1, grid=(n, nb),
            in_specs=[pl.BlockSpec((None, tr, c), lambda j, i, p: (j, i + p[0] * nb, 0)), spec],
            out_specs=spec),
        compiler_params=_params(("parallel", "parallel")),
    )(pos, g, theirs)


def _chip_sum(by_source, pair_sum, pos, *, name):
    n, h, c = by_source.shape
    tr = _row_tile(h)
    nb = h // tr

    def body(pos_ref, p0, p1, p2, p3, own_ref, o_ref):
        own = own_ref[...].astype(jnp.float32)
        term = lambda k, ref: jnp.where(pos_ref[1] == k, own, ref[...].astype(jnp.float32))
        o_ref[...] = ((term(0, p0) + term(1, p1)) + term(2, p2)) + term(3, p3)

    def source(k):
        return pl.BlockSpec((None, tr, c), lambda i, p: (jnp.where(p[1] == k, (k + 1) % 4, k), i, 0))

    return pl.pallas_call(
        body, name=name, out_shape=jax.ShapeDtypeStruct((2 * h, c), jnp.float32),
        grid_spec=pltpu.PrefetchScalarGridSpec(
            num_scalar_prefetch=1, grid=(nb,),
            in_specs=[source(0), source(1), source(2), source(3),
                      pl.BlockSpec((None, tr, c), lambda i, p: (p[1], i, 0))],
            out_specs=pl.BlockSpec((tr, c), lambda i, p: (i + p[0] * nb, 0))),
        compiler_params=_params(("parallel",)),
    )(pos, by_source, by_source, by_source, by_source, pair_sum)


def _adamw(w, g, m, v, *, name):
    r, c = w.shape
    tr = _row_tile(r)

    def body(w_ref, g_ref, m_ref, v_ref, d_ref, mo_ref, vo_ref):
        gv = g_ref[...]
        mn = ADAM_B1 * m_ref[...] + (1.0 - ADAM_B1) * gv
        vn = ADAM_B2 * v_ref[...] + (1.0 - ADAM_B2) * (gv * gv)
        m_hat = mn / (1.0 - ADAM_B1 ** ADAM_STEP)
        v_hat = vn / (1.0 - ADAM_B2 ** ADAM_STEP)
        d_ref[...] = -ADAM_LR * (m_hat / (jnp.sqrt(v_hat) + ADAM_EPS) + ADAM_WD * w_ref[...])
        mo_ref[...] = mn
        vo_ref[...] = vn

    spec = pl.BlockSpec((tr, c), lambda i: (i, 0))
    out = jax.ShapeDtypeStruct((r, c), jnp.float32)
    return pl.pallas_call(
        body, name=name, grid=(r // tr,), in_specs=[spec] * 4, out_specs=[spec] * 3, out_shape=[out] * 3,
        compiler_params=_params(("parallel",)),
    )(w, g, m, v)


def _position():
    return lax.axis_index("x"), lax.axis_index("y"), lax.axis_index("c")


def _other_chips(x, y):
    return [(2 * (1 - x) + y, (1 - x, y)), (2 * x + (1 - y), (x, 1 - y)), (2 * (1 - x) + (1 - y), (1 - x, 1 - y))]


def _gather_weights(slots):
    n = len(slots)

    def body(*refs):
        out = refs[n:2 * n]
        send_sems, recv_sems = refs[2 * n:]
        x, y, c = _position()
        me = 2 * x + y
        sibling = (x, y, 1 - c)
        chips = _other_chips(x, y)

        def half(ref, chip, which):
            r = ref.shape[1]
            return ref.at[chip, pl.ds(which * (r // 2), r // 2), :]

        def over_ici(w, k):
            return pltpu.make_async_remote_copy(
                src_ref=half(out[w], me, c), dst_ref=half(out[w], me, c),
                send_sem=send_sems.at[6 * w + k], recv_sem=recv_sems.at[6 * w + k],
                device_id=(*chips[k][1], c), device_id_type=MESH)

        def landed(w, k):
            return pltpu.make_async_remote_copy(
                src_ref=half(out[w], chips[k][0], c), dst_ref=half(out[w], chips[k][0], c),
                send_sem=send_sems.at[6 * w + k], recv_sem=recv_sems.at[6 * w + k],
                device_id=(*chips[k][1], c), device_id_type=MESH)

        def over_d2d(w, k, which):
            return pltpu.make_async_remote_copy(
                src_ref=half(out[w], chips[k][0], which), dst_ref=half(out[w], chips[k][0], which),
                send_sem=send_sems.at[6 * w + 3 + k], recv_sem=recv_sems.at[6 * w + 3 + k],
                device_id=sibling, device_id_type=MESH)

        for w in range(n):
            for k in range(3):
                over_ici(w, k).start()
        for w in range(n):
            for k in range(3):
                landed(w, k).wait_recv()
                over_d2d(w, k, c).start()
        for w in range(n):
            for k in range(3):
                over_d2d(w, k, 1 - c).wait_recv()
        for w in range(n):
            for k in range(3):
                over_ici(w, k).wait_send()
                over_d2d(w, k, c).wait_send()

    return pl.pallas_call(
        body, name="gather_weights",
        in_specs=[ANY] * n, out_specs=[ANY] * n,
        out_shape=[jax.ShapeDtypeStruct(s.shape, s.dtype) for s in slots],
        input_output_aliases={w: w for w in range(n)},
        scratch_shapes=[pltpu.SemaphoreType.DMA((6 * n,)), pltpu.SemaphoreType.DMA((6 * n,))],
        compiler_params=pltpu.CompilerParams(has_side_effects=True),
    )(*slots)


def _pair_exchange(grads):
    n = len(grads)

    def body(*refs):
        src, theirs = refs[:n], refs[n:2 * n]
        send_sems, recv_sems = refs[2 * n:]
        x, y, c = _position()

        def rows(ref, which):
            r = ref.shape[1]
            return ref.at[:, pl.ds(which * (r // 2), r // 2), :]

        remote = [pltpu.make_async_remote_copy(
            src_ref=rows(src[w], 1 - c), dst_ref=theirs[w], send_sem=send_sems.at[w], recv_sem=recv_sems.at[w],
            device_id=(x, y, 1 - c), device_id_type=MESH) for w in range(n)]
        for w in range(n):
            remote[w].start()
        for w in range(n):
            remote[w].wait()

    halves = [jax.ShapeDtypeStruct((4, g.shape[1] // 2, g.shape[2]), g.dtype) for g in grads]
    return pl.pallas_call(
        body, name="grad_pair_exchange",
        in_specs=[ANY] * n, out_specs=[ANY] * n, out_shape=halves,
        scratch_shapes=[pltpu.SemaphoreType.DMA((n,)), pltpu.SemaphoreType.DMA((n,))],
        compiler_params=pltpu.CompilerParams(has_side_effects=True),
    )(*grads)


def _chip_scatter(pair_sums):
    n = len(pair_sums)

    def body(*refs):
        src, out = refs[:n], refs[n:2 * n]
        send_sems, recv_sems = refs[2 * n:]
        x, y, c = _position()
        me = 2 * x + y
        chips = _other_chips(x, y)

        remote = [[pltpu.make_async_remote_copy(
            src_ref=src[w].at[chips[k][0]], dst_ref=out[w].at[me],
            send_sem=send_sems.at[3 * w + k], recv_sem=recv_sems.at[3 * w + k],
            device_id=(*chips[k][1], c), device_id_type=MESH) for k in range(3)] for w in range(n)]
        for w in range(n):
            for k in range(3):
                remote[w][k].start()
        for w in range(n):
            for k in range(3):
                pltpu.make_async_remote_copy(
                    src_ref=src[w].at[chips[k][0]], dst_ref=out[w].at[chips[k][0]],
                    send_sem=send_sems.at[3 * w + k], recv_sem=recv_sems.at[3 * w + k],
                    device_id=(*chips[k][1], c), device_id_type=MESH).wait_recv()
        for w in range(n):
            for k in range(3):
                remote[w][k].wait_send()

    return pl.pallas_call(
        body, name="grad_chip_scatter",
        in_specs=[ANY] * n, out_specs=[ANY] * n,
        out_shape=[jax.ShapeDtypeStruct(p.shape, p.dtype) for p in pair_sums],
        scratch_shapes=[pltpu.SemaphoreType.DMA((3 * n,)), pltpu.SemaphoreType.DMA((3 * n,))],
        compiler_params=pltpu.CompilerParams(has_side_effects=True),
    )(*pair_sums)


def _pair_share(wholes):
    n = len(wholes)

    def body(*refs):
        out = refs[n:2 * n]
        send_sems, recv_sems = refs[2 * n:]
        x, y, c = _position()

        def rows(ref, which):
            r = ref.shape[0]
            return ref.at[pl.ds(which * (r // 2), r // 2), :]

        remote = [pltpu.make_async_remote_copy(
            src_ref=rows(out[w], c), dst_ref=rows(out[w], c), send_sem=send_sems.at[w], recv_sem=recv_sems.at[w],
            device_id=(x, y, 1 - c), device_id_type=MESH) for w in range(n)]
        for w in range(n):
            remote[w].start()
        for w in range(n):
            remote[w].wait_send()
            pltpu.make_async_remote_copy(
                src_ref=rows(out[w], 1 - c), dst_ref=rows(out[w], 1 - c), send_sem=send_sems.at[w],
                recv_sem=recv_sems.at[w], device_id=(x, y, 1 - c), device_id_type=MESH).wait_recv()

    return pl.pallas_call(
        body, name="grad_pair_share",
        in_specs=[ANY] * n, out_specs=[ANY] * n,
        out_shape=[jax.ShapeDtypeStruct(h.shape, h.dtype) for h in wholes],
        input_output_aliases={w: w for w in range(n)},
        scratch_shapes=[pltpu.SemaphoreType.DMA((n,)), pltpu.SemaphoreType.DMA((n,))],
        compiler_params=pltpu.CompilerParams(has_side_effects=True),
    )(*wholes)


def _all_reduce_small(block):
    r, c = block.shape

    def body(src_ref, out_ref, stage_ref, send_sems, recv_sems):
        x, y, cc = _position()
        me = 4 * x + 2 * y + cc
        stage_ref[me] = src_ref[...]
        flip = lambda v, on: 1 - v if on else v
        peers = [(flip(x, dx), flip(y, dy), flip(cc, dc)) for dx in (0, 1) for dy in (0, 1) for dc in (0, 1)][1:]
        copies = [pltpu.make_async_remote_copy(
            src_ref=stage_ref.at[me], dst_ref=stage_ref.at[me],
            send_sem=send_sems.at[k], recv_sem=recv_sems.at[k], device_id=peer, device_id_type=MESH)
            for k, peer in enumerate(peers)]
        for cp in copies:
            cp.start()
        for k, (px, py, pc) in enumerate(peers):
            them = 4 * px + 2 * py + pc
            pltpu.make_async_remote_copy(
                src_ref=stage_ref.at[them], dst_ref=stage_ref.at[them],
                send_sem=send_sems.at[k], recv_sem=recv_sems.at[k], device_id=(px, py, pc),
                device_id_type=MESH).wait_recv()
        for cp in copies:
            cp.wait_send()
        total = stage_ref[0]
        for d in range(1, 8):
            total = total + stage_ref[d]
        out_ref[...] = total

    return pl.pallas_call(
        body, name="all_reduce_small",
        in_specs=[pl.BlockSpec(memory_space=pltpu.VMEM)], out_specs=pl.BlockSpec(memory_space=pltpu.VMEM),
        out_shape=jax.ShapeDtypeStruct((r, c), jnp.float32),
        scratch_shapes=[pltpu.VMEM((8, r, c), jnp.float32), pltpu.SemaphoreType.DMA((7,)),
                        pltpu.SemaphoreType.DMA((7,))],
        compiler_params=pltpu.CompilerParams(has_side_effects=True),
    )(block)


def _cols_from_shards(g):
    n, r, c = g.shape
    return jnp.transpose(g, (1, 0, 2)).reshape(r, n * c)


def _cols_to_shards(w, n=4):
    r, c = w.shape
    return jnp.transpose(w.reshape(r, n, c // n), (1, 0, 2))


def _pad_w_in(full):
    d = full.shape[0]
    zeros = jnp.zeros((d, COL_CKV - COL_KR - QK_ROPE), full.dtype)
    return jnp.concatenate([full[:, :COL_KR], full[:, IN_WIDTH - QK_ROPE:], zeros,
                            full[:, COL_KR:COL_KR + KV_RANK]], axis=1)


def _unpad_w_in(padded):
    return jnp.concatenate([padded[:, :COL_KR], padded[:, COL_CKV:COL_CKV + KV_RANK],
                            padded[:, COL_KR:COL_KR + QK_ROPE]], axis=1)


def _pad_w_uq(full):
    r = full.shape[0]
    per_head = full.reshape(r, N_HEADS, QK_NOPE + QK_ROPE)
    return jnp.pad(per_head, ((0, 0), (0, 0), (0, HEAD_PAD - QK_NOPE - QK_ROPE))).reshape(r, N_HEADS * HEAD_PAD)


def _unpad_w_uq(padded):
    r = padded.shape[0]
    return padded.reshape(r, N_HEADS, HEAD_PAD)[:, :, :QK_NOPE + QK_ROPE].reshape(r, N_HEADS * (QK_NOPE + QK_ROPE))


_SMALL = ["pre_mix_g", "post_mix_g", "pre_mlp_g", "post_mlp_g", "conv_attn_g", "qkv_g", "conv_w0", "conv_w1", "conv_w2"]
SMALL_ROWS = 16


def _pack_small(d, pre_mix, post_mix, pre_mlp, post_mlp, conv_out, attn_out, q_norm, kv_norm, conv_w):
    row = lambda *parts: jnp.pad(jnp.concatenate(parts, axis=1), ((0, 0), (0, d - sum(p.shape[1] for p in parts))))
    rows = [row(pre_mix), row(post_mix), row(pre_mlp), row(post_mlp), row(conv_out, attn_out), row(q_norm, kv_norm),
            row(conv_w[0:1]), row(conv_w[1:2]), row(conv_w[2:3])]
    return jnp.pad(jnp.concatenate(rows, axis=0), ((0, SMALL_ROWS - len(rows)), (0, 0)))


def _unpack_small(p, chip):
    cw = CONV_WIDTH // 4
    conv_w = lax.dynamic_slice(p[6:9, :CONV_WIDTH], (0, chip * cw), (3, cw))
    return dict(pre_mix_g=p[0:1], post_mix_g=p[1:2], pre_mlp_g=p[2:3], post_mlp_g=p[3:4],
                conv_out_g=p[4:5, :CONV_WIDTH], attn_out_g=p[4:5, CONV_WIDTH:2 * CONV_WIDTH],
                q_norm_g=p[5:6, :Q_RANK], kv_norm_g=p[5:6, Q_RANK:Q_RANK + KV_RANK], conv_w=conv_w[None])


def kernel(x, pre_mix_g, w_in, conv_w, q_norm_g, w_uq, kv_norm_g, w_ukv, conv_out_g, attn_out_g, w_o, post_mix_g, pre_mlp_g, w_up, w_down, post_mlp_g, loss_target, m_pre_mix_g, m_w_in, m_conv_w, m_q_norm_g, m_w_uq, m_kv_norm_g, m_w_ukv, m_conv_out_g, m_attn_out_g, m_w_o, m_post_mix_g, m_pre_mlp_g, m_w_up, m_w_down, m_post_mlp_g, v_pre_mix_g, v_w_in, v_conv_w, v_q_norm_g, v_w_uq, v_kv_norm_g, v_w_ukv, v_conv_out_g, v_attn_out_g, v_w_o, v_post_mix_g, v_pre_mlp_g, v_w_up, v_w_down, v_post_mlp_g):
    bf16 = jnp.bfloat16
    s, d = x.shape[1], x.shape[2]
    d_ff = 4 * d
    chip = 2 * lax.axis_index("x") + lax.axis_index("y")
    xs = x.reshape(s, d)
    target = loss_target.reshape(s, d)
    tm = min(256, s)
    t_attn = min(256, s)
    mt = min(1024, s)

    big = dict(w_in=w_in[0], w_uq=w_uq[0], w_ukv=w_ukv[0], w_o=w_o[0], w_up=w_up[0], w_down=w_down[0])
    names = list(big)
    pos = jnp.stack([lax.axis_index("c"), chip]).astype(jnp.int32)
    gathered = dict(zip(names, _gather_weights([_cast_into_slot(big[k], pos, name="cast_" + k) for k in names])))
    win = _pad_w_in(_cols_from_shards(gathered["w_in"]))
    wuq = _pad_w_uq(_cols_from_shards(gathered["w_uq"]))
    wukv = _cols_from_shards(gathered["w_ukv"])
    wo = gathered["w_o"].reshape(-1, d)
    wup = gathered["w_up"]
    wdown = gathered["w_down"].reshape(d_ff, d)
    ff4 = d_ff // 4

    spread = lambda a: lax.dynamic_update_slice(jnp.zeros((3, CONV_WIDTH), jnp.float32), a[0],
                                                (0, chip * (CONV_WIDTH // 4)))
    conv_w_mine = jnp.where(lax.axis_index("c") == 0, spread(conv_w), 0.0)
    conv_w_full = _all_reduce_small(jnp.pad(conv_w_mine, ((0, 5), (0, 0))))[0:3]

    h1 = _rms_fwd(xs, pre_mix_g, width=d, col=0, tm=tm, name="rms_pre_mix")
    proj = _matmul(h1, win, dims=_NN, mnk=(s, IN_PAD, d), tiles=(mt, 512, 512), name="mm_proj")
    y_conv = _conv_fwd(proj, conv_w_full, conv_out_g)
    cqn = _rms_fwd(proj, q_norm_g, width=Q_RANK, col=COL_CQ // Q_RANK, tm=tm, name="rms_q")
    ckvn = _rms_fwd(proj, kv_norm_g, width=KV_RANK, col=COL_CKV // KV_RANK, tm=tm, name="rms_kv")
    q_pad = _matmul(cqn, wuq, dims=_NN, mnk=(s, N_HEADS * HEAD_PAD, Q_RANK), tiles=(mt, 1024, Q_RANK), name="mm_q")
    kv = _matmul(ckvn, wukv, dims=_NN, mnk=(s, N_HEADS * HEAD_PAD, KV_RANK), tiles=(mt, 1024, KV_RANK),
                 name="mm_kv", out_dtype=bf16)
    cos, sin = _rope_tables(s)
    q_rot, kr_rot = _qk_rope_fwd(q_pad, proj, cos, sin, tm=tm)
    o, lse, y_attn = _attn_fwd(q_rot, kv, kr_rot, attn_out_g, t=t_attn)
    ycat = jnp.concatenate([y_conv, y_attn], axis=1)
    mix = _matmul(ycat, wo, dims=_NN, mnk=(s, d, 2 * CONV_WIDTH), tiles=(mt, 1024, 512), name="mm_out")
    x2, h2 = _mix_residual_fwd(xs, mix, post_mix_g, pre_mlp_g, tm=tm)

    def up_epilogue(acc, extra_refs, out_refs):
        r = jnp.maximum(acc, 0.0)
        out_refs[0][...] = acc.astype(bf16)
        out_refs[1][...] = (r * r).astype(bf16)

    n_ff = ff4 // 1024
    act = jax.ShapeDtypeStruct((s, d_ff), bf16)
    up, act_sq = _matmul(
        h2, wup, dims=_NN, mnk=(s, d_ff, d), tiles=(mt, 1024, 512), name="mm_up",
        b_spec=pl.BlockSpec((None, 512, 1024), lambda i, j, l: (j // n_ff, l, j % n_ff)),
        out_shape=(act, act), o_spec=(pl.BlockSpec((mt, 1024), lambda i, j, l: (i, j)),) * 2, epilogue=up_epilogue)
    mlp = _matmul(act_sq, wdown, dims=_NN, mnk=(s, d, d_ff), tiles=(mt, 1024, 512), name="mm_down")
    dx3, dmlp, dg_post_mlp, loss_part = _loss_head(x2, mlp, target, post_mlp_g, tm=tm)

    def dup_epilogue(acc, extra_refs, out_refs):
        out_refs[0][...] = (acc * (2.0 * jnp.maximum(extra_refs[0][...].astype(jnp.float32), 0.0))).astype(bf16)

    dup = _matmul(dmlp, wdown, dims=_NT, mnk=(s, d_ff, d), tiles=(mt, 1024, 512), name="mm_dact",
                  out_dtype=bf16, epilogue=dup_epilogue, extra=(up,),
                  extra_specs=(pl.BlockSpec((mt, 1024), lambda i, j, l: (i, j)),))
    g_wdown = _matmul(act_sq, dmlp, dims=_TN, mnk=(d_ff, d, s), tiles=(1024, 1024, min(512, s)), name="mm_gw_down",
                      out_dtype=bf16).reshape(4, ff4, d)
    n_ffk = ff4 // 512
    dh2 = _matmul(dup, wup, dims=_NT, mnk=(s, d, d_ff), tiles=(mt, 1024, 512), name="mm_dh2",
                  b_spec=pl.BlockSpec((None, 1024, 512), lambda i, j, l: (l // n_ffk, j, l % n_ffk)))
    g_wup = _matmul(h2, dup, dims=_TN, mnk=(d, d_ff, s), tiles=(1024, 1024, min(512, s)), name="mm_gw_up",
                    out_shape=jax.ShapeDtypeStruct((4, d, ff4), bf16),
                    o_spec=pl.BlockSpec((None, 1024, 1024), lambda i, j, l: (j // n_ff, i, j % n_ff)))
    dx2, dmix, dg_pre_mlp, dg_post_mix = _mix_residual_bwd(dx3, dh2, x2, mix, pre_mlp_g, post_mix_g, tm=tm)

    dycat = _matmul(dmix, wo, dims=_NT, mnk=(s, 2 * CONV_WIDTH, d), tiles=(mt, 1024, 512), name="mm_dycat")
    g_wo = _matmul(ycat, dmix, dims=_TN, mnk=(2 * CONV_WIDTH, d, s), tiles=(1024, 1024, min(512, s)),
                   name="mm_gw_out", out_dtype=bf16).reshape(4, CONV_WIDTH // 2, d)
    du, dgb, dgc, dg_conv_w, dg_conv_out = _conv_bwd(proj, conv_w_full, conv_out_g, dycat)
    do, delta, dg_attn_out = _attn_norm_bwd(o, attn_out_g, dycat)
    dq_pad, dk_pad, dv = _attn_bwd(q_rot, kv, kr_rot, do, lse, delta, t=t_attn)
    dq_raw, dkv, dkr = _qk_rope_bwd(dq_pad, dk_pad, dv, cos, sin, tm=tm)
    wq_cols = N_HEADS * HEAD_PAD
    g_wuq = _matmul(cqn, dq_raw, dims=_TN, mnk=(Q_RANK, wq_cols, s), tiles=(Q_RANK, 1024, min(512, s)),
                    name="mm_gw_uq", out_dtype=bf16)
    dcqn = _matmul(dq_raw, wuq, dims=_NT, mnk=(s, Q_RANK, wq_cols), tiles=(mt, Q_RANK, 512), name="mm_dcq")
    g_wukv = _matmul(ckvn, dkv, dims=_TN, mnk=(KV_RANK, wq_cols, s), tiles=(KV_RANK, 1024, min(512, s)),
                     name="mm_gw_ukv", out_dtype=bf16)
    dckvn = _matmul(dkv, wukv, dims=_NT, mnk=(s, KV_RANK, wq_cols), tiles=(mt, KV_RANK, 512), name="mm_dckv")
    dcq, dg_q_norm = _rms_bwd(proj, q_norm_g, dcqn, width=Q_RANK, col=COL_CQ // Q_RANK, tm=tm, name="rms_q_bwd")
    dckv, dg_kv_norm = _rms_bwd(proj, kv_norm_g, dckvn, width=KV_RANK, col=COL_CKV // KV_RANK, tm=tm,
                                name="rms_kv_bwd")
    dproj = jnp.concatenate([du, dgb, dgc, dcq, dkr, jnp.zeros((s, COL_CKV - COL_KR - 128), bf16), dckv], axis=1)
    dh1 = _matmul(dproj, win, dims=_NT, mnk=(s, d, IN_PAD), tiles=(mt, 1024, 512), name="mm_dh1")
    g_win = _matmul(h1, dproj, dims=_TN, mnk=(d, IN_PAD, s), tiles=(1024, 512, min(512, s)), name="mm_gw_in",
                    out_dtype=bf16)
    grad_x, dg_pre_mix = _input_bwd(dx2, dh1, xs, pre_mix_g, tm=tm)

    grads = [_cols_to_shards(_unpad_w_in(g_win)), _cols_to_shards(_unpad_w_uq(g_wuq)), _cols_to_shards(g_wukv),
             g_wo, g_wup, g_wdown]
    theirs = _pair_exchange(grads)
    pair_sums = [_pair_add(g, t, pos, name="pair_add_" + k) for k, g, t in zip(names, grads, theirs)]
    by_source = _chip_scatter(pair_sums)
    halves = [_chip_sum(b, p, pos, name="chip_sum_" + k) for k, b, p in zip(names, by_source, pair_sums)]
    full_grads = dict(zip(names, _pair_share(halves)))

    small_g = _all_reduce_small(_pack_small(d, dg_pre_mix, dg_post_mix, dg_pre_mlp, dg_post_mlp, dg_conv_out,
                                            dg_attn_out, dg_q_norm, dg_kv_norm, dg_conv_w))
    pack_w = lambda cw, pre_mix, post_mix, pre_mlp, post_mlp, conv_out, attn_out, q_norm, kv_norm: _pack_small(
        d, pre_mix, post_mix, pre_mlp, post_mlp, conv_out, attn_out, q_norm, kv_norm, cw)
    small_w = pack_w(conv_w_full, pre_mix_g, post_mix_g, pre_mlp_g, post_mlp_g, conv_out_g, attn_out_g, q_norm_g,
                     kv_norm_g)
    small_m = pack_w(spread(m_conv_w), m_pre_mix_g, m_post_mix_g, m_pre_mlp_g, m_post_mlp_g, m_conv_out_g,
                     m_attn_out_g, m_q_norm_g, m_kv_norm_g)
    small_v = pack_w(spread(v_conv_w), v_pre_mix_g, v_post_mix_g, v_pre_mlp_g, v_post_mlp_g, v_conv_out_g,
                     v_attn_out_g, v_q_norm_g, v_kv_norm_g)
    small_d, small_nm, small_nv = _adamw(small_w, small_g, small_m, small_v, name="adamw_small")
    sg, sd, snm, snv = (_unpack_small(p, chip) for p in (small_g, small_d, small_nm, small_nv))

    moments = dict(w_in=(m_w_in, v_w_in), w_uq=(m_w_uq, v_w_uq), w_ukv=(m_w_ukv, v_w_ukv), w_o=(m_w_o, v_w_o),
                   w_up=(m_w_up, v_w_up), w_down=(m_w_down, v_w_down))
    gw, dw, nm, nv = {}, {}, {}, {}
    for k in names:
        g = full_grads[k]
        delta_k, nm_k, nv_k = _adamw(big[k], g, moments[k][0][0], moments[k][1][0], name="adamw_" + k)
        gw[k], dw[k], nm[k], nv[k] = g[None], delta_k[None], nm_k[None], nv_k[None]
    for src, dst in ((sg, gw), (sd, dw), (snm, nm), (snv, nv)):
        dst.update(src)

    loss = lax.psum(loss_part[0, 0], ("x", "y", "c"))
    order = ["pre_mix_g", "w_in", "conv_w", "q_norm_g", "w_uq", "kv_norm_g", "w_ukv", "conv_out_g", "attn_out_g",
             "w_o", "post_mix_g", "pre_mlp_g", "w_up", "w_down", "post_mlp_g"]
    return (loss, grad_x.reshape(1, s, d), *[gw[k] for k in order], *[dw[k] for k in order],
            *[nm[k] for k in order], *[nv[k] for k in order])
```

```python
import functools

import jax
import jax.numpy as jnp
from jax import lax
from jax.experimental import pallas as pl
from jax.experimental.pallas import tpu as pltpu

EPS = 1e-6
NEG_INF = -1e30
CHUNK_SHIFT = 6
N_HEADS = 8
HEAD_PAD = 256
QK_NOPE = 128
QK_ROPE = 64
V_DIM = 128
CONV_WIDTH = 1024
Q_RANK = 768
KV_RANK = 512
ROPE_THETA = 10000.0
ATTN_SCALE = (QK_NOPE + QK_ROPE) ** -0.5
ADAM_LR, ADAM_B1, ADAM_B2, ADAM_EPS, ADAM_WD, ADAM_STEP = 0.001, 0.9, 0.999, 1e-08, 0.01, 10

COL_CQ = 3 * CONV_WIDTH
COL_KR = COL_CQ + Q_RANK
COL_CKV = 4096
IN_PAD = COL_CKV + KV_RANK
IN_WIDTH = 3 * CONV_WIDTH + Q_RANK + KV_RANK + QK_ROPE

VMEM_LIMIT_BYTES = 56 * 1024 * 1024
MESH = pl.DeviceIdType.MESH
ANY = pl.BlockSpec(memory_space=pl.ANY)

_NN = (((1,), (0,)), ((), ()))
_NT = (((1,), (1,)), ((), ()))
_TN = (((0,), (0,)), ((), ()))


def _params(sem):
    return pltpu.CompilerParams(dimension_semantics=sem, vmem_limit_bytes=VMEM_LIMIT_BYTES)


class _Buf:
    def __init__(self, arr):
        self.arr = arr


def _position():
    return lax.axis_index("x"), lax.axis_index("y"), lax.axis_index("c")


def _other_chips(x, y):
    return [(2 * (1 - x) + y, (1 - x, y)), (2 * x + (1 - y), (x, 1 - y)), (2 * (1 - x) + (1 - y), (1 - x, 1 - y))]


def _remote(src, dst, sems, k, to):
    send, recv, off = sems
    return pltpu.make_async_remote_copy(src_ref=src, dst_ref=dst, send_sem=send.at[off + k], recv_sem=recv.at[off + k],
                                        device_id=to, device_id_type=MESH)


class _GatherIci:
    n_sems = 3

    def __init__(self, buf, lo, n):
        self.buf, self.lo, self.n, self.buffers, self.sources = buf, lo, n, [buf], []

    def _rows(self, ref, slot, which):
        return ref.at[slot, pl.ds(which * (ref.shape[1] // 2) + self.lo, self.n), :]

    def start(self, refs, sems):
        ref = refs[id(self.buf)]
        x, y, c = _position()
        mine = self._rows(ref, 2 * x + y, c)
        for k, (_, xy) in enumerate(_other_chips(x, y)):
            _remote(mine, mine, sems, k, (*xy, c)).start()

    def wait(self, refs, sems):
        ref = refs[id(self.buf)]
        x, y, c = _position()
        mine = self._rows(ref, 2 * x + y, c)
        for k, (slot, xy) in enumerate(_other_chips(x, y)):
            landed = self._rows(ref, slot, c)
            _remote(landed, landed, sems, k, (*xy, c)).wait_recv()
            _remote(mine, mine, sems, k, (*xy, c)).wait_send()


class _GatherD2d(_GatherIci):
    def start(self, refs, sems):
        ref = refs[id(self.buf)]
        x, y, c = _position()
        for k, (slot, _) in enumerate(_other_chips(x, y)):
            rows = self._rows(ref, slot, c)
            _remote(rows, rows, sems, k, (x, y, 1 - c)).start()

    def wait(self, refs, sems):
        ref = refs[id(self.buf)]
        x, y, c = _position()
        for k, (slot, _) in enumerate(_other_chips(x, y)):
            sent, landed = self._rows(ref, slot, c), self._rows(ref, slot, 1 - c)
            _remote(landed, landed, sems, k, (x, y, 1 - c)).wait_recv()
            _remote(sent, sent, sems, k, (x, y, 1 - c)).wait_send()


class _ScatterIci:
    n_sems = 3

    def __init__(self, src, dst, lo, n):
        self.src, self.dst, self.lo, self.n, self.buffers, self.sources = src, dst, lo, n, [dst], [src]

    def _rows(self, ref, slot):
        return ref.at[slot, pl.ds(self.lo, self.n), :]

    def start(self, refs, sems):
        src, dst = refs[id(self.src)], refs[id(self.dst)]
        x, y, c = _position()
        for k, (slot, xy) in enumerate(_other_chips(x, y)):
            _remote(self._rows(src, slot), self._rows(dst, 2 * x + y), sems, k, (*xy, c)).start()

    def wait(self, refs, sems):
        src, dst = refs[id(self.src)], refs[id(self.dst)]
        x, y, c = _position()
        for k, (slot, xy) in enumerate(_other_chips(x, y)):
            _remote(self._rows(src, slot), self._rows(dst, slot), sems, k, (*xy, c)).wait_recv()
            _remote(self._rows(src, slot), self._rows(dst, slot), sems, k, (*xy, c)).wait_send()


class _PairExchange:
    n_sems = 1

    def __init__(self, src, dst):
        self.src, self.dst, self.buffers, self.sources = src, dst, [dst], [src]

    def _copy(self, refs, sems):
        src, dst = refs[id(self.src)], refs[id(self.dst)]
        x, y, c = _position()
        h = src.shape[1] // 2
        return _remote(src.at[:, pl.ds((1 - c) * h, h), :], dst, sems, 0, (x, y, 1 - c))

    def start(self, refs, sems):
        self._copy(refs, sems).start()

    def wait(self, refs, sems):
        self._copy(refs, sems).wait()


class _PairShare:
    n_sems = 1

    def __init__(self, buf):
        self.buf, self.buffers, self.sources = buf, [buf], []

    def _rows(self, ref, which):
        h = ref.shape[0] // 2
        return ref.at[pl.ds(which * h, h), :]

    def start(self, refs, sems):
        ref = refs[id(self.buf)]
        x, y, c = _position()
        _remote(self._rows(ref, c), self._rows(ref, c), sems, 0, (x, y, 1 - c)).start()

    def wait(self, refs, sems):
        ref = refs[id(self.buf)]
        x, y, c = _position()
        _remote(self._rows(ref, c), self._rows(ref, c), sems, 0, (x, y, 1 - c)).wait_send()
        _remote(self._rows(ref, 1 - c), self._rows(ref, 1 - c), sems, 0, (x, y, 1 - c)).wait_recv()


def _unique(items):
    seen, out = set(), []
    for it in items:
        if id(it) not in seen:
            seen.add(id(it))
            out.append(it)
    return out


def _job_operands(jobs):
    sources = _unique([a for j in jobs for a in j.sources])
    buffers = _unique([b for j in jobs for b in j.buffers])
    held = [b for b in buffers if not isinstance(b.arr, jax.ShapeDtypeStruct)]
    fresh = [b for b in buffers if isinstance(b.arr, jax.ShapeDtypeStruct)]
    return sources, held, fresh


def _sem_offsets(jobs):
    offs, total = [], 0
    for j in jobs:
        offs.append(total)
        total += j.n_sems
    return offs, total


def _call(body, *, name, grid, in_specs, out_specs, out_shape, args, semantics, scratch_shapes=(), jobs=()):
    if not jobs:
        return pl.pallas_call(
            body, name=name, grid=grid, in_specs=list(in_specs), out_specs=list(out_specs),
            out_shape=list(out_shape), scratch_shapes=list(scratch_shapes), compiler_params=_params(semantics),
        )(*args)
    sources, held, fresh = _job_operands(jobs)
    offs, n_sem = _sem_offsets(jobs)
    n_in, n_out, n_scr = len(in_specs), len(out_specs), len(scratch_shapes)
    n_src, n_held, n_fresh = len(sources), len(held), len(fresh)

    def carried(*refs):
        ins = refs[:n_in]
        src_refs = refs[n_in:n_in + n_src]
        o0 = n_in + n_src + n_held
        outs = refs[o0:o0 + n_out]
        buf_refs = refs[o0 + n_out:o0 + n_out + n_held + n_fresh]
        s0 = o0 + n_out + n_held + n_fresh
        scratch = refs[s0:s0 + n_scr]
        send, recv = refs[s0 + n_scr], refs[s0 + n_scr + 1]
        where = {id(a): r for a, r in zip(sources, src_refs)}
        where.update({id(b): r for b, r in zip(held + fresh, buf_refs)})
        ids = [pl.program_id(a) for a in range(len(grid))]
        first = functools.reduce(jnp.logical_and, [i == 0 for i in ids])
        last = functools.reduce(jnp.logical_and, [i == g - 1 for i, g in zip(ids, grid)])

        @pl.when(first)
        def _():
            for j, off in zip(jobs, offs):
                j.start(where, (send, recv, off))

        body(*ins, *outs, *scratch)

        @pl.when(last)
        def _():
            for j, off in zip(jobs, offs):
                j.wait(where, (send, recv, off))

    shape_of = lambda b: jax.ShapeDtypeStruct(b.arr.shape, b.arr.dtype)
    res = pl.pallas_call(
        carried, name=name, grid=grid,
        in_specs=[*in_specs, *[ANY] * (n_src + n_held)],
        out_specs=[*out_specs, *[ANY] * (n_held + n_fresh)],
        out_shape=[*out_shape, *[shape_of(b) for b in held + fresh]],
        input_output_aliases={n_in + n_src + i: n_out + i for i in range(n_held)},
        scratch_shapes=[*scratch_shapes, pltpu.SemaphoreType.DMA((n_sem,)), pltpu.SemaphoreType.DMA((n_sem,))],
        compiler_params=pltpu.CompilerParams(dimension_semantics=("arbitrary",) * len(grid),
                                             vmem_limit_bytes=VMEM_LIMIT_BYTES, has_side_effects=True),
    )(*args, *sources, *[b.arr for b in held])
    for b, new in zip(held + fresh, res[n_out:]):
        b.arr = new
    return list(res[:n_out])


def _comm(name, phases):
    jobs = [j for ph in phases for j in ph]
    sources, held, fresh = _job_operands(jobs)
    offs, n_sem = _sem_offsets(jobs)
    off_of = {id(j): o for j, o in zip(jobs, offs)}
    n_src, n_held, n_fresh = len(sources), len(held), len(fresh)

    def body(*refs):
        src_refs = refs[:n_src]
        buf_refs = refs[n_src + n_held:n_src + 2 * n_held + n_fresh]
        send, recv = refs[-2], refs[-1]
        where = {id(a): r for a, r in zip(sources, src_refs)}
        where.update({id(b): r for b, r in zip(held + fresh, buf_refs)})
        for ph in phases:
            for j in ph:
                j.start(where, (send, recv, off_of[id(j)]))
            for j in ph:
                j.wait(where, (send, recv, off_of[id(j)]))

    shape_of = lambda b: jax.ShapeDtypeStruct(b.arr.shape, b.arr.dtype)
    res = pl.pallas_call(
        body, name=name,
        in_specs=[ANY] * (n_src + n_held), out_specs=[ANY] * (n_held + n_fresh),
        out_shape=[shape_of(b) for b in held + fresh],
        input_output_aliases={n_src + i: i for i in range(n_held)},
        scratch_shapes=[pltpu.SemaphoreType.DMA((n_sem,)), pltpu.SemaphoreType.DMA((n_sem,))],
        compiler_params=pltpu.CompilerParams(has_side_effects=True),
    )(*sources, *[b.arr for b in held])
    for b, new in zip(held + fresh, res):
        b.arr = new


def _matmul(a, b, *, dims, mnk, tiles, name, out_dtype=jnp.float32, a_spec=None, b_spec=None,
            out_shape=None, o_spec=None, epilogue=None, extra=(), extra_specs=(), jobs=()):
    m, n, k = mnk
    tm, tn, tk = tiles
    assert m % tm == 0 and n % tn == 0 and k % tk == 0, (name, mnk, tiles)
    gm, gn, gk = m // tm, n // tn, k // tk
    if a_spec is None:
        a_spec = (pl.BlockSpec((tk, tm), lambda i, j, l: (l, i)) if dims is _TN
                  else pl.BlockSpec((tm, tk), lambda i, j, l: (i, l)))
    if b_spec is None:
        b_spec = (pl.BlockSpec((tn, tk), lambda i, j, l: (j, l)) if dims is _NT
                  else pl.BlockSpec((tk, tn), lambda i, j, l: (l, j)))
    if out_shape is None:
        out_shape = jax.ShapeDtypeStruct((m, n), out_dtype)
    if o_spec is None:
        o_spec = pl.BlockSpec((tm, tn), lambda i, j, l: (i, j))
    single = not isinstance(out_shape, (tuple, list))
    n_extra = len(extra)

    def body(*refs):
        a_ref, b_ref = refs[0], refs[1]
        extra_refs = refs[2:2 + n_extra]
        out_refs = refs[2 + n_extra:-1]
        acc_ref = refs[-1]
        step = pl.program_id(2)

        @pl.when(step == 0)
        def _():
            acc_ref[...] = jnp.zeros_like(acc_ref)

        acc_ref[...] += lax.dot_general(a_ref[...], b_ref[...], dims, preferred_element_type=jnp.float32)

        @pl.when(step == gk - 1)
        def _():
            if epilogue is None:
                out_refs[0][...] = acc_ref[...].astype(out_refs[0].dtype)
            else:
                epilogue(acc_ref[...], extra_refs, out_refs)

    res = _call(
        body, name=name, grid=(gm, gn, gk),
        in_specs=[a_spec, b_spec, *extra_specs],
        out_specs=[o_spec] if single else list(o_spec),
        out_shape=[out_shape] if single else list(out_shape),
        scratch_shapes=[pltpu.VMEM((tm, tn), jnp.float32)],
        semantics=("parallel", "parallel", "arbitrary"), args=(a, b, *extra), jobs=jobs)
    return res[0] if single else res


def _rstd(x):
    return lax.rsqrt(jnp.mean(x * x, axis=-1, keepdims=True) + EPS)


def _rms_bwd_rows(x, g, dy):
    r = _rstd(x)
    xn = x * r
    dyg = dy * g
    dx = r * (dyg - xn * jnp.mean(xn * dyg, axis=-1, keepdims=True))
    return dx, dy * xn


def _acc_rows(ref, rows, first):
    part = jnp.sum(rows, axis=0, keepdims=True)

    @pl.when(first)
    def _():
        ref[...] = part

    @pl.when(jnp.logical_not(first))
    def _():
        ref[...] += part


def _rms_fwd(x, g, *, width, col, tm, name):
    s = x.shape[0]

    def body(x_ref, g_ref, o_ref):
        v = x_ref[...]
        o_ref[...] = (v * _rstd(v) * g_ref[...]).astype(o_ref.dtype)

    return pl.pallas_call(
        body, name=name, grid=(s // tm,),
        in_specs=[pl.BlockSpec((tm, width), lambda i: (i, col)), pl.BlockSpec((1, width), lambda i: (0, 0))],
        out_specs=pl.BlockSpec((tm, width), lambda i: (i, 0)),
        out_shape=jax.ShapeDtypeStruct((s, width), jnp.bfloat16),
        compiler_params=_params(("parallel",)),
    )(x, g)


def _rms_bwd(x, g, dy, *, width, col, tm, name):
    s = x.shape[0]

    def body(x_ref, g_ref, dy_ref, dx_ref, dg_ref):
        dx, dgr = _rms_bwd_rows(x_ref[...], g_ref[...], dy_ref[...])
        dx_ref[...] = dx.astype(dx_ref.dtype)
        _acc_rows(dg_ref, dgr, pl.program_id(0) == 0)

    return pl.pallas_call(
        body, name=name, grid=(s // tm,),
        in_specs=[pl.BlockSpec((tm, width), lambda i: (i, col)), pl.BlockSpec((1, width), lambda i: (0, 0)),
                  pl.BlockSpec((tm, width), lambda i: (i, 0))],
        out_specs=[pl.BlockSpec((tm, width), lambda i: (i, 0)), pl.BlockSpec((1, width), lambda i: (0, 0))],
        out_shape=[jax.ShapeDtypeStruct((s, width), jnp.bfloat16), jax.ShapeDtypeStruct((1, width), jnp.float32)],
        compiler_params=_params(("arbitrary",)),
    )(x, g, dy)


def _row_specs(tm, d, n):
    return [pl.BlockSpec((tm, d), lambda i: (i, 0)) for _ in range(n)]


def _gain_specs(d, n):
    return [pl.BlockSpec((1, d), lambda i: (0, 0)) for _ in range(n)]


def _mix_residual_fwd(x, mix, g_post_mix, g_pre_mlp, *, tm, jobs=()):
    s, d = x.shape

    def body(x_ref, mix_ref, g1_ref, g2_ref, x2_ref, h2_ref):
        mixv = mix_ref[...]
        x2 = x_ref[...] + mixv * _rstd(mixv) * g1_ref[...]
        x2_ref[...] = x2
        h2_ref[...] = (x2 * _rstd(x2) * g2_ref[...]).astype(h2_ref.dtype)

    return _call(
        body, name="mix_residual_fwd", grid=(s // tm,),
        in_specs=_row_specs(tm, d, 2) + _gain_specs(d, 2),
        out_specs=_row_specs(tm, d, 2),
        out_shape=[jax.ShapeDtypeStruct((s, d), jnp.float32), jax.ShapeDtypeStruct((s, d), jnp.bfloat16)],
        semantics=("parallel",), args=(x, mix, g_post_mix, g_pre_mlp), jobs=jobs)


def _loss_head(x2, mlp, target, g_post_mlp, *, tm):
    s, d = x2.shape

    def body(x2_ref, m_ref, t_ref, g_ref, dx3_ref, dm_ref, dg_ref, loss_ref):
        first = pl.program_id(0) == 0
        mv = m_ref[...]
        g = g_ref[...]
        diff = x2_ref[...] + mv * _rstd(mv) * g - t_ref[...]
        dx3 = diff * (1.0 / d)
        dx3_ref[...] = dx3
        dm, dgr = _rms_bwd_rows(mv, g, dx3)
        dm_ref[...] = dm.astype(dm_ref.dtype)
        _acc_rows(dg_ref, dgr, first)
        part = 0.5 * jnp.sum(jnp.mean(diff * diff, axis=-1, keepdims=True), axis=0, keepdims=True)
        _acc_rows(loss_ref, jnp.broadcast_to(part, (1, 128)), first)

    return pl.pallas_call(
        body, name="loss_head", grid=(s // tm,),
        in_specs=_row_specs(tm, d, 3) + _gain_specs(d, 1),
        out_specs=_row_specs(tm, d, 2) + _gain_specs(d, 1) + [pl.BlockSpec((1, 128), lambda i: (0, 0))],
        out_shape=[jax.ShapeDtypeStruct((s, d), jnp.float32), jax.ShapeDtypeStruct((s, d), jnp.bfloat16),
                   jax.ShapeDtypeStruct((1, d), jnp.float32), jax.ShapeDtypeStruct((1, 128), jnp.float32)],
        compiler_params=_params(("arbitrary",)),
    )(x2, mlp, target, g_post_mlp)


def _mix_residual_bwd(dx3, dh2, x2, mix, g_pre_mlp, g_post_mix, *, tm):
    s, d = x2.shape

    def body(dx3_ref, dh2_ref, x2_ref, mix_ref, g2_ref, g1_ref, dx2_ref, dmix_ref, dg2_ref, dg1_ref):
        first = pl.program_id(0) == 0
        d_in, dgr2 = _rms_bwd_rows(x2_ref[...], g2_ref[...], dh2_ref[...])
        dx2 = dx3_ref[...] + d_in
        dx2_ref[...] = dx2
        dmix, dgr1 = _rms_bwd_rows(mix_ref[...], g1_ref[...], dx2)
        dmix_ref[...] = dmix.astype(dmix_ref.dtype)
        _acc_rows(dg2_ref, dgr2, first)
        _acc_rows(dg1_ref, dgr1, first)

    return pl.pallas_call(
        body, name="mix_residual_bwd", grid=(s // tm,),
        in_specs=_row_specs(tm, d, 4) + _gain_specs(d, 2),
        out_specs=_row_specs(tm, d, 2) + _gain_specs(d, 2),
        out_shape=[jax.ShapeDtypeStruct((s, d), jnp.float32), jax.ShapeDtypeStruct((s, d), jnp.bfloat16),
                   jax.ShapeDtypeStruct((1, d), jnp.float32), jax.ShapeDtypeStruct((1, d), jnp.float32)],
        compiler_params=_params(("arbitrary",)),
    )(dx3, dh2, x2, mix, g_pre_mlp, g_post_mix)


def _input_bwd(dx2, dh1, x, g_pre_mix, *, tm, jobs=()):
    s, d = x.shape

    def body(dx2_ref, dh1_ref, x_ref, g_ref, dx_ref, dg_ref):
        d_in, dgr = _rms_bwd_rows(x_ref[...], g_ref[...], dh1_ref[...])
        dx_ref[...] = dx2_ref[...] + d_in
        _acc_rows(dg_ref, dgr, pl.program_id(0) == 0)

    return _call(
        body, name="input_bwd", grid=(s // tm,),
        in_specs=_row_specs(tm, d, 3) + _gain_specs(d, 1),
        out_specs=_row_specs(tm, d, 1) + _gain_specs(d, 1),
        out_shape=[jax.ShapeDtypeStruct((s, d), jnp.float32), jax.ShapeDtypeStruct((1, d), jnp.float32)],
        semantics=("arbitrary",), args=(dx2, dh1, x, g_pre_mix), jobs=jobs)


def _shift_rows(z, by):
    s = z.shape[0]
    rows = lax.broadcasted_iota(jnp.int32, z.shape, 0)
    rolled = pltpu.roll(z, by % s, axis=0)
    keep = rows >= by if by > 0 else rows < s + by
    return jnp.where(keep, rolled, 0.0)


def _conv_fwd(proj, conv_w, conv_out_g, jobs=()):
    s = proj.shape[0]
    groups = CONV_WIDTH // 128

    def body(u_ref, gb_ref, gc_ref, w_ref, g_ref, y_ref):
        z = gc_ref[...] * u_ref[...]
        w = w_ref[...]
        conv = w[0:1, :] * _shift_rows(z, 2) + w[1:2, :] * _shift_rows(z, 1) + w[2:3, :] * z
        y = gb_ref[...] * conv
        y_ref[...] = (y * _rstd(y) * g_ref[...]).astype(y_ref.dtype)

    col = lambda base: pl.BlockSpec((s, 128), lambda j: (0, base + j))
    return _call(
        body, name="conv_fwd", grid=(groups,),
        in_specs=[col(0), col(groups), col(2 * groups), pl.BlockSpec((3, 128), lambda j: (0, j)),
                  pl.BlockSpec((1, 128), lambda j: (0, j))],
        out_specs=[pl.BlockSpec((s, 128), lambda j: (0, j))],
        out_shape=[jax.ShapeDtypeStruct((s, CONV_WIDTH), jnp.bfloat16)],
        semantics=("parallel",), args=(proj, proj, proj, conv_w, conv_out_g), jobs=jobs)[0]


def _conv_bwd(proj, conv_w, conv_out_g, dycat):
    s = proj.shape[0]
    groups = CONV_WIDTH // 128

    def body(u_ref, gb_ref, gc_ref, w_ref, g_ref, dy_ref, du_ref, dgb_ref, dgc_ref, dw_ref, dg_ref):
        u, gb, gc = u_ref[...], gb_ref[...], gc_ref[...]
        w = w_ref[...]
        z = gc * u
        z1, z2 = _shift_rows(z, 1), _shift_rows(z, 2)
        conv = w[0:1, :] * z2 + w[1:2, :] * z1 + w[2:3, :] * z
        dyr, dgr = _rms_bwd_rows(gb * conv, g_ref[...], dy_ref[...])
        dg_ref[...] = jnp.sum(dgr, axis=0, keepdims=True)
        dgb_ref[...] = (dyr * conv).astype(dgb_ref.dtype)
        dconv = dyr * gb
        dw_ref[0:1, :] = jnp.sum(dconv * z2, axis=0, keepdims=True)
        dw_ref[1:2, :] = jnp.sum(dconv * z1, axis=0, keepdims=True)
        dw_ref[2:3, :] = jnp.sum(dconv * z, axis=0, keepdims=True)
        dz = w[2:3, :] * dconv + w[1:2, :] * _shift_rows(dconv, -1) + w[0:1, :] * _shift_rows(dconv, -2)
        dgc_ref[...] = (dz * u).astype(dgc_ref.dtype)
        du_ref[...] = (dz * gc).astype(du_ref.dtype)

    col = lambda base: pl.BlockSpec((s, 128), lambda j: (0, base + j))
    act = jax.ShapeDtypeStruct((s, CONV_WIDTH), jnp.bfloat16)
    return pl.pallas_call(
        body, name="conv_bwd", grid=(groups,),
        in_specs=[col(0), col(groups), col(2 * groups), pl.BlockSpec((3, 128), lambda j: (0, j)),
                  pl.BlockSpec((1, 128), lambda j: (0, j)), col(0)],
        out_specs=[col(0), col(0), col(0), pl.BlockSpec((3, 128), lambda j: (0, j)),
                   pl.BlockSpec((1, 128), lambda j: (0, j))],
        out_shape=[act, act, act, jax.ShapeDtypeStruct((3, CONV_WIDTH), jnp.float32),
                   jax.ShapeDtypeStruct((1, CONV_WIDTH), jnp.float32)],
        compiler_params=_params(("parallel",)),
    )(proj, proj, proj, conv_w, conv_out_g, dycat)


def _rope_tables(s):
    pos = jnp.arange(s, dtype=jnp.float32)
    inv_freq = jnp.power(ROPE_THETA, -jnp.arange(0, QK_ROPE, 2, dtype=jnp.float32) / QK_ROPE)
    ang = pos[:, None] * inv_freq[None, :]
    cos, sin = jnp.cos(ang), jnp.sin(ang)
    zeros = jnp.zeros((s, 128 - QK_ROPE), jnp.float32)
    return (jnp.concatenate([cos, cos, zeros], axis=1), jnp.concatenate([-sin, sin, zeros], axis=1))


def _swap_halves(x):
    lane = lax.broadcasted_iota(jnp.int32, x.shape, 1)
    swapped = jnp.where(lane < QK_ROPE // 2, pltpu.roll(x, 128 - QK_ROPE // 2, axis=1),
                        pltpu.roll(x, QK_ROPE // 2, axis=1))
    return jnp.where(lane < QK_ROPE, swapped, 0.0)


def _rope(x, cos, sin):
    return x * cos + _swap_halves(x) * sin


def _rope_transposed(d, cos, sin):
    return d * cos + _swap_halves(d * sin)


def _qk_rope_fwd(q_pad, proj, cos, sin, *, tm):
    s = q_pad.shape[0]
    wq = N_HEADS * HEAD_PAD

    def body(q_ref, kr_ref, cos_ref, sin_ref, qo_ref, kro_ref):
        c, sn = cos_ref[...], sin_ref[...]
        for h in range(N_HEADS):
            lo = h * HEAD_PAD
            qo_ref[:, lo:lo + 128] = q_ref[:, lo:lo + 128].astype(qo_ref.dtype)
            qo_ref[:, lo + 128:lo + 256] = _rope(q_ref[:, lo + 128:lo + 256], c, sn).astype(qo_ref.dtype)
        kro_ref[...] = _rope(kr_ref[...], c, sn).astype(kro_ref.dtype)

    return pl.pallas_call(
        body, name="qk_rope_fwd", grid=(s // tm,),
        in_specs=[pl.BlockSpec((tm, wq), lambda i: (i, 0)), pl.BlockSpec((tm, 128), lambda i: (i, COL_KR // 128)),
                  pl.BlockSpec((tm, 128), lambda i: (i, 0)), pl.BlockSpec((tm, 128), lambda i: (i, 0))],
        out_specs=[pl.BlockSpec((tm, wq), lambda i: (i, 0)), pl.BlockSpec((tm, 128), lambda i: (i, 0))],
        out_shape=[jax.ShapeDtypeStruct((s, wq), jnp.bfloat16), jax.ShapeDtypeStruct((s, 128), jnp.bfloat16)],
        compiler_params=_params(("parallel",)),
    )(q_pad, proj, cos, sin)


def _qk_rope_bwd(dq_pad, dk_pad, dv, cos, sin, *, tm):
    s = dq_pad.shape[0]
    wq = N_HEADS * HEAD_PAD

    def body(dq_ref, dk_ref, dv_ref, cos_ref, sin_ref, dqo_ref, dkv_ref, dkr_ref):
        c, sn = cos_ref[...], sin_ref[...]
        dkr = jnp.zeros((tm, 128), jnp.float32)
        for h in range(N_HEADS):
            lo = h * HEAD_PAD
            dqo_ref[:, lo:lo + 128] = dq_ref[:, lo:lo + 128].astype(dqo_ref.dtype)
            dqo_ref[:, lo + 128:lo + 256] = _rope_transposed(dq_ref[:, lo + 128:lo + 256], c, sn).astype(dqo_ref.dtype)
            dkv_ref[:, lo:lo + 128] = dk_ref[:, lo:lo + 128].astype(dkv_ref.dtype)
            dkv_ref[:, lo + 128:lo + 256] = dv_ref[:, h * V_DIM:(h + 1) * V_DIM].astype(dkv_ref.dtype)
            dkr = dkr + dk_ref[:, lo + 128:lo + 256]
        dkr_ref[...] = _rope_transposed(dkr, c, sn).astype(dkr_ref.dtype)

    return pl.pallas_call(
        body, name="qk_rope_bwd", grid=(s // tm,),
        in_specs=[pl.BlockSpec((tm, wq), lambda i: (i, 0)), pl.BlockSpec((tm, wq), lambda i: (i, 0)),
                  pl.BlockSpec((tm, N_HEADS * V_DIM), lambda i: (i, 0)),
                  pl.BlockSpec((tm, 128), lambda i: (i, 0)), pl.BlockSpec((tm, 128), lambda i: (i, 0))],
        out_specs=[pl.BlockSpec((tm, wq), lambda i: (i, 0)), pl.BlockSpec((tm, wq), lambda i: (i, 0)),
                   pl.BlockSpec((tm, 128), lambda i: (i, 0))],
        out_shape=[jax.ShapeDtypeStruct((s, wq), jnp.bfloat16), jax.ShapeDtypeStruct((s, wq), jnp.bfloat16),
                   jax.ShapeDtypeStruct((s, 128), jnp.bfloat16)],
        compiler_params=_params(("parallel",)),
    )(dq_pad, dk_pad, dv, cos, sin)


def _visible(q0, k0, t):
    qpos = q0 + lax.broadcasted_iota(jnp.int32, (t, t), 0)
    kpos = k0 + lax.broadcasted_iota(jnp.int32, (t, t), 1)
    return lax.shift_right_logical(kpos, CHUNK_SHIFT) <= lax.shift_right_logical(qpos, CHUNK_SHIFT)


def _attn_fwd(q, kv, kr, attn_out_g, *, t, jobs=()):
    s = q.shape[0]
    nq = s // t

    def body(q_ref, kn_ref, v_ref, kr_ref, g_ref, o_ref, lse_ref, y_ref, kcat_ref):
        i = pl.program_id(1)

        @pl.when(i == 0)
        def _():
            kcat_ref[:, 0:128] = kn_ref[...]
            kcat_ref[:, 128:256] = kr_ref[...]

        qv = q_ref[...]

        def step(j, carry):
            m, l, acc = carry
            k = kcat_ref[pl.ds(pl.multiple_of(j * t, t), t), :]
            v = v_ref[pl.ds(pl.multiple_of(j * t, t), t), :]
            sc = lax.dot_general(qv, k, _NT, preferred_element_type=jnp.float32) * ATTN_SCALE
            sc = jnp.where(_visible(i * t, j * t, t), sc, NEG_INF)
            m_new = jnp.maximum(m, jnp.max(sc, axis=-1, keepdims=True))
            p = jnp.exp(sc - m_new)
            alpha = jnp.exp(m - m_new)
            l = alpha * l + jnp.sum(p, axis=-1, keepdims=True)
            acc = alpha * acc + lax.dot_general(p.astype(jnp.bfloat16), v, _NN, preferred_element_type=jnp.float32)
            return m_new, l, acc

        init = (jnp.full((t, 1), NEG_INF, jnp.float32), jnp.zeros((t, 1), jnp.float32),
                jnp.zeros((t, V_DIM), jnp.float32))
        m, l, acc = lax.fori_loop(0, i + 1, step, init)
        o = acc / l
        o_ref[...] = o
        lse_ref[...] = jnp.broadcast_to(m + jnp.log(l), (t, 128))
        y_ref[...] = (o * _rstd(o) * g_ref[...]).astype(y_ref.dtype)

    head_rows = lambda w, f: pl.BlockSpec((s, w), lambda h, i: (0, f(h)))
    blk = pl.BlockSpec((t, 128), lambda h, i: (i, h))
    full = jax.ShapeDtypeStruct((s, N_HEADS * V_DIM), jnp.float32)
    return _call(
        body, name="attn_fwd", grid=(N_HEADS, nq),
        in_specs=[pl.BlockSpec((t, HEAD_PAD), lambda h, i: (i, h)), head_rows(128, lambda h: 2 * h),
                  head_rows(128, lambda h: 2 * h + 1), head_rows(128, lambda h: 0),
                  pl.BlockSpec((1, 128), lambda h, i: (0, h))],
        out_specs=[blk, blk, blk],
        out_shape=[full, full, jax.ShapeDtypeStruct((s, N_HEADS * V_DIM), jnp.bfloat16)],
        scratch_shapes=[pltpu.VMEM((s, HEAD_PAD), jnp.bfloat16)],
        semantics=("arbitrary", "arbitrary"), args=(q, kv, kv, kr, attn_out_g), jobs=jobs)


def _attn_norm_bwd(o, attn_out_g, dycat):
    s = o.shape[0]

    def body(o_ref, g_ref, dy_ref, do_ref, delta_ref, dg_ref):
        ov = o_ref[...]
        do, dgr = _rms_bwd_rows(ov, g_ref[...], dy_ref[...])
        do_ref[...] = do.astype(do_ref.dtype)
        delta_ref[...] = jnp.broadcast_to(jnp.sum(do * ov, axis=-1, keepdims=True), (s, 128))
        dg_ref[...] = jnp.sum(dgr, axis=0, keepdims=True)

    col = lambda base: pl.BlockSpec((s, 128), lambda h: (0, base + h))
    return pl.pallas_call(
        body, name="attn_norm_bwd", grid=(N_HEADS,),
        in_specs=[col(0), pl.BlockSpec((1, 128), lambda h: (0, h)), col(CONV_WIDTH // 128)],
        out_specs=[col(0), col(0), pl.BlockSpec((1, 128), lambda h: (0, h))],
        out_shape=[jax.ShapeDtypeStruct((s, N_HEADS * V_DIM), jnp.bfloat16),
                   jax.ShapeDtypeStruct((s, N_HEADS * V_DIM), jnp.float32),
                   jax.ShapeDtypeStruct((1, N_HEADS * V_DIM), jnp.float32)],
        compiler_params=_params(("parallel",)),
    )(o, attn_out_g, dycat)


def _attn_bwd(q, kv, kr, do, lse, delta, *, t, jobs=()):
    s = q.shape[0]
    nq = s // t

    def body(q_ref, kn_ref, v_ref, kr_ref, do_ref, lse_ref, delta_ref, dq_ref, dk_ref, dv_ref, kcat_ref):
        kcat_ref[:, 0:128] = kn_ref[...]
        kcat_ref[:, 128:256] = kr_ref[...]
        dq_ref[...] = jnp.zeros_like(dq_ref)
        dk_ref[...] = jnp.zeros_like(dk_ref)
        dv_ref[...] = jnp.zeros_like(dv_ref)

        def kv_step(j, _):
            krows = pl.ds(pl.multiple_of(j * t, t), t)
            k = kcat_ref[krows, :]
            v = v_ref[krows, :]

            def q_step(i, _):
                qrows = pl.ds(pl.multiple_of(i * t, t), t)
                qv = q_ref[qrows, :]
                dov = do_ref[qrows, :]
                sc = lax.dot_general(qv, k, _NT, preferred_element_type=jnp.float32) * ATTN_SCALE
                sc = jnp.where(_visible(i * t, j * t, t), sc, NEG_INF)
                p = jnp.exp(sc - lse_ref[qrows, :][:, 0:1])
                dp = lax.dot_general(dov, v, _NT, preferred_element_type=jnp.float32)
                ds = (p * (dp - delta_ref[qrows, :][:, 0:1]) * ATTN_SCALE).astype(jnp.bfloat16)
                dv_ref[krows, :] += lax.dot_general(p.astype(jnp.bfloat16), dov, _TN,
                                                    preferred_element_type=jnp.float32)
                dk_ref[krows, :] += lax.dot_general(ds, qv, _TN, preferred_element_type=jnp.float32)
                dq_ref[qrows, :] += lax.dot_general(ds, k, _NN, preferred_element_type=jnp.float32)
                return 0

            lax.fori_loop(j, nq, q_step, 0)
            return 0

        lax.fori_loop(0, nq, kv_step, 0)

    col = lambda w, f: pl.BlockSpec((s, w), lambda h: (0, f(h)))
    return _call(
        body, name="attn_bwd", grid=(N_HEADS,),
        in_specs=[col(HEAD_PAD, lambda h: h), col(128, lambda h: 2 * h), col(128, lambda h: 2 * h + 1),
                  col(128, lambda h: 0), col(128, lambda h: h), col(128, lambda h: h), col(128, lambda h: h)],
        out_specs=[col(HEAD_PAD, lambda h: h), col(HEAD_PAD, lambda h: h), col(128, lambda h: h)],
        out_shape=[jax.ShapeDtypeStruct((s, N_HEADS * HEAD_PAD), jnp.float32),
                   jax.ShapeDtypeStruct((s, N_HEADS * HEAD_PAD), jnp.float32),
                   jax.ShapeDtypeStruct((s, N_HEADS * V_DIM), jnp.float32)],
        scratch_shapes=[pltpu.VMEM((s, HEAD_PAD), jnp.bfloat16)],
        semantics=("parallel",), args=(q, kv, kv, kr, do, lse, delta), jobs=jobs)


def _row_tile(rows):
    for cand in (256, 128, 64, 32, 16, 8):
        if rows % cand == 0:
            return cand
    return rows


def _cast_into_slot(w, pos, *, name):
    r, c = w.shape
    tr = _row_tile(r)

    def body(pos_ref, w_ref, o_ref):
        o_ref[...] = w_ref[...].astype(o_ref.dtype)

    return pl.pallas_call(
        body, name=name, out_shape=jax.ShapeDtypeStruct((4, r, c), jnp.bfloat16),
        grid_spec=pltpu.PrefetchScalarGridSpec(
            num_scalar_prefetch=1, grid=(r // tr,),
            in_specs=[pl.BlockSpec((tr, c), lambda i, p: (i, 0))],
            out_specs=pl.BlockSpec((None, tr, c), lambda i, p: (p[1], i, 0))),
        compiler_params=_params(("parallel",)),
    )(pos, w)


def _pair_add(g, theirs, pos, *, name):
    n, h, c = theirs.shape
    tr = _row_tile(h)
    nb = h // tr

    def body(pos_ref, a_ref, b_ref, o_ref):
        o_ref[...] = (a_ref[...].astype(jnp.float32) + b_ref[...].astype(jnp.float32)).astype(o_ref.dtype)

    spec = pl.BlockSpec((None, tr, c), lambda j, i, p: (j, i, 0))
    return pl.pallas_call(
        body, name=name, out_shape=jax.ShapeDtypeStruct(theirs.shape, jnp.bfloat16),
        grid_spec=pltpu.PrefetchScalarGridSpec(
            num_scalar_prefetch=1, grid=(n, nb),
            in_specs=[pl.BlockSpec((None, tr, c), lambda j, i, p: (j, i + p[0] * nb, 0)), spec],
            out_specs=spec),
        compiler_params=_params(("parallel", "parallel")),
    )(pos, g, theirs)


def _chip_sum(by_source, pair_sum, pos, *, name):
    n, h, c = by_source.shape
    tr = _row_tile(h)
    nb = h // tr

    def body(pos_ref, p0, p1, p2, p3, own_ref, o_ref):
        own = own_ref[...].astype(jnp.float32)
        term = lambda k, ref: jnp.where(pos_ref[1] == k, own, ref[...].astype(jnp.float32))
        o_ref[...] = ((term(0, p0) + term(1, p1)) + term(2, p2)) + term(3, p3)

    def source(k):
        return pl.BlockSpec((None, tr, c), lambda i, p: (jnp.where(p[1] == k, (k + 1) % 4, k), i, 0))

    return pl.pallas_call(
        body, name=name, out_shape=jax.ShapeDtypeStruct((2 * h, c), jnp.float32),
        grid_spec=pltpu.PrefetchScalarGridSpec(
            num_scalar_prefetch=1, grid=(nb,),
            in_specs=[source(0), source(1), source(2), source(3),
                      pl.BlockSpec((None, tr, c), lambda i, p: (p[1], i, 0))],
            out_specs=pl.BlockSpec((tr, c), lambda i, p: (i + p[0] * nb, 0))),
        compiler_params=_params(("parallel",)),
    )(pos, by_source, by_source, by_source, by_source, pair_sum)


def _adamw(w, g, m, v, *, name, jobs=()):
    r, c = w.shape
    tr = _row_tile(r)

    def body(w_ref, g_ref, m_ref, v_ref, d_ref, mo_ref, vo_ref):
        gv = g_ref[...]
        mn = ADAM_B1 * m_ref[...] + (1.0 - ADAM_B1) * gv
        vn = ADAM_B2 * v_ref[...] + (1.0 - ADAM_B2) * (gv * gv)
        m_hat = mn / (1.0 - ADAM_B1 ** ADAM_STEP)
        v_hat = vn / (1.0 - ADAM_B2 ** ADAM_STEP)
        d_ref[...] = -ADAM_LR * (m_hat / (jnp.sqrt(v_hat) + ADAM_EPS) + ADAM_WD * w_ref[...])
        mo_ref[...] = mn
        vo_ref[...] = vn

    spec = pl.BlockSpec((tr, c), lambda i: (i, 0))
    out = jax.ShapeDtypeStruct((r, c), jnp.float32)
    return _call(body, name=name, grid=(r // tr,), in_specs=[spec] * 4, out_specs=[spec] * 3, out_shape=[out] * 3,
                 semantics=("parallel",), args=(w, g, m, v), jobs=jobs)


def _all_reduce_small(block):
    r, c = block.shape

    def body(src_ref, out_ref, stage_ref, send_sems, recv_sems):
        x, y, cc = _position()
        me = 4 * x + 2 * y + cc
        stage_ref[me] = src_ref[...]
        flip = lambda v, on: 1 - v if on else v
        peers = [(flip(x, dx), flip(y, dy), flip(cc, dc)) for dx in (0, 1) for dy in (0, 1) for dc in (0, 1)][1:]
        copies = [pltpu.make_async_remote_copy(
            src_ref=stage_ref.at[me], dst_ref=stage_ref.at[me],
            send_sem=send_sems.at[k], recv_sem=recv_sems.at[k], device_id=peer, device_id_type=MESH)
            for k, peer in enumerate(peers)]
        for cp in copies:
            cp.start()
        for k, (px, py, pc) in enumerate(peers):
            them = 4 * px + 2 * py + pc
            pltpu.make_async_remote_copy(
                src_ref=stage_ref.at[them], dst_ref=stage_ref.at[them],
                send_sem=send_sems.at[k], recv_sem=recv_sems.at[k], device_id=(px, py, pc),
                device_id_type=MESH).wait_recv()
        for cp in copies:
            cp.wait_send()
        total = stage_ref[0]
        for d in range(1, 8):
            total = total + stage_ref[d]
        out_ref[...] = total

    return pl.pallas_call(
        body, name="all_reduce_small",
        in_specs=[pl.BlockSpec(memory_space=pltpu.VMEM)], out_specs=pl.BlockSpec(memory_space=pltpu.VMEM),
        out_shape=jax.ShapeDtypeStruct((r, c), jnp.float32),
        scratch_shapes=[pltpu.VMEM((8, r, c), jnp.float32), pltpu.SemaphoreType.DMA((7,)),
                        pltpu.SemaphoreType.DMA((7,))],
        compiler_params=pltpu.CompilerParams(has_side_effects=True),
    )(block)


def _cols_from_shards(g):
    n, r, c = g.shape
    return jnp.transpose(g, (1, 0, 2)).reshape(r, n * c)


def _cols_to_shards(w, n=4):
    r, c = w.shape
    return jnp.transpose(w.reshape(r, n, c // n), (1, 0, 2))


def _pad_w_in(full):
    d = full.shape[0]
    zeros = jnp.zeros((d, COL_CKV - COL_KR - QK_ROPE), full.dtype)
    return jnp.concatenate([full[:, :COL_KR], full[:, IN_WIDTH - QK_ROPE:], zeros,
                            full[:, COL_KR:COL_KR + KV_RANK]], axis=1)


def _unpad_w_in(padded):
    return jnp.concatenate([padded[:, :COL_KR], padded[:, COL_CKV:COL_CKV + KV_RANK],
                            padded[:, COL_KR:COL_KR + QK_ROPE]], axis=1)


def _pad_w_uq(full):
    r = full.shape[0]
    per_head = full.reshape(r, N_HEADS, QK_NOPE + QK_ROPE)
    return jnp.pad(per_head, ((0, 0), (0, 0), (0, HEAD_PAD - QK_NOPE - QK_ROPE))).reshape(r, N_HEADS * HEAD_PAD)


def _unpad_w_uq(padded):
    r = padded.shape[0]
    return padded.reshape(r, N_HEADS, HEAD_PAD)[:, :, :QK_NOPE + QK_ROPE].reshape(r, N_HEADS * (QK_NOPE + QK_ROPE))


SMALL_ROWS = 16


def _pack_small(d, pre_mix, post_mix, pre_mlp, post_mlp, conv_out, attn_out, q_norm, kv_norm, conv_w):
    row = lambda *parts: jnp.pad(jnp.concatenate(parts, axis=1), ((0, 0), (0, d - sum(p.shape[1] for p in parts))))
    rows = [row(pre_mix), row(post_mix), row(pre_mlp), row(post_mlp), row(conv_out, attn_out), row(q_norm, kv_norm),
            row(conv_w[0:1]), row(conv_w[1:2]), row(conv_w[2:3])]
    return jnp.pad(jnp.concatenate(rows, axis=0), ((0, SMALL_ROWS - len(rows)), (0, 0)))


def _unpack_small(p, chip):
    cw = CONV_WIDTH // 4
    conv_w = lax.dynamic_slice(p[6:9, :CONV_WIDTH], (0, chip * cw), (3, cw))
    return dict(pre_mix_g=p[0:1], post_mix_g=p[1:2], pre_mlp_g=p[2:3], post_mlp_g=p[3:4],
                conv_out_g=p[4:5, :CONV_WIDTH], attn_out_g=p[4:5, CONV_WIDTH:2 * CONV_WIDTH],
                q_norm_g=p[5:6, :Q_RANK], kv_norm_g=p[5:6, Q_RANK:Q_RANK + KV_RANK], conv_w=conv_w[None])


def kernel(x, pre_mix_g, w_in, conv_w, q_norm_g, w_uq, kv_norm_g, w_ukv, conv_out_g, attn_out_g, w_o, post_mix_g, pre_mlp_g, w_up, w_down, post_mlp_g, loss_target, m_pre_mix_g, m_w_in, m_conv_w, m_q_norm_g, m_w_uq, m_kv_norm_g, m_w_ukv, m_conv_out_g, m_attn_out_g, m_w_o, m_post_mix_g, m_pre_mlp_g, m_w_up, m_w_down, m_post_mlp_g, v_pre_mix_g, v_w_in, v_conv_w, v_q_norm_g, v_w_uq, v_kv_norm_g, v_w_ukv, v_conv_out_g, v_attn_out_g, v_w_o, v_post_mix_g, v_pre_mlp_g, v_w_up, v_w_down, v_post_mlp_g):
    bf16 = jnp.bfloat16
    s, d = x.shape[1], x.shape[2]
    d_ff = 4 * d
    chip = 2 * lax.axis_index("x") + lax.axis_index("y")
    xs = x.reshape(s, d)
    target = loss_target.reshape(s, d)
    tm = min(256, s)
    t_attn = min(256, s)
    mt = min(1024, s)

    big = dict(w_in=w_in[0], w_uq=w_uq[0], w_ukv=w_ukv[0], w_o=w_o[0], w_up=w_up[0], w_down=w_down[0])
    names = list(big)
    pos = jnp.stack([lax.axis_index("c"), chip]).astype(jnp.int32)
    wb = {k: _Buf(_cast_into_slot(big[k], pos, name="cast_" + k)) for k in names}
    half = {k: big[k].shape[0] // 2 for k in names}
    ici = lambda k, lo=0, n=None: _GatherIci(wb[k], lo, half[k] - lo if n is None else n)
    d2d = lambda k, lo=0, n=None: _GatherD2d(wb[k], lo, half[k] - lo if n is None else n)
    down_cuts = (0, 3 * half["w_down"] // 16, 5 * half["w_down"] // 16, half["w_down"])
    _comm("gather_w_in", [[ici("w_in")], [d2d("w_in")]])
    win = _pad_w_in(_cols_from_shards(wb["w_in"].arr))
    ff4 = d_ff // 4

    spread = lambda a: lax.dynamic_update_slice(jnp.zeros((3, CONV_WIDTH), jnp.float32), a[0],
                                                (0, chip * (CONV_WIDTH // 4)))
    conv_w_mine = jnp.where(lax.axis_index("c") == 0, spread(conv_w), 0.0)
    conv_w_full = _all_reduce_small(jnp.pad(conv_w_mine, ((0, 5), (0, 0))))[0:3]

    h1 = _rms_fwd(xs, pre_mix_g, width=d, col=0, tm=tm, name="rms_pre_mix")
    proj = _matmul(h1, win, dims=_NN, mnk=(s, IN_PAD, d), tiles=(mt, 512, 512), name="mm_proj",
                   jobs=[ici("w_uq"), ici("w_ukv"), ici("w_o")])
    y_conv = _conv_fwd(proj, conv_w_full, conv_out_g, jobs=[d2d("w_uq"), d2d("w_ukv"), d2d("w_o")])
    wuq = _pad_w_uq(_cols_from_shards(wb["w_uq"].arr))
    wukv = _cols_from_shards(wb["w_ukv"].arr)
    wo = wb["w_o"].arr.reshape(-1, d)
    cqn = _rms_fwd(proj, q_norm_g, width=Q_RANK, col=COL_CQ // Q_RANK, tm=tm, name="rms_q")
    ckvn = _rms_fwd(proj, kv_norm_g, width=KV_RANK, col=COL_CKV // KV_RANK, tm=tm, name="rms_kv")
    q_pad = _matmul(cqn, wuq, dims=_NN, mnk=(s, N_HEADS * HEAD_PAD, Q_RANK), tiles=(mt, 1024, Q_RANK), name="mm_q")
    kv = _matmul(ckvn, wukv, dims=_NN, mnk=(s, N_HEADS * HEAD_PAD, KV_RANK), tiles=(mt, 1024, KV_RANK),
                 name="mm_kv", out_dtype=bf16)
    cos, sin = _rope_tables(s)
    q_rot, kr_rot = _qk_rope_fwd(q_pad, proj, cos, sin, tm=tm)
    o, lse, y_attn = _attn_fwd(q_rot, kv, kr_rot, attn_out_g, t=t_attn, jobs=[ici("w_up")])
    ycat = jnp.concatenate([y_conv, y_attn], axis=1)
    mix = _matmul(ycat, wo, dims=_NN, mnk=(s, d, 2 * CONV_WIDTH), tiles=(mt, 1024, 512), name="mm_out",
                  jobs=[d2d("w_up"), ici("w_down", down_cuts[0], down_cuts[1] - down_cuts[0])])
    x2, h2 = _mix_residual_fwd(xs, mix, post_mix_g, pre_mlp_g, tm=tm,
                               jobs=[d2d("w_down", down_cuts[0], down_cuts[1] - down_cuts[0]),
                                     ici("w_down", down_cuts[1], down_cuts[2] - down_cuts[1])])
    wup = wb["w_up"].arr

    def up_epilogue(acc, extra_refs, out_refs):
        r = jnp.maximum(acc, 0.0)
        out_refs[0][...] = acc.astype(bf16)
        out_refs[1][...] = (r * r).astype(bf16)

    n_ff = ff4 // 1024
    act = jax.ShapeDtypeStruct((s, d_ff), bf16)
    up, act_sq = _matmul(
        h2, wup, dims=_NN, mnk=(s, d_ff, d), tiles=(mt, 1024, 512), name="mm_up",
        b_spec=pl.BlockSpec((None, 512, 1024), lambda i, j, l: (j // n_ff, l, j % n_ff)),
        out_shape=(act, act), o_spec=(pl.BlockSpec((mt, 1024), lambda i, j, l: (i, j)),) * 2, epilogue=up_epilogue,
        jobs=[d2d("w_down", down_cuts[1], down_cuts[2] - down_cuts[1]), ici("w_down", down_cuts[2])])
    _comm("gather_w_down_tail", [[d2d("w_down", down_cuts[2])]])
    wdown = wb["w_down"].arr.reshape(d_ff, d)
    mlp = _matmul(act_sq, wdown, dims=_NN, mnk=(s, d, d_ff), tiles=(mt, 1024, 512), name="mm_down")
    dx3, dmlp, dg_post_mlp, loss_part = _loss_head(x2, mlp, target, post_mlp_g, tm=tm)

    def dup_epilogue(acc, extra_refs, out_refs):
        out_refs[0][...] = (acc * (2.0 * jnp.maximum(extra_refs[0][...].astype(jnp.float32), 0.0))).astype(bf16)

    dup = _matmul(dmlp, wdown, dims=_NT, mnk=(s, d_ff, d), tiles=(mt, 1024, 512), name="mm_dact",
                  out_dtype=bf16, epilogue=dup_epilogue, extra=(up,),
                  extra_specs=(pl.BlockSpec((mt, 1024), lambda i, j, l: (i, j)),))
    pair_sums, by_source = {}, {}

    def pair_reduce(grads):
        theirs = {k: _Buf(jax.ShapeDtypeStruct((4, g.shape[1] // 2, g.shape[2]), bf16)) for k, g in grads.items()}
        _comm("pair_exchange_" + "_".join(grads), [[_PairExchange(g, theirs[k]) for k, g in grads.items()]])
        for k, g in grads.items():
            pair_sums[k] = _pair_add(g, theirs[k].arr, pos, name="pair_add_" + k)
            by_source[k] = _Buf(jax.ShapeDtypeStruct(pair_sums[k].shape, bf16))

    scatter = lambda k, lo=0, n=None: _ScatterIci(pair_sums[k], by_source[k], lo, half[k] - lo if n is None else n)

    g_wdown = _matmul(act_sq, dmlp, dims=_TN, mnk=(d_ff, d, s), tiles=(1024, 1024, min(512, s)), name="mm_gw_down",
                      out_dtype=bf16).reshape(4, ff4, d)
    pair_reduce(dict(w_down=g_wdown))
    n_ffk = ff4 // 512
    h_ff = half["w_down"]
    dh2 = _matmul(dup, wup, dims=_NT, mnk=(s, d, d_ff), tiles=(mt, 1024, 512), name="mm_dh2",
                  b_spec=pl.BlockSpec((None, 1024, 512), lambda i, j, l: (l // n_ffk, j, l % n_ffk)),
                  jobs=[scatter("w_down", 0, h_ff // 2)])
    g_wup = _matmul(h2, dup, dims=_TN, mnk=(d, d_ff, s), tiles=(1024, 1024, min(512, s)), name="mm_gw_up",
                    out_shape=jax.ShapeDtypeStruct((4, d, ff4), bf16),
                    o_spec=pl.BlockSpec((None, 1024, 1024), lambda i, j, l: (j // n_ff, i, j % n_ff)),
                    jobs=[scatter("w_down", h_ff // 2)])
    pair_reduce(dict(w_up=g_wup))
    dx2, dmix, dg_pre_mlp, dg_post_mix = _mix_residual_bwd(dx3, dh2, x2, mix, pre_mlp_g, post_mix_g, tm=tm)

    up_cuts = (0, 3 * half["w_up"] // 16, 11 * half["w_up"] // 32, half["w_up"])
    dycat = _matmul(dmix, wo, dims=_NT, mnk=(s, 2 * CONV_WIDTH, d), tiles=(mt, 1024, 512), name="mm_dycat",
                    jobs=[scatter("w_up", up_cuts[0], up_cuts[1] - up_cuts[0])])
    g_wo = _matmul(ycat, dmix, dims=_TN, mnk=(2 * CONV_WIDTH, d, s), tiles=(1024, 1024, min(512, s)),
                   name="mm_gw_out", out_dtype=bf16, jobs=[scatter("w_up", up_cuts[1], up_cuts[2] - up_cuts[1])]
                   ).reshape(4, CONV_WIDTH // 2, d)
    pair_reduce(dict(w_o=g_wo))
    du, dgb, dgc, dg_conv_w, dg_conv_out = _conv_bwd(proj, conv_w_full, conv_out_g, dycat)
    do, delta, dg_attn_out = _attn_norm_bwd(o, attn_out_g, dycat)
    dq_pad, dk_pad, dv = _attn_bwd(q_rot, kv, kr_rot, do, lse, delta, t=t_attn,
                                   jobs=[scatter("w_up", up_cuts[2]), scatter("w_o")])
    dq_raw, dkv, dkr = _qk_rope_bwd(dq_pad, dk_pad, dv, cos, sin, tm=tm)
    wq_cols = N_HEADS * HEAD_PAD
    g_wuq = _matmul(cqn, dq_raw, dims=_TN, mnk=(Q_RANK, wq_cols, s), tiles=(Q_RANK, 1024, min(512, s)),
                    name="mm_gw_uq", out_dtype=bf16)
    dcqn = _matmul(dq_raw, wuq, dims=_NT, mnk=(s, Q_RANK, wq_cols), tiles=(mt, Q_RANK, 512), name="mm_dcq")
    g_wukv = _matmul(ckvn, dkv, dims=_TN, mnk=(KV_RANK, wq_cols, s), tiles=(KV_RANK, 1024, min(512, s)),
                     name="mm_gw_ukv", out_dtype=bf16)
    dckvn = _matmul(dkv, wukv, dims=_NT, mnk=(s, KV_RANK, wq_cols), tiles=(mt, KV_RANK, 512), name="mm_dckv")
    dcq, dg_q_norm = _rms_bwd(proj, q_norm_g, dcqn, width=Q_RANK, col=COL_CQ // Q_RANK, tm=tm, name="rms_q_bwd")
    dckv, dg_kv_norm = _rms_bwd(proj, kv_norm_g, dckvn, width=KV_RANK, col=COL_CKV // KV_RANK, tm=tm,
                                name="rms_kv_bwd")
    pair_reduce(dict(w_uq=_cols_to_shards(_unpad_w_uq(g_wuq)), w_ukv=_cols_to_shards(g_wukv)))
    dproj = jnp.concatenate([du, dgb, dgc, dcq, dkr, jnp.zeros((s, COL_CKV - COL_KR - 128), bf16), dckv], axis=1)
    dh1 = _matmul(dproj, win, dims=_NT, mnk=(s, d, IN_PAD), tiles=(mt, 1024, 512), name="mm_dh1",
                  jobs=[scatter("w_uq"), scatter("w_ukv")])
    g_win = _matmul(h1, dproj, dims=_TN, mnk=(d, IN_PAD, s), tiles=(1024, 512, min(512, s)), name="mm_gw_in",
                    out_dtype=bf16)
    pair_reduce(dict(w_in=_cols_to_shards(_unpad_w_in(g_win))))
    in_cuts = (0, half["w_in"] // 4, 5 * half["w_in"] // 8, half["w_in"])
    grad_x, dg_pre_mix = _input_bwd(dx2, dh1, xs, pre_mix_g, tm=tm,
                                    jobs=[scatter("w_in", in_cuts[0], in_cuts[1] - in_cuts[0])])

    moments = dict(w_in=(m_w_in, v_w_in), w_uq=(m_w_uq, v_w_uq), w_ukv=(m_w_ukv, v_w_ukv), w_o=(m_w_o, v_w_o),
                   w_up=(m_w_up, v_w_up), w_down=(m_w_down, v_w_down))
    gw, dw, nm, nv = {}, {}, {}, {}

    def finish(keys, carried):
        whole = {k: _Buf(_chip_sum(by_source[k].arr, pair_sums[k], pos, name="chip_sum_" + k)) for k in keys}
        _comm("pair_share_" + "_".join(keys), [[_PairShare(whole[k]) for k in keys]])
        for k in keys:
            g = whole[k].arr
            delta_k, nm_k, nv_k = _adamw(big[k], g, moments[k][0][0], moments[k][1][0], name="adamw_" + k,
                                         jobs=carried.get(k, ()))
            gw[k], dw[k], nm[k], nv[k] = g[None], delta_k[None], nm_k[None], nv_k[None]

    finish(["w_down", "w_up", "w_o", "w_uq", "w_ukv"],
           dict(w_down=[scatter("w_in", in_cuts[1], in_cuts[2] - in_cuts[1])], w_up=[scatter("w_in", in_cuts[2])]))
    finish(["w_in"], {})

    small_g = _all_reduce_small(_pack_small(d, dg_pre_mix, dg_post_mix, dg_pre_mlp, dg_post_mlp, dg_conv_out,
                                            dg_attn_out, dg_q_norm, dg_kv_norm, dg_conv_w))
    pack_w = lambda cw, pre_mix, post_mix, pre_mlp, post_mlp, conv_out, attn_out, q_norm, kv_norm: _pack_small(
        d, pre_mix, post_mix, pre_mlp, post_mlp, conv_out, attn_out, q_norm, kv_norm, cw)
    small_w = pack_w(conv_w_full, pre_mix_g, post_mix_g, pre_mlp_g, post_mlp_g, conv_out_g, attn_out_g, q_norm_g,
                     kv_norm_g)
    small_m = pack_w(spread(m_conv_w), m_pre_mix_g, m_post_mix_g, m_pre_mlp_g, m_post_mlp_g, m_conv_out_g,
                     m_attn_out_g, m_q_norm_g, m_kv_norm_g)
    small_v = pack_w(spread(v_conv_w), v_pre_mix_g, v_post_mix_g, v_pre_mlp_g, v_post_mlp_g, v_conv_out_g,
                     v_attn_out_g, v_q_norm_g, v_kv_norm_g)
    small_d, small_nm, small_nv = _adamw(small_w, small_g, small_m, small_v, name="adamw_small")
    sg, sd, snm, snv = (_unpack_small(p, chip) for p in (small_g, small_d, small_nm, small_nv))

    for src, dst in ((sg, gw), (sd, dw), (snm, nm), (snv, nv)):
        dst.update(src)

    loss = lax.psum(loss_part[0, 0], ("x", "y", "c"))
    order = ["pre_mix_g", "w_in", "conv_w", "q_norm_g", "w_uq", "kv_norm_g", "w_ukv", "conv_out_g", "attn_out_g",
             "w_o", "post_mix_g", "pre_mlp_g", "w_up", "w_down", "post_mlp_g"]
    return (loss, grad_x.reshape(1, s, d), *[gw[k] for k in order], *[dw[k] for k in order],
            *[nm[k] for k in order], *[nv[k] for k in order])
```

```python
import functools

import jax
import jax.numpy as jnp
from jax import lax
from jax.experimental import pallas as pl
from jax.experimental.pallas import tpu as pltpu

EPS = 1e-6
NEG_INF = -1e30
CHUNK_SHIFT = 6
N_HEADS = 8
HEAD_PAD = 256
QK_NOPE = 128
QK_ROPE = 64
V_DIM = 128
CONV_WIDTH = 1024
Q_RANK = 768
KV_RANK = 512
ROPE_THETA = 10000.0
ATTN_SCALE = (QK_NOPE + QK_ROPE) ** -0.5
ADAM_LR, ADAM_B1, ADAM_B2, ADAM_EPS, ADAM_WD, ADAM_STEP = 0.001, 0.9, 0.999, 1e-08, 0.01, 10

COL_CQ = 3 * CONV_WIDTH
COL_KR = COL_CQ + Q_RANK
COL_CKV = 4096
IN_PAD = COL_CKV + KV_RANK
IN_WIDTH = 3 * CONV_WIDTH + Q_RANK + KV_RANK + QK_ROPE

VMEM_LIMIT_BYTES = 56 * 1024 * 1024
MESH = pl.DeviceIdType.MESH
ANY = pl.BlockSpec(memory_space=pl.ANY)

_NN = (((1,), (0,)), ((), ()))
_NT = (((1,), (1,)), ((), ()))
_TN = (((0,), (0,)), ((), ()))


def _params(sem):
    return pltpu.CompilerParams(dimension_semantics=sem, vmem_limit_bytes=VMEM_LIMIT_BYTES)


class _Buf:
    def __init__(self, arr):
        self.arr = arr


def _position():
    return lax.axis_index("x"), lax.axis_index("y"), lax.axis_index("c")


def _other_chips(x, y):
    return [(2 * (1 - x) + y, (1 - x, y)), (2 * x + (1 - y), (x, 1 - y)), (2 * (1 - x) + (1 - y), (1 - x, 1 - y))]


def _remote(src, dst, sems, k, to):
    send, recv, off = sems
    return pltpu.make_async_remote_copy(src_ref=src, dst_ref=dst, send_sem=send.at[off + k], recv_sem=recv.at[off + k],
                                        device_id=to, device_id_type=MESH)


class _GatherIci:
    n_sems = 3

    def __init__(self, buf, lo, n):
        self.buf, self.lo, self.n, self.buffers, self.sources = buf, lo, n, [buf], []

    def _rows(self, ref, slot, which):
        return ref.at[slot, pl.ds(which * (ref.shape[1] // 2) + self.lo, self.n), :]

    def start(self, refs, sems):
        ref = refs[id(self.buf)]
        x, y, c = _position()
        mine = self._rows(ref, 2 * x + y, c)
        for k, (_, xy) in enumerate(_other_chips(x, y)):
            _remote(mine, mine, sems, k, (*xy, c)).start()

    def wait(self, refs, sems):
        ref = refs[id(self.buf)]
        x, y, c = _position()
        mine = self._rows(ref, 2 * x + y, c)
        for k, (slot, xy) in enumerate(_other_chips(x, y)):
            landed = self._rows(ref, slot, c)
            _remote(landed, landed, sems, k, (*xy, c)).wait_recv()
            _remote(mine, mine, sems, k, (*xy, c)).wait_send()


class _GatherD2d(_GatherIci):
    def start(self, refs, sems):
        ref = refs[id(self.buf)]
        x, y, c = _position()
        for k, (slot, _) in enumerate(_other_chips(x, y)):
            rows = self._rows(ref, slot, c)
            _remote(rows, rows, sems, k, (x, y, 1 - c)).start()

    def wait(self, refs, sems):
        ref = refs[id(self.buf)]
        x, y, c = _position()
        for k, (slot, _) in enumerate(_other_chips(x, y)):
            sent, landed = self._rows(ref, slot, c), self._rows(ref, slot, 1 - c)
            _remote(landed, landed, sems, k, (x, y, 1 - c)).wait_recv()
            _remote(sent, sent, sems, k, (x, y, 1 - c)).wait_send()


class _ScatterIci:
    n_sems = 3

    def __init__(self, src, dst, lo, n):
        self.src, self.dst, self.lo, self.n, self.buffers, self.sources = src, dst, lo, n, [dst], [src]

    def _rows(self, ref, slot):
        return ref.at[slot, pl.ds(self.lo, self.n), :]

    def start(self, refs, sems):
        src, dst = refs[id(self.src)], refs[id(self.dst)]
        x, y, c = _position()
        for k, (slot, xy) in enumerate(_other_chips(x, y)):
            _remote(self._rows(src, slot), self._rows(dst, 2 * x + y), sems, k, (*xy, c)).start()

    def wait(self, refs, sems):
        src, dst = refs[id(self.src)], refs[id(self.dst)]
        x, y, c = _position()
        for k, (slot, xy) in enumerate(_other_chips(x, y)):
            _remote(self._rows(src, slot), self._rows(dst, slot), sems, k, (*xy, c)).wait_recv()
            _remote(self._rows(src, slot), self._rows(dst, slot), sems, k, (*xy, c)).wait_send()


class _PairExchange:
    n_sems = 1

    def __init__(self, src, dst):
        self.src, self.dst, self.buffers, self.sources = src, dst, [dst], [src]

    def _copy(self, refs, sems):
        src, dst = refs[id(self.src)], refs[id(self.dst)]
        x, y, c = _position()
        h = src.shape[1] // 2
        return _remote(src.at[:, pl.ds((1 - c) * h, h), :], dst, sems, 0, (x, y, 1 - c))

    def start(self, refs, sems):
        self._copy(refs, sems).start()

    def wait(self, refs, sems):
        self._copy(refs, sems).wait()


class _PairShare:
    n_sems = 1

    def __init__(self, buf):
        self.buf, self.buffers, self.sources = buf, [buf], []

    def _rows(self, ref, which):
        h = ref.shape[0] // 2
        return ref.at[pl.ds(which * h, h), :]

    def start(self, refs, sems):
        ref = refs[id(self.buf)]
        x, y, c = _position()
        _remote(self._rows(ref, c), self._rows(ref, c), sems, 0, (x, y, 1 - c)).start()

    def wait(self, refs, sems):
        ref = refs[id(self.buf)]
        x, y, c = _position()
        _remote(self._rows(ref, c), self._rows(ref, c), sems, 0, (x, y, 1 - c)).wait_send()
        _remote(self._rows(ref, 1 - c), self._rows(ref, 1 - c), sems, 0, (x, y, 1 - c)).wait_recv()


def _unique(items):
    seen, out = set(), []
    for it in items:
        if id(it) not in seen:
            seen.add(id(it))
            out.append(it)
    return out


def _job_operands(jobs):
    sources = _unique([a for j in jobs for a in j.sources])
    buffers = _unique([b for j in jobs for b in j.buffers])
    held = [b for b in buffers if not isinstance(b.arr, jax.ShapeDtypeStruct)]
    fresh = [b for b in buffers if isinstance(b.arr, jax.ShapeDtypeStruct)]
    return sources, held, fresh


def _sem_offsets(jobs):
    offs, total = [], 0
    for j in jobs:
        offs.append(total)
        total += j.n_sems
    return offs, total


def _call(body, *, name, grid, in_specs, out_specs, out_shape, args, semantics, scratch_shapes=(), jobs=()):
    if not jobs:
        return pl.pallas_call(
            body, name=name, grid=grid, in_specs=list(in_specs), out_specs=list(out_specs),
            out_shape=list(out_shape), scratch_shapes=list(scratch_shapes), compiler_params=_params(semantics),
        )(*args)
    sources, held, fresh = _job_operands(jobs)
    offs, n_sem = _sem_offsets(jobs)
    n_in, n_out, n_scr = len(in_specs), len(out_specs), len(scratch_shapes)
    n_src, n_held, n_fresh = len(sources), len(held), len(fresh)

    def carried(*refs):
        ins = refs[:n_in]
        src_refs = refs[n_in:n_in + n_src]
        o0 = n_in + n_src + n_held
        outs = refs[o0:o0 + n_out]
        buf_refs = refs[o0 + n_out:o0 + n_out + n_held + n_fresh]
        s0 = o0 + n_out + n_held + n_fresh
        scratch = refs[s0:s0 + n_scr]
        send, recv = refs[s0 + n_scr], refs[s0 + n_scr + 1]
        where = {id(a): r for a, r in zip(sources, src_refs)}
        where.update({id(b): r for b, r in zip(held + fresh, buf_refs)})
        ids = [pl.program_id(a) for a in range(len(grid))]
        first = functools.reduce(jnp.logical_and, [i == 0 for i in ids])
        last = functools.reduce(jnp.logical_and, [i == g - 1 for i, g in zip(ids, grid)])

        @pl.when(first)
        def _():
            for j, off in zip(jobs, offs):
                j.start(where, (send, recv, off))

        body(*ins, *outs, *scratch)

        @pl.when(last)
        def _():
            for j, off in zip(jobs, offs):
                j.wait(where, (send, recv, off))

    shape_of = lambda b: jax.ShapeDtypeStruct(b.arr.shape, b.arr.dtype)
    res = pl.pallas_call(
        carried, name=name, grid=grid,
        in_specs=[*in_specs, *[ANY] * (n_src + n_held)],
        out_specs=[*out_specs, *[ANY] * (n_held + n_fresh)],
        out_shape=[*out_shape, *[shape_of(b) for b in held + fresh]],
        input_output_aliases={n_in + n_src + i: n_out + i for i in range(n_held)},
        scratch_shapes=[*scratch_shapes, pltpu.SemaphoreType.DMA((n_sem,)), pltpu.SemaphoreType.DMA((n_sem,))],
        compiler_params=pltpu.CompilerParams(dimension_semantics=("arbitrary",) * len(grid),
                                             vmem_limit_bytes=VMEM_LIMIT_BYTES, has_side_effects=True),
    )(*args, *sources, *[b.arr for b in held])
    for b, new in zip(held + fresh, res[n_out:]):
        b.arr = new
    return list(res[:n_out])


def _comm(name, phases):
    jobs = [j for ph in phases for j in ph]
    sources, held, fresh = _job_operands(jobs)
    offs, n_sem = _sem_offsets(jobs)
    off_of = {id(j): o for j, o in zip(jobs, offs)}
    n_src, n_held, n_fresh = len(sources), len(held), len(fresh)

    def body(*refs):
        src_refs = refs[:n_src]
        buf_refs = refs[n_src + n_held:n_src + 2 * n_held + n_fresh]
        send, recv = refs[-2], refs[-1]
        where = {id(a): r for a, r in zip(sources, src_refs)}
        where.update({id(b): r for b, r in zip(held + fresh, buf_refs)})
        for ph in phases:
            for j in ph:
                j.start(where, (send, recv, off_of[id(j)]))
            for j in ph:
                j.wait(where, (send, recv, off_of[id(j)]))

    shape_of = lambda b: jax.ShapeDtypeStruct(b.arr.shape, b.arr.dtype)
    res = pl.pallas_call(
        body, name=name,
        in_specs=[ANY] * (n_src + n_held), out_specs=[ANY] * (n_held + n_fresh),
        out_shape=[shape_of(b) for b in held + fresh],
        input_output_aliases={n_src + i: i for i in range(n_held)},
        scratch_shapes=[pltpu.SemaphoreType.DMA((n_sem,)), pltpu.SemaphoreType.DMA((n_sem,))],
        compiler_params=pltpu.CompilerParams(has_side_effects=True),
    )(*sources, *[b.arr for b in held])
    for b, new in zip(held + fresh, res):
        b.arr = new


def _matmul(a, b, *, dims, mnk, tiles, name, out_dtype=jnp.float32, a_spec=None, b_spec=None,
            out_shape=None, o_spec=None, epilogue=None, extra=(), extra_specs=(), jobs=()):
    m, n, k = mnk
    tm, tn, tk = tiles
    assert m % tm == 0 and n % tn == 0 and k % tk == 0, (name, mnk, tiles)
    gm, gn, gk = m // tm, n // tn, k // tk
    if a_spec is None:
        a_spec = (pl.BlockSpec((tk, tm), lambda i, j, l: (l, i)) if dims is _TN
                  else pl.BlockSpec((tm, tk), lambda i, j, l: (i, l)))
    if b_spec is None:
        b_spec = (pl.BlockSpec((tn, tk), lambda i, j, l: (j, l)) if dims is _NT
                  else pl.BlockSpec((tk, tn), lambda i, j, l: (l, j)))
    if out_shape is None:
        out_shape = jax.ShapeDtypeStruct((m, n), out_dtype)
    if o_spec is None:
        o_spec = pl.BlockSpec((tm, tn), lambda i, j, l: (i, j))
    single = not isinstance(out_shape, (tuple, list))
    n_extra = len(extra)

    def finish(acc, extra_refs, out_refs):
        if epilogue is None:
            out_refs[0][...] = acc.astype(out_refs[0].dtype)
        else:
            epilogue(acc, extra_refs, out_refs)

    def body_whole_k(*refs):
        a_ref, b_ref = refs[0], refs[1]
        acc = lax.dot_general(a_ref[...], b_ref[...], dims, preferred_element_type=jnp.float32)
        finish(acc, refs[2:2 + n_extra], refs[2 + n_extra:])

    def body_split_k(*refs):
        a_ref, b_ref = refs[0], refs[1]
        extra_refs = refs[2:2 + n_extra]
        out_refs = refs[2 + n_extra:-1]
        acc_ref = refs[-1]
        step = pl.program_id(2)
        part = lax.dot_general(a_ref[...], b_ref[...], dims, preferred_element_type=jnp.float32)

        @pl.when(step == 0)
        def _():
            acc_ref[...] = part

        @pl.when(jnp.logical_and(step > 0, step < gk - 1))
        def _():
            acc_ref[...] += part

        @pl.when(step == gk - 1)
        def _():
            finish(acc_ref[...] + part, extra_refs, out_refs)

    res = _call(
        body_whole_k if gk == 1 else body_split_k, name=name, grid=(gm, gn, gk),
        in_specs=[a_spec, b_spec, *extra_specs],
        out_specs=[o_spec] if single else list(o_spec),
        out_shape=[out_shape] if single else list(out_shape),
        scratch_shapes=[] if gk == 1 else [pltpu.VMEM((tm, tn), jnp.float32)],
        semantics=("parallel", "parallel", "arbitrary"), args=(a, b, *extra), jobs=jobs)
    return res[0] if single else res


def _rstd(x):
    return lax.rsqrt(jnp.mean(x * x, axis=-1, keepdims=True) + EPS)


def _rms_bwd_rows(x, g, dy):
    r = _rstd(x)
    xn = x * r
    dyg = dy * g
    dx = r * (dyg - xn * jnp.mean(xn * dyg, axis=-1, keepdims=True))
    return dx, dy * xn


def _acc_rows(ref, rows, first):
    part = jnp.sum(rows, axis=0, keepdims=True)

    @pl.when(first)
    def _():
        ref[...] = part

    @pl.when(jnp.logical_not(first))
    def _():
        ref[...] += part


def _rms_fwd(x, g, *, width, col, tm, name):
    s = x.shape[0]

    def body(x_ref, g_ref, o_ref):
        v = x_ref[...]
        o_ref[...] = (v * _rstd(v) * g_ref[...]).astype(o_ref.dtype)

    return pl.pallas_call(
        body, name=name, grid=(s // tm,),
        in_specs=[pl.BlockSpec((tm, width), lambda i: (i, col)), pl.BlockSpec((1, width), lambda i: (0, 0))],
        out_specs=pl.BlockSpec((tm, width), lambda i: (i, 0)),
        out_shape=jax.ShapeDtypeStruct((s, width), jnp.bfloat16),
        compiler_params=_params(("parallel",)),
    )(x, g)


def _rms_bwd(x, g, dy, *, width, col, tm, name):
    s = x.shape[0]

    def body(x_ref, g_ref, dy_ref, dx_ref, dg_ref):
        dx, dgr = _rms_bwd_rows(x_ref[...], g_ref[...], dy_ref[...])
        dx_ref[...] = dx.astype(dx_ref.dtype)
        _acc_rows(dg_ref, dgr, pl.program_id(0) == 0)

    return pl.pallas_call(
        body, name=name, grid=(s // tm,),
        in_specs=[pl.BlockSpec((tm, width), lambda i: (i, col)), pl.BlockSpec((1, width), lambda i: (0, 0)),
                  pl.BlockSpec((tm, width), lambda i: (i, 0))],
        out_specs=[pl.BlockSpec((tm, width), lambda i: (i, 0)), pl.BlockSpec((1, width), lambda i: (0, 0))],
        out_shape=[jax.ShapeDtypeStruct((s, width), jnp.bfloat16), jax.ShapeDtypeStruct((1, width), jnp.float32)],
        compiler_params=_params(("arbitrary",)),
    )(x, g, dy)


def _row_specs(tm, d, n):
    return [pl.BlockSpec((tm, d), lambda i: (i, 0)) for _ in range(n)]


def _gain_specs(d, n):
    return [pl.BlockSpec((1, d), lambda i: (0, 0)) for _ in range(n)]


def _mix_residual_fwd(x, mix, g_post_mix, g_pre_mlp, *, tm, jobs=()):
    s, d = x.shape

    def body(x_ref, mix_ref, g1_ref, g2_ref, x2_ref, h2_ref):
        mixv = mix_ref[...]
        x2 = x_ref[...] + mixv * _rstd(mixv) * g1_ref[...]
        x2_ref[...] = x2
        h2_ref[...] = (x2 * _rstd(x2) * g2_ref[...]).astype(h2_ref.dtype)

    return _call(
        body, name="mix_residual_fwd", grid=(s // tm,),
        in_specs=_row_specs(tm, d, 2) + _gain_specs(d, 2),
        out_specs=_row_specs(tm, d, 2),
        out_shape=[jax.ShapeDtypeStruct((s, d), jnp.float32), jax.ShapeDtypeStruct((s, d), jnp.bfloat16)],
        semantics=("parallel",), args=(x, mix, g_post_mix, g_pre_mlp), jobs=jobs)


def _loss_head(x2, mlp, target, g_post_mlp, *, tm):
    s, d = x2.shape

    def body(x2_ref, m_ref, t_ref, g_ref, dx3_ref, dm_ref, dg_ref, loss_ref):
        first = pl.program_id(0) == 0
        mv = m_ref[...]
        g = g_ref[...]
        diff = x2_ref[...] + mv * _rstd(mv) * g - t_ref[...]
        dx3 = diff * (1.0 / d)
        dx3_ref[...] = dx3
        dm, dgr = _rms_bwd_rows(mv, g, dx3)
        dm_ref[...] = dm.astype(dm_ref.dtype)
        _acc_rows(dg_ref, dgr, first)
        part = 0.5 * jnp.sum(jnp.mean(diff * diff, axis=-1, keepdims=True), axis=0, keepdims=True)
        _acc_rows(loss_ref, jnp.broadcast_to(part, (1, 128)), first)

    return pl.pallas_call(
        body, name="loss_head", grid=(s // tm,),
        in_specs=_row_specs(tm, d, 3) + _gain_specs(d, 1),
        out_specs=_row_specs(tm, d, 2) + _gain_specs(d, 1) + [pl.BlockSpec((1, 128), lambda i: (0, 0))],
        out_shape=[jax.ShapeDtypeStruct((s, d), jnp.float32), jax.ShapeDtypeStruct((s, d), jnp.bfloat16),
                   jax.ShapeDtypeStruct((1, d), jnp.float32), jax.ShapeDtypeStruct((1, 128), jnp.float32)],
        compiler_params=_params(("arbitrary",)),
    )(x2, mlp, target, g_post_mlp)


def _mix_residual_bwd(dx3, dh2, x2, mix, g_pre_mlp, g_post_mix, *, tm):
    s, d = x2.shape

    def body(dx3_ref, dh2_ref, x2_ref, mix_ref, g2_ref, g1_ref, dx2_ref, dmix_ref, dg2_ref, dg1_ref):
        first = pl.program_id(0) == 0
        d_in, dgr2 = _rms_bwd_rows(x2_ref[...], g2_ref[...], dh2_ref[...])
        dx2 = dx3_ref[...] + d_in
        dx2_ref[...] = dx2
        dmix, dgr1 = _rms_bwd_rows(mix_ref[...], g1_ref[...], dx2)
        dmix_ref[...] = dmix.astype(dmix_ref.dtype)
        _acc_rows(dg2_ref, dgr2, first)
        _acc_rows(dg1_ref, dgr1, first)

    return pl.pallas_call(
        body, name="mix_residual_bwd", grid=(s // tm,),
        in_specs=_row_specs(tm, d, 4) + _gain_specs(d, 2),
        out_specs=_row_specs(tm, d, 2) + _gain_specs(d, 2),
        out_shape=[jax.ShapeDtypeStruct((s, d), jnp.float32), jax.ShapeDtypeStruct((s, d), jnp.bfloat16),
                   jax.ShapeDtypeStruct((1, d), jnp.float32), jax.ShapeDtypeStruct((1, d), jnp.float32)],
        compiler_params=_params(("arbitrary",)),
    )(dx3, dh2, x2, mix, g_pre_mlp, g_post_mix)


def _input_bwd(dx2, dh1, x, g_pre_mix, *, tm, jobs=()):
    s, d = x.shape

    def body(dx2_ref, dh1_ref, x_ref, g_ref, dx_ref, dg_ref):
        d_in, dgr = _rms_bwd_rows(x_ref[...], g_ref[...], dh1_ref[...])
        dx_ref[...] = dx2_ref[...] + d_in
        _acc_rows(dg_ref, dgr, pl.program_id(0) == 0)

    return _call(
        body, name="input_bwd", grid=(s // tm,),
        in_specs=_row_specs(tm, d, 3) + _gain_specs(d, 1),
        out_specs=_row_specs(tm, d, 1) + _gain_specs(d, 1),
        out_shape=[jax.ShapeDtypeStruct((s, d), jnp.float32), jax.ShapeDtypeStruct((1, d), jnp.float32)],
        semantics=("arbitrary",), args=(dx2, dh1, x, g_pre_mix), jobs=jobs)


def _shift_rows(z, by):
    s = z.shape[0]
    rows = lax.broadcasted_iota(jnp.int32, z.shape, 0)
    rolled = pltpu.roll(z, by % s, axis=0)
    keep = rows >= by if by > 0 else rows < s + by
    return jnp.where(keep, rolled, 0.0)


def _conv_fwd(proj, conv_w, conv_out_g, jobs=()):
    s = proj.shape[0]
    groups = CONV_WIDTH // 128

    def body(u_ref, gb_ref, gc_ref, w_ref, g_ref, y_ref):
        z = gc_ref[...] * u_ref[...]
        w = w_ref[...]
        conv = w[0:1, :] * _shift_rows(z, 2) + w[1:2, :] * _shift_rows(z, 1) + w[2:3, :] * z
        y = gb_ref[...] * conv
        y_ref[...] = (y * _rstd(y) * g_ref[...]).astype(y_ref.dtype)

    col = lambda base: pl.BlockSpec((s, 128), lambda j: (0, base + j))
    return _call(
        body, name="conv_fwd", grid=(groups,),
        in_specs=[col(0), col(groups), col(2 * groups), pl.BlockSpec((3, 128), lambda j: (0, j)),
                  pl.BlockSpec((1, 128), lambda j: (0, j))],
        out_specs=[pl.BlockSpec((s, 128), lambda j: (0, j))],
        out_shape=[jax.ShapeDtypeStruct((s, CONV_WIDTH), jnp.bfloat16)],
        semantics=("parallel",), args=(proj, proj, proj, conv_w, conv_out_g), jobs=jobs)[0]


def _conv_bwd(proj, conv_w, conv_out_g, dycat):
    s = proj.shape[0]
    groups = CONV_WIDTH // 128

    def body(u_ref, gb_ref, gc_ref, w_ref, g_ref, dy_ref, du_ref, dgb_ref, dgc_ref, dw_ref, dg_ref):
        u, gb, gc = u_ref[...], gb_ref[...], gc_ref[...]
        w = w_ref[...]
        z = gc * u
        z1, z2 = _shift_rows(z, 1), _shift_rows(z, 2)
        conv = w[0:1, :] * z2 + w[1:2, :] * z1 + w[2:3, :] * z
        dyr, dgr = _rms_bwd_rows(gb * conv, g_ref[...], dy_ref[...])
        dg_ref[...] = jnp.sum(dgr, axis=0, keepdims=True)
        dgb_ref[...] = (dyr * conv).astype(dgb_ref.dtype)
        dconv = dyr * gb
        dw_ref[0:1, :] = jnp.sum(dconv * z2, axis=0, keepdims=True)
        dw_ref[1:2, :] = jnp.sum(dconv * z1, axis=0, keepdims=True)
        dw_ref[2:3, :] = jnp.sum(dconv * z, axis=0, keepdims=True)
        dz = w[2:3, :] * dconv + w[1:2, :] * _shift_rows(dconv, -1) + w[0:1, :] * _shift_rows(dconv, -2)
        dgc_ref[...] = (dz * u).astype(dgc_ref.dtype)
        du_ref[...] = (dz * gc).astype(du_ref.dtype)

    col = lambda base: pl.BlockSpec((s, 128), lambda j: (0, base + j))
    act = jax.ShapeDtypeStruct((s, CONV_WIDTH), jnp.bfloat16)
    return pl.pallas_call(
        body, name="conv_bwd", grid=(groups,),
        in_specs=[col(0), col(groups), col(2 * groups), pl.BlockSpec((3, 128), lambda j: (0, j)),
                  pl.BlockSpec((1, 128), lambda j: (0, j)), col(0)],
        out_specs=[col(0), col(0), col(0), pl.BlockSpec((3, 128), lambda j: (0, j)),
                   pl.BlockSpec((1, 128), lambda j: (0, j))],
        out_shape=[act, act, act, jax.ShapeDtypeStruct((3, CONV_WIDTH), jnp.float32),
                   jax.ShapeDtypeStruct((1, CONV_WIDTH), jnp.float32)],
        compiler_params=_params(("parallel",)),
    )(proj, proj, proj, conv_w, conv_out_g, dycat)


def _rope_tables(s):
    pos = jnp.arange(s, dtype=jnp.float32)
    inv_freq = jnp.power(ROPE_THETA, -jnp.arange(0, QK_ROPE, 2, dtype=jnp.float32) / QK_ROPE)
    ang = pos[:, None] * inv_freq[None, :]
    cos, sin = jnp.cos(ang), jnp.sin(ang)
    zeros = jnp.zeros((s, 128 - QK_ROPE), jnp.float32)
    return (jnp.concatenate([cos, cos, zeros], axis=1), jnp.concatenate([-sin, sin, zeros], axis=1))


def _swap_halves(x):
    lane = lax.broadcasted_iota(jnp.int32, x.shape, 1)
    swapped = jnp.where(lane < QK_ROPE // 2, pltpu.roll(x, 128 - QK_ROPE // 2, axis=1),
                        pltpu.roll(x, QK_ROPE // 2, axis=1))
    return jnp.where(lane < QK_ROPE, swapped, 0.0)


def _rope(x, cos, sin):
    return x * cos + _swap_halves(x) * sin


def _rope_transposed(d, cos, sin):
    return d * cos + _swap_halves(d * sin)


def _qk_rope_fwd(q_pad, proj, cos, sin, *, tm):
    s = q_pad.shape[0]
    wq = N_HEADS * HEAD_PAD

    def body(q_ref, kr_ref, cos_ref, sin_ref, qo_ref, kro_ref):
        c, sn = cos_ref[...], sin_ref[...]
        for h in range(N_HEADS):
            lo = h * HEAD_PAD
            qo_ref[:, lo:lo + 128] = q_ref[:, lo:lo + 128].astype(qo_ref.dtype)
            qo_ref[:, lo + 128:lo + 256] = _rope(q_ref[:, lo + 128:lo + 256], c, sn).astype(qo_ref.dtype)
        kro_ref[...] = _rope(kr_ref[...], c, sn).astype(kro_ref.dtype)

    return pl.pallas_call(
        body, name="qk_rope_fwd", grid=(s // tm,),
        in_specs=[pl.BlockSpec((tm, wq), lambda i: (i, 0)), pl.BlockSpec((tm, 128), lambda i: (i, COL_KR // 128)),
                  pl.BlockSpec((tm, 128), lambda i: (i, 0)), pl.BlockSpec((tm, 128), lambda i: (i, 0))],
        out_specs=[pl.BlockSpec((tm, wq), lambda i: (i, 0)), pl.BlockSpec((tm, 128), lambda i: (i, 0))],
        out_shape=[jax.ShapeDtypeStruct((s, wq), jnp.bfloat16), jax.ShapeDtypeStruct((s, 128), jnp.bfloat16)],
        compiler_params=_params(("parallel",)),
    )(q_pad, proj, cos, sin)


def _qk_rope_bwd(dq_pad, dk_pad, dv, cos, sin, *, tm):
    s = dq_pad.shape[0]
    wq = N_HEADS * HEAD_PAD

    def body(dq_ref, dk_ref, dv_ref, cos_ref, sin_ref, dqo_ref, dkv_ref, dkr_ref):
        c, sn = cos_ref[...], sin_ref[...]
        dkr = jnp.zeros((tm, 128), jnp.float32)
        for h in range(N_HEADS):
            lo = h * HEAD_PAD
            dqo_ref[:, lo:lo + 128] = dq_ref[:, lo:lo + 128].astype(dqo_ref.dtype)
            dqo_ref[:, lo + 128:lo + 256] = _rope_transposed(dq_ref[:, lo + 128:lo + 256], c, sn).astype(dqo_ref.dtype)
            dkv_ref[:, lo:lo + 128] = dk_ref[:, lo:lo + 128].astype(dkv_ref.dtype)
            dkv_ref[:, lo + 128:lo + 256] = dv_ref[:, h * V_DIM:(h + 1) * V_DIM].astype(dkv_ref.dtype)
            dkr = dkr + dk_ref[:, lo + 128:lo + 256]
        dkr_ref[...] = _rope_transposed(dkr, c, sn).astype(dkr_ref.dtype)

    return pl.pallas_call(
        body, name="qk_rope_bwd", grid=(s // tm,),
        in_specs=[pl.BlockSpec((tm, wq), lambda i: (i, 0)), pl.BlockSpec((tm, wq), lambda i: (i, 0)),
                  pl.BlockSpec((tm, N_HEADS * V_DIM), lambda i: (i, 0)),
                  pl.BlockSpec((tm, 128), lambda i: (i, 0)), pl.BlockSpec((tm, 128), lambda i: (i, 0))],
        out_specs=[pl.BlockSpec((tm, wq), lambda i: (i, 0)), pl.BlockSpec((tm, wq), lambda i: (i, 0)),
                   pl.BlockSpec((tm, 128), lambda i: (i, 0))],
        out_shape=[jax.ShapeDtypeStruct((s, wq), jnp.bfloat16), jax.ShapeDtypeStruct((s, wq), jnp.bfloat16),
                   jax.ShapeDtypeStruct((s, 128), jnp.bfloat16)],
        compiler_params=_params(("parallel",)),
    )(dq_pad, dk_pad, dv, cos, sin)


def _visible(q0, k0, t):
    qpos = q0 + lax.broadcasted_iota(jnp.int32, (t, t), 0)
    kpos = k0 + lax.broadcasted_iota(jnp.int32, (t, t), 1)
    return lax.shift_right_logical(kpos, CHUNK_SHIFT) <= lax.shift_right_logical(qpos, CHUNK_SHIFT)


def _attn_fwd(q, kv, kr, attn_out_g, *, t, jobs=()):
    s = q.shape[0]
    nq = s // t

    def body(q_ref, kn_ref, v_ref, kr_ref, g_ref, o_ref, lse_ref, y_ref, kcat_ref):
        i = pl.program_id(1)

        @pl.when(i == 0)
        def _():
            kcat_ref[:, 0:128] = kn_ref[...]
            kcat_ref[:, 128:256] = kr_ref[...]

        qv = q_ref[...]

        def step(j, carry, diagonal):
            m, l, acc = carry
            k = kcat_ref[pl.ds(pl.multiple_of(j * t, t), t), :]
            v = v_ref[pl.ds(pl.multiple_of(j * t, t), t), :]
            sc = lax.dot_general(qv, k, _NT, preferred_element_type=jnp.float32) * ATTN_SCALE
            if diagonal:
                sc = jnp.where(_visible(0, 0, t), sc, NEG_INF)
            m_new = jnp.maximum(m, jnp.max(sc, axis=-1, keepdims=True))
            p = jnp.exp(sc - m_new)
            alpha = jnp.exp(m - m_new)
            l = alpha * l + jnp.sum(p, axis=-1, keepdims=True)
            acc = alpha * acc + lax.dot_general(p.astype(jnp.bfloat16), v, _NN, preferred_element_type=jnp.float32)
            return m_new, l, acc

        init = (jnp.full((t, 1), NEG_INF, jnp.float32), jnp.zeros((t, 1), jnp.float32),
                jnp.zeros((t, V_DIM), jnp.float32))
        before = lax.fori_loop(0, i, functools.partial(step, diagonal=False), init)
        m, l, acc = step(i, before, True)
        o = acc / l
        o_ref[...] = o
        lse_ref[...] = jnp.broadcast_to(m + jnp.log(l), (t, 128))
        y_ref[...] = (o * _rstd(o) * g_ref[...]).astype(y_ref.dtype)

    head_rows = lambda w, f: pl.BlockSpec((s, w), lambda h, i: (0, f(h)))
    blk = pl.BlockSpec((t, 128), lambda h, i: (i, h))
    full = jax.ShapeDtypeStruct((s, N_HEADS * V_DIM), jnp.float32)
    return _call(
        body, name="attn_fwd", grid=(N_HEADS, nq),
        in_specs=[pl.BlockSpec((t, HEAD_PAD), lambda h, i: (i, h)), head_rows(128, lambda h: 2 * h),
                  head_rows(128, lambda h: 2 * h + 1), head_rows(128, lambda h: 0),
                  pl.BlockSpec((1, 128), lambda h, i: (0, h))],
        out_specs=[blk, blk, blk],
        out_shape=[full, full, jax.ShapeDtypeStruct((s, N_HEADS * V_DIM), jnp.bfloat16)],
        scratch_shapes=[pltpu.VMEM((s, HEAD_PAD), jnp.bfloat16)],
        semantics=("arbitrary", "arbitrary"), args=(q, kv, kv, kr, attn_out_g), jobs=jobs)


def _attn_norm_bwd(o, attn_out_g, dycat):
    s = o.shape[0]

    def body(o_ref, g_ref, dy_ref, do_ref, delta_ref, dg_ref):
        ov = o_ref[...]
        do, dgr = _rms_bwd_rows(ov, g_ref[...], dy_ref[...])
        do_ref[...] = do.astype(do_ref.dtype)
        delta_ref[...] = jnp.broadcast_to(jnp.sum(do * ov, axis=-1, keepdims=True), (s, 128))
        dg_ref[...] = jnp.sum(dgr, axis=0, keepdims=True)

    col = lambda base: pl.BlockSpec((s, 128), lambda h: (0, base + h))
    return pl.pallas_call(
        body, name="attn_norm_bwd", grid=(N_HEADS,),
        in_specs=[col(0), pl.BlockSpec((1, 128), lambda h: (0, h)), col(CONV_WIDTH // 128)],
        out_specs=[col(0), col(0), pl.BlockSpec((1, 128), lambda h: (0, h))],
        out_shape=[jax.ShapeDtypeStruct((s, N_HEADS * V_DIM), jnp.bfloat16),
                   jax.ShapeDtypeStruct((s, N_HEADS * V_DIM), jnp.float32),
                   jax.ShapeDtypeStruct((1, N_HEADS * V_DIM), jnp.float32)],
        compiler_params=_params(("parallel",)),
    )(o, attn_out_g, dycat)


def _attn_bwd(q, kv, kr, do, lse, delta, *, t, jobs=()):
    s = q.shape[0]
    nq = s // t

    def body(q_ref, kn_ref, v_ref, kr_ref, do_ref, lse_ref, delta_ref, dq_ref, dk_ref, dv_ref, kcat_ref):
        kcat_ref[:, 0:128] = kn_ref[...]
        kcat_ref[:, 128:256] = kr_ref[...]
        dq_ref[...] = jnp.zeros_like(dq_ref)
        dk_ref[...] = jnp.zeros_like(dk_ref)
        dv_ref[...] = jnp.zeros_like(dv_ref)

        def kv_step(j, _):
            krows = pl.ds(pl.multiple_of(j * t, t), t)
            k = kcat_ref[krows, :]
            v = v_ref[krows, :]

            def q_step(i, _, diagonal):
                qrows = pl.ds(pl.multiple_of(i * t, t), t)
                qv = q_ref[qrows, :]
                dov = do_ref[qrows, :]
                sc = lax.dot_general(qv, k, _NT, preferred_element_type=jnp.float32) * ATTN_SCALE
                if diagonal:
                    sc = jnp.where(_visible(0, 0, t), sc, NEG_INF)
                p = jnp.exp(sc - lse_ref[qrows, :][:, 0:1])
                dp = lax.dot_general(dov, v, _NT, preferred_element_type=jnp.float32)
                ds = (p * (dp - delta_ref[qrows, :][:, 0:1]) * ATTN_SCALE).astype(jnp.bfloat16)
                dv_ref[krows, :] += lax.dot_general(p.astype(jnp.bfloat16), dov, _TN,
                                                    preferred_element_type=jnp.float32)
                dk_ref[krows, :] += lax.dot_general(ds, qv, _TN, preferred_element_type=jnp.float32)
                dq_ref[qrows, :] += lax.dot_general(ds, k, _NN, preferred_element_type=jnp.float32)
                return 0

            q_step(j, 0, True)
            lax.fori_loop(j + 1, nq, functools.partial(q_step, diagonal=False), 0)
            return 0

        lax.fori_loop(0, nq, kv_step, 0)

    col = lambda w, f: pl.BlockSpec((s, w), lambda h: (0, f(h)))
    return _call(
        body, name="attn_bwd", grid=(N_HEADS,),
        in_specs=[col(HEAD_PAD, lambda h: h), col(128, lambda h: 2 * h), col(128, lambda h: 2 * h + 1),
                  col(128, lambda h: 0), col(128, lambda h: h), col(128, lambda h: h), col(128, lambda h: h)],
        out_specs=[col(HEAD_PAD, lambda h: h), col(HEAD_PAD, lambda h: h), col(128, lambda h: h)],
        out_shape=[jax.ShapeDtypeStruct((s, N_HEADS * HEAD_PAD), jnp.float32),
                   jax.ShapeDtypeStruct((s, N_HEADS * HEAD_PAD), jnp.float32),
                   jax.ShapeDtypeStruct((s, N_HEADS * V_DIM), jnp.float32)],
        scratch_shapes=[pltpu.VMEM((s, HEAD_PAD), jnp.bfloat16)],
        semantics=("parallel",), args=(q, kv, kv, kr, do, lse, delta), jobs=jobs)


def _row_tile(rows):
    for cand in (256, 128, 64, 32, 16, 8):
        if rows % cand == 0:
            return cand
    return rows


def _cast_into_slot(w, pos, *, name):
    r, c = w.shape
    tr = _row_tile(r)

    def body(pos_ref, w_ref, o_ref):
        o_ref[...] = w_ref[...].astype(o_ref.dtype)

    return pl.pallas_call(
        body, name=name, out_shape=jax.ShapeDtypeStruct((4, r, c), jnp.bfloat16),
        grid_spec=pltpu.PrefetchScalarGridSpec(
            num_scalar_prefetch=1, grid=(r // tr,),
            in_specs=[pl.BlockSpec((tr, c), lambda i, p: (i, 0))],
            out_specs=pl.BlockSpec((None, tr, c), lambda i, p: (p[1], i, 0))),
        compiler_params=_params(("parallel",)),
    )(pos, w)


def _pair_add(g, theirs, pos, *, name):
    n, h, c = theirs.shape
    tr = _row_tile(h)
    nb = h // tr

    def body(pos_ref, a_ref, b_ref, o_ref):
        o_ref[...] = (a_ref[...].astype(jnp.float32) + b_ref[...].astype(jnp.float32)).astype(o_ref.dtype)

    spec = pl.BlockSpec((None, tr, c), lambda j, i, p: (j, i, 0))
    return pl.pallas_call(
        body, name=name, out_shape=jax.ShapeDtypeStruct(theirs.shape, jnp.bfloat16),
        grid_spec=pltpu.PrefetchScalarGridSpec(
            num_scalar_prefetch=1, grid=(n, nb),
            in_specs=[pl.BlockSpec((None, tr, c), lambda j, i, p: (j, i + p[0] * nb, 0)), spec],
            out_specs=spec),
        compiler_params=_params(("parallel", "parallel")),
    )(pos, g, theirs)


def _chip_sum(by_source, pair_sum, pos, *, name):
    n, h, c = by_source.shape
    tr = _row_tile(h)
    nb = h // tr

    def body(pos_ref, p0, p1, p2, p3, own_ref, o_ref):
        own = own_ref[...].astype(jnp.float32)
        term = lambda k, ref: jnp.where(pos_ref[1] == k, own, ref[...].astype(jnp.float32))
        o_ref[...] = ((term(0, p0) + term(1, p1)) + term(2, p2)) + term(3, p3)

    def source(k):
        return pl.BlockSpec((None, tr, c), lambda i, p: (jnp.where(p[1] == k, (k + 1) % 4, k), i, 0))

    return pl.pallas_call(
        body, name=name, out_shape=jax.ShapeDtypeStruct((2 * h, c), jnp.float32),
        grid_spec=pltpu.PrefetchScalarGridSpec(
            num_scalar_prefetch=1, grid=(nb,),
            in_specs=[source(0), source(1), source(2), source(3),
                      pl.BlockSpec((None, tr, c), lambda i, p: (p[1], i, 0))],
            out_specs=pl.BlockSpec((tr, c), lambda i, p: (i + p[0] * nb, 0))),
        compiler_params=_params(("parallel",)),
    )(pos, by_source, by_source, by_source, by_source, pair_sum)


def _adamw(w, g, m, v, *, name, jobs=()):
    r, c = w.shape
    tr = _row_tile(r)

    def body(w_ref, g_ref, m_ref, v_ref, d_ref, mo_ref, vo_ref, go_ref):
        gv = g_ref[...]
        go_ref[...] = gv
        mn = ADAM_B1 * m_ref[...] + (1.0 - ADAM_B1) * gv
        vn = ADAM_B2 * v_ref[...] + (1.0 - ADAM_B2) * (gv * gv)
        m_hat = mn / (1.0 - ADAM_B1 ** ADAM_STEP)
        v_hat = vn / (1.0 - ADAM_B2 ** ADAM_STEP)
        d_ref[...] = -ADAM_LR * (m_hat / (jnp.sqrt(v_hat) + ADAM_EPS) + ADAM_WD * w_ref[...])
        mo_ref[...] = mn
        vo_ref[...] = vn

    spec = pl.BlockSpec((tr, c), lambda i: (i, 0))
    out = jax.ShapeDtypeStruct((r, c), jnp.float32)
    return _call(body, name=name, grid=(r // tr,), in_specs=[spec] * 4, out_specs=[spec] * 4, out_shape=[out] * 4,
                 semantics=("parallel",), args=(w, g, m, v), jobs=jobs)


def _all_reduce_small(block):
    r, c = block.shape

    def body(src_ref, out_ref, stage_ref, send_sems, recv_sems):
        x, y, cc = _position()
        me = 4 * x + 2 * y + cc
        stage_ref[me] = src_ref[...]
        flip = lambda v, on: 1 - v if on else v
        peers = [(flip(x, dx), flip(y, dy), flip(cc, dc)) for dx in (0, 1) for dy in (0, 1) for dc in (0, 1)][1:]
        copies = [pltpu.make_async_remote_copy(
            src_ref=stage_ref.at[me], dst_ref=stage_ref.at[me],
            send_sem=send_sems.at[k], recv_sem=recv_sems.at[k], device_id=peer, device_id_type=MESH)
            for k, peer in enumerate(peers)]
        for cp in copies:
            cp.start()
        for k, (px, py, pc) in enumerate(peers):
            them = 4 * px + 2 * py + pc
            pltpu.make_async_remote_copy(
                src_ref=stage_ref.at[them], dst_ref=stage_ref.at[them],
                send_sem=send_sems.at[k], recv_sem=recv_sems.at[k], device_id=(px, py, pc),
                device_id_type=MESH).wait_recv()
        for cp in copies:
            cp.wait_send()
        total = stage_ref[0]
        for d in range(1, 8):
            total = total + stage_ref[d]
        out_ref[...] = total

    return pl.pallas_call(
        body, name="all_reduce_small",
        in_specs=[pl.BlockSpec(memory_space=pltpu.VMEM)], out_specs=pl.BlockSpec(memory_space=pltpu.VMEM),
        out_shape=jax.ShapeDtypeStruct((r, c), jnp.float32),
        scratch_shapes=[pltpu.VMEM((8, r, c), jnp.float32), pltpu.SemaphoreType.DMA((7,)),
                        pltpu.SemaphoreType.DMA((7,))],
        compiler_params=pltpu.CompilerParams(has_side_effects=True),
    )(block)


def _cols_from_shards(g):
    n, r, c = g.shape
    return jnp.transpose(g, (1, 0, 2)).reshape(r, n * c)


def _cols_to_shards(w, n=4):
    r, c = w.shape
    return jnp.transpose(w.reshape(r, n, c // n), (1, 0, 2))


def _pad_w_in(full):
    d = full.shape[0]
    zeros = jnp.zeros((d, COL_CKV - COL_KR - QK_ROPE), full.dtype)
    return jnp.concatenate([full[:, :COL_KR], full[:, IN_WIDTH - QK_ROPE:], zeros,
                            full[:, COL_KR:COL_KR + KV_RANK]], axis=1)


def _unpad_w_in(padded):
    return jnp.concatenate([padded[:, :COL_KR], padded[:, COL_CKV:COL_CKV + KV_RANK],
                            padded[:, COL_KR:COL_KR + QK_ROPE]], axis=1)


def _pad_w_uq(full):
    r = full.shape[0]
    per_head = full.reshape(r, N_HEADS, QK_NOPE + QK_ROPE)
    return jnp.pad(per_head, ((0, 0), (0, 0), (0, HEAD_PAD - QK_NOPE - QK_ROPE))).reshape(r, N_HEADS * HEAD_PAD)


def _unpad_w_uq(padded):
    r = padded.shape[0]
    return padded.reshape(r, N_HEADS, HEAD_PAD)[:, :, :QK_NOPE + QK_ROPE].reshape(r, N_HEADS * (QK_NOPE + QK_ROPE))


SMALL_ROWS = 16


def _pack_small(d, pre_mix, post_mix, pre_mlp, post_mlp, conv_out, attn_out, q_norm, kv_norm, conv_w):
    row = lambda *parts: jnp.pad(jnp.concatenate(parts, axis=1), ((0, 0), (0, d - sum(p.shape[1] for p in parts))))
    rows = [row(pre_mix), row(post_mix), row(pre_mlp), row(post_mlp), row(conv_out, attn_out), row(q_norm, kv_norm),
            row(conv_w[0:1]), row(conv_w[1:2]), row(conv_w[2:3])]
    return jnp.pad(jnp.concatenate(rows, axis=0), ((0, SMALL_ROWS - len(rows)), (0, 0)))


def _unpack_small(p, chip):
    cw = CONV_WIDTH // 4
    conv_w = lax.dynamic_slice(p[6:9, :CONV_WIDTH], (0, chip * cw), (3, cw))
    return dict(pre_mix_g=p[0:1], post_mix_g=p[1:2], pre_mlp_g=p[2:3], post_mlp_g=p[3:4],
                conv_out_g=p[4:5, :CONV_WIDTH], attn_out_g=p[4:5, CONV_WIDTH:2 * CONV_WIDTH],
                q_norm_g=p[5:6, :Q_RANK], kv_norm_g=p[5:6, Q_RANK:Q_RANK + KV_RANK], conv_w=conv_w[None])


def kernel(x, pre_mix_g, w_in, conv_w, q_norm_g, w_uq, kv_norm_g, w_ukv, conv_out_g, attn_out_g, w_o, post_mix_g, pre_mlp_g, w_up, w_down, post_mlp_g, loss_target, m_pre_mix_g, m_w_in, m_conv_w, m_q_norm_g, m_w_uq, m_kv_norm_g, m_w_ukv, m_conv_out_g, m_attn_out_g, m_w_o, m_post_mix_g, m_pre_mlp_g, m_w_up, m_w_down, m_post_mlp_g, v_pre_mix_g, v_w_in, v_conv_w, v_q_norm_g, v_w_uq, v_kv_norm_g, v_w_ukv, v_conv_out_g, v_attn_out_g, v_w_o, v_post_mix_g, v_pre_mlp_g, v_w_up, v_w_down, v_post_mlp_g):
    bf16 = jnp.bfloat16
    s, d = x.shape[1], x.shape[2]
    d_ff = 4 * d
    chip = 2 * lax.axis_index("x") + lax.axis_index("y")
    xs = x.reshape(s, d)
    target = loss_target.reshape(s, d)
    tm = min(256, s)
    t_attn = min(512, s)
    mt = min(1024, s)
    kt = min(2048, s)

    big = dict(w_in=w_in[0], w_uq=w_uq[0], w_ukv=w_ukv[0], w_o=w_o[0], w_up=w_up[0], w_down=w_down[0])
    names = list(big)
    pos = jnp.stack([lax.axis_index("c"), chip]).astype(jnp.int32)
    wb = {k: _Buf(_cast_into_slot(big[k], pos, name="cast_" + k)) for k in names}
    half = {k: big[k].shape[0] // 2 for k in names}
    ici = lambda k, lo=0, n=None: _GatherIci(wb[k], lo, half[k] - lo if n is None else n)
    d2d = lambda k, lo=0, n=None: _GatherD2d(wb[k], lo, half[k] - lo if n is None else n)
    down_cuts = (0, 3 * half["w_down"] // 16, 5 * half["w_down"] // 16, half["w_down"])
    _comm("gather_w_in", [[ici("w_in")], [d2d("w_in")]])
    win = _pad_w_in(_cols_from_shards(wb["w_in"].arr))
    ff4 = d_ff // 4

    spread = lambda a: lax.dynamic_update_slice(jnp.zeros((3, CONV_WIDTH), jnp.float32), a[0],
                                                (0, chip * (CONV_WIDTH // 4)))
    conv_w_mine = jnp.where(lax.axis_index("c") == 0, spread(conv_w), 0.0)
    conv_w_full = _all_reduce_small(jnp.pad(conv_w_mine, ((0, 5), (0, 0))))[0:3]

    h1 = _rms_fwd(xs, pre_mix_g, width=d, col=0, tm=tm, name="rms_pre_mix")
    proj = _matmul(h1, win, dims=_NN, mnk=(s, IN_PAD, d), tiles=(mt, IN_PAD // 3, d), name="mm_proj",
                   jobs=[ici("w_uq"), ici("w_ukv"), ici("w_o")])
    y_conv = _conv_fwd(proj, conv_w_full, conv_out_g, jobs=[d2d("w_uq"), d2d("w_ukv"), d2d("w_o")])
    wuq = _pad_w_uq(_cols_from_shards(wb["w_uq"].arr))
    wukv = _cols_from_shards(wb["w_ukv"].arr)
    wo = wb["w_o"].arr.reshape(-1, d)
    cqn = _rms_fwd(proj, q_norm_g, width=Q_RANK, col=COL_CQ // Q_RANK, tm=tm, name="rms_q")
    ckvn = _rms_fwd(proj, kv_norm_g, width=KV_RANK, col=COL_CKV // KV_RANK, tm=tm, name="rms_kv")
    q_pad = _matmul(cqn, wuq, dims=_NN, mnk=(s, N_HEADS * HEAD_PAD, Q_RANK), tiles=(mt, 1024, Q_RANK), name="mm_q")
    kv = _matmul(ckvn, wukv, dims=_NN, mnk=(s, N_HEADS * HEAD_PAD, KV_RANK), tiles=(mt, 1024, KV_RANK),
                 name="mm_kv", out_dtype=bf16)
    cos, sin = _rope_tables(s)
    q_rot, kr_rot = _qk_rope_fwd(q_pad, proj, cos, sin, tm=tm)
    o, lse, y_attn = _attn_fwd(q_rot, kv, kr_rot, attn_out_g, t=t_attn, jobs=[ici("w_up")])
    ycat = jnp.concatenate([y_conv, y_attn], axis=1)
    mix = _matmul(ycat, wo, dims=_NN, mnk=(s, d, 2 * CONV_WIDTH), tiles=(mt, 1024, 2 * CONV_WIDTH), name="mm_out",
                  jobs=[d2d("w_up"), ici("w_down", down_cuts[0], down_cuts[1] - down_cuts[0])])
    x2, h2 = _mix_residual_fwd(xs, mix, post_mix_g, pre_mlp_g, tm=tm,
                               jobs=[d2d("w_down", down_cuts[0], down_cuts[1] - down_cuts[0]),
                                     ici("w_down", down_cuts[1], down_cuts[2] - down_cuts[1])])
    wup = wb["w_up"].arr

    def up_epilogue(acc, extra_refs, out_refs):
        r = jnp.maximum(acc, 0.0)
        out_refs[0][...] = acc.astype(bf16)
        out_refs[1][...] = (r * r).astype(bf16)

    n_ff = ff4 // 1024
    act = jax.ShapeDtypeStruct((s, d_ff), bf16)
    up, act_sq = _matmul(
        h2, wup, dims=_NN, mnk=(s, d_ff, d), tiles=(mt, 1024, d), name="mm_up",
        b_spec=pl.BlockSpec((None, d, 1024), lambda i, j, l: (j // n_ff, l, j % n_ff)),
        out_shape=(act, act), o_spec=(pl.BlockSpec((mt, 1024), lambda i, j, l: (i, j)),) * 2, epilogue=up_epilogue,
        jobs=[d2d("w_down", down_cuts[1], down_cuts[2] - down_cuts[1]), ici("w_down", down_cuts[2])])
    _comm("gather_w_down_tail", [[d2d("w_down", down_cuts[2])]])
    wdown = wb["w_down"].arr.reshape(d_ff, d)
    mlp = _matmul(act_sq, wdown, dims=_NN, mnk=(s, d, d_ff), tiles=(mt, 1024, 2048), name="mm_down")
    dx3, dmlp, dg_post_mlp, loss_part = _loss_head(x2, mlp, target, post_mlp_g, tm=tm)

    def dup_epilogue(acc, extra_refs, out_refs):
        out_refs[0][...] = (acc * (2.0 * jnp.maximum(extra_refs[0][...].astype(jnp.float32), 0.0))).astype(bf16)

    dup = _matmul(dmlp, wdown, dims=_NT, mnk=(s, d_ff, d), tiles=(mt, 1024, d), name="mm_dact",
                  out_dtype=bf16, epilogue=dup_epilogue, extra=(up,),
                  extra_specs=(pl.BlockSpec((mt, 1024), lambda i, j, l: (i, j)),))
    pair_sums, by_source = {}, {}

    def pair_reduce(grads):
        theirs = {k: _Buf(jax.ShapeDtypeStruct((4, g.shape[1] // 2, g.shape[2]), bf16)) for k, g in grads.items()}
        _comm("pair_exchange_" + "_".join(grads), [[_PairExchange(g, theirs[k]) for k, g in grads.items()]])
        for k, g in grads.items():
            pair_sums[k] = _pair_add(g, theirs[k].arr, pos, name="pair_add_" + k)
            by_source[k] = _Buf(jax.ShapeDtypeStruct(pair_sums[k].shape, bf16))

    scatter = lambda k, lo=0, n=None: _ScatterIci(pair_sums[k], by_source[k], lo, half[k] - lo if n is None else n)

    g_wdown = _matmul(act_sq, dmlp, dims=_TN, mnk=(d_ff, d, s), tiles=(1024, 1024, kt), name="mm_gw_down",
                      out_dtype=bf16).reshape(4, ff4, d)
    pair_reduce(dict(w_down=g_wdown))
    h_ff = half["w_down"]
    dh2 = _matmul(dup, wup, dims=_NT, mnk=(s, d, d_ff), tiles=(mt, 1024, ff4), name="mm_dh2",
                  b_spec=pl.BlockSpec((None, 1024, ff4), lambda i, j, l: (l, j, 0)),
                  jobs=[scatter("w_down", 0, h_ff // 2)])
    g_wup = _matmul(h2, dup, dims=_TN, mnk=(d, d_ff, s), tiles=(1024, 1024, kt), name="mm_gw_up",
                    out_shape=jax.ShapeDtypeStruct((4, d, ff4), bf16),
                    o_spec=pl.BlockSpec((None, 1024, 1024), lambda i, j, l: (j // n_ff, i, j % n_ff)),
                    jobs=[scatter("w_down", h_ff // 2)])
    pair_reduce(dict(w_up=g_wup))
    dx2, dmix, dg_pre_mlp, dg_post_mix = _mix_residual_bwd(dx3, dh2, x2, mix, pre_mlp_g, post_mix_g, tm=tm)

    up_cuts = (0, 3 * half["w_up"] // 16, 11 * half["w_up"] // 32, half["w_up"])
    dycat = _matmul(dmix, wo, dims=_NT, mnk=(s, 2 * CONV_WIDTH, d), tiles=(mt, 1024, d), name="mm_dycat",
                    jobs=[scatter("w_up", up_cuts[0], up_cuts[1] - up_cuts[0])])
    g_wo = _matmul(ycat, dmix, dims=_TN, mnk=(2 * CONV_WIDTH, d, s), tiles=(1024, 1024, kt),
                   name="mm_gw_out", out_dtype=bf16, jobs=[scatter("w_up", up_cuts[1], up_cuts[2] - up_cuts[1])]
                   ).reshape(4, CONV_WIDTH // 2, d)
    pair_reduce(dict(w_o=g_wo))
    du, dgb, dgc, dg_conv_w, dg_conv_out = _conv_bwd(proj, conv_w_full, conv_out_g, dycat)
    do, delta, dg_attn_out = _attn_norm_bwd(o, attn_out_g, dycat)
    dq_pad, dk_pad, dv = _attn_bwd(q_rot, kv, kr_rot, do, lse, delta, t=t_attn,
                                   jobs=[scatter("w_up", up_cuts[2]), scatter("w_o")])
    dq_raw, dkv, dkr = _qk_rope_bwd(dq_pad, dk_pad, dv, cos, sin, tm=tm)
    wq_cols = N_HEADS * HEAD_PAD
    g_wuq = _matmul(cqn, dq_raw, dims=_TN, mnk=(Q_RANK, wq_cols, s), tiles=(Q_RANK, 1024, kt),
                    name="mm_gw_uq", out_dtype=bf16)
    dcqn = _matmul(dq_raw, wuq, dims=_NT, mnk=(s, Q_RANK, wq_cols), tiles=(mt, Q_RANK, wq_cols), name="mm_dcq")
    g_wukv = _matmul(ckvn, dkv, dims=_TN, mnk=(KV_RANK, wq_cols, s), tiles=(KV_RANK, 1024, kt),
                     name="mm_gw_ukv", out_dtype=bf16)
    dckvn = _matmul(dkv, wukv, dims=_NT, mnk=(s, KV_RANK, wq_cols), tiles=(mt, KV_RANK, wq_cols), name="mm_dckv")
    dcq, dg_q_norm = _rms_bwd(proj, q_norm_g, dcqn, width=Q_RANK, col=COL_CQ // Q_RANK, tm=tm, name="rms_q_bwd")
    dckv, dg_kv_norm = _rms_bwd(proj, kv_norm_g, dckvn, width=KV_RANK, col=COL_CKV // KV_RANK, tm=tm,
                                name="rms_kv_bwd")
    pair_reduce(dict(w_uq=_cols_to_shards(_unpad_w_uq(g_wuq)), w_ukv=_cols_to_shards(g_wukv)))
    dproj = jnp.concatenate([du, dgb, dgc, dcq, dkr, jnp.zeros((s, COL_CKV - COL_KR - 128), bf16), dckv], axis=1)
    g_win = _matmul(h1, dproj, dims=_TN, mnk=(d, IN_PAD, s), tiles=(1024, IN_PAD // 3, kt), name="mm_gw_in",
                    out_dtype=bf16, jobs=[scatter("w_uq"), scatter("w_ukv")])
    pair_reduce(dict(w_in=_cols_to_shards(_unpad_w_in(g_win))))
    in_cut = 11 * half["w_in"] // 16
    dh1 = _matmul(dproj, win, dims=_NT, mnk=(s, d, IN_PAD), tiles=(mt, 1024, IN_PAD // 2), name="mm_dh1",
                  jobs=[scatter("w_in", 0, in_cut)])
    grad_x, dg_pre_mix = _input_bwd(dx2, dh1, xs, pre_mix_g, tm=tm, jobs=[scatter("w_in", in_cut)])

    moments = dict(w_in=(m_w_in, v_w_in), w_uq=(m_w_uq, v_w_uq), w_ukv=(m_w_ukv, v_w_ukv), w_o=(m_w_o, v_w_o),
                   w_up=(m_w_up, v_w_up), w_down=(m_w_down, v_w_down))
    gw, dw, nm, nv = {}, {}, {}, {}

    def finish(keys, carried):
        whole = {k: _Buf(_chip_sum(by_source[k].arr, pair_sums[k], pos, name="chip_sum_" + k)) for k in keys}
        _comm("pair_share_" + "_".join(keys), [[_PairShare(whole[k]) for k in keys]])
        for k in keys:
            delta_k, nm_k, nv_k, g = _adamw(big[k], whole[k].arr, moments[k][0][0], moments[k][1][0],
                                            name="adamw_" + k, jobs=carried.get(k, ()))
            gw[k], dw[k], nm[k], nv[k] = g[None], delta_k[None], nm_k[None], nv_k[None]

    finish(["w_down", "w_up", "w_o", "w_uq", "w_ukv"], {})
    finish(["w_in"], {})

    small_g = _all_reduce_small(_pack_small(d, dg_pre_mix, dg_post_mix, dg_pre_mlp, dg_post_mlp, dg_conv_out,
                                            dg_attn_out, dg_q_norm, dg_kv_norm, dg_conv_w))
    pack_w = lambda cw, pre_mix, post_mix, pre_mlp, post_mlp, conv_out, attn_out, q_norm, kv_norm: _pack_small(
        d, pre_mix, post_mix, pre_mlp, post_mlp, conv_out, attn_out, q_norm, kv_norm, cw)
    small_w = pack_w(conv_w_full, pre_mix_g, post_mix_g, pre_mlp_g, post_mlp_g, conv_out_g, attn_out_g, q_norm_g,
                     kv_norm_g)
    small_m = pack_w(spread(m_conv_w), m_pre_mix_g, m_post_mix_g, m_pre_mlp_g, m_post_mlp_g, m_conv_out_g,
                     m_attn_out_g, m_q_norm_g, m_kv_norm_g)
    small_v = pack_w(spread(v_conv_w), v_pre_mix_g, v_post_mix_g, v_pre_mlp_g, v_post_mlp_g, v_conv_out_g,
                     v_attn_out_g, v_q_norm_g, v_kv_norm_g)
    small_d, small_nm, small_nv, small_g = _adamw(small_w, small_g, small_m, small_v, name="adamw_small")
    sg, sd, snm, snv = (_unpack_small(p, chip) for p in (small_g, small_d, small_nm, small_nv))

    for src, dst in ((sg, gw), (sd, dw), (snm, nm), (snv, nv)):
        dst.update(src)

    loss = lax.psum(loss_part[0, 0], ("x", "y", "c"))
    order = ["pre_mix_g", "w_in", "conv_w", "q_norm_g", "w_uq", "kv_norm_g", "w_ukv", "conv_out_g", "attn_out_g",
             "w_o", "post_mix_g", "pre_mlp_g", "w_up", "w_down", "post_mlp_g"]
    return (loss, grad_x.reshape(1, s, d), *[gw[k] for k in order], *[dw[k] for k in order],
            *[nm[k] for k in order], *[nv[k] for k in order])
```

```python
import functools

import jax
import jax.numpy as jnp
from jax import lax
from jax.experimental import pallas as pl
from jax.experimental.pallas import tpu as pltpu

EPS = 1e-6
NEG_INF = -1e30
CHUNK_SHIFT = 6
N_HEADS = 8
HEAD_PAD = 256
QK_NOPE = 128
QK_ROPE = 64
V_DIM = 128
CONV_WIDTH = 1024
Q_RANK = 768
KV_RANK = 512
ROPE_THETA = 10000.0
ATTN_SCALE = (QK_NOPE + QK_ROPE) ** -0.5
ADAM_LR, ADAM_B1, ADAM_B2, ADAM_EPS, ADAM_WD, ADAM_STEP = 0.001, 0.9, 0.999, 1e-08, 0.01, 10

COL_CQ = 3 * CONV_WIDTH
COL_KR = COL_CQ + Q_RANK
COL_CKV = 4096
IN_PAD = COL_CKV + KV_RANK
IN_WIDTH = 3 * CONV_WIDTH + Q_RANK + KV_RANK + QK_ROPE

VMEM_LIMIT_BYTES = 56 * 1024 * 1024
MESH = pl.DeviceIdType.MESH
ANY = pl.BlockSpec(memory_space=pl.ANY)

_NN = (((1,), (0,)), ((), ()))
_NT = (((1,), (1,)), ((), ()))
_TN = (((0,), (0,)), ((), ()))


def _params(sem):
    return pltpu.CompilerParams(dimension_semantics=sem, vmem_limit_bytes=VMEM_LIMIT_BYTES)


class _Buf:
    def __init__(self, arr):
        self.arr = arr


def _position():
    return lax.axis_index("x"), lax.axis_index("y"), lax.axis_index("c")


def _other_chips(x, y):
    return [(2 * (1 - x) + y, (1 - x, y)), (2 * x + (1 - y), (x, 1 - y)), (2 * (1 - x) + (1 - y), (1 - x, 1 - y))]


def _remote(src, dst, sems, k, to):
    send, recv, off = sems
    return pltpu.make_async_remote_copy(src_ref=src, dst_ref=dst, send_sem=send.at[off + k], recv_sem=recv.at[off + k],
                                        device_id=to, device_id_type=MESH)


class _GatherIci:
    n_sems = 3

    def __init__(self, buf, lo, n):
        self.buf, self.lo, self.n, self.buffers, self.sources = buf, lo, n, [buf], []

    def _rows(self, ref, slot, which):
        return ref.at[slot, pl.ds(which * (ref.shape[1] // 2) + self.lo, self.n), :]

    def start(self, refs, sems):
        ref = refs[id(self.buf)]
        x, y, c = _position()
        mine = self._rows(ref, 2 * x + y, c)
        for k, (_, xy) in enumerate(_other_chips(x, y)):
            _remote(mine, mine, sems, k, (*xy, c)).start()

    def wait(self, refs, sems):
        ref = refs[id(self.buf)]
        x, y, c = _position()
        mine = self._rows(ref, 2 * x + y, c)
        for k, (slot, xy) in enumerate(_other_chips(x, y)):
            landed = self._rows(ref, slot, c)
            _remote(landed, landed, sems, k, (*xy, c)).wait_recv()
            _remote(mine, mine, sems, k, (*xy, c)).wait_send()


class _GatherD2d(_GatherIci):
    def start(self, refs, sems):
        ref = refs[id(self.buf)]
        x, y, c = _position()
        for k, (slot, _) in enumerate(_other_chips(x, y)):
            rows = self._rows(ref, slot, c)
            _remote(rows, rows, sems, k, (x, y, 1 - c)).start()

    def wait(self, refs, sems):
        ref = refs[id(self.buf)]
        x, y, c = _position()
        for k, (slot, _) in enumerate(_other_chips(x, y)):
            sent, landed = self._rows(ref, slot, c), self._rows(ref, slot, 1 - c)
            _remote(landed, landed, sems, k, (x, y, 1 - c)).wait_recv()
            _remote(sent, sent, sems, k, (x, y, 1 - c)).wait_send()


class _ScatterIci:
    n_sems = 3

    def __init__(self, src, dst, lo, n):
        self.src, self.dst, self.lo, self.n, self.buffers, self.sources = src, dst, lo, n, [dst], [src]

    def _rows(self, ref, slot):
        return ref.at[slot, pl.ds(self.lo, self.n), :]

    def start(self, refs, sems):
        src, dst = refs[id(self.src)], refs[id(self.dst)]
        x, y, c = _position()
        for k, (slot, xy) in enumerate(_other_chips(x, y)):
            _remote(self._rows(src, slot), self._rows(dst, 2 * x + y), sems, k, (*xy, c)).start()

    def wait(self, refs, sems):
        src, dst = refs[id(self.src)], refs[id(self.dst)]
        x, y, c = _position()
        for k, (slot, xy) in enumerate(_other_chips(x, y)):
            _remote(self._rows(src, slot), self._rows(dst, slot), sems, k, (*xy, c)).wait_recv()
            _remote(self._rows(src, slot), self._rows(dst, slot), sems, k, (*xy, c)).wait_send()


class _PairExchange:
    n_sems = 1

    def __init__(self, src, dst):
        self.src, self.dst, self.buffers, self.sources = src, dst, [dst], [src]

    def _copy(self, refs, sems):
        src, dst = refs[id(self.src)], refs[id(self.dst)]
        x, y, c = _position()
        h = src.shape[1] // 2
        return _remote(src.at[:, pl.ds((1 - c) * h, h), :], dst, sems, 0, (x, y, 1 - c))

    def start(self, refs, sems):
        self._copy(refs, sems).start()

    def wait(self, refs, sems):
        self._copy(refs, sems).wait()


class _PairShare:
    n_sems = 1

    def __init__(self, buf):
        self.buf, self.buffers, self.sources = buf, [buf], []

    def _rows(self, ref, which):
        h = ref.shape[0] // 2
        return ref.at[pl.ds(which * h, h), :]

    def start(self, refs, sems):
        ref = refs[id(self.buf)]
        x, y, c = _position()
        _remote(self._rows(ref, c), self._rows(ref, c), sems, 0, (x, y, 1 - c)).start()

    def wait(self, refs, sems):
        ref = refs[id(self.buf)]
        x, y, c = _position()
        _remote(self._rows(ref, c), self._rows(ref, c), sems, 0, (x, y, 1 - c)).wait_send()
        _remote(self._rows(ref, 1 - c), self._rows(ref, 1 - c), sems, 0, (x, y, 1 - c)).wait_recv()


def _unique(items):
    seen, out = set(), []
    for it in items:
        if id(it) not in seen:
            seen.add(id(it))
            out.append(it)
    return out


def _job_operands(jobs):
    sources = _unique([a for j in jobs for a in j.sources])
    buffers = _unique([b for j in jobs for b in j.buffers])
    held = [b for b in buffers if not isinstance(b.arr, jax.ShapeDtypeStruct)]
    fresh = [b for b in buffers if isinstance(b.arr, jax.ShapeDtypeStruct)]
    return sources, held, fresh


def _sem_offsets(jobs):
    offs, total = [], 0
    for j in jobs:
        offs.append(total)
        total += j.n_sems
    return offs, total


def _call(body, *, name, grid, in_specs, out_specs, out_shape, args, semantics, scratch_shapes=(), jobs=()):
    if not jobs:
        return pl.pallas_call(
            body, name=name, grid=grid, in_specs=list(in_specs), out_specs=list(out_specs),
            out_shape=list(out_shape), scratch_shapes=list(scratch_shapes), compiler_params=_params(semantics),
        )(*args)
    sources, held, fresh = _job_operands(jobs)
    offs, n_sem = _sem_offsets(jobs)
    n_in, n_out, n_scr = len(in_specs), len(out_specs), len(scratch_shapes)
    n_src, n_held, n_fresh = len(sources), len(held), len(fresh)

    def carried(*refs):
        ins = refs[:n_in]
        src_refs = refs[n_in:n_in + n_src]
        o0 = n_in + n_src + n_held
        outs = refs[o0:o0 + n_out]
        buf_refs = refs[o0 + n_out:o0 + n_out + n_held + n_fresh]
        s0 = o0 + n_out + n_held + n_fresh
        scratch = refs[s0:s0 + n_scr]
        send, recv = refs[s0 + n_scr], refs[s0 + n_scr + 1]
        where = {id(a): r for a, r in zip(sources, src_refs)}
        where.update({id(b): r for b, r in zip(held + fresh, buf_refs)})
        ids = [pl.program_id(a) for a in range(len(grid))]
        first = functools.reduce(jnp.logical_and, [i == 0 for i in ids])
        last = functools.reduce(jnp.logical_and, [i == g - 1 for i, g in zip(ids, grid)])

        @pl.when(first)
        def _():
            for j, off in zip(jobs, offs):
                j.start(where, (send, recv, off))

        body(*ins, *outs, *scratch)

        @pl.when(last)
        def _():
            for j, off in zip(jobs, offs):
                j.wait(where, (send, recv, off))

    shape_of = lambda b: jax.ShapeDtypeStruct(b.arr.shape, b.arr.dtype)
    res = pl.pallas_call(
        carried, name=name, grid=grid,
        in_specs=[*in_specs, *[ANY] * (n_src + n_held)],
        out_specs=[*out_specs, *[ANY] * (n_held + n_fresh)],
        out_shape=[*out_shape, *[shape_of(b) for b in held + fresh]],
        input_output_aliases={n_in + n_src + i: n_out + i for i in range(n_held)},
        scratch_shapes=[*scratch_shapes, pltpu.SemaphoreType.DMA((n_sem,)), pltpu.SemaphoreType.DMA((n_sem,))],
        compiler_params=pltpu.CompilerParams(dimension_semantics=("arbitrary",) * len(grid),
                                             vmem_limit_bytes=VMEM_LIMIT_BYTES, has_side_effects=True),
    )(*args, *sources, *[b.arr for b in held])
    for b, new in zip(held + fresh, res[n_out:]):
        b.arr = new
    return list(res[:n_out])


def _comm(name, phases):
    jobs = [j for ph in phases for j in ph]
    sources, held, fresh = _job_operands(jobs)
    offs, n_sem = _sem_offsets(jobs)
    off_of = {id(j): o for j, o in zip(jobs, offs)}
    n_src, n_held, n_fresh = len(sources), len(held), len(fresh)

    def body(*refs):
        src_refs = refs[:n_src]
        buf_refs = refs[n_src + n_held:n_src + 2 * n_held + n_fresh]
        send, recv = refs[-2], refs[-1]
        where = {id(a): r for a, r in zip(sources, src_refs)}
        where.update({id(b): r for b, r in zip(held + fresh, buf_refs)})
        for ph in phases:
            for j in ph:
                j.start(where, (send, recv, off_of[id(j)]))
            for j in ph:
                j.wait(where, (send, recv, off_of[id(j)]))

    shape_of = lambda b: jax.ShapeDtypeStruct(b.arr.shape, b.arr.dtype)
    res = pl.pallas_call(
        body, name=name,
        in_specs=[ANY] * (n_src + n_held), out_specs=[ANY] * (n_held + n_fresh),
        out_shape=[shape_of(b) for b in held + fresh],
        input_output_aliases={n_src + i: i for i in range(n_held)},
        scratch_shapes=[pltpu.SemaphoreType.DMA((n_sem,)), pltpu.SemaphoreType.DMA((n_sem,))],
        compiler_params=pltpu.CompilerParams(has_side_effects=True),
    )(*sources, *[b.arr for b in held])
    for b, new in zip(held + fresh, res):
        b.arr = new


def _matmul(a, b, *, dims, mnk, tiles, name, out_dtype=jnp.float32, a_spec=None, b_spec=None,
            out_shape=None, o_spec=None, epilogue=None, extra=(), extra_specs=(), jobs=()):
    m, n, k = mnk
    tm, tn, tk = tiles
    assert m % tm == 0 and n % tn == 0 and k % tk == 0, (name, mnk, tiles)
    gm, gn, gk = m // tm, n // tn, k // tk
    if a_spec is None:
        a_spec = (pl.BlockSpec((tk, tm), lambda i, j, l: (l, i)) if dims is _TN
                  else pl.BlockSpec((tm, tk), lambda i, j, l: (i, l)))
    if b_spec is None:
        b_spec = (pl.BlockSpec((tn, tk), lambda i, j, l: (j, l)) if dims is _NT
                  else pl.BlockSpec((tk, tn), lambda i, j, l: (l, j)))
    if out_shape is None:
        out_shape = jax.ShapeDtypeStruct((m, n), out_dtype)
    if o_spec is None:
        o_spec = pl.BlockSpec((tm, tn), lambda i, j, l: (i, j))
    single = not isinstance(out_shape, (tuple, list))
    n_extra = len(extra)

    def finish(acc, extra_refs, out_refs):
        if epilogue is None:
            out_refs[0][...] = acc.astype(out_refs[0].dtype)
        else:
            epilogue(acc, extra_refs, out_refs)

    def body_whole_k(*refs):
        a_ref, b_ref = refs[0], refs[1]
        acc = lax.dot_general(a_ref[...], b_ref[...], dims, preferred_element_type=jnp.float32)
        finish(acc, refs[2:2 + n_extra], refs[2 + n_extra:])

    def body_split_k(*refs):
        a_ref, b_ref = refs[0], refs[1]
        extra_refs = refs[2:2 + n_extra]
        out_refs = refs[2 + n_extra:-1]
        acc_ref = refs[-1]
        step = pl.program_id(2)
        part = lax.dot_general(a_ref[...], b_ref[...], dims, preferred_element_type=jnp.float32)

        @pl.when(step == 0)
        def _():
            acc_ref[...] = part

        @pl.when(jnp.logical_and(step > 0, step < gk - 1))
        def _():
            acc_ref[...] += part

        @pl.when(step == gk - 1)
        def _():
            finish(acc_ref[...] + part, extra_refs, out_refs)

    res = _call(
        body_whole_k if gk == 1 else body_split_k, name=name, grid=(gm, gn, gk),
        in_specs=[a_spec, b_spec, *extra_specs],
        out_specs=[o_spec] if single else list(o_spec),
        out_shape=[out_shape] if single else list(out_shape),
        scratch_shapes=[] if gk == 1 else [pltpu.VMEM((tm, tn), jnp.float32)],
        semantics=("parallel", "parallel", "arbitrary"), args=(a, b, *extra), jobs=jobs)
    return res[0] if single else res


def _rstd(x):
    return lax.rsqrt(jnp.mean(x * x, axis=-1, keepdims=True) + EPS)


def _rms_bwd_rows(x, g, dy):
    r = _rstd(x)
    xn = x * r
    dyg = dy * g
    dx = r * (dyg - xn * jnp.mean(xn * dyg, axis=-1, keepdims=True))
    return dx, dy * xn


def _acc_rows(ref, rows, first):
    part = jnp.sum(rows, axis=0, keepdims=True)

    @pl.when(first)
    def _():
        ref[...] = part

    @pl.when(jnp.logical_not(first))
    def _():
        ref[...] += part


def _rms_fwd(x, g, *, width, col, tm, name, jobs=()):
    s = x.shape[0]

    def body(x_ref, g_ref, o_ref):
        v = x_ref[...]
        o_ref[...] = (v * _rstd(v) * g_ref[...]).astype(o_ref.dtype)

    return _call(
        body, name=name, grid=(s // tm,),
        in_specs=[pl.BlockSpec((tm, width), lambda i: (i, col)), pl.BlockSpec((1, width), lambda i: (0, 0))],
        out_specs=[pl.BlockSpec((tm, width), lambda i: (i, 0))],
        out_shape=[jax.ShapeDtypeStruct((s, width), jnp.bfloat16)],
        semantics=("parallel",), args=(x, g), jobs=jobs)[0]


def _rms_bwd(x, g, dy, *, width, col, tm, name, jobs=()):
    s = x.shape[0]

    def body(x_ref, g_ref, dy_ref, dx_ref, dg_ref):
        dx, dgr = _rms_bwd_rows(x_ref[...], g_ref[...], dy_ref[...])
        dx_ref[...] = dx.astype(dx_ref.dtype)
        _acc_rows(dg_ref, dgr, pl.program_id(0) == 0)

    return _call(
        body, name=name, grid=(s // tm,),
        in_specs=[pl.BlockSpec((tm, width), lambda i: (i, col)), pl.BlockSpec((1, width), lambda i: (0, 0)),
                  pl.BlockSpec((tm, width), lambda i: (i, 0))],
        out_specs=[pl.BlockSpec((tm, width), lambda i: (i, 0)), pl.BlockSpec((1, width), lambda i: (0, 0))],
        out_shape=[jax.ShapeDtypeStruct((s, width), jnp.bfloat16), jax.ShapeDtypeStruct((1, width), jnp.float32)],
        semantics=("arbitrary",), args=(x, g, dy), jobs=jobs)


def _row_specs(tm, d, n):
    return [pl.BlockSpec((tm, d), lambda i: (i, 0)) for _ in range(n)]


def _gain_specs(d, n):
    return [pl.BlockSpec((1, d), lambda i: (0, 0)) for _ in range(n)]


def _mix_residual_fwd(x, mix, g_post_mix, g_pre_mlp, *, tm, jobs=()):
    s, d = x.shape

    def body(x_ref, mix_ref, g1_ref, g2_ref, x2_ref, h2_ref):
        mixv = mix_ref[...]
        x2 = x_ref[...] + mixv * _rstd(mixv) * g1_ref[...]
        x2_ref[...] = x2
        h2_ref[...] = (x2 * _rstd(x2) * g2_ref[...]).astype(h2_ref.dtype)

    return _call(
        body, name="mix_residual_fwd", grid=(s // tm,),
        in_specs=_row_specs(tm, d, 2) + _gain_specs(d, 2),
        out_specs=_row_specs(tm, d, 2),
        out_shape=[jax.ShapeDtypeStruct((s, d), jnp.float32), jax.ShapeDtypeStruct((s, d), jnp.bfloat16)],
        semantics=("parallel",), args=(x, mix, g_post_mix, g_pre_mlp), jobs=jobs)


def _loss_head(x2, mlp, target, g_post_mlp, *, tm, jobs=()):
    s, d = x2.shape

    def body(x2_ref, m_ref, t_ref, g_ref, dx3_ref, dm_ref, dg_ref, loss_ref):
        first = pl.program_id(0) == 0
        mv = m_ref[...]
        g = g_ref[...]
        diff = x2_ref[...] + mv * _rstd(mv) * g - t_ref[...]
        dx3 = diff * (1.0 / d)
        dx3_ref[...] = dx3
        dm, dgr = _rms_bwd_rows(mv, g, dx3)
        dm_ref[...] = dm.astype(dm_ref.dtype)
        _acc_rows(dg_ref, dgr, first)
        part = 0.5 * jnp.sum(jnp.mean(diff * diff, axis=-1, keepdims=True), axis=0, keepdims=True)
        _acc_rows(loss_ref, jnp.broadcast_to(part, (1, 128)), first)

    return _call(
        body, name="loss_head", grid=(s // tm,),
        in_specs=_row_specs(tm, d, 3) + _gain_specs(d, 1),
        out_specs=_row_specs(tm, d, 2) + _gain_specs(d, 1) + [pl.BlockSpec((1, 128), lambda i: (0, 0))],
        out_shape=[jax.ShapeDtypeStruct((s, d), jnp.float32), jax.ShapeDtypeStruct((s, d), jnp.bfloat16),
                   jax.ShapeDtypeStruct((1, d), jnp.float32), jax.ShapeDtypeStruct((1, 128), jnp.float32)],
        semantics=("arbitrary",), args=(x2, mlp, target, g_post_mlp), jobs=jobs)


def _mix_residual_bwd(dx3, dh2, x2, mix, g_pre_mlp, g_post_mix, *, tm, jobs=()):
    s, d = x2.shape

    def body(dx3_ref, dh2_ref, x2_ref, mix_ref, g2_ref, g1_ref, dx2_ref, dmix_ref, dg2_ref, dg1_ref):
        first = pl.program_id(0) == 0
        d_in, dgr2 = _rms_bwd_rows(x2_ref[...], g2_ref[...], dh2_ref[...])
        dx2 = dx3_ref[...] + d_in
        dx2_ref[...] = dx2
        dmix, dgr1 = _rms_bwd_rows(mix_ref[...], g1_ref[...], dx2)
        dmix_ref[...] = dmix.astype(dmix_ref.dtype)
        _acc_rows(dg2_ref, dgr2, first)
        _acc_rows(dg1_ref, dgr1, first)

    return _call(
        body, name="mix_residual_bwd", grid=(s // tm,),
        in_specs=_row_specs(tm, d, 4) + _gain_specs(d, 2),
        out_specs=_row_specs(tm, d, 2) + _gain_specs(d, 2),
        out_shape=[jax.ShapeDtypeStruct((s, d), jnp.float32), jax.ShapeDtypeStruct((s, d), jnp.bfloat16),
                   jax.ShapeDtypeStruct((1, d), jnp.float32), jax.ShapeDtypeStruct((1, d), jnp.float32)],
        semantics=("arbitrary",), args=(dx3, dh2, x2, mix, g_pre_mlp, g_post_mix), jobs=jobs)


def _input_bwd(dx2, dh1, x, g_pre_mix, *, tm, jobs=()):
    s, d = x.shape

    def body(dx2_ref, dh1_ref, x_ref, g_ref, dx_ref, dg_ref):
        d_in, dgr = _rms_bwd_rows(x_ref[...], g_ref[...], dh1_ref[...])
        dx_ref[...] = dx2_ref[...] + d_in
        _acc_rows(dg_ref, dgr, pl.program_id(0) == 0)

    return _call(
        body, name="input_bwd", grid=(s // tm,),
        in_specs=_row_specs(tm, d, 3) + _gain_specs(d, 1),
        out_specs=_row_specs(tm, d, 1) + _gain_specs(d, 1),
        out_shape=[jax.ShapeDtypeStruct((s, d), jnp.float32), jax.ShapeDtypeStruct((1, d), jnp.float32)],
        semantics=("arbitrary",), args=(dx2, dh1, x, g_pre_mix), jobs=jobs)


def _shift_rows(z, by):
    s = z.shape[0]
    rows = lax.broadcasted_iota(jnp.int32, z.shape, 0)
    rolled = pltpu.roll(z, by % s, axis=0)
    keep = rows >= by if by > 0 else rows < s + by
    return jnp.where(keep, rolled, 0.0)


def _conv_fwd(proj, conv_w, conv_out_g, jobs=()):
    s = proj.shape[0]
    groups = CONV_WIDTH // 128

    def body(u_ref, gb_ref, gc_ref, w_ref, g_ref, y_ref):
        z = gc_ref[...] * u_ref[...]
        w = w_ref[...]
        conv = w[0:1, :] * _shift_rows(z, 2) + w[1:2, :] * _shift_rows(z, 1) + w[2:3, :] * z
        y = gb_ref[...] * conv
        y_ref[...] = (y * _rstd(y) * g_ref[...]).astype(y_ref.dtype)

    col = lambda base: pl.BlockSpec((s, 128), lambda j: (0, base + j))
    return _call(
        body, name="conv_fwd", grid=(groups,),
        in_specs=[col(0), col(groups), col(2 * groups), pl.BlockSpec((3, 128), lambda j: (0, j)),
                  pl.BlockSpec((1, 128), lambda j: (0, j))],
        out_specs=[pl.BlockSpec((s, 128), lambda j: (0, j))],
        out_shape=[jax.ShapeDtypeStruct((s, CONV_WIDTH), jnp.bfloat16)],
        semantics=("parallel",), args=(proj, proj, proj, conv_w, conv_out_g), jobs=jobs)[0]


def _conv_bwd(proj, conv_w, conv_out_g, dycat, jobs=()):
    s = proj.shape[0]
    groups = CONV_WIDTH // 128

    def body(u_ref, gb_ref, gc_ref, w_ref, g_ref, dy_ref, du_ref, dgb_ref, dgc_ref, dw_ref, dg_ref):
        u, gb, gc = u_ref[...], gb_ref[...], gc_ref[...]
        w = w_ref[...]
        z = gc * u
        z1, z2 = _shift_rows(z, 1), _shift_rows(z, 2)
        conv = w[0:1, :] * z2 + w[1:2, :] * z1 + w[2:3, :] * z
        dyr, dgr = _rms_bwd_rows(gb * conv, g_ref[...], dy_ref[...])
        dg_ref[...] = jnp.sum(dgr, axis=0, keepdims=True)
        dgb_ref[...] = (dyr * conv).astype(dgb_ref.dtype)
        dconv = dyr * gb
        dw_ref[0:1, :] = jnp.sum(dconv * z2, axis=0, keepdims=True)
        dw_ref[1:2, :] = jnp.sum(dconv * z1, axis=0, keepdims=True)
        dw_ref[2:3, :] = jnp.sum(dconv * z, axis=0, keepdims=True)
        dz = w[2:3, :] * dconv + w[1:2, :] * _shift_rows(dconv, -1) + w[0:1, :] * _shift_rows(dconv, -2)
        dgc_ref[...] = (dz * u).astype(dgc_ref.dtype)
        du_ref[...] = (dz * gc).astype(du_ref.dtype)

    col = lambda base: pl.BlockSpec((s, 128), lambda j: (0, base + j))
    act = jax.ShapeDtypeStruct((s, CONV_WIDTH), jnp.bfloat16)
    return _call(
        body, name="conv_bwd", grid=(groups,),
        in_specs=[col(0), col(groups), col(2 * groups), pl.BlockSpec((3, 128), lambda j: (0, j)),
                  pl.BlockSpec((1, 128), lambda j: (0, j)), col(0)],
        out_specs=[col(0), col(0), col(0), pl.BlockSpec((3, 128), lambda j: (0, j)),
                   pl.BlockSpec((1, 128), lambda j: (0, j))],
        out_shape=[act, act, act, jax.ShapeDtypeStruct((3, CONV_WIDTH), jnp.float32),
                   jax.ShapeDtypeStruct((1, CONV_WIDTH), jnp.float32)],
        semantics=("parallel",), args=(proj, proj, proj, conv_w, conv_out_g, dycat), jobs=jobs)


def _rope_tables(s):
    pos = jnp.arange(s, dtype=jnp.float32)
    inv_freq = jnp.power(ROPE_THETA, -jnp.arange(0, QK_ROPE, 2, dtype=jnp.float32) / QK_ROPE)
    ang = pos[:, None] * inv_freq[None, :]
    cos, sin = jnp.cos(ang), jnp.sin(ang)
    zeros = jnp.zeros((s, 128 - QK_ROPE), jnp.float32)
    return (jnp.concatenate([cos, cos, zeros], axis=1), jnp.concatenate([-sin, sin, zeros], axis=1))


def _swap_halves(x):
    lane = lax.broadcasted_iota(jnp.int32, x.shape, 1)
    swapped = jnp.where(lane < QK_ROPE // 2, pltpu.roll(x, 128 - QK_ROPE // 2, axis=1),
                        pltpu.roll(x, QK_ROPE // 2, axis=1))
    return jnp.where(lane < QK_ROPE, swapped, 0.0)


def _rope(x, cos, sin):
    return x * cos + _swap_halves(x) * sin


def _rope_transposed(d, cos, sin):
    return d * cos + _swap_halves(d * sin)


def _qk_rope_fwd(q_pad, proj, cos, sin, *, tm, jobs=()):
    s = q_pad.shape[0]
    wq = N_HEADS * HEAD_PAD

    def body(q_ref, kr_ref, cos_ref, sin_ref, qo_ref, kro_ref):
        c, sn = cos_ref[...], sin_ref[...]
        for h in range(N_HEADS):
            lo = h * HEAD_PAD
            qo_ref[:, lo:lo + 128] = q_ref[:, lo:lo + 128].astype(qo_ref.dtype)
            qo_ref[:, lo + 128:lo + 256] = _rope(q_ref[:, lo + 128:lo + 256], c, sn).astype(qo_ref.dtype)
        kro_ref[...] = _rope(kr_ref[...], c, sn).astype(kro_ref.dtype)

    return _call(
        body, name="qk_rope_fwd", grid=(s // tm,),
        in_specs=[pl.BlockSpec((tm, wq), lambda i: (i, 0)), pl.BlockSpec((tm, 128), lambda i: (i, COL_KR // 128)),
                  pl.BlockSpec((tm, 128), lambda i: (i, 0)), pl.BlockSpec((tm, 128), lambda i: (i, 0))],
        out_specs=[pl.BlockSpec((tm, wq), lambda i: (i, 0)), pl.BlockSpec((tm, 128), lambda i: (i, 0))],
        out_shape=[jax.ShapeDtypeStruct((s, wq), jnp.bfloat16), jax.ShapeDtypeStruct((s, 128), jnp.bfloat16)],
        semantics=("parallel",), args=(q_pad, proj, cos, sin), jobs=jobs)


def _qk_rope_bwd(dq_pad, dk_pad, dv, cos, sin, *, tm, jobs=()):
    s = dq_pad.shape[0]
    wq = N_HEADS * HEAD_PAD

    def body(dq_ref, dk_ref, dv_ref, cos_ref, sin_ref, dqo_ref, dkv_ref, dkr_ref):
        c, sn = cos_ref[...], sin_ref[...]
        dkr = jnp.zeros((tm, 128), jnp.float32)
        for h in range(N_HEADS):
            lo = h * HEAD_PAD
            dqo_ref[:, lo:lo + 128] = dq_ref[:, lo:lo + 128].astype(dqo_ref.dtype)
            dqo_ref[:, lo + 128:lo + 256] = _rope_transposed(dq_ref[:, lo + 128:lo + 256], c, sn).astype(dqo_ref.dtype)
            dkv_ref[:, lo:lo + 128] = dk_ref[:, lo:lo + 128].astype(dkv_ref.dtype)
            dkv_ref[:, lo + 128:lo + 256] = dv_ref[:, h * V_DIM:(h + 1) * V_DIM].astype(dkv_ref.dtype)
            dkr = dkr + dk_ref[:, lo + 128:lo + 256]
        dkr_ref[...] = _rope_transposed(dkr, c, sn).astype(dkr_ref.dtype)

    return _call(
        body, name="qk_rope_bwd", grid=(s // tm,),
        in_specs=[pl.BlockSpec((tm, wq), lambda i: (i, 0)), pl.BlockSpec((tm, wq), lambda i: (i, 0)),
                  pl.BlockSpec((tm, N_HEADS * V_DIM), lambda i: (i, 0)),
                  pl.BlockSpec((tm, 128), lambda i: (i, 0)), pl.BlockSpec((tm, 128), lambda i: (i, 0))],
        out_specs=[pl.BlockSpec((tm, wq), lambda i: (i, 0)), pl.BlockSpec((tm, wq), lambda i: (i, 0)),
                   pl.BlockSpec((tm, 128), lambda i: (i, 0))],
        out_shape=[jax.ShapeDtypeStruct((s, wq), jnp.bfloat16), jax.ShapeDtypeStruct((s, wq), jnp.bfloat16),
                   jax.ShapeDtypeStruct((s, 128), jnp.bfloat16)],
        semantics=("parallel",), args=(dq_pad, dk_pad, dv, cos, sin), jobs=jobs)


def _visible(q0, k0, t):
    qpos = q0 + lax.broadcasted_iota(jnp.int32, (t, t), 0)
    kpos = k0 + lax.broadcasted_iota(jnp.int32, (t, t), 1)
    return lax.shift_right_logical(kpos, CHUNK_SHIFT) <= lax.shift_right_logical(qpos, CHUNK_SHIFT)


def _attn_fwd(q, kv, kr, attn_out_g, *, t, jobs=()):
    s = q.shape[0]
    nq = s // t

    def body(q_ref, kn_ref, v_ref, kr_ref, g_ref, o_ref, lse_ref, y_ref, kcat_ref):
        i = pl.program_id(1)

        @pl.when(i == 0)
        def _():
            kcat_ref[:, 0:128] = kn_ref[...]
            kcat_ref[:, 128:256] = kr_ref[...]

        qv = q_ref[...]

        def step(j, carry, diagonal):
            m, l, acc = carry
            k = kcat_ref[pl.ds(pl.multiple_of(j * t, t), t), :]
            v = v_ref[pl.ds(pl.multiple_of(j * t, t), t), :]
            sc = lax.dot_general(qv, k, _NT, preferred_element_type=jnp.float32) * ATTN_SCALE
            if diagonal:
                sc = jnp.where(_visible(0, 0, t), sc, NEG_INF)
            m_new = jnp.maximum(m, jnp.max(sc, axis=-1, keepdims=True))
            p = jnp.exp(sc - m_new)
            alpha = jnp.exp(m - m_new)
            l = alpha * l + jnp.sum(p, axis=-1, keepdims=True)
            acc = alpha * acc + lax.dot_general(p.astype(jnp.bfloat16), v, _NN, preferred_element_type=jnp.float32)
            return m_new, l, acc

        init = (jnp.full((t, 1), NEG_INF, jnp.float32), jnp.zeros((t, 1), jnp.float32),
                jnp.zeros((t, V_DIM), jnp.float32))
        before = lax.fori_loop(0, i, functools.partial(step, diagonal=False), init)
        m, l, acc = step(i, before, True)
        o = acc / l
        o_ref[...] = o
        lse_ref[...] = jnp.broadcast_to(m + jnp.log(l), (t, 128))
        y_ref[...] = (o * _rstd(o) * g_ref[...]).astype(y_ref.dtype)

    head_rows = lambda w, f: pl.BlockSpec((s, w), lambda h, i: (0, f(h)))
    blk = pl.BlockSpec((t, 128), lambda h, i: (i, h))
    full = jax.ShapeDtypeStruct((s, N_HEADS * V_DIM), jnp.float32)
    return _call(
        body, name="attn_fwd", grid=(N_HEADS, nq),
        in_specs=[pl.BlockSpec((t, HEAD_PAD), lambda h, i: (i, h)), head_rows(128, lambda h: 2 * h),
                  head_rows(128, lambda h: 2 * h + 1), head_rows(128, lambda h: 0),
                  pl.BlockSpec((1, 128), lambda h, i: (0, h))],
        out_specs=[blk, blk, blk],
        out_shape=[full, full, jax.ShapeDtypeStruct((s, N_HEADS * V_DIM), jnp.bfloat16)],
        scratch_shapes=[pltpu.VMEM((s, HEAD_PAD), jnp.bfloat16)],
        semantics=("arbitrary", "arbitrary"), args=(q, kv, kv, kr, attn_out_g), jobs=jobs)


def _attn_norm_bwd(o, attn_out_g, dycat, jobs=()):
    s = o.shape[0]

    def body(o_ref, g_ref, dy_ref, do_ref, delta_ref, dg_ref):
        ov = o_ref[...]
        do, dgr = _rms_bwd_rows(ov, g_ref[...], dy_ref[...])
        do_ref[...] = do.astype(do_ref.dtype)
        delta_ref[...] = jnp.broadcast_to(jnp.sum(do * ov, axis=-1, keepdims=True), (s, 128))
        dg_ref[...] = jnp.sum(dgr, axis=0, keepdims=True)

    col = lambda base: pl.BlockSpec((s, 128), lambda h: (0, base + h))
    return _call(
        body, name="attn_norm_bwd", grid=(N_HEADS,),
        in_specs=[col(0), pl.BlockSpec((1, 128), lambda h: (0, h)), col(CONV_WIDTH // 128)],
        out_specs=[col(0), col(0), pl.BlockSpec((1, 128), lambda h: (0, h))],
        out_shape=[jax.ShapeDtypeStruct((s, N_HEADS * V_DIM), jnp.bfloat16),
                   jax.ShapeDtypeStruct((s, N_HEADS * V_DIM), jnp.float32),
                   jax.ShapeDtypeStruct((1, N_HEADS * V_DIM), jnp.float32)],
        semantics=("parallel",), args=(o, attn_out_g, dycat), jobs=jobs)


def _attn_bwd(q, kv, kr, do, lse, delta, *, t, jobs=()):
    s = q.shape[0]
    nq = s // t

    def body(q_ref, kn_ref, v_ref, kr_ref, do_ref, lse_ref, delta_ref, dq_ref, dk_ref, dv_ref, kcat_ref):
        kcat_ref[:, 0:128] = kn_ref[...]
        kcat_ref[:, 128:256] = kr_ref[...]
        dq_ref[...] = jnp.zeros_like(dq_ref)
        dk_ref[...] = jnp.zeros_like(dk_ref)
        dv_ref[...] = jnp.zeros_like(dv_ref)

        def kv_step(j, _):
            krows = pl.ds(pl.multiple_of(j * t, t), t)
            k = kcat_ref[krows, :]
            v = v_ref[krows, :]

            def q_step(i, _, diagonal):
                qrows = pl.ds(pl.multiple_of(i * t, t), t)
                qv = q_ref[qrows, :]
                dov = do_ref[qrows, :]
                sc = lax.dot_general(qv, k, _NT, preferred_element_type=jnp.float32) * ATTN_SCALE
                if diagonal:
                    sc = jnp.where(_visible(0, 0, t), sc, NEG_INF)
                p = jnp.exp(sc - lse_ref[qrows, :][:, 0:1])
                dp = lax.dot_general(dov, v, _NT, preferred_element_type=jnp.float32)
                ds = (p * (dp - delta_ref[qrows, :][:, 0:1]) * ATTN_SCALE).astype(jnp.bfloat16)
                dv_ref[krows, :] += lax.dot_general(p.astype(jnp.bfloat16), dov, _TN,
                                                    preferred_element_type=jnp.float32)
                dk_ref[krows, :] += lax.dot_general(ds, qv, _TN, preferred_element_type=jnp.float32)
                dq_ref[qrows, :] += lax.dot_general(ds, k, _NN, preferred_element_type=jnp.float32)
                return 0

            q_step(j, 0, True)
            lax.fori_loop(j + 1, nq, functools.partial(q_step, diagonal=False), 0)
            return 0

        lax.fori_loop(0, nq, kv_step, 0)

    col = lambda w, f: pl.BlockSpec((s, w), lambda h: (0, f(h)))
    return _call(
        body, name="attn_bwd", grid=(N_HEADS,),
        in_specs=[col(HEAD_PAD, lambda h: h), col(128, lambda h: 2 * h), col(128, lambda h: 2 * h + 1),
                  col(128, lambda h: 0), col(128, lambda h: h), col(128, lambda h: h), col(128, lambda h: h)],
        out_specs=[col(HEAD_PAD, lambda h: h), col(HEAD_PAD, lambda h: h), col(128, lambda h: h)],
        out_shape=[jax.ShapeDtypeStruct((s, N_HEADS * HEAD_PAD), jnp.float32),
                   jax.ShapeDtypeStruct((s, N_HEADS * HEAD_PAD), jnp.float32),
                   jax.ShapeDtypeStruct((s, N_HEADS * V_DIM), jnp.float32)],
        scratch_shapes=[pltpu.VMEM((s, HEAD_PAD), jnp.bfloat16)],
        semantics=("parallel",), args=(q, kv, kv, kr, do, lse, delta), jobs=jobs)


def _row_tile(rows):
    for cand in (256, 128, 64, 32, 16, 8):
        if rows % cand == 0:
            return cand
    return rows


def _cast_into_slot(w, pos, *, name):
    r, c = w.shape
    tr = _row_tile(r)

    def body(pos_ref, w_ref, o_ref):
        o_ref[...] = w_ref[...].astype(o_ref.dtype)

    return pl.pallas_call(
        body, name=name, out_shape=jax.ShapeDtypeStruct((4, r, c), jnp.bfloat16),
        grid_spec=pltpu.PrefetchScalarGridSpec(
            num_scalar_prefetch=1, grid=(r // tr,),
            in_specs=[pl.BlockSpec((tr, c), lambda i, p: (i, 0))],
            out_specs=pl.BlockSpec((None, tr, c), lambda i, p: (p[1], i, 0))),
        compiler_params=_params(("parallel",)),
    )(pos, w)


def _pair_add(g, theirs, pos, *, name):
    n, h, c = theirs.shape
    tr = _row_tile(h)
    nb = h // tr

    def body(pos_ref, a_ref, b_ref, o_ref):
        o_ref[...] = (a_ref[...].astype(jnp.float32) + b_ref[...].astype(jnp.float32)).astype(o_ref.dtype)

    spec = pl.BlockSpec((None, tr, c), lambda j, i, p: (j, i, 0))
    return pl.pallas_call(
        body, name=name, out_shape=jax.ShapeDtypeStruct(theirs.shape, jnp.bfloat16),
        grid_spec=pltpu.PrefetchScalarGridSpec(
            num_scalar_prefetch=1, grid=(n, nb),
            in_specs=[pl.BlockSpec((None, tr, c), lambda j, i, p: (j, i + p[0] * nb, 0)), spec],
            out_specs=spec),
        compiler_params=_params(("parallel", "parallel")),
    )(pos, g, theirs)


def _chip_sum(by_source, pair_sum, pos, *, name):
    n, h, c = by_source.shape
    tr = _row_tile(h)
    nb = h // tr

    def body(pos_ref, p0, p1, p2, p3, own_ref, o_ref):
        own = own_ref[...].astype(jnp.float32)
        term = lambda k, ref: jnp.where(pos_ref[1] == k, own, ref[...].astype(jnp.float32))
        o_ref[...] = ((term(0, p0) + term(1, p1)) + term(2, p2)) + term(3, p3)

    def source(k):
        return pl.BlockSpec((None, tr, c), lambda i, p: (jnp.where(p[1] == k, (k + 1) % 4, k), i, 0))

    return pl.pallas_call(
        body, name=name, out_shape=jax.ShapeDtypeStruct((2 * h, c), jnp.float32),
        grid_spec=pltpu.PrefetchScalarGridSpec(
            num_scalar_prefetch=1, grid=(nb,),
            in_specs=[source(0), source(1), source(2), source(3),
                      pl.BlockSpec((None, tr, c), lambda i, p: (p[1], i, 0))],
            out_specs=pl.BlockSpec((tr, c), lambda i, p: (i + p[0] * nb, 0))),
        compiler_params=_params(("parallel",)),
    )(pos, by_source, by_source, by_source, by_source, pair_sum)


def _adamw(w, g, m, v, *, name, jobs=()):
    r, c = w.shape
    tr = _row_tile(r)

    def body(w_ref, g_ref, m_ref, v_ref, d_ref, mo_ref, vo_ref, go_ref):
        gv = g_ref[...]
        go_ref[...] = gv
        mn = ADAM_B1 * m_ref[...] + (1.0 - ADAM_B1) * gv
        vn = ADAM_B2 * v_ref[...] + (1.0 - ADAM_B2) * (gv * gv)
        m_hat = mn / (1.0 - ADAM_B1 ** ADAM_STEP)
        v_hat = vn / (1.0 - ADAM_B2 ** ADAM_STEP)
        d_ref[...] = -ADAM_LR * (m_hat / (jnp.sqrt(v_hat) + ADAM_EPS) + ADAM_WD * w_ref[...])
        mo_ref[...] = mn
        vo_ref[...] = vn

    spec = pl.BlockSpec((tr, c), lambda i: (i, 0))
    out = jax.ShapeDtypeStruct((r, c), jnp.float32)
    return _call(body, name=name, grid=(r // tr,), in_specs=[spec] * 4, out_specs=[spec] * 4, out_shape=[out] * 4,
                 semantics=("parallel",), args=(w, g, m, v), jobs=jobs)


def _all_reduce_small(block):
    r, c = block.shape

    def body(src_ref, out_ref, stage_ref, send_sems, recv_sems):
        x, y, cc = _position()
        me = 4 * x + 2 * y + cc
        stage_ref[me] = src_ref[...]
        flip = lambda v, on: 1 - v if on else v
        peers = [(flip(x, dx), flip(y, dy), flip(cc, dc)) for dx in (0, 1) for dy in (0, 1) for dc in (0, 1)][1:]
        copies = [pltpu.make_async_remote_copy(
            src_ref=stage_ref.at[me], dst_ref=stage_ref.at[me],
            send_sem=send_sems.at[k], recv_sem=recv_sems.at[k], device_id=peer, device_id_type=MESH)
            for k, peer in enumerate(peers)]
        for cp in copies:
            cp.start()
        for k, (px, py, pc) in enumerate(peers):
            them = 4 * px + 2 * py + pc
            pltpu.make_async_remote_copy(
                src_ref=stage_ref.at[them], dst_ref=stage_ref.at[them],
                send_sem=send_sems.at[k], recv_sem=recv_sems.at[k], device_id=(px, py, pc),
                device_id_type=MESH).wait_recv()
        for cp in copies:
            cp.wait_send()
        total = stage_ref[0]
        for d in range(1, 8):
            total = total + stage_ref[d]
        out_ref[...] = total

    return pl.pallas_call(
        body, name="all_reduce_small",
        in_specs=[pl.BlockSpec(memory_space=pltpu.VMEM)], out_specs=pl.BlockSpec(memory_space=pltpu.VMEM),
        out_shape=jax.ShapeDtypeStruct((r, c), jnp.float32),
        scratch_shapes=[pltpu.VMEM((8, r, c), jnp.float32), pltpu.SemaphoreType.DMA((7,)),
                        pltpu.SemaphoreType.DMA((7,))],
        compiler_params=pltpu.CompilerParams(has_side_effects=True),
    )(block)


def _cols_from_shards(g):
    n, r, c = g.shape
    return jnp.transpose(g, (1, 0, 2)).reshape(r, n * c)


def _cols_to_shards(w, n=4):
    r, c = w.shape
    return jnp.transpose(w.reshape(r, n, c // n), (1, 0, 2))


def _pad_w_in(full):
    d = full.shape[0]
    zeros = jnp.zeros((d, COL_CKV - COL_KR - QK_ROPE), full.dtype)
    return jnp.concatenate([full[:, :COL_KR], full[:, IN_WIDTH - QK_ROPE:], zeros,
                            full[:, COL_KR:COL_KR + KV_RANK]], axis=1)


def _unpad_w_in(padded):
    return jnp.concatenate([padded[:, :COL_KR], padded[:, COL_CKV:COL_CKV + KV_RANK],
                            padded[:, COL_KR:COL_KR + QK_ROPE]], axis=1)


def _pad_w_uq(full):
    r = full.shape[0]
    per_head = full.reshape(r, N_HEADS, QK_NOPE + QK_ROPE)
    return jnp.pad(per_head, ((0, 0), (0, 0), (0, HEAD_PAD - QK_NOPE - QK_ROPE))).reshape(r, N_HEADS * HEAD_PAD)


def _unpad_w_uq(padded):
    r = padded.shape[0]
    return padded.reshape(r, N_HEADS, HEAD_PAD)[:, :, :QK_NOPE + QK_ROPE].reshape(r, N_HEADS * (QK_NOPE + QK_ROPE))


SMALL_ROWS = 16


def _pack_small(d, pre_mix, post_mix, pre_mlp, post_mlp, conv_out, attn_out, q_norm, kv_norm, conv_w):
    row = lambda *parts: jnp.pad(jnp.concatenate(parts, axis=1), ((0, 0), (0, d - sum(p.shape[1] for p in parts))))
    rows = [row(pre_mix), row(post_mix), row(pre_mlp), row(post_mlp), row(conv_out, attn_out), row(q_norm, kv_norm),
            row(conv_w[0:1]), row(conv_w[1:2]), row(conv_w[2:3])]
    return jnp.pad(jnp.concatenate(rows, axis=0), ((0, SMALL_ROWS - len(rows)), (0, 0)))


def _unpack_small(p, chip):
    cw = CONV_WIDTH // 4
    conv_w = lax.dynamic_slice(p[6:9, :CONV_WIDTH], (0, chip * cw), (3, cw))
    return dict(pre_mix_g=p[0:1], post_mix_g=p[1:2], pre_mlp_g=p[2:3], post_mlp_g=p[3:4],
                conv_out_g=p[4:5, :CONV_WIDTH], attn_out_g=p[4:5, CONV_WIDTH:2 * CONV_WIDTH],
                q_norm_g=p[5:6, :Q_RANK], kv_norm_g=p[5:6, Q_RANK:Q_RANK + KV_RANK], conv_w=conv_w[None])


def kernel(x, pre_mix_g, w_in, conv_w, q_norm_g, w_uq, kv_norm_g, w_ukv, conv_out_g, attn_out_g, w_o, post_mix_g, pre_mlp_g, w_up, w_down, post_mlp_g, loss_target, m_pre_mix_g, m_w_in, m_conv_w, m_q_norm_g, m_w_uq, m_kv_norm_g, m_w_ukv, m_conv_out_g, m_attn_out_g, m_w_o, m_post_mix_g, m_pre_mlp_g, m_w_up, m_w_down, m_post_mlp_g, v_pre_mix_g, v_w_in, v_conv_w, v_q_norm_g, v_w_uq, v_kv_norm_g, v_w_ukv, v_conv_out_g, v_attn_out_g, v_w_o, v_post_mix_g, v_pre_mlp_g, v_w_up, v_w_down, v_post_mlp_g):
    bf16 = jnp.bfloat16
    s, d = x.shape[1], x.shape[2]
    d_ff = 4 * d
    chip = 2 * lax.axis_index("x") + lax.axis_index("y")
    xs = x.reshape(s, d)
    target = loss_target.reshape(s, d)
    tm = min(256, s)
    t_attn = min(512, s)
    mt = min(1024, s)
    kt = min(2048, s)

    big = dict(w_in=w_in[0], w_uq=w_uq[0], w_ukv=w_ukv[0], w_o=w_o[0], w_up=w_up[0], w_down=w_down[0])
    names = list(big)
    pos = jnp.stack([lax.axis_index("c"), chip]).astype(jnp.int32)
    wb = {k: _Buf(_cast_into_slot(big[k], pos, name="cast_" + k)) for k in names}
    half = {k: big[k].shape[0] // 2 for k in names}

    def rows(k, a, b):
        lo, n = a * half[k] // 64, (b - a) * half[k] // 64
        assert lo % 16 == 0 and n % 16 == 0 and n > 0, (k, a, b)
        return lo, n

    ici = lambda k, a=0, b=64: _GatherIci(wb[k], *rows(k, a, b))
    d2d = lambda k, a=0, b=64: _GatherD2d(wb[k], *rows(k, a, b))
    _comm("gather_w_in", [[ici("w_in")], [d2d("w_in")]])
    win = _pad_w_in(_cols_from_shards(wb["w_in"].arr))
    ff4 = d_ff // 4

    spread = lambda a: lax.dynamic_update_slice(jnp.zeros((3, CONV_WIDTH), jnp.float32), a[0],
                                                (0, chip * (CONV_WIDTH // 4)))
    conv_w_mine = jnp.where(lax.axis_index("c") == 0, spread(conv_w), 0.0)
    conv_w_full = _all_reduce_small(jnp.pad(conv_w_mine, ((0, 5), (0, 0))))[0:3]

    h1 = _rms_fwd(xs, pre_mix_g, width=d, col=0, tm=tm, name="rms_pre_mix")
    proj = _matmul(h1, win, dims=_NN, mnk=(s, IN_PAD, d), tiles=(mt, IN_PAD // 3, d), name="mm_proj",
                   jobs=[ici("w_uq"), ici("w_ukv"), ici("w_o")])
    y_conv = _conv_fwd(proj, conv_w_full, conv_out_g,
                       jobs=[d2d("w_uq"), d2d("w_ukv"), d2d("w_o"), ici("w_up", 0, 4)])
    wuq = _pad_w_uq(_cols_from_shards(wb["w_uq"].arr))
    wukv = _cols_from_shards(wb["w_ukv"].arr)
    wo = wb["w_o"].arr.reshape(-1, d)
    cqn = _rms_fwd(proj, q_norm_g, width=Q_RANK, col=COL_CQ // Q_RANK, tm=tm, name="rms_q",
                   jobs=[ici("w_up", 4, 7)])
    ckvn = _rms_fwd(proj, kv_norm_g, width=KV_RANK, col=COL_CKV // KV_RANK, tm=tm, name="rms_kv",
                    jobs=[ici("w_up", 7, 9)])
    q_pad = _matmul(cqn, wuq, dims=_NN, mnk=(s, N_HEADS * HEAD_PAD, Q_RANK), tiles=(mt, 1024, Q_RANK), name="mm_q",
                    jobs=[ici("w_up", 9, 13), d2d("w_up", 0, 9)])
    kv = _matmul(ckvn, wukv, dims=_NN, mnk=(s, N_HEADS * HEAD_PAD, KV_RANK), tiles=(mt, 1024, KV_RANK),
                 name="mm_kv", out_dtype=bf16, jobs=[ici("w_up", 13, 16)])
    cos, sin = _rope_tables(s)
    q_rot, kr_rot = _qk_rope_fwd(q_pad, proj, cos, sin, tm=tm, jobs=[ici("w_up", 16, 21), d2d("w_up", 9, 16)])
    o, lse, y_attn = _attn_fwd(q_rot, kv, kr_rot, attn_out_g, t=t_attn,
                               jobs=[ici("w_up", 21, 61), d2d("w_up", 16, 21)])
    ycat = jnp.concatenate([y_conv, y_attn], axis=1)
    mix = _matmul(ycat, wo, dims=_NN, mnk=(s, d, 2 * CONV_WIDTH), tiles=(mt, 1024, 2 * CONV_WIDTH), name="mm_out",
                  jobs=[ici("w_up", 61, 64), ici("w_down", 0, 8), d2d("w_up", 21, 61)])
    x2, h2 = _mix_residual_fwd(xs, mix, post_mix_g, pre_mlp_g, tm=tm,
                               jobs=[ici("w_down", 8, 15), d2d("w_up", 61, 64), d2d("w_down", 0, 8)])
    wup = wb["w_up"].arr

    def up_epilogue(acc, extra_refs, out_refs):
        r = jnp.maximum(acc, 0.0)
        out_refs[0][...] = acc.astype(bf16)
        out_refs[1][...] = (r * r).astype(bf16)

    n_ff = ff4 // 1024
    act = jax.ShapeDtypeStruct((s, d_ff), bf16)
    up, act_sq = _matmul(
        h2, wup, dims=_NN, mnk=(s, d_ff, d), tiles=(mt, 1024, d), name="mm_up",
        b_spec=pl.BlockSpec((None, d, 1024), lambda i, j, l: (j // n_ff, l, j % n_ff)),
        out_shape=(act, act), o_spec=(pl.BlockSpec((mt, 1024), lambda i, j, l: (i, j)),) * 2, epilogue=up_epilogue,
        jobs=[ici("w_down", 15, 44), d2d("w_down", 8, 15)])
    _comm("gather_w_down_tail", [[ici("w_down", 44, 64)], [d2d("w_down", 15, 64)]])
    wdown = wb["w_down"].arr.reshape(d_ff, d)
    mlp = _matmul(act_sq, wdown, dims=_NN, mnk=(s, d, d_ff), tiles=(mt, 1024, 2048), name="mm_down")
    dx3, dmlp, dg_post_mlp, loss_part = _loss_head(x2, mlp, target, post_mlp_g, tm=tm)

    def dup_epilogue(acc, extra_refs, out_refs):
        out_refs[0][...] = (acc * (2.0 * jnp.maximum(extra_refs[0][...].astype(jnp.float32), 0.0))).astype(bf16)

    dup = _matmul(dmlp, wdown, dims=_NT, mnk=(s, d_ff, d), tiles=(mt, 1024, d), name="mm_dact",
                  out_dtype=bf16, epilogue=dup_epilogue, extra=(up,),
                  extra_specs=(pl.BlockSpec((mt, 1024), lambda i, j, l: (i, j)),))
    grads, theirs, pair_sums, by_source, whole = {}, {}, {}, {}, {}

    def exchange(k, g):
        grads[k] = g
        theirs[k] = _Buf(jax.ShapeDtypeStruct((4, g.shape[1] // 2, g.shape[2]), bf16))
        return _PairExchange(g, theirs[k])

    def pair_sum(k):
        pair_sums[k] = _pair_add(grads[k], theirs[k].arr, pos, name="pair_add_" + k)
        by_source[k] = _Buf(jax.ShapeDtypeStruct(pair_sums[k].shape, bf16))

    scatter = lambda k, a=0, b=64: _ScatterIci(pair_sums[k], by_source[k], *rows(k, a, b))

    def share(k):
        whole[k] = _Buf(_chip_sum(by_source[k].arr, pair_sums[k], pos, name="chip_sum_" + k))
        return _PairShare(whole[k])

    g_wdown = _matmul(act_sq, dmlp, dims=_TN, mnk=(d_ff, d, s), tiles=(1024, 1024, kt), name="mm_gw_down",
                      out_dtype=bf16).reshape(4, ff4, d)
    dh2 = _matmul(dup, wup, dims=_NT, mnk=(s, d, d_ff), tiles=(mt, 1024, ff4), name="mm_dh2",
                  b_spec=pl.BlockSpec((None, 1024, ff4), lambda i, j, l: (l, j, 0)),
                  jobs=[exchange("w_down", g_wdown)])
    pair_sum("w_down")
    g_wup = _matmul(h2, dup, dims=_TN, mnk=(d, d_ff, s), tiles=(1024, 1024, kt), name="mm_gw_up",
                    out_shape=jax.ShapeDtypeStruct((4, d, ff4), bf16),
                    o_spec=pl.BlockSpec((None, 1024, 1024), lambda i, j, l: (j // n_ff, i, j % n_ff)),
                    jobs=[scatter("w_down", 0, 26)])
    dx2, dmix, dg_pre_mlp, dg_post_mix = _mix_residual_bwd(
        dx3, dh2, x2, mix, pre_mlp_g, post_mix_g, tm=tm, jobs=[exchange("w_up", g_wup), scatter("w_down", 26, 37)])
    pair_sum("w_up")

    dycat = _matmul(dmix, wo, dims=_NT, mnk=(s, 2 * CONV_WIDTH, d), tiles=(mt, 1024, d), name="mm_dycat",
                    jobs=[scatter("w_down", 37, 48)])
    g_wo = _matmul(ycat, dmix, dims=_TN, mnk=(2 * CONV_WIDTH, d, s), tiles=(1024, 1024, kt),
                   name="mm_gw_out", out_dtype=bf16, jobs=[scatter("w_down", 48, 58)]).reshape(4, CONV_WIDTH // 2, d)
    du, dgb, dgc, dg_conv_w, dg_conv_out = _conv_bwd(proj, conv_w_full, conv_out_g, dycat,
                                                     jobs=[exchange("w_o", g_wo), scatter("w_down", 58, 64)])
    pair_sum("w_o")
    do, delta, dg_attn_out = _attn_norm_bwd(o, attn_out_g, dycat, jobs=[scatter("w_up", 0, 4)])
    dq_pad, dk_pad, dv = _attn_bwd(q_rot, kv, kr_rot, do, lse, delta, t=t_attn,
                                   jobs=[scatter("w_up", 4, 40), share("w_down")])
    dq_raw, dkv, dkr = _qk_rope_bwd(dq_pad, dk_pad, dv, cos, sin, tm=tm, jobs=[scatter("w_up", 40, 47)])
    wq_cols = N_HEADS * HEAD_PAD
    g_wuq = _matmul(cqn, dq_raw, dims=_TN, mnk=(Q_RANK, wq_cols, s), tiles=(Q_RANK, 1024, kt),
                    name="mm_gw_uq", out_dtype=bf16, jobs=[scatter("w_up", 47, 51)])
    dcqn = _matmul(dq_raw, wuq, dims=_NT, mnk=(s, Q_RANK, wq_cols), tiles=(mt, Q_RANK, wq_cols), name="mm_dcq",
                   jobs=[scatter("w_up", 51, 55)])
    g_wukv = _matmul(ckvn, dkv, dims=_TN, mnk=(KV_RANK, wq_cols, s), tiles=(KV_RANK, 1024, kt),
                     name="mm_gw_ukv", out_dtype=bf16, jobs=[scatter("w_up", 55, 58)])
    dckvn = _matmul(dkv, wukv, dims=_NT, mnk=(s, KV_RANK, wq_cols), tiles=(mt, KV_RANK, wq_cols), name="mm_dckv",
                    jobs=[scatter("w_up", 58, 61), exchange("w_uq", _cols_to_shards(_unpad_w_uq(g_wuq))),
                          exchange("w_ukv", _cols_to_shards(g_wukv))])
    pair_sum("w_uq")
    pair_sum("w_ukv")
    dcq, dg_q_norm = _rms_bwd(proj, q_norm_g, dcqn, width=Q_RANK, col=COL_CQ // Q_RANK, tm=tm, name="rms_q_bwd",
                              jobs=[scatter("w_up", 61, 64)])
    dckv, dg_kv_norm = _rms_bwd(proj, kv_norm_g, dckvn, width=KV_RANK, col=COL_CKV // KV_RANK, tm=tm,
                                name="rms_kv_bwd", jobs=[scatter("w_o", 0, 16)])
    dproj = jnp.concatenate([du, dgb, dgc, dcq, dkr, jnp.zeros((s, COL_CKV - COL_KR - 128), bf16), dckv], axis=1)
    g_win = _matmul(h1, dproj, dims=_TN, mnk=(d, IN_PAD, s), tiles=(1024, IN_PAD // 3, kt), name="mm_gw_in",
                    out_dtype=bf16, jobs=[scatter("w_o", 16, 64), scatter("w_uq"), share("w_up")])
    _comm("pair_exchange_w_in", [[exchange("w_in", _cols_to_shards(_unpad_w_in(g_win)))]])
    pair_sum("w_in")
    dh1 = _matmul(dproj, win, dims=_NT, mnk=(s, d, IN_PAD), tiles=(mt, 1024, IN_PAD // 2), name="mm_dh1",
                  jobs=[scatter("w_ukv"), scatter("w_in", 0, 36), share("w_o"), share("w_uq")])
    grad_x, dg_pre_mix = _input_bwd(dx2, dh1, xs, pre_mix_g, tm=tm, jobs=[scatter("w_in", 36, 64)])
    _comm("pair_share_w_ukv_w_in", [[share("w_ukv"), share("w_in")]])

    moments = dict(w_in=(m_w_in, v_w_in), w_uq=(m_w_uq, v_w_uq), w_ukv=(m_w_ukv, v_w_ukv), w_o=(m_w_o, v_w_o),
                   w_up=(m_w_up, v_w_up), w_down=(m_w_down, v_w_down))
    gw, dw, nm, nv = {}, {}, {}, {}
    for k in names:
        delta_k, nm_k, nv_k, g = _adamw(big[k], whole[k].arr, moments[k][0][0], moments[k][1][0], name="adamw_" + k)
        gw[k], dw[k], nm[k], nv[k] = g[None], delta_k[None], nm_k[None], nv_k[None]

    small_g = _all_reduce_small(_pack_small(d, dg_pre_mix, dg_post_mix, dg_pre_mlp, dg_post_mlp, dg_conv_out,
                                            dg_attn_out, dg_q_norm, dg_kv_norm, dg_conv_w))
    pack_w = lambda cw, pre_mix, post_mix, pre_mlp, post_mlp, conv_out, attn_out, q_norm, kv_norm: _pack_small(
        d, pre_mix, post_mix, pre_mlp, post_mlp, conv_out, attn_out, q_norm, kv_norm, cw)
    small_w = pack_w(conv_w_full, pre_mix_g, post_mix_g, pre_mlp_g, post_mlp_g, conv_out_g, attn_out_g, q_norm_g,
                     kv_norm_g)
    small_m = pack_w(spread(m_conv_w), m_pre_mix_g, m_post_mix_g, m_pre_mlp_g, m_post_mlp_g, m_conv_out_g,
                     m_attn_out_g, m_q_norm_g, m_kv_norm_g)
    small_v = pack_w(spread(v_conv_w), v_pre_mix_g, v_post_mix_g, v_pre_mlp_g, v_post_mlp_g, v_conv_out_g,
                     v_attn_out_g, v_q_norm_g, v_kv_norm_g)
    small_d, small_nm, small_nv, small_g = _adamw(small_w, small_g, small_m, small_v, name="adamw_small")
    sg, sd, snm, snv = (_unpack_small(p, chip) for p in (small_g, small_d, small_nm, small_nv))

    for src, dst in ((sg, gw), (sd, dw), (snm, nm), (snv, nv)):
        dst.update(src)

    loss = lax.psum(loss_part[0, 0], ("x", "y", "c"))
    order = ["pre_mix_g", "w_in", "conv_w", "q_norm_g", "w_uq", "kv_norm_g", "w_ukv", "conv_out_g", "attn_out_g",
             "w_o", "post_mix_g", "pre_mlp_g", "w_up", "w_down", "post_mlp_g"]
    return (loss, grad_x.reshape(1, s, d), *[gw[k] for k in order], *[dw[k] for k in order],
            *[nm[k] for k in order], *[nv[k] for k in order])
```

```python
import functools

import jax
import jax.numpy as jnp
from jax import lax
from jax.experimental import pallas as pl
from jax.experimental.pallas import tpu as pltpu

EPS = 1e-6
NEG_INF = -1e30
CHUNK_SHIFT = 6
N_HEADS = 8
HEAD_PAD = 256
QK_NOPE = 128
QK_ROPE = 64
V_DIM = 128
CONV_WIDTH = 1024
Q_RANK = 768
KV_RANK = 512
ROPE_THETA = 10000.0
ATTN_SCALE = (QK_NOPE + QK_ROPE) ** -0.5
ADAM_LR, ADAM_B1, ADAM_B2, ADAM_EPS, ADAM_WD, ADAM_STEP = 0.001, 0.9, 0.999, 1e-08, 0.01, 10

COL_CQ = 3 * CONV_WIDTH
COL_KR = COL_CQ + Q_RANK
COL_CKV = 4096
IN_PAD = COL_CKV + KV_RANK
IN_WIDTH = 3 * CONV_WIDTH + Q_RANK + KV_RANK + QK_ROPE

VMEM_LIMIT_BYTES = 56 * 1024 * 1024
MESH = pl.DeviceIdType.MESH
ANY = pl.BlockSpec(memory_space=pl.ANY)

_NN = (((1,), (0,)), ((), ()))
_NT = (((1,), (1,)), ((), ()))
_TN = (((0,), (0,)), ((), ()))


def _params(sem):
    return pltpu.CompilerParams(dimension_semantics=sem, vmem_limit_bytes=VMEM_LIMIT_BYTES)


class _Buf:
    def __init__(self, arr):
        self.arr = arr


def _position():
    return lax.axis_index("x"), lax.axis_index("y"), lax.axis_index("c")


def _other_chips(x, y):
    return [(2 * (1 - x) + y, (1 - x, y)), (2 * x + (1 - y), (x, 1 - y)), (2 * (1 - x) + (1 - y), (1 - x, 1 - y))]


def _remote(src, dst, sems, k, to):
    send, recv, off = sems
    return pltpu.make_async_remote_copy(src_ref=src, dst_ref=dst, send_sem=send.at[off + k], recv_sem=recv.at[off + k],
                                        device_id=to, device_id_type=MESH)


class _GatherIci:
    n_sems = 3

    def __init__(self, buf, lo, n):
        self.buf, self.lo, self.n, self.buffers, self.sources = buf, lo, n, [buf], []

    def _rows(self, ref, slot, which):
        return ref.at[slot, pl.ds(which * (ref.shape[1] // 2) + self.lo, self.n), :]

    def start(self, refs, sems):
        ref = refs[id(self.buf)]
        x, y, c = _position()
        mine = self._rows(ref, 2 * x + y, c)
        for k, (_, xy) in enumerate(_other_chips(x, y)):
            _remote(mine, mine, sems, k, (*xy, c)).start()

    def wait(self, refs, sems):
        ref = refs[id(self.buf)]
        x, y, c = _position()
        mine = self._rows(ref, 2 * x + y, c)
        for k, (slot, xy) in enumerate(_other_chips(x, y)):
            landed = self._rows(ref, slot, c)
            _remote(landed, landed, sems, k, (*xy, c)).wait_recv()
            _remote(mine, mine, sems, k, (*xy, c)).wait_send()


class _GatherD2d(_GatherIci):
    def start(self, refs, sems):
        ref = refs[id(self.buf)]
        x, y, c = _position()
        for k, (slot, _) in enumerate(_other_chips(x, y)):
            rows = self._rows(ref, slot, c)
            _remote(rows, rows, sems, k, (x, y, 1 - c)).start()

    def wait(self, refs, sems):
        ref = refs[id(self.buf)]
        x, y, c = _position()
        for k, (slot, _) in enumerate(_other_chips(x, y)):
            sent, landed = self._rows(ref, slot, c), self._rows(ref, slot, 1 - c)
            _remote(landed, landed, sems, k, (x, y, 1 - c)).wait_recv()
            _remote(sent, sent, sems, k, (x, y, 1 - c)).wait_send()


class _ScatterIci:
    n_sems = 3

    def __init__(self, src, dst, lo, n):
        self.src, self.dst, self.lo, self.n, self.buffers, self.sources = src, dst, lo, n, [dst], [src]

    def _rows(self, ref, slot):
        return ref.at[slot, pl.ds(self.lo, self.n), :]

    def start(self, refs, sems):
        src, dst = refs[id(self.src)], refs[id(self.dst)]
        x, y, c = _position()
        for k, (slot, xy) in enumerate(_other_chips(x, y)):
            _remote(self._rows(src, slot), self._rows(dst, 2 * x + y), sems, k, (*xy, c)).start()

    def wait(self, refs, sems):
        src, dst = refs[id(self.src)], refs[id(self.dst)]
        x, y, c = _position()
        for k, (slot, xy) in enumerate(_other_chips(x, y)):
            _remote(self._rows(src, slot), self._rows(dst, slot), sems, k, (*xy, c)).wait_recv()
            _remote(self._rows(src, slot), self._rows(dst, slot), sems, k, (*xy, c)).wait_send()


class _PairExchange:
    n_sems = 1

    def __init__(self, src, dst):
        self.src, self.dst, self.buffers, self.sources = src, dst, [dst], [src]

    def _copy(self, refs, sems):
        src, dst = refs[id(self.src)], refs[id(self.dst)]
        x, y, c = _position()
        h = src.shape[1] // 2
        return _remote(src.at[:, pl.ds((1 - c) * h, h), :], dst, sems, 0, (x, y, 1 - c))

    def start(self, refs, sems):
        self._copy(refs, sems).start()

    def wait(self, refs, sems):
        self._copy(refs, sems).wait()


class _PairShare:
    n_sems = 1

    def __init__(self, buf):
        self.buf, self.buffers, self.sources = buf, [buf], []

    def _rows(self, ref, which):
        h = ref.shape[0] // 2
        return ref.at[pl.ds(which * h, h), :]

    def start(self, refs, sems):
        ref = refs[id(self.buf)]
        x, y, c = _position()
        _remote(self._rows(ref, c), self._rows(ref, c), sems, 0, (x, y, 1 - c)).start()

    def wait(self, refs, sems):
        ref = refs[id(self.buf)]
        x, y, c = _position()
        _remote(self._rows(ref, c), self._rows(ref, c), sems, 0, (x, y, 1 - c)).wait_send()
        _remote(self._rows(ref, 1 - c), self._rows(ref, 1 - c), sems, 0, (x, y, 1 - c)).wait_recv()


def _unique(items):
    seen, out = set(), []
    for it in items:
        if id(it) not in seen:
            seen.add(id(it))
            out.append(it)
    return out


def _job_operands(jobs):
    sources = _unique([a for j in jobs for a in j.sources])
    buffers = _unique([b for j in jobs for b in j.buffers])
    held = [b for b in buffers if not isinstance(b.arr, jax.ShapeDtypeStruct)]
    fresh = [b for b in buffers if isinstance(b.arr, jax.ShapeDtypeStruct)]
    return sources, held, fresh


def _sem_offsets(jobs):
    offs, total = [], 0
    for j in jobs:
        offs.append(total)
        total += j.n_sems
    return offs, total


def _call(body, *, name, grid, in_specs, out_specs, out_shape, args, semantics, scratch_shapes=(), jobs=(),
          prefetch=None):
    n_pre = 0 if prefetch is None else 1

    def launch(fn, in_specs, out_specs, scratch, **kw):
        if prefetch is None:
            return pl.pallas_call(fn, name=name, grid=grid, in_specs=in_specs, out_specs=out_specs,
                                  scratch_shapes=scratch, **kw)
        return pl.pallas_call(fn, name=name, grid_spec=pltpu.PrefetchScalarGridSpec(
            num_scalar_prefetch=1, grid=grid, in_specs=in_specs, out_specs=out_specs, scratch_shapes=scratch), **kw)

    pre = () if prefetch is None else (prefetch,)
    if not jobs:
        return launch(body, list(in_specs), list(out_specs), list(scratch_shapes), out_shape=list(out_shape),
                      compiler_params=_params(semantics))(*pre, *args)
    sources, held, fresh = _job_operands(jobs)
    offs, n_sem = _sem_offsets(jobs)
    n_in, n_out, n_scr = len(in_specs), len(out_specs), len(scratch_shapes)
    n_src, n_held, n_fresh = len(sources), len(held), len(fresh)

    def carried(*refs):
        pre_refs, refs = refs[:n_pre], refs[n_pre:]
        ins = refs[:n_in]
        src_refs = refs[n_in:n_in + n_src]
        o0 = n_in + n_src + n_held
        outs = refs[o0:o0 + n_out]
        buf_refs = refs[o0 + n_out:o0 + n_out + n_held + n_fresh]
        s0 = o0 + n_out + n_held + n_fresh
        scratch = refs[s0:s0 + n_scr]
        send, recv = refs[s0 + n_scr], refs[s0 + n_scr + 1]
        where = {id(a): r for a, r in zip(sources, src_refs)}
        where.update({id(b): r for b, r in zip(held + fresh, buf_refs)})
        ids = [pl.program_id(a) for a in range(len(grid))]
        first = functools.reduce(jnp.logical_and, [i == 0 for i in ids])
        last = functools.reduce(jnp.logical_and, [i == g - 1 for i, g in zip(ids, grid)])

        @pl.when(first)
        def _():
            for j, off in zip(jobs, offs):
                j.start(where, (send, recv, off))

        body(*pre_refs, *ins, *outs, *scratch)

        @pl.when(last)
        def _():
            for j, off in zip(jobs, offs):
                j.wait(where, (send, recv, off))

    shape_of = lambda b: jax.ShapeDtypeStruct(b.arr.shape, b.arr.dtype)
    res = launch(
        carried, [*in_specs, *[ANY] * (n_src + n_held)], [*out_specs, *[ANY] * (n_held + n_fresh)],
        [*scratch_shapes, pltpu.SemaphoreType.DMA((n_sem,)), pltpu.SemaphoreType.DMA((n_sem,))],
        out_shape=[*out_shape, *[shape_of(b) for b in held + fresh]],
        input_output_aliases={n_pre + n_in + n_src + i: n_out + i for i in range(n_held)},
        compiler_params=pltpu.CompilerParams(dimension_semantics=("arbitrary",) * len(grid),
                                             vmem_limit_bytes=VMEM_LIMIT_BYTES, has_side_effects=True),
    )(*pre, *args, *sources, *[b.arr for b in held])
    for b, new in zip(held + fresh, res[n_out:]):
        b.arr = new
    return list(res[:n_out])


def _comm(name, phases):
    jobs = [j for ph in phases for j in ph]
    sources, held, fresh = _job_operands(jobs)
    offs, n_sem = _sem_offsets(jobs)
    off_of = {id(j): o for j, o in zip(jobs, offs)}
    n_src, n_held, n_fresh = len(sources), len(held), len(fresh)

    def body(*refs):
        src_refs = refs[:n_src]
        buf_refs = refs[n_src + n_held:n_src + 2 * n_held + n_fresh]
        send, recv = refs[-2], refs[-1]
        where = {id(a): r for a, r in zip(sources, src_refs)}
        where.update({id(b): r for b, r in zip(held + fresh, buf_refs)})
        for ph in phases:
            for j in ph:
                j.start(where, (send, recv, off_of[id(j)]))
            for j in ph:
                j.wait(where, (send, recv, off_of[id(j)]))

    shape_of = lambda b: jax.ShapeDtypeStruct(b.arr.shape, b.arr.dtype)
    res = pl.pallas_call(
        body, name=name,
        in_specs=[ANY] * (n_src + n_held), out_specs=[ANY] * (n_held + n_fresh),
        out_shape=[shape_of(b) for b in held + fresh],
        input_output_aliases={n_src + i: i for i in range(n_held)},
        scratch_shapes=[pltpu.SemaphoreType.DMA((n_sem,)), pltpu.SemaphoreType.DMA((n_sem,))],
        compiler_params=pltpu.CompilerParams(has_side_effects=True),
    )(*sources, *[b.arr for b in held])
    for b, new in zip(held + fresh, res):
        b.arr = new


def _matmul(a, b, *, dims, mnk, tiles, name, out_dtype=jnp.float32, a_spec=None, b_spec=None,
            out_shape=None, o_spec=None, epilogue=None, extra=(), extra_specs=(), jobs=()):
    m, n, k = mnk
    tm, tn, tk = tiles
    assert m % tm == 0 and n % tn == 0 and k % tk == 0, (name, mnk, tiles)
    gm, gn, gk = m // tm, n // tn, k // tk
    if a_spec is None:
        a_spec = (pl.BlockSpec((tk, tm), lambda i, j, l: (l, i)) if dims is _TN
                  else pl.BlockSpec((tm, tk), lambda i, j, l: (i, l)))
    if b_spec is None:
        b_spec = (pl.BlockSpec((tn, tk), lambda i, j, l: (j, l)) if dims is _NT
                  else pl.BlockSpec((tk, tn), lambda i, j, l: (l, j)))
    if out_shape is None:
        out_shape = jax.ShapeDtypeStruct((m, n), out_dtype)
    if o_spec is None:
        o_spec = pl.BlockSpec((tm, tn), lambda i, j, l: (i, j))
    single = not isinstance(out_shape, (tuple, list))
    n_extra = len(extra)

    def finish(acc, extra_refs, out_refs):
        if epilogue is None:
            out_refs[0][...] = acc.astype(out_refs[0].dtype)
        else:
            epilogue(acc, extra_refs, out_refs)

    def body_whole_k(*refs):
        a_ref, b_ref = refs[0], refs[1]
        acc = lax.dot_general(a_ref[...], b_ref[...], dims, preferred_element_type=jnp.float32)
        finish(acc, refs[2:2 + n_extra], refs[2 + n_extra:])

    def body_split_k(*refs):
        a_ref, b_ref = refs[0], refs[1]
        extra_refs = refs[2:2 + n_extra]
        out_refs = refs[2 + n_extra:-1]
        acc_ref = refs[-1]
        step = pl.program_id(2)
        part = lax.dot_general(a_ref[...], b_ref[...], dims, preferred_element_type=jnp.float32)

        @pl.when(step == 0)
        def _():
            acc_ref[...] = part

        @pl.when(jnp.logical_and(step > 0, step < gk - 1))
        def _():
            acc_ref[...] += part

        @pl.when(step == gk - 1)
        def _():
            finish(acc_ref[...] + part, extra_refs, out_refs)

    res = _call(
        body_whole_k if gk == 1 else body_split_k, name=name, grid=(gm, gn, gk),
        in_specs=[a_spec, b_spec, *extra_specs],
        out_specs=[o_spec] if single else list(o_spec),
        out_shape=[out_shape] if single else list(out_shape),
        scratch_shapes=[] if gk == 1 else [pltpu.VMEM((tm, tn), jnp.float32)],
        semantics=("parallel", "parallel", "arbitrary"), args=(a, b, *extra), jobs=jobs)
    return res[0] if single else res


def _rstd(x):
    return lax.rsqrt(jnp.mean(x * x, axis=-1, keepdims=True) + EPS)


def _rms_bwd_rows(x, g, dy):
    r = _rstd(x)
    xn = x * r
    dyg = dy * g
    dx = r * (dyg - xn * jnp.mean(xn * dyg, axis=-1, keepdims=True))
    return dx, dy * xn


def _acc_rows(ref, rows, first):
    part = jnp.sum(rows, axis=0, keepdims=True)

    @pl.when(first)
    def _():
        ref[...] = part

    @pl.when(jnp.logical_not(first))
    def _():
        ref[...] += part


def _rms_fwd(x, g, *, width, col, tm, name, jobs=()):
    s = x.shape[0]

    def body(x_ref, g_ref, o_ref):
        v = x_ref[...]
        o_ref[...] = (v * _rstd(v) * g_ref[...]).astype(o_ref.dtype)

    return _call(
        body, name=name, grid=(s // tm,),
        in_specs=[pl.BlockSpec((tm, width), lambda i: (i, col)), pl.BlockSpec((1, width), lambda i: (0, 0))],
        out_specs=[pl.BlockSpec((tm, width), lambda i: (i, 0))],
        out_shape=[jax.ShapeDtypeStruct((s, width), jnp.bfloat16)],
        semantics=("parallel",), args=(x, g), jobs=jobs)[0]


def _rms_bwd(x, g, dy, *, width, col, tm, name, jobs=()):
    s = x.shape[0]

    def body(x_ref, g_ref, dy_ref, dx_ref, dg_ref):
        dx, dgr = _rms_bwd_rows(x_ref[...], g_ref[...], dy_ref[...])
        dx_ref[...] = dx.astype(dx_ref.dtype)
        _acc_rows(dg_ref, dgr, pl.program_id(0) == 0)

    return _call(
        body, name=name, grid=(s // tm,),
        in_specs=[pl.BlockSpec((tm, width), lambda i: (i, col)), pl.BlockSpec((1, width), lambda i: (0, 0)),
                  pl.BlockSpec((tm, width), lambda i: (i, 0))],
        out_specs=[pl.BlockSpec((tm, width), lambda i: (i, 0)), pl.BlockSpec((1, width), lambda i: (0, 0))],
        out_shape=[jax.ShapeDtypeStruct((s, width), jnp.bfloat16), jax.ShapeDtypeStruct((1, width), jnp.float32)],
        semantics=("arbitrary",), args=(x, g, dy), jobs=jobs)


def _row_specs(tm, d, n):
    return [pl.BlockSpec((tm, d), lambda i: (i, 0)) for _ in range(n)]


def _gain_specs(d, n):
    return [pl.BlockSpec((1, d), lambda i: (0, 0)) for _ in range(n)]


def _mix_residual_fwd(x, mix, g_post_mix, g_pre_mlp, *, tm, jobs=()):
    s, d = x.shape

    def body(x_ref, mix_ref, g1_ref, g2_ref, x2_ref, h2_ref):
        mixv = mix_ref[...]
        x2 = x_ref[...] + mixv * _rstd(mixv) * g1_ref[...]
        x2_ref[...] = x2
        h2_ref[...] = (x2 * _rstd(x2) * g2_ref[...]).astype(h2_ref.dtype)

    return _call(
        body, name="mix_residual_fwd", grid=(s // tm,),
        in_specs=_row_specs(tm, d, 2) + _gain_specs(d, 2),
        out_specs=_row_specs(tm, d, 2),
        out_shape=[jax.ShapeDtypeStruct((s, d), jnp.float32), jax.ShapeDtypeStruct((s, d), jnp.bfloat16)],
        semantics=("parallel",), args=(x, mix, g_post_mix, g_pre_mlp), jobs=jobs)


def _loss_head(x2, mlp, target, g_post_mlp, *, tm, jobs=()):
    s, d = x2.shape

    def body(x2_ref, m_ref, t_ref, g_ref, dx3_ref, dm_ref, dg_ref, loss_ref):
        first = pl.program_id(0) == 0
        mv = m_ref[...]
        g = g_ref[...]
        diff = x2_ref[...] + mv * _rstd(mv) * g - t_ref[...]
        dx3 = diff * (1.0 / d)
        dx3_ref[...] = dx3
        dm, dgr = _rms_bwd_rows(mv, g, dx3)
        dm_ref[...] = dm.astype(dm_ref.dtype)
        _acc_rows(dg_ref, dgr, first)
        part = 0.5 * jnp.sum(jnp.mean(diff * diff, axis=-1, keepdims=True), axis=0, keepdims=True)
        _acc_rows(loss_ref, jnp.broadcast_to(part, (1, 128)), first)

    return _call(
        body, name="loss_head", grid=(s // tm,),
        in_specs=_row_specs(tm, d, 3) + _gain_specs(d, 1),
        out_specs=_row_specs(tm, d, 2) + _gain_specs(d, 1) + [pl.BlockSpec((1, 128), lambda i: (0, 0))],
        out_shape=[jax.ShapeDtypeStruct((s, d), jnp.float32), jax.ShapeDtypeStruct((s, d), jnp.bfloat16),
                   jax.ShapeDtypeStruct((1, d), jnp.float32), jax.ShapeDtypeStruct((1, 128), jnp.float32)],
        semantics=("arbitrary",), args=(x2, mlp, target, g_post_mlp), jobs=jobs)


def _mix_residual_bwd(dx3, dh2, x2, mix, g_pre_mlp, g_post_mix, *, tm, jobs=()):
    s, d = x2.shape

    def body(dx3_ref, dh2_ref, x2_ref, mix_ref, g2_ref, g1_ref, dx2_ref, dmix_ref, dg2_ref, dg1_ref):
        first = pl.program_id(0) == 0
        d_in, dgr2 = _rms_bwd_rows(x2_ref[...], g2_ref[...], dh2_ref[...])
        dx2 = dx3_ref[...] + d_in
        dx2_ref[...] = dx2
        dmix, dgr1 = _rms_bwd_rows(mix_ref[...], g1_ref[...], dx2)
        dmix_ref[...] = dmix.astype(dmix_ref.dtype)
        _acc_rows(dg2_ref, dgr2, first)
        _acc_rows(dg1_ref, dgr1, first)

    return _call(
        body, name="mix_residual_bwd", grid=(s // tm,),
        in_specs=_row_specs(tm, d, 4) + _gain_specs(d, 2),
        out_specs=_row_specs(tm, d, 2) + _gain_specs(d, 2),
        out_shape=[jax.ShapeDtypeStruct((s, d), jnp.float32), jax.ShapeDtypeStruct((s, d), jnp.bfloat16),
                   jax.ShapeDtypeStruct((1, d), jnp.float32), jax.ShapeDtypeStruct((1, d), jnp.float32)],
        semantics=("arbitrary",), args=(dx3, dh2, x2, mix, g_pre_mlp, g_post_mix), jobs=jobs)


def _input_bwd(dx2, dh1, x, g_pre_mix, *, tm, jobs=()):
    s, d = x.shape

    def body(dx2_ref, dh1_ref, x_ref, g_ref, dx_ref, dg_ref):
        d_in, dgr = _rms_bwd_rows(x_ref[...], g_ref[...], dh1_ref[...])
        dx_ref[...] = dx2_ref[...] + d_in
        _acc_rows(dg_ref, dgr, pl.program_id(0) == 0)

    return _call(
        body, name="input_bwd", grid=(s // tm,),
        in_specs=_row_specs(tm, d, 3) + _gain_specs(d, 1),
        out_specs=_row_specs(tm, d, 1) + _gain_specs(d, 1),
        out_shape=[jax.ShapeDtypeStruct((s, d), jnp.float32), jax.ShapeDtypeStruct((1, d), jnp.float32)],
        semantics=("arbitrary",), args=(dx2, dh1, x, g_pre_mix), jobs=jobs)


def _shift_rows(z, by):
    s = z.shape[0]
    rows = lax.broadcasted_iota(jnp.int32, z.shape, 0)
    rolled = pltpu.roll(z, by % s, axis=0)
    keep = rows >= by if by > 0 else rows < s + by
    return jnp.where(keep, rolled, 0.0)


def _conv_fwd(proj, conv_w, conv_out_g, jobs=()):
    s = proj.shape[0]
    groups = CONV_WIDTH // 128

    def body(u_ref, gb_ref, gc_ref, w_ref, g_ref, y_ref):
        z = gc_ref[...] * u_ref[...]
        w = w_ref[...]
        conv = w[0:1, :] * _shift_rows(z, 2) + w[1:2, :] * _shift_rows(z, 1) + w[2:3, :] * z
        y = gb_ref[...] * conv
        y_ref[...] = (y * _rstd(y) * g_ref[...]).astype(y_ref.dtype)

    col = lambda base: pl.BlockSpec((s, 128), lambda j: (0, base + j))
    return _call(
        body, name="conv_fwd", grid=(groups,),
        in_specs=[col(0), col(groups), col(2 * groups), pl.BlockSpec((3, 128), lambda j: (0, j)),
                  pl.BlockSpec((1, 128), lambda j: (0, j))],
        out_specs=[pl.BlockSpec((s, 128), lambda j: (0, j))],
        out_shape=[jax.ShapeDtypeStruct((s, CONV_WIDTH), jnp.bfloat16)],
        semantics=("parallel",), args=(proj, proj, proj, conv_w, conv_out_g), jobs=jobs)[0]


def _conv_bwd(proj, conv_w, conv_out_g, dycat, jobs=()):
    s = proj.shape[0]
    groups = CONV_WIDTH // 128

    def body(u_ref, gb_ref, gc_ref, w_ref, g_ref, dy_ref, du_ref, dgb_ref, dgc_ref, dw_ref, dg_ref):
        u, gb, gc = u_ref[...], gb_ref[...], gc_ref[...]
        w = w_ref[...]
        z = gc * u
        z1, z2 = _shift_rows(z, 1), _shift_rows(z, 2)
        conv = w[0:1, :] * z2 + w[1:2, :] * z1 + w[2:3, :] * z
        dyr, dgr = _rms_bwd_rows(gb * conv, g_ref[...], dy_ref[...])
        dg_ref[...] = jnp.sum(dgr, axis=0, keepdims=True)
        dgb_ref[...] = (dyr * conv).astype(dgb_ref.dtype)
        dconv = dyr * gb
        dw_ref[0:1, :] = jnp.sum(dconv * z2, axis=0, keepdims=True)
        dw_ref[1:2, :] = jnp.sum(dconv * z1, axis=0, keepdims=True)
        dw_ref[2:3, :] = jnp.sum(dconv * z, axis=0, keepdims=True)
        dz = w[2:3, :] * dconv + w[1:2, :] * _shift_rows(dconv, -1) + w[0:1, :] * _shift_rows(dconv, -2)
        dgc_ref[...] = (dz * u).astype(dgc_ref.dtype)
        du_ref[...] = (dz * gc).astype(du_ref.dtype)

    col = lambda base: pl.BlockSpec((s, 128), lambda j: (0, base + j))
    act = jax.ShapeDtypeStruct((s, CONV_WIDTH), jnp.bfloat16)
    return _call(
        body, name="conv_bwd", grid=(groups,),
        in_specs=[col(0), col(groups), col(2 * groups), pl.BlockSpec((3, 128), lambda j: (0, j)),
                  pl.BlockSpec((1, 128), lambda j: (0, j)), col(0)],
        out_specs=[col(0), col(0), col(0), pl.BlockSpec((3, 128), lambda j: (0, j)),
                   pl.BlockSpec((1, 128), lambda j: (0, j))],
        out_shape=[act, act, act, jax.ShapeDtypeStruct((3, CONV_WIDTH), jnp.float32),
                   jax.ShapeDtypeStruct((1, CONV_WIDTH), jnp.float32)],
        semantics=("parallel",), args=(proj, proj, proj, conv_w, conv_out_g, dycat), jobs=jobs)


def _rope_tables(s):
    pos = jnp.arange(s, dtype=jnp.float32)
    inv_freq = jnp.power(ROPE_THETA, -jnp.arange(0, QK_ROPE, 2, dtype=jnp.float32) / QK_ROPE)
    ang = pos[:, None] * inv_freq[None, :]
    cos, sin = jnp.cos(ang), jnp.sin(ang)
    zeros = jnp.zeros((s, 128 - QK_ROPE), jnp.float32)
    return (jnp.concatenate([cos, cos, zeros], axis=1), jnp.concatenate([-sin, sin, zeros], axis=1))


def _swap_halves(x):
    lane = lax.broadcasted_iota(jnp.int32, x.shape, 1)
    swapped = jnp.where(lane < QK_ROPE // 2, pltpu.roll(x, 128 - QK_ROPE // 2, axis=1),
                        pltpu.roll(x, QK_ROPE // 2, axis=1))
    return jnp.where(lane < QK_ROPE, swapped, 0.0)


def _rope(x, cos, sin):
    return x * cos + _swap_halves(x) * sin


def _rope_transposed(d, cos, sin):
    return d * cos + _swap_halves(d * sin)


def _qk_rope_fwd(q_pad, proj, cos, sin, *, tm, jobs=()):
    s = q_pad.shape[0]
    wq = N_HEADS * HEAD_PAD

    def body(q_ref, kr_ref, cos_ref, sin_ref, qo_ref, kro_ref):
        c, sn = cos_ref[...], sin_ref[...]
        for h in range(N_HEADS):
            lo = h * HEAD_PAD
            qo_ref[:, lo:lo + 128] = q_ref[:, lo:lo + 128].astype(qo_ref.dtype)
            qo_ref[:, lo + 128:lo + 256] = _rope(q_ref[:, lo + 128:lo + 256], c, sn).astype(qo_ref.dtype)
        kro_ref[...] = _rope(kr_ref[...], c, sn).astype(kro_ref.dtype)

    return _call(
        body, name="qk_rope_fwd", grid=(s // tm,),
        in_specs=[pl.BlockSpec((tm, wq), lambda i: (i, 0)), pl.BlockSpec((tm, 128), lambda i: (i, COL_KR // 128)),
                  pl.BlockSpec((tm, 128), lambda i: (i, 0)), pl.BlockSpec((tm, 128), lambda i: (i, 0))],
        out_specs=[pl.BlockSpec((tm, wq), lambda i: (i, 0)), pl.BlockSpec((tm, 128), lambda i: (i, 0))],
        out_shape=[jax.ShapeDtypeStruct((s, wq), jnp.bfloat16), jax.ShapeDtypeStruct((s, 128), jnp.bfloat16)],
        semantics=("parallel",), args=(q_pad, proj, cos, sin), jobs=jobs)


def _qk_rope_bwd(dq_pad, dk_pad, dv, cos, sin, *, tm, jobs=()):
    s = dq_pad.shape[0]
    wq = N_HEADS * HEAD_PAD

    def body(dq_ref, dk_ref, dv_ref, cos_ref, sin_ref, dqo_ref, dkv_ref, dkr_ref):
        c, sn = cos_ref[...], sin_ref[...]
        dkr = jnp.zeros((tm, 128), jnp.float32)
        for h in range(N_HEADS):
            lo = h * HEAD_PAD
            dqo_ref[:, lo:lo + 128] = dq_ref[:, lo:lo + 128].astype(dqo_ref.dtype)
            dqo_ref[:, lo + 128:lo + 256] = _rope_transposed(dq_ref[:, lo + 128:lo + 256], c, sn).astype(dqo_ref.dtype)
            dkv_ref[:, lo:lo + 128] = dk_ref[:, lo:lo + 128].astype(dkv_ref.dtype)
            dkv_ref[:, lo + 128:lo + 256] = dv_ref[:, h * V_DIM:(h + 1) * V_DIM].astype(dkv_ref.dtype)
            dkr = dkr + dk_ref[:, lo + 128:lo + 256]
        dkr_ref[...] = _rope_transposed(dkr, c, sn).astype(dkr_ref.dtype)

    return _call(
        body, name="qk_rope_bwd", grid=(s // tm,),
        in_specs=[pl.BlockSpec((tm, wq), lambda i: (i, 0)), pl.BlockSpec((tm, wq), lambda i: (i, 0)),
                  pl.BlockSpec((tm, N_HEADS * V_DIM), lambda i: (i, 0)),
                  pl.BlockSpec((tm, 128), lambda i: (i, 0)), pl.BlockSpec((tm, 128), lambda i: (i, 0))],
        out_specs=[pl.BlockSpec((tm, wq), lambda i: (i, 0)), pl.BlockSpec((tm, wq), lambda i: (i, 0)),
                   pl.BlockSpec((tm, 128), lambda i: (i, 0))],
        out_shape=[jax.ShapeDtypeStruct((s, wq), jnp.bfloat16), jax.ShapeDtypeStruct((s, wq), jnp.bfloat16),
                   jax.ShapeDtypeStruct((s, 128), jnp.bfloat16)],
        semantics=("parallel",), args=(dq_pad, dk_pad, dv, cos, sin), jobs=jobs)


def _visible(q0, k0, t):
    qpos = q0 + lax.broadcasted_iota(jnp.int32, (t, t), 0)
    kpos = k0 + lax.broadcasted_iota(jnp.int32, (t, t), 1)
    return lax.shift_right_logical(kpos, CHUNK_SHIFT) <= lax.shift_right_logical(qpos, CHUNK_SHIFT)


def _attn_fwd(q, kv, kr, attn_out_g, *, t, jobs=()):
    s = q.shape[0]
    nq = s // t

    def body(q_ref, kn_ref, v_ref, kr_ref, g_ref, o_ref, lse_ref, y_ref, kcat_ref):
        i = pl.program_id(1)

        @pl.when(i == 0)
        def _():
            kcat_ref[:, 0:128] = kn_ref[...]
            kcat_ref[:, 128:256] = kr_ref[...]

        qv = q_ref[...]

        def step(j, carry, diagonal):
            m, l, acc = carry
            k = kcat_ref[pl.ds(pl.multiple_of(j * t, t), t), :]
            v = v_ref[pl.ds(pl.multiple_of(j * t, t), t), :]
            sc = lax.dot_general(qv, k, _NT, preferred_element_type=jnp.float32) * ATTN_SCALE
            if diagonal:
                sc = jnp.where(_visible(0, 0, t), sc, NEG_INF)
            m_new = jnp.maximum(m, jnp.max(sc, axis=-1, keepdims=True))
            p = jnp.exp(sc - m_new)
            alpha = jnp.exp(m - m_new)
            l = alpha * l + jnp.sum(p, axis=-1, keepdims=True)
            acc = alpha * acc + lax.dot_general(p.astype(jnp.bfloat16), v, _NN, preferred_element_type=jnp.float32)
            return m_new, l, acc

        init = (jnp.full((t, 1), NEG_INF, jnp.float32), jnp.zeros((t, 1), jnp.float32),
                jnp.zeros((t, V_DIM), jnp.float32))
        before = lax.fori_loop(0, i, functools.partial(step, diagonal=False), init)
        m, l, acc = step(i, before, True)
        o = acc / l
        o_ref[...] = o
        lse_ref[...] = jnp.broadcast_to(m + jnp.log(l), (t, 128))
        y_ref[...] = (o * _rstd(o) * g_ref[...]).astype(y_ref.dtype)

    head_rows = lambda w, f: pl.BlockSpec((s, w), lambda h, i: (0, f(h)))
    blk = pl.BlockSpec((t, 128), lambda h, i: (i, h))
    full = jax.ShapeDtypeStruct((s, N_HEADS * V_DIM), jnp.float32)
    return _call(
        body, name="attn_fwd", grid=(N_HEADS, nq),
        in_specs=[pl.BlockSpec((t, HEAD_PAD), lambda h, i: (i, h)), head_rows(128, lambda h: 2 * h),
                  head_rows(128, lambda h: 2 * h + 1), head_rows(128, lambda h: 0),
                  pl.BlockSpec((1, 128), lambda h, i: (0, h))],
        out_specs=[blk, blk, blk],
        out_shape=[full, full, jax.ShapeDtypeStruct((s, N_HEADS * V_DIM), jnp.bfloat16)],
        scratch_shapes=[pltpu.VMEM((s, HEAD_PAD), jnp.bfloat16)],
        semantics=("arbitrary", "arbitrary"), args=(q, kv, kv, kr, attn_out_g), jobs=jobs)


def _attn_norm_bwd(o, attn_out_g, dycat, jobs=()):
    s = o.shape[0]

    def body(o_ref, g_ref, dy_ref, do_ref, delta_ref, dg_ref):
        ov = o_ref[...]
        do, dgr = _rms_bwd_rows(ov, g_ref[...], dy_ref[...])
        do_ref[...] = do.astype(do_ref.dtype)
        delta_ref[...] = jnp.broadcast_to(jnp.sum(do * ov, axis=-1, keepdims=True), (s, 128))
        dg_ref[...] = jnp.sum(dgr, axis=0, keepdims=True)

    col = lambda base: pl.BlockSpec((s, 128), lambda h: (0, base + h))
    return _call(
        body, name="attn_norm_bwd", grid=(N_HEADS,),
        in_specs=[col(0), pl.BlockSpec((1, 128), lambda h: (0, h)), col(CONV_WIDTH // 128)],
        out_specs=[col(0), col(0), pl.BlockSpec((1, 128), lambda h: (0, h))],
        out_shape=[jax.ShapeDtypeStruct((s, N_HEADS * V_DIM), jnp.bfloat16),
                   jax.ShapeDtypeStruct((s, N_HEADS * V_DIM), jnp.float32),
                   jax.ShapeDtypeStruct((1, N_HEADS * V_DIM), jnp.float32)],
        semantics=("parallel",), args=(o, attn_out_g, dycat), jobs=jobs)


def _attn_bwd(q, kv, kr, do, lse, delta, *, t, jobs=()):
    s = q.shape[0]
    nq = s // t

    def body(q_ref, kn_ref, v_ref, kr_ref, do_ref, lse_ref, delta_ref, dq_ref, dk_ref, dv_ref, kcat_ref):
        kcat_ref[:, 0:128] = kn_ref[...]
        kcat_ref[:, 128:256] = kr_ref[...]
        dq_ref[...] = jnp.zeros_like(dq_ref)
        dk_ref[...] = jnp.zeros_like(dk_ref)
        dv_ref[...] = jnp.zeros_like(dv_ref)

        def kv_step(j, _):
            krows = pl.ds(pl.multiple_of(j * t, t), t)
            k = kcat_ref[krows, :]
            v = v_ref[krows, :]

            def q_step(i, _, diagonal):
                qrows = pl.ds(pl.multiple_of(i * t, t), t)
                qv = q_ref[qrows, :]
                dov = do_ref[qrows, :]
                sc = lax.dot_general(qv, k, _NT, preferred_element_type=jnp.float32) * ATTN_SCALE
                if diagonal:
                    sc = jnp.where(_visible(0, 0, t), sc, NEG_INF)
                p = jnp.exp(sc - lse_ref[qrows, :][:, 0:1])
                dp = lax.dot_general(dov, v, _NT, preferred_element_type=jnp.float32)
                ds = (p * (dp - delta_ref[qrows, :][:, 0:1]) * ATTN_SCALE).astype(jnp.bfloat16)
                dv_ref[krows, :] += lax.dot_general(p.astype(jnp.bfloat16), dov, _TN,
                                                    preferred_element_type=jnp.float32)
                dk_ref[krows, :] += lax.dot_general(ds, qv, _TN, preferred_element_type=jnp.float32)
                dq_ref[qrows, :] += lax.dot_general(ds, k, _NN, preferred_element_type=jnp.float32)
                return 0

            q_step(j, 0, True)
            lax.fori_loop(j + 1, nq, functools.partial(q_step, diagonal=False), 0)
            return 0

        lax.fori_loop(0, nq, kv_step, 0)

    col = lambda w, f: pl.BlockSpec((s, w), lambda h: (0, f(h)))
    return _call(
        body, name="attn_bwd", grid=(N_HEADS,),
        in_specs=[col(HEAD_PAD, lambda h: h), col(128, lambda h: 2 * h), col(128, lambda h: 2 * h + 1),
                  col(128, lambda h: 0), col(128, lambda h: h), col(128, lambda h: h), col(128, lambda h: h)],
        out_specs=[col(HEAD_PAD, lambda h: h), col(HEAD_PAD, lambda h: h), col(128, lambda h: h)],
        out_shape=[jax.ShapeDtypeStruct((s, N_HEADS * HEAD_PAD), jnp.float32),
                   jax.ShapeDtypeStruct((s, N_HEADS * HEAD_PAD), jnp.float32),
                   jax.ShapeDtypeStruct((s, N_HEADS * V_DIM), jnp.float32)],
        scratch_shapes=[pltpu.VMEM((s, HEAD_PAD), jnp.bfloat16)],
        semantics=("parallel",), args=(q, kv, kv, kr, do, lse, delta), jobs=jobs)


def _row_tile(rows):
    for cand in (256, 128, 64, 32, 16, 8):
        if rows % cand == 0:
            return cand
    return rows


def _cast_into_slot(w, pos, *, name, jobs=()):
    r, c = w.shape
    tr = _row_tile(r)

    def body(pos_ref, w_ref, o_ref):
        o_ref[...] = w_ref[...].astype(o_ref.dtype)

    return _call(
        body, name=name, grid=(r // tr,), prefetch=pos,
        in_specs=[pl.BlockSpec((tr, c), lambda i, p: (i, 0))],
        out_specs=[pl.BlockSpec((None, tr, c), lambda i, p: (p[1], i, 0))],
        out_shape=[jax.ShapeDtypeStruct((4, r, c), jnp.bfloat16)],
        semantics=("parallel",), args=(w,), jobs=jobs)[0]


def _pair_add(g, theirs, pos, *, name, jobs=()):
    n, h, c = theirs.shape
    tr = _row_tile(h)
    nb = h // tr

    def body(pos_ref, a_ref, b_ref, o_ref):
        o_ref[...] = (a_ref[...].astype(jnp.float32) + b_ref[...].astype(jnp.float32)).astype(o_ref.dtype)

    spec = pl.BlockSpec((None, tr, c), lambda j, i, p: (j, i, 0))
    return _call(
        body, name=name, grid=(n, nb), prefetch=pos,
        in_specs=[pl.BlockSpec((None, tr, c), lambda j, i, p: (j, i + p[0] * nb, 0)), spec],
        out_specs=[spec], out_shape=[jax.ShapeDtypeStruct(theirs.shape, jnp.bfloat16)],
        semantics=("parallel", "parallel"), args=(g, theirs), jobs=jobs)[0]


def _chip_sum(by_source, pair_sum, pos, *, name):
    n, h, c = by_source.shape
    tr = _row_tile(h)
    nb = h // tr

    def body(pos_ref, p0, p1, p2, p3, own_ref, o_ref):
        own = own_ref[...].astype(jnp.float32)
        term = lambda k, ref: jnp.where(pos_ref[1] == k, own, ref[...].astype(jnp.float32))
        o_ref[...] = ((term(0, p0) + term(1, p1)) + term(2, p2)) + term(3, p3)

    def source(k):
        return pl.BlockSpec((None, tr, c), lambda i, p: (jnp.where(p[1] == k, (k + 1) % 4, k), i, 0))

    return pl.pallas_call(
        body, name=name, out_shape=jax.ShapeDtypeStruct((2 * h, c), jnp.float32),
        grid_spec=pltpu.PrefetchScalarGridSpec(
            num_scalar_prefetch=1, grid=(nb,),
            in_specs=[source(0), source(1), source(2), source(3),
                      pl.BlockSpec((None, tr, c), lambda i, p: (p[1], i, 0))],
            out_specs=pl.BlockSpec((tr, c), lambda i, p: (i + p[0] * nb, 0))),
        compiler_params=_params(("parallel",)),
    )(pos, by_source, by_source, by_source, by_source, pair_sum)


def _adamw(w, g, m, v, *, name, jobs=()):
    r, c = w.shape
    tr = _row_tile(r)

    def body(w_ref, g_ref, m_ref, v_ref, d_ref, mo_ref, vo_ref, go_ref):
        gv = g_ref[...]
        go_ref[...] = gv
        mn = ADAM_B1 * m_ref[...] + (1.0 - ADAM_B1) * gv
        vn = ADAM_B2 * v_ref[...] + (1.0 - ADAM_B2) * (gv * gv)
        m_hat = mn / (1.0 - ADAM_B1 ** ADAM_STEP)
        v_hat = vn / (1.0 - ADAM_B2 ** ADAM_STEP)
        d_ref[...] = -ADAM_LR * (m_hat / (jnp.sqrt(v_hat) + ADAM_EPS) + ADAM_WD * w_ref[...])
        mo_ref[...] = mn
        vo_ref[...] = vn

    spec = pl.BlockSpec((tr, c), lambda i: (i, 0))
    out = jax.ShapeDtypeStruct((r, c), jnp.float32)
    return _call(body, name=name, grid=(r // tr,), in_specs=[spec] * 4, out_specs=[spec] * 4, out_shape=[out] * 4,
                 semantics=("parallel",), args=(w, g, m, v), jobs=jobs)


def _all_reduce_small(block):
    r, c = block.shape

    def body(src_ref, out_ref, stage_ref, send_sems, recv_sems):
        x, y, cc = _position()
        me = 4 * x + 2 * y + cc
        stage_ref[me] = src_ref[...]
        flip = lambda v, on: 1 - v if on else v
        peers = [(flip(x, dx), flip(y, dy), flip(cc, dc)) for dx in (0, 1) for dy in (0, 1) for dc in (0, 1)][1:]
        copies = [pltpu.make_async_remote_copy(
            src_ref=stage_ref.at[me], dst_ref=stage_ref.at[me],
            send_sem=send_sems.at[k], recv_sem=recv_sems.at[k], device_id=peer, device_id_type=MESH)
            for k, peer in enumerate(peers)]
        for cp in copies:
            cp.start()
        for k, (px, py, pc) in enumerate(peers):
            them = 4 * px + 2 * py + pc
            pltpu.make_async_remote_copy(
                src_ref=stage_ref.at[them], dst_ref=stage_ref.at[them],
                send_sem=send_sems.at[k], recv_sem=recv_sems.at[k], device_id=(px, py, pc),
                device_id_type=MESH).wait_recv()
        for cp in copies:
            cp.wait_send()
        total = stage_ref[0]
        for d in range(1, 8):
            total = total + stage_ref[d]
        out_ref[...] = total

    return pl.pallas_call(
        body, name="all_reduce_small",
        in_specs=[pl.BlockSpec(memory_space=pltpu.VMEM)], out_specs=pl.BlockSpec(memory_space=pltpu.VMEM),
        out_shape=jax.ShapeDtypeStruct((r, c), jnp.float32),
        scratch_shapes=[pltpu.VMEM((8, r, c), jnp.float32), pltpu.SemaphoreType.DMA((7,)),
                        pltpu.SemaphoreType.DMA((7,))],
        compiler_params=pltpu.CompilerParams(has_side_effects=True),
    )(block)


def _cols_from_shards(g):
    n, r, c = g.shape
    return jnp.transpose(g, (1, 0, 2)).reshape(r, n * c)


def _cols_to_shards(w, n=4):
    r, c = w.shape
    return jnp.transpose(w.reshape(r, n, c // n), (1, 0, 2))


def _pad_w_in(full):
    d = full.shape[0]
    zeros = jnp.zeros((d, COL_CKV - COL_KR - QK_ROPE), full.dtype)
    return jnp.concatenate([full[:, :COL_KR], full[:, IN_WIDTH - QK_ROPE:], zeros,
                            full[:, COL_KR:COL_KR + KV_RANK]], axis=1)


def _unpad_w_in(padded):
    return jnp.concatenate([padded[:, :COL_KR], padded[:, COL_CKV:COL_CKV + KV_RANK],
                            padded[:, COL_KR:COL_KR + QK_ROPE]], axis=1)


def _pad_w_uq(full):
    r = full.shape[0]
    per_head = full.reshape(r, N_HEADS, QK_NOPE + QK_ROPE)
    return jnp.pad(per_head, ((0, 0), (0, 0), (0, HEAD_PAD - QK_NOPE - QK_ROPE))).reshape(r, N_HEADS * HEAD_PAD)


def _unpad_w_uq(padded):
    r = padded.shape[0]
    return padded.reshape(r, N_HEADS, HEAD_PAD)[:, :, :QK_NOPE + QK_ROPE].reshape(r, N_HEADS * (QK_NOPE + QK_ROPE))


SMALL_ROWS = 16


def _pack_small(d, pre_mix, post_mix, pre_mlp, post_mlp, conv_out, attn_out, q_norm, kv_norm, conv_w):
    row = lambda *parts: jnp.pad(jnp.concatenate(parts, axis=1), ((0, 0), (0, d - sum(p.shape[1] for p in parts))))
    rows = [row(pre_mix), row(post_mix), row(pre_mlp), row(post_mlp), row(conv_out, attn_out), row(q_norm, kv_norm),
            row(conv_w[0:1]), row(conv_w[1:2]), row(conv_w[2:3])]
    return jnp.pad(jnp.concatenate(rows, axis=0), ((0, SMALL_ROWS - len(rows)), (0, 0)))


def _unpack_small(p, chip):
    cw = CONV_WIDTH // 4
    conv_w = lax.dynamic_slice(p[6:9, :CONV_WIDTH], (0, chip * cw), (3, cw))
    return dict(pre_mix_g=p[0:1], post_mix_g=p[1:2], pre_mlp_g=p[2:3], post_mlp_g=p[3:4],
                conv_out_g=p[4:5, :CONV_WIDTH], attn_out_g=p[4:5, CONV_WIDTH:2 * CONV_WIDTH],
                q_norm_g=p[5:6, :Q_RANK], kv_norm_g=p[5:6, Q_RANK:Q_RANK + KV_RANK], conv_w=conv_w[None])


def kernel(x, pre_mix_g, w_in, conv_w, q_norm_g, w_uq, kv_norm_g, w_ukv, conv_out_g, attn_out_g, w_o, post_mix_g, pre_mlp_g, w_up, w_down, post_mlp_g, loss_target, m_pre_mix_g, m_w_in, m_conv_w, m_q_norm_g, m_w_uq, m_kv_norm_g, m_w_ukv, m_conv_out_g, m_attn_out_g, m_w_o, m_post_mix_g, m_pre_mlp_g, m_w_up, m_w_down, m_post_mlp_g, v_pre_mix_g, v_w_in, v_conv_w, v_q_norm_g, v_w_uq, v_kv_norm_g, v_w_ukv, v_conv_out_g, v_attn_out_g, v_w_o, v_post_mix_g, v_pre_mlp_g, v_w_up, v_w_down, v_post_mlp_g):
    bf16 = jnp.bfloat16
    s, d = x.shape[1], x.shape[2]
    d_ff = 4 * d
    chip = 2 * lax.axis_index("x") + lax.axis_index("y")
    xs = x.reshape(s, d)
    target = loss_target.reshape(s, d)
    tm = min(256, s)
    t_attn = min(512, s)
    mt = min(1024, s)
    kt = min(2048, s)

    big = dict(w_in=w_in[0], w_uq=w_uq[0], w_ukv=w_ukv[0], w_o=w_o[0], w_up=w_up[0], w_down=w_down[0])
    names = list(big)
    pos = jnp.stack([lax.axis_index("c"), chip]).astype(jnp.int32)
    wb = {}
    half = {k: big[k].shape[0] // 2 for k in names}

    def rows(k, a, b):
        lo, n = a * half[k] // 64, (b - a) * half[k] // 64
        assert lo % 16 == 0 and n % 16 == 0 and n > 0, (k, a, b)
        return lo, n

    ici = lambda k, a=0, b=64: _GatherIci(wb[k], *rows(k, a, b))
    d2d = lambda k, a=0, b=64: _GatherD2d(wb[k], *rows(k, a, b))
    wb["w_in"] = _Buf(_cast_into_slot(big["w_in"], pos, name="cast_w_in"))
    for k, (a, b) in dict(w_up=(0, 7), w_down=(7, 14), w_o=(14, 16), w_uq=(16, 19), w_ukv=(19, 22)).items():
        wb[k] = _Buf(_cast_into_slot(big[k], pos, name="cast_" + k, jobs=[ici("w_in", a, b)]))
    h1 = _rms_fwd(xs, pre_mix_g, width=d, col=0, tm=tm, name="rms_pre_mix", jobs=[ici("w_in", 22, 29)])
    _comm("gather_w_in", [[ici("w_in", 29, 64)], [d2d("w_in")]])
    win = _pad_w_in(_cols_from_shards(wb["w_in"].arr))
    ff4 = d_ff // 4

    spread = lambda a: lax.dynamic_update_slice(jnp.zeros((3, CONV_WIDTH), jnp.float32), a[0],
                                                (0, chip * (CONV_WIDTH // 4)))
    conv_w_mine = jnp.where(lax.axis_index("c") == 0, spread(conv_w), 0.0)
    conv_w_full = _all_reduce_small(jnp.pad(conv_w_mine, ((0, 5), (0, 0))))[0:3]

    proj = _matmul(h1, win, dims=_NN, mnk=(s, IN_PAD, d), tiles=(mt, IN_PAD // 3, d), name="mm_proj",
                   jobs=[ici("w_uq"), ici("w_ukv"), ici("w_o")])
    y_conv = _conv_fwd(proj, conv_w_full, conv_out_g,
                       jobs=[d2d("w_uq"), d2d("w_ukv"), d2d("w_o"), ici("w_up", 0, 4)])
    wuq = _pad_w_uq(_cols_from_shards(wb["w_uq"].arr))
    wukv = _cols_from_shards(wb["w_ukv"].arr)
    wo = wb["w_o"].arr.reshape(-1, d)
    cqn = _rms_fwd(proj, q_norm_g, width=Q_RANK, col=COL_CQ // Q_RANK, tm=tm, name="rms_q",
                   jobs=[ici("w_up", 4, 7)])
    ckvn = _rms_fwd(proj, kv_norm_g, width=KV_RANK, col=COL_CKV // KV_RANK, tm=tm, name="rms_kv",
                    jobs=[ici("w_up", 7, 9)])
    q_pad = _matmul(cqn, wuq, dims=_NN, mnk=(s, N_HEADS * HEAD_PAD, Q_RANK), tiles=(mt, 1024, Q_RANK), name="mm_q",
                    jobs=[ici("w_up", 9, 13), d2d("w_up", 0, 9)])
    kv = _matmul(ckvn, wukv, dims=_NN, mnk=(s, N_HEADS * HEAD_PAD, KV_RANK), tiles=(mt, 1024, KV_RANK),
                 name="mm_kv", out_dtype=bf16, jobs=[ici("w_up", 13, 16)])
    cos, sin = _rope_tables(s)
    q_rot, kr_rot = _qk_rope_fwd(q_pad, proj, cos, sin, tm=tm, jobs=[ici("w_up", 16, 21), d2d("w_up", 9, 16)])
    o, lse, y_attn = _attn_fwd(q_rot, kv, kr_rot, attn_out_g, t=t_attn,
                               jobs=[ici("w_up", 21, 61), d2d("w_up", 16, 21)])
    ycat = jnp.concatenate([y_conv, y_attn], axis=1)
    mix = _matmul(ycat, wo, dims=_NN, mnk=(s, d, 2 * CONV_WIDTH), tiles=(mt, 1024, 2 * CONV_WIDTH), name="mm_out",
                  jobs=[ici("w_up", 61, 64), ici("w_down", 0, 8), d2d("w_up", 21, 61)])
    x2, h2 = _mix_residual_fwd(xs, mix, post_mix_g, pre_mlp_g, tm=tm,
                               jobs=[ici("w_down", 8, 15), d2d("w_up", 61, 64), d2d("w_down", 0, 8)])
    wup = wb["w_up"].arr

    def up_epilogue(acc, extra_refs, out_refs):
        r = jnp.maximum(acc, 0.0)
        out_refs[0][...] = acc.astype(bf16)
        out_refs[1][...] = (r * r).astype(bf16)

    n_ff = ff4 // 1024
    act = jax.ShapeDtypeStruct((s, d_ff), bf16)
    up, act_sq = _matmul(
        h2, wup, dims=_NN, mnk=(s, d_ff, d), tiles=(mt, 1024, d), name="mm_up",
        b_spec=pl.BlockSpec((None, d, 1024), lambda i, j, l: (j // n_ff, l, j % n_ff)),
        out_shape=(act, act), o_spec=(pl.BlockSpec((mt, 1024), lambda i, j, l: (i, j)),) * 2, epilogue=up_epilogue,
        jobs=[ici("w_down", 15, 44), d2d("w_down", 8, 15)])
    _comm("gather_w_down_tail", [[ici("w_down", 44, 64)], [d2d("w_down", 15, 64)]])
    wdown = wb["w_down"].arr.reshape(d_ff, d)
    mlp = _matmul(act_sq, wdown, dims=_NN, mnk=(s, d, d_ff), tiles=(mt, 1024, 2048), name="mm_down")
    dx3, dmlp, dg_post_mlp, loss_part = _loss_head(x2, mlp, target, post_mlp_g, tm=tm)

    def dup_epilogue(acc, extra_refs, out_refs):
        out_refs[0][...] = (acc * (2.0 * jnp.maximum(extra_refs[0][...].astype(jnp.float32), 0.0))).astype(bf16)

    grads, theirs, pair_sums, by_source, whole = {}, {}, {}, {}, {}

    def exchange(k, g):
        grads[k] = g
        theirs[k] = _Buf(jax.ShapeDtypeStruct((4, g.shape[1] // 2, g.shape[2]), bf16))
        return _PairExchange(g, theirs[k])

    def pair_sum(k, jobs=()):
        pair_sums[k] = _pair_add(grads[k], theirs[k].arr, pos, name="pair_add_" + k, jobs=jobs)
        by_source[k] = _Buf(jax.ShapeDtypeStruct(pair_sums[k].shape, bf16))

    scatter = lambda k, a=0, b=64: _ScatterIci(pair_sums[k], by_source[k], *rows(k, a, b))

    def share(k):
        whole[k] = _Buf(_chip_sum(by_source[k].arr, pair_sums[k], pos, name="chip_sum_" + k))
        return _PairShare(whole[k])

    g_wdown = _matmul(act_sq, dmlp, dims=_TN, mnk=(d_ff, d, s), tiles=(1024, 1024, kt), name="mm_gw_down",
                      out_dtype=bf16).reshape(4, ff4, d)
    dup = _matmul(dmlp, wdown, dims=_NT, mnk=(s, d_ff, d), tiles=(mt, 1024, d), name="mm_dact",
                  out_dtype=bf16, epilogue=dup_epilogue, extra=(up,),
                  extra_specs=(pl.BlockSpec((mt, 1024), lambda i, j, l: (i, j)),),
                  jobs=[exchange("w_down", g_wdown)])
    pair_sum("w_down")
    g_wup = _matmul(h2, dup, dims=_TN, mnk=(d, d_ff, s), tiles=(1024, 1024, kt), name="mm_gw_up",
                    out_shape=jax.ShapeDtypeStruct((4, d, ff4), bf16),
                    o_spec=pl.BlockSpec((None, 1024, 1024), lambda i, j, l: (j // n_ff, i, j % n_ff)),
                    jobs=[scatter("w_down", 0, 24)])
    dh2 = _matmul(dup, wup, dims=_NT, mnk=(s, d, d_ff), tiles=(mt, 1024, ff4), name="mm_dh2",
                  b_spec=pl.BlockSpec((None, 1024, ff4), lambda i, j, l: (l, j, 0)),
                  jobs=[exchange("w_up", g_wup), scatter("w_down", 24, 49)])
    pair_sum("w_up", jobs=[scatter("w_down", 49, 55)])
    dx2, dmix, dg_pre_mlp, dg_post_mix = _mix_residual_bwd(
        dx3, dh2, x2, mix, pre_mlp_g, post_mix_g, tm=tm, jobs=[scatter("w_down", 55, 64)])

    dycat = _matmul(dmix, wo, dims=_NT, mnk=(s, 2 * CONV_WIDTH, d), tiles=(mt, 1024, d), name="mm_dycat",
                    jobs=[scatter("w_up", 0, 6)])
    g_wo = _matmul(ycat, dmix, dims=_TN, mnk=(2 * CONV_WIDTH, d, s), tiles=(1024, 1024, kt),
                   name="mm_gw_out", out_dtype=bf16, jobs=[scatter("w_up", 6, 12)]).reshape(4, CONV_WIDTH // 2, d)
    du, dgb, dgc, dg_conv_w, dg_conv_out = _conv_bwd(proj, conv_w_full, conv_out_g, dycat,
                                                     jobs=[exchange("w_o", g_wo), scatter("w_up", 12, 18)])
    pair_sum("w_o")
    do, delta, dg_attn_out = _attn_norm_bwd(o, attn_out_g, dycat, jobs=[scatter("w_up", 18, 22)])
    dq_pad, dk_pad, dv = _attn_bwd(q_rot, kv, kr_rot, do, lse, delta, t=t_attn,
                                   jobs=[scatter("w_up", 22, 44), share("w_down")])
    dq_raw, dkv, dkr = _qk_rope_bwd(dq_pad, dk_pad, dv, cos, sin, tm=tm, jobs=[scatter("w_up", 44, 51)])
    wq_cols = N_HEADS * HEAD_PAD
    g_wuq = _matmul(cqn, dq_raw, dims=_TN, mnk=(Q_RANK, wq_cols, s), tiles=(Q_RANK, 1024, kt),
                    name="mm_gw_uq", out_dtype=bf16, jobs=[scatter("w_up", 51, 54)])
    dcqn = _matmul(dq_raw, wuq, dims=_NT, mnk=(s, Q_RANK, wq_cols), tiles=(mt, Q_RANK, wq_cols), name="mm_dcq",
                   jobs=[scatter("w_up", 54, 57)])
    g_wukv = _matmul(ckvn, dkv, dims=_TN, mnk=(KV_RANK, wq_cols, s), tiles=(KV_RANK, 1024, kt),
                     name="mm_gw_ukv", out_dtype=bf16, jobs=[scatter("w_up", 57, 59)])
    dckvn = _matmul(dkv, wukv, dims=_NT, mnk=(s, KV_RANK, wq_cols), tiles=(mt, KV_RANK, wq_cols), name="mm_dckv",
                    jobs=[scatter("w_up", 59, 61), exchange("w_uq", _cols_to_shards(_unpad_w_uq(g_wuq))),
                          exchange("w_ukv", _cols_to_shards(g_wukv))])
    pair_sum("w_uq")
    pair_sum("w_ukv")
    dcq, dg_q_norm = _rms_bwd(proj, q_norm_g, dcqn, width=Q_RANK, col=COL_CQ // Q_RANK, tm=tm, name="rms_q_bwd",
                              jobs=[scatter("w_up", 61, 64)])
    dckv, dg_kv_norm = _rms_bwd(proj, kv_norm_g, dckvn, width=KV_RANK, col=COL_CKV // KV_RANK, tm=tm,
                                name="rms_kv_bwd", jobs=[scatter("w_o", 0, 12)])
    dproj = jnp.concatenate([du, dgb, dgc, dcq, dkr, jnp.zeros((s, COL_CKV - COL_KR - 128), bf16), dckv], axis=1)
    g_win = _matmul(h1, dproj, dims=_TN, mnk=(d, IN_PAD, s), tiles=(1024, IN_PAD // 3, kt), name="mm_gw_in",
                    out_dtype=bf16, jobs=[scatter("w_o", 12, 64), share("w_up")])
    _comm("pair_exchange_w_in", [[exchange("w_in", _cols_to_shards(_unpad_w_in(g_win)))]])
    pair_sum("w_in", jobs=[scatter("w_uq"), scatter("w_ukv")])
    dh1 = _matmul(dproj, win, dims=_NT, mnk=(s, d, IN_PAD), tiles=(mt, 1024, IN_PAD // 2), name="mm_dh1",
                  jobs=[scatter("w_in", 0, 34), share("w_o")])
    grad_x, dg_pre_mix = _input_bwd(dx2, dh1, xs, pre_mix_g, tm=tm,
                                    jobs=[scatter("w_in", 34, 64), share("w_uq"), share("w_ukv")])
    _comm("pair_share_w_in", [[share("w_in")]])

    moments = dict(w_in=(m_w_in, v_w_in), w_uq=(m_w_uq, v_w_uq), w_ukv=(m_w_ukv, v_w_ukv), w_o=(m_w_o, v_w_o),
                   w_up=(m_w_up, v_w_up), w_down=(m_w_down, v_w_down))
    gw, dw, nm, nv = {}, {}, {}, {}
    for k in names:
        delta_k, nm_k, nv_k, g = _adamw(big[k], whole[k].arr, moments[k][0][0], moments[k][1][0], name="adamw_" + k)
        gw[k], dw[k], nm[k], nv[k] = g[None], delta_k[None], nm_k[None], nv_k[None]

    small_g = _all_reduce_small(_pack_small(d, dg_pre_mix, dg_post_mix, dg_pre_mlp, dg_post_mlp, dg_conv_out,
                                            dg_attn_out, dg_q_norm, dg_kv_norm, dg_conv_w))
    pack_w = lambda cw, pre_mix, post_mix, pre_mlp, post_mlp, conv_out, attn_out, q_norm, kv_norm: _pack_small(
        d, pre_mix, post_mix, pre_mlp, post_mlp, conv_out, attn_out, q_norm, kv_norm, cw)
    small_w = pack_w(conv_w_full, pre_mix_g, post_mix_g, pre_mlp_g, post_mlp_g, conv_out_g, attn_out_g, q_norm_g,
                     kv_norm_g)
    small_m = pack_w(spread(m_conv_w), m_pre_mix_g, m_post_mix_g, m_pre_mlp_g, m_post_mlp_g, m_conv_out_g,
                     m_attn_out_g, m_q_norm_g, m_kv_norm_g)
    small_v = pack_w(spread(v_conv_w), v_pre_mix_g, v_post_mix_g, v_pre_mlp_g, v_post_mlp_g, v_conv_out_g,
                     v_attn_out_g, v_q_norm_g, v_kv_norm_g)
    small_d, small_nm, small_nv, small_g = _adamw(small_w, small_g, small_m, small_v, name="adamw_small")
    sg, sd, snm, snv = (_unpack_small(p, chip) for p in (small_g, small_d, small_nm, small_nv))

    for src, dst in ((sg, gw), (sd, dw), (snm, nm), (snv, nv)):
        dst.update(src)

    loss = lax.psum(loss_part[0, 0], ("x", "y", "c"))
    order = ["pre_mix_g", "w_in", "conv_w", "q_norm_g", "w_uq", "kv_norm_g", "w_ukv", "conv_out_g", "attn_out_g",
             "w_o", "post_mix_g", "pre_mlp_g", "w_up", "w_down", "post_mlp_g"]
    return (loss, grad_x.reshape(1, s, d), *[gw[k] for k in order], *[dw[k] for k in order],
            *[nm[k] for k in order], *[nv[k] for k in order])
```

```python
import functools

import jax
import jax.numpy as jnp
from jax import lax
from jax.experimental import pallas as pl
from jax.experimental.pallas import tpu as pltpu

EPS = 1e-6
NEG_INF = -1e30
CHUNK_SHIFT = 6
N_HEADS = 8
HEAD_PAD = 256
QK_NOPE = 128
QK_ROPE = 64
V_DIM = 128
CONV_WIDTH = 1024
Q_RANK = 768
KV_RANK = 512
ROPE_THETA = 10000.0
ATTN_SCALE = (QK_NOPE + QK_ROPE) ** -0.5
ADAM_LR, ADAM_B1, ADAM_B2, ADAM_EPS, ADAM_WD, ADAM_STEP = 0.001, 0.9, 0.999, 1e-08, 0.01, 10

COL_CQ = 3 * CONV_WIDTH
COL_KR = COL_CQ + Q_RANK
COL_CKV = 4096
IN_PAD = COL_CKV + KV_RANK
IN_WIDTH = 3 * CONV_WIDTH + Q_RANK + KV_RANK + QK_ROPE

VMEM_LIMIT_BYTES = 56 * 1024 * 1024
MESH = pl.DeviceIdType.MESH
ANY = pl.BlockSpec(memory_space=pl.ANY)

_NN = (((1,), (0,)), ((), ()))
_NT = (((1,), (1,)), ((), ()))
_TN = (((0,), (0,)), ((), ()))


def _params(sem):
    return pltpu.CompilerParams(dimension_semantics=sem, vmem_limit_bytes=VMEM_LIMIT_BYTES)


class _Buf:
    def __init__(self, arr):
        self.arr = arr


def _position():
    return lax.axis_index("x"), lax.axis_index("y"), lax.axis_index("c")


def _other_chips(x, y):
    return [(2 * (1 - x) + y, (1 - x, y)), (2 * x + (1 - y), (x, 1 - y)), (2 * (1 - x) + (1 - y), (1 - x, 1 - y))]


def _remote(src, dst, sems, k, to):
    send, recv, off = sems
    return pltpu.make_async_remote_copy(src_ref=src, dst_ref=dst, send_sem=send.at[off + k], recv_sem=recv.at[off + k],
                                        device_id=to, device_id_type=MESH)


class _GatherIci:
    n_sems = 2

    def __init__(self, buf, lo, n):
        self.buf, self.lo, self.n, self.buffers, self.sources = buf, lo, n, [buf], []

    def _rows(self, ref, slot, which, lo=None, n=None):
        lo, n = (self.lo, self.n) if lo is None else (lo, n)
        return ref.at[slot, pl.ds(which * (ref.shape[1] // 2) + lo, n), :]

    def start(self, refs, sems):
        ref = refs[id(self.buf)]
        x, y, c = _position()
        mine = self._rows(ref, 2 * x + y, c)
        for k, (_, xy) in enumerate(_other_chips(x, y)[:2]):
            _remote(mine, mine, sems, k, (*xy, c)).start()

    def wait(self, refs, sems):
        ref = refs[id(self.buf)]
        x, y, c = _position()
        mine = self._rows(ref, 2 * x + y, c)
        for k, (slot, xy) in enumerate(_other_chips(x, y)[:2]):
            landed = self._rows(ref, slot, c)
            _remote(landed, landed, sems, k, (*xy, c)).wait_recv()
            _remote(mine, mine, sems, k, (*xy, c)).wait_send()


class _GatherForward(_GatherIci):
    def _ways(self, x, y):
        (slot_x, xy_x), (slot_y, xy_y), (slot_d, _) = _other_chips(x, y)
        h = self.n // 2
        assert h % 16 == 0, self.n
        return [(slot_x, slot_d, self.lo, xy_y), (slot_y, slot_d, self.lo + h, xy_x)], h

    def start(self, refs, sems):
        ref = refs[id(self.buf)]
        x, y, c = _position()
        ways, h = self._ways(x, y)
        for k, (slot, _, lo, xy) in enumerate(ways):
            rows = self._rows(ref, slot, c, lo, h)
            _remote(rows, rows, sems, k, (*xy, c)).start()

    def wait(self, refs, sems):
        ref = refs[id(self.buf)]
        x, y, c = _position()
        ways, h = self._ways(x, y)
        for k, (slot, lands, lo, xy) in enumerate(ways):
            landed, sent = self._rows(ref, lands, c, lo, h), self._rows(ref, slot, c, lo, h)
            _remote(landed, landed, sems, k, (*xy, c)).wait_recv()
            _remote(sent, sent, sems, k, (*xy, c)).wait_send()


class _GatherD2d(_GatherIci):
    def __init__(self, buf, lo, n, slots):
        super().__init__(buf, lo, n)
        self.slots, self.n_sems = slots, len(slots)

    def start(self, refs, sems):
        ref = refs[id(self.buf)]
        x, y, c = _position()
        chips = _other_chips(x, y)
        for k, which in enumerate(self.slots):
            rows = self._rows(ref, chips[which][0], c)
            _remote(rows, rows, sems, k, (x, y, 1 - c)).start()

    def wait(self, refs, sems):
        ref = refs[id(self.buf)]
        x, y, c = _position()
        chips = _other_chips(x, y)
        for k, which in enumerate(self.slots):
            sent, landed = self._rows(ref, chips[which][0], c), self._rows(ref, chips[which][0], 1 - c)
            _remote(landed, landed, sems, k, (x, y, 1 - c)).wait_recv()
            _remote(sent, sent, sems, k, (x, y, 1 - c)).wait_send()


class _ScatterIci:
    n_sems = 3

    def __init__(self, src, dst, lo, n):
        self.src, self.dst, self.lo, self.n, self.buffers, self.sources = src, dst, lo, n, [dst], [src]

    def _rows(self, ref, slot):
        return ref.at[slot, pl.ds(self.lo, self.n), :]

    def start(self, refs, sems):
        src, dst = refs[id(self.src)], refs[id(self.dst)]
        x, y, c = _position()
        for k, (slot, xy) in enumerate(_other_chips(x, y)):
            _remote(self._rows(src, slot), self._rows(dst, 2 * x + y), sems, k, (*xy, c)).start()

    def wait(self, refs, sems):
        src, dst = refs[id(self.src)], refs[id(self.dst)]
        x, y, c = _position()
        for k, (slot, xy) in enumerate(_other_chips(x, y)):
            _remote(self._rows(src, slot), self._rows(dst, slot), sems, k, (*xy, c)).wait_recv()
            _remote(self._rows(src, slot), self._rows(dst, slot), sems, k, (*xy, c)).wait_send()


class _PairExchange:
    n_sems = 1

    def __init__(self, src, dst):
        self.src, self.dst, self.buffers, self.sources = src, dst, [dst], [src]

    def _copy(self, refs, sems):
        src, dst = refs[id(self.src)], refs[id(self.dst)]
        x, y, c = _position()
        h = src.shape[1] // 2
        return _remote(src.at[:, pl.ds((1 - c) * h, h), :], dst, sems, 0, (x, y, 1 - c))

    def start(self, refs, sems):
        self._copy(refs, sems).start()

    def wait(self, refs, sems):
        self._copy(refs, sems).wait()


class _PairShare:
    n_sems = 1

    def __init__(self, buf):
        self.buf, self.buffers, self.sources = buf, [buf], []

    def _rows(self, ref, which):
        h = ref.shape[0] // 2
        return ref.at[pl.ds(which * h, h), :]

    def start(self, refs, sems):
        ref = refs[id(self.buf)]
        x, y, c = _position()
        _remote(self._rows(ref, c), self._rows(ref, c), sems, 0, (x, y, 1 - c)).start()

    def wait(self, refs, sems):
        ref = refs[id(self.buf)]
        x, y, c = _position()
        _remote(self._rows(ref, c), self._rows(ref, c), sems, 0, (x, y, 1 - c)).wait_send()
        _remote(self._rows(ref, 1 - c), self._rows(ref, 1 - c), sems, 0, (x, y, 1 - c)).wait_recv()


def _unique(items):
    seen, out = set(), []
    for it in items:
        if id(it) not in seen:
            seen.add(id(it))
            out.append(it)
    return out


def _job_operands(jobs):
    sources = _unique([a for j in jobs for a in j.sources])
    buffers = _unique([b for j in jobs for b in j.buffers])
    held = [b for b in buffers if not isinstance(b.arr, jax.ShapeDtypeStruct)]
    fresh = [b for b in buffers if isinstance(b.arr, jax.ShapeDtypeStruct)]
    return sources, held, fresh


def _sem_offsets(jobs):
    offs, total = [], 0
    for j in jobs:
        offs.append(total)
        total += j.n_sems
    return offs, total


def _call(body, *, name, grid, in_specs, out_specs, out_shape, args, semantics, scratch_shapes=(), jobs=(),
          prefetch=None):
    n_pre = 0 if prefetch is None else 1

    def launch(fn, in_specs, out_specs, scratch, **kw):
        if prefetch is None:
            return pl.pallas_call(fn, name=name, grid=grid, in_specs=in_specs, out_specs=out_specs,
                                  scratch_shapes=scratch, **kw)
        return pl.pallas_call(fn, name=name, grid_spec=pltpu.PrefetchScalarGridSpec(
            num_scalar_prefetch=1, grid=grid, in_specs=in_specs, out_specs=out_specs, scratch_shapes=scratch), **kw)

    pre = () if prefetch is None else (prefetch,)
    if not jobs:
        return launch(body, list(in_specs), list(out_specs), list(scratch_shapes), out_shape=list(out_shape),
                      compiler_params=_params(semantics))(*pre, *args)
    sources, held, fresh = _job_operands(jobs)
    offs, n_sem = _sem_offsets(jobs)
    n_in, n_out, n_scr = len(in_specs), len(out_specs), len(scratch_shapes)
    n_src, n_held, n_fresh = len(sources), len(held), len(fresh)

    def carried(*refs):
        pre_refs, refs = refs[:n_pre], refs[n_pre:]
        ins = refs[:n_in]
        src_refs = refs[n_in:n_in + n_src]
        o0 = n_in + n_src + n_held
        outs = refs[o0:o0 + n_out]
        buf_refs = refs[o0 + n_out:o0 + n_out + n_held + n_fresh]
        s0 = o0 + n_out + n_held + n_fresh
        scratch = refs[s0:s0 + n_scr]
        send, recv = refs[s0 + n_scr], refs[s0 + n_scr + 1]
        where = {id(a): r for a, r in zip(sources, src_refs)}
        where.update({id(b): r for b, r in zip(held + fresh, buf_refs)})
        ids = [pl.program_id(a) for a in range(len(grid))]
        first = functools.reduce(jnp.logical_and, [i == 0 for i in ids])
        last = functools.reduce(jnp.logical_and, [i == g - 1 for i, g in zip(ids, grid)])

        @pl.when(first)
        def _():
            for j, off in zip(jobs, offs):
                j.start(where, (send, recv, off))

        body(*pre_refs, *ins, *outs, *scratch)

        @pl.when(last)
        def _():
            for j, off in zip(jobs, offs):
                j.wait(where, (send, recv, off))

    shape_of = lambda b: jax.ShapeDtypeStruct(b.arr.shape, b.arr.dtype)
    res = launch(
        carried, [*in_specs, *[ANY] * (n_src + n_held)], [*out_specs, *[ANY] * (n_held + n_fresh)],
        [*scratch_shapes, pltpu.SemaphoreType.DMA((n_sem,)), pltpu.SemaphoreType.DMA((n_sem,))],
        out_shape=[*out_shape, *[shape_of(b) for b in held + fresh]],
        input_output_aliases={n_pre + n_in + n_src + i: n_out + i for i in range(n_held)},
        compiler_params=pltpu.CompilerParams(dimension_semantics=("arbitrary",) * len(grid),
                                             vmem_limit_bytes=VMEM_LIMIT_BYTES, has_side_effects=True),
    )(*pre, *args, *sources, *[b.arr for b in held])
    for b, new in zip(held + fresh, res[n_out:]):
        b.arr = new
    return list(res[:n_out])


def _comm(name, phases):
    jobs = [j for ph in phases for j in ph]
    sources, held, fresh = _job_operands(jobs)
    offs, n_sem = _sem_offsets(jobs)
    off_of = {id(j): o for j, o in zip(jobs, offs)}
    n_src, n_held, n_fresh = len(sources), len(held), len(fresh)

    def body(*refs):
        src_refs = refs[:n_src]
        buf_refs = refs[n_src + n_held:n_src + 2 * n_held + n_fresh]
        send, recv = refs[-2], refs[-1]
        where = {id(a): r for a, r in zip(sources, src_refs)}
        where.update({id(b): r for b, r in zip(held + fresh, buf_refs)})
        for ph in phases:
            for j in ph:
                j.start(where, (send, recv, off_of[id(j)]))
            for j in ph:
                j.wait(where, (send, recv, off_of[id(j)]))

    shape_of = lambda b: jax.ShapeDtypeStruct(b.arr.shape, b.arr.dtype)
    res = pl.pallas_call(
        body, name=name,
        in_specs=[ANY] * (n_src + n_held), out_specs=[ANY] * (n_held + n_fresh),
        out_shape=[shape_of(b) for b in held + fresh],
        input_output_aliases={n_src + i: i for i in range(n_held)},
        scratch_shapes=[pltpu.SemaphoreType.DMA((n_sem,)), pltpu.SemaphoreType.DMA((n_sem,))],
        compiler_params=pltpu.CompilerParams(has_side_effects=True),
    )(*sources, *[b.arr for b in held])
    for b, new in zip(held + fresh, res):
        b.arr = new


def _matmul(a, b, *, dims, mnk, tiles, name, out_dtype=jnp.float32, a_spec=None, b_spec=None,
            out_shape=None, o_spec=None, epilogue=None, extra=(), extra_specs=(), jobs=()):
    m, n, k = mnk
    tm, tn, tk = tiles
    assert m % tm == 0 and n % tn == 0 and k % tk == 0, (name, mnk, tiles)
    gm, gn, gk = m // tm, n // tn, k // tk
    if a_spec is None:
        a_spec = (pl.BlockSpec((tk, tm), lambda i, j, l: (l, i)) if dims is _TN
                  else pl.BlockSpec((tm, tk), lambda i, j, l: (i, l)))
    if b_spec is None:
        b_spec = (pl.BlockSpec((tn, tk), lambda i, j, l: (j, l)) if dims is _NT
                  else pl.BlockSpec((tk, tn), lambda i, j, l: (l, j)))
    if out_shape is None:
        out_shape = jax.ShapeDtypeStruct((m, n), out_dtype)
    if o_spec is None:
        o_spec = pl.BlockSpec((tm, tn), lambda i, j, l: (i, j))
    single = not isinstance(out_shape, (tuple, list))
    n_extra = len(extra)

    def finish(acc, extra_refs, out_refs):
        if epilogue is None:
            out_refs[0][...] = acc.astype(out_refs[0].dtype)
        else:
            epilogue(acc, extra_refs, out_refs)

    def body_whole_k(*refs):
        a_ref, b_ref = refs[0], refs[1]
        acc = lax.dot_general(a_ref[...], b_ref[...], dims, preferred_element_type=jnp.float32)
        finish(acc, refs[2:2 + n_extra], refs[2 + n_extra:])

    def body_split_k(*refs):
        a_ref, b_ref = refs[0], refs[1]
        extra_refs = refs[2:2 + n_extra]
        out_refs = refs[2 + n_extra:-1]
        acc_ref = refs[-1]
        step = pl.program_id(2)
        part = lax.dot_general(a_ref[...], b_ref[...], dims, preferred_element_type=jnp.float32)

        @pl.when(step == 0)
        def _():
            acc_ref[...] = part

        @pl.when(jnp.logical_and(step > 0, step < gk - 1))
        def _():
            acc_ref[...] += part

        @pl.when(step == gk - 1)
        def _():
            finish(acc_ref[...] + part, extra_refs, out_refs)

    res = _call(
        body_whole_k if gk == 1 else body_split_k, name=name, grid=(gm, gn, gk),
        in_specs=[a_spec, b_spec, *extra_specs],
        out_specs=[o_spec] if single else list(o_spec),
        out_shape=[out_shape] if single else list(out_shape),
        scratch_shapes=[] if gk == 1 else [pltpu.VMEM((tm, tn), jnp.float32)],
        semantics=("parallel", "parallel", "arbitrary"), args=(a, b, *extra), jobs=jobs)
    return res[0] if single else res


def _rstd(x):
    return lax.rsqrt(jnp.mean(x * x, axis=-1, keepdims=True) + EPS)


def _rms_bwd_rows(x, g, dy):
    r = _rstd(x)
    xn = x * r
    dyg = dy * g
    dx = r * (dyg - xn * jnp.mean(xn * dyg, axis=-1, keepdims=True))
    return dx, dy * xn


def _acc_rows(ref, rows, first):
    part = jnp.sum(rows, axis=0, keepdims=True)

    @pl.when(first)
    def _():
        ref[...] = part

    @pl.when(jnp.logical_not(first))
    def _():
        ref[...] += part


def _rms_fwd(x, g, *, width, col, tm, name, jobs=()):
    s = x.shape[0]

    def body(x_ref, g_ref, o_ref):
        v = x_ref[...]
        o_ref[...] = (v * _rstd(v) * g_ref[...]).astype(o_ref.dtype)

    return _call(
        body, name=name, grid=(s // tm,),
        in_specs=[pl.BlockSpec((tm, width), lambda i: (i, col)), pl.BlockSpec((1, width), lambda i: (0, 0))],
        out_specs=[pl.BlockSpec((tm, width), lambda i: (i, 0))],
        out_shape=[jax.ShapeDtypeStruct((s, width), jnp.bfloat16)],
        semantics=("parallel",), args=(x, g), jobs=jobs)[0]


def _rms_bwd(x, g, dy, *, width, col, tm, name, jobs=()):
    s = x.shape[0]

    def body(x_ref, g_ref, dy_ref, dx_ref, dg_ref):
        dx, dgr = _rms_bwd_rows(x_ref[...], g_ref[...], dy_ref[...])
        dx_ref[...] = dx.astype(dx_ref.dtype)
        _acc_rows(dg_ref, dgr, pl.program_id(0) == 0)

    return _call(
        body, name=name, grid=(s // tm,),
        in_specs=[pl.BlockSpec((tm, width), lambda i: (i, col)), pl.BlockSpec((1, width), lambda i: (0, 0)),
                  pl.BlockSpec((tm, width), lambda i: (i, 0))],
        out_specs=[pl.BlockSpec((tm, width), lambda i: (i, 0)), pl.BlockSpec((1, width), lambda i: (0, 0))],
        out_shape=[jax.ShapeDtypeStruct((s, width), jnp.bfloat16), jax.ShapeDtypeStruct((1, width), jnp.float32)],
        semantics=("arbitrary",), args=(x, g, dy), jobs=jobs)


def _row_specs(tm, d, n):
    return [pl.BlockSpec((tm, d), lambda i: (i, 0)) for _ in range(n)]


def _gain_specs(d, n):
    return [pl.BlockSpec((1, d), lambda i: (0, 0)) for _ in range(n)]


def _mix_residual_fwd(x, mix, g_post_mix, g_pre_mlp, *, tm, jobs=()):
    s, d = x.shape

    def body(x_ref, mix_ref, g1_ref, g2_ref, x2_ref, h2_ref):
        mixv = mix_ref[...]
        x2 = x_ref[...] + mixv * _rstd(mixv) * g1_ref[...]
        x2_ref[...] = x2
        h2_ref[...] = (x2 * _rstd(x2) * g2_ref[...]).astype(h2_ref.dtype)

    return _call(
        body, name="mix_residual_fwd", grid=(s // tm,),
        in_specs=_row_specs(tm, d, 2) + _gain_specs(d, 2),
        out_specs=_row_specs(tm, d, 2),
        out_shape=[jax.ShapeDtypeStruct((s, d), jnp.float32), jax.ShapeDtypeStruct((s, d), jnp.bfloat16)],
        semantics=("parallel",), args=(x, mix, g_post_mix, g_pre_mlp), jobs=jobs)


def _loss_head(x2, mlp, target, g_post_mlp, *, tm, jobs=()):
    s, d = x2.shape

    def body(x2_ref, m_ref, t_ref, g_ref, dx3_ref, dm_ref, dg_ref, loss_ref):
        first = pl.program_id(0) == 0
        mv = m_ref[...]
        g = g_ref[...]
        diff = x2_ref[...] + mv * _rstd(mv) * g - t_ref[...]
        dx3 = diff * (1.0 / d)
        dx3_ref[...] = dx3
        dm, dgr = _rms_bwd_rows(mv, g, dx3)
        dm_ref[...] = dm.astype(dm_ref.dtype)
        _acc_rows(dg_ref, dgr, first)
        part = 0.5 * jnp.sum(jnp.mean(diff * diff, axis=-1, keepdims=True), axis=0, keepdims=True)
        _acc_rows(loss_ref, jnp.broadcast_to(part, (1, 128)), first)

    return _call(
        body, name="loss_head", grid=(s // tm,),
        in_specs=_row_specs(tm, d, 3) + _gain_specs(d, 1),
        out_specs=_row_specs(tm, d, 2) + _gain_specs(d, 1) + [pl.BlockSpec((1, 128), lambda i: (0, 0))],
        out_shape=[jax.ShapeDtypeStruct((s, d), jnp.float32), jax.ShapeDtypeStruct((s, d), jnp.bfloat16),
                   jax.ShapeDtypeStruct((1, d), jnp.float32), jax.ShapeDtypeStruct((1, 128), jnp.float32)],
        semantics=("arbitrary",), args=(x2, mlp, target, g_post_mlp), jobs=jobs)


def _mix_residual_bwd(dx3, dh2, x2, mix, g_pre_mlp, g_post_mix, *, tm, jobs=()):
    s, d = x2.shape

    def body(dx3_ref, dh2_ref, x2_ref, mix_ref, g2_ref, g1_ref, dx2_ref, dmix_ref, dg2_ref, dg1_ref):
        first = pl.program_id(0) == 0
        d_in, dgr2 = _rms_bwd_rows(x2_ref[...], g2_ref[...], dh2_ref[...])
        dx2 = dx3_ref[...] + d_in
        dx2_ref[...] = dx2
        dmix, dgr1 = _rms_bwd_rows(mix_ref[...], g1_ref[...], dx2)
        dmix_ref[...] = dmix.astype(dmix_ref.dtype)
        _acc_rows(dg2_ref, dgr2, first)
        _acc_rows(dg1_ref, dgr1, first)

    return _call(
        body, name="mix_residual_bwd", grid=(s // tm,),
        in_specs=_row_specs(tm, d, 4) + _gain_specs(d, 2),
        out_specs=_row_specs(tm, d, 2) + _gain_specs(d, 2),
        out_shape=[jax.ShapeDtypeStruct((s, d), jnp.float32), jax.ShapeDtypeStruct((s, d), jnp.bfloat16),
                   jax.ShapeDtypeStruct((1, d), jnp.float32), jax.ShapeDtypeStruct((1, d), jnp.float32)],
        semantics=("arbitrary",), args=(dx3, dh2, x2, mix, g_pre_mlp, g_post_mix), jobs=jobs)


def _input_bwd(dx2, dh1, x, g_pre_mix, *, tm, jobs=()):
    s, d = x.shape

    def body(dx2_ref, dh1_ref, x_ref, g_ref, dx_ref, dg_ref):
        d_in, dgr = _rms_bwd_rows(x_ref[...], g_ref[...], dh1_ref[...])
        dx_ref[...] = dx2_ref[...] + d_in
        _acc_rows(dg_ref, dgr, pl.program_id(0) == 0)

    return _call(
        body, name="input_bwd", grid=(s // tm,),
        in_specs=_row_specs(tm, d, 3) + _gain_specs(d, 1),
        out_specs=_row_specs(tm, d, 1) + _gain_specs(d, 1),
        out_shape=[jax.ShapeDtypeStruct((s, d), jnp.float32), jax.ShapeDtypeStruct((1, d), jnp.float32)],
        semantics=("arbitrary",), args=(dx2, dh1, x, g_pre_mix), jobs=jobs)


def _shift_rows(z, by):
    s = z.shape[0]
    rows = lax.broadcasted_iota(jnp.int32, z.shape, 0)
    rolled = pltpu.roll(z, by % s, axis=0)
    keep = rows >= by if by > 0 else rows < s + by
    return jnp.where(keep, rolled, 0.0)


def _conv_fwd(proj, conv_w, conv_out_g, jobs=()):
    s = proj.shape[0]
    groups = CONV_WIDTH // 128

    def body(u_ref, gb_ref, gc_ref, w_ref, g_ref, y_ref):
        z = gc_ref[...] * u_ref[...]
        w = w_ref[...]
        conv = w[0:1, :] * _shift_rows(z, 2) + w[1:2, :] * _shift_rows(z, 1) + w[2:3, :] * z
        y = gb_ref[...] * conv
        y_ref[...] = (y * _rstd(y) * g_ref[...]).astype(y_ref.dtype)

    col = lambda base: pl.BlockSpec((s, 128), lambda j: (0, base + j))
    return _call(
        body, name="conv_fwd", grid=(groups,),
        in_specs=[col(0), col(groups), col(2 * groups), pl.BlockSpec((3, 128), lambda j: (0, j)),
                  pl.BlockSpec((1, 128), lambda j: (0, j))],
        out_specs=[pl.BlockSpec((s, 128), lambda j: (0, j))],
        out_shape=[jax.ShapeDtypeStruct((s, CONV_WIDTH), jnp.bfloat16)],
        semantics=("parallel",), args=(proj, proj, proj, conv_w, conv_out_g), jobs=jobs)[0]


def _conv_bwd(proj, conv_w, conv_out_g, dycat, jobs=()):
    s = proj.shape[0]
    groups = CONV_WIDTH // 128

    def body(u_ref, gb_ref, gc_ref, w_ref, g_ref, dy_ref, du_ref, dgb_ref, dgc_ref, dw_ref, dg_ref):
        u, gb, gc = u_ref[...], gb_ref[...], gc_ref[...]
        w = w_ref[...]
        z = gc * u
        z1, z2 = _shift_rows(z, 1), _shift_rows(z, 2)
        conv = w[0:1, :] * z2 + w[1:2, :] * z1 + w[2:3, :] * z
        dyr, dgr = _rms_bwd_rows(gb * conv, g_ref[...], dy_ref[...])
        dg_ref[...] = jnp.sum(dgr, axis=0, keepdims=True)
        dgb_ref[...] = (dyr * conv).astype(dgb_ref.dtype)
        dconv = dyr * gb
        dw_ref[0:1, :] = jnp.sum(dconv * z2, axis=0, keepdims=True)
        dw_ref[1:2, :] = jnp.sum(dconv * z1, axis=0, keepdims=True)
        dw_ref[2:3, :] = jnp.sum(dconv * z, axis=0, keepdims=True)
        dz = w[2:3, :] * dconv + w[1:2, :] * _shift_rows(dconv, -1) + w[0:1, :] * _shift_rows(dconv, -2)
        dgc_ref[...] = (dz * u).astype(dgc_ref.dtype)
        du_ref[...] = (dz * gc).astype(du_ref.dtype)

    col = lambda base: pl.BlockSpec((s, 128), lambda j: (0, base + j))
    act = jax.ShapeDtypeStruct((s, CONV_WIDTH), jnp.bfloat16)
    return _call(
        body, name="conv_bwd", grid=(groups,),
        in_specs=[col(0), col(groups), col(2 * groups), pl.BlockSpec((3, 128), lambda j: (0, j)),
                  pl.BlockSpec((1, 128), lambda j: (0, j)), col(0)],
        out_specs=[col(0), col(0), col(0), pl.BlockSpec((3, 128), lambda j: (0, j)),
                   pl.BlockSpec((1, 128), lambda j: (0, j))],
        out_shape=[act, act, act, jax.ShapeDtypeStruct((3, CONV_WIDTH), jnp.float32),
                   jax.ShapeDtypeStruct((1, CONV_WIDTH), jnp.float32)],
        semantics=("parallel",), args=(proj, proj, proj, conv_w, conv_out_g, dycat), jobs=jobs)


def _rope_tables(s):
    pos = jnp.arange(s, dtype=jnp.float32)
    inv_freq = jnp.power(ROPE_THETA, -jnp.arange(0, QK_ROPE, 2, dtype=jnp.float32) / QK_ROPE)
    ang = pos[:, None] * inv_freq[None, :]
    cos, sin = jnp.cos(ang), jnp.sin(ang)
    zeros = jnp.zeros((s, 128 - QK_ROPE), jnp.float32)
    return (jnp.concatenate([cos, cos, zeros], axis=1), jnp.concatenate([-sin, sin, zeros], axis=1))


def _swap_halves(x):
    lane = lax.broadcasted_iota(jnp.int32, x.shape, 1)
    swapped = jnp.where(lane < QK_ROPE // 2, pltpu.roll(x, 128 - QK_ROPE // 2, axis=1),
                        pltpu.roll(x, QK_ROPE // 2, axis=1))
    return jnp.where(lane < QK_ROPE, swapped, 0.0)


def _rope(x, cos, sin):
    return x * cos + _swap_halves(x) * sin


def _rope_transposed(d, cos, sin):
    return d * cos + _swap_halves(d * sin)


def _qk_rope_fwd(q_pad, proj, cos, sin, *, tm, jobs=()):
    s = q_pad.shape[0]
    wq = N_HEADS * HEAD_PAD

    def body(q_ref, kr_ref, cos_ref, sin_ref, qo_ref, kro_ref):
        c, sn = cos_ref[...], sin_ref[...]
        for h in range(N_HEADS):
            lo = h * HEAD_PAD
            qo_ref[:, lo:lo + 128] = q_ref[:, lo:lo + 128].astype(qo_ref.dtype)
            qo_ref[:, lo + 128:lo + 256] = _rope(q_ref[:, lo + 128:lo + 256], c, sn).astype(qo_ref.dtype)
        kro_ref[...] = _rope(kr_ref[...], c, sn).astype(kro_ref.dtype)

    return _call(
        body, name="qk_rope_fwd", grid=(s // tm,),
        in_specs=[pl.BlockSpec((tm, wq), lambda i: (i, 0)), pl.BlockSpec((tm, 128), lambda i: (i, COL_KR // 128)),
                  pl.BlockSpec((tm, 128), lambda i: (i, 0)), pl.BlockSpec((tm, 128), lambda i: (i, 0))],
        out_specs=[pl.BlockSpec((tm, wq), lambda i: (i, 0)), pl.BlockSpec((tm, 128), lambda i: (i, 0))],
        out_shape=[jax.ShapeDtypeStruct((s, wq), jnp.bfloat16), jax.ShapeDtypeStruct((s, 128), jnp.bfloat16)],
        semantics=("parallel",), args=(q_pad, proj, cos, sin), jobs=jobs)


def _qk_rope_bwd(dq_pad, dk_pad, dv, cos, sin, *, tm, jobs=()):
    s = dq_pad.shape[0]
    wq = N_HEADS * HEAD_PAD

    def body(dq_ref, dk_ref, dv_ref, cos_ref, sin_ref, dqo_ref, dkv_ref, dkr_ref):
        c, sn = cos_ref[...], sin_ref[...]
        dkr = jnp.zeros((tm, 128), jnp.float32)
        for h in range(N_HEADS):
            lo = h * HEAD_PAD
            dqo_ref[:, lo:lo + 128] = dq_ref[:, lo:lo + 128].astype(dqo_ref.dtype)
            dqo_ref[:, lo + 128:lo + 256] = _rope_transposed(dq_ref[:, lo + 128:lo + 256], c, sn).astype(dqo_ref.dtype)
            dkv_ref[:, lo:lo + 128] = dk_ref[:, lo:lo + 128].astype(dkv_ref.dtype)
            dkv_ref[:, lo + 128:lo + 256] = dv_ref[:, h * V_DIM:(h + 1) * V_DIM].astype(dkv_ref.dtype)
            dkr = dkr + dk_ref[:, lo + 128:lo + 256]
        dkr_ref[...] = _rope_transposed(dkr, c, sn).astype(dkr_ref.dtype)

    return _call(
        body, name="qk_rope_bwd", grid=(s // tm,),
        in_specs=[pl.BlockSpec((tm, wq), lambda i: (i, 0)), pl.BlockSpec((tm, wq), lambda i: (i, 0)),
                  pl.BlockSpec((tm, N_HEADS * V_DIM), lambda i: (i, 0)),
                  pl.BlockSpec((tm, 128), lambda i: (i, 0)), pl.BlockSpec((tm, 128), lambda i: (i, 0))],
        out_specs=[pl.BlockSpec((tm, wq), lambda i: (i, 0)), pl.BlockSpec((tm, wq), lambda i: (i, 0)),
                   pl.BlockSpec((tm, 128), lambda i: (i, 0))],
        out_shape=[jax.ShapeDtypeStruct((s, wq), jnp.bfloat16), jax.ShapeDtypeStruct((s, wq), jnp.bfloat16),
                   jax.ShapeDtypeStruct((s, 128), jnp.bfloat16)],
        semantics=("parallel",), args=(dq_pad, dk_pad, dv, cos, sin), jobs=jobs)


def _visible(q0, k0, t):
    qpos = q0 + lax.broadcasted_iota(jnp.int32, (t, t), 0)
    kpos = k0 + lax.broadcasted_iota(jnp.int32, (t, t), 1)
    return lax.shift_right_logical(kpos, CHUNK_SHIFT) <= lax.shift_right_logical(qpos, CHUNK_SHIFT)


def _attn_fwd(q, kv, kr, attn_out_g, *, t, jobs=()):
    s = q.shape[0]
    nq = s // t

    def body(q_ref, kn_ref, v_ref, kr_ref, g_ref, o_ref, lse_ref, y_ref, kcat_ref):
        i = pl.program_id(1)

        @pl.when(i == 0)
        def _():
            kcat_ref[:, 0:128] = kn_ref[...]
            kcat_ref[:, 128:256] = kr_ref[...]

        qv = q_ref[...]

        def step(j, carry, diagonal):
            m, l, acc = carry
            k = kcat_ref[pl.ds(pl.multiple_of(j * t, t), t), :]
            v = v_ref[pl.ds(pl.multiple_of(j * t, t), t), :]
            sc = lax.dot_general(qv, k, _NT, preferred_element_type=jnp.float32) * ATTN_SCALE
            if diagonal:
                sc = jnp.where(_visible(0, 0, t), sc, NEG_INF)
            m_new = jnp.maximum(m, jnp.max(sc, axis=-1, keepdims=True))
            p = jnp.exp(sc - m_new)
            alpha = jnp.exp(m - m_new)
            l = alpha * l + jnp.sum(p, axis=-1, keepdims=True)
            acc = alpha * acc + lax.dot_general(p.astype(jnp.bfloat16), v, _NN, preferred_element_type=jnp.float32)
            return m_new, l, acc

        init = (jnp.full((t, 1), NEG_INF, jnp.float32), jnp.zeros((t, 1), jnp.float32),
                jnp.zeros((t, V_DIM), jnp.float32))
        before = lax.fori_loop(0, i, functools.partial(step, diagonal=False), init)
        m, l, acc = step(i, before, True)
        o = acc / l
        o_ref[...] = o
        lse_ref[...] = jnp.broadcast_to(m + jnp.log(l), (t, 128))
        y_ref[...] = (o * _rstd(o) * g_ref[...]).astype(y_ref.dtype)

    head_rows = lambda w, f: pl.BlockSpec((s, w), lambda h, i: (0, f(h)))
    blk = pl.BlockSpec((t, 128), lambda h, i: (i, h))
    full = jax.ShapeDtypeStruct((s, N_HEADS * V_DIM), jnp.float32)
    return _call(
        body, name="attn_fwd", grid=(N_HEADS, nq),
        in_specs=[pl.BlockSpec((t, HEAD_PAD), lambda h, i: (i, h)), head_rows(128, lambda h: 2 * h),
                  head_rows(128, lambda h: 2 * h + 1), head_rows(128, lambda h: 0),
                  pl.BlockSpec((1, 128), lambda h, i: (0, h))],
        out_specs=[blk, blk, blk],
        out_shape=[full, full, jax.ShapeDtypeStruct((s, N_HEADS * V_DIM), jnp.bfloat16)],
        scratch_shapes=[pltpu.VMEM((s, HEAD_PAD), jnp.bfloat16)],
        semantics=("arbitrary", "arbitrary"), args=(q, kv, kv, kr, attn_out_g), jobs=jobs)


def _attn_norm_bwd(o, attn_out_g, dycat, jobs=()):
    s = o.shape[0]

    def body(o_ref, g_ref, dy_ref, do_ref, delta_ref, dg_ref):
        ov = o_ref[...]
        do, dgr = _rms_bwd_rows(ov, g_ref[...], dy_ref[...])
        do_ref[...] = do.astype(do_ref.dtype)
        delta_ref[...] = jnp.broadcast_to(jnp.sum(do * ov, axis=-1, keepdims=True), (s, 128))
        dg_ref[...] = jnp.sum(dgr, axis=0, keepdims=True)

    col = lambda base: pl.BlockSpec((s, 128), lambda h: (0, base + h))
    return _call(
        body, name="attn_norm_bwd", grid=(N_HEADS,),
        in_specs=[col(0), pl.BlockSpec((1, 128), lambda h: (0, h)), col(CONV_WIDTH // 128)],
        out_specs=[col(0), col(0), pl.BlockSpec((1, 128), lambda h: (0, h))],
        out_shape=[jax.ShapeDtypeStruct((s, N_HEADS * V_DIM), jnp.bfloat16),
                   jax.ShapeDtypeStruct((s, N_HEADS * V_DIM), jnp.float32),
                   jax.ShapeDtypeStruct((1, N_HEADS * V_DIM), jnp.float32)],
        semantics=("parallel",), args=(o, attn_out_g, dycat), jobs=jobs)


def _attn_bwd(q, kv, kr, do, lse, delta, *, t, jobs=()):
    s = q.shape[0]
    nq = s // t

    def body(q_ref, kn_ref, v_ref, kr_ref, do_ref, lse_ref, delta_ref, dq_ref, dk_ref, dv_ref, kcat_ref):
        kcat_ref[:, 0:128] = kn_ref[...]
        kcat_ref[:, 128:256] = kr_ref[...]
        dq_ref[...] = jnp.zeros_like(dq_ref)
        dk_ref[...] = jnp.zeros_like(dk_ref)
        dv_ref[...] = jnp.zeros_like(dv_ref)

        def kv_step(j, _):
            krows = pl.ds(pl.multiple_of(j * t, t), t)
            k = kcat_ref[krows, :]
            v = v_ref[krows, :]

            def q_step(i, _, diagonal):
                qrows = pl.ds(pl.multiple_of(i * t, t), t)
                qv = q_ref[qrows, :]
                dov = do_ref[qrows, :]
                sc = lax.dot_general(qv, k, _NT, preferred_element_type=jnp.float32) * ATTN_SCALE
                if diagonal:
                    sc = jnp.where(_visible(0, 0, t), sc, NEG_INF)
                p = jnp.exp(sc - lse_ref[qrows, :][:, 0:1])
                dp = lax.dot_general(dov, v, _NT, preferred_element_type=jnp.float32)
                ds = (p * (dp - delta_ref[qrows, :][:, 0:1]) * ATTN_SCALE).astype(jnp.bfloat16)
                dv_ref[krows, :] += lax.dot_general(p.astype(jnp.bfloat16), dov, _TN,
                                                    preferred_element_type=jnp.float32)
                dk_ref[krows, :] += lax.dot_general(ds, qv, _TN, preferred_element_type=jnp.float32)
                dq_ref[qrows, :] += lax.dot_general(ds, k, _NN, preferred_element_type=jnp.float32)
                return 0

            q_step(j, 0, True)
            lax.fori_loop(j + 1, nq, functools.partial(q_step, diagonal=False), 0)
            return 0

        lax.fori_loop(0, nq, kv_step, 0)

    col = lambda w, f: pl.BlockSpec((s, w), lambda h: (0, f(h)))
    return _call(
        body, name="attn_bwd", grid=(N_HEADS,),
        in_specs=[col(HEAD_PAD, lambda h: h), col(128, lambda h: 2 * h), col(128, lambda h: 2 * h + 1),
                  col(128, lambda h: 0), col(128, lambda h: h), col(128, lambda h: h), col(128, lambda h: h)],
        out_specs=[col(HEAD_PAD, lambda h: h), col(HEAD_PAD, lambda h: h), col(128, lambda h: h)],
        out_shape=[jax.ShapeDtypeStruct((s, N_HEADS * HEAD_PAD), jnp.float32),
                   jax.ShapeDtypeStruct((s, N_HEADS * HEAD_PAD), jnp.float32),
                   jax.ShapeDtypeStruct((s, N_HEADS * V_DIM), jnp.float32)],
        scratch_shapes=[pltpu.VMEM((s, HEAD_PAD), jnp.bfloat16)],
        semantics=("parallel",), args=(q, kv, kv, kr, do, lse, delta), jobs=jobs)


def _row_tile(rows):
    for cand in (256, 128, 64, 32, 16, 8):
        if rows % cand == 0:
            return cand
    return rows


def _cast_into_slot(w, pos, *, name, jobs=()):
    r, c = w.shape
    tr = _row_tile(r)

    def body(pos_ref, w_ref, o_ref):
        o_ref[...] = w_ref[...].astype(o_ref.dtype)

    return _call(
        body, name=name, grid=(r // tr,), prefetch=pos,
        in_specs=[pl.BlockSpec((tr, c), lambda i, p: (i, 0))],
        out_specs=[pl.BlockSpec((None, tr, c), lambda i, p: (p[1], i, 0))],
        out_shape=[jax.ShapeDtypeStruct((4, r, c), jnp.bfloat16)],
        semantics=("parallel",), args=(w,), jobs=jobs)[0]


def _pair_add(g, theirs, pos, *, name, jobs=()):
    n, h, c = theirs.shape
    tr = _row_tile(h)
    nb = h // tr

    def body(pos_ref, a_ref, b_ref, o_ref):
        o_ref[...] = (a_ref[...].astype(jnp.float32) + b_ref[...].astype(jnp.float32)).astype(o_ref.dtype)

    spec = pl.BlockSpec((None, tr, c), lambda j, i, p: (j, i, 0))
    return _call(
        body, name=name, grid=(n, nb), prefetch=pos,
        in_specs=[pl.BlockSpec((None, tr, c), lambda j, i, p: (j, i + p[0] * nb, 0)), spec],
        out_specs=[spec], out_shape=[jax.ShapeDtypeStruct(theirs.shape, jnp.bfloat16)],
        semantics=("parallel", "parallel"), args=(g, theirs), jobs=jobs)[0]


def _chip_sum(by_source, pair_sum, pos, *, name):
    n, h, c = by_source.shape
    tr = _row_tile(h)
    nb = h // tr

    def body(pos_ref, p0, p1, p2, p3, own_ref, o_ref):
        own = own_ref[...].astype(jnp.float32)
        term = lambda k, ref: jnp.where(pos_ref[1] == k, own, ref[...].astype(jnp.float32))
        o_ref[...] = ((term(0, p0) + term(1, p1)) + term(2, p2)) + term(3, p3)

    def source(k):
        return pl.BlockSpec((None, tr, c), lambda i, p: (jnp.where(p[1] == k, (k + 1) % 4, k), i, 0))

    return pl.pallas_call(
        body, name=name, out_shape=jax.ShapeDtypeStruct((2 * h, c), jnp.float32),
        grid_spec=pltpu.PrefetchScalarGridSpec(
            num_scalar_prefetch=1, grid=(nb,),
            in_specs=[source(0), source(1), source(2), source(3),
                      pl.BlockSpec((None, tr, c), lambda i, p: (p[1], i, 0))],
            out_specs=pl.BlockSpec((tr, c), lambda i, p: (i + p[0] * nb, 0))),
        compiler_params=_params(("parallel",)),
    )(pos, by_source, by_source, by_source, by_source, pair_sum)


def _adamw(w, g, m, v, *, name, jobs=()):
    r, c = w.shape
    tr = _row_tile(r)

    def body(w_ref, g_ref, m_ref, v_ref, d_ref, mo_ref, vo_ref, go_ref):
        gv = g_ref[...]
        go_ref[...] = gv
        mn = ADAM_B1 * m_ref[...] + (1.0 - ADAM_B1) * gv
        vn = ADAM_B2 * v_ref[...] + (1.0 - ADAM_B2) * (gv * gv)
        m_hat = mn / (1.0 - ADAM_B1 ** ADAM_STEP)
        v_hat = vn / (1.0 - ADAM_B2 ** ADAM_STEP)
        d_ref[...] = -ADAM_LR * (m_hat / (jnp.sqrt(v_hat) + ADAM_EPS) + ADAM_WD * w_ref[...])
        mo_ref[...] = mn
        vo_ref[...] = vn

    spec = pl.BlockSpec((tr, c), lambda i: (i, 0))
    out = jax.ShapeDtypeStruct((r, c), jnp.float32)
    return _call(body, name=name, grid=(r // tr,), in_specs=[spec] * 4, out_specs=[spec] * 4, out_shape=[out] * 4,
                 semantics=("parallel",), args=(w, g, m, v), jobs=jobs)


def _all_reduce_small(block):
    r, c = block.shape

    def body(src_ref, out_ref, stage_ref, send_sems, recv_sems):
        x, y, cc = _position()
        me = 4 * x + 2 * y + cc
        stage_ref[me] = src_ref[...]
        flip = lambda v, on: 1 - v if on else v
        peers = [(flip(x, dx), flip(y, dy), flip(cc, dc)) for dx in (0, 1) for dy in (0, 1) for dc in (0, 1)][1:]
        copies = [pltpu.make_async_remote_copy(
            src_ref=stage_ref.at[me], dst_ref=stage_ref.at[me],
            send_sem=send_sems.at[k], recv_sem=recv_sems.at[k], device_id=peer, device_id_type=MESH)
            for k, peer in enumerate(peers)]
        for cp in copies:
            cp.start()
        for k, (px, py, pc) in enumerate(peers):
            them = 4 * px + 2 * py + pc
            pltpu.make_async_remote_copy(
                src_ref=stage_ref.at[them], dst_ref=stage_ref.at[them],
                send_sem=send_sems.at[k], recv_sem=recv_sems.at[k], device_id=(px, py, pc),
                device_id_type=MESH).wait_recv()
        for cp in copies:
            cp.wait_send()
        total = stage_ref[0]
        for d in range(1, 8):
            total = total + stage_ref[d]
        out_ref[...] = total

    return pl.pallas_call(
        body, name="all_reduce_small",
        in_specs=[pl.BlockSpec(memory_space=pltpu.VMEM)], out_specs=pl.BlockSpec(memory_space=pltpu.VMEM),
        out_shape=jax.ShapeDtypeStruct((r, c), jnp.float32),
        scratch_shapes=[pltpu.VMEM((8, r, c), jnp.float32), pltpu.SemaphoreType.DMA((7,)),
                        pltpu.SemaphoreType.DMA((7,))],
        compiler_params=pltpu.CompilerParams(has_side_effects=True),
    )(block)


def _cols_from_shards(g):
    n, r, c = g.shape
    return jnp.transpose(g, (1, 0, 2)).reshape(r, n * c)


def _cols_to_shards(w, n=4):
    r, c = w.shape
    return jnp.transpose(w.reshape(r, n, c // n), (1, 0, 2))


def _pad_w_in(full):
    d = full.shape[0]
    zeros = jnp.zeros((d, COL_CKV - COL_KR - QK_ROPE), full.dtype)
    return jnp.concatenate([full[:, :COL_KR], full[:, IN_WIDTH - QK_ROPE:], zeros,
                            full[:, COL_KR:COL_KR + KV_RANK]], axis=1)


def _unpad_w_in(padded):
    return jnp.concatenate([padded[:, :COL_KR], padded[:, COL_CKV:COL_CKV + KV_RANK],
                            padded[:, COL_KR:COL_KR + QK_ROPE]], axis=1)


def _pad_w_uq(full):
    r = full.shape[0]
    per_head = full.reshape(r, N_HEADS, QK_NOPE + QK_ROPE)
    return jnp.pad(per_head, ((0, 0), (0, 0), (0, HEAD_PAD - QK_NOPE - QK_ROPE))).reshape(r, N_HEADS * HEAD_PAD)


def _unpad_w_uq(padded):
    r = padded.shape[0]
    return padded.reshape(r, N_HEADS, HEAD_PAD)[:, :, :QK_NOPE + QK_ROPE].reshape(r, N_HEADS * (QK_NOPE + QK_ROPE))


SMALL_ROWS = 16


def _pack_small(d, pre_mix, post_mix, pre_mlp, post_mlp, conv_out, attn_out, q_norm, kv_norm, conv_w):
    row = lambda *parts: jnp.pad(jnp.concatenate(parts, axis=1), ((0, 0), (0, d - sum(p.shape[1] for p in parts))))
    rows = [row(pre_mix), row(post_mix), row(pre_mlp), row(post_mlp), row(conv_out, attn_out), row(q_norm, kv_norm),
            row(conv_w[0:1]), row(conv_w[1:2]), row(conv_w[2:3])]
    return jnp.pad(jnp.concatenate(rows, axis=0), ((0, SMALL_ROWS - len(rows)), (0, 0)))


def _unpack_small(p, chip):
    cw = CONV_WIDTH // 4
    conv_w = lax.dynamic_slice(p[6:9, :CONV_WIDTH], (0, chip * cw), (3, cw))
    return dict(pre_mix_g=p[0:1], post_mix_g=p[1:2], pre_mlp_g=p[2:3], post_mlp_g=p[3:4],
                conv_out_g=p[4:5, :CONV_WIDTH], attn_out_g=p[4:5, CONV_WIDTH:2 * CONV_WIDTH],
                q_norm_g=p[5:6, :Q_RANK], kv_norm_g=p[5:6, Q_RANK:Q_RANK + KV_RANK], conv_w=conv_w[None])


def kernel(x, pre_mix_g, w_in, conv_w, q_norm_g, w_uq, kv_norm_g, w_ukv, conv_out_g, attn_out_g, w_o, post_mix_g, pre_mlp_g, w_up, w_down, post_mlp_g, loss_target, m_pre_mix_g, m_w_in, m_conv_w, m_q_norm_g, m_w_uq, m_kv_norm_g, m_w_ukv, m_conv_out_g, m_attn_out_g, m_w_o, m_post_mix_g, m_pre_mlp_g, m_w_up, m_w_down, m_post_mlp_g, v_pre_mix_g, v_w_in, v_conv_w, v_q_norm_g, v_w_uq, v_kv_norm_g, v_w_ukv, v_conv_out_g, v_attn_out_g, v_w_o, v_post_mix_g, v_pre_mlp_g, v_w_up, v_w_down, v_post_mlp_g):
    bf16 = jnp.bfloat16
    s, d = x.shape[1], x.shape[2]
    d_ff = 4 * d
    chip = 2 * lax.axis_index("x") + lax.axis_index("y")
    xs = x.reshape(s, d)
    target = loss_target.reshape(s, d)
    tm = min(256, s)
    t_attn = min(512, s)
    mt = min(1024, s)
    kt = min(2048, s)

    big = dict(w_in=w_in[0], w_uq=w_uq[0], w_ukv=w_ukv[0], w_o=w_o[0], w_up=w_up[0], w_down=w_down[0])
    names = list(big)
    pos = jnp.stack([lax.axis_index("c"), chip]).astype(jnp.int32)
    wb = {}
    half = {k: big[k].shape[0] // 2 for k in names}

    def rows(k, a, b):
        lo, n = a * half[k] // 64, (b - a) * half[k] // 64
        assert lo % 16 == 0 and n % 16 == 0 and n > 0, (k, a, b)
        return lo, n

    ici = lambda k, a=0, b=64: _GatherIci(wb[k], *rows(k, a, b))
    fwd = lambda k, a=0, b=64: _GatherForward(wb[k], *rows(k, a, b))
    near = lambda k, a=0, b=64: _GatherD2d(wb[k], *rows(k, a, b), slots=(0, 1))
    far = lambda k, a=0, b=64: _GatherD2d(wb[k], *rows(k, a, b), slots=(2,))
    wb["w_in"] = _Buf(_cast_into_slot(big["w_in"], pos, name="cast_w_in"))
    in_plan = dict(w_up=[ici("w_in", 0, 16)],
                   w_down=[ici("w_in", 16, 32), fwd("w_in", 0, 16)],
                   w_o=[ici("w_in", 32, 40), fwd("w_in", 16, 32)],
                   w_uq=[ici("w_in", 40, 48), fwd("w_in", 32, 40), near("w_in", 0, 32)],
                   w_ukv=[ici("w_in", 48, 56), fwd("w_in", 40, 48), far("w_in", 0, 32)])
    for k, jobs in in_plan.items():
        wb[k] = _Buf(_cast_into_slot(big[k], pos, name="cast_" + k, jobs=jobs))
    h1 = _rms_fwd(xs, pre_mix_g, width=d, col=0, tm=tm, name="rms_pre_mix",
                  jobs=[ici("w_in", 56, 64), fwd("w_in", 48, 56), near("w_in", 32, 48), far("w_in", 32, 40)])
    _comm("gather_w_in", [[fwd("w_in", 56, 64), near("w_in", 48, 64), far("w_in", 40, 56)], [far("w_in", 56, 64)]])
    win = _pad_w_in(_cols_from_shards(wb["w_in"].arr))
    ff4 = d_ff // 4

    spread = lambda a: lax.dynamic_update_slice(jnp.zeros((3, CONV_WIDTH), jnp.float32), a[0],
                                                (0, chip * (CONV_WIDTH // 4)))
    conv_w_mine = jnp.where(lax.axis_index("c") == 0, spread(conv_w), 0.0)
    conv_w_full = _all_reduce_small(jnp.pad(conv_w_mine, ((0, 5), (0, 0))))[0:3]

    proj = _matmul(h1, win, dims=_NN, mnk=(s, IN_PAD, d), tiles=(mt, IN_PAD // 3, d), name="mm_proj",
                   jobs=[ici("w_uq"), ici("w_ukv"), ici("w_o")])
    y_conv = _conv_fwd(proj, conv_w_full, conv_out_g,
                       jobs=[fwd("w_uq"), fwd("w_ukv"), fwd("w_o"), near("w_uq"), near("w_ukv"), near("w_o")])
    cqn = _rms_fwd(proj, q_norm_g, width=Q_RANK, col=COL_CQ // Q_RANK, tm=tm, name="rms_q",
                   jobs=[far("w_uq"), far("w_ukv"), far("w_o"), ici("w_up", 0, 4)])
    wuq = _pad_w_uq(_cols_from_shards(wb["w_uq"].arr))
    wukv = _cols_from_shards(wb["w_ukv"].arr)
    wo = wb["w_o"].arr.reshape(-1, d)
    ckvn = _rms_fwd(proj, kv_norm_g, width=KV_RANK, col=COL_CKV // KV_RANK, tm=tm, name="rms_kv",
                    jobs=[ici("w_up", 4, 10), fwd("w_up", 0, 4)])
    q_pad = _matmul(cqn, wuq, dims=_NN, mnk=(s, N_HEADS * HEAD_PAD, Q_RANK), tiles=(mt, 1024, Q_RANK), name="mm_q",
                    jobs=[ici("w_up", 10, 18), fwd("w_up", 4, 10), near("w_up", 0, 4)])
    kv = _matmul(ckvn, wukv, dims=_NN, mnk=(s, N_HEADS * HEAD_PAD, KV_RANK), tiles=(mt, 1024, KV_RANK),
                 name="mm_kv", out_dtype=bf16,
                 jobs=[ici("w_up", 18, 24), fwd("w_up", 10, 18), near("w_up", 4, 10), far("w_up", 0, 4)])
    cos, sin = _rope_tables(s)
    q_rot, kr_rot = _qk_rope_fwd(
        q_pad, proj, cos, sin, tm=tm,
        jobs=[ici("w_up", 24, 32), fwd("w_up", 18, 24), near("w_up", 10, 18), far("w_up", 4, 10)])
    o, lse, y_attn = _attn_fwd(
        q_rot, kv, kr_rot, attn_out_g, t=t_attn,
        jobs=[ici("w_up", 32, 64), ici("w_down", 0, 8), fwd("w_up", 24, 32), near("w_up", 18, 24), far("w_up", 10, 18)])
    ycat = jnp.concatenate([y_conv, y_attn], axis=1)
    mix = _matmul(ycat, wo, dims=_NN, mnk=(s, d, 2 * CONV_WIDTH), tiles=(mt, 1024, 2 * CONV_WIDTH), name="mm_out",
                  jobs=[fwd("w_up", 32, 64), near("w_up", 24, 32), far("w_up", 18, 24)])
    x2, h2 = _mix_residual_fwd(
        xs, mix, post_mix_g, pre_mlp_g, tm=tm,
        jobs=[ici("w_down", 8, 20), fwd("w_down", 0, 8), near("w_up", 32, 64), far("w_up", 24, 64)])
    wup = wb["w_up"].arr

    def up_epilogue(acc, extra_refs, out_refs):
        r = jnp.maximum(acc, 0.0)
        out_refs[0][...] = acc.astype(bf16)
        out_refs[1][...] = (r * r).astype(bf16)

    n_ff = ff4 // 1024
    act = jax.ShapeDtypeStruct((s, d_ff), bf16)
    up, act_sq = _matmul(
        h2, wup, dims=_NN, mnk=(s, d_ff, d), tiles=(mt, 1024, d), name="mm_up",
        b_spec=pl.BlockSpec((None, d, 1024), lambda i, j, l: (j // n_ff, l, j % n_ff)),
        out_shape=(act, act), o_spec=(pl.BlockSpec((mt, 1024), lambda i, j, l: (i, j)),) * 2, epilogue=up_epilogue,
        jobs=[ici("w_down", 20, 64), fwd("w_down", 8, 20), near("w_down", 0, 8)])
    _comm("gather_w_down_tail", [[fwd("w_down", 20, 64), near("w_down", 8, 64), far("w_down", 0, 20)],
                                 [far("w_down", 20, 64)]])
    wdown = wb["w_down"].arr.reshape(d_ff, d)
    mlp = _matmul(act_sq, wdown, dims=_NN, mnk=(s, d, d_ff), tiles=(mt, 1024, 2048), name="mm_down")
    dx3, dmlp, dg_post_mlp, loss_part = _loss_head(x2, mlp, target, post_mlp_g, tm=tm)

    def dup_epilogue(acc, extra_refs, out_refs):
        out_refs[0][...] = (acc * (2.0 * jnp.maximum(extra_refs[0][...].astype(jnp.float32), 0.0))).astype(bf16)

    grads, theirs, pair_sums, by_source, whole = {}, {}, {}, {}, {}

    def exchange(k, g):
        grads[k] = g
        theirs[k] = _Buf(jax.ShapeDtypeStruct((4, g.shape[1] // 2, g.shape[2]), bf16))
        return _PairExchange(g, theirs[k])

    def pair_sum(k, jobs=()):
        pair_sums[k] = _pair_add(grads[k], theirs[k].arr, pos, name="pair_add_" + k, jobs=jobs)
        by_source[k] = _Buf(jax.ShapeDtypeStruct(pair_sums[k].shape, bf16))

    scatter = lambda k, a=0, b=64: _ScatterIci(pair_sums[k], by_source[k], *rows(k, a, b))

    def share(k):
        whole[k] = _Buf(_chip_sum(by_source[k].arr, pair_sums[k], pos, name="chip_sum_" + k))
        return _PairShare(whole[k])

    g_wdown = _matmul(act_sq, dmlp, dims=_TN, mnk=(d_ff, d, s), tiles=(1024, 1024, kt), name="mm_gw_down",
                      out_dtype=bf16).reshape(4, ff4, d)
    dup = _matmul(dmlp, wdown, dims=_NT, mnk=(s, d_ff, d), tiles=(mt, 1024, d), name="mm_dact",
                  out_dtype=bf16, epilogue=dup_epilogue, extra=(up,),
                  extra_specs=(pl.BlockSpec((mt, 1024), lambda i, j, l: (i, j)),),
                  jobs=[exchange("w_down", g_wdown)])
    pair_sum("w_down")
    g_wup = _matmul(h2, dup, dims=_TN, mnk=(d, d_ff, s), tiles=(1024, 1024, kt), name="mm_gw_up",
                    out_shape=jax.ShapeDtypeStruct((4, d, ff4), bf16),
                    o_spec=pl.BlockSpec((None, 1024, 1024), lambda i, j, l: (j // n_ff, i, j % n_ff)),
                    jobs=[scatter("w_down", 0, 24)])
    dh2 = _matmul(dup, wup, dims=_NT, mnk=(s, d, d_ff), tiles=(mt, 1024, ff4), name="mm_dh2",
                  b_spec=pl.BlockSpec((None, 1024, ff4), lambda i, j, l: (l, j, 0)),
                  jobs=[exchange("w_up", g_wup), scatter("w_down", 24, 49)])
    pair_sum("w_up", jobs=[scatter("w_down", 49, 55)])
    dx2, dmix, dg_pre_mlp, dg_post_mix = _mix_residual_bwd(
        dx3, dh2, x2, mix, pre_mlp_g, post_mix_g, tm=tm, jobs=[scatter("w_down", 55, 64)])

    dycat = _matmul(dmix, wo, dims=_NT, mnk=(s, 2 * CONV_WIDTH, d), tiles=(mt, 1024, d), name="mm_dycat",
                    jobs=[scatter("w_up", 0, 6)])
    g_wo = _matmul(ycat, dmix, dims=_TN, mnk=(2 * CONV_WIDTH, d, s), tiles=(1024, 1024, kt),
                   name="mm_gw_out", out_dtype=bf16, jobs=[scatter("w_up", 6, 12)]).reshape(4, CONV_WIDTH // 2, d)
    du, dgb, dgc, dg_conv_w, dg_conv_out = _conv_bwd(proj, conv_w_full, conv_out_g, dycat,
                                                     jobs=[exchange("w_o", g_wo), scatter("w_up", 12, 18)])
    pair_sum("w_o")
    do, delta, dg_attn_out = _attn_norm_bwd(o, attn_out_g, dycat, jobs=[scatter("w_up", 18, 22)])
    dq_pad, dk_pad, dv = _attn_bwd(q_rot, kv, kr_rot, do, lse, delta, t=t_attn,
                                   jobs=[scatter("w_up", 22, 44), share("w_down")])
    dq_raw, dkv, dkr = _qk_rope_bwd(dq_pad, dk_pad, dv, cos, sin, tm=tm, jobs=[scatter("w_up", 44, 51)])
    wq_cols = N_HEADS * HEAD_PAD
    g_wuq = _matmul(cqn, dq_raw, dims=_TN, mnk=(Q_RANK, wq_cols, s), tiles=(Q_RANK, 1024, kt),
                    name="mm_gw_uq", out_dtype=bf16, jobs=[scatter("w_up", 51, 54)])
    dcqn = _matmul(dq_raw, wuq, dims=_NT, mnk=(s, Q_RANK, wq_cols), tiles=(mt, Q_RANK, wq_cols), name="mm_dcq",
                   jobs=[scatter("w_up", 54, 57)])
    g_wukv = _matmul(ckvn, dkv, dims=_TN, mnk=(KV_RANK, wq_cols, s), tiles=(KV_RANK, 1024, kt),
                     name="mm_gw_ukv", out_dtype=bf16, jobs=[scatter("w_up", 57, 59)])
    dckvn = _matmul(dkv, wukv, dims=_NT, mnk=(s, KV_RANK, wq_cols), tiles=(mt, KV_RANK, wq_cols), name="mm_dckv",
                    jobs=[scatter("w_up", 59, 61), exchange("w_uq", _cols_to_shards(_unpad_w_uq(g_wuq))),
                          exchange("w_ukv", _cols_to_shards(g_wukv))])
    pair_sum("w_uq")
    pair_sum("w_ukv")
    dcq, dg_q_norm = _rms_bwd(proj, q_norm_g, dcqn, width=Q_RANK, col=COL_CQ // Q_RANK, tm=tm, name="rms_q_bwd",
                              jobs=[scatter("w_up", 61, 64)])
    dckv, dg_kv_norm = _rms_bwd(proj, kv_norm_g, dckvn, width=KV_RANK, col=COL_CKV // KV_RANK, tm=tm,
                                name="rms_kv_bwd", jobs=[scatter("w_o", 0, 12)])
    dproj = jnp.concatenate([du, dgb, dgc, dcq, dkr, jnp.zeros((s, COL_CKV - COL_KR - 128), bf16), dckv], axis=1)
    g_win = _matmul(h1, dproj, dims=_TN, mnk=(d, IN_PAD, s), tiles=(1024, IN_PAD // 3, kt), name="mm_gw_in",
                    out_dtype=bf16, jobs=[scatter("w_o", 12, 64), share("w_up")])
    _comm("pair_exchange_w_in", [[exchange("w_in", _cols_to_shards(_unpad_w_in(g_win)))]])
    pair_sum("w_in", jobs=[scatter("w_uq"), scatter("w_ukv")])
    dh1 = _matmul(dproj, win, dims=_NT, mnk=(s, d, IN_PAD), tiles=(mt, 1024, IN_PAD // 2), name="mm_dh1",
                  jobs=[scatter("w_in", 0, 34), share("w_o")])
    grad_x, dg_pre_mix = _input_bwd(dx2, dh1, xs, pre_mix_g, tm=tm,
                                    jobs=[scatter("w_in", 34, 64), share("w_uq"), share("w_ukv")])
    _comm("pair_share_w_in", [[share("w_in")]])

    moments = dict(w_in=(m_w_in, v_w_in), w_uq=(m_w_uq, v_w_uq), w_ukv=(m_w_ukv, v_w_ukv), w_o=(m_w_o, v_w_o),
                   w_up=(m_w_up, v_w_up), w_down=(m_w_down, v_w_down))
    gw, dw, nm, nv = {}, {}, {}, {}
    for k in names:
        delta_k, nm_k, nv_k, g = _adamw(big[k], whole[k].arr, moments[k][0][0], moments[k][1][0], name="adamw_" + k)
        gw[k], dw[k], nm[k], nv[k] = g[None], delta_k[None], nm_k[None], nv_k[None]

    small_g = _all_reduce_small(_pack_small(d, dg_pre_mix, dg_post_mix, dg_pre_mlp, dg_post_mlp, dg_conv_out,
                                            dg_attn_out, dg_q_norm, dg_kv_norm, dg_conv_w))
    pack_w = lambda cw, pre_mix, post_mix, pre_mlp, post_mlp, conv_out, attn_out, q_norm, kv_norm: _pack_small(
        d, pre_mix, post_mix, pre_mlp, post_mlp, conv_out, attn_out, q_norm, kv_norm, cw)
    small_w = pack_w(conv_w_full, pre_mix_g, post_mix_g, pre_mlp_g, post_mlp_g, conv_out_g, attn_out_g, q_norm_g,
                     kv_norm_g)
    small_m = pack_w(spread(m_conv_w), m_pre_mix_g, m_post_mix_g, m_pre_mlp_g, m_post_mlp_g, m_conv_out_g,
                     m_attn_out_g, m_q_norm_g, m_kv_norm_g)
    small_v = pack_w(spread(v_conv_w), v_pre_mix_g, v_post_mix_g, v_pre_mlp_g, v_post_mlp_g, v_conv_out_g,
                     v_attn_out_g, v_q_norm_g, v_kv_norm_g)
    small_d, small_nm, small_nv, small_g = _adamw(small_w, small_g, small_m, small_v, name="adamw_small")
    sg, sd, snm, snv = (_unpack_small(p, chip) for p in (small_g, small_d, small_nm, small_nv))

    for src, dst in ((sg, gw), (sd, dw), (snm, nm), (snv, nv)):
        dst.update(src)

    loss = lax.psum(loss_part[0, 0], ("x", "y", "c"))
    order = ["pre_mix_g", "w_in", "conv_w", "q_norm_g", "w_uq", "kv_norm_g", "w_ukv", "conv_out_g", "attn_out_g",
             "w_o", "post_mix_g", "pre_mlp_g", "w_up", "w_down", "post_mlp_g"]
    return (loss, grad_x.reshape(1, s, d), *[gw[k] for k in order], *[dw[k] for k in order],
            *[nm[k] for k in order], *[nv[k] for k in order])
```

```python
import functools

import jax
import jax.numpy as jnp
from jax import lax
from jax.experimental import pallas as pl
from jax.experimental.pallas import tpu as pltpu

EPS = 1e-6
NEG_INF = -1e30
CHUNK_SHIFT = 6
N_HEADS = 8
HEAD_PAD = 256
QK_NOPE = 128
QK_ROPE = 64
V_DIM = 128
CONV_WIDTH = 1024
Q_RANK = 768
KV_RANK = 512
ROPE_THETA = 10000.0
ATTN_SCALE = (QK_NOPE + QK_ROPE) ** -0.5
ADAM_LR, ADAM_B1, ADAM_B2, ADAM_EPS, ADAM_WD, ADAM_STEP = 0.001, 0.9, 0.999, 1e-08, 0.01, 10

COL_CQ = 3 * CONV_WIDTH
COL_KR = COL_CQ + Q_RANK
COL_CKV = 4096
IN_PAD = COL_CKV + KV_RANK
IN_WIDTH = 3 * CONV_WIDTH + Q_RANK + KV_RANK + QK_ROPE

VMEM_LIMIT_BYTES = 56 * 1024 * 1024
MESH = pl.DeviceIdType.MESH
ANY = pl.BlockSpec(memory_space=pl.ANY)

_NN = (((1,), (0,)), ((), ()))
_NT = (((1,), (1,)), ((), ()))
_TN = (((0,), (0,)), ((), ()))


def _params(sem):
    return pltpu.CompilerParams(dimension_semantics=sem, vmem_limit_bytes=VMEM_LIMIT_BYTES)


class _Buf:
    def __init__(self, arr):
        self.arr = arr


def _position():
    return lax.axis_index("x"), lax.axis_index("y"), lax.axis_index("c")


def _other_chips(x, y):
    return [(2 * (1 - x) + y, (1 - x, y)), (2 * x + (1 - y), (x, 1 - y)), (2 * (1 - x) + (1 - y), (1 - x, 1 - y))]


def _remote(src, dst, sems, k, to):
    send, recv, off = sems
    return pltpu.make_async_remote_copy(src_ref=src, dst_ref=dst, send_sem=send.at[off + k], recv_sem=recv.at[off + k],
                                        device_id=to, device_id_type=MESH)


class _GatherIci:
    n_sems = 2

    def __init__(self, buf, lo, n):
        self.buf, self.lo, self.n, self.buffers, self.sources = buf, lo, n, [buf], []

    def _rows(self, ref, slot, which, lo=None, n=None):
        lo, n = (self.lo, self.n) if lo is None else (lo, n)
        return ref.at[slot, pl.ds(which * (ref.shape[1] // 2) + lo, n), :]

    def start(self, refs, sems):
        ref = refs[id(self.buf)]
        x, y, c = _position()
        mine = self._rows(ref, 2 * x + y, c)
        for k, (_, xy) in enumerate(_other_chips(x, y)[:2]):
            _remote(mine, mine, sems, k, (*xy, c)).start()

    def wait(self, refs, sems):
        ref = refs[id(self.buf)]
        x, y, c = _position()
        mine = self._rows(ref, 2 * x + y, c)
        for k, (slot, xy) in enumerate(_other_chips(x, y)[:2]):
            landed = self._rows(ref, slot, c)
            _remote(landed, landed, sems, k, (*xy, c)).wait_recv()
            _remote(mine, mine, sems, k, (*xy, c)).wait_send()


class _GatherForward(_GatherIci):
    def _ways(self, x, y):
        (slot_x, xy_x), (slot_y, xy_y), (slot_d, _) = _other_chips(x, y)
        h = self.n // 2
        assert h % 16 == 0, self.n
        return [(slot_x, slot_d, self.lo, xy_y), (slot_y, slot_d, self.lo + h, xy_x)], h

    def start(self, refs, sems):
        ref = refs[id(self.buf)]
        x, y, c = _position()
        ways, h = self._ways(x, y)
        for k, (slot, _, lo, xy) in enumerate(ways):
            rows = self._rows(ref, slot, c, lo, h)
            _remote(rows, rows, sems, k, (*xy, c)).start()

    def wait(self, refs, sems):
        ref = refs[id(self.buf)]
        x, y, c = _position()
        ways, h = self._ways(x, y)
        for k, (slot, lands, lo, xy) in enumerate(ways):
            landed, sent = self._rows(ref, lands, c, lo, h), self._rows(ref, slot, c, lo, h)
            _remote(landed, landed, sems, k, (*xy, c)).wait_recv()
            _remote(sent, sent, sems, k, (*xy, c)).wait_send()


class _GatherD2d(_GatherIci):
    def __init__(self, buf, lo, n, slots):
        super().__init__(buf, lo, n)
        self.slots, self.n_sems = slots, len(slots)

    def start(self, refs, sems):
        ref = refs[id(self.buf)]
        x, y, c = _position()
        chips = _other_chips(x, y)
        for k, which in enumerate(self.slots):
            rows = self._rows(ref, chips[which][0], c)
            _remote(rows, rows, sems, k, (x, y, 1 - c)).start()

    def wait(self, refs, sems):
        ref = refs[id(self.buf)]
        x, y, c = _position()
        chips = _other_chips(x, y)
        for k, which in enumerate(self.slots):
            sent, landed = self._rows(ref, chips[which][0], c), self._rows(ref, chips[which][0], 1 - c)
            _remote(landed, landed, sems, k, (x, y, 1 - c)).wait_recv()
            _remote(sent, sent, sems, k, (x, y, 1 - c)).wait_send()


class _ScatterDiag:
    n_sems = 2

    def __init__(self, src, via, lo, n):
        self.src, self.via, self.lo, self.n, self.buffers, self.sources = src, via, lo, n, [via], [src]

    def _copies(self, refs, sems):
        src, via = refs[id(self.src)], refs[id(self.via)]
        x, y, c = _position()
        (_, xy_x), (_, xy_y), (slot_d, _) = _other_chips(x, y)
        h2 = via.shape[0] // 2
        return [_remote(src.at[slot_d, pl.ds(first + self.lo, self.n), :], via.at[pl.ds(first + self.lo, self.n), :],
                        sems, k, (*xy, c)) for k, (first, xy) in enumerate(((0, xy_x), (h2, xy_y)))]

    def start(self, refs, sems):
        for cp in self._copies(refs, sems):
            cp.start()

    def wait(self, refs, sems):
        for cp in self._copies(refs, sems):
            cp.wait_recv()
            cp.wait_send()


class _ScatterNear:
    n_sems = 2

    def __init__(self, src, dst, lo, n):
        self.src, self.dst, self.lo, self.n, self.buffers, self.sources = src, dst, lo, n, [dst], [src]

    def _copies(self, refs, sems, landing):
        src, dst = refs[id(self.src)], refs[id(self.dst)]
        x, y, c = _position()
        rows = pl.ds(self.lo, self.n)
        return [_remote(src.at[k, rows, :], dst.at[slot if landing else 2 * x + y, rows, :], sems, k, (*xy, c))
                for k, (slot, xy) in enumerate(_other_chips(x, y)[:2])]

    def start(self, refs, sems):
        for cp in self._copies(refs, sems, False):
            cp.start()

    def wait(self, refs, sems):
        for cp in self._copies(refs, sems, True):
            cp.wait_recv()
            cp.wait_send()


class _PairExchange:
    n_sems = 1

    def __init__(self, src, dst):
        self.src, self.dst, self.buffers, self.sources = src, dst, [dst], [src]

    def _copy(self, refs, sems):
        src, dst = refs[id(self.src)], refs[id(self.dst)]
        x, y, c = _position()
        h = src.shape[1] // 2
        return _remote(src.at[:, pl.ds((1 - c) * h, h), :], dst, sems, 0, (x, y, 1 - c))

    def start(self, refs, sems):
        self._copy(refs, sems).start()

    def wait(self, refs, sems):
        self._copy(refs, sems).wait()


class _PairShare:
    n_sems = 1

    def __init__(self, buf):
        self.buf, self.buffers, self.sources = buf, [buf], []

    def _rows(self, ref, which):
        h = ref.shape[0] // 2
        return ref.at[pl.ds(which * h, h), :]

    def start(self, refs, sems):
        ref = refs[id(self.buf)]
        x, y, c = _position()
        _remote(self._rows(ref, c), self._rows(ref, c), sems, 0, (x, y, 1 - c)).start()

    def wait(self, refs, sems):
        ref = refs[id(self.buf)]
        x, y, c = _position()
        _remote(self._rows(ref, c), self._rows(ref, c), sems, 0, (x, y, 1 - c)).wait_send()
        _remote(self._rows(ref, 1 - c), self._rows(ref, 1 - c), sems, 0, (x, y, 1 - c)).wait_recv()


def _unique(items):
    seen, out = set(), []
    for it in items:
        if id(it) not in seen:
            seen.add(id(it))
            out.append(it)
    return out


def _job_operands(jobs):
    sources = _unique([a for j in jobs for a in j.sources])
    buffers = _unique([b for j in jobs for b in j.buffers])
    held = [b for b in buffers if not isinstance(b.arr, jax.ShapeDtypeStruct)]
    fresh = [b for b in buffers if isinstance(b.arr, jax.ShapeDtypeStruct)]
    return sources, held, fresh


def _sem_offsets(jobs):
    offs, total = [], 0
    for j in jobs:
        offs.append(total)
        total += j.n_sems
    return offs, total


def _call(body, *, name, grid, in_specs, out_specs, out_shape, args, semantics, scratch_shapes=(), jobs=(),
          prefetch=None):
    n_pre = 0 if prefetch is None else 1

    def launch(fn, in_specs, out_specs, scratch, **kw):
        if prefetch is None:
            return pl.pallas_call(fn, name=name, grid=grid, in_specs=in_specs, out_specs=out_specs,
                                  scratch_shapes=scratch, **kw)
        return pl.pallas_call(fn, name=name, grid_spec=pltpu.PrefetchScalarGridSpec(
            num_scalar_prefetch=1, grid=grid, in_specs=in_specs, out_specs=out_specs, scratch_shapes=scratch), **kw)

    pre = () if prefetch is None else (prefetch,)
    if not jobs:
        return launch(body, list(in_specs), list(out_specs), list(scratch_shapes), out_shape=list(out_shape),
                      compiler_params=_params(semantics))(*pre, *args)
    sources, held, fresh = _job_operands(jobs)
    offs, n_sem = _sem_offsets(jobs)
    n_in, n_out, n_scr = len(in_specs), len(out_specs), len(scratch_shapes)
    n_src, n_held, n_fresh = len(sources), len(held), len(fresh)

    def carried(*refs):
        pre_refs, refs = refs[:n_pre], refs[n_pre:]
        ins = refs[:n_in]
        src_refs = refs[n_in:n_in + n_src]
        o0 = n_in + n_src + n_held
        outs = refs[o0:o0 + n_out]
        buf_refs = refs[o0 + n_out:o0 + n_out + n_held + n_fresh]
        s0 = o0 + n_out + n_held + n_fresh
        scratch = refs[s0:s0 + n_scr]
        send, recv = refs[s0 + n_scr], refs[s0 + n_scr + 1]
        where = {id(a): r for a, r in zip(sources, src_refs)}
        where.update({id(b): r for b, r in zip(held + fresh, buf_refs)})
        ids = [pl.program_id(a) for a in range(len(grid))]
        first = functools.reduce(jnp.logical_and, [i == 0 for i in ids])
        last = functools.reduce(jnp.logical_and, [i == g - 1 for i, g in zip(ids, grid)])

        @pl.when(first)
        def _():
            for j, off in zip(jobs, offs):
                j.start(where, (send, recv, off))

        body(*pre_refs, *ins, *outs, *scratch)

        @pl.when(last)
        def _():
            for j, off in zip(jobs, offs):
                j.wait(where, (send, recv, off))

    shape_of = lambda b: jax.ShapeDtypeStruct(b.arr.shape, b.arr.dtype)
    res = launch(
        carried, [*in_specs, *[ANY] * (n_src + n_held)], [*out_specs, *[ANY] * (n_held + n_fresh)],
        [*scratch_shapes, pltpu.SemaphoreType.DMA((n_sem,)), pltpu.SemaphoreType.DMA((n_sem,))],
        out_shape=[*out_shape, *[shape_of(b) for b in held + fresh]],
        input_output_aliases={n_pre + n_in + n_src + i: n_out + i for i in range(n_held)},
        compiler_params=pltpu.CompilerParams(dimension_semantics=("arbitrary",) * len(grid),
                                             vmem_limit_bytes=VMEM_LIMIT_BYTES, has_side_effects=True),
    )(*pre, *args, *sources, *[b.arr for b in held])
    for b, new in zip(held + fresh, res[n_out:]):
        b.arr = new
    return list(res[:n_out])


def _comm(name, phases):
    jobs = [j for ph in phases for j in ph]
    sources, held, fresh = _job_operands(jobs)
    offs, n_sem = _sem_offsets(jobs)
    off_of = {id(j): o for j, o in zip(jobs, offs)}
    n_src, n_held, n_fresh = len(sources), len(held), len(fresh)

    def body(*refs):
        src_refs = refs[:n_src]
        buf_refs = refs[n_src + n_held:n_src + 2 * n_held + n_fresh]
        send, recv = refs[-2], refs[-1]
        where = {id(a): r for a, r in zip(sources, src_refs)}
        where.update({id(b): r for b, r in zip(held + fresh, buf_refs)})
        for ph in phases:
            for j in ph:
                j.start(where, (send, recv, off_of[id(j)]))
            for j in ph:
                j.wait(where, (send, recv, off_of[id(j)]))

    shape_of = lambda b: jax.ShapeDtypeStruct(b.arr.shape, b.arr.dtype)
    res = pl.pallas_call(
        body, name=name,
        in_specs=[ANY] * (n_src + n_held), out_specs=[ANY] * (n_held + n_fresh),
        out_shape=[shape_of(b) for b in held + fresh],
        input_output_aliases={n_src + i: i for i in range(n_held)},
        scratch_shapes=[pltpu.SemaphoreType.DMA((n_sem,)), pltpu.SemaphoreType.DMA((n_sem,))],
        compiler_params=pltpu.CompilerParams(has_side_effects=True),
    )(*sources, *[b.arr for b in held])
    for b, new in zip(held + fresh, res):
        b.arr = new


def _matmul(a, b, *, dims, mnk, tiles, name, out_dtype=jnp.float32, a_spec=None, b_spec=None,
            out_shape=None, o_spec=None, epilogue=None, extra=(), extra_specs=(), jobs=()):
    m, n, k = mnk
    tm, tn, tk = tiles
    assert m % tm == 0 and n % tn == 0 and k % tk == 0, (name, mnk, tiles)
    gm, gn, gk = m // tm, n // tn, k // tk
    if a_spec is None:
        a_spec = (pl.BlockSpec((tk, tm), lambda i, j, l: (l, i)) if dims is _TN
                  else pl.BlockSpec((tm, tk), lambda i, j, l: (i, l)))
    if b_spec is None:
        b_spec = (pl.BlockSpec((tn, tk), lambda i, j, l: (j, l)) if dims is _NT
                  else pl.BlockSpec((tk, tn), lambda i, j, l: (l, j)))
    if out_shape is None:
        out_shape = jax.ShapeDtypeStruct((m, n), out_dtype)
    if o_spec is None:
        o_spec = pl.BlockSpec((tm, tn), lambda i, j, l: (i, j))
    single = not isinstance(out_shape, (tuple, list))
    n_extra = len(extra)

    def finish(acc, extra_refs, out_refs):
        if epilogue is None:
            out_refs[0][...] = acc.astype(out_refs[0].dtype)
        else:
            epilogue(acc, extra_refs, out_refs)

    def body_whole_k(*refs):
        a_ref, b_ref = refs[0], refs[1]
        acc = lax.dot_general(a_ref[...], b_ref[...], dims, preferred_element_type=jnp.float32)
        finish(acc, refs[2:2 + n_extra], refs[2 + n_extra:])

    def body_split_k(*refs):
        a_ref, b_ref = refs[0], refs[1]
        extra_refs = refs[2:2 + n_extra]
        out_refs = refs[2 + n_extra:-1]
        acc_ref = refs[-1]
        step = pl.program_id(2)
        part = lax.dot_general(a_ref[...], b_ref[...], dims, preferred_element_type=jnp.float32)

        @pl.when(step == 0)
        def _():
            acc_ref[...] = part

        @pl.when(jnp.logical_and(step > 0, step < gk - 1))
        def _():
            acc_ref[...] += part

        @pl.when(step == gk - 1)
        def _():
            finish(acc_ref[...] + part, extra_refs, out_refs)

    res = _call(
        body_whole_k if gk == 1 else body_split_k, name=name, grid=(gm, gn, gk),
        in_specs=[a_spec, b_spec, *extra_specs],
        out_specs=[o_spec] if single else list(o_spec),
        out_shape=[out_shape] if single else list(out_shape),
        scratch_shapes=[] if gk == 1 else [pltpu.VMEM((tm, tn), jnp.float32)],
        semantics=("parallel", "parallel", "arbitrary"), args=(a, b, *extra), jobs=jobs)
    return res[0] if single else res


def _rstd(x):
    return lax.rsqrt(jnp.mean(x * x, axis=-1, keepdims=True) + EPS)


def _rms_bwd_rows(x, g, dy):
    r = _rstd(x)
    xn = x * r
    dyg = dy * g
    dx = r * (dyg - xn * jnp.mean(xn * dyg, axis=-1, keepdims=True))
    return dx, dy * xn


def _acc_rows(ref, rows, first):
    part = jnp.sum(rows, axis=0, keepdims=True)

    @pl.when(first)
    def _():
        ref[...] = part

    @pl.when(jnp.logical_not(first))
    def _():
        ref[...] += part


def _rms_fwd(x, g, *, width, col, tm, name, jobs=()):
    s = x.shape[0]

    def body(x_ref, g_ref, o_ref):
        v = x_ref[...]
        o_ref[...] = (v * _rstd(v) * g_ref[...]).astype(o_ref.dtype)

    return _call(
        body, name=name, grid=(s // tm,),
        in_specs=[pl.BlockSpec((tm, width), lambda i: (i, col)), pl.BlockSpec((1, width), lambda i: (0, 0))],
        out_specs=[pl.BlockSpec((tm, width), lambda i: (i, 0))],
        out_shape=[jax.ShapeDtypeStruct((s, width), jnp.bfloat16)],
        semantics=("parallel",), args=(x, g), jobs=jobs)[0]


def _rms_bwd(x, g, dy, *, width, col, tm, name, jobs=()):
    s = x.shape[0]

    def body(x_ref, g_ref, dy_ref, dx_ref, dg_ref):
        dx, dgr = _rms_bwd_rows(x_ref[...], g_ref[...], dy_ref[...])
        dx_ref[...] = dx.astype(dx_ref.dtype)
        _acc_rows(dg_ref, dgr, pl.program_id(0) == 0)

    return _call(
        body, name=name, grid=(s // tm,),
        in_specs=[pl.BlockSpec((tm, width), lambda i: (i, col)), pl.BlockSpec((1, width), lambda i: (0, 0)),
                  pl.BlockSpec((tm, width), lambda i: (i, 0))],
        out_specs=[pl.BlockSpec((tm, width), lambda i: (i, 0)), pl.BlockSpec((1, width), lambda i: (0, 0))],
        out_shape=[jax.ShapeDtypeStruct((s, width), jnp.bfloat16), jax.ShapeDtypeStruct((1, width), jnp.float32)],
        semantics=("arbitrary",), args=(x, g, dy), jobs=jobs)


def _row_specs(tm, d, n):
    return [pl.BlockSpec((tm, d), lambda i: (i, 0)) for _ in range(n)]


def _gain_specs(d, n):
    return [pl.BlockSpec((1, d), lambda i: (0, 0)) for _ in range(n)]


def _mix_residual_fwd(x, mix, g_post_mix, g_pre_mlp, *, tm, jobs=()):
    s, d = x.shape

    def body(x_ref, mix_ref, g1_ref, g2_ref, x2_ref, h2_ref):
        mixv = mix_ref[...]
        x2 = x_ref[...] + mixv * _rstd(mixv) * g1_ref[...]
        x2_ref[...] = x2
        h2_ref[...] = (x2 * _rstd(x2) * g2_ref[...]).astype(h2_ref.dtype)

    return _call(
        body, name="mix_residual_fwd", grid=(s // tm,),
        in_specs=_row_specs(tm, d, 2) + _gain_specs(d, 2),
        out_specs=_row_specs(tm, d, 2),
        out_shape=[jax.ShapeDtypeStruct((s, d), jnp.float32), jax.ShapeDtypeStruct((s, d), jnp.bfloat16)],
        semantics=("parallel",), args=(x, mix, g_post_mix, g_pre_mlp), jobs=jobs)


def _loss_head(x2, mlp, target, g_post_mlp, *, tm, jobs=()):
    s, d = x2.shape

    def body(x2_ref, m_ref, t_ref, g_ref, dx3_ref, dm_ref, dg_ref, loss_ref):
        first = pl.program_id(0) == 0
        mv = m_ref[...]
        g = g_ref[...]
        diff = x2_ref[...] + mv * _rstd(mv) * g - t_ref[...]
        dx3 = diff * (1.0 / d)
        dx3_ref[...] = dx3
        dm, dgr = _rms_bwd_rows(mv, g, dx3)
        dm_ref[...] = dm.astype(dm_ref.dtype)
        _acc_rows(dg_ref, dgr, first)
        part = 0.5 * jnp.sum(jnp.mean(diff * diff, axis=-1, keepdims=True), axis=0, keepdims=True)
        _acc_rows(loss_ref, jnp.broadcast_to(part, (1, 128)), first)

    return _call(
        body, name="loss_head", grid=(s // tm,),
        in_specs=_row_specs(tm, d, 3) + _gain_specs(d, 1),
        out_specs=_row_specs(tm, d, 2) + _gain_specs(d, 1) + [pl.BlockSpec((1, 128), lambda i: (0, 0))],
        out_shape=[jax.ShapeDtypeStruct((s, d), jnp.float32), jax.ShapeDtypeStruct((s, d), jnp.bfloat16),
                   jax.ShapeDtypeStruct((1, d), jnp.float32), jax.ShapeDtypeStruct((1, 128), jnp.float32)],
        semantics=("arbitrary",), args=(x2, mlp, target, g_post_mlp), jobs=jobs)


def _mix_residual_bwd(dx3, dh2, x2, mix, g_pre_mlp, g_post_mix, *, tm, jobs=()):
    s, d = x2.shape

    def body(dx3_ref, dh2_ref, x2_ref, mix_ref, g2_ref, g1_ref, dx2_ref, dmix_ref, dg2_ref, dg1_ref):
        first = pl.program_id(0) == 0
        d_in, dgr2 = _rms_bwd_rows(x2_ref[...], g2_ref[...], dh2_ref[...])
        dx2 = dx3_ref[...] + d_in
        dx2_ref[...] = dx2
        dmix, dgr1 = _rms_bwd_rows(mix_ref[...], g1_ref[...], dx2)
        dmix_ref[...] = dmix.astype(dmix_ref.dtype)
        _acc_rows(dg2_ref, dgr2, first)
        _acc_rows(dg1_ref, dgr1, first)

    return _call(
        body, name="mix_residual_bwd", grid=(s // tm,),
        in_specs=_row_specs(tm, d, 4) + _gain_specs(d, 2),
        out_specs=_row_specs(tm, d, 2) + _gain_specs(d, 2),
        out_shape=[jax.ShapeDtypeStruct((s, d), jnp.float32), jax.ShapeDtypeStruct((s, d), jnp.bfloat16),
                   jax.ShapeDtypeStruct((1, d), jnp.float32), jax.ShapeDtypeStruct((1, d), jnp.float32)],
        semantics=("arbitrary",), args=(dx3, dh2, x2, mix, g_pre_mlp, g_post_mix), jobs=jobs)


def _input_bwd(dx2, dh1, x, g_pre_mix, *, tm, jobs=()):
    s, d = x.shape

    def body(dx2_ref, dh1_ref, x_ref, g_ref, dx_ref, dg_ref):
        d_in, dgr = _rms_bwd_rows(x_ref[...], g_ref[...], dh1_ref[...])
        dx_ref[...] = dx2_ref[...] + d_in
        _acc_rows(dg_ref, dgr, pl.program_id(0) == 0)

    return _call(
        body, name="input_bwd", grid=(s // tm,),
        in_specs=_row_specs(tm, d, 3) + _gain_specs(d, 1),
        out_specs=_row_specs(tm, d, 1) + _gain_specs(d, 1),
        out_shape=[jax.ShapeDtypeStruct((s, d), jnp.float32), jax.ShapeDtypeStruct((1, d), jnp.float32)],
        semantics=("arbitrary",), args=(dx2, dh1, x, g_pre_mix), jobs=jobs)


def _shift_rows(z, by):
    s = z.shape[0]
    rows = lax.broadcasted_iota(jnp.int32, z.shape, 0)
    rolled = pltpu.roll(z, by % s, axis=0)
    keep = rows >= by if by > 0 else rows < s + by
    return jnp.where(keep, rolled, 0.0)


def _conv_fwd(proj, conv_w, conv_out_g, jobs=()):
    s = proj.shape[0]
    groups = CONV_WIDTH // 128

    def body(u_ref, gb_ref, gc_ref, w_ref, g_ref, y_ref):
        z = gc_ref[...] * u_ref[...]
        w = w_ref[...]
        conv = w[0:1, :] * _shift_rows(z, 2) + w[1:2, :] * _shift_rows(z, 1) + w[2:3, :] * z
        y = gb_ref[...] * conv
        y_ref[...] = (y * _rstd(y) * g_ref[...]).astype(y_ref.dtype)

    col = lambda base: pl.BlockSpec((s, 128), lambda j: (0, base + j))
    return _call(
        body, name="conv_fwd", grid=(groups,),
        in_specs=[col(0), col(groups), col(2 * groups), pl.BlockSpec((3, 128), lambda j: (0, j)),
                  pl.BlockSpec((1, 128), lambda j: (0, j))],
        out_specs=[pl.BlockSpec((s, 128), lambda j: (0, j))],
        out_shape=[jax.ShapeDtypeStruct((s, CONV_WIDTH), jnp.bfloat16)],
        semantics=("parallel",), args=(proj, proj, proj, conv_w, conv_out_g), jobs=jobs)[0]


def _conv_bwd(proj, conv_w, conv_out_g, dycat, jobs=()):
    s = proj.shape[0]
    groups = CONV_WIDTH // 128

    def body(u_ref, gb_ref, gc_ref, w_ref, g_ref, dy_ref, du_ref, dgb_ref, dgc_ref, dw_ref, dg_ref):
        u, gb, gc = u_ref[...], gb_ref[...], gc_ref[...]
        w = w_ref[...]
        z = gc * u
        z1, z2 = _shift_rows(z, 1), _shift_rows(z, 2)
        conv = w[0:1, :] * z2 + w[1:2, :] * z1 + w[2:3, :] * z
        dyr, dgr = _rms_bwd_rows(gb * conv, g_ref[...], dy_ref[...])
        dg_ref[...] = jnp.sum(dgr, axis=0, keepdims=True)
        dgb_ref[...] = (dyr * conv).astype(dgb_ref.dtype)
        dconv = dyr * gb
        dw_ref[0:1, :] = jnp.sum(dconv * z2, axis=0, keepdims=True)
        dw_ref[1:2, :] = jnp.sum(dconv * z1, axis=0, keepdims=True)
        dw_ref[2:3, :] = jnp.sum(dconv * z, axis=0, keepdims=True)
        dz = w[2:3, :] * dconv + w[1:2, :] * _shift_rows(dconv, -1) + w[0:1, :] * _shift_rows(dconv, -2)
        dgc_ref[...] = (dz * u).astype(dgc_ref.dtype)
        du_ref[...] = (dz * gc).astype(du_ref.dtype)

    col = lambda base: pl.BlockSpec((s, 128), lambda j: (0, base + j))
    act = jax.ShapeDtypeStruct((s, CONV_WIDTH), jnp.bfloat16)
    return _call(
        body, name="conv_bwd", grid=(groups,),
        in_specs=[col(0), col(groups), col(2 * groups), pl.BlockSpec((3, 128), lambda j: (0, j)),
                  pl.BlockSpec((1, 128), lambda j: (0, j)), col(0)],
        out_specs=[col(0), col(0), col(0), pl.BlockSpec((3, 128), lambda j: (0, j)),
                   pl.BlockSpec((1, 128), lambda j: (0, j))],
        out_shape=[act, act, act, jax.ShapeDtypeStruct((3, CONV_WIDTH), jnp.float32),
                   jax.ShapeDtypeStruct((1, CONV_WIDTH), jnp.float32)],
        semantics=("parallel",), args=(proj, proj, proj, conv_w, conv_out_g, dycat), jobs=jobs)


def _rope_tables(s):
    pos = jnp.arange(s, dtype=jnp.float32)
    inv_freq = jnp.power(ROPE_THETA, -jnp.arange(0, QK_ROPE, 2, dtype=jnp.float32) / QK_ROPE)
    ang = pos[:, None] * inv_freq[None, :]
    cos, sin = jnp.cos(ang), jnp.sin(ang)
    zeros = jnp.zeros((s, 128 - QK_ROPE), jnp.float32)
    return (jnp.concatenate([cos, cos, zeros], axis=1), jnp.concatenate([-sin, sin, zeros], axis=1))


def _swap_halves(x):
    lane = lax.broadcasted_iota(jnp.int32, x.shape, 1)
    swapped = jnp.where(lane < QK_ROPE // 2, pltpu.roll(x, 128 - QK_ROPE // 2, axis=1),
                        pltpu.roll(x, QK_ROPE // 2, axis=1))
    return jnp.where(lane < QK_ROPE, swapped, 0.0)


def _rope(x, cos, sin):
    return x * cos + _swap_halves(x) * sin


def _rope_transposed(d, cos, sin):
    return d * cos + _swap_halves(d * sin)


def _qk_rope_fwd(q_pad, proj, cos, sin, *, tm, jobs=()):
    s = q_pad.shape[0]
    wq = N_HEADS * HEAD_PAD

    def body(q_ref, kr_ref, cos_ref, sin_ref, qo_ref, kro_ref):
        c, sn = cos_ref[...], sin_ref[...]
        for h in range(N_HEADS):
            lo = h * HEAD_PAD
            qo_ref[:, lo:lo + 128] = q_ref[:, lo:lo + 128].astype(qo_ref.dtype)
            qo_ref[:, lo + 128:lo + 256] = _rope(q_ref[:, lo + 128:lo + 256], c, sn).astype(qo_ref.dtype)
        kro_ref[...] = _rope(kr_ref[...], c, sn).astype(kro_ref.dtype)

    return _call(
        body, name="qk_rope_fwd", grid=(s // tm,),
        in_specs=[pl.BlockSpec((tm, wq), lambda i: (i, 0)), pl.BlockSpec((tm, 128), lambda i: (i, COL_KR // 128)),
                  pl.BlockSpec((tm, 128), lambda i: (i, 0)), pl.BlockSpec((tm, 128), lambda i: (i, 0))],
        out_specs=[pl.BlockSpec((tm, wq), lambda i: (i, 0)), pl.BlockSpec((tm, 128), lambda i: (i, 0))],
        out_shape=[jax.ShapeDtypeStruct((s, wq), jnp.bfloat16), jax.ShapeDtypeStruct((s, 128), jnp.bfloat16)],
        semantics=("parallel",), args=(q_pad, proj, cos, sin), jobs=jobs)


def _qk_rope_bwd(dq_pad, dk_pad, dv, cos, sin, *, tm, jobs=()):
    s = dq_pad.shape[0]
    wq = N_HEADS * HEAD_PAD

    def body(dq_ref, dk_ref, dv_ref, cos_ref, sin_ref, dqo_ref, dkv_ref, dkr_ref):
        c, sn = cos_ref[...], sin_ref[...]
        dkr = jnp.zeros((tm, 128), jnp.float32)
        for h in range(N_HEADS):
            lo = h * HEAD_PAD
            dqo_ref[:, lo:lo + 128] = dq_ref[:, lo:lo + 128].astype(dqo_ref.dtype)
            dqo_ref[:, lo + 128:lo + 256] = _rope_transposed(dq_ref[:, lo + 128:lo + 256], c, sn).astype(dqo_ref.dtype)
            dkv_ref[:, lo:lo + 128] = dk_ref[:, lo:lo + 128].astype(dkv_ref.dtype)
            dkv_ref[:, lo + 128:lo + 256] = dv_ref[:, h * V_DIM:(h + 1) * V_DIM].astype(dkv_ref.dtype)
            dkr = dkr + dk_ref[:, lo + 128:lo + 256]
        dkr_ref[...] = _rope_transposed(dkr, c, sn).astype(dkr_ref.dtype)

    return _call(
        body, name="qk_rope_bwd", grid=(s // tm,),
        in_specs=[pl.BlockSpec((tm, wq), lambda i: (i, 0)), pl.BlockSpec((tm, wq), lambda i: (i, 0)),
                  pl.BlockSpec((tm, N_HEADS * V_DIM), lambda i: (i, 0)),
                  pl.BlockSpec((tm, 128), lambda i: (i, 0)), pl.BlockSpec((tm, 128), lambda i: (i, 0))],
        out_specs=[pl.BlockSpec((tm, wq), lambda i: (i, 0)), pl.BlockSpec((tm, wq), lambda i: (i, 0)),
                   pl.BlockSpec((tm, 128), lambda i: (i, 0))],
        out_shape=[jax.ShapeDtypeStruct((s, wq), jnp.bfloat16), jax.ShapeDtypeStruct((s, wq), jnp.bfloat16),
                   jax.ShapeDtypeStruct((s, 128), jnp.bfloat16)],
        semantics=("parallel",), args=(dq_pad, dk_pad, dv, cos, sin), jobs=jobs)


def _visible(q0, k0, t):
    qpos = q0 + lax.broadcasted_iota(jnp.int32, (t, t), 0)
    kpos = k0 + lax.broadcasted_iota(jnp.int32, (t, t), 1)
    return lax.shift_right_logical(kpos, CHUNK_SHIFT) <= lax.shift_right_logical(qpos, CHUNK_SHIFT)


def _attn_fwd(q, kv, kr, attn_out_g, *, t, jobs=()):
    s = q.shape[0]
    nq = s // t

    def body(q_ref, kn_ref, v_ref, kr_ref, g_ref, o_ref, lse_ref, y_ref, kcat_ref):
        i = pl.program_id(1)

        @pl.when(i == 0)
        def _():
            kcat_ref[:, 0:128] = kn_ref[...]
            kcat_ref[:, 128:256] = kr_ref[...]

        qv = q_ref[...]

        def step(j, carry, diagonal):
            m, l, acc = carry
            k = kcat_ref[pl.ds(pl.multiple_of(j * t, t), t), :]
            v = v_ref[pl.ds(pl.multiple_of(j * t, t), t), :]
            sc = lax.dot_general(qv, k, _NT, preferred_element_type=jnp.float32) * ATTN_SCALE
            if diagonal:
                sc = jnp.where(_visible(0, 0, t), sc, NEG_INF)
            m_new = jnp.maximum(m, jnp.max(sc, axis=-1, keepdims=True))
            p = jnp.exp(sc - m_new)
            alpha = jnp.exp(m - m_new)
            l = alpha * l + jnp.sum(p, axis=-1, keepdims=True)
            acc = alpha * acc + lax.dot_general(p.astype(jnp.bfloat16), v, _NN, preferred_element_type=jnp.float32)
            return m_new, l, acc

        init = (jnp.full((t, 1), NEG_INF, jnp.float32), jnp.zeros((t, 1), jnp.float32),
                jnp.zeros((t, V_DIM), jnp.float32))
        before = lax.fori_loop(0, i, functools.partial(step, diagonal=False), init)
        m, l, acc = step(i, before, True)
        o = acc / l
        o_ref[...] = o
        lse_ref[...] = jnp.broadcast_to(m + jnp.log(l), (t, 128))
        y_ref[...] = (o * _rstd(o) * g_ref[...]).astype(y_ref.dtype)

    head_rows = lambda w, f: pl.BlockSpec((s, w), lambda h, i: (0, f(h)))
    blk = pl.BlockSpec((t, 128), lambda h, i: (i, h))
    full = jax.ShapeDtypeStruct((s, N_HEADS * V_DIM), jnp.float32)
    return _call(
        body, name="attn_fwd", grid=(N_HEADS, nq),
        in_specs=[pl.BlockSpec((t, HEAD_PAD), lambda h, i: (i, h)), head_rows(128, lambda h: 2 * h),
                  head_rows(128, lambda h: 2 * h + 1), head_rows(128, lambda h: 0),
                  pl.BlockSpec((1, 128), lambda h, i: (0, h))],
        out_specs=[blk, blk, blk],
        out_shape=[full, full, jax.ShapeDtypeStruct((s, N_HEADS * V_DIM), jnp.bfloat16)],
        scratch_shapes=[pltpu.VMEM((s, HEAD_PAD), jnp.bfloat16)],
        semantics=("arbitrary", "arbitrary"), args=(q, kv, kv, kr, attn_out_g), jobs=jobs)


def _attn_norm_bwd(o, attn_out_g, dycat, jobs=()):
    s = o.shape[0]

    def body(o_ref, g_ref, dy_ref, do_ref, delta_ref, dg_ref):
        ov = o_ref[...]
        do, dgr = _rms_bwd_rows(ov, g_ref[...], dy_ref[...])
        do_ref[...] = do.astype(do_ref.dtype)
        delta_ref[...] = jnp.broadcast_to(jnp.sum(do * ov, axis=-1, keepdims=True), (s, 128))
        dg_ref[...] = jnp.sum(dgr, axis=0, keepdims=True)

    col = lambda base: pl.BlockSpec((s, 128), lambda h: (0, base + h))
    return _call(
        body, name="attn_norm_bwd", grid=(N_HEADS,),
        in_specs=[col(0), pl.BlockSpec((1, 128), lambda h: (0, h)), col(CONV_WIDTH // 128)],
        out_specs=[col(0), col(0), pl.BlockSpec((1, 128), lambda h: (0, h))],
        out_shape=[jax.ShapeDtypeStruct((s, N_HEADS * V_DIM), jnp.bfloat16),
                   jax.ShapeDtypeStruct((s, N_HEADS * V_DIM), jnp.float32),
                   jax.ShapeDtypeStruct((1, N_HEADS * V_DIM), jnp.float32)],
        semantics=("parallel",), args=(o, attn_out_g, dycat), jobs=jobs)


def _attn_bwd(q, kv, kr, do, lse, delta, *, t, jobs=()):
    s = q.shape[0]
    nq = s // t

    def body(q_ref, kn_ref, v_ref, kr_ref, do_ref, lse_ref, delta_ref, dq_ref, dk_ref, dv_ref, kcat_ref):
        kcat_ref[:, 0:128] = kn_ref[...]
        kcat_ref[:, 128:256] = kr_ref[...]
        dq_ref[...] = jnp.zeros_like(dq_ref)
        dk_ref[...] = jnp.zeros_like(dk_ref)
        dv_ref[...] = jnp.zeros_like(dv_ref)

        def kv_step(j, _):
            krows = pl.ds(pl.multiple_of(j * t, t), t)
            k = kcat_ref[krows, :]
            v = v_ref[krows, :]

            def q_step(i, _, diagonal):
                qrows = pl.ds(pl.multiple_of(i * t, t), t)
                qv = q_ref[qrows, :]
                dov = do_ref[qrows, :]
                sc = lax.dot_general(qv, k, _NT, preferred_element_type=jnp.float32) * ATTN_SCALE
                if diagonal:
                    sc = jnp.where(_visible(0, 0, t), sc, NEG_INF)
                p = jnp.exp(sc - lse_ref[qrows, :][:, 0:1])
                dp = lax.dot_general(dov, v, _NT, preferred_element_type=jnp.float32)
                ds = (p * (dp - delta_ref[qrows, :][:, 0:1]) * ATTN_SCALE).astype(jnp.bfloat16)
                dv_ref[krows, :] += lax.dot_general(p.astype(jnp.bfloat16), dov, _TN,
                                                    preferred_element_type=jnp.float32)
                dk_ref[krows, :] += lax.dot_general(ds, qv, _TN, preferred_element_type=jnp.float32)
                dq_ref[qrows, :] += lax.dot_general(ds, k, _NN, preferred_element_type=jnp.float32)
                return 0

            q_step(j, 0, True)
            lax.fori_loop(j + 1, nq, functools.partial(q_step, diagonal=False), 0)
            return 0

        lax.fori_loop(0, nq, kv_step, 0)

    col = lambda w, f: pl.BlockSpec((s, w), lambda h: (0, f(h)))
    return _call(
        body, name="attn_bwd", grid=(N_HEADS,),
        in_specs=[col(HEAD_PAD, lambda h: h), col(128, lambda h: 2 * h), col(128, lambda h: 2 * h + 1),
                  col(128, lambda h: 0), col(128, lambda h: h), col(128, lambda h: h), col(128, lambda h: h)],
        out_specs=[col(HEAD_PAD, lambda h: h), col(HEAD_PAD, lambda h: h), col(128, lambda h: h)],
        out_shape=[jax.ShapeDtypeStruct((s, N_HEADS * HEAD_PAD), jnp.float32),
                   jax.ShapeDtypeStruct((s, N_HEADS * HEAD_PAD), jnp.float32),
                   jax.ShapeDtypeStruct((s, N_HEADS * V_DIM), jnp.float32)],
        scratch_shapes=[pltpu.VMEM((s, HEAD_PAD), jnp.bfloat16)],
        semantics=("parallel",), args=(q, kv, kv, kr, do, lse, delta), jobs=jobs)


def _row_tile(rows):
    for cand in (256, 128, 64, 32, 16, 8):
        if rows % cand == 0:
            return cand
    return rows


def _cast_into_slot(w, pos, *, name, jobs=()):
    r, c = w.shape
    tr = _row_tile(r)

    def body(pos_ref, w_ref, o_ref):
        o_ref[...] = w_ref[...].astype(o_ref.dtype)

    return _call(
        body, name=name, grid=(r // tr,), prefetch=pos,
        in_specs=[pl.BlockSpec((tr, c), lambda i, p: (i, 0))],
        out_specs=[pl.BlockSpec((None, tr, c), lambda i, p: (p[1], i, 0))],
        out_shape=[jax.ShapeDtypeStruct((4, r, c), jnp.bfloat16)],
        semantics=("parallel",), args=(w,), jobs=jobs)[0]


def _pair_add(g, theirs, pos, *, name, jobs=()):
    n, h, c = theirs.shape
    tr = _row_tile(h)
    nb = h // tr

    def body(pos_ref, a_ref, b_ref, o_ref):
        o_ref[...] = (a_ref[...].astype(jnp.float32) + b_ref[...].astype(jnp.float32)).astype(o_ref.dtype)

    spec = pl.BlockSpec((None, tr, c), lambda j, i, p: (j, i, 0))
    return _call(
        body, name=name, grid=(n, nb), prefetch=pos,
        in_specs=[pl.BlockSpec((None, tr, c), lambda j, i, p: (j, i + p[0] * nb, 0)), spec],
        out_specs=[spec], out_shape=[jax.ShapeDtypeStruct(theirs.shape, jnp.bfloat16)],
        semantics=("parallel", "parallel"), args=(g, theirs), jobs=jobs)[0]


def _fold_diag(pair_sum, via, pos, *, name):
    n, h, c = pair_sum.shape
    tr = _row_tile(h // 2)
    nb = h // tr

    def body(pos_ref, p_ref, via_ref, o_ref):
        j, i = pl.program_id(0), pl.program_id(1)
        mine = p_ref[...].astype(jnp.float32)
        add = (j == 0) == (i >= nb // 2)
        o_ref[...] = jnp.where(add, mine + via_ref[...].astype(jnp.float32), mine).astype(o_ref.dtype)

    return _call(
        body, name=name, grid=(2, nb), prefetch=pos,
        in_specs=[pl.BlockSpec((None, tr, c), lambda j, i, p: (jnp.bitwise_xor(p[1], 2 - j), i, 0)),
                  pl.BlockSpec((tr, c), lambda j, i, p: (i, 0))],
        out_specs=[pl.BlockSpec((None, tr, c), lambda j, i, p: (j, i, 0))],
        out_shape=[jax.ShapeDtypeStruct((2, h, c), jnp.bfloat16)],
        semantics=("parallel", "parallel"), args=(pair_sum, via))[0]


def _chip_sum(by_source, pair_sum, pos, *, name):
    n, h, c = by_source.shape
    tr = _row_tile(h)
    nb = h // tr

    def body(pos_ref, own_ref, px_ref, py_ref, o_ref):
        f = lambda ref: ref[...].astype(jnp.float32)
        o_ref[...] = (f(own_ref) + f(px_ref)) + f(py_ref)

    slot = lambda flip: pl.BlockSpec((None, tr, c), lambda i, p: (jnp.bitwise_xor(p[1], flip), i, 0))
    return pl.pallas_call(
        body, name=name, out_shape=jax.ShapeDtypeStruct((2 * h, c), jnp.float32),
        grid_spec=pltpu.PrefetchScalarGridSpec(
            num_scalar_prefetch=1, grid=(nb,),
            in_specs=[slot(0), slot(2), slot(1)],
            out_specs=pl.BlockSpec((tr, c), lambda i, p: (i + p[0] * nb, 0))),
        compiler_params=_params(("parallel",)),
    )(pos, pair_sum, by_source, by_source)


def _adamw(w, g, m, v, *, name, jobs=()):
    r, c = w.shape
    tr = _row_tile(r)

    def body(w_ref, g_ref, m_ref, v_ref, d_ref, mo_ref, vo_ref, go_ref):
        gv = g_ref[...]
        go_ref[...] = gv
        mn = ADAM_B1 * m_ref[...] + (1.0 - ADAM_B1) * gv
        vn = ADAM_B2 * v_ref[...] + (1.0 - ADAM_B2) * (gv * gv)
        m_hat = mn / (1.0 - ADAM_B1 ** ADAM_STEP)
        v_hat = vn / (1.0 - ADAM_B2 ** ADAM_STEP)
        d_ref[...] = -ADAM_LR * (m_hat / (jnp.sqrt(v_hat) + ADAM_EPS) + ADAM_WD * w_ref[...])
        mo_ref[...] = mn
        vo_ref[...] = vn

    spec = pl.BlockSpec((tr, c), lambda i: (i, 0))
    out = jax.ShapeDtypeStruct((r, c), jnp.float32)
    return _call(body, name=name, grid=(r // tr,), in_specs=[spec] * 4, out_specs=[spec] * 4, out_shape=[out] * 4,
                 semantics=("parallel",), args=(w, g, m, v), jobs=jobs)


def _all_reduce_small(block):
    r, c = block.shape

    def body(src_ref, out_ref, stage_ref, send_sems, recv_sems):
        x, y, cc = _position()
        me = 4 * x + 2 * y + cc
        stage_ref[me] = src_ref[...]
        flip = lambda v, on: 1 - v if on else v
        peers = [(flip(x, dx), flip(y, dy), flip(cc, dc)) for dx in (0, 1) for dy in (0, 1) for dc in (0, 1)][1:]
        copies = [pltpu.make_async_remote_copy(
            src_ref=stage_ref.at[me], dst_ref=stage_ref.at[me],
            send_sem=send_sems.at[k], recv_sem=recv_sems.at[k], device_id=peer, device_id_type=MESH)
            for k, peer in enumerate(peers)]
        for cp in copies:
            cp.start()
        for k, (px, py, pc) in enumerate(peers):
            them = 4 * px + 2 * py + pc
            pltpu.make_async_remote_copy(
                src_ref=stage_ref.at[them], dst_ref=stage_ref.at[them],
                send_sem=send_sems.at[k], recv_sem=recv_sems.at[k], device_id=(px, py, pc),
                device_id_type=MESH).wait_recv()
        for cp in copies:
            cp.wait_send()
        total = stage_ref[0]
        for d in range(1, 8):
            total = total + stage_ref[d]
        out_ref[...] = total

    return pl.pallas_call(
        body, name="all_reduce_small",
        in_specs=[pl.BlockSpec(memory_space=pltpu.VMEM)], out_specs=pl.BlockSpec(memory_space=pltpu.VMEM),
        out_shape=jax.ShapeDtypeStruct((r, c), jnp.float32),
        scratch_shapes=[pltpu.VMEM((8, r, c), jnp.float32), pltpu.SemaphoreType.DMA((7,)),
                        pltpu.SemaphoreType.DMA((7,))],
        compiler_params=pltpu.CompilerParams(has_side_effects=True),
    )(block)


def _cols_from_shards(g):
    n, r, c = g.shape
    return jnp.transpose(g, (1, 0, 2)).reshape(r, n * c)


def _cols_to_shards(w, n=4):
    r, c = w.shape
    return jnp.transpose(w.reshape(r, n, c // n), (1, 0, 2))


def _pad_w_in(full):
    d = full.shape[0]
    zeros = jnp.zeros((d, COL_CKV - COL_KR - QK_ROPE), full.dtype)
    return jnp.concatenate([full[:, :COL_KR], full[:, IN_WIDTH - QK_ROPE:], zeros,
                            full[:, COL_KR:COL_KR + KV_RANK]], axis=1)


def _unpad_w_in(padded):
    return jnp.concatenate([padded[:, :COL_KR], padded[:, COL_CKV:COL_CKV + KV_RANK],
                            padded[:, COL_KR:COL_KR + QK_ROPE]], axis=1)


def _pad_w_uq(full):
    r = full.shape[0]
    per_head = full.reshape(r, N_HEADS, QK_NOPE + QK_ROPE)
    return jnp.pad(per_head, ((0, 0), (0, 0), (0, HEAD_PAD - QK_NOPE - QK_ROPE))).reshape(r, N_HEADS * HEAD_PAD)


def _unpad_w_uq(padded):
    r = padded.shape[0]
    return padded.reshape(r, N_HEADS, HEAD_PAD)[:, :, :QK_NOPE + QK_ROPE].reshape(r, N_HEADS * (QK_NOPE + QK_ROPE))


SMALL_ROWS = 16


def _pack_small(d, pre_mix, post_mix, pre_mlp, post_mlp, conv_out, attn_out, q_norm, kv_norm, conv_w):
    row = lambda *parts: jnp.pad(jnp.concatenate(parts, axis=1), ((0, 0), (0, d - sum(p.shape[1] for p in parts))))
    rows = [row(pre_mix), row(post_mix), row(pre_mlp), row(post_mlp), row(conv_out, attn_out), row(q_norm, kv_norm),
            row(conv_w[0:1]), row(conv_w[1:2]), row(conv_w[2:3])]
    return jnp.pad(jnp.concatenate(rows, axis=0), ((0, SMALL_ROWS - len(rows)), (0, 0)))


def _unpack_small(p, chip):
    cw = CONV_WIDTH // 4
    conv_w = lax.dynamic_slice(p[6:9, :CONV_WIDTH], (0, chip * cw), (3, cw))
    return dict(pre_mix_g=p[0:1], post_mix_g=p[1:2], pre_mlp_g=p[2:3], post_mlp_g=p[3:4],
                conv_out_g=p[4:5, :CONV_WIDTH], attn_out_g=p[4:5, CONV_WIDTH:2 * CONV_WIDTH],
                q_norm_g=p[5:6, :Q_RANK], kv_norm_g=p[5:6, Q_RANK:Q_RANK + KV_RANK], conv_w=conv_w[None])


def kernel(x, pre_mix_g, w_in, conv_w, q_norm_g, w_uq, kv_norm_g, w_ukv, conv_out_g, attn_out_g, w_o, post_mix_g, pre_mlp_g, w_up, w_down, post_mlp_g, loss_target, m_pre_mix_g, m_w_in, m_conv_w, m_q_norm_g, m_w_uq, m_kv_norm_g, m_w_ukv, m_conv_out_g, m_attn_out_g, m_w_o, m_post_mix_g, m_pre_mlp_g, m_w_up, m_w_down, m_post_mlp_g, v_pre_mix_g, v_w_in, v_conv_w, v_q_norm_g, v_w_uq, v_kv_norm_g, v_w_ukv, v_conv_out_g, v_attn_out_g, v_w_o, v_post_mix_g, v_pre_mlp_g, v_w_up, v_w_down, v_post_mlp_g):
    bf16 = jnp.bfloat16
    s, d = x.shape[1], x.shape[2]
    d_ff = 4 * d
    chip = 2 * lax.axis_index("x") + lax.axis_index("y")
    xs = x.reshape(s, d)
    target = loss_target.reshape(s, d)
    tm = min(256, s)
    t_attn = min(512, s)
    mt = min(1024, s)
    kt = min(2048, s)

    big = dict(w_in=w_in[0], w_uq=w_uq[0], w_ukv=w_ukv[0], w_o=w_o[0], w_up=w_up[0], w_down=w_down[0])
    names = list(big)
    pos = jnp.stack([lax.axis_index("c"), chip]).astype(jnp.int32)
    wb = {}
    half = {k: big[k].shape[0] // 2 for k in names}

    def rows(k, a, b):
        lo, n = a * half[k] // 64, (b - a) * half[k] // 64
        assert lo % 16 == 0 and n % 16 == 0 and n > 0, (k, a, b)
        return lo, n

    ici = lambda k, a=0, b=64: _GatherIci(wb[k], *rows(k, a, b))
    fwd = lambda k, a=0, b=64: _GatherForward(wb[k], *rows(k, a, b))
    near = lambda k, a=0, b=64: _GatherD2d(wb[k], *rows(k, a, b), slots=(0, 1))
    far = lambda k, a=0, b=64: _GatherD2d(wb[k], *rows(k, a, b), slots=(2,))
    wb["w_in"] = _Buf(_cast_into_slot(big["w_in"], pos, name="cast_w_in"))
    in_plan = dict(w_up=[ici("w_in", 0, 16)],
                   w_down=[ici("w_in", 16, 32), fwd("w_in", 0, 16)],
                   w_o=[ici("w_in", 32, 40), fwd("w_in", 16, 32)],
                   w_uq=[ici("w_in", 40, 48), fwd("w_in", 32, 40), near("w_in", 0, 32)],
                   w_ukv=[ici("w_in", 48, 56), fwd("w_in", 40, 48), far("w_in", 0, 32)])
    for k, jobs in in_plan.items():
        wb[k] = _Buf(_cast_into_slot(big[k], pos, name="cast_" + k, jobs=jobs))
    h1 = _rms_fwd(xs, pre_mix_g, width=d, col=0, tm=tm, name="rms_pre_mix",
                  jobs=[ici("w_in", 56, 64), fwd("w_in", 48, 56), near("w_in", 32, 48), far("w_in", 32, 40)])
    _comm("gather_w_in", [[fwd("w_in", 56, 64), near("w_in", 48, 64), far("w_in", 40, 56)], [far("w_in", 56, 64)]])
    win = _pad_w_in(_cols_from_shards(wb["w_in"].arr))
    ff4 = d_ff // 4

    spread = lambda a: lax.dynamic_update_slice(jnp.zeros((3, CONV_WIDTH), jnp.float32), a[0],
                                                (0, chip * (CONV_WIDTH // 4)))
    conv_w_mine = jnp.where(lax.axis_index("c") == 0, spread(conv_w), 0.0)
    conv_w_full = _all_reduce_small(jnp.pad(conv_w_mine, ((0, 5), (0, 0))))[0:3]

    proj = _matmul(h1, win, dims=_NN, mnk=(s, IN_PAD, d), tiles=(mt, IN_PAD // 3, d), name="mm_proj",
                   jobs=[ici("w_uq"), ici("w_ukv"), ici("w_o")])
    y_conv = _conv_fwd(proj, conv_w_full, conv_out_g,
                       jobs=[fwd("w_uq"), fwd("w_ukv"), fwd("w_o"), near("w_uq"), near("w_ukv"), near("w_o")])
    cqn = _rms_fwd(proj, q_norm_g, width=Q_RANK, col=COL_CQ // Q_RANK, tm=tm, name="rms_q",
                   jobs=[far("w_uq"), far("w_ukv"), far("w_o"), ici("w_up", 0, 4)])
    wuq = _pad_w_uq(_cols_from_shards(wb["w_uq"].arr))
    wukv = _cols_from_shards(wb["w_ukv"].arr)
    wo = wb["w_o"].arr.reshape(-1, d)
    ckvn = _rms_fwd(proj, kv_norm_g, width=KV_RANK, col=COL_CKV // KV_RANK, tm=tm, name="rms_kv",
                    jobs=[ici("w_up", 4, 10), fwd("w_up", 0, 4)])
    q_pad = _matmul(cqn, wuq, dims=_NN, mnk=(s, N_HEADS * HEAD_PAD, Q_RANK), tiles=(mt, 1024, Q_RANK), name="mm_q",
                    jobs=[ici("w_up", 10, 18), fwd("w_up", 4, 10), near("w_up", 0, 4)])
    kv = _matmul(ckvn, wukv, dims=_NN, mnk=(s, N_HEADS * HEAD_PAD, KV_RANK), tiles=(mt, 1024, KV_RANK),
                 name="mm_kv", out_dtype=bf16,
                 jobs=[ici("w_up", 18, 24), fwd("w_up", 10, 18), near("w_up", 4, 10), far("w_up", 0, 4)])
    cos, sin = _rope_tables(s)
    q_rot, kr_rot = _qk_rope_fwd(
        q_pad, proj, cos, sin, tm=tm,
        jobs=[ici("w_up", 24, 32), fwd("w_up", 18, 24), near("w_up", 10, 18), far("w_up", 4, 10)])
    o, lse, y_attn = _attn_fwd(
        q_rot, kv, kr_rot, attn_out_g, t=t_attn,
        jobs=[ici("w_up", 32, 64), ici("w_down", 0, 8), fwd("w_up", 24, 32), near("w_up", 18, 24), far("w_up", 10, 18)])
    ycat = jnp.concatenate([y_conv, y_attn], axis=1)
    mix = _matmul(ycat, wo, dims=_NN, mnk=(s, d, 2 * CONV_WIDTH), tiles=(mt, 1024, 2 * CONV_WIDTH), name="mm_out",
                  jobs=[fwd("w_up", 32, 64), near("w_up", 24, 32), far("w_up", 18, 24)])
    x2, h2 = _mix_residual_fwd(
        xs, mix, post_mix_g, pre_mlp_g, tm=tm,
        jobs=[ici("w_down", 8, 20), fwd("w_down", 0, 8), near("w_up", 32, 64), far("w_up", 24, 64)])
    wup = wb["w_up"].arr

    def up_epilogue(acc, extra_refs, out_refs):
        r = jnp.maximum(acc, 0.0)
        out_refs[0][...] = acc.astype(bf16)
        out_refs[1][...] = (r * r).astype(bf16)

    n_ff = ff4 // 1024
    act = jax.ShapeDtypeStruct((s, d_ff), bf16)
    up, act_sq = _matmul(
        h2, wup, dims=_NN, mnk=(s, d_ff, d), tiles=(mt, 1024, d), name="mm_up",
        b_spec=pl.BlockSpec((None, d, 1024), lambda i, j, l: (j // n_ff, l, j % n_ff)),
        out_shape=(act, act), o_spec=(pl.BlockSpec((mt, 1024), lambda i, j, l: (i, j)),) * 2, epilogue=up_epilogue,
        jobs=[ici("w_down", 20, 64), fwd("w_down", 8, 20), near("w_down", 0, 8)])
    _comm("gather_w_down_tail", [[fwd("w_down", 20, 64), near("w_down", 8, 64), far("w_down", 0, 20)],
                                 [far("w_down", 20, 64)]])
    wdown = wb["w_down"].arr.reshape(d_ff, d)
    mlp = _matmul(act_sq, wdown, dims=_NN, mnk=(s, d, d_ff), tiles=(mt, 1024, 2048), name="mm_down")
    dx3, dmlp, dg_post_mlp, loss_part = _loss_head(x2, mlp, target, post_mlp_g, tm=tm)

    def dup_epilogue(acc, extra_refs, out_refs):
        out_refs[0][...] = (acc * (2.0 * jnp.maximum(extra_refs[0][...].astype(jnp.float32), 0.0))).astype(bf16)

    grads, theirs, pair_sums, via, folded, by_source, whole = {}, {}, {}, {}, {}, {}, {}

    def exchange(k, g):
        grads[k] = g
        theirs[k] = _Buf(jax.ShapeDtypeStruct((4, g.shape[1] // 2, g.shape[2]), bf16))
        return _PairExchange(g, theirs[k])

    def pair_sum(k, jobs=()):
        pair_sums[k] = _pair_add(grads[k], theirs[k].arr, pos, name="pair_add_" + k, jobs=jobs)
        via[k] = _Buf(jax.ShapeDtypeStruct(pair_sums[k].shape[1:], bf16))
        by_source[k] = _Buf(jax.ShapeDtypeStruct(pair_sums[k].shape, bf16))

    def diag(k, a=0, b=32):
        lo, n = a * half[k] // 64, (b - a) * half[k] // 64
        assert lo % 16 == 0 and n % 16 == 0 and n > 0, (k, a, b)
        return _ScatterDiag(pair_sums[k], via[k], lo, n)

    def fold(k):
        folded[k] = _fold_diag(pair_sums[k], via[k].arr, pos, name="fold_" + k)

    scatter = lambda k, a=0, b=64: _ScatterNear(folded[k], by_source[k], *rows(k, a, b))

    def share(k):
        whole[k] = _Buf(_chip_sum(by_source[k].arr, pair_sums[k], pos, name="chip_sum_" + k))
        return _PairShare(whole[k])

    g_wdown = _matmul(act_sq, dmlp, dims=_TN, mnk=(d_ff, d, s), tiles=(1024, 1024, kt), name="mm_gw_down",
                      out_dtype=bf16).reshape(4, ff4, d)
    dup = _matmul(dmlp, wdown, dims=_NT, mnk=(s, d_ff, d), tiles=(mt, 1024, d), name="mm_dact",
                  out_dtype=bf16, epilogue=dup_epilogue, extra=(up,),
                  extra_specs=(pl.BlockSpec((mt, 1024), lambda i, j, l: (i, j)),),
                  jobs=[exchange("w_down", g_wdown)])
    pair_sum("w_down")
    g_wup = _matmul(h2, dup, dims=_TN, mnk=(d, d_ff, s), tiles=(1024, 1024, kt), name="mm_gw_up",
                    out_shape=jax.ShapeDtypeStruct((4, d, ff4), bf16),
                    o_spec=pl.BlockSpec((None, 1024, 1024), lambda i, j, l: (j // n_ff, i, j % n_ff)),
                    jobs=[diag("w_down")])
    fold("w_down")
    dh2 = _matmul(dup, wup, dims=_NT, mnk=(s, d, d_ff), tiles=(mt, 1024, ff4), name="mm_dh2",
                  b_spec=pl.BlockSpec((None, 1024, ff4), lambda i, j, l: (l, j, 0)),
                  jobs=[exchange("w_up", g_wup), scatter("w_down", 0, 48)])
    pair_sum("w_up", jobs=[scatter("w_down", 48, 64)])
    dx2, dmix, dg_pre_mlp, dg_post_mix = _mix_residual_bwd(
        dx3, dh2, x2, mix, pre_mlp_g, post_mix_g, tm=tm, jobs=[diag("w_up", 0, 20)])

    dycat = _matmul(dmix, wo, dims=_NT, mnk=(s, 2 * CONV_WIDTH, d), tiles=(mt, 1024, d), name="mm_dycat",
                    jobs=[diag("w_up", 20, 32)])
    fold("w_up")
    g_wo = _matmul(ycat, dmix, dims=_TN, mnk=(2 * CONV_WIDTH, d, s), tiles=(1024, 1024, kt),
                   name="mm_gw_out", out_dtype=bf16, jobs=[scatter("w_up", 0, 12)]).reshape(4, CONV_WIDTH // 2, d)
    du, dgb, dgc, dg_conv_w, dg_conv_out = _conv_bwd(proj, conv_w_full, conv_out_g, dycat,
                                                     jobs=[exchange("w_o", g_wo), scatter("w_up", 12, 24)])
    pair_sum("w_o")
    do, delta, dg_attn_out = _attn_norm_bwd(o, attn_out_g, dycat)
    dq_pad, dk_pad, dv = _attn_bwd(q_rot, kv, kr_rot, do, lse, delta, t=t_attn,
                                   jobs=[scatter("w_up", 24, 64), diag("w_o"), share("w_down")])
    fold("w_o")
    dq_raw, dkv, dkr = _qk_rope_bwd(dq_pad, dk_pad, dv, cos, sin, tm=tm, jobs=[scatter("w_o")])
    wq_cols = N_HEADS * HEAD_PAD
    g_wuq = _matmul(cqn, dq_raw, dims=_TN, mnk=(Q_RANK, wq_cols, s), tiles=(Q_RANK, 1024, kt),
                    name="mm_gw_uq", out_dtype=bf16)
    dcqn = _matmul(dq_raw, wuq, dims=_NT, mnk=(s, Q_RANK, wq_cols), tiles=(mt, Q_RANK, wq_cols), name="mm_dcq")
    g_wukv = _matmul(ckvn, dkv, dims=_TN, mnk=(KV_RANK, wq_cols, s), tiles=(KV_RANK, 1024, kt),
                     name="mm_gw_ukv", out_dtype=bf16)
    dckvn = _matmul(dkv, wukv, dims=_NT, mnk=(s, KV_RANK, wq_cols), tiles=(mt, KV_RANK, wq_cols), name="mm_dckv",
                    jobs=[exchange("w_uq", _cols_to_shards(_unpad_w_uq(g_wuq))),
                          exchange("w_ukv", _cols_to_shards(g_wukv))])
    pair_sum("w_uq")
    pair_sum("w_ukv")
    dcq, dg_q_norm = _rms_bwd(proj, q_norm_g, dcqn, width=Q_RANK, col=COL_CQ // Q_RANK, tm=tm, name="rms_q_bwd",
                              jobs=[diag("w_uq"), diag("w_ukv")])
    fold("w_uq")
    fold("w_ukv")
    dckv, dg_kv_norm = _rms_bwd(proj, kv_norm_g, dckvn, width=KV_RANK, col=COL_CKV // KV_RANK, tm=tm,
                                name="rms_kv_bwd")
    dproj = jnp.concatenate([du, dgb, dgc, dcq, dkr, jnp.zeros((s, COL_CKV - COL_KR - 128), bf16), dckv], axis=1)
    g_win = _matmul(h1, dproj, dims=_TN, mnk=(d, IN_PAD, s), tiles=(1024, IN_PAD // 3, kt), name="mm_gw_in",
                    out_dtype=bf16, jobs=[scatter("w_uq"), scatter("w_ukv"), share("w_up"), share("w_o")])
    _comm("pair_exchange_w_in", [[exchange("w_in", _cols_to_shards(_unpad_w_in(g_win)))]])
    pair_sum("w_in")
    _comm("scatter_diag_w_in", [[diag("w_in")]])
    fold("w_in")
    dh1 = _matmul(dproj, win, dims=_NT, mnk=(s, d, IN_PAD), tiles=(mt, 1024, IN_PAD // 2), name="mm_dh1",
                  jobs=[scatter("w_in"), share("w_uq"), share("w_ukv")])
    grad_x, dg_pre_mix = _input_bwd(dx2, dh1, xs, pre_mix_g, tm=tm)
    _comm("pair_share_w_in", [[share("w_in")]])

    moments = dict(w_in=(m_w_in, v_w_in), w_uq=(m_w_uq, v_w_uq), w_ukv=(m_w_ukv, v_w_ukv), w_o=(m_w_o, v_w_o),
                   w_up=(m_w_up, v_w_up), w_down=(m_w_down, v_w_down))
    gw, dw, nm, nv = {}, {}, {}, {}
    for k in names:
        delta_k, nm_k, nv_k, g = _adamw(big[k], whole[k].arr, moments[k][0][0], moments[k][1][0], name="adamw_" + k)
        gw[k], dw[k], nm[k], nv[k] = g[None], delta_k[None], nm_k[None], nv_k[None]

    small_g = _all_reduce_small(_pack_small(d, dg_pre_mix, dg_post_mix, dg_pre_mlp, dg_post_mlp, dg_conv_out,
                                            dg_attn_out, dg_q_norm, dg_kv_norm, dg_conv_w))
    pack_w = lambda cw, pre_mix, post_mix, pre_mlp, post_mlp, conv_out, attn_out, q_norm, kv_norm: _pack_small(
        d, pre_mix, post_mix, pre_mlp, post_mlp, conv_out, attn_out, q_norm, kv_norm, cw)
    small_w = pack_w(conv_w_full, pre_mix_g, post_mix_g, pre_mlp_g, post_mlp_g, conv_out_g, attn_out_g, q_norm_g,
                     kv_norm_g)
    small_m = pack_w(spread(m_conv_w), m_pre_mix_g, m_post_mix_g, m_pre_mlp_g, m_post_mlp_g, m_conv_out_g,
                     m_attn_out_g, m_q_norm_g, m_kv_norm_g)
    small_v = pack_w(spread(v_conv_w), v_pre_mix_g, v_post_mix_g, v_pre_mlp_g, v_post_mlp_g, v_conv_out_g,
                     v_attn_out_g, v_q_norm_g, v_kv_norm_g)
    small_d, small_nm, small_nv, small_g = _adamw(small_w, small_g, small_m, small_v, name="adamw_small")
    sg, sd, snm, snv = (_unpack_small(p, chip) for p in (small_g, small_d, small_nm, small_nv))

    for src, dst in ((sg, gw), (sd, dw), (snm, nm), (snv, nv)):
        dst.update(src)

    loss = lax.psum(loss_part[0, 0], ("x", "y", "c"))
    order = ["pre_mix_g", "w_in", "conv_w", "q_norm_g", "w_uq", "kv_norm_g", "w_ukv", "conv_out_g", "attn_out_g",
             "w_o", "post_mix_g", "pre_mlp_g", "w_up", "w_down", "post_mlp_g"]
    return (loss, grad_x.reshape(1, s, d), *[gw[k] for k in order], *[dw[k] for k in order],
            *[nm[k] for k in order], *[nv[k] for k in order])
```

```python
import functools

import jax
import jax.numpy as jnp
from jax import lax
from jax.experimental import pallas as pl
from jax.experimental.pallas import tpu as pltpu

EPS = 1e-6
NEG_INF = -1e30
CHUNK_SHIFT = 6
N_HEADS = 8
HEAD_PAD = 256
QK_NOPE = 128
QK_ROPE = 64
V_DIM = 128
CONV_WIDTH = 1024
Q_RANK = 768
KV_RANK = 512
ROPE_THETA = 10000.0
ATTN_SCALE = (QK_NOPE + QK_ROPE) ** -0.5
ADAM_LR, ADAM_B1, ADAM_B2, ADAM_EPS, ADAM_WD, ADAM_STEP = 0.001, 0.9, 0.999, 1e-08, 0.01, 10

COL_CQ = 3 * CONV_WIDTH
COL_KR = COL_CQ + Q_RANK
COL_CKV = 4096
IN_PAD = COL_CKV + KV_RANK
IN_WIDTH = 3 * CONV_WIDTH + Q_RANK + KV_RANK + QK_ROPE

VMEM_LIMIT_BYTES = 56 * 1024 * 1024
MESH = pl.DeviceIdType.MESH
ANY = pl.BlockSpec(memory_space=pl.ANY)

_NN = (((1,), (0,)), ((), ()))
_NT = (((1,), (1,)), ((), ()))
_TN = (((0,), (0,)), ((), ()))


def _params(sem):
    return pltpu.CompilerParams(dimension_semantics=sem, vmem_limit_bytes=VMEM_LIMIT_BYTES)


class _Buf:
    def __init__(self, arr):
        self.arr = arr


def _position():
    return lax.axis_index("x"), lax.axis_index("y"), lax.axis_index("c")


def _other_chips(x, y):
    return [(2 * (1 - x) + y, (1 - x, y)), (2 * x + (1 - y), (x, 1 - y)), (2 * (1 - x) + (1 - y), (1 - x, 1 - y))]


def _remote(src, dst, sems, k, to):
    send, recv, off = sems
    return pltpu.make_async_remote_copy(src_ref=src, dst_ref=dst, send_sem=send.at[off + k], recv_sem=recv.at[off + k],
                                        device_id=to, device_id_type=MESH)


class _GatherIci:
    n_sems = 2

    def __init__(self, buf, lo, n):
        self.buf, self.lo, self.n, self.buffers, self.sources = buf, lo, n, [buf], []

    def _rows(self, ref, slot, which, lo=None, n=None):
        lo, n = (self.lo, self.n) if lo is None else (lo, n)
        return ref.at[slot, pl.ds(which * (ref.shape[1] // 2) + lo, n), :]

    def start(self, refs, sems):
        ref = refs[id(self.buf)]
        x, y, c = _position()
        mine = self._rows(ref, 2 * x + y, c)
        for k, (_, xy) in enumerate(_other_chips(x, y)[:2]):
            _remote(mine, mine, sems, k, (*xy, c)).start()

    def wait(self, refs, sems):
        ref = refs[id(self.buf)]
        x, y, c = _position()
        mine = self._rows(ref, 2 * x + y, c)
        for k, (slot, xy) in enumerate(_other_chips(x, y)[:2]):
            landed = self._rows(ref, slot, c)
            _remote(landed, landed, sems, k, (*xy, c)).wait_recv()
            _remote(mine, mine, sems, k, (*xy, c)).wait_send()


class _GatherForward(_GatherIci):
    def _ways(self, x, y):
        (slot_x, xy_x), (slot_y, xy_y), (slot_d, _) = _other_chips(x, y)
        h = self.n // 2
        assert h % 16 == 0, self.n
        return [(slot_x, slot_d, self.lo, xy_y), (slot_y, slot_d, self.lo + h, xy_x)], h

    def start(self, refs, sems):
        ref = refs[id(self.buf)]
        x, y, c = _position()
        ways, h = self._ways(x, y)
        for k, (slot, _, lo, xy) in enumerate(ways):
            rows = self._rows(ref, slot, c, lo, h)
            _remote(rows, rows, sems, k, (*xy, c)).start()

    def wait(self, refs, sems):
        ref = refs[id(self.buf)]
        x, y, c = _position()
        ways, h = self._ways(x, y)
        for k, (slot, lands, lo, xy) in enumerate(ways):
            landed, sent = self._rows(ref, lands, c, lo, h), self._rows(ref, slot, c, lo, h)
            _remote(landed, landed, sems, k, (*xy, c)).wait_recv()
            _remote(sent, sent, sems, k, (*xy, c)).wait_send()


class _GatherD2d(_GatherIci):
    def __init__(self, buf, lo, n, slots):
        super().__init__(buf, lo, n)
        self.slots, self.n_sems = slots, len(slots)

    def start(self, refs, sems):
        ref = refs[id(self.buf)]
        x, y, c = _position()
        chips = _other_chips(x, y)
        for k, which in enumerate(self.slots):
            rows = self._rows(ref, chips[which][0], c)
            _remote(rows, rows, sems, k, (x, y, 1 - c)).start()

    def wait(self, refs, sems):
        ref = refs[id(self.buf)]
        x, y, c = _position()
        chips = _other_chips(x, y)
        for k, which in enumerate(self.slots):
            sent, landed = self._rows(ref, chips[which][0], c), self._rows(ref, chips[which][0], 1 - c)
            _remote(landed, landed, sems, k, (x, y, 1 - c)).wait_recv()
            _remote(sent, sent, sems, k, (x, y, 1 - c)).wait_send()


class _ScatterDiag:
    n_sems = 2

    def __init__(self, src, via, lo, n):
        self.src, self.via, self.lo, self.n, self.buffers, self.sources = src, via, lo, n, [via], [src]

    def _copies(self, refs, sems):
        src, via = refs[id(self.src)], refs[id(self.via)]
        x, y, c = _position()
        (_, xy_x), (_, xy_y), (slot_d, _) = _other_chips(x, y)
        h2 = via.shape[0] // 2
        return [_remote(src.at[slot_d, pl.ds(first + self.lo, self.n), :], via.at[pl.ds(first + self.lo, self.n), :],
                        sems, k, (*xy, c)) for k, (first, xy) in enumerate(((0, xy_x), (h2, xy_y)))]

    def start(self, refs, sems):
        for cp in self._copies(refs, sems):
            cp.start()

    def wait(self, refs, sems):
        for cp in self._copies(refs, sems):
            cp.wait_recv()
            cp.wait_send()


class _ScatterNear:
    n_sems = 2

    def __init__(self, src, dst, lo, n):
        self.src, self.dst, self.lo, self.n, self.buffers, self.sources = src, dst, lo, n, [dst], [src]

    def _copies(self, refs, sems, landing):
        src, dst = refs[id(self.src)], refs[id(self.dst)]
        x, y, c = _position()
        rows = pl.ds(self.lo, self.n)
        return [_remote(src.at[k, rows, :], dst.at[slot if landing else 2 * x + y, rows, :], sems, k, (*xy, c))
                for k, (slot, xy) in enumerate(_other_chips(x, y)[:2])]

    def start(self, refs, sems):
        for cp in self._copies(refs, sems, False):
            cp.start()

    def wait(self, refs, sems):
        for cp in self._copies(refs, sems, True):
            cp.wait_recv()
            cp.wait_send()


class _PairExchange:
    n_sems = 1

    def __init__(self, src, dst):
        self.src, self.dst, self.buffers, self.sources = src, dst, [dst], [src]

    def _copy(self, refs, sems):
        src, dst = refs[id(self.src)], refs[id(self.dst)]
        x, y, c = _position()
        h = src.shape[1] // 2
        return _remote(src.at[:, pl.ds((1 - c) * h, h), :], dst, sems, 0, (x, y, 1 - c))

    def start(self, refs, sems):
        self._copy(refs, sems).start()

    def wait(self, refs, sems):
        self._copy(refs, sems).wait()


class _PairShare:
    n_sems = 1

    def __init__(self, buf):
        self.buf, self.buffers, self.sources = buf, [buf], []

    def _rows(self, ref, which):
        h = ref.shape[0] // 2
        return ref.at[pl.ds(which * h, h), :]

    def start(self, refs, sems):
        ref = refs[id(self.buf)]
        x, y, c = _position()
        _remote(self._rows(ref, c), self._rows(ref, c), sems, 0, (x, y, 1 - c)).start()

    def wait(self, refs, sems):
        ref = refs[id(self.buf)]
        x, y, c = _position()
        _remote(self._rows(ref, c), self._rows(ref, c), sems, 0, (x, y, 1 - c)).wait_send()
        _remote(self._rows(ref, 1 - c), self._rows(ref, 1 - c), sems, 0, (x, y, 1 - c)).wait_recv()


def _unique(items):
    seen, out = set(), []
    for it in items:
        if id(it) not in seen:
            seen.add(id(it))
            out.append(it)
    return out


def _job_operands(jobs):
    sources = _unique([a for j in jobs for a in j.sources])
    buffers = _unique([b for j in jobs for b in j.buffers])
    held = [b for b in buffers if not isinstance(b.arr, jax.ShapeDtypeStruct)]
    fresh = [b for b in buffers if isinstance(b.arr, jax.ShapeDtypeStruct)]
    return sources, held, fresh


def _sem_offsets(jobs):
    offs, total = [], 0
    for j in jobs:
        offs.append(total)
        total += j.n_sems
    return offs, total


def _call(body, *, name, grid, in_specs, out_specs, out_shape, args, semantics, scratch_shapes=(), jobs=(),
          prefetch=None):
    n_pre = 0 if prefetch is None else 1

    def launch(fn, in_specs, out_specs, scratch, **kw):
        if prefetch is None:
            return pl.pallas_call(fn, name=name, grid=grid, in_specs=in_specs, out_specs=out_specs,
                                  scratch_shapes=scratch, **kw)
        return pl.pallas_call(fn, name=name, grid_spec=pltpu.PrefetchScalarGridSpec(
            num_scalar_prefetch=1, grid=grid, in_specs=in_specs, out_specs=out_specs, scratch_shapes=scratch), **kw)

    pre = () if prefetch is None else (prefetch,)
    if not jobs:
        return launch(body, list(in_specs), list(out_specs), list(scratch_shapes), out_shape=list(out_shape),
                      compiler_params=_params(semantics))(*pre, *args)
    sources, held, fresh = _job_operands(jobs)
    offs, n_sem = _sem_offsets(jobs)
    n_in, n_out, n_scr = len(in_specs), len(out_specs), len(scratch_shapes)
    n_src, n_held, n_fresh = len(sources), len(held), len(fresh)

    def carried(*refs):
        pre_refs, refs = refs[:n_pre], refs[n_pre:]
        ins = refs[:n_in]
        src_refs = refs[n_in:n_in + n_src]
        o0 = n_in + n_src + n_held
        outs = refs[o0:o0 + n_out]
        buf_refs = refs[o0 + n_out:o0 + n_out + n_held + n_fresh]
        s0 = o0 + n_out + n_held + n_fresh
        scratch = refs[s0:s0 + n_scr]
        send, recv = refs[s0 + n_scr], refs[s0 + n_scr + 1]
        where = {id(a): r for a, r in zip(sources, src_refs)}
        where.update({id(b): r for b, r in zip(held + fresh, buf_refs)})
        ids = [pl.program_id(a) for a in range(len(grid))]
        first = functools.reduce(jnp.logical_and, [i == 0 for i in ids])
        last = functools.reduce(jnp.logical_and, [i == g - 1 for i, g in zip(ids, grid)])

        @pl.when(first)
        def _():
            for j, off in zip(jobs, offs):
                j.start(where, (send, recv, off))

        body(*pre_refs, *ins, *outs, *scratch)

        @pl.when(last)
        def _():
            for j, off in zip(jobs, offs):
                j.wait(where, (send, recv, off))

    shape_of = lambda b: jax.ShapeDtypeStruct(b.arr.shape, b.arr.dtype)
    res = launch(
        carried, [*in_specs, *[ANY] * (n_src + n_held)], [*out_specs, *[ANY] * (n_held + n_fresh)],
        [*scratch_shapes, pltpu.SemaphoreType.DMA((n_sem,)), pltpu.SemaphoreType.DMA((n_sem,))],
        out_shape=[*out_shape, *[shape_of(b) for b in held + fresh]],
        input_output_aliases={n_pre + n_in + n_src + i: n_out + i for i in range(n_held)},
        compiler_params=pltpu.CompilerParams(dimension_semantics=("arbitrary",) * len(grid),
                                             vmem_limit_bytes=VMEM_LIMIT_BYTES, has_side_effects=True),
    )(*pre, *args, *sources, *[b.arr for b in held])
    for b, new in zip(held + fresh, res[n_out:]):
        b.arr = new
    return list(res[:n_out])


def _comm(name, phases):
    jobs = [j for ph in phases for j in ph]
    sources, held, fresh = _job_operands(jobs)
    offs, n_sem = _sem_offsets(jobs)
    off_of = {id(j): o for j, o in zip(jobs, offs)}
    n_src, n_held, n_fresh = len(sources), len(held), len(fresh)

    def body(*refs):
        src_refs = refs[:n_src]
        buf_refs = refs[n_src + n_held:n_src + 2 * n_held + n_fresh]
        send, recv = refs[-2], refs[-1]
        where = {id(a): r for a, r in zip(sources, src_refs)}
        where.update({id(b): r for b, r in zip(held + fresh, buf_refs)})
        for ph in phases:
            for j in ph:
                j.start(where, (send, recv, off_of[id(j)]))
            for j in ph:
                j.wait(where, (send, recv, off_of[id(j)]))

    shape_of = lambda b: jax.ShapeDtypeStruct(b.arr.shape, b.arr.dtype)
    res = pl.pallas_call(
        body, name=name,
        in_specs=[ANY] * (n_src + n_held), out_specs=[ANY] * (n_held + n_fresh),
        out_shape=[shape_of(b) for b in held + fresh],
        input_output_aliases={n_src + i: i for i in range(n_held)},
        scratch_shapes=[pltpu.SemaphoreType.DMA((n_sem,)), pltpu.SemaphoreType.DMA((n_sem,))],
        compiler_params=pltpu.CompilerParams(has_side_effects=True),
    )(*sources, *[b.arr for b in held])
    for b, new in zip(held + fresh, res):
        b.arr = new


def _matmul(a, b, *, dims, mnk, tiles, name, out_dtype=jnp.float32, a_spec=None, b_spec=None,
            out_shape=None, o_spec=None, epilogue=None, extra=(), extra_specs=(), jobs=()):
    m, n, k = mnk
    tm, tn, tk = tiles
    assert m % tm == 0 and n % tn == 0 and k % tk == 0, (name, mnk, tiles)
    gm, gn, gk = m // tm, n // tn, k // tk
    if a_spec is None:
        a_spec = (pl.BlockSpec((tk, tm), lambda i, j, l: (l, i)) if dims is _TN
                  else pl.BlockSpec((tm, tk), lambda i, j, l: (i, l)))
    if b_spec is None:
        b_spec = (pl.BlockSpec((tn, tk), lambda i, j, l: (j, l)) if dims is _NT
                  else pl.BlockSpec((tk, tn), lambda i, j, l: (l, j)))
    if out_shape is None:
        out_shape = jax.ShapeDtypeStruct((m, n), out_dtype)
    if o_spec is None:
        o_spec = pl.BlockSpec((tm, tn), lambda i, j, l: (i, j))
    single = not isinstance(out_shape, (tuple, list))
    n_extra = len(extra)

    def finish(acc, extra_refs, out_refs):
        if epilogue is None:
            out_refs[0][...] = acc.astype(out_refs[0].dtype)
        else:
            epilogue(acc, extra_refs, out_refs)

    def body_whole_k(*refs):
        a_ref, b_ref = refs[0], refs[1]
        acc = lax.dot_general(a_ref[...], b_ref[...], dims, preferred_element_type=jnp.float32)
        finish(acc, refs[2:2 + n_extra], refs[2 + n_extra:])

    def body_split_k(*refs):
        a_ref, b_ref = refs[0], refs[1]
        extra_refs = refs[2:2 + n_extra]
        out_refs = refs[2 + n_extra:-1]
        acc_ref = refs[-1]
        step = pl.program_id(2)
        part = lax.dot_general(a_ref[...], b_ref[...], dims, preferred_element_type=jnp.float32)

        @pl.when(step == 0)
        def _():
            acc_ref[...] = part

        @pl.when(jnp.logical_and(step > 0, step < gk - 1))
        def _():
            acc_ref[...] += part

        @pl.when(step == gk - 1)
        def _():
            finish(acc_ref[...] + part, extra_refs, out_refs)

    res = _call(
        body_whole_k if gk == 1 else body_split_k, name=name, grid=(gm, gn, gk),
        in_specs=[a_spec, b_spec, *extra_specs],
        out_specs=[o_spec] if single else list(o_spec),
        out_shape=[out_shape] if single else list(out_shape),
        scratch_shapes=[] if gk == 1 else [pltpu.VMEM((tm, tn), jnp.float32)],
        semantics=("parallel", "parallel", "arbitrary"), args=(a, b, *extra), jobs=jobs)
    return res[0] if single else res


def _rstd(x):
    return lax.rsqrt(jnp.mean(x * x, axis=-1, keepdims=True) + EPS)


def _rms_bwd_rows(x, g, dy):
    r = _rstd(x)
    xn = x * r
    dyg = dy * g
    dx = r * (dyg - xn * jnp.mean(xn * dyg, axis=-1, keepdims=True))
    return dx, dy * xn


def _acc_rows(ref, rows, first):
    part = jnp.sum(rows, axis=0, keepdims=True)

    @pl.when(first)
    def _():
        ref[...] = part

    @pl.when(jnp.logical_not(first))
    def _():
        ref[...] += part


def _rms_fwd(x, g, *, width, col, tm, name, jobs=()):
    s = x.shape[0]

    def body(x_ref, g_ref, o_ref):
        v = x_ref[...]
        o_ref[...] = (v * _rstd(v) * g_ref[...]).astype(o_ref.dtype)

    return _call(
        body, name=name, grid=(s // tm,),
        in_specs=[pl.BlockSpec((tm, width), lambda i: (i, col)), pl.BlockSpec((1, width), lambda i: (0, 0))],
        out_specs=[pl.BlockSpec((tm, width), lambda i: (i, 0))],
        out_shape=[jax.ShapeDtypeStruct((s, width), jnp.bfloat16)],
        semantics=("parallel",), args=(x, g), jobs=jobs)[0]


def _rms_bwd(x, g, dy, *, width, col, tm, name, jobs=()):
    s = x.shape[0]

    def body(x_ref, g_ref, dy_ref, dx_ref, dg_ref):
        dx, dgr = _rms_bwd_rows(x_ref[...], g_ref[...], dy_ref[...])
        dx_ref[...] = dx.astype(dx_ref.dtype)
        _acc_rows(dg_ref, dgr, pl.program_id(0) == 0)

    return _call(
        body, name=name, grid=(s // tm,),
        in_specs=[pl.BlockSpec((tm, width), lambda i: (i, col)), pl.BlockSpec((1, width), lambda i: (0, 0)),
                  pl.BlockSpec((tm, width), lambda i: (i, 0))],
        out_specs=[pl.BlockSpec((tm, width), lambda i: (i, 0)), pl.BlockSpec((1, width), lambda i: (0, 0))],
        out_shape=[jax.ShapeDtypeStruct((s, width), jnp.bfloat16), jax.ShapeDtypeStruct((1, width), jnp.float32)],
        semantics=("arbitrary",), args=(x, g, dy), jobs=jobs)


def _row_specs(tm, d, n):
    return [pl.BlockSpec((tm, d), lambda i: (i, 0)) for _ in range(n)]


def _gain_specs(d, n):
    return [pl.BlockSpec((1, d), lambda i: (0, 0)) for _ in range(n)]


def _mix_residual_fwd(x, mix, g_post_mix, g_pre_mlp, *, tm, jobs=()):
    s, d = x.shape

    def body(x_ref, mix_ref, g1_ref, g2_ref, x2_ref, h2_ref):
        mixv = mix_ref[...]
        x2 = x_ref[...] + mixv * _rstd(mixv) * g1_ref[...]
        x2_ref[...] = x2
        h2_ref[...] = (x2 * _rstd(x2) * g2_ref[...]).astype(h2_ref.dtype)

    return _call(
        body, name="mix_residual_fwd", grid=(s // tm,),
        in_specs=_row_specs(tm, d, 2) + _gain_specs(d, 2),
        out_specs=_row_specs(tm, d, 2),
        out_shape=[jax.ShapeDtypeStruct((s, d), jnp.float32), jax.ShapeDtypeStruct((s, d), jnp.bfloat16)],
        semantics=("parallel",), args=(x, mix, g_post_mix, g_pre_mlp), jobs=jobs)


def _loss_head(x2, mlp, target, g_post_mlp, *, tm, jobs=()):
    s, d = x2.shape

    def body(x2_ref, m_ref, t_ref, g_ref, dx3_ref, dm_ref, dg_ref, loss_ref):
        first = pl.program_id(0) == 0
        mv = m_ref[...]
        g = g_ref[...]
        diff = x2_ref[...] + mv * _rstd(mv) * g - t_ref[...]
        dx3 = diff * (1.0 / d)
        dx3_ref[...] = dx3
        dm, dgr = _rms_bwd_rows(mv, g, dx3)
        dm_ref[...] = dm.astype(dm_ref.dtype)
        _acc_rows(dg_ref, dgr, first)
        part = 0.5 * jnp.sum(jnp.mean(diff * diff, axis=-1, keepdims=True), axis=0, keepdims=True)
        _acc_rows(loss_ref, jnp.broadcast_to(part, (1, 128)), first)

    return _call(
        body, name="loss_head", grid=(s // tm,),
        in_specs=_row_specs(tm, d, 3) + _gain_specs(d, 1),
        out_specs=_row_specs(tm, d, 2) + _gain_specs(d, 1) + [pl.BlockSpec((1, 128), lambda i: (0, 0))],
        out_shape=[jax.ShapeDtypeStruct((s, d), jnp.float32), jax.ShapeDtypeStruct((s, d), jnp.bfloat16),
                   jax.ShapeDtypeStruct((1, d), jnp.float32), jax.ShapeDtypeStruct((1, 128), jnp.float32)],
        semantics=("arbitrary",), args=(x2, mlp, target, g_post_mlp), jobs=jobs)


def _mix_residual_bwd(dx3, dh2, x2, mix, g_pre_mlp, g_post_mix, *, tm, jobs=()):
    s, d = x2.shape

    def body(dx3_ref, dh2_ref, x2_ref, mix_ref, g2_ref, g1_ref, dx2_ref, dmix_ref, dg2_ref, dg1_ref):
        first = pl.program_id(0) == 0
        d_in, dgr2 = _rms_bwd_rows(x2_ref[...], g2_ref[...], dh2_ref[...])
        dx2 = dx3_ref[...] + d_in
        dx2_ref[...] = dx2
        dmix, dgr1 = _rms_bwd_rows(mix_ref[...], g1_ref[...], dx2)
        dmix_ref[...] = dmix.astype(dmix_ref.dtype)
        _acc_rows(dg2_ref, dgr2, first)
        _acc_rows(dg1_ref, dgr1, first)

    return _call(
        body, name="mix_residual_bwd", grid=(s // tm,),
        in_specs=_row_specs(tm, d, 4) + _gain_specs(d, 2),
        out_specs=_row_specs(tm, d, 2) + _gain_specs(d, 2),
        out_shape=[jax.ShapeDtypeStruct((s, d), jnp.float32), jax.ShapeDtypeStruct((s, d), jnp.bfloat16),
                   jax.ShapeDtypeStruct((1, d), jnp.float32), jax.ShapeDtypeStruct((1, d), jnp.float32)],
        semantics=("arbitrary",), args=(dx3, dh2, x2, mix, g_pre_mlp, g_post_mix), jobs=jobs)


def _input_bwd(dx2, dh1, x, g_pre_mix, *, tm, jobs=()):
    s, d = x.shape

    def body(dx2_ref, dh1_ref, x_ref, g_ref, dx_ref, dg_ref):
        d_in, dgr = _rms_bwd_rows(x_ref[...], g_ref[...], dh1_ref[...])
        dx_ref[...] = dx2_ref[...] + d_in
        _acc_rows(dg_ref, dgr, pl.program_id(0) == 0)

    return _call(
        body, name="input_bwd", grid=(s // tm,),
        in_specs=_row_specs(tm, d, 3) + _gain_specs(d, 1),
        out_specs=_row_specs(tm, d, 1) + _gain_specs(d, 1),
        out_shape=[jax.ShapeDtypeStruct((s, d), jnp.float32), jax.ShapeDtypeStruct((1, d), jnp.float32)],
        semantics=("arbitrary",), args=(dx2, dh1, x, g_pre_mix), jobs=jobs)


def _shift_rows(z, by):
    s = z.shape[0]
    rows = lax.broadcasted_iota(jnp.int32, z.shape, 0)
    rolled = pltpu.roll(z, by % s, axis=0)
    keep = rows >= by if by > 0 else rows < s + by
    return jnp.where(keep, rolled, 0.0)


def _conv_fwd(proj, conv_w, conv_out_g, jobs=()):
    s = proj.shape[0]
    groups = CONV_WIDTH // 128

    def body(u_ref, gb_ref, gc_ref, w_ref, g_ref, y_ref):
        z = gc_ref[...] * u_ref[...]
        w = w_ref[...]
        conv = w[0:1, :] * _shift_rows(z, 2) + w[1:2, :] * _shift_rows(z, 1) + w[2:3, :] * z
        y = gb_ref[...] * conv
        y_ref[...] = (y * _rstd(y) * g_ref[...]).astype(y_ref.dtype)

    col = lambda base: pl.BlockSpec((s, 128), lambda j: (0, base + j))
    return _call(
        body, name="conv_fwd", grid=(groups,),
        in_specs=[col(0), col(groups), col(2 * groups), pl.BlockSpec((3, 128), lambda j: (0, j)),
                  pl.BlockSpec((1, 128), lambda j: (0, j))],
        out_specs=[pl.BlockSpec((s, 128), lambda j: (0, j))],
        out_shape=[jax.ShapeDtypeStruct((s, CONV_WIDTH), jnp.bfloat16)],
        semantics=("parallel",), args=(proj, proj, proj, conv_w, conv_out_g), jobs=jobs)[0]


def _conv_bwd(proj, conv_w, conv_out_g, dycat, jobs=()):
    s = proj.shape[0]
    groups = CONV_WIDTH // 128

    def body(u_ref, gb_ref, gc_ref, w_ref, g_ref, dy_ref, du_ref, dgb_ref, dgc_ref, dw_ref, dg_ref):
        u, gb, gc = u_ref[...], gb_ref[...], gc_ref[...]
        w = w_ref[...]
        z = gc * u
        z1, z2 = _shift_rows(z, 1), _shift_rows(z, 2)
        conv = w[0:1, :] * z2 + w[1:2, :] * z1 + w[2:3, :] * z
        dyr, dgr = _rms_bwd_rows(gb * conv, g_ref[...], dy_ref[...])
        dg_ref[...] = jnp.sum(dgr, axis=0, keepdims=True)
        dgb_ref[...] = (dyr * conv).astype(dgb_ref.dtype)
        dconv = dyr * gb
        dw_ref[0:1, :] = jnp.sum(dconv * z2, axis=0, keepdims=True)
        dw_ref[1:2, :] = jnp.sum(dconv * z1, axis=0, keepdims=True)
        dw_ref[2:3, :] = jnp.sum(dconv * z, axis=0, keepdims=True)
        dz = w[2:3, :] * dconv + w[1:2, :] * _shift_rows(dconv, -1) + w[0:1, :] * _shift_rows(dconv, -2)
        dgc_ref[...] = (dz * u).astype(dgc_ref.dtype)
        du_ref[...] = (dz * gc).astype(du_ref.dtype)

    col = lambda base: pl.BlockSpec((s, 128), lambda j: (0, base + j))
    act = jax.ShapeDtypeStruct((s, CONV_WIDTH), jnp.bfloat16)
    return _call(
        body, name="conv_bwd", grid=(groups,),
        in_specs=[col(0), col(groups), col(2 * groups), pl.BlockSpec((3, 128), lambda j: (0, j)),
                  pl.BlockSpec((1, 128), lambda j: (0, j)), col(0)],
        out_specs=[col(0), col(0), col(0), pl.BlockSpec((3, 128), lambda j: (0, j)),
                   pl.BlockSpec((1, 128), lambda j: (0, j))],
        out_shape=[act, act, act, jax.ShapeDtypeStruct((3, CONV_WIDTH), jnp.float32),
                   jax.ShapeDtypeStruct((1, CONV_WIDTH), jnp.float32)],
        semantics=("parallel",), args=(proj, proj, proj, conv_w, conv_out_g, dycat), jobs=jobs)


def _rope_tables(s):
    pos = jnp.arange(s, dtype=jnp.float32)
    inv_freq = jnp.power(ROPE_THETA, -jnp.arange(0, QK_ROPE, 2, dtype=jnp.float32) / QK_ROPE)
    ang = pos[:, None] * inv_freq[None, :]
    cos, sin = jnp.cos(ang), jnp.sin(ang)
    zeros = jnp.zeros((s, 128 - QK_ROPE), jnp.float32)
    return (jnp.concatenate([cos, cos, zeros], axis=1), jnp.concatenate([-sin, sin, zeros], axis=1))


def _swap_halves(x):
    lane = lax.broadcasted_iota(jnp.int32, x.shape, 1)
    swapped = jnp.where(lane < QK_ROPE // 2, pltpu.roll(x, 128 - QK_ROPE // 2, axis=1),
                        pltpu.roll(x, QK_ROPE // 2, axis=1))
    return jnp.where(lane < QK_ROPE, swapped, 0.0)


def _rope(x, cos, sin):
    return x * cos + _swap_halves(x) * sin


def _rope_transposed(d, cos, sin):
    return d * cos + _swap_halves(d * sin)


def _qk_rope_fwd(q_pad, proj, cos, sin, *, tm, jobs=()):
    s = q_pad.shape[0]
    wq = N_HEADS * HEAD_PAD

    def body(q_ref, kr_ref, cos_ref, sin_ref, qo_ref, kro_ref):
        c, sn = cos_ref[...], sin_ref[...]
        for h in range(N_HEADS):
            lo = h * HEAD_PAD
            qo_ref[:, lo:lo + 128] = q_ref[:, lo:lo + 128].astype(qo_ref.dtype)
            qo_ref[:, lo + 128:lo + 256] = _rope(q_ref[:, lo + 128:lo + 256], c, sn).astype(qo_ref.dtype)
        kro_ref[...] = _rope(kr_ref[...], c, sn).astype(kro_ref.dtype)

    return _call(
        body, name="qk_rope_fwd", grid=(s // tm,),
        in_specs=[pl.BlockSpec((tm, wq), lambda i: (i, 0)), pl.BlockSpec((tm, 128), lambda i: (i, COL_KR // 128)),
                  pl.BlockSpec((tm, 128), lambda i: (i, 0)), pl.BlockSpec((tm, 128), lambda i: (i, 0))],
        out_specs=[pl.BlockSpec((tm, wq), lambda i: (i, 0)), pl.BlockSpec((tm, 128), lambda i: (i, 0))],
        out_shape=[jax.ShapeDtypeStruct((s, wq), jnp.bfloat16), jax.ShapeDtypeStruct((s, 128), jnp.bfloat16)],
        semantics=("parallel",), args=(q_pad, proj, cos, sin), jobs=jobs)


def _qk_rope_bwd(dq_pad, dk_pad, dv, cos, sin, *, tm, jobs=()):
    s = dq_pad.shape[0]
    wq = N_HEADS * HEAD_PAD

    def body(dq_ref, dk_ref, dv_ref, cos_ref, sin_ref, dqo_ref, dkv_ref, dkr_ref):
        c, sn = cos_ref[...], sin_ref[...]
        dkr = jnp.zeros((tm, 128), jnp.float32)
        for h in range(N_HEADS):
            lo = h * HEAD_PAD
            dqo_ref[:, lo:lo + 128] = dq_ref[:, lo:lo + 128].astype(dqo_ref.dtype)
            dqo_ref[:, lo + 128:lo + 256] = _rope_transposed(dq_ref[:, lo + 128:lo + 256], c, sn).astype(dqo_ref.dtype)
            dkv_ref[:, lo:lo + 128] = dk_ref[:, lo:lo + 128].astype(dkv_ref.dtype)
            dkv_ref[:, lo + 128:lo + 256] = dv_ref[:, h * V_DIM:(h + 1) * V_DIM].astype(dkv_ref.dtype)
            dkr = dkr + dk_ref[:, lo + 128:lo + 256]
        dkr_ref[...] = _rope_transposed(dkr, c, sn).astype(dkr_ref.dtype)

    return _call(
        body, name="qk_rope_bwd", grid=(s // tm,),
        in_specs=[pl.BlockSpec((tm, wq), lambda i: (i, 0)), pl.BlockSpec((tm, wq), lambda i: (i, 0)),
                  pl.BlockSpec((tm, N_HEADS * V_DIM), lambda i: (i, 0)),
                  pl.BlockSpec((tm, 128), lambda i: (i, 0)), pl.BlockSpec((tm, 128), lambda i: (i, 0))],
        out_specs=[pl.BlockSpec((tm, wq), lambda i: (i, 0)), pl.BlockSpec((tm, wq), lambda i: (i, 0)),
                   pl.BlockSpec((tm, 128), lambda i: (i, 0))],
        out_shape=[jax.ShapeDtypeStruct((s, wq), jnp.bfloat16), jax.ShapeDtypeStruct((s, wq), jnp.bfloat16),
                   jax.ShapeDtypeStruct((s, 128), jnp.bfloat16)],
        semantics=("parallel",), args=(dq_pad, dk_pad, dv, cos, sin), jobs=jobs)


def _visible(q0, k0, t):
    qpos = q0 + lax.broadcasted_iota(jnp.int32, (t, t), 0)
    kpos = k0 + lax.broadcasted_iota(jnp.int32, (t, t), 1)
    return lax.shift_right_logical(kpos, CHUNK_SHIFT) <= lax.shift_right_logical(qpos, CHUNK_SHIFT)


def _attn_fwd(q, kv, kr, attn_out_g, *, t, jobs=()):
    s = q.shape[0]
    nq = s // t

    def body(q_ref, kn_ref, v_ref, kr_ref, g_ref, o_ref, lse_ref, y_ref, kcat_ref):
        i = pl.program_id(1)

        @pl.when(i == 0)
        def _():
            kcat_ref[:, 0:128] = kn_ref[...]
            kcat_ref[:, 128:256] = kr_ref[...]

        qv = q_ref[...]

        def step(j, carry, diagonal):
            m, l, acc = carry
            k = kcat_ref[pl.ds(pl.multiple_of(j * t, t), t), :]
            v = v_ref[pl.ds(pl.multiple_of(j * t, t), t), :]
            sc = lax.dot_general(qv, k, _NT, preferred_element_type=jnp.float32) * ATTN_SCALE
            if diagonal:
                sc = jnp.where(_visible(0, 0, t), sc, NEG_INF)
            m_new = jnp.maximum(m, jnp.max(sc, axis=-1, keepdims=True))
            p = jnp.exp(sc - m_new)
            alpha = jnp.exp(m - m_new)
            l = alpha * l + jnp.sum(p, axis=-1, keepdims=True)
            acc = alpha * acc + lax.dot_general(p.astype(jnp.bfloat16), v, _NN, preferred_element_type=jnp.float32)
            return m_new, l, acc

        init = (jnp.full((t, 1), NEG_INF, jnp.float32), jnp.zeros((t, 1), jnp.float32),
                jnp.zeros((t, V_DIM), jnp.float32))
        before = lax.fori_loop(0, i, functools.partial(step, diagonal=False), init)
        m, l, acc = step(i, before, True)
        o = acc / l
        o_ref[...] = o
        lse_ref[...] = jnp.broadcast_to(m + jnp.log(l), (t, 128))
        y_ref[...] = (o * _rstd(o) * g_ref[...]).astype(y_ref.dtype)

    head_rows = lambda w, f: pl.BlockSpec((s, w), lambda h, i: (0, f(h)))
    blk = pl.BlockSpec((t, 128), lambda h, i: (i, h))
    full = jax.ShapeDtypeStruct((s, N_HEADS * V_DIM), jnp.float32)
    return _call(
        body, name="attn_fwd", grid=(N_HEADS, nq),
        in_specs=[pl.BlockSpec((t, HEAD_PAD), lambda h, i: (i, h)), head_rows(128, lambda h: 2 * h),
                  head_rows(128, lambda h: 2 * h + 1), head_rows(128, lambda h: 0),
                  pl.BlockSpec((1, 128), lambda h, i: (0, h))],
        out_specs=[blk, blk, blk],
        out_shape=[full, full, jax.ShapeDtypeStruct((s, N_HEADS * V_DIM), jnp.bfloat16)],
        scratch_shapes=[pltpu.VMEM((s, HEAD_PAD), jnp.bfloat16)],
        semantics=("arbitrary", "arbitrary"), args=(q, kv, kv, kr, attn_out_g), jobs=jobs)


def _attn_norm_bwd(o, attn_out_g, dycat, jobs=()):
    s = o.shape[0]

    def body(o_ref, g_ref, dy_ref, do_ref, delta_ref, dg_ref):
        ov = o_ref[...]
        do, dgr = _rms_bwd_rows(ov, g_ref[...], dy_ref[...])
        do_ref[...] = do.astype(do_ref.dtype)
        delta_ref[...] = jnp.broadcast_to(jnp.sum(do * ov, axis=-1, keepdims=True), (s, 128))
        dg_ref[...] = jnp.sum(dgr, axis=0, keepdims=True)

    col = lambda base: pl.BlockSpec((s, 128), lambda h: (0, base + h))
    return _call(
        body, name="attn_norm_bwd", grid=(N_HEADS,),
        in_specs=[col(0), pl.BlockSpec((1, 128), lambda h: (0, h)), col(CONV_WIDTH // 128)],
        out_specs=[col(0), col(0), pl.BlockSpec((1, 128), lambda h: (0, h))],
        out_shape=[jax.ShapeDtypeStruct((s, N_HEADS * V_DIM), jnp.bfloat16),
                   jax.ShapeDtypeStruct((s, N_HEADS * V_DIM), jnp.float32),
                   jax.ShapeDtypeStruct((1, N_HEADS * V_DIM), jnp.float32)],
        semantics=("parallel",), args=(o, attn_out_g, dycat), jobs=jobs)


def _attn_bwd(q, kv, kr, do, lse, delta, *, t, jobs=()):
    s = q.shape[0]
    nq = s // t

    def body(q_ref, kn_ref, v_ref, kr_ref, do_ref, lse_ref, delta_ref, dq_ref, dk_ref, dv_ref, kcat_ref):
        kcat_ref[:, 0:128] = kn_ref[...]
        kcat_ref[:, 128:256] = kr_ref[...]
        dq_ref[...] = jnp.zeros_like(dq_ref)
        dk_ref[...] = jnp.zeros_like(dk_ref)
        dv_ref[...] = jnp.zeros_like(dv_ref)

        def kv_step(j, _):
            krows = pl.ds(pl.multiple_of(j * t, t), t)
            k = kcat_ref[krows, :]
            v = v_ref[krows, :]

            def q_step(i, _, diagonal):
                qrows = pl.ds(pl.multiple_of(i * t, t), t)
                qv = q_ref[qrows, :]
                dov = do_ref[qrows, :]
                sc = lax.dot_general(qv, k, _NT, preferred_element_type=jnp.float32) * ATTN_SCALE
                if diagonal:
                    sc = jnp.where(_visible(0, 0, t), sc, NEG_INF)
                p = jnp.exp(sc - lse_ref[qrows, :][:, 0:1])
                dp = lax.dot_general(dov, v, _NT, preferred_element_type=jnp.float32)
                ds = (p * (dp - delta_ref[qrows, :][:, 0:1]) * ATTN_SCALE).astype(jnp.bfloat16)
                dv_ref[krows, :] += lax.dot_general(p.astype(jnp.bfloat16), dov, _TN,
                                                    preferred_element_type=jnp.float32)
                dk_ref[krows, :] += lax.dot_general(ds, qv, _TN, preferred_element_type=jnp.float32)
                dq_ref[qrows, :] += lax.dot_general(ds, k, _NN, preferred_element_type=jnp.float32)
                return 0

            q_step(j, 0, True)
            lax.fori_loop(j + 1, nq, functools.partial(q_step, diagonal=False), 0)
            return 0

        lax.fori_loop(0, nq, kv_step, 0)

    col = lambda w, f: pl.BlockSpec((s, w), lambda h: (0, f(h)))
    return _call(
        body, name="attn_bwd", grid=(N_HEADS,),
        in_specs=[col(HEAD_PAD, lambda h: h), col(128, lambda h: 2 * h), col(128, lambda h: 2 * h + 1),
                  col(128, lambda h: 0), col(128, lambda h: h), col(128, lambda h: h), col(128, lambda h: h)],
        out_specs=[col(HEAD_PAD, lambda h: h), col(HEAD_PAD, lambda h: h), col(128, lambda h: h)],
        out_shape=[jax.ShapeDtypeStruct((s, N_HEADS * HEAD_PAD), jnp.float32),
                   jax.ShapeDtypeStruct((s, N_HEADS * HEAD_PAD), jnp.float32),
                   jax.ShapeDtypeStruct((s, N_HEADS * V_DIM), jnp.float32)],
        scratch_shapes=[pltpu.VMEM((s, HEAD_PAD), jnp.bfloat16)],
        semantics=("parallel",), args=(q, kv, kv, kr, do, lse, delta), jobs=jobs)


def _row_tile(rows):
    for cand in (256, 128, 64, 32, 16, 8):
        if rows % cand == 0:
            return cand
    return rows


def _cast_into_slot(w, pos, *, name, jobs=()):
    r, c = w.shape
    tr = _row_tile(r)

    def body(pos_ref, w_ref, o_ref):
        o_ref[...] = w_ref[...].astype(o_ref.dtype)

    return _call(
        body, name=name, grid=(r // tr,), prefetch=pos,
        in_specs=[pl.BlockSpec((tr, c), lambda i, p: (i, 0))],
        out_specs=[pl.BlockSpec((None, tr, c), lambda i, p: (p[1], i, 0))],
        out_shape=[jax.ShapeDtypeStruct((4, r, c), jnp.bfloat16)],
        semantics=("parallel",), args=(w,), jobs=jobs)[0]


def _cast_many_into_slots(ws, pos, *, name, jobs=()):
    steps = 8
    assert all(w.shape[0] % (16 * steps) == 0 for w in ws), [w.shape for w in ws]

    def body(pos_ref, *refs):
        for w_ref, o_ref in zip(refs[:len(ws)], refs[len(ws):]):
            o_ref[...] = w_ref[...].astype(o_ref.dtype)

    return _call(
        body, name=name, grid=(steps,), prefetch=pos,
        in_specs=[pl.BlockSpec((w.shape[0] // steps, w.shape[1]), lambda i, p: (i, 0)) for w in ws],
        out_specs=[pl.BlockSpec((None, w.shape[0] // steps, w.shape[1]), lambda i, p: (p[1], i, 0)) for w in ws],
        out_shape=[jax.ShapeDtypeStruct((4, *w.shape), jnp.bfloat16) for w in ws],
        semantics=("parallel",), args=tuple(ws), jobs=jobs)


def _pair_add(g, theirs, pos, *, name, jobs=()):
    n, h, c = theirs.shape
    tr = _row_tile(h)
    nb = h // tr

    def body(pos_ref, a_ref, b_ref, o_ref):
        o_ref[...] = (a_ref[...].astype(jnp.float32) + b_ref[...].astype(jnp.float32)).astype(o_ref.dtype)

    spec = pl.BlockSpec((None, tr, c), lambda j, i, p: (j, i, 0))
    return _call(
        body, name=name, grid=(n, nb), prefetch=pos,
        in_specs=[pl.BlockSpec((None, tr, c), lambda j, i, p: (j, i + p[0] * nb, 0)), spec],
        out_specs=[spec], out_shape=[jax.ShapeDtypeStruct(theirs.shape, jnp.bfloat16)],
        semantics=("parallel", "parallel"), args=(g, theirs), jobs=jobs)[0]


def _fold_diag(pair_sum, via, pos, *, name):
    n, h, c = pair_sum.shape
    tr = _row_tile(h // 2)
    nb = h // tr

    def body(pos_ref, p_ref, via_ref, o_ref):
        j, i = pl.program_id(0), pl.program_id(1)
        mine = p_ref[...].astype(jnp.float32)
        add = (j == 0) == (i >= nb // 2)
        o_ref[...] = jnp.where(add, mine + via_ref[...].astype(jnp.float32), mine).astype(o_ref.dtype)

    return _call(
        body, name=name, grid=(2, nb), prefetch=pos,
        in_specs=[pl.BlockSpec((None, tr, c), lambda j, i, p: (jnp.bitwise_xor(p[1], 2 - j), i, 0)),
                  pl.BlockSpec((tr, c), lambda j, i, p: (i, 0))],
        out_specs=[pl.BlockSpec((None, tr, c), lambda j, i, p: (j, i, 0))],
        out_shape=[jax.ShapeDtypeStruct((2, h, c), jnp.bfloat16)],
        semantics=("parallel", "parallel"), args=(pair_sum, via))[0]


def _chip_sum(by_source, pair_sum, pos, *, name):
    n, h, c = by_source.shape
    tr = _row_tile(h)
    nb = h // tr

    def body(pos_ref, own_ref, px_ref, py_ref, o_ref):
        f = lambda ref: ref[...].astype(jnp.float32)
        o_ref[...] = (f(own_ref) + f(px_ref)) + f(py_ref)

    slot = lambda flip: pl.BlockSpec((None, tr, c), lambda i, p: (jnp.bitwise_xor(p[1], flip), i, 0))
    return pl.pallas_call(
        body, name=name, out_shape=jax.ShapeDtypeStruct((2 * h, c), jnp.float32),
        grid_spec=pltpu.PrefetchScalarGridSpec(
            num_scalar_prefetch=1, grid=(nb,),
            in_specs=[slot(0), slot(2), slot(1)],
            out_specs=pl.BlockSpec((tr, c), lambda i, p: (i + p[0] * nb, 0))),
        compiler_params=_params(("parallel",)),
    )(pos, pair_sum, by_source, by_source)


def _adamw(w, g, m, v, *, name, jobs=()):
    r, c = w.shape
    tr = _row_tile(r)

    def body(w_ref, g_ref, m_ref, v_ref, d_ref, mo_ref, vo_ref, go_ref):
        gv = g_ref[...]
        go_ref[...] = gv
        mn = ADAM_B1 * m_ref[...] + (1.0 - ADAM_B1) * gv
        vn = ADAM_B2 * v_ref[...] + (1.0 - ADAM_B2) * (gv * gv)
        m_hat = mn / (1.0 - ADAM_B1 ** ADAM_STEP)
        v_hat = vn / (1.0 - ADAM_B2 ** ADAM_STEP)
        d_ref[...] = -ADAM_LR * (m_hat / (jnp.sqrt(v_hat) + ADAM_EPS) + ADAM_WD * w_ref[...])
        mo_ref[...] = mn
        vo_ref[...] = vn

    spec = pl.BlockSpec((tr, c), lambda i: (i, 0))
    out = jax.ShapeDtypeStruct((r, c), jnp.float32)
    return _call(body, name=name, grid=(r // tr,), in_specs=[spec] * 4, out_specs=[spec] * 4, out_shape=[out] * 4,
                 semantics=("parallel",), args=(w, g, m, v), jobs=jobs)


def _all_reduce_small(block):
    r, c = block.shape

    def body(src_ref, out_ref, stage_ref, send_sems, recv_sems):
        x, y, cc = _position()
        me = 4 * x + 2 * y + cc
        stage_ref[me] = src_ref[...]
        flip = lambda v, on: 1 - v if on else v
        peers = [(flip(x, dx), flip(y, dy), flip(cc, dc)) for dx in (0, 1) for dy in (0, 1) for dc in (0, 1)][1:]
        copies = [pltpu.make_async_remote_copy(
            src_ref=stage_ref.at[me], dst_ref=stage_ref.at[me],
            send_sem=send_sems.at[k], recv_sem=recv_sems.at[k], device_id=peer, device_id_type=MESH)
            for k, peer in enumerate(peers)]
        for cp in copies:
            cp.start()
        for k, (px, py, pc) in enumerate(peers):
            them = 4 * px + 2 * py + pc
            pltpu.make_async_remote_copy(
                src_ref=stage_ref.at[them], dst_ref=stage_ref.at[them],
                send_sem=send_sems.at[k], recv_sem=recv_sems.at[k], device_id=(px, py, pc),
                device_id_type=MESH).wait_recv()
        for cp in copies:
            cp.wait_send()
        total = stage_ref[0]
        for d in range(1, 8):
            total = total + stage_ref[d]
        out_ref[...] = total

    return pl.pallas_call(
        body, name="all_reduce_small",
        in_specs=[pl.BlockSpec(memory_space=pltpu.VMEM)], out_specs=pl.BlockSpec(memory_space=pltpu.VMEM),
        out_shape=jax.ShapeDtypeStruct((r, c), jnp.float32),
        scratch_shapes=[pltpu.VMEM((8, r, c), jnp.float32), pltpu.SemaphoreType.DMA((7,)),
                        pltpu.SemaphoreType.DMA((7,))],
        compiler_params=pltpu.CompilerParams(has_side_effects=True),
    )(block)


def _cols_from_shards(g):
    n, r, c = g.shape
    return jnp.transpose(g, (1, 0, 2)).reshape(r, n * c)


def _cols_to_shards(w, n=4):
    r, c = w.shape
    return jnp.transpose(w.reshape(r, n, c // n), (1, 0, 2))


def _pad_w_in(full):
    d = full.shape[0]
    zeros = jnp.zeros((d, COL_CKV - COL_KR - QK_ROPE), full.dtype)
    return jnp.concatenate([full[:, :COL_KR], full[:, IN_WIDTH - QK_ROPE:], zeros,
                            full[:, COL_KR:COL_KR + KV_RANK]], axis=1)


def _unpad_w_in(padded):
    return jnp.concatenate([padded[:, :COL_KR], padded[:, COL_CKV:COL_CKV + KV_RANK],
                            padded[:, COL_KR:COL_KR + QK_ROPE]], axis=1)


def _pad_w_uq(full):
    r = full.shape[0]
    per_head = full.reshape(r, N_HEADS, QK_NOPE + QK_ROPE)
    return jnp.pad(per_head, ((0, 0), (0, 0), (0, HEAD_PAD - QK_NOPE - QK_ROPE))).reshape(r, N_HEADS * HEAD_PAD)


def _unpad_w_uq(padded):
    r = padded.shape[0]
    return padded.reshape(r, N_HEADS, HEAD_PAD)[:, :, :QK_NOPE + QK_ROPE].reshape(r, N_HEADS * (QK_NOPE + QK_ROPE))


SMALL_ROWS = 16


def _pack_small(d, pre_mix, post_mix, pre_mlp, post_mlp, conv_out, attn_out, q_norm, kv_norm, conv_w):
    row = lambda *parts: jnp.pad(jnp.concatenate(parts, axis=1), ((0, 0), (0, d - sum(p.shape[1] for p in parts))))
    rows = [row(pre_mix), row(post_mix), row(pre_mlp), row(post_mlp), row(conv_out, attn_out), row(q_norm, kv_norm),
            row(conv_w[0:1]), row(conv_w[1:2]), row(conv_w[2:3])]
    return jnp.pad(jnp.concatenate(rows, axis=0), ((0, SMALL_ROWS - len(rows)), (0, 0)))


def _unpack_small(p, chip):
    cw = CONV_WIDTH // 4
    conv_w = lax.dynamic_slice(p[6:9, :CONV_WIDTH], (0, chip * cw), (3, cw))
    return dict(pre_mix_g=p[0:1], post_mix_g=p[1:2], pre_mlp_g=p[2:3], post_mlp_g=p[3:4],
                conv_out_g=p[4:5, :CONV_WIDTH], attn_out_g=p[4:5, CONV_WIDTH:2 * CONV_WIDTH],
                q_norm_g=p[5:6, :Q_RANK], kv_norm_g=p[5:6, Q_RANK:Q_RANK + KV_RANK], conv_w=conv_w[None])


def kernel(x, pre_mix_g, w_in, conv_w, q_norm_g, w_uq, kv_norm_g, w_ukv, conv_out_g, attn_out_g, w_o, post_mix_g, pre_mlp_g, w_up, w_down, post_mlp_g, loss_target, m_pre_mix_g, m_w_in, m_conv_w, m_q_norm_g, m_w_uq, m_kv_norm_g, m_w_ukv, m_conv_out_g, m_attn_out_g, m_w_o, m_post_mix_g, m_pre_mlp_g, m_w_up, m_w_down, m_post_mlp_g, v_pre_mix_g, v_w_in, v_conv_w, v_q_norm_g, v_w_uq, v_kv_norm_g, v_w_ukv, v_conv_out_g, v_attn_out_g, v_w_o, v_post_mix_g, v_pre_mlp_g, v_w_up, v_w_down, v_post_mlp_g):
    bf16 = jnp.bfloat16
    s, d = x.shape[1], x.shape[2]
    d_ff = 4 * d
    chip = 2 * lax.axis_index("x") + lax.axis_index("y")
    xs = x.reshape(s, d)
    target = loss_target.reshape(s, d)
    tm = min(256, s)
    t_attn = min(512, s)
    mt = min(1024, s)
    kt = min(2048, s)

    big = dict(w_in=w_in[0], w_uq=w_uq[0], w_ukv=w_ukv[0], w_o=w_o[0], w_up=w_up[0], w_down=w_down[0])
    names = list(big)
    pos = jnp.stack([lax.axis_index("c"), chip]).astype(jnp.int32)
    wb = {}
    half = {k: big[k].shape[0] // 2 for k in names}

    def rows(k, a, b):
        lo, n = a * half[k] // 64, (b - a) * half[k] // 64
        assert lo % 16 == 0 and n % 16 == 0 and n > 0, (k, a, b)
        return lo, n

    ici = lambda k, a=0, b=64: _GatherIci(wb[k], *rows(k, a, b))
    fwd = lambda k, a=0, b=64: _GatherForward(wb[k], *rows(k, a, b))
    near = lambda k, a=0, b=64: _GatherD2d(wb[k], *rows(k, a, b), slots=(0, 1))
    far = lambda k, a=0, b=64: _GatherD2d(wb[k], *rows(k, a, b), slots=(2,))
    wb["w_in"] = _Buf(_cast_into_slot(big["w_in"], pos, name="cast_w_in"))
    rest = [k for k in names if k != "w_in"]
    for k, slot in zip(rest, _cast_many_into_slots([big[k] for k in rest], pos, name="cast_rest",
                                                   jobs=[ici("w_in")])):
        wb[k] = _Buf(slot)
    h1 = _rms_fwd(xs, pre_mix_g, width=d, col=0, tm=tm, name="rms_pre_mix", jobs=[fwd("w_in"), near("w_in")])
    _comm("gather_w_in", [[far("w_in")]])
    win = _pad_w_in(_cols_from_shards(wb["w_in"].arr))
    ff4 = d_ff // 4

    spread = lambda a: lax.dynamic_update_slice(jnp.zeros((3, CONV_WIDTH), jnp.float32), a[0],
                                                (0, chip * (CONV_WIDTH // 4)))
    conv_w_mine = jnp.where(lax.axis_index("c") == 0, spread(conv_w), 0.0)
    conv_w_full = _all_reduce_small(jnp.pad(conv_w_mine, ((0, 5), (0, 0))))[0:3]

    proj = _matmul(h1, win, dims=_NN, mnk=(s, IN_PAD, d), tiles=(mt, IN_PAD // 3, d), name="mm_proj",
                   jobs=[ici("w_uq"), ici("w_ukv"), ici("w_o")])
    y_conv = _conv_fwd(proj, conv_w_full, conv_out_g,
                       jobs=[fwd("w_uq"), fwd("w_ukv"), fwd("w_o"), near("w_uq"), near("w_ukv"), near("w_o")])
    cqn = _rms_fwd(proj, q_norm_g, width=Q_RANK, col=COL_CQ // Q_RANK, tm=tm, name="rms_q",
                   jobs=[far("w_uq"), far("w_ukv"), far("w_o"), ici("w_up", 0, 4)])
    wuq = _pad_w_uq(_cols_from_shards(wb["w_uq"].arr))
    wukv = _cols_from_shards(wb["w_ukv"].arr)
    wo = wb["w_o"].arr.reshape(-1, d)
    ckvn = _rms_fwd(proj, kv_norm_g, width=KV_RANK, col=COL_CKV // KV_RANK, tm=tm, name="rms_kv",
                    jobs=[ici("w_up", 4, 10), fwd("w_up", 0, 4)])
    q_pad = _matmul(cqn, wuq, dims=_NN, mnk=(s, N_HEADS * HEAD_PAD, Q_RANK), tiles=(mt, 1024, Q_RANK), name="mm_q",
                    jobs=[ici("w_up", 10, 18), fwd("w_up", 4, 10), near("w_up", 0, 4)])
    kv = _matmul(ckvn, wukv, dims=_NN, mnk=(s, N_HEADS * HEAD_PAD, KV_RANK), tiles=(mt, 1024, KV_RANK),
                 name="mm_kv", out_dtype=bf16,
                 jobs=[ici("w_up", 18, 24), fwd("w_up", 10, 18), near("w_up", 4, 10), far("w_up", 0, 4)])
    cos, sin = _rope_tables(s)
    q_rot, kr_rot = _qk_rope_fwd(
        q_pad, proj, cos, sin, tm=tm,
        jobs=[ici("w_up", 24, 32), fwd("w_up", 18, 24), near("w_up", 10, 18), far("w_up", 4, 10)])
    o, lse, y_attn = _attn_fwd(
        q_rot, kv, kr_rot, attn_out_g, t=t_attn,
        jobs=[ici("w_up", 32, 64), ici("w_down", 0, 8), fwd("w_up", 24, 32), near("w_up", 18, 24), far("w_up", 10, 18)])
    ycat = jnp.concatenate([y_conv, y_attn], axis=1)
    mix = _matmul(ycat, wo, dims=_NN, mnk=(s, d, 2 * CONV_WIDTH), tiles=(mt, 1024, 2 * CONV_WIDTH), name="mm_out",
                  jobs=[fwd("w_up", 32, 64), near("w_up", 24, 32), far("w_up", 18, 24)])
    x2, h2 = _mix_residual_fwd(
        xs, mix, post_mix_g, pre_mlp_g, tm=tm,
        jobs=[ici("w_down", 8, 20), fwd("w_down", 0, 8), near("w_up", 32, 64), far("w_up", 24, 64)])
    wup = wb["w_up"].arr

    def up_epilogue(acc, extra_refs, out_refs):
        r = jnp.maximum(acc, 0.0)
        out_refs[0][...] = acc.astype(bf16)
        out_refs[1][...] = (r * r).astype(bf16)

    n_ff = ff4 // 1024
    act = jax.ShapeDtypeStruct((s, d_ff), bf16)
    up, act_sq = _matmul(
        h2, wup, dims=_NN, mnk=(s, d_ff, d), tiles=(mt, 1024, d), name="mm_up",
        b_spec=pl.BlockSpec((None, d, 1024), lambda i, j, l: (j // n_ff, l, j % n_ff)),
        out_shape=(act, act), o_spec=(pl.BlockSpec((mt, 1024), lambda i, j, l: (i, j)),) * 2, epilogue=up_epilogue,
        jobs=[ici("w_down", 20, 64), fwd("w_down", 8, 20), near("w_down", 0, 8)])
    _comm("gather_w_down_tail", [[fwd("w_down", 20, 64), near("w_down", 8, 64), far("w_down", 0, 20)],
                                 [far("w_down", 20, 64)]])
    wdown = wb["w_down"].arr.reshape(d_ff, d)
    mlp = _matmul(act_sq, wdown, dims=_NN, mnk=(s, d, d_ff), tiles=(min(512, s), 512, d_ff), name="mm_down")
    dx3, dmlp, dg_post_mlp, loss_part = _loss_head(x2, mlp, target, post_mlp_g, tm=tm)

    def dup_epilogue(acc, extra_refs, out_refs):
        out_refs[0][...] = (acc * (2.0 * jnp.maximum(extra_refs[0][...].astype(jnp.float32), 0.0))).astype(bf16)

    grads, theirs, pair_sums, via, folded, by_source, whole = {}, {}, {}, {}, {}, {}, {}

    def exchange(k, g):
        grads[k] = g
        theirs[k] = _Buf(jax.ShapeDtypeStruct((4, g.shape[1] // 2, g.shape[2]), bf16))
        return _PairExchange(g, theirs[k])

    def pair_sum(k, jobs=()):
        pair_sums[k] = _pair_add(grads[k], theirs[k].arr, pos, name="pair_add_" + k, jobs=jobs)
        via[k] = _Buf(jax.ShapeDtypeStruct(pair_sums[k].shape[1:], bf16))
        by_source[k] = _Buf(jax.ShapeDtypeStruct(pair_sums[k].shape, bf16))

    def diag(k, a=0, b=32):
        lo, n = a * half[k] // 64, (b - a) * half[k] // 64
        assert lo % 16 == 0 and n % 16 == 0 and n > 0, (k, a, b)
        return _ScatterDiag(pair_sums[k], via[k], lo, n)

    def fold(k):
        folded[k] = _fold_diag(pair_sums[k], via[k].arr, pos, name="fold_" + k)

    scatter = lambda k, a=0, b=64: _ScatterNear(folded[k], by_source[k], *rows(k, a, b))

    def share(k):
        whole[k] = _Buf(_chip_sum(by_source[k].arr, pair_sums[k], pos, name="chip_sum_" + k))
        return _PairShare(whole[k])

    g_wdown = _matmul(act_sq, dmlp, dims=_TN, mnk=(d_ff, d, s), tiles=(1024, 1024, kt), name="mm_gw_down",
                      out_dtype=bf16).reshape(4, ff4, d)
    dup = _matmul(dmlp, wdown, dims=_NT, mnk=(s, d_ff, d), tiles=(mt, 1024, d), name="mm_dact",
                  out_dtype=bf16, epilogue=dup_epilogue, extra=(up,),
                  extra_specs=(pl.BlockSpec((mt, 1024), lambda i, j, l: (i, j)),),
                  jobs=[exchange("w_down", g_wdown)])
    pair_sum("w_down")
    g_wup = _matmul(h2, dup, dims=_TN, mnk=(d, d_ff, s), tiles=(1024, 1024, kt), name="mm_gw_up",
                    out_shape=jax.ShapeDtypeStruct((4, d, ff4), bf16),
                    o_spec=pl.BlockSpec((None, 1024, 1024), lambda i, j, l: (j // n_ff, i, j % n_ff)),
                    jobs=[diag("w_down")])
    fold("w_down")
    dh2 = _matmul(dup, wup, dims=_NT, mnk=(s, d, d_ff), tiles=(mt, 1024, ff4), name="mm_dh2",
                  b_spec=pl.BlockSpec((None, 1024, ff4), lambda i, j, l: (l, j, 0)),
                  jobs=[exchange("w_up", g_wup), scatter("w_down", 0, 48)])
    pair_sum("w_up", jobs=[scatter("w_down", 48, 64)])
    dx2, dmix, dg_pre_mlp, dg_post_mix = _mix_residual_bwd(
        dx3, dh2, x2, mix, pre_mlp_g, post_mix_g, tm=tm, jobs=[diag("w_up", 0, 20)])

    dycat = _matmul(dmix, wo, dims=_NT, mnk=(s, 2 * CONV_WIDTH, d), tiles=(mt, 1024, d), name="mm_dycat",
                    jobs=[diag("w_up", 20, 32)])
    fold("w_up")
    g_wo = _matmul(ycat, dmix, dims=_TN, mnk=(2 * CONV_WIDTH, d, s), tiles=(1024, 1024, kt),
                   name="mm_gw_out", out_dtype=bf16, jobs=[scatter("w_up", 0, 12)]).reshape(4, CONV_WIDTH // 2, d)
    du, dgb, dgc, dg_conv_w, dg_conv_out = _conv_bwd(proj, conv_w_full, conv_out_g, dycat,
                                                     jobs=[exchange("w_o", g_wo), scatter("w_up", 12, 24)])
    pair_sum("w_o")
    do, delta, dg_attn_out = _attn_norm_bwd(o, attn_out_g, dycat)
    dq_pad, dk_pad, dv = _attn_bwd(q_rot, kv, kr_rot, do, lse, delta, t=t_attn,
                                   jobs=[scatter("w_up", 24, 64), diag("w_o"), share("w_down")])
    fold("w_o")
    dq_raw, dkv, dkr = _qk_rope_bwd(dq_pad, dk_pad, dv, cos, sin, tm=tm, jobs=[scatter("w_o")])
    wq_cols = N_HEADS * HEAD_PAD
    g_wuq = _matmul(cqn, dq_raw, dims=_TN, mnk=(Q_RANK, wq_cols, s), tiles=(Q_RANK, 1024, kt),
                    name="mm_gw_uq", out_dtype=bf16)
    dcqn = _matmul(dq_raw, wuq, dims=_NT, mnk=(s, Q_RANK, wq_cols), tiles=(mt, Q_RANK, wq_cols), name="mm_dcq")
    g_wukv = _matmul(ckvn, dkv, dims=_TN, mnk=(KV_RANK, wq_cols, s), tiles=(KV_RANK, 1024, kt),
                     name="mm_gw_ukv", out_dtype=bf16)
    dckvn = _matmul(dkv, wukv, dims=_NT, mnk=(s, KV_RANK, wq_cols), tiles=(mt, KV_RANK, wq_cols), name="mm_dckv",
                    jobs=[exchange("w_uq", _cols_to_shards(_unpad_w_uq(g_wuq))),
                          exchange("w_ukv", _cols_to_shards(g_wukv))])
    pair_sum("w_uq")
    pair_sum("w_ukv")
    dcq, dg_q_norm = _rms_bwd(proj, q_norm_g, dcqn, width=Q_RANK, col=COL_CQ // Q_RANK, tm=tm, name="rms_q_bwd",
                              jobs=[diag("w_uq"), diag("w_ukv")])
    fold("w_uq")
    fold("w_ukv")
    dckv, dg_kv_norm = _rms_bwd(proj, kv_norm_g, dckvn, width=KV_RANK, col=COL_CKV // KV_RANK, tm=tm,
                                name="rms_kv_bwd")
    dproj = jnp.concatenate([du, dgb, dgc, dcq, dkr, jnp.zeros((s, COL_CKV - COL_KR - 128), bf16), dckv], axis=1)
    g_win = _matmul(h1, dproj, dims=_TN, mnk=(d, IN_PAD, s), tiles=(1024, IN_PAD // 3, kt), name="mm_gw_in",
                    out_dtype=bf16, jobs=[scatter("w_uq"), scatter("w_ukv"), share("w_up"), share("w_o")])
    _comm("pair_exchange_w_in", [[exchange("w_in", _cols_to_shards(_unpad_w_in(g_win)))]])
    pair_sum("w_in")
    _comm("scatter_diag_w_in", [[diag("w_in")]])
    fold("w_in")
    dh1 = _matmul(dproj, win, dims=_NT, mnk=(s, d, IN_PAD), tiles=(mt, 1024, IN_PAD // 2), name="mm_dh1",
                  jobs=[scatter("w_in"), share("w_uq"), share("w_ukv")])
    grad_x, dg_pre_mix = _input_bwd(dx2, dh1, xs, pre_mix_g, tm=tm)
    _comm("pair_share_w_in", [[share("w_in")]])

    moments = dict(w_in=(m_w_in, v_w_in), w_uq=(m_w_uq, v_w_uq), w_ukv=(m_w_ukv, v_w_ukv), w_o=(m_w_o, v_w_o),
                   w_up=(m_w_up, v_w_up), w_down=(m_w_down, v_w_down))
    gw, dw, nm, nv = {}, {}, {}, {}
    for k in names:
        delta_k, nm_k, nv_k, g = _adamw(big[k], whole[k].arr, moments[k][0][0], moments[k][1][0], name="adamw_" + k)
        gw[k], dw[k], nm[k], nv[k] = g[None], delta_k[None], nm_k[None], nv_k[None]

    small_g = _all_reduce_small(_pack_small(d, dg_pre_mix, dg_post_mix, dg_pre_mlp, dg_post_mlp, dg_conv_out,
                                            dg_attn_out, dg_q_norm, dg_kv_norm, dg_conv_w
                                            ).at[SMALL_ROWS - 1, :128].set(loss_part[0]))
    loss = small_g[SMALL_ROWS - 1, 0]
    pack_w = lambda cw, pre_mix, post_mix, pre_mlp, post_mlp, conv_out, attn_out, q_norm, kv_norm: _pack_small(
        d, pre_mix, post_mix, pre_mlp, post_mlp, conv_out, attn_out, q_norm, kv_norm, cw)
    small_w = pack_w(conv_w_full, pre_mix_g, post_mix_g, pre_mlp_g, post_mlp_g, conv_out_g, attn_out_g, q_norm_g,
                     kv_norm_g)
    small_m = pack_w(spread(m_conv_w), m_pre_mix_g, m_post_mix_g, m_pre_mlp_g, m_post_mlp_g, m_conv_out_g,
                     m_attn_out_g, m_q_norm_g, m_kv_norm_g)
    small_v = pack_w(spread(v_conv_w), v_pre_mix_g, v_post_mix_g, v_pre_mlp_g, v_post_mlp_g, v_conv_out_g,
                     v_attn_out_g, v_q_norm_g, v_kv_norm_g)
    small_d, small_nm, small_nv, small_g = _adamw(small_w, small_g, small_m, small_v, name="adamw_small")
    sg, sd, snm, snv = (_unpack_small(p, chip) for p in (small_g, small_d, small_nm, small_nv))

    for src, dst in ((sg, gw), (sd, dw), (snm, nm), (snv, nv)):
        dst.update(src)

    order = ["pre_mix_g", "w_in", "conv_w", "q_norm_g", "w_uq", "kv_norm_g", "w_ukv", "conv_out_g", "attn_out_g",
             "w_o", "post_mix_g", "pre_mlp_g", "w_up", "w_down", "post_mlp_g"]
    return (loss, grad_x.reshape(1, s, d), *[gw[k] for k in order], *[dw[k] for k in order],
            *[nm[k] for k in order], *[nv[k] for k in order])
```

```python
import functools

import jax
import jax.numpy as jnp
from jax import lax
from jax.experimental import pallas as pl
from jax.experimental.pallas import tpu as pltpu

EPS = 1e-6
NEG_INF = -1e30
CHUNK_SHIFT = 6
N_HEADS = 8
HEAD_PAD = 256
QK_NOPE = 128
QK_ROPE = 64
V_DIM = 128
CONV_WIDTH = 1024
Q_RANK = 768
KV_RANK = 512
ROPE_THETA = 10000.0
ATTN_SCALE = (QK_NOPE + QK_ROPE) ** -0.5
ADAM_LR, ADAM_B1, ADAM_B2, ADAM_EPS, ADAM_WD, ADAM_STEP = 0.001, 0.9, 0.999, 1e-08, 0.01, 10

COL_CQ = 3 * CONV_WIDTH
COL_KR = COL_CQ + Q_RANK
COL_CKV = 4096
IN_PAD = COL_CKV + KV_RANK
IN_WIDTH = 3 * CONV_WIDTH + Q_RANK + KV_RANK + QK_ROPE

VMEM_LIMIT_BYTES = 56 * 1024 * 1024
MESH = pl.DeviceIdType.MESH
ANY = pl.BlockSpec(memory_space=pl.ANY)

_NN = (((1,), (0,)), ((), ()))
_NT = (((1,), (1,)), ((), ()))
_TN = (((0,), (0,)), ((), ()))


def _params(sem):
    return pltpu.CompilerParams(dimension_semantics=sem, vmem_limit_bytes=VMEM_LIMIT_BYTES)


class _Buf:
    def __init__(self, arr):
        self.arr = arr


def _position():
    return lax.axis_index("x"), lax.axis_index("y"), lax.axis_index("c")


def _other_chips(x, y):
    return [(2 * (1 - x) + y, (1 - x, y)), (2 * x + (1 - y), (x, 1 - y)), (2 * (1 - x) + (1 - y), (1 - x, 1 - y))]


def _remote(src, dst, sems, k, to):
    send, recv, off = sems
    return pltpu.make_async_remote_copy(src_ref=src, dst_ref=dst, send_sem=send.at[off + k], recv_sem=recv.at[off + k],
                                        device_id=to, device_id_type=MESH)


class _GatherIci:
    n_sems = 2
    link = "ici"

    def __init__(self, buf, lo, n):
        self.buf, self.lo, self.n, self.buffers, self.sources = buf, lo, n, [buf], []

    def _rows(self, ref, slot, which, lo=None, n=None):
        lo, n = (self.lo, self.n) if lo is None else (lo, n)
        return ref.at[slot, pl.ds(which * (ref.shape[1] // 2) + lo, n), :]

    def start(self, refs, sems):
        ref = refs[id(self.buf)]
        x, y, c = _position()
        mine = self._rows(ref, 2 * x + y, c)
        for k, (_, xy) in enumerate(_other_chips(x, y)[:2]):
            _remote(mine, mine, sems, k, (*xy, c)).start()

    def wait(self, refs, sems):
        ref = refs[id(self.buf)]
        x, y, c = _position()
        mine = self._rows(ref, 2 * x + y, c)
        for k, (slot, xy) in enumerate(_other_chips(x, y)[:2]):
            landed = self._rows(ref, slot, c)
            _remote(landed, landed, sems, k, (*xy, c)).wait_recv()
            _remote(mine, mine, sems, k, (*xy, c)).wait_send()


class _GatherForward(_GatherIci):
    def _ways(self, x, y):
        (slot_x, xy_x), (slot_y, xy_y), (slot_d, _) = _other_chips(x, y)
        h = self.n // 2
        assert h % 16 == 0, self.n
        return [(slot_x, slot_d, self.lo, xy_y), (slot_y, slot_d, self.lo + h, xy_x)], h

    def start(self, refs, sems):
        ref = refs[id(self.buf)]
        x, y, c = _position()
        ways, h = self._ways(x, y)
        for k, (slot, _, lo, xy) in enumerate(ways):
            rows = self._rows(ref, slot, c, lo, h)
            _remote(rows, rows, sems, k, (*xy, c)).start()

    def wait(self, refs, sems):
        ref = refs[id(self.buf)]
        x, y, c = _position()
        ways, h = self._ways(x, y)
        for k, (slot, lands, lo, xy) in enumerate(ways):
            landed, sent = self._rows(ref, lands, c, lo, h), self._rows(ref, slot, c, lo, h)
            _remote(landed, landed, sems, k, (*xy, c)).wait_recv()
            _remote(sent, sent, sems, k, (*xy, c)).wait_send()


class _GatherD2d(_GatherIci):
    link = "d2d"

    def __init__(self, buf, lo, n, slots):
        super().__init__(buf, lo, n)
        self.slots, self.n_sems = slots, len(slots)

    def start(self, refs, sems):
        ref = refs[id(self.buf)]
        x, y, c = _position()
        chips = _other_chips(x, y)
        for k, which in enumerate(self.slots):
            rows = self._rows(ref, chips[which][0], c)
            _remote(rows, rows, sems, k, (x, y, 1 - c)).start()

    def wait(self, refs, sems):
        ref = refs[id(self.buf)]
        x, y, c = _position()
        chips = _other_chips(x, y)
        for k, which in enumerate(self.slots):
            sent, landed = self._rows(ref, chips[which][0], c), self._rows(ref, chips[which][0], 1 - c)
            _remote(landed, landed, sems, k, (x, y, 1 - c)).wait_recv()
            _remote(sent, sent, sems, k, (x, y, 1 - c)).wait_send()


class _ScatterDiag:
    n_sems = 2
    link = "ici"

    def __init__(self, src, via, lo, n):
        self.src, self.via, self.lo, self.n, self.buffers, self.sources = src, via, lo, n, [via], [src]

    def _copies(self, refs, sems):
        src, via = refs[id(self.src)], refs[id(self.via)]
        x, y, c = _position()
        (_, xy_x), (_, xy_y), (slot_d, _) = _other_chips(x, y)
        h2 = via.shape[0] // 2
        return [_remote(src.at[slot_d, pl.ds(first + self.lo, self.n), :], via.at[pl.ds(first + self.lo, self.n), :],
                        sems, k, (*xy, c)) for k, (first, xy) in enumerate(((0, xy_x), (h2, xy_y)))]

    def start(self, refs, sems):
        for cp in self._copies(refs, sems):
            cp.start()

    def wait(self, refs, sems):
        for cp in self._copies(refs, sems):
            cp.wait_recv()
            cp.wait_send()


class _ScatterNear:
    n_sems = 2
    link = "ici"

    def __init__(self, src, dst, lo, n):
        self.src, self.dst, self.lo, self.n, self.buffers, self.sources = src, dst, lo, n, [dst], [src]

    def _copies(self, refs, sems, landing):
        src, dst = refs[id(self.src)], refs[id(self.dst)]
        x, y, c = _position()
        rows = pl.ds(self.lo, self.n)
        return [_remote(src.at[k, rows, :], dst.at[slot if landing else 2 * x + y, rows, :], sems, k, (*xy, c))
                for k, (slot, xy) in enumerate(_other_chips(x, y)[:2])]

    def start(self, refs, sems):
        for cp in self._copies(refs, sems, False):
            cp.start()

    def wait(self, refs, sems):
        for cp in self._copies(refs, sems, True):
            cp.wait_recv()
            cp.wait_send()


class _PairExchange:
    n_sems = 1
    link = "d2d"

    def __init__(self, src, dst):
        self.src, self.dst, self.buffers, self.sources = src, dst, [dst], [src]

    def _copy(self, refs, sems):
        src, dst = refs[id(self.src)], refs[id(self.dst)]
        x, y, c = _position()
        h = src.shape[1] // 2
        return _remote(src.at[:, pl.ds((1 - c) * h, h), :], dst, sems, 0, (x, y, 1 - c))

    def start(self, refs, sems):
        self._copy(refs, sems).start()

    def wait(self, refs, sems):
        self._copy(refs, sems).wait()


class _PairShare:
    n_sems = 1
    link = "d2d"

    def __init__(self, buf):
        self.buf, self.buffers, self.sources = buf, [buf], []

    def _rows(self, ref, which):
        h = ref.shape[0] // 2
        return ref.at[pl.ds(which * h, h), :]

    def start(self, refs, sems):
        ref = refs[id(self.buf)]
        x, y, c = _position()
        _remote(self._rows(ref, c), self._rows(ref, c), sems, 0, (x, y, 1 - c)).start()

    def wait(self, refs, sems):
        ref = refs[id(self.buf)]
        x, y, c = _position()
        _remote(self._rows(ref, c), self._rows(ref, c), sems, 0, (x, y, 1 - c)).wait_send()
        _remote(self._rows(ref, 1 - c), self._rows(ref, 1 - c), sems, 0, (x, y, 1 - c)).wait_recv()


_COLLECTIVE_IDS = {("ici",): 1, ("d2d",): 2, ("d2d", "ici"): 3}


def _links(jobs):
    return tuple(sorted({j.link for j in jobs}))


def _handshake(links):
    x, y, c = _position()
    peers = ([(1 - x, y, c), (x, 1 - y, c)] if "ici" in links else []) + ([(x, y, 1 - c)] if "d2d" in links else [])
    barrier = pltpu.get_barrier_semaphore()
    for peer in peers:
        pl.semaphore_signal(barrier, inc=1, device_id=peer, device_id_type=MESH)
    pl.semaphore_wait(barrier, len(peers))


def _unique(items):
    seen, out = set(), []
    for it in items:
        if id(it) not in seen:
            seen.add(id(it))
            out.append(it)
    return out


def _job_operands(jobs):
    sources = _unique([a for j in jobs for a in j.sources])
    buffers = _unique([b for j in jobs for b in j.buffers])
    held = [b for b in buffers if not isinstance(b.arr, jax.ShapeDtypeStruct)]
    fresh = [b for b in buffers if isinstance(b.arr, jax.ShapeDtypeStruct)]
    return sources, held, fresh


def _sem_offsets(jobs):
    offs, total = [], 0
    for j in jobs:
        offs.append(total)
        total += j.n_sems
    return offs, total


def _call(body, *, name, grid, in_specs, out_specs, out_shape, args, semantics, scratch_shapes=(), jobs=(),
          prefetch=None):
    n_pre = 0 if prefetch is None else 1

    def launch(fn, in_specs, out_specs, scratch, **kw):
        if prefetch is None:
            return pl.pallas_call(fn, name=name, grid=grid, in_specs=in_specs, out_specs=out_specs,
                                  scratch_shapes=scratch, **kw)
        return pl.pallas_call(fn, name=name, grid_spec=pltpu.PrefetchScalarGridSpec(
            num_scalar_prefetch=1, grid=grid, in_specs=in_specs, out_specs=out_specs, scratch_shapes=scratch), **kw)

    pre = () if prefetch is None else (prefetch,)
    if not jobs:
        return launch(body, list(in_specs), list(out_specs), list(scratch_shapes), out_shape=list(out_shape),
                      compiler_params=_params(semantics))(*pre, *args)
    sources, held, fresh = _job_operands(jobs)
    offs, n_sem = _sem_offsets(jobs)
    links = _links(jobs)
    n_in, n_out, n_scr = len(in_specs), len(out_specs), len(scratch_shapes)
    n_src, n_held, n_fresh = len(sources), len(held), len(fresh)

    def carried(*refs):
        pre_refs, refs = refs[:n_pre], refs[n_pre:]
        ins = refs[:n_in]
        src_refs = refs[n_in:n_in + n_src]
        o0 = n_in + n_src + n_held
        outs = refs[o0:o0 + n_out]
        buf_refs = refs[o0 + n_out:o0 + n_out + n_held + n_fresh]
        s0 = o0 + n_out + n_held + n_fresh
        scratch = refs[s0:s0 + n_scr]
        send, recv = refs[s0 + n_scr], refs[s0 + n_scr + 1]
        where = {id(a): r for a, r in zip(sources, src_refs)}
        where.update({id(b): r for b, r in zip(held + fresh, buf_refs)})
        ids = [pl.program_id(a) for a in range(len(grid))]
        first = functools.reduce(jnp.logical_and, [i == 0 for i in ids])
        last = functools.reduce(jnp.logical_and, [i == g - 1 for i, g in zip(ids, grid)])

        @pl.when(first)
        def _():
            _handshake(links)
            for j, off in zip(jobs, offs):
                j.start(where, (send, recv, off))

        body(*pre_refs, *ins, *outs, *scratch)

        @pl.when(last)
        def _():
            for j, off in zip(jobs, offs):
                j.wait(where, (send, recv, off))

    shape_of = lambda b: jax.ShapeDtypeStruct(b.arr.shape, b.arr.dtype)
    res = launch(
        carried, [*in_specs, *[ANY] * (n_src + n_held)], [*out_specs, *[ANY] * (n_held + n_fresh)],
        [*scratch_shapes, pltpu.SemaphoreType.DMA((n_sem,)), pltpu.SemaphoreType.DMA((n_sem,))],
        out_shape=[*out_shape, *[shape_of(b) for b in held + fresh]],
        input_output_aliases={n_pre + n_in + n_src + i: n_out + i for i in range(n_held)},
        compiler_params=pltpu.CompilerParams(dimension_semantics=("arbitrary",) * len(grid),
                                             vmem_limit_bytes=VMEM_LIMIT_BYTES, has_side_effects=True,
                                             collective_id=_COLLECTIVE_IDS[links]),
    )(*pre, *args, *sources, *[b.arr for b in held])
    for b, new in zip(held + fresh, res[n_out:]):
        b.arr = new
    return list(res[:n_out])


def _comm(name, phases):
    jobs = [j for ph in phases for j in ph]
    sources, held, fresh = _job_operands(jobs)
    offs, n_sem = _sem_offsets(jobs)
    off_of = {id(j): o for j, o in zip(jobs, offs)}
    links = _links(jobs)
    n_src, n_held, n_fresh = len(sources), len(held), len(fresh)

    def body(*refs):
        src_refs = refs[:n_src]
        buf_refs = refs[n_src + n_held:n_src + 2 * n_held + n_fresh]
        send, recv = refs[-2], refs[-1]
        where = {id(a): r for a, r in zip(sources, src_refs)}
        where.update({id(b): r for b, r in zip(held + fresh, buf_refs)})
        _handshake(links)
        for ph in phases:
            for j in ph:
                j.start(where, (send, recv, off_of[id(j)]))
            for j in ph:
                j.wait(where, (send, recv, off_of[id(j)]))

    shape_of = lambda b: jax.ShapeDtypeStruct(b.arr.shape, b.arr.dtype)
    res = pl.pallas_call(
        body, name=name,
        in_specs=[ANY] * (n_src + n_held), out_specs=[ANY] * (n_held + n_fresh),
        out_shape=[shape_of(b) for b in held + fresh],
        input_output_aliases={n_src + i: i for i in range(n_held)},
        scratch_shapes=[pltpu.SemaphoreType.DMA((n_sem,)), pltpu.SemaphoreType.DMA((n_sem,))],
        compiler_params=pltpu.CompilerParams(has_side_effects=True, collective_id=_COLLECTIVE_IDS[links]),
    )(*sources, *[b.arr for b in held])
    for b, new in zip(held + fresh, res):
        b.arr = new


def _matmul(a, b, *, dims, mnk, tiles, name, out_dtype=jnp.float32, a_spec=None, b_spec=None,
            out_shape=None, o_spec=None, epilogue=None, extra=(), extra_specs=(), jobs=()):
    m, n, k = mnk
    tm, tn, tk = tiles
    assert m % tm == 0 and n % tn == 0 and k % tk == 0, (name, mnk, tiles)
    gm, gn, gk = m // tm, n // tn, k // tk
    if a_spec is None:
        a_spec = (pl.BlockSpec((tk, tm), lambda i, j, l: (l, i)) if dims is _TN
                  else pl.BlockSpec((tm, tk), lambda i, j, l: (i, l)))
    if b_spec is None:
        b_spec = (pl.BlockSpec((tn, tk), lambda i, j, l: (j, l)) if dims is _NT
                  else pl.BlockSpec((tk, tn), lambda i, j, l: (l, j)))
    if out_shape is None:
        out_shape = jax.ShapeDtypeStruct((m, n), out_dtype)
    if o_spec is None:
        o_spec = pl.BlockSpec((tm, tn), lambda i, j, l: (i, j))
    single = not isinstance(out_shape, (tuple, list))
    n_extra = len(extra)

    def finish(acc, extra_refs, out_refs):
        if epilogue is None:
            out_refs[0][...] = acc.astype(out_refs[0].dtype)
        else:
            epilogue(acc, extra_refs, out_refs)

    def body_whole_k(*refs):
        a_ref, b_ref = refs[0], refs[1]
        acc = lax.dot_general(a_ref[...], b_ref[...], dims, preferred_element_type=jnp.float32)
        finish(acc, refs[2:2 + n_extra], refs[2 + n_extra:])

    def body_split_k(*refs):
        a_ref, b_ref = refs[0], refs[1]
        extra_refs = refs[2:2 + n_extra]
        out_refs = refs[2 + n_extra:-1]
        acc_ref = refs[-1]
        step = pl.program_id(2)
        part = lax.dot_general(a_ref[...], b_ref[...], dims, preferred_element_type=jnp.float32)

        @pl.when(step == 0)
        def _():
            acc_ref[...] = part

        @pl.when(jnp.logical_and(step > 0, step < gk - 1))
        def _():
            acc_ref[...] += part

        @pl.when(step == gk - 1)
        def _():
            finish(acc_ref[...] + part, extra_refs, out_refs)

    res = _call(
        body_whole_k if gk == 1 else body_split_k, name=name, grid=(gm, gn, gk),
        in_specs=[a_spec, b_spec, *extra_specs],
        out_specs=[o_spec] if single else list(o_spec),
        out_shape=[out_shape] if single else list(out_shape),
        scratch_shapes=[] if gk == 1 else [pltpu.VMEM((tm, tn), jnp.float32)],
        semantics=("parallel", "parallel", "arbitrary"), args=(a, b, *extra), jobs=jobs)
    return res[0] if single else res


def _rstd(x):
    return lax.rsqrt(jnp.mean(x * x, axis=-1, keepdims=True) + EPS)


def _rms_bwd_rows(x, g, dy):
    r = _rstd(x)
    xn = x * r
    dyg = dy * g
    dx = r * (dyg - xn * jnp.mean(xn * dyg, axis=-1, keepdims=True))
    return dx, dy * xn


def _acc_rows(ref, rows, first):
    part = jnp.sum(rows, axis=0, keepdims=True)

    @pl.when(first)
    def _():
        ref[...] = part

    @pl.when(jnp.logical_not(first))
    def _():
        ref[...] += part


def _rms_fwd(x, g, *, width, col, tm, name, jobs=()):
    s = x.shape[0]

    def body(x_ref, g_ref, o_ref):
        v = x_ref[...]
        o_ref[...] = (v * _rstd(v) * g_ref[...]).astype(o_ref.dtype)

    return _call(
        body, name=name, grid=(s // tm,),
        in_specs=[pl.BlockSpec((tm, width), lambda i: (i, col)), pl.BlockSpec((1, width), lambda i: (0, 0))],
        out_specs=[pl.BlockSpec((tm, width), lambda i: (i, 0))],
        out_shape=[jax.ShapeDtypeStruct((s, width), jnp.bfloat16)],
        semantics=("parallel",), args=(x, g), jobs=jobs)[0]


def _rms_bwd(x, g, dy, *, width, col, tm, name, jobs=()):
    s = x.shape[0]

    def body(x_ref, g_ref, dy_ref, dx_ref, dg_ref):
        dx, dgr = _rms_bwd_rows(x_ref[...], g_ref[...], dy_ref[...])
        dx_ref[...] = dx.astype(dx_ref.dtype)
        _acc_rows(dg_ref, dgr, pl.program_id(0) == 0)

    return _call(
        body, name=name, grid=(s // tm,),
        in_specs=[pl.BlockSpec((tm, width), lambda i: (i, col)), pl.BlockSpec((1, width), lambda i: (0, 0)),
                  pl.BlockSpec((tm, width), lambda i: (i, 0))],
        out_specs=[pl.BlockSpec((tm, width), lambda i: (i, 0)), pl.BlockSpec((1, width), lambda i: (0, 0))],
        out_shape=[jax.ShapeDtypeStruct((s, width), jnp.bfloat16), jax.ShapeDtypeStruct((1, width), jnp.float32)],
        semantics=("arbitrary",), args=(x, g, dy), jobs=jobs)


def _row_specs(tm, d, n):
    return [pl.BlockSpec((tm, d), lambda i: (i, 0)) for _ in range(n)]


def _gain_specs(d, n):
    return [pl.BlockSpec((1, d), lambda i: (0, 0)) for _ in range(n)]


def _mix_residual_fwd(x, mix, g_post_mix, g_pre_mlp, *, tm, jobs=()):
    s, d = x.shape

    def body(x_ref, mix_ref, g1_ref, g2_ref, x2_ref, h2_ref):
        mixv = mix_ref[...]
        x2 = x_ref[...] + mixv * _rstd(mixv) * g1_ref[...]
        x2_ref[...] = x2
        h2_ref[...] = (x2 * _rstd(x2) * g2_ref[...]).astype(h2_ref.dtype)

    return _call(
        body, name="mix_residual_fwd", grid=(s // tm,),
        in_specs=_row_specs(tm, d, 2) + _gain_specs(d, 2),
        out_specs=_row_specs(tm, d, 2),
        out_shape=[jax.ShapeDtypeStruct((s, d), jnp.float32), jax.ShapeDtypeStruct((s, d), jnp.bfloat16)],
        semantics=("parallel",), args=(x, mix, g_post_mix, g_pre_mlp), jobs=jobs)


def _loss_head(x2, mlp, target, g_post_mlp, *, tm, jobs=()):
    s, d = x2.shape

    def body(x2_ref, m_ref, t_ref, g_ref, dx3_ref, dm_ref, dg_ref, loss_ref):
        first = pl.program_id(0) == 0
        mv = m_ref[...]
        g = g_ref[...]
        diff = x2_ref[...] + mv * _rstd(mv) * g - t_ref[...]
        dx3 = diff * (1.0 / d)
        dx3_ref[...] = dx3
        dm, dgr = _rms_bwd_rows(mv, g, dx3)
        dm_ref[...] = dm.astype(dm_ref.dtype)
        _acc_rows(dg_ref, dgr, first)
        part = 0.5 * jnp.sum(jnp.mean(diff * diff, axis=-1, keepdims=True), axis=0, keepdims=True)
        _acc_rows(loss_ref, jnp.broadcast_to(part, (1, 128)), first)

    return _call(
        body, name="loss_head", grid=(s // tm,),
        in_specs=_row_specs(tm, d, 3) + _gain_specs(d, 1),
        out_specs=_row_specs(tm, d, 2) + _gain_specs(d, 1) + [pl.BlockSpec((1, 128), lambda i: (0, 0))],
        out_shape=[jax.ShapeDtypeStruct((s, d), jnp.float32), jax.ShapeDtypeStruct((s, d), jnp.bfloat16),
                   jax.ShapeDtypeStruct((1, d), jnp.float32), jax.ShapeDtypeStruct((1, 128), jnp.float32)],
        semantics=("arbitrary",), args=(x2, mlp, target, g_post_mlp), jobs=jobs)


def _mix_residual_bwd(dx3, dh2, x2, mix, g_pre_mlp, g_post_mix, *, tm, jobs=()):
    s, d = x2.shape

    def body(dx3_ref, dh2_ref, x2_ref, mix_ref, g2_ref, g1_ref, dx2_ref, dmix_ref, dg2_ref, dg1_ref):
        first = pl.program_id(0) == 0
        d_in, dgr2 = _rms_bwd_rows(x2_ref[...], g2_ref[...], dh2_ref[...])
        dx2 = dx3_ref[...] + d_in
        dx2_ref[...] = dx2
        dmix, dgr1 = _rms_bwd_rows(mix_ref[...], g1_ref[...], dx2)
        dmix_ref[...] = dmix.astype(dmix_ref.dtype)
        _acc_rows(dg2_ref, dgr2, first)
        _acc_rows(dg1_ref, dgr1, first)

    return _call(
        body, name="mix_residual_bwd", grid=(s // tm,),
        in_specs=_row_specs(tm, d, 4) + _gain_specs(d, 2),
        out_specs=_row_specs(tm, d, 2) + _gain_specs(d, 2),
        out_shape=[jax.ShapeDtypeStruct((s, d), jnp.float32), jax.ShapeDtypeStruct((s, d), jnp.bfloat16),
                   jax.ShapeDtypeStruct((1, d), jnp.float32), jax.ShapeDtypeStruct((1, d), jnp.float32)],
        semantics=("arbitrary",), args=(dx3, dh2, x2, mix, g_pre_mlp, g_post_mix), jobs=jobs)


def _input_bwd(dx2, dh1, x, g_pre_mix, *, tm, jobs=()):
    s, d = x.shape

    def body(dx2_ref, dh1_ref, x_ref, g_ref, dx_ref, dg_ref):
        d_in, dgr = _rms_bwd_rows(x_ref[...], g_ref[...], dh1_ref[...])
        dx_ref[...] = dx2_ref[...] + d_in
        _acc_rows(dg_ref, dgr, pl.program_id(0) == 0)

    return _call(
        body, name="input_bwd", grid=(s // tm,),
        in_specs=_row_specs(tm, d, 3) + _gain_specs(d, 1),
        out_specs=_row_specs(tm, d, 1) + _gain_specs(d, 1),
        out_shape=[jax.ShapeDtypeStruct((s, d), jnp.float32), jax.ShapeDtypeStruct((1, d), jnp.float32)],
        semantics=("arbitrary",), args=(dx2, dh1, x, g_pre_mix), jobs=jobs)


def _shift_rows(z, by):
    s = z.shape[0]
    rows = lax.broadcasted_iota(jnp.int32, z.shape, 0)
    rolled = pltpu.roll(z, by % s, axis=0)
    keep = rows >= by if by > 0 else rows < s + by
    return jnp.where(keep, rolled, 0.0)


def _conv_fwd(proj, conv_w, conv_out_g, jobs=()):
    s = proj.shape[0]
    groups = CONV_WIDTH // 128

    def body(u_ref, gb_ref, gc_ref, w_ref, g_ref, y_ref):
        z = gc_ref[...] * u_ref[...]
        w = w_ref[...]
        conv = w[0:1, :] * _shift_rows(z, 2) + w[1:2, :] * _shift_rows(z, 1) + w[2:3, :] * z
        y = gb_ref[...] * conv
        y_ref[...] = (y * _rstd(y) * g_ref[...]).astype(y_ref.dtype)

    col = lambda base: pl.BlockSpec((s, 128), lambda j: (0, base + j))
    return _call(
        body, name="conv_fwd", grid=(groups,),
        in_specs=[col(0), col(groups), col(2 * groups), pl.BlockSpec((3, 128), lambda j: (0, j)),
                  pl.BlockSpec((1, 128), lambda j: (0, j))],
        out_specs=[pl.BlockSpec((s, 128), lambda j: (0, j))],
        out_shape=[jax.ShapeDtypeStruct((s, CONV_WIDTH), jnp.bfloat16)],
        semantics=("parallel",), args=(proj, proj, proj, conv_w, conv_out_g), jobs=jobs)[0]


def _conv_bwd(proj, conv_w, conv_out_g, dycat, jobs=()):
    s = proj.shape[0]
    groups = CONV_WIDTH // 128

    def body(u_ref, gb_ref, gc_ref, w_ref, g_ref, dy_ref, du_ref, dgb_ref, dgc_ref, dw_ref, dg_ref):
        u, gb, gc = u_ref[...], gb_ref[...], gc_ref[...]
        w = w_ref[...]
        z = gc * u
        z1, z2 = _shift_rows(z, 1), _shift_rows(z, 2)
        conv = w[0:1, :] * z2 + w[1:2, :] * z1 + w[2:3, :] * z
        dyr, dgr = _rms_bwd_rows(gb * conv, g_ref[...], dy_ref[...])
        dg_ref[...] = jnp.sum(dgr, axis=0, keepdims=True)
        dgb_ref[...] = (dyr * conv).astype(dgb_ref.dtype)
        dconv = dyr * gb
        dw_ref[0:1, :] = jnp.sum(dconv * z2, axis=0, keepdims=True)
        dw_ref[1:2, :] = jnp.sum(dconv * z1, axis=0, keepdims=True)
        dw_ref[2:3, :] = jnp.sum(dconv * z, axis=0, keepdims=True)
        dz = w[2:3, :] * dconv + w[1:2, :] * _shift_rows(dconv, -1) + w[0:1, :] * _shift_rows(dconv, -2)
        dgc_ref[...] = (dz * u).astype(dgc_ref.dtype)
        du_ref[...] = (dz * gc).astype(du_ref.dtype)

    col = lambda base: pl.BlockSpec((s, 128), lambda j: (0, base + j))
    act = jax.ShapeDtypeStruct((s, CONV_WIDTH), jnp.bfloat16)
    return _call(
        body, name="conv_bwd", grid=(groups,),
        in_specs=[col(0), col(groups), col(2 * groups), pl.BlockSpec((3, 128), lambda j: (0, j)),
                  pl.BlockSpec((1, 128), lambda j: (0, j)), col(0)],
        out_specs=[col(0), col(0), col(0), pl.BlockSpec((3, 128), lambda j: (0, j)),
                   pl.BlockSpec((1, 128), lambda j: (0, j))],
        out_shape=[act, act, act, jax.ShapeDtypeStruct((3, CONV_WIDTH), jnp.float32),
                   jax.ShapeDtypeStruct((1, CONV_WIDTH), jnp.float32)],
        semantics=("parallel",), args=(proj, proj, proj, conv_w, conv_out_g, dycat), jobs=jobs)


def _rope_tables(s):
    pos = jnp.arange(s, dtype=jnp.float32)
    inv_freq = jnp.power(ROPE_THETA, -jnp.arange(0, QK_ROPE, 2, dtype=jnp.float32) / QK_ROPE)
    ang = pos[:, None] * inv_freq[None, :]
    cos, sin = jnp.cos(ang), jnp.sin(ang)
    zeros = jnp.zeros((s, 128 - QK_ROPE), jnp.float32)
    return (jnp.concatenate([cos, cos, zeros], axis=1), jnp.concatenate([-sin, sin, zeros], axis=1))


def _swap_halves(x):
    lane = lax.broadcasted_iota(jnp.int32, x.shape, 1)
    swapped = jnp.where(lane < QK_ROPE // 2, pltpu.roll(x, 128 - QK_ROPE // 2, axis=1),
                        pltpu.roll(x, QK_ROPE // 2, axis=1))
    return jnp.where(lane < QK_ROPE, swapped, 0.0)


def _rope(x, cos, sin):
    return x * cos + _swap_halves(x) * sin


def _rope_transposed(d, cos, sin):
    return d * cos + _swap_halves(d * sin)


def _qk_rope_fwd(q_pad, proj, cos, sin, *, tm, jobs=()):
    s = q_pad.shape[0]
    wq = N_HEADS * HEAD_PAD

    def body(q_ref, kr_ref, cos_ref, sin_ref, qo_ref, kro_ref):
        c, sn = cos_ref[...], sin_ref[...]
        for h in range(N_HEADS):
            lo = h * HEAD_PAD
            qo_ref[:, lo:lo + 128] = q_ref[:, lo:lo + 128].astype(qo_ref.dtype)
            qo_ref[:, lo + 128:lo + 256] = _rope(q_ref[:, lo + 128:lo + 256], c, sn).astype(qo_ref.dtype)
        kro_ref[...] = _rope(kr_ref[...], c, sn).astype(kro_ref.dtype)

    return _call(
        body, name="qk_rope_fwd", grid=(s // tm,),
        in_specs=[pl.BlockSpec((tm, wq), lambda i: (i, 0)), pl.BlockSpec((tm, 128), lambda i: (i, COL_KR // 128)),
                  pl.BlockSpec((tm, 128), lambda i: (i, 0)), pl.BlockSpec((tm, 128), lambda i: (i, 0))],
        out_specs=[pl.BlockSpec((tm, wq), lambda i: (i, 0)), pl.BlockSpec((tm, 128), lambda i: (i, 0))],
        out_shape=[jax.ShapeDtypeStruct((s, wq), jnp.bfloat16), jax.ShapeDtypeStruct((s, 128), jnp.bfloat16)],
        semantics=("parallel",), args=(q_pad, proj, cos, sin), jobs=jobs)


def _qk_rope_bwd(dq_pad, dk_pad, dv, cos, sin, *, tm, jobs=()):
    s = dq_pad.shape[0]
    wq = N_HEADS * HEAD_PAD

    def body(dq_ref, dk_ref, dv_ref, cos_ref, sin_ref, dqo_ref, dkv_ref, dkr_ref):
        c, sn = cos_ref[...], sin_ref[...]
        dkr = jnp.zeros((tm, 128), jnp.float32)
        for h in range(N_HEADS):
            lo = h * HEAD_PAD
            dqo_ref[:, lo:lo + 128] = dq_ref[:, lo:lo + 128].astype(dqo_ref.dtype)
            dqo_ref[:, lo + 128:lo + 256] = _rope_transposed(dq_ref[:, lo + 128:lo + 256], c, sn).astype(dqo_ref.dtype)
            dkv_ref[:, lo:lo + 128] = dk_ref[:, lo:lo + 128].astype(dkv_ref.dtype)
            dkv_ref[:, lo + 128:lo + 256] = dv_ref[:, h * V_DIM:(h + 1) * V_DIM].astype(dkv_ref.dtype)
            dkr = dkr + dk_ref[:, lo + 128:lo + 256]
        dkr_ref[...] = _rope_transposed(dkr, c, sn).astype(dkr_ref.dtype)

    return _call(
        body, name="qk_rope_bwd", grid=(s // tm,),
        in_specs=[pl.BlockSpec((tm, wq), lambda i: (i, 0)), pl.BlockSpec((tm, wq), lambda i: (i, 0)),
                  pl.BlockSpec((tm, N_HEADS * V_DIM), lambda i: (i, 0)),
                  pl.BlockSpec((tm, 128), lambda i: (i, 0)), pl.BlockSpec((tm, 128), lambda i: (i, 0))],
        out_specs=[pl.BlockSpec((tm, wq), lambda i: (i, 0)), pl.BlockSpec((tm, wq), lambda i: (i, 0)),
                   pl.BlockSpec((tm, 128), lambda i: (i, 0))],
        out_shape=[jax.ShapeDtypeStruct((s, wq), jnp.bfloat16), jax.ShapeDtypeStruct((s, wq), jnp.bfloat16),
                   jax.ShapeDtypeStruct((s, 128), jnp.bfloat16)],
        semantics=("parallel",), args=(dq_pad, dk_pad, dv, cos, sin), jobs=jobs)


def _visible(q0, k0, t):
    qpos = q0 + lax.broadcasted_iota(jnp.int32, (t, t), 0)
    kpos = k0 + lax.broadcasted_iota(jnp.int32, (t, t), 1)
    return lax.shift_right_logical(kpos, CHUNK_SHIFT) <= lax.shift_right_logical(qpos, CHUNK_SHIFT)


def _attn_fwd(q, kv, kr, attn_out_g, *, t, jobs=()):
    s = q.shape[0]
    nq = s // t

    def body(q_ref, kn_ref, v_ref, kr_ref, g_ref, o_ref, lse_ref, y_ref, kcat_ref):
        i = pl.program_id(1)

        @pl.when(i == 0)
        def _():
            kcat_ref[:, 0:128] = kn_ref[...]
            kcat_ref[:, 128:256] = kr_ref[...]

        qv = q_ref[...]

        def step(j, carry, diagonal):
            m, l, acc = carry
            k = kcat_ref[pl.ds(pl.multiple_of(j * t, t), t), :]
            v = v_ref[pl.ds(pl.multiple_of(j * t, t), t), :]
            sc = lax.dot_general(qv, k, _NT, preferred_element_type=jnp.float32) * ATTN_SCALE
            if diagonal:
                sc = jnp.where(_visible(0, 0, t), sc, NEG_INF)
            m_new = jnp.maximum(m, jnp.max(sc, axis=-1, keepdims=True))
            p = jnp.exp(sc - m_new)
            alpha = jnp.exp(m - m_new)
            l = alpha * l + jnp.sum(p, axis=-1, keepdims=True)
            acc = alpha * acc + lax.dot_general(p.astype(jnp.bfloat16), v, _NN, preferred_element_type=jnp.float32)
            return m_new, l, acc

        init = (jnp.full((t, 1), NEG_INF, jnp.float32), jnp.zeros((t, 1), jnp.float32),
                jnp.zeros((t, V_DIM), jnp.float32))
        before = lax.fori_loop(0, i, functools.partial(step, diagonal=False), init)
        m, l, acc = step(i, before, True)
        o = acc / l
        o_ref[...] = o
        lse_ref[...] = jnp.broadcast_to(m + jnp.log(l), (t, 128))
        y_ref[...] = (o * _rstd(o) * g_ref[...]).astype(y_ref.dtype)

    head_rows = lambda w, f: pl.BlockSpec((s, w), lambda h, i: (0, f(h)))
    blk = pl.BlockSpec((t, 128), lambda h, i: (i, h))
    full = jax.ShapeDtypeStruct((s, N_HEADS * V_DIM), jnp.float32)
    return _call(
        body, name="attn_fwd", grid=(N_HEADS, nq),
        in_specs=[pl.BlockSpec((t, HEAD_PAD), lambda h, i: (i, h)), head_rows(128, lambda h: 2 * h),
                  head_rows(128, lambda h: 2 * h + 1), head_rows(128, lambda h: 0),
                  pl.BlockSpec((1, 128), lambda h, i: (0, h))],
        out_specs=[blk, blk, blk],
        out_shape=[full, full, jax.ShapeDtypeStruct((s, N_HEADS * V_DIM), jnp.bfloat16)],
        scratch_shapes=[pltpu.VMEM((s, HEAD_PAD), jnp.bfloat16)],
        semantics=("arbitrary", "arbitrary"), args=(q, kv, kv, kr, attn_out_g), jobs=jobs)


def _attn_norm_bwd(o, attn_out_g, dycat, jobs=()):
    s = o.shape[0]

    def body(o_ref, g_ref, dy_ref, do_ref, delta_ref, dg_ref):
        ov = o_ref[...]
        do, dgr = _rms_bwd_rows(ov, g_ref[...], dy_ref[...])
        do_ref[...] = do.astype(do_ref.dtype)
        delta_ref[...] = jnp.broadcast_to(jnp.sum(do * ov, axis=-1, keepdims=True), (s, 128))
        dg_ref[...] = jnp.sum(dgr, axis=0, keepdims=True)

    col = lambda base: pl.BlockSpec((s, 128), lambda h: (0, base + h))
    return _call(
        body, name="attn_norm_bwd", grid=(N_HEADS,),
        in_specs=[col(0), pl.BlockSpec((1, 128), lambda h: (0, h)), col(CONV_WIDTH // 128)],
        out_specs=[col(0), col(0), pl.BlockSpec((1, 128), lambda h: (0, h))],
        out_shape=[jax.ShapeDtypeStruct((s, N_HEADS * V_DIM), jnp.bfloat16),
                   jax.ShapeDtypeStruct((s, N_HEADS * V_DIM), jnp.float32),
                   jax.ShapeDtypeStruct((1, N_HEADS * V_DIM), jnp.float32)],
        semantics=("parallel",), args=(o, attn_out_g, dycat), jobs=jobs)


def _attn_bwd(q, kv, kr, do, lse, delta, *, t, jobs=()):
    s = q.shape[0]
    nq = s // t

    def body(q_ref, kn_ref, v_ref, kr_ref, do_ref, lse_ref, delta_ref, dq_ref, dk_ref, dv_ref, kcat_ref):
        kcat_ref[:, 0:128] = kn_ref[...]
        kcat_ref[:, 128:256] = kr_ref[...]
        dq_ref[...] = jnp.zeros_like(dq_ref)
        dk_ref[...] = jnp.zeros_like(dk_ref)
        dv_ref[...] = jnp.zeros_like(dv_ref)

        def kv_step(j, _):
            krows = pl.ds(pl.multiple_of(j * t, t), t)
            k = kcat_ref[krows, :]
            v = v_ref[krows, :]

            def q_step(i, _, diagonal):
                qrows = pl.ds(pl.multiple_of(i * t, t), t)
                qv = q_ref[qrows, :]
                dov = do_ref[qrows, :]
                sc = lax.dot_general(qv, k, _NT, preferred_element_type=jnp.float32) * ATTN_SCALE
                if diagonal:
                    sc = jnp.where(_visible(0, 0, t), sc, NEG_INF)
                p = jnp.exp(sc - lse_ref[qrows, :][:, 0:1])
                dp = lax.dot_general(dov, v, _NT, preferred_element_type=jnp.float32)
                ds = (p * (dp - delta_ref[qrows, :][:, 0:1]) * ATTN_SCALE).astype(jnp.bfloat16)
                dv_ref[krows, :] += lax.dot_general(p.astype(jnp.bfloat16), dov, _TN,
                                                    preferred_element_type=jnp.float32)
                dk_ref[krows, :] += lax.dot_general(ds, qv, _TN, preferred_element_type=jnp.float32)
                dq_ref[qrows, :] += lax.dot_general(ds, k, _NN, preferred_element_type=jnp.float32)
                return 0

            q_step(j, 0, True)
            lax.fori_loop(j + 1, nq, functools.partial(q_step, diagonal=False), 0)
            return 0

        lax.fori_loop(0, nq, kv_step, 0)

    col = lambda w, f: pl.BlockSpec((s, w), lambda h: (0, f(h)))
    return _call(
        body, name="attn_bwd", grid=(N_HEADS,),
        in_specs=[col(HEAD_PAD, lambda h: h), col(128, lambda h: 2 * h), col(128, lambda h: 2 * h + 1),
                  col(128, lambda h: 0), col(128, lambda h: h), col(128, lambda h: h), col(128, lambda h: h)],
        out_specs=[col(HEAD_PAD, lambda h: h), col(HEAD_PAD, lambda h: h), col(128, lambda h: h)],
        out_shape=[jax.ShapeDtypeStruct((s, N_HEADS * HEAD_PAD), jnp.float32),
                   jax.ShapeDtypeStruct((s, N_HEADS * HEAD_PAD), jnp.float32),
                   jax.ShapeDtypeStruct((s, N_HEADS * V_DIM), jnp.float32)],
        scratch_shapes=[pltpu.VMEM((s, HEAD_PAD), jnp.bfloat16)],
        semantics=("parallel",), args=(q, kv, kv, kr, do, lse, delta), jobs=jobs)


def _row_tile(rows):
    for cand in (256, 128, 64, 32, 16, 8):
        if rows % cand == 0:
            return cand
    return rows


def _cast_into_slot(w, pos, *, name, jobs=()):
    r, c = w.shape
    tr = _row_tile(r)

    def body(pos_ref, w_ref, o_ref):
        o_ref[...] = w_ref[...].astype(o_ref.dtype)

    return _call(
        body, name=name, grid=(r // tr,), prefetch=pos,
        in_specs=[pl.BlockSpec((tr, c), lambda i, p: (i, 0))],
        out_specs=[pl.BlockSpec((None, tr, c), lambda i, p: (p[1], i, 0))],
        out_shape=[jax.ShapeDtypeStruct((4, r, c), jnp.bfloat16)],
        semantics=("parallel",), args=(w,), jobs=jobs)[0]


def _cast_many_into_slots(ws, pos, *, name, jobs=()):
    steps = 8
    assert all(w.shape[0] % (16 * steps) == 0 for w in ws), [w.shape for w in ws]

    def body(pos_ref, *refs):
        for w_ref, o_ref in zip(refs[:len(ws)], refs[len(ws):]):
            o_ref[...] = w_ref[...].astype(o_ref.dtype)

    return _call(
        body, name=name, grid=(steps,), prefetch=pos,
        in_specs=[pl.BlockSpec((w.shape[0] // steps, w.shape[1]), lambda i, p: (i, 0)) for w in ws],
        out_specs=[pl.BlockSpec((None, w.shape[0] // steps, w.shape[1]), lambda i, p: (p[1], i, 0)) for w in ws],
        out_shape=[jax.ShapeDtypeStruct((4, *w.shape), jnp.bfloat16) for w in ws],
        semantics=("parallel",), args=tuple(ws), jobs=jobs)


def _pair_add(g, theirs, pos, *, name, jobs=()):
    n, h, c = theirs.shape
    tr = _row_tile(h)
    nb = h // tr

    def body(pos_ref, a_ref, b_ref, o_ref):
        o_ref[...] = (a_ref[...].astype(jnp.float32) + b_ref[...].astype(jnp.float32)).astype(o_ref.dtype)

    spec = pl.BlockSpec((None, tr, c), lambda j, i, p: (j, i, 0))
    return _call(
        body, name=name, grid=(n, nb), prefetch=pos,
        in_specs=[pl.BlockSpec((None, tr, c), lambda j, i, p: (j, i + p[0] * nb, 0)), spec],
        out_specs=[spec], out_shape=[jax.ShapeDtypeStruct(theirs.shape, jnp.bfloat16)],
        semantics=("parallel", "parallel"), args=(g, theirs), jobs=jobs)[0]


def _fold_diag(pair_sum, via, pos, *, name):
    n, h, c = pair_sum.shape
    tr = _row_tile(h // 2)
    nb = h // tr

    def body(pos_ref, p_ref, via_ref, o_ref):
        j, i = pl.program_id(0), pl.program_id(1)
        mine = p_ref[...].astype(jnp.float32)
        add = (j == 0) == (i >= nb // 2)
        o_ref[...] = jnp.where(add, mine + via_ref[...].astype(jnp.float32), mine).astype(o_ref.dtype)

    return _call(
        body, name=name, grid=(2, nb), prefetch=pos,
        in_specs=[pl.BlockSpec((None, tr, c), lambda j, i, p: (jnp.bitwise_xor(p[1], 2 - j), i, 0)),
                  pl.BlockSpec((tr, c), lambda j, i, p: (i, 0))],
        out_specs=[pl.BlockSpec((None, tr, c), lambda j, i, p: (j, i, 0))],
        out_shape=[jax.ShapeDtypeStruct((2, h, c), jnp.bfloat16)],
        semantics=("parallel", "parallel"), args=(pair_sum, via))[0]


def _chip_sum(by_source, pair_sum, pos, *, name):
    n, h, c = by_source.shape
    tr = _row_tile(h)
    nb = h // tr

    def body(pos_ref, own_ref, px_ref, py_ref, o_ref):
        f = lambda ref: ref[...].astype(jnp.float32)
        o_ref[...] = (f(own_ref) + f(px_ref)) + f(py_ref)

    slot = lambda flip: pl.BlockSpec((None, tr, c), lambda i, p: (jnp.bitwise_xor(p[1], flip), i, 0))
    return pl.pallas_call(
        body, name=name, out_shape=jax.ShapeDtypeStruct((2 * h, c), jnp.float32),
        grid_spec=pltpu.PrefetchScalarGridSpec(
            num_scalar_prefetch=1, grid=(nb,),
            in_specs=[slot(0), slot(2), slot(1)],
            out_specs=pl.BlockSpec((tr, c), lambda i, p: (i + p[0] * nb, 0))),
        compiler_params=_params(("parallel",)),
    )(pos, pair_sum, by_source, by_source)


def _adamw(w, g, m, v, *, name, jobs=()):
    r, c = w.shape
    tr = _row_tile(r)

    def body(w_ref, g_ref, m_ref, v_ref, d_ref, mo_ref, vo_ref, go_ref):
        gv = g_ref[...]
        go_ref[...] = gv
        mn = ADAM_B1 * m_ref[...] + (1.0 - ADAM_B1) * gv
        vn = ADAM_B2 * v_ref[...] + (1.0 - ADAM_B2) * (gv * gv)
        m_hat = mn / (1.0 - ADAM_B1 ** ADAM_STEP)
        v_hat = vn / (1.0 - ADAM_B2 ** ADAM_STEP)
        d_ref[...] = -ADAM_LR * (m_hat / (jnp.sqrt(v_hat) + ADAM_EPS) + ADAM_WD * w_ref[...])
        mo_ref[...] = mn
        vo_ref[...] = vn

    spec = pl.BlockSpec((tr, c), lambda i: (i, 0))
    out = jax.ShapeDtypeStruct((r, c), jnp.float32)
    return _call(body, name=name, grid=(r // tr,), in_specs=[spec] * 4, out_specs=[spec] * 4, out_shape=[out] * 4,
                 semantics=("parallel",), args=(w, g, m, v), jobs=jobs)


def _all_reduce_small(block):
    r, c = block.shape

    def body(src_ref, out_ref, stage_ref, send_sems, recv_sems):
        x, y, cc = _position()
        me = 4 * x + 2 * y + cc
        stage_ref[me] = src_ref[...]
        flip = lambda v, on: 1 - v if on else v
        peers = [(flip(x, dx), flip(y, dy), flip(cc, dc)) for dx in (0, 1) for dy in (0, 1) for dc in (0, 1)][1:]
        copies = [pltpu.make_async_remote_copy(
            src_ref=stage_ref.at[me], dst_ref=stage_ref.at[me],
            send_sem=send_sems.at[k], recv_sem=recv_sems.at[k], device_id=peer, device_id_type=MESH)
            for k, peer in enumerate(peers)]
        for cp in copies:
            cp.start()
        for k, (px, py, pc) in enumerate(peers):
            them = 4 * px + 2 * py + pc
            pltpu.make_async_remote_copy(
                src_ref=stage_ref.at[them], dst_ref=stage_ref.at[them],
                send_sem=send_sems.at[k], recv_sem=recv_sems.at[k], device_id=(px, py, pc),
                device_id_type=MESH).wait_recv()
        for cp in copies:
            cp.wait_send()
        total = stage_ref[0]
        for d in range(1, 8):
            total = total + stage_ref[d]
        out_ref[...] = total

    return pl.pallas_call(
        body, name="all_reduce_small",
        in_specs=[pl.BlockSpec(memory_space=pltpu.VMEM)], out_specs=pl.BlockSpec(memory_space=pltpu.VMEM),
        out_shape=jax.ShapeDtypeStruct((r, c), jnp.float32),
        scratch_shapes=[pltpu.VMEM((8, r, c), jnp.float32), pltpu.SemaphoreType.DMA((7,)),
                        pltpu.SemaphoreType.DMA((7,))],
        compiler_params=pltpu.CompilerParams(has_side_effects=True),
    )(block)


def _cols_from_shards(g):
    n, r, c = g.shape
    return jnp.transpose(g, (1, 0, 2)).reshape(r, n * c)


def _cols_to_shards(w, n=4):
    r, c = w.shape
    return jnp.transpose(w.reshape(r, n, c // n), (1, 0, 2))


def _pad_w_in(full):
    d = full.shape[0]
    zeros = jnp.zeros((d, COL_CKV - COL_KR - QK_ROPE), full.dtype)
    return jnp.concatenate([full[:, :COL_KR], full[:, IN_WIDTH - QK_ROPE:], zeros,
                            full[:, COL_KR:COL_KR + KV_RANK]], axis=1)


def _unpad_w_in(padded):
    return jnp.concatenate([padded[:, :COL_KR], padded[:, COL_CKV:COL_CKV + KV_RANK],
                            padded[:, COL_KR:COL_KR + QK_ROPE]], axis=1)


def _pad_w_uq(full):
    r = full.shape[0]
    per_head = full.reshape(r, N_HEADS, QK_NOPE + QK_ROPE)
    return jnp.pad(per_head, ((0, 0), (0, 0), (0, HEAD_PAD - QK_NOPE - QK_ROPE))).reshape(r, N_HEADS * HEAD_PAD)


def _unpad_w_uq(padded):
    r = padded.shape[0]
    return padded.reshape(r, N_HEADS, HEAD_PAD)[:, :, :QK_NOPE + QK_ROPE].reshape(r, N_HEADS * (QK_NOPE + QK_ROPE))


SMALL_ROWS = 16


def _pack_small(d, pre_mix, post_mix, pre_mlp, post_mlp, conv_out, attn_out, q_norm, kv_norm, conv_w):
    row = lambda *parts: jnp.pad(jnp.concatenate(parts, axis=1), ((0, 0), (0, d - sum(p.shape[1] for p in parts))))
    rows = [row(pre_mix), row(post_mix), row(pre_mlp), row(post_mlp), row(conv_out, attn_out), row(q_norm, kv_norm),
            row(conv_w[0:1]), row(conv_w[1:2]), row(conv_w[2:3])]
    return jnp.pad(jnp.concatenate(rows, axis=0), ((0, SMALL_ROWS - len(rows)), (0, 0)))


def _unpack_small(p, chip):
    cw = CONV_WIDTH // 4
    conv_w = lax.dynamic_slice(p[6:9, :CONV_WIDTH], (0, chip * cw), (3, cw))
    return dict(pre_mix_g=p[0:1], post_mix_g=p[1:2], pre_mlp_g=p[2:3], post_mlp_g=p[3:4],
                conv_out_g=p[4:5, :CONV_WIDTH], attn_out_g=p[4:5, CONV_WIDTH:2 * CONV_WIDTH],
                q_norm_g=p[5:6, :Q_RANK], kv_norm_g=p[5:6, Q_RANK:Q_RANK + KV_RANK], conv_w=conv_w[None])


def kernel(x, pre_mix_g, w_in, conv_w, q_norm_g, w_uq, kv_norm_g, w_ukv, conv_out_g, attn_out_g, w_o, post_mix_g, pre_mlp_g, w_up, w_down, post_mlp_g, loss_target, m_pre_mix_g, m_w_in, m_conv_w, m_q_norm_g, m_w_uq, m_kv_norm_g, m_w_ukv, m_conv_out_g, m_attn_out_g, m_w_o, m_post_mix_g, m_pre_mlp_g, m_w_up, m_w_down, m_post_mlp_g, v_pre_mix_g, v_w_in, v_conv_w, v_q_norm_g, v_w_uq, v_kv_norm_g, v_w_ukv, v_conv_out_g, v_attn_out_g, v_w_o, v_post_mix_g, v_pre_mlp_g, v_w_up, v_w_down, v_post_mlp_g):
    bf16 = jnp.bfloat16
    s, d = x.shape[1], x.shape[2]
    d_ff = 4 * d
    chip = 2 * lax.axis_index("x") + lax.axis_index("y")
    xs = x.reshape(s, d)
    target = loss_target.reshape(s, d)
    tm = min(256, s)
    t_attn = min(512, s)
    mt = min(1024, s)
    kt = min(2048, s)

    big = dict(w_in=w_in[0], w_uq=w_uq[0], w_ukv=w_ukv[0], w_o=w_o[0], w_up=w_up[0], w_down=w_down[0])
    names = list(big)
    pos = jnp.stack([lax.axis_index("c"), chip]).astype(jnp.int32)
    wb = {}
    half = {k: big[k].shape[0] // 2 for k in names}

    def rows(k, a, b):
        lo, n = a * half[k] // 64, (b - a) * half[k] // 64
        assert lo % 16 == 0 and n % 16 == 0 and n > 0, (k, a, b)
        return lo, n

    ici = lambda k, a=0, b=64: _GatherIci(wb[k], *rows(k, a, b))
    fwd = lambda k, a=0, b=64: _GatherForward(wb[k], *rows(k, a, b))
    near = lambda k, a=0, b=64: _GatherD2d(wb[k], *rows(k, a, b), slots=(0, 1))
    far = lambda k, a=0, b=64: _GatherD2d(wb[k], *rows(k, a, b), slots=(2,))
    wb["w_in"] = _Buf(_cast_into_slot(big["w_in"], pos, name="cast_w_in"))
    rest = [k for k in names if k != "w_in"]
    for k, slot in zip(rest, _cast_many_into_slots([big[k] for k in rest], pos, name="cast_rest",
                                                   jobs=[ici("w_in")])):
        wb[k] = _Buf(slot)
    h1 = _rms_fwd(xs, pre_mix_g, width=d, col=0, tm=tm, name="rms_pre_mix", jobs=[fwd("w_in"), near("w_in")])
    _comm("gather_w_in", [[far("w_in")]])
    win = _pad_w_in(_cols_from_shards(wb["w_in"].arr))
    ff4 = d_ff // 4

    spread = lambda a: lax.dynamic_update_slice(jnp.zeros((3, CONV_WIDTH), jnp.float32), a[0],
                                                (0, chip * (CONV_WIDTH // 4)))
    conv_w_mine = jnp.where(lax.axis_index("c") == 0, spread(conv_w), 0.0)
    conv_w_full = _all_reduce_small(jnp.pad(conv_w_mine, ((0, 5), (0, 0))))[0:3]

    proj = _matmul(h1, win, dims=_NN, mnk=(s, IN_PAD, d), tiles=(mt, IN_PAD // 3, d), name="mm_proj",
                   jobs=[ici("w_uq"), ici("w_ukv"), ici("w_o")])
    y_conv = _conv_fwd(proj, conv_w_full, conv_out_g,
                       jobs=[fwd("w_uq"), fwd("w_ukv"), fwd("w_o"), near("w_uq"), near("w_ukv"), near("w_o")])
    cqn = _rms_fwd(proj, q_norm_g, width=Q_RANK, col=COL_CQ // Q_RANK, tm=tm, name="rms_q",
                   jobs=[far("w_uq"), far("w_ukv"), far("w_o"), ici("w_up", 0, 4)])
    wuq = _pad_w_uq(_cols_from_shards(wb["w_uq"].arr))
    wukv = _cols_from_shards(wb["w_ukv"].arr)
    wo = wb["w_o"].arr.reshape(-1, d)
    ckvn = _rms_fwd(proj, kv_norm_g, width=KV_RANK, col=COL_CKV // KV_RANK, tm=tm, name="rms_kv",
                    jobs=[ici("w_up", 4, 10), fwd("w_up", 0, 4)])
    q_pad = _matmul(cqn, wuq, dims=_NN, mnk=(s, N_HEADS * HEAD_PAD, Q_RANK), tiles=(mt, 1024, Q_RANK), name="mm_q",
                    jobs=[ici("w_up", 10, 18), fwd("w_up", 4, 10), near("w_up", 0, 4)])
    kv = _matmul(ckvn, wukv, dims=_NN, mnk=(s, N_HEADS * HEAD_PAD, KV_RANK), tiles=(mt, 1024, KV_RANK),
                 name="mm_kv", out_dtype=bf16,
                 jobs=[ici("w_up", 18, 24), fwd("w_up", 10, 18), near("w_up", 4, 10), far("w_up", 0, 4)])
    cos, sin = _rope_tables(s)
    q_rot, kr_rot = _qk_rope_fwd(
        q_pad, proj, cos, sin, tm=tm,
        jobs=[ici("w_up", 24, 32), fwd("w_up", 18, 24), near("w_up", 10, 18), far("w_up", 4, 10)])
    o, lse, y_attn = _attn_fwd(
        q_rot, kv, kr_rot, attn_out_g, t=t_attn,
        jobs=[ici("w_up", 32, 64), ici("w_down", 0, 8), fwd("w_up", 24, 32), near("w_up", 18, 24), far("w_up", 10, 18)])
    ycat = jnp.concatenate([y_conv, y_attn], axis=1)
    mix = _matmul(ycat, wo, dims=_NN, mnk=(s, d, 2 * CONV_WIDTH), tiles=(mt, 1024, 2 * CONV_WIDTH), name="mm_out",
                  jobs=[fwd("w_up", 32, 64), near("w_up", 24, 32), far("w_up", 18, 24)])
    x2, h2 = _mix_residual_fwd(
        xs, mix, post_mix_g, pre_mlp_g, tm=tm,
        jobs=[ici("w_down", 8, 20), fwd("w_down", 0, 8), near("w_up", 32, 64), far("w_up", 24, 64)])
    wup = wb["w_up"].arr

    def up_epilogue(acc, extra_refs, out_refs):
        r = jnp.maximum(acc, 0.0)
        out_refs[0][...] = acc.astype(bf16)
        out_refs[1][...] = (r * r).astype(bf16)

    n_ff = ff4 // 1024
    act = jax.ShapeDtypeStruct((s, d_ff), bf16)
    up, act_sq = _matmul(
        h2, wup, dims=_NN, mnk=(s, d_ff, d), tiles=(mt, 1024, d), name="mm_up",
        b_spec=pl.BlockSpec((None, d, 1024), lambda i, j, l: (j // n_ff, l, j % n_ff)),
        out_shape=(act, act), o_spec=(pl.BlockSpec((mt, 1024), lambda i, j, l: (i, j)),) * 2, epilogue=up_epilogue,
        jobs=[ici("w_down", 20, 64), fwd("w_down", 8, 20), near("w_down", 0, 8)])
    _comm("gather_w_down_tail", [[fwd("w_down", 20, 64), near("w_down", 8, 64), far("w_down", 0, 20)],
                                 [far("w_down", 20, 64)]])
    wdown = wb["w_down"].arr.reshape(d_ff, d)
    mlp = _matmul(act_sq, wdown, dims=_NN, mnk=(s, d, d_ff), tiles=(min(512, s), 512, d_ff), name="mm_down")
    dx3, dmlp, dg_post_mlp, loss_part = _loss_head(x2, mlp, target, post_mlp_g, tm=tm)

    def dup_epilogue(acc, extra_refs, out_refs):
        out_refs[0][...] = (acc * (2.0 * jnp.maximum(extra_refs[0][...].astype(jnp.float32), 0.0))).astype(bf16)

    grads, theirs, pair_sums, via, folded, by_source, whole = {}, {}, {}, {}, {}, {}, {}

    def exchange(k, g):
        grads[k] = g
        theirs[k] = _Buf(jax.ShapeDtypeStruct((4, g.shape[1] // 2, g.shape[2]), bf16))
        return _PairExchange(g, theirs[k])

    def pair_sum(k, jobs=()):
        pair_sums[k] = _pair_add(grads[k], theirs[k].arr, pos, name="pair_add_" + k, jobs=jobs)
        via[k] = _Buf(jax.ShapeDtypeStruct(pair_sums[k].shape[1:], bf16))
        by_source[k] = _Buf(jax.ShapeDtypeStruct(pair_sums[k].shape, bf16))

    def diag(k, a=0, b=32):
        lo, n = a * half[k] // 64, (b - a) * half[k] // 64
        assert lo % 16 == 0 and n % 16 == 0 and n > 0, (k, a, b)
        return _ScatterDiag(pair_sums[k], via[k], lo, n)

    def fold(k):
        folded[k] = _fold_diag(pair_sums[k], via[k].arr, pos, name="fold_" + k)

    scatter = lambda k, a=0, b=64: _ScatterNear(folded[k], by_source[k], *rows(k, a, b))

    def share(k):
        whole[k] = _Buf(_chip_sum(by_source[k].arr, pair_sums[k], pos, name="chip_sum_" + k))
        return _PairShare(whole[k])

    g_wdown = _matmul(act_sq, dmlp, dims=_TN, mnk=(d_ff, d, s), tiles=(1024, 1024, kt), name="mm_gw_down",
                      out_dtype=bf16).reshape(4, ff4, d)
    dup = _matmul(dmlp, wdown, dims=_NT, mnk=(s, d_ff, d), tiles=(mt, 1024, d), name="mm_dact",
                  out_dtype=bf16, epilogue=dup_epilogue, extra=(up,),
                  extra_specs=(pl.BlockSpec((mt, 1024), lambda i, j, l: (i, j)),),
                  jobs=[exchange("w_down", g_wdown)])
    pair_sum("w_down")
    g_wup = _matmul(h2, dup, dims=_TN, mnk=(d, d_ff, s), tiles=(1024, 1024, kt), name="mm_gw_up",
                    out_shape=jax.ShapeDtypeStruct((4, d, ff4), bf16),
                    o_spec=pl.BlockSpec((None, 1024, 1024), lambda i, j, l: (j // n_ff, i, j % n_ff)),
                    jobs=[diag("w_down")])
    fold("w_down")
    dh2 = _matmul(dup, wup, dims=_NT, mnk=(s, d, d_ff), tiles=(mt, 1024, ff4), name="mm_dh2",
                  b_spec=pl.BlockSpec((None, 1024, ff4), lambda i, j, l: (l, j, 0)),
                  jobs=[exchange("w_up", g_wup), scatter("w_down", 0, 48)])
    pair_sum("w_up", jobs=[scatter("w_down", 48, 64)])
    dx2, dmix, dg_pre_mlp, dg_post_mix = _mix_residual_bwd(
        dx3, dh2, x2, mix, pre_mlp_g, post_mix_g, tm=tm, jobs=[diag("w_up", 0, 20)])

    dycat = _matmul(dmix, wo, dims=_NT, mnk=(s, 2 * CONV_WIDTH, d), tiles=(mt, 1024, d), name="mm_dycat",
                    jobs=[diag("w_up", 20, 32)])
    fold("w_up")
    g_wo = _matmul(ycat, dmix, dims=_TN, mnk=(2 * CONV_WIDTH, d, s), tiles=(1024, 1024, kt),
                   name="mm_gw_out", out_dtype=bf16, jobs=[scatter("w_up", 0, 12)]).reshape(4, CONV_WIDTH // 2, d)
    du, dgb, dgc, dg_conv_w, dg_conv_out = _conv_bwd(proj, conv_w_full, conv_out_g, dycat,
                                                     jobs=[exchange("w_o", g_wo), scatter("w_up", 12, 24)])
    pair_sum("w_o")
    do, delta, dg_attn_out = _attn_norm_bwd(o, attn_out_g, dycat)
    dq_pad, dk_pad, dv = _attn_bwd(q_rot, kv, kr_rot, do, lse, delta, t=t_attn,
                                   jobs=[scatter("w_up", 24, 64), diag("w_o"), share("w_down")])
    fold("w_o")
    dq_raw, dkv, dkr = _qk_rope_bwd(dq_pad, dk_pad, dv, cos, sin, tm=tm, jobs=[scatter("w_o")])
    wq_cols = N_HEADS * HEAD_PAD
    g_wuq = _matmul(cqn, dq_raw, dims=_TN, mnk=(Q_RANK, wq_cols, s), tiles=(Q_RANK, 1024, kt),
                    name="mm_gw_uq", out_dtype=bf16)
    dcqn = _matmul(dq_raw, wuq, dims=_NT, mnk=(s, Q_RANK, wq_cols), tiles=(mt, Q_RANK, wq_cols), name="mm_dcq")
    g_wukv = _matmul(ckvn, dkv, dims=_TN, mnk=(KV_RANK, wq_cols, s), tiles=(KV_RANK, 1024, kt),
                     name="mm_gw_ukv", out_dtype=bf16)
    dckvn = _matmul(dkv, wukv, dims=_NT, mnk=(s, KV_RANK, wq_cols), tiles=(mt, KV_RANK, wq_cols), name="mm_dckv",
                    jobs=[exchange("w_uq", _cols_to_shards(_unpad_w_uq(g_wuq))),
                          exchange("w_ukv", _cols_to_shards(g_wukv))])
    pair_sum("w_uq")
    pair_sum("w_ukv")
    dcq, dg_q_norm = _rms_bwd(proj, q_norm_g, dcqn, width=Q_RANK, col=COL_CQ // Q_RANK, tm=tm, name="rms_q_bwd",
                              jobs=[diag("w_uq"), diag("w_ukv")])
    fold("w_uq")
    fold("w_ukv")
    dckv, dg_kv_norm = _rms_bwd(proj, kv_norm_g, dckvn, width=KV_RANK, col=COL_CKV // KV_RANK, tm=tm,
                                name="rms_kv_bwd")
    dproj = jnp.concatenate([du, dgb, dgc, dcq, dkr, jnp.zeros((s, COL_CKV - COL_KR - 128), bf16), dckv], axis=1)
    g_win = _matmul(h1, dproj, dims=_TN, mnk=(d, IN_PAD, s), tiles=(1024, IN_PAD // 3, kt), name="mm_gw_in",
                    out_dtype=bf16, jobs=[scatter("w_uq"), scatter("w_ukv"), share("w_up"), share("w_o")])
    _comm("pair_exchange_w_in", [[exchange("w_in", _cols_to_shards(_unpad_w_in(g_win)))]])
    pair_sum("w_in")
    _comm("scatter_diag_w_in", [[diag("w_in")]])
    fold("w_in")
    dh1 = _matmul(dproj, win, dims=_NT, mnk=(s, d, IN_PAD), tiles=(mt, 1024, IN_PAD // 2), name="mm_dh1",
                  jobs=[scatter("w_in"), share("w_uq"), share("w_ukv")])
    grad_x, dg_pre_mix = _input_bwd(dx2, dh1, xs, pre_mix_g, tm=tm)
    _comm("pair_share_w_in", [[share("w_in")]])

    moments = dict(w_in=(m_w_in, v_w_in), w_uq=(m_w_uq, v_w_uq), w_ukv=(m_w_ukv, v_w_ukv), w_o=(m_w_o, v_w_o),
                   w_up=(m_w_up, v_w_up), w_down=(m_w_down, v_w_down))
    gw, dw, nm, nv = {}, {}, {}, {}
    for k in names:
        delta_k, nm_k, nv_k, g = _adamw(big[k], whole[k].arr, moments[k][0][0], moments[k][1][0], name="adamw_" + k)
        gw[k], dw[k], nm[k], nv[k] = g[None], delta_k[None], nm_k[None], nv_k[None]

    small_g = _all_reduce_small(_pack_small(d, dg_pre_mix, dg_post_mix, dg_pre_mlp, dg_post_mlp, dg_conv_out,
                                            dg_attn_out, dg_q_norm, dg_kv_norm, dg_conv_w
                                            ).at[SMALL_ROWS - 1, :128].set(loss_part[0]))
    loss = small_g[SMALL_ROWS - 1, 0]
    pack_w = lambda cw, pre_mix, post_mix, pre_mlp, post_mlp, conv_out, attn_out, q_norm, kv_norm: _pack_small(
        d, pre_mix, post_mix, pre_mlp, post_mlp, conv_out, attn_out, q_norm, kv_norm, cw)
    small_w = pack_w(conv_w_full, pre_mix_g, post_mix_g, pre_mlp_g, post_mlp_g, conv_out_g, attn_out_g, q_norm_g,
                     kv_norm_g)
    small_m = pack_w(spread(m_conv_w), m_pre_mix_g, m_post_mix_g, m_pre_mlp_g, m_post_mlp_g, m_conv_out_g,
                     m_attn_out_g, m_q_norm_g, m_kv_norm_g)
    small_v = pack_w(spread(v_conv_w), v_pre_mix_g, v_post_mix_g, v_pre_mlp_g, v_post_mlp_g, v_conv_out_g,
                     v_attn_out_g, v_q_norm_g, v_kv_norm_g)
    small_d, small_nm, small_nv, small_g = _adamw(small_w, small_g, small_m, small_v, name="adamw_small")
    sg, sd, snm, snv = (_unpack_small(p, chip) for p in (small_g, small_d, small_nm, small_nv))

    for src, dst in ((sg, gw), (sd, dw), (snm, nm), (snv, nv)):
        dst.update(src)

    order = ["pre_mix_g", "w_in", "conv_w", "q_norm_g", "w_uq", "kv_norm_g", "w_ukv", "conv_out_g", "attn_out_g",
             "w_o", "post_mix_g", "pre_mlp_g", "w_up", "w_down", "post_mlp_g"]
    return (loss, grad_x.reshape(1, s, d), *[gw[k] for k in order], *[dw[k] for k in order],
            *[nm[k] for k in order], *[nv[k] for k in order])
```

```python
import functools

import jax
import jax.numpy as jnp
from jax import lax
from jax.experimental import pallas as pl
from jax.experimental.pallas import tpu as pltpu

EPS = 1e-6
NEG_INF = -1e30
CHUNK_SHIFT = 6
N_HEADS = 8
HEAD_PAD = 256
QK_NOPE = 128
QK_ROPE = 64
V_DIM = 128
CONV_WIDTH = 1024
Q_RANK = 768
KV_RANK = 512
ROPE_THETA = 10000.0
ATTN_SCALE = (QK_NOPE + QK_ROPE) ** -0.5
ADAM_LR, ADAM_B1, ADAM_B2, ADAM_EPS, ADAM_WD, ADAM_STEP = 0.001, 0.9, 0.999, 1e-08, 0.01, 10

COL_CQ = 3 * CONV_WIDTH
COL_KR = COL_CQ + Q_RANK
COL_CKV = 4096
IN_PAD = COL_CKV + KV_RANK
IN_WIDTH = 3 * CONV_WIDTH + Q_RANK + KV_RANK + QK_ROPE

VMEM_LIMIT_BYTES = 56 * 1024 * 1024
MESH = pl.DeviceIdType.MESH
ANY = pl.BlockSpec(memory_space=pl.ANY)

_NN = (((1,), (0,)), ((), ()))
_NT = (((1,), (1,)), ((), ()))
_TN = (((0,), (0,)), ((), ()))


def _params(sem):
    return pltpu.CompilerParams(dimension_semantics=sem, vmem_limit_bytes=VMEM_LIMIT_BYTES)


class _Buf:
    def __init__(self, arr):
        self.arr = arr


def _position():
    return lax.axis_index("x"), lax.axis_index("y"), lax.axis_index("c")


def _other_chips(x, y):
    return [(2 * (1 - x) + y, (1 - x, y)), (2 * x + (1 - y), (x, 1 - y)), (2 * (1 - x) + (1 - y), (1 - x, 1 - y))]


def _remote(src, dst, sems, k, to):
    send, recv, off = sems
    return pltpu.make_async_remote_copy(src_ref=src, dst_ref=dst, send_sem=send.at[off + k], recv_sem=recv.at[off + k],
                                        device_id=to, device_id_type=MESH)


class _GatherIci:
    n_sems = 2
    link = "ici"

    def __init__(self, buf, lo, n):
        self.buf, self.lo, self.n, self.buffers, self.sources = buf, lo, n, [buf], []

    def _rows(self, ref, slot, which, lo=None, n=None):
        lo, n = (self.lo, self.n) if lo is None else (lo, n)
        return ref.at[slot, pl.ds(which * (ref.shape[1] // 2) + lo, n), :]

    def start(self, refs, sems):
        ref = refs[id(self.buf)]
        x, y, c = _position()
        mine = self._rows(ref, 2 * x + y, c)
        for k, (_, xy) in enumerate(_other_chips(x, y)[:2]):
            _remote(mine, mine, sems, k, (*xy, c)).start()

    def wait(self, refs, sems):
        ref = refs[id(self.buf)]
        x, y, c = _position()
        mine = self._rows(ref, 2 * x + y, c)
        for k, (slot, xy) in enumerate(_other_chips(x, y)[:2]):
            landed = self._rows(ref, slot, c)
            _remote(landed, landed, sems, k, (*xy, c)).wait_recv()
            _remote(mine, mine, sems, k, (*xy, c)).wait_send()


class _GatherForward(_GatherIci):
    def _ways(self, x, y):
        (slot_x, xy_x), (slot_y, xy_y), (slot_d, _) = _other_chips(x, y)
        h = self.n // 2
        assert h % 16 == 0, self.n
        return [(slot_x, slot_d, self.lo, xy_y), (slot_y, slot_d, self.lo + h, xy_x)], h

    def start(self, refs, sems):
        ref = refs[id(self.buf)]
        x, y, c = _position()
        ways, h = self._ways(x, y)
        for k, (slot, _, lo, xy) in enumerate(ways):
            rows = self._rows(ref, slot, c, lo, h)
            _remote(rows, rows, sems, k, (*xy, c)).start()

    def wait(self, refs, sems):
        ref = refs[id(self.buf)]
        x, y, c = _position()
        ways, h = self._ways(x, y)
        for k, (slot, lands, lo, xy) in enumerate(ways):
            landed, sent = self._rows(ref, lands, c, lo, h), self._rows(ref, slot, c, lo, h)
            _remote(landed, landed, sems, k, (*xy, c)).wait_recv()
            _remote(sent, sent, sems, k, (*xy, c)).wait_send()


class _GatherD2d(_GatherIci):
    link = "d2d"

    def __init__(self, buf, lo, n, slots):
        super().__init__(buf, lo, n)
        self.slots, self.n_sems = slots, len(slots)

    def start(self, refs, sems):
        ref = refs[id(self.buf)]
        x, y, c = _position()
        chips = _other_chips(x, y)
        for k, which in enumerate(self.slots):
            rows = self._rows(ref, chips[which][0], c)
            _remote(rows, rows, sems, k, (x, y, 1 - c)).start()

    def wait(self, refs, sems):
        ref = refs[id(self.buf)]
        x, y, c = _position()
        chips = _other_chips(x, y)
        for k, which in enumerate(self.slots):
            sent, landed = self._rows(ref, chips[which][0], c), self._rows(ref, chips[which][0], 1 - c)
            _remote(landed, landed, sems, k, (x, y, 1 - c)).wait_recv()
            _remote(sent, sent, sems, k, (x, y, 1 - c)).wait_send()


class _ScatterDiag:
    n_sems = 2
    link = "ici"

    def __init__(self, src, via, lo, n):
        self.src, self.via, self.lo, self.n, self.buffers, self.sources = src, via, lo, n, [via], [src]

    def _copies(self, refs, sems):
        src, via = refs[id(self.src)], refs[id(self.via)]
        x, y, c = _position()
        (_, xy_x), (_, xy_y), (slot_d, _) = _other_chips(x, y)
        h2 = via.shape[0] // 2
        return [_remote(src.at[slot_d, pl.ds(first + self.lo, self.n), :], via.at[pl.ds(first + self.lo, self.n), :],
                        sems, k, (*xy, c)) for k, (first, xy) in enumerate(((0, xy_x), (h2, xy_y)))]

    def start(self, refs, sems):
        for cp in self._copies(refs, sems):
            cp.start()

    def wait(self, refs, sems):
        for cp in self._copies(refs, sems):
            cp.wait_recv()
            cp.wait_send()


class _ScatterNear:
    n_sems = 2
    link = "ici"

    def __init__(self, src, dst, lo, n):
        self.src, self.dst, self.lo, self.n, self.buffers, self.sources = src, dst, lo, n, [dst], [src]

    def _copies(self, refs, sems, landing):
        src, dst = refs[id(self.src)], refs[id(self.dst)]
        x, y, c = _position()
        rows = pl.ds(self.lo, self.n)
        return [_remote(src.at[k, rows, :], dst.at[slot if landing else 2 * x + y, rows, :], sems, k, (*xy, c))
                for k, (slot, xy) in enumerate(_other_chips(x, y)[:2])]

    def start(self, refs, sems):
        for cp in self._copies(refs, sems, False):
            cp.start()

    def wait(self, refs, sems):
        for cp in self._copies(refs, sems, True):
            cp.wait_recv()
            cp.wait_send()


class _PairExchange:
    n_sems = 1
    link = "d2d"

    def __init__(self, src, dst):
        self.src, self.dst, self.buffers, self.sources = src, dst, [dst], [src]

    def _copy(self, refs, sems):
        src, dst = refs[id(self.src)], refs[id(self.dst)]
        x, y, c = _position()
        h = src.shape[1] // 2
        return _remote(src.at[:, pl.ds((1 - c) * h, h), :], dst, sems, 0, (x, y, 1 - c))

    def start(self, refs, sems):
        self._copy(refs, sems).start()

    def wait(self, refs, sems):
        self._copy(refs, sems).wait()


class _PairShare:
    n_sems = 1
    link = "d2d"

    def __init__(self, buf):
        self.buf, self.buffers, self.sources = buf, [buf], []

    def _rows(self, ref, which):
        h = ref.shape[0] // 2
        return ref.at[pl.ds(which * h, h), :]

    def start(self, refs, sems):
        ref = refs[id(self.buf)]
        x, y, c = _position()
        _remote(self._rows(ref, c), self._rows(ref, c), sems, 0, (x, y, 1 - c)).start()

    def wait(self, refs, sems):
        ref = refs[id(self.buf)]
        x, y, c = _position()
        _remote(self._rows(ref, c), self._rows(ref, c), sems, 0, (x, y, 1 - c)).wait_send()
        _remote(self._rows(ref, 1 - c), self._rows(ref, 1 - c), sems, 0, (x, y, 1 - c)).wait_recv()


_COLLECTIVE_IDS = {("ici",): 1, ("d2d",): 2, ("d2d", "ici"): 3}


def _links(jobs):
    return tuple(sorted({j.link for j in jobs}))


def _handshake(links):
    x, y, c = _position()
    peers = ([(1 - x, y, c), (x, 1 - y, c)] if "ici" in links else []) + ([(x, y, 1 - c)] if "d2d" in links else [])
    barrier = pltpu.get_barrier_semaphore()
    for peer in peers:
        pl.semaphore_signal(barrier, inc=1, device_id=peer, device_id_type=MESH)
    pl.semaphore_wait(barrier, len(peers))


def _unique(items):
    seen, out = set(), []
    for it in items:
        if id(it) not in seen:
            seen.add(id(it))
            out.append(it)
    return out


def _job_operands(jobs):
    sources = _unique([a for j in jobs for a in j.sources])
    buffers = _unique([b for j in jobs for b in j.buffers])
    held = [b for b in buffers if not isinstance(b.arr, jax.ShapeDtypeStruct)]
    fresh = [b for b in buffers if isinstance(b.arr, jax.ShapeDtypeStruct)]
    return sources, held, fresh


def _sem_offsets(jobs):
    offs, total = [], 0
    for j in jobs:
        offs.append(total)
        total += j.n_sems
    return offs, total


def _call(body, *, name, grid, in_specs, out_specs, out_shape, args, semantics, scratch_shapes=(), jobs=(),
          prefetch=None):
    n_pre = 0 if prefetch is None else 1

    def launch(fn, in_specs, out_specs, scratch, **kw):
        if prefetch is None:
            return pl.pallas_call(fn, name=name, grid=grid, in_specs=in_specs, out_specs=out_specs,
                                  scratch_shapes=scratch, **kw)
        return pl.pallas_call(fn, name=name, grid_spec=pltpu.PrefetchScalarGridSpec(
            num_scalar_prefetch=1, grid=grid, in_specs=in_specs, out_specs=out_specs, scratch_shapes=scratch), **kw)

    pre = () if prefetch is None else (prefetch,)
    if not jobs:
        return launch(body, list(in_specs), list(out_specs), list(scratch_shapes), out_shape=list(out_shape),
                      compiler_params=_params(semantics))(*pre, *args)
    sources, held, fresh = _job_operands(jobs)
    offs, n_sem = _sem_offsets(jobs)
    links = _links(jobs)
    n_in, n_out, n_scr = len(in_specs), len(out_specs), len(scratch_shapes)
    n_src, n_held, n_fresh = len(sources), len(held), len(fresh)

    def carried(*refs):
        pre_refs, refs = refs[:n_pre], refs[n_pre:]
        ins = refs[:n_in]
        src_refs = refs[n_in:n_in + n_src]
        o0 = n_in + n_src + n_held
        outs = refs[o0:o0 + n_out]
        buf_refs = refs[o0 + n_out:o0 + n_out + n_held + n_fresh]
        s0 = o0 + n_out + n_held + n_fresh
        scratch = refs[s0:s0 + n_scr]
        send, recv = refs[s0 + n_scr], refs[s0 + n_scr + 1]
        where = {id(a): r for a, r in zip(sources, src_refs)}
        where.update({id(b): r for b, r in zip(held + fresh, buf_refs)})
        ids = [pl.program_id(a) for a in range(len(grid))]
        first = functools.reduce(jnp.logical_and, [i == 0 for i in ids])
        last = functools.reduce(jnp.logical_and, [i == g - 1 for i, g in zip(ids, grid)])

        @pl.when(first)
        def _():
            _handshake(links)
            for j, off in zip(jobs, offs):
                j.start(where, (send, recv, off))

        body(*pre_refs, *ins, *outs, *scratch)

        @pl.when(last)
        def _():
            for j, off in zip(jobs, offs):
                j.wait(where, (send, recv, off))

    shape_of = lambda b: jax.ShapeDtypeStruct(b.arr.shape, b.arr.dtype)
    res = launch(
        carried, [*in_specs, *[ANY] * (n_src + n_held)], [*out_specs, *[ANY] * (n_held + n_fresh)],
        [*scratch_shapes, pltpu.SemaphoreType.DMA((n_sem,)), pltpu.SemaphoreType.DMA((n_sem,))],
        out_shape=[*out_shape, *[shape_of(b) for b in held + fresh]],
        input_output_aliases={n_pre + n_in + n_src + i: n_out + i for i in range(n_held)},
        compiler_params=pltpu.CompilerParams(dimension_semantics=("arbitrary",) * len(grid),
                                             vmem_limit_bytes=VMEM_LIMIT_BYTES, has_side_effects=True,
                                             collective_id=_COLLECTIVE_IDS[links]),
    )(*pre, *args, *sources, *[b.arr for b in held])
    for b, new in zip(held + fresh, res[n_out:]):
        b.arr = new
    return list(res[:n_out])


def _comm(name, phases):
    jobs = [j for ph in phases for j in ph]
    sources, held, fresh = _job_operands(jobs)
    offs, n_sem = _sem_offsets(jobs)
    off_of = {id(j): o for j, o in zip(jobs, offs)}
    links = _links(jobs)
    n_src, n_held, n_fresh = len(sources), len(held), len(fresh)

    def body(*refs):
        src_refs = refs[:n_src]
        buf_refs = refs[n_src + n_held:n_src + 2 * n_held + n_fresh]
        send, recv = refs[-2], refs[-1]
        where = {id(a): r for a, r in zip(sources, src_refs)}
        where.update({id(b): r for b, r in zip(held + fresh, buf_refs)})
        _handshake(links)
        for ph in phases:
            for j in ph:
                j.start(where, (send, recv, off_of[id(j)]))
            for j in ph:
                j.wait(where, (send, recv, off_of[id(j)]))

    shape_of = lambda b: jax.ShapeDtypeStruct(b.arr.shape, b.arr.dtype)
    res = pl.pallas_call(
        body, name=name,
        in_specs=[ANY] * (n_src + n_held), out_specs=[ANY] * (n_held + n_fresh),
        out_shape=[shape_of(b) for b in held + fresh],
        input_output_aliases={n_src + i: i for i in range(n_held)},
        scratch_shapes=[pltpu.SemaphoreType.DMA((n_sem,)), pltpu.SemaphoreType.DMA((n_sem,))],
        compiler_params=pltpu.CompilerParams(has_side_effects=True, collective_id=_COLLECTIVE_IDS[links]),
    )(*sources, *[b.arr for b in held])
    for b, new in zip(held + fresh, res):
        b.arr = new


def _matmul(a, b, *, dims, mnk, tiles, name, out_dtype=jnp.float32, a_spec=None, b_spec=None,
            out_shape=None, o_spec=None, epilogue=None, extra=(), extra_specs=(), jobs=()):
    m, n, k = mnk
    tm, tn, tk = tiles
    assert m % tm == 0 and n % tn == 0 and k % tk == 0, (name, mnk, tiles)
    gm, gn, gk = m // tm, n // tn, k // tk
    if a_spec is None:
        a_spec = (pl.BlockSpec((tk, tm), lambda i, j, l: (l, i)) if dims is _TN
                  else pl.BlockSpec((tm, tk), lambda i, j, l: (i, l)))
    if b_spec is None:
        b_spec = (pl.BlockSpec((tn, tk), lambda i, j, l: (j, l)) if dims is _NT
                  else pl.BlockSpec((tk, tn), lambda i, j, l: (l, j)))
    if out_shape is None:
        out_shape = jax.ShapeDtypeStruct((m, n), out_dtype)
    if o_spec is None:
        o_spec = pl.BlockSpec((tm, tn), lambda i, j, l: (i, j))
    single = not isinstance(out_shape, (tuple, list))
    n_extra = len(extra)

    def finish(acc, extra_refs, out_refs):
        if epilogue is None:
            out_refs[0][...] = acc.astype(out_refs[0].dtype)
        else:
            epilogue(acc, extra_refs, out_refs)

    def body_whole_k(*refs):
        a_ref, b_ref = refs[0], refs[1]
        acc = lax.dot_general(a_ref[...], b_ref[...], dims, preferred_element_type=jnp.float32)
        finish(acc, refs[2:2 + n_extra], refs[2 + n_extra:])

    def body_split_k(*refs):
        a_ref, b_ref = refs[0], refs[1]
        extra_refs = refs[2:2 + n_extra]
        out_refs = refs[2 + n_extra:-1]
        acc_ref = refs[-1]
        step = pl.program_id(2)
        part = lax.dot_general(a_ref[...], b_ref[...], dims, preferred_element_type=jnp.float32)

        @pl.when(step == 0)
        def _():
            acc_ref[...] = part

        @pl.when(jnp.logical_and(step > 0, step < gk - 1))
        def _():
            acc_ref[...] += part

        @pl.when(step == gk - 1)
        def _():
            finish(acc_ref[...] + part, extra_refs, out_refs)

    res = _call(
        body_whole_k if gk == 1 else body_split_k, name=name, grid=(gm, gn, gk),
        in_specs=[a_spec, b_spec, *extra_specs],
        out_specs=[o_spec] if single else list(o_spec),
        out_shape=[out_shape] if single else list(out_shape),
        scratch_shapes=[] if gk == 1 else [pltpu.VMEM((tm, tn), jnp.float32)],
        semantics=("parallel", "parallel", "arbitrary"), args=(a, b, *extra), jobs=jobs)
    return res[0] if single else res


def _rstd(x):
    return lax.rsqrt(jnp.mean(x * x, axis=-1, keepdims=True) + EPS)


def _rms_bwd_rows(x, g, dy):
    r = _rstd(x)
    xn = x * r
    dyg = dy * g
    dx = r * (dyg - xn * jnp.mean(xn * dyg, axis=-1, keepdims=True))
    return dx, dy * xn


def _acc_rows(ref, rows, first):
    part = jnp.sum(rows, axis=0, keepdims=True)

    @pl.when(first)
    def _():
        ref[...] = part

    @pl.when(jnp.logical_not(first))
    def _():
        ref[...] += part


def _rms_fwd(x, g, *, width, col, tm, name, jobs=()):
    s = x.shape[0]

    def body(x_ref, g_ref, o_ref):
        v = x_ref[...]
        o_ref[...] = (v * _rstd(v) * g_ref[...]).astype(o_ref.dtype)

    return _call(
        body, name=name, grid=(s // tm,),
        in_specs=[pl.BlockSpec((tm, width), lambda i: (i, col)), pl.BlockSpec((1, width), lambda i: (0, 0))],
        out_specs=[pl.BlockSpec((tm, width), lambda i: (i, 0))],
        out_shape=[jax.ShapeDtypeStruct((s, width), jnp.bfloat16)],
        semantics=("parallel",), args=(x, g), jobs=jobs)[0]


def _rms_bwd(x, g, dy, *, width, col, tm, name, jobs=()):
    s = x.shape[0]

    def body(x_ref, g_ref, dy_ref, dx_ref, dg_ref):
        dx, dgr = _rms_bwd_rows(x_ref[...], g_ref[...], dy_ref[...])
        dx_ref[...] = dx.astype(dx_ref.dtype)
        _acc_rows(dg_ref, dgr, pl.program_id(0) == 0)

    return _call(
        body, name=name, grid=(s // tm,),
        in_specs=[pl.BlockSpec((tm, width), lambda i: (i, col)), pl.BlockSpec((1, width), lambda i: (0, 0)),
                  pl.BlockSpec((tm, width), lambda i: (i, 0))],
        out_specs=[pl.BlockSpec((tm, width), lambda i: (i, 0)), pl.BlockSpec((1, width), lambda i: (0, 0))],
        out_shape=[jax.ShapeDtypeStruct((s, width), jnp.bfloat16), jax.ShapeDtypeStruct((1, width), jnp.float32)],
        semantics=("arbitrary",), args=(x, g, dy), jobs=jobs)


def _row_specs(tm, d, n):
    return [pl.BlockSpec((tm, d), lambda i: (i, 0)) for _ in range(n)]


def _gain_specs(d, n):
    return [pl.BlockSpec((1, d), lambda i: (0, 0)) for _ in range(n)]


def _mix_residual_fwd(x, mix, g_post_mix, g_pre_mlp, *, tm, jobs=()):
    s, d = x.shape

    def body(x_ref, mix_ref, g1_ref, g2_ref, x2_ref, h2_ref):
        mixv = mix_ref[...]
        x2 = x_ref[...] + mixv * _rstd(mixv) * g1_ref[...]
        x2_ref[...] = x2
        h2_ref[...] = (x2 * _rstd(x2) * g2_ref[...]).astype(h2_ref.dtype)

    return _call(
        body, name="mix_residual_fwd", grid=(s // tm,),
        in_specs=_row_specs(tm, d, 2) + _gain_specs(d, 2),
        out_specs=_row_specs(tm, d, 2),
        out_shape=[jax.ShapeDtypeStruct((s, d), jnp.float32), jax.ShapeDtypeStruct((s, d), jnp.bfloat16)],
        semantics=("parallel",), args=(x, mix, g_post_mix, g_pre_mlp), jobs=jobs)


def _loss_head(x2, mlp, target, g_post_mlp, *, tm, jobs=()):
    s, d = x2.shape

    def body(x2_ref, m_ref, t_ref, g_ref, dx3_ref, dm_ref, dg_ref, loss_ref):
        first = pl.program_id(0) == 0
        mv = m_ref[...]
        g = g_ref[...]
        diff = x2_ref[...] + mv * _rstd(mv) * g - t_ref[...]
        dx3 = diff * (1.0 / d)
        dx3_ref[...] = dx3
        dm, dgr = _rms_bwd_rows(mv, g, dx3)
        dm_ref[...] = dm.astype(dm_ref.dtype)
        _acc_rows(dg_ref, dgr, first)
        part = 0.5 * jnp.sum(jnp.mean(diff * diff, axis=-1, keepdims=True), axis=0, keepdims=True)
        _acc_rows(loss_ref, jnp.broadcast_to(part, (1, 128)), first)

    return _call(
        body, name="loss_head", grid=(s // tm,),
        in_specs=_row_specs(tm, d, 3) + _gain_specs(d, 1),
        out_specs=_row_specs(tm, d, 2) + _gain_specs(d, 1) + [pl.BlockSpec((1, 128), lambda i: (0, 0))],
        out_shape=[jax.ShapeDtypeStruct((s, d), jnp.float32), jax.ShapeDtypeStruct((s, d), jnp.bfloat16),
                   jax.ShapeDtypeStruct((1, d), jnp.float32), jax.ShapeDtypeStruct((1, 128), jnp.float32)],
        semantics=("arbitrary",), args=(x2, mlp, target, g_post_mlp), jobs=jobs)


def _mix_residual_bwd(dx3, dh2, x2, mix, g_pre_mlp, g_post_mix, *, tm, jobs=()):
    s, d = x2.shape

    def body(dx3_ref, dh2_ref, x2_ref, mix_ref, g2_ref, g1_ref, dx2_ref, dmix_ref, dg2_ref, dg1_ref):
        first = pl.program_id(0) == 0
        d_in, dgr2 = _rms_bwd_rows(x2_ref[...], g2_ref[...], dh2_ref[...])
        dx2 = dx3_ref[...] + d_in
        dx2_ref[...] = dx2
        dmix, dgr1 = _rms_bwd_rows(mix_ref[...], g1_ref[...], dx2)
        dmix_ref[...] = dmix.astype(dmix_ref.dtype)
        _acc_rows(dg2_ref, dgr2, first)
        _acc_rows(dg1_ref, dgr1, first)

    return _call(
        body, name="mix_residual_bwd", grid=(s // tm,),
        in_specs=_row_specs(tm, d, 4) + _gain_specs(d, 2),
        out_specs=_row_specs(tm, d, 2) + _gain_specs(d, 2),
        out_shape=[jax.ShapeDtypeStruct((s, d), jnp.float32), jax.ShapeDtypeStruct((s, d), jnp.bfloat16),
                   jax.ShapeDtypeStruct((1, d), jnp.float32), jax.ShapeDtypeStruct((1, d), jnp.float32)],
        semantics=("arbitrary",), args=(dx3, dh2, x2, mix, g_pre_mlp, g_post_mix), jobs=jobs)


def _input_bwd(dx2, dh1, x, g_pre_mix, *, tm, jobs=()):
    s, d = x.shape

    def body(dx2_ref, dh1_ref, x_ref, g_ref, dx_ref, dg_ref):
        d_in, dgr = _rms_bwd_rows(x_ref[...], g_ref[...], dh1_ref[...])
        dx_ref[...] = dx2_ref[...] + d_in
        _acc_rows(dg_ref, dgr, pl.program_id(0) == 0)

    return _call(
        body, name="input_bwd", grid=(s // tm,),
        in_specs=_row_specs(tm, d, 3) + _gain_specs(d, 1),
        out_specs=_row_specs(tm, d, 1) + _gain_specs(d, 1),
        out_shape=[jax.ShapeDtypeStruct((s, d), jnp.float32), jax.ShapeDtypeStruct((1, d), jnp.float32)],
        semantics=("arbitrary",), args=(dx2, dh1, x, g_pre_mix), jobs=jobs)


def _shift_rows(z, by):
    s = z.shape[0]
    rows = lax.broadcasted_iota(jnp.int32, z.shape, 0)
    rolled = pltpu.roll(z, by % s, axis=0)
    keep = rows >= by if by > 0 else rows < s + by
    return jnp.where(keep, rolled, 0.0)


def _conv_fwd(proj, conv_w, conv_out_g, jobs=()):
    s = proj.shape[0]
    groups = CONV_WIDTH // 128

    def body(u_ref, gb_ref, gc_ref, w_ref, g_ref, y_ref):
        z = gc_ref[...] * u_ref[...]
        w = w_ref[...]
        conv = w[0:1, :] * _shift_rows(z, 2) + w[1:2, :] * _shift_rows(z, 1) + w[2:3, :] * z
        y = gb_ref[...] * conv
        y_ref[...] = (y * _rstd(y) * g_ref[...]).astype(y_ref.dtype)

    col = lambda base: pl.BlockSpec((s, 128), lambda j: (0, base + j))
    return _call(
        body, name="conv_fwd", grid=(groups,),
        in_specs=[col(0), col(groups), col(2 * groups), pl.BlockSpec((3, 128), lambda j: (0, j)),
                  pl.BlockSpec((1, 128), lambda j: (0, j))],
        out_specs=[pl.BlockSpec((s, 128), lambda j: (0, j))],
        out_shape=[jax.ShapeDtypeStruct((s, CONV_WIDTH), jnp.bfloat16)],
        semantics=("parallel",), args=(proj, proj, proj, conv_w, conv_out_g), jobs=jobs)[0]


def _conv_bwd(proj, conv_w, conv_out_g, dycat, jobs=()):
    s = proj.shape[0]
    groups = CONV_WIDTH // 128

    def body(u_ref, gb_ref, gc_ref, w_ref, g_ref, dy_ref, du_ref, dgb_ref, dgc_ref, dw_ref, dg_ref):
        u, gb, gc = u_ref[...], gb_ref[...], gc_ref[...]
        w = w_ref[...]
        z = gc * u
        z1, z2 = _shift_rows(z, 1), _shift_rows(z, 2)
        conv = w[0:1, :] * z2 + w[1:2, :] * z1 + w[2:3, :] * z
        dyr, dgr = _rms_bwd_rows(gb * conv, g_ref[...], dy_ref[...])
        dg_ref[...] = jnp.sum(dgr, axis=0, keepdims=True)
        dgb_ref[...] = (dyr * conv).astype(dgb_ref.dtype)
        dconv = dyr * gb
        dw_ref[0:1, :] = jnp.sum(dconv * z2, axis=0, keepdims=True)
        dw_ref[1:2, :] = jnp.sum(dconv * z1, axis=0, keepdims=True)
        dw_ref[2:3, :] = jnp.sum(dconv * z, axis=0, keepdims=True)
        dz = w[2:3, :] * dconv + w[1:2, :] * _shift_rows(dconv, -1) + w[0:1, :] * _shift_rows(dconv, -2)
        dgc_ref[...] = (dz * u).astype(dgc_ref.dtype)
        du_ref[...] = (dz * gc).astype(du_ref.dtype)

    col = lambda base: pl.BlockSpec((s, 128), lambda j: (0, base + j))
    act = jax.ShapeDtypeStruct((s, CONV_WIDTH), jnp.bfloat16)
    return _call(
        body, name="conv_bwd", grid=(groups,),
        in_specs=[col(0), col(groups), col(2 * groups), pl.BlockSpec((3, 128), lambda j: (0, j)),
                  pl.BlockSpec((1, 128), lambda j: (0, j)), col(0)],
        out_specs=[col(0), col(0), col(0), pl.BlockSpec((3, 128), lambda j: (0, j)),
                   pl.BlockSpec((1, 128), lambda j: (0, j))],
        out_shape=[act, act, act, jax.ShapeDtypeStruct((3, CONV_WIDTH), jnp.float32),
                   jax.ShapeDtypeStruct((1, CONV_WIDTH), jnp.float32)],
        semantics=("parallel",), args=(proj, proj, proj, conv_w, conv_out_g, dycat), jobs=jobs)


def _rope_tables(s):
    pos = jnp.arange(s, dtype=jnp.float32)
    inv_freq = jnp.power(ROPE_THETA, -jnp.arange(0, QK_ROPE, 2, dtype=jnp.float32) / QK_ROPE)
    ang = pos[:, None] * inv_freq[None, :]
    cos, sin = jnp.cos(ang), jnp.sin(ang)
    zeros = jnp.zeros((s, 128 - QK_ROPE), jnp.float32)
    return (jnp.concatenate([cos, cos, zeros], axis=1), jnp.concatenate([-sin, sin, zeros], axis=1))


def _swap_halves(x):
    lane = lax.broadcasted_iota(jnp.int32, x.shape, 1)
    swapped = jnp.where(lane < QK_ROPE // 2, pltpu.roll(x, 128 - QK_ROPE // 2, axis=1),
                        pltpu.roll(x, QK_ROPE // 2, axis=1))
    return jnp.where(lane < QK_ROPE, swapped, 0.0)


def _rope(x, cos, sin):
    return x * cos + _swap_halves(x) * sin


def _rope_transposed(d, cos, sin):
    return d * cos + _swap_halves(d * sin)


def _qk_rope_fwd(q_pad, proj, cos, sin, *, tm, jobs=()):
    s = q_pad.shape[0]
    wq = N_HEADS * HEAD_PAD

    def body(q_ref, kr_ref, cos_ref, sin_ref, qo_ref, kro_ref):
        c, sn = cos_ref[...], sin_ref[...]
        for h in range(N_HEADS):
            lo = h * HEAD_PAD
            qo_ref[:, lo:lo + 128] = q_ref[:, lo:lo + 128].astype(qo_ref.dtype)
            qo_ref[:, lo + 128:lo + 256] = _rope(q_ref[:, lo + 128:lo + 256], c, sn).astype(qo_ref.dtype)
        kro_ref[...] = _rope(kr_ref[...], c, sn).astype(kro_ref.dtype)

    return _call(
        body, name="qk_rope_fwd", grid=(s // tm,),
        in_specs=[pl.BlockSpec((tm, wq), lambda i: (i, 0)), pl.BlockSpec((tm, 128), lambda i: (i, COL_KR // 128)),
                  pl.BlockSpec((tm, 128), lambda i: (i, 0)), pl.BlockSpec((tm, 128), lambda i: (i, 0))],
        out_specs=[pl.BlockSpec((tm, wq), lambda i: (i, 0)), pl.BlockSpec((tm, 128), lambda i: (i, 0))],
        out_shape=[jax.ShapeDtypeStruct((s, wq), jnp.bfloat16), jax.ShapeDtypeStruct((s, 128), jnp.bfloat16)],
        semantics=("parallel",), args=(q_pad, proj, cos, sin), jobs=jobs)


def _qk_rope_bwd(dq_pad, dk_pad, dv, cos, sin, *, tm, jobs=()):
    s = dq_pad.shape[0]
    wq = N_HEADS * HEAD_PAD

    def body(dq_ref, dk_ref, dv_ref, cos_ref, sin_ref, dqo_ref, dkv_ref, dkr_ref):
        c, sn = cos_ref[...], sin_ref[...]
        dkr = jnp.zeros((tm, 128), jnp.float32)
        for h in range(N_HEADS):
            lo = h * HEAD_PAD
            dqo_ref[:, lo:lo + 128] = dq_ref[:, lo:lo + 128].astype(dqo_ref.dtype)
            dqo_ref[:, lo + 128:lo + 256] = _rope_transposed(dq_ref[:, lo + 128:lo + 256], c, sn).astype(dqo_ref.dtype)
            dkv_ref[:, lo:lo + 128] = dk_ref[:, lo:lo + 128].astype(dkv_ref.dtype)
            dkv_ref[:, lo + 128:lo + 256] = dv_ref[:, h * V_DIM:(h + 1) * V_DIM].astype(dkv_ref.dtype)
            dkr = dkr + dk_ref[:, lo + 128:lo + 256]
        dkr_ref[...] = _rope_transposed(dkr, c, sn).astype(dkr_ref.dtype)

    return _call(
        body, name="qk_rope_bwd", grid=(s // tm,),
        in_specs=[pl.BlockSpec((tm, wq), lambda i: (i, 0)), pl.BlockSpec((tm, wq), lambda i: (i, 0)),
                  pl.BlockSpec((tm, N_HEADS * V_DIM), lambda i: (i, 0)),
                  pl.BlockSpec((tm, 128), lambda i: (i, 0)), pl.BlockSpec((tm, 128), lambda i: (i, 0))],
        out_specs=[pl.BlockSpec((tm, wq), lambda i: (i, 0)), pl.BlockSpec((tm, wq), lambda i: (i, 0)),
                   pl.BlockSpec((tm, 128), lambda i: (i, 0))],
        out_shape=[jax.ShapeDtypeStruct((s, wq), jnp.bfloat16), jax.ShapeDtypeStruct((s, wq), jnp.bfloat16),
                   jax.ShapeDtypeStruct((s, 128), jnp.bfloat16)],
        semantics=("parallel",), args=(dq_pad, dk_pad, dv, cos, sin), jobs=jobs)


def _visible(q0, k0, t):
    qpos = q0 + lax.broadcasted_iota(jnp.int32, (t, t), 0)
    kpos = k0 + lax.broadcasted_iota(jnp.int32, (t, t), 1)
    return lax.shift_right_logical(kpos, CHUNK_SHIFT) <= lax.shift_right_logical(qpos, CHUNK_SHIFT)


def _attn_fwd(q, kv, kr, attn_out_g, *, t, jobs=()):
    s = q.shape[0]
    nq = s // t

    def body(q_ref, kn_ref, v_ref, kr_ref, g_ref, o_ref, lse_ref, y_ref, kcat_ref):
        i = pl.program_id(1)

        @pl.when(i == 0)
        def _():
            kcat_ref[:, 0:128] = kn_ref[...]
            kcat_ref[:, 128:256] = kr_ref[...]

        qv = q_ref[...]

        def step(j, carry, diagonal):
            m, l, acc = carry
            k = kcat_ref[pl.ds(pl.multiple_of(j * t, t), t), :]
            v = v_ref[pl.ds(pl.multiple_of(j * t, t), t), :]
            sc = lax.dot_general(qv, k, _NT, preferred_element_type=jnp.float32) * ATTN_SCALE
            if diagonal:
                sc = jnp.where(_visible(0, 0, t), sc, NEG_INF)
            m_new = jnp.maximum(m, jnp.max(sc, axis=-1, keepdims=True))
            p = jnp.exp(sc - m_new)
            alpha = jnp.exp(m - m_new)
            l = alpha * l + jnp.sum(p, axis=-1, keepdims=True)
            acc = alpha * acc + lax.dot_general(p.astype(jnp.bfloat16), v, _NN, preferred_element_type=jnp.float32)
            return m_new, l, acc

        init = (jnp.full((t, 1), NEG_INF, jnp.float32), jnp.zeros((t, 1), jnp.float32),
                jnp.zeros((t, V_DIM), jnp.float32))
        before = lax.fori_loop(0, i, functools.partial(step, diagonal=False), init)
        m, l, acc = step(i, before, True)
        o = acc / l
        o_ref[...] = o
        lse_ref[...] = jnp.broadcast_to(m + jnp.log(l), (t, 128))
        y_ref[...] = (o * _rstd(o) * g_ref[...]).astype(y_ref.dtype)

    head_rows = lambda w, f: pl.BlockSpec((s, w), lambda h, i: (0, f(h)))
    blk = pl.BlockSpec((t, 128), lambda h, i: (i, h))
    full = jax.ShapeDtypeStruct((s, N_HEADS * V_DIM), jnp.float32)
    return _call(
        body, name="attn_fwd", grid=(N_HEADS, nq),
        in_specs=[pl.BlockSpec((t, HEAD_PAD), lambda h, i: (i, h)), head_rows(128, lambda h: 2 * h),
                  head_rows(128, lambda h: 2 * h + 1), head_rows(128, lambda h: 0),
                  pl.BlockSpec((1, 128), lambda h, i: (0, h))],
        out_specs=[blk, blk, blk],
        out_shape=[full, full, jax.ShapeDtypeStruct((s, N_HEADS * V_DIM), jnp.bfloat16)],
        scratch_shapes=[pltpu.VMEM((s, HEAD_PAD), jnp.bfloat16)],
        semantics=("arbitrary", "arbitrary"), args=(q, kv, kv, kr, attn_out_g), jobs=jobs)


def _attn_norm_bwd(o, attn_out_g, dycat, jobs=()):
    s = o.shape[0]

    def body(o_ref, g_ref, dy_ref, do_ref, delta_ref, dg_ref):
        ov = o_ref[...]
        do, dgr = _rms_bwd_rows(ov, g_ref[...], dy_ref[...])
        do_ref[...] = do.astype(do_ref.dtype)
        delta_ref[...] = jnp.broadcast_to(jnp.sum(do * ov, axis=-1, keepdims=True), (s, 128))
        dg_ref[...] = jnp.sum(dgr, axis=0, keepdims=True)

    col = lambda base: pl.BlockSpec((s, 128), lambda h: (0, base + h))
    return _call(
        body, name="attn_norm_bwd", grid=(N_HEADS,),
        in_specs=[col(0), pl.BlockSpec((1, 128), lambda h: (0, h)), col(CONV_WIDTH // 128)],
        out_specs=[col(0), col(0), pl.BlockSpec((1, 128), lambda h: (0, h))],
        out_shape=[jax.ShapeDtypeStruct((s, N_HEADS * V_DIM), jnp.bfloat16),
                   jax.ShapeDtypeStruct((s, N_HEADS * V_DIM), jnp.float32),
                   jax.ShapeDtypeStruct((1, N_HEADS * V_DIM), jnp.float32)],
        semantics=("parallel",), args=(o, attn_out_g, dycat), jobs=jobs)


def _attn_bwd(q, kv, kr, do, lse, delta, *, t, jobs=()):
    s = q.shape[0]
    nq = s // t

    def body(q_ref, kn_ref, v_ref, kr_ref, do_ref, lse_ref, delta_ref, dq_ref, dk_ref, dv_ref, kcat_ref):
        kcat_ref[:, 0:128] = kn_ref[...]
        kcat_ref[:, 128:256] = kr_ref[...]
        dq_ref[...] = jnp.zeros_like(dq_ref)
        dk_ref[...] = jnp.zeros_like(dk_ref)
        dv_ref[...] = jnp.zeros_like(dv_ref)

        def kv_step(j, _):
            krows = pl.ds(pl.multiple_of(j * t, t), t)
            k = kcat_ref[krows, :]
            v = v_ref[krows, :]

            def q_step(i, _, diagonal):
                qrows = pl.ds(pl.multiple_of(i * t, t), t)
                qv = q_ref[qrows, :]
                dov = do_ref[qrows, :]
                sc = lax.dot_general(qv, k, _NT, preferred_element_type=jnp.float32) * ATTN_SCALE
                if diagonal:
                    sc = jnp.where(_visible(0, 0, t), sc, NEG_INF)
                p = jnp.exp(sc - lse_ref[qrows, :][:, 0:1])
                dp = lax.dot_general(dov, v, _NT, preferred_element_type=jnp.float32)
                ds = (p * (dp - delta_ref[qrows, :][:, 0:1]) * ATTN_SCALE).astype(jnp.bfloat16)
                dv_ref[krows, :] += lax.dot_general(p.astype(jnp.bfloat16), dov, _TN,
                                                    preferred_element_type=jnp.float32)
                dk_ref[krows, :] += lax.dot_general(ds, qv, _TN, preferred_element_type=jnp.float32)
                dq_ref[qrows, :] += lax.dot_general(ds, k, _NN, preferred_element_type=jnp.float32)
                return 0

            q_step(j, 0, True)
            lax.fori_loop(j + 1, nq, functools.partial(q_step, diagonal=False), 0)
            return 0

        lax.fori_loop(0, nq, kv_step, 0)

    col = lambda w, f: pl.BlockSpec((s, w), lambda h: (0, f(h)))
    return _call(
        body, name="attn_bwd", grid=(N_HEADS,),
        in_specs=[col(HEAD_PAD, lambda h: h), col(128, lambda h: 2 * h), col(128, lambda h: 2 * h + 1),
                  col(128, lambda h: 0), col(128, lambda h: h), col(128, lambda h: h), col(128, lambda h: h)],
        out_specs=[col(HEAD_PAD, lambda h: h), col(HEAD_PAD, lambda h: h), col(128, lambda h: h)],
        out_shape=[jax.ShapeDtypeStruct((s, N_HEADS * HEAD_PAD), jnp.float32),
                   jax.ShapeDtypeStruct((s, N_HEADS * HEAD_PAD), jnp.float32),
                   jax.ShapeDtypeStruct((s, N_HEADS * V_DIM), jnp.float32)],
        scratch_shapes=[pltpu.VMEM((s, HEAD_PAD), jnp.bfloat16)],
        semantics=("parallel",), args=(q, kv, kv, kr, do, lse, delta), jobs=jobs)


def _row_tile(rows, cap=256):
    for cand in (512, 256, 128, 64, 32, 16, 8):
        if cand <= cap and rows % cand == 0:
            return cand
    return rows


def _cast_into_slot(w, pos, *, name, jobs=()):
    r, c = w.shape
    tr = _row_tile(r)

    def body(pos_ref, w_ref, o_ref):
        o_ref[...] = w_ref[...].astype(o_ref.dtype)

    return _call(
        body, name=name, grid=(r // tr,), prefetch=pos,
        in_specs=[pl.BlockSpec((tr, c), lambda i, p: (i, 0))],
        out_specs=[pl.BlockSpec((None, tr, c), lambda i, p: (p[1], i, 0))],
        out_shape=[jax.ShapeDtypeStruct((4, r, c), jnp.bfloat16)],
        semantics=("parallel",), args=(w,), jobs=jobs)[0]


def _cast_many_into_slots(ws, pos, *, name, jobs=()):
    steps = 8
    assert all(w.shape[0] % (16 * steps) == 0 for w in ws), [w.shape for w in ws]

    def body(pos_ref, *refs):
        for w_ref, o_ref in zip(refs[:len(ws)], refs[len(ws):]):
            o_ref[...] = w_ref[...].astype(o_ref.dtype)

    return _call(
        body, name=name, grid=(steps,), prefetch=pos,
        in_specs=[pl.BlockSpec((w.shape[0] // steps, w.shape[1]), lambda i, p: (i, 0)) for w in ws],
        out_specs=[pl.BlockSpec((None, w.shape[0] // steps, w.shape[1]), lambda i, p: (p[1], i, 0)) for w in ws],
        out_shape=[jax.ShapeDtypeStruct((4, *w.shape), jnp.bfloat16) for w in ws],
        semantics=("parallel",), args=tuple(ws), jobs=jobs)


def _pair_add(g, theirs, pos, *, name, jobs=()):
    n, h, c = theirs.shape
    tr = _row_tile(h, cap=512)
    nb = h // tr

    def body(pos_ref, a_ref, b_ref, o_ref):
        o_ref[...] = (a_ref[...].astype(jnp.float32) + b_ref[...].astype(jnp.float32)).astype(o_ref.dtype)

    spec = pl.BlockSpec((None, tr, c), lambda j, i, p: (j, i, 0))
    return _call(
        body, name=name, grid=(n, nb), prefetch=pos,
        in_specs=[pl.BlockSpec((None, tr, c), lambda j, i, p: (j, i + p[0] * nb, 0)), spec],
        out_specs=[spec], out_shape=[jax.ShapeDtypeStruct(theirs.shape, jnp.bfloat16)],
        semantics=("parallel", "parallel"), args=(g, theirs), jobs=jobs)[0]


def _fold_diag(pair_sum, via, pos, *, name):
    n, h, c = pair_sum.shape
    tr = _row_tile(h // 2, cap=512)
    nb = h // tr

    def body(pos_ref, p_ref, via_ref, o_ref):
        j, i = pl.program_id(0), pl.program_id(1)
        mine = p_ref[...].astype(jnp.float32)
        add = (j == 0) == (i >= nb // 2)
        o_ref[...] = jnp.where(add, mine + via_ref[...].astype(jnp.float32), mine).astype(o_ref.dtype)

    return _call(
        body, name=name, grid=(2, nb), prefetch=pos,
        in_specs=[pl.BlockSpec((None, tr, c), lambda j, i, p: (jnp.bitwise_xor(p[1], 2 - j), i, 0)),
                  pl.BlockSpec((tr, c), lambda j, i, p: (i, 0))],
        out_specs=[pl.BlockSpec((None, tr, c), lambda j, i, p: (j, i, 0))],
        out_shape=[jax.ShapeDtypeStruct((2, h, c), jnp.bfloat16)],
        semantics=("parallel", "parallel"), args=(pair_sum, via))[0]


def _chip_sum(by_source, pair_sum, pos, *, name):
    n, h, c = by_source.shape
    tr = _row_tile(h, cap=512)
    nb = h // tr

    def body(pos_ref, own_ref, px_ref, py_ref, o_ref):
        f = lambda ref: ref[...].astype(jnp.float32)
        o_ref[...] = (f(own_ref) + f(px_ref)) + f(py_ref)

    slot = lambda flip: pl.BlockSpec((None, tr, c), lambda i, p: (jnp.bitwise_xor(p[1], flip), i, 0))
    return pl.pallas_call(
        body, name=name, out_shape=jax.ShapeDtypeStruct((2 * h, c), jnp.float32),
        grid_spec=pltpu.PrefetchScalarGridSpec(
            num_scalar_prefetch=1, grid=(nb,),
            in_specs=[slot(0), slot(2), slot(1)],
            out_specs=pl.BlockSpec((tr, c), lambda i, p: (i + p[0] * nb, 0))),
        compiler_params=_params(("parallel",)),
    )(pos, pair_sum, by_source, by_source)


def _adamw(w, g, m, v, *, name, jobs=()):
    r, c = w.shape
    tr = _row_tile(r)

    def body(w_ref, g_ref, m_ref, v_ref, d_ref, mo_ref, vo_ref, go_ref):
        gv = g_ref[...]
        go_ref[...] = gv
        mn = ADAM_B1 * m_ref[...] + (1.0 - ADAM_B1) * gv
        vn = ADAM_B2 * v_ref[...] + (1.0 - ADAM_B2) * (gv * gv)
        m_hat = mn / (1.0 - ADAM_B1 ** ADAM_STEP)
        v_hat = vn / (1.0 - ADAM_B2 ** ADAM_STEP)
        d_ref[...] = -ADAM_LR * (m_hat / (jnp.sqrt(v_hat) + ADAM_EPS) + ADAM_WD * w_ref[...])
        mo_ref[...] = mn
        vo_ref[...] = vn

    spec = pl.BlockSpec((tr, c), lambda i: (i, 0))
    out = jax.ShapeDtypeStruct((r, c), jnp.float32)
    return _call(body, name=name, grid=(r // tr,), in_specs=[spec] * 4, out_specs=[spec] * 4, out_shape=[out] * 4,
                 semantics=("parallel",), args=(w, g, m, v), jobs=jobs)


def _all_reduce_small(block):
    r, c = block.shape

    def body(src_ref, out_ref, stage_ref, send_sems, recv_sems):
        x, y, cc = _position()
        me = 4 * x + 2 * y + cc
        stage_ref[me] = src_ref[...]
        flip = lambda v, on: 1 - v if on else v
        peers = [(flip(x, dx), flip(y, dy), flip(cc, dc)) for dx in (0, 1) for dy in (0, 1) for dc in (0, 1)][1:]
        copies = [pltpu.make_async_remote_copy(
            src_ref=stage_ref.at[me], dst_ref=stage_ref.at[me],
            send_sem=send_sems.at[k], recv_sem=recv_sems.at[k], device_id=peer, device_id_type=MESH)
            for k, peer in enumerate(peers)]
        for cp in copies:
            cp.start()
        for k, (px, py, pc) in enumerate(peers):
            them = 4 * px + 2 * py + pc
            pltpu.make_async_remote_copy(
                src_ref=stage_ref.at[them], dst_ref=stage_ref.at[them],
                send_sem=send_sems.at[k], recv_sem=recv_sems.at[k], device_id=(px, py, pc),
                device_id_type=MESH).wait_recv()
        for cp in copies:
            cp.wait_send()
        total = stage_ref[0]
        for d in range(1, 8):
            total = total + stage_ref[d]
        out_ref[...] = total

    return pl.pallas_call(
        body, name="all_reduce_small",
        in_specs=[pl.BlockSpec(memory_space=pltpu.VMEM)], out_specs=pl.BlockSpec(memory_space=pltpu.VMEM),
        out_shape=jax.ShapeDtypeStruct((r, c), jnp.float32),
        scratch_shapes=[pltpu.VMEM((8, r, c), jnp.float32), pltpu.SemaphoreType.DMA((7,)),
                        pltpu.SemaphoreType.DMA((7,))],
        compiler_params=pltpu.CompilerParams(has_side_effects=True),
    )(block)


def _cols_from_shards(g):
    n, r, c = g.shape
    return jnp.transpose(g, (1, 0, 2)).reshape(r, n * c)


def _cols_to_shards(w, n=4):
    r, c = w.shape
    return jnp.transpose(w.reshape(r, n, c // n), (1, 0, 2))


def _pad_w_in(g):
    _, d, c = g.shape
    cut = COL_KR - 3 * c
    zeros = jnp.zeros((d, COL_CKV - COL_KR - QK_ROPE), g.dtype)
    return jnp.concatenate([g[0], g[1], g[2], g[3][:, :cut], g[3][:, cut + KV_RANK:], zeros,
                            g[3][:, cut:cut + KV_RANK]], axis=1)


def _unpad_w_in(padded):
    c = IN_WIDTH // 4
    last = jnp.concatenate([padded[:, 3 * c:COL_KR], padded[:, COL_CKV:COL_CKV + KV_RANK],
                            padded[:, COL_KR:COL_KR + QK_ROPE]], axis=1)
    return jnp.stack([padded[:, 0:c], padded[:, c:2 * c], padded[:, 2 * c:3 * c], last])


def _pad_w_uq(full):
    r = full.shape[0]
    per_head = full.reshape(r, N_HEADS, QK_NOPE + QK_ROPE)
    return jnp.pad(per_head, ((0, 0), (0, 0), (0, HEAD_PAD - QK_NOPE - QK_ROPE))).reshape(r, N_HEADS * HEAD_PAD)


def _unpad_w_uq(padded):
    r = padded.shape[0]
    return padded.reshape(r, N_HEADS, HEAD_PAD)[:, :, :QK_NOPE + QK_ROPE].reshape(r, N_HEADS * (QK_NOPE + QK_ROPE))


SMALL_ROWS = 16


def _pack_small(d, pre_mix, post_mix, pre_mlp, post_mlp, conv_out, attn_out, q_norm, kv_norm, conv_w):
    row = lambda *parts: jnp.pad(jnp.concatenate(parts, axis=1), ((0, 0), (0, d - sum(p.shape[1] for p in parts))))
    rows = [row(pre_mix), row(post_mix), row(pre_mlp), row(post_mlp), row(conv_out, attn_out), row(q_norm, kv_norm),
            row(conv_w[0:1]), row(conv_w[1:2]), row(conv_w[2:3])]
    return jnp.pad(jnp.concatenate(rows, axis=0), ((0, SMALL_ROWS - len(rows)), (0, 0)))


def _unpack_small(p, chip):
    cw = CONV_WIDTH // 4
    conv_w = lax.dynamic_slice(p[6:9, :CONV_WIDTH], (0, chip * cw), (3, cw))
    return dict(pre_mix_g=p[0:1], post_mix_g=p[1:2], pre_mlp_g=p[2:3], post_mlp_g=p[3:4],
                conv_out_g=p[4:5, :CONV_WIDTH], attn_out_g=p[4:5, CONV_WIDTH:2 * CONV_WIDTH],
                q_norm_g=p[5:6, :Q_RANK], kv_norm_g=p[5:6, Q_RANK:Q_RANK + KV_RANK], conv_w=conv_w[None])


def kernel(x, pre_mix_g, w_in, conv_w, q_norm_g, w_uq, kv_norm_g, w_ukv, conv_out_g, attn_out_g, w_o, post_mix_g, pre_mlp_g, w_up, w_down, post_mlp_g, loss_target, m_pre_mix_g, m_w_in, m_conv_w, m_q_norm_g, m_w_uq, m_kv_norm_g, m_w_ukv, m_conv_out_g, m_attn_out_g, m_w_o, m_post_mix_g, m_pre_mlp_g, m_w_up, m_w_down, m_post_mlp_g, v_pre_mix_g, v_w_in, v_conv_w, v_q_norm_g, v_w_uq, v_kv_norm_g, v_w_ukv, v_conv_out_g, v_attn_out_g, v_w_o, v_post_mix_g, v_pre_mlp_g, v_w_up, v_w_down, v_post_mlp_g):
    bf16 = jnp.bfloat16
    s, d = x.shape[1], x.shape[2]
    d_ff = 4 * d
    chip = 2 * lax.axis_index("x") + lax.axis_index("y")
    xs = x.reshape(s, d)
    target = loss_target.reshape(s, d)
    tm = min(256, s)
    t_attn = min(512, s)
    mt = min(1024, s)
    kt = min(2048, s)

    big = dict(w_in=w_in[0], w_uq=w_uq[0], w_ukv=w_ukv[0], w_o=w_o[0], w_up=w_up[0], w_down=w_down[0])
    names = list(big)
    pos = jnp.stack([lax.axis_index("c"), chip]).astype(jnp.int32)
    wb = {}
    half = {k: big[k].shape[0] // 2 for k in names}

    def rows(k, a, b):
        lo, n = a * half[k] // 64, (b - a) * half[k] // 64
        assert lo % 16 == 0 and n % 16 == 0 and n > 0, (k, a, b)
        return lo, n

    ici = lambda k, a=0, b=64: _GatherIci(wb[k], *rows(k, a, b))
    fwd = lambda k, a=0, b=64: _GatherForward(wb[k], *rows(k, a, b))
    near = lambda k, a=0, b=64: _GatherD2d(wb[k], *rows(k, a, b), slots=(0, 1))
    far = lambda k, a=0, b=64: _GatherD2d(wb[k], *rows(k, a, b), slots=(2,))
    wb["w_in"] = _Buf(_cast_into_slot(big["w_in"], pos, name="cast_w_in"))
    rest = [k for k in names if k != "w_in"]
    for k, slot in zip(rest, _cast_many_into_slots([big[k] for k in rest], pos, name="cast_rest",
                                                   jobs=[ici("w_in")])):
        wb[k] = _Buf(slot)
    h1 = _rms_fwd(xs, pre_mix_g, width=d, col=0, tm=tm, name="rms_pre_mix", jobs=[fwd("w_in"), near("w_in")])
    _comm("gather_w_in", [[far("w_in")]])
    win = _pad_w_in(wb["w_in"].arr)
    ff4 = d_ff // 4

    spread = lambda a: lax.dynamic_update_slice(jnp.zeros((3, CONV_WIDTH), jnp.float32), a[0],
                                                (0, chip * (CONV_WIDTH // 4)))
    conv_w_mine = jnp.where(lax.axis_index("c") == 0, spread(conv_w), 0.0)
    conv_w_full = _all_reduce_small(jnp.pad(conv_w_mine, ((0, 5), (0, 0))))[0:3]

    proj = _matmul(h1, win, dims=_NN, mnk=(s, IN_PAD, d), tiles=(mt, IN_PAD // 3, d), name="mm_proj",
                   jobs=[ici("w_uq"), ici("w_ukv"), ici("w_o")])
    y_conv = _conv_fwd(proj, conv_w_full, conv_out_g,
                       jobs=[fwd("w_uq"), fwd("w_ukv"), fwd("w_o"), near("w_uq"), near("w_ukv"), near("w_o")])
    cqn = _rms_fwd(proj, q_norm_g, width=Q_RANK, col=COL_CQ // Q_RANK, tm=mt, name="rms_q",
                   jobs=[far("w_uq"), far("w_ukv"), far("w_o"), ici("w_up", 0, 4)])
    wuq = _pad_w_uq(_cols_from_shards(wb["w_uq"].arr))
    wukv = _cols_from_shards(wb["w_ukv"].arr)
    wo = wb["w_o"].arr.reshape(-1, d)
    ckvn = _rms_fwd(proj, kv_norm_g, width=KV_RANK, col=COL_CKV // KV_RANK, tm=mt, name="rms_kv",
                    jobs=[ici("w_up", 4, 10), fwd("w_up", 0, 4)])
    q_pad = _matmul(cqn, wuq, dims=_NN, mnk=(s, N_HEADS * HEAD_PAD, Q_RANK), tiles=(mt, 1024, Q_RANK), name="mm_q",
                    jobs=[ici("w_up", 10, 18), fwd("w_up", 4, 10), near("w_up", 0, 4)])
    kv = _matmul(ckvn, wukv, dims=_NN, mnk=(s, N_HEADS * HEAD_PAD, KV_RANK), tiles=(mt, 1024, KV_RANK),
                 name="mm_kv", out_dtype=bf16,
                 jobs=[ici("w_up", 18, 24), fwd("w_up", 10, 18), near("w_up", 4, 10), far("w_up", 0, 4)])
    cos, sin = _rope_tables(s)
    q_rot, kr_rot = _qk_rope_fwd(
        q_pad, proj, cos, sin, tm=tm,
        jobs=[ici("w_up", 24, 32), fwd("w_up", 18, 24), near("w_up", 10, 18), far("w_up", 4, 10)])
    o, lse, y_attn = _attn_fwd(
        q_rot, kv, kr_rot, attn_out_g, t=t_attn,
        jobs=[ici("w_up", 32, 64), ici("w_down", 0, 8), fwd("w_up", 24, 32), near("w_up", 18, 24), far("w_up", 10, 18)])
    ycat = jnp.concatenate([y_conv, y_attn], axis=1)
    mix = _matmul(ycat, wo, dims=_NN, mnk=(s, d, 2 * CONV_WIDTH), tiles=(mt, 1024, 2 * CONV_WIDTH), name="mm_out",
                  jobs=[fwd("w_up", 32, 64), near("w_up", 24, 32), far("w_up", 18, 24)])
    x2, h2 = _mix_residual_fwd(
        xs, mix, post_mix_g, pre_mlp_g, tm=tm,
        jobs=[ici("w_down", 8, 20), fwd("w_down", 0, 8), near("w_up", 32, 64), far("w_up", 24, 64)])
    wup = wb["w_up"].arr

    def up_epilogue(acc, extra_refs, out_refs):
        r = jnp.maximum(acc, 0.0)
        out_refs[0][...] = acc.astype(bf16)
        out_refs[1][...] = (r * r).astype(bf16)

    n_ff = ff4 // 1024
    act = jax.ShapeDtypeStruct((s, d_ff), bf16)
    up, act_sq = _matmul(
        h2, wup, dims=_NN, mnk=(s, d_ff, d), tiles=(mt, 1024, d), name="mm_up",
        b_spec=pl.BlockSpec((None, d, 1024), lambda i, j, l: (j // n_ff, l, j % n_ff)),
        out_shape=(act, act), o_spec=(pl.BlockSpec((mt, 1024), lambda i, j, l: (i, j)),) * 2, epilogue=up_epilogue,
        jobs=[ici("w_down", 20, 64), fwd("w_down", 8, 20), near("w_down", 0, 8)])
    _comm("gather_w_down_tail", [[fwd("w_down", 20, 64), near("w_down", 8, 64), far("w_down", 0, 20)],
                                 [far("w_down", 20, 64)]])
    wdown = wb["w_down"].arr.reshape(d_ff, d)
    mlp = _matmul(act_sq, wdown, dims=_NN, mnk=(s, d, d_ff), tiles=(min(512, s), 512, d_ff), name="mm_down")
    dx3, dmlp, dg_post_mlp, loss_part = _loss_head(x2, mlp, target, post_mlp_g, tm=tm)

    def dup_epilogue(acc, extra_refs, out_refs):
        out_refs[0][...] = (acc * (2.0 * jnp.maximum(extra_refs[0][...].astype(jnp.float32), 0.0))).astype(bf16)

    grads, theirs, pair_sums, via, folded, by_source, whole = {}, {}, {}, {}, {}, {}, {}

    def exchange(k, g):
        grads[k] = g
        theirs[k] = _Buf(jax.ShapeDtypeStruct((4, g.shape[1] // 2, g.shape[2]), bf16))
        return _PairExchange(g, theirs[k])

    def pair_sum(k, jobs=()):
        pair_sums[k] = _pair_add(grads[k], theirs[k].arr, pos, name="pair_add_" + k, jobs=jobs)
        via[k] = _Buf(jax.ShapeDtypeStruct(pair_sums[k].shape[1:], bf16))
        by_source[k] = _Buf(jax.ShapeDtypeStruct(pair_sums[k].shape, bf16))

    def diag(k, a=0, b=32):
        lo, n = a * half[k] // 64, (b - a) * half[k] // 64
        assert lo % 16 == 0 and n % 16 == 0 and n > 0, (k, a, b)
        return _ScatterDiag(pair_sums[k], via[k], lo, n)

    def fold(k):
        folded[k] = _fold_diag(pair_sums[k], via[k].arr, pos, name="fold_" + k)

    scatter = lambda k, a=0, b=64: _ScatterNear(folded[k], by_source[k], *rows(k, a, b))

    def share(k):
        whole[k] = _Buf(_chip_sum(by_source[k].arr, pair_sums[k], pos, name="chip_sum_" + k))
        return _PairShare(whole[k])

    g_wdown = _matmul(act_sq, dmlp, dims=_TN, mnk=(d_ff, d, s), tiles=(1024, 1024, kt), name="mm_gw_down",
                      out_dtype=bf16).reshape(4, ff4, d)
    dup = _matmul(dmlp, wdown, dims=_NT, mnk=(s, d_ff, d), tiles=(mt, 1024, d), name="mm_dact",
                  out_dtype=bf16, epilogue=dup_epilogue, extra=(up,),
                  extra_specs=(pl.BlockSpec((mt, 1024), lambda i, j, l: (i, j)),),
                  jobs=[exchange("w_down", g_wdown)])
    pair_sum("w_down")
    g_wup = _matmul(h2, dup, dims=_TN, mnk=(d, d_ff, s), tiles=(1024, 1024, kt), name="mm_gw_up",
                    out_shape=jax.ShapeDtypeStruct((4, d, ff4), bf16),
                    o_spec=pl.BlockSpec((None, 1024, 1024), lambda i, j, l: (j // n_ff, i, j % n_ff)),
                    jobs=[diag("w_down")])
    fold("w_down")
    dh2 = _matmul(dup, wup, dims=_NT, mnk=(s, d, d_ff), tiles=(mt, 1024, ff4), name="mm_dh2",
                  b_spec=pl.BlockSpec((None, 1024, ff4), lambda i, j, l: (l, j, 0)),
                  jobs=[exchange("w_up", g_wup), scatter("w_down", 0, 48)])
    pair_sum("w_up", jobs=[scatter("w_down", 48, 64)])
    dx2, dmix, dg_pre_mlp, dg_post_mix = _mix_residual_bwd(
        dx3, dh2, x2, mix, pre_mlp_g, post_mix_g, tm=tm, jobs=[diag("w_up", 0, 20)])

    dycat = _matmul(dmix, wo, dims=_NT, mnk=(s, 2 * CONV_WIDTH, d), tiles=(mt, 1024, d), name="mm_dycat",
                    jobs=[diag("w_up", 20, 32)])
    fold("w_up")
    g_wo = _matmul(ycat, dmix, dims=_TN, mnk=(2 * CONV_WIDTH, d, s), tiles=(1024, 1024, kt),
                   name="mm_gw_out", out_dtype=bf16, jobs=[scatter("w_up", 0, 12)]).reshape(4, CONV_WIDTH // 2, d)
    du, dgb, dgc, dg_conv_w, dg_conv_out = _conv_bwd(proj, conv_w_full, conv_out_g, dycat,
                                                     jobs=[exchange("w_o", g_wo), scatter("w_up", 12, 24)])
    pair_sum("w_o")
    do, delta, dg_attn_out = _attn_norm_bwd(o, attn_out_g, dycat)
    dq_pad, dk_pad, dv = _attn_bwd(q_rot, kv, kr_rot, do, lse, delta, t=t_attn,
                                   jobs=[scatter("w_up", 24, 64), diag("w_o"), share("w_down")])
    fold("w_o")
    dq_raw, dkv, dkr = _qk_rope_bwd(dq_pad, dk_pad, dv, cos, sin, tm=tm, jobs=[scatter("w_o")])
    wq_cols = N_HEADS * HEAD_PAD
    g_wuq = _matmul(cqn, dq_raw, dims=_TN, mnk=(Q_RANK, wq_cols, s), tiles=(Q_RANK, 1024, kt),
                    name="mm_gw_uq", out_dtype=bf16)
    dcqn = _matmul(dq_raw, wuq, dims=_NT, mnk=(s, Q_RANK, wq_cols), tiles=(mt, Q_RANK, wq_cols), name="mm_dcq")
    g_wukv = _matmul(ckvn, dkv, dims=_TN, mnk=(KV_RANK, wq_cols, s), tiles=(KV_RANK, 1024, kt),
                     name="mm_gw_ukv", out_dtype=bf16)
    dckvn = _matmul(dkv, wukv, dims=_NT, mnk=(s, KV_RANK, wq_cols), tiles=(mt, KV_RANK, wq_cols), name="mm_dckv",
                    jobs=[exchange("w_uq", _cols_to_shards(_unpad_w_uq(g_wuq))),
                          exchange("w_ukv", _cols_to_shards(g_wukv))])
    pair_sum("w_uq")
    pair_sum("w_ukv")
    dcq, dg_q_norm = _rms_bwd(proj, q_norm_g, dcqn, width=Q_RANK, col=COL_CQ // Q_RANK, tm=mt, name="rms_q_bwd",
                              jobs=[diag("w_uq"), diag("w_ukv")])
    fold("w_uq")
    fold("w_ukv")
    dckv, dg_kv_norm = _rms_bwd(proj, kv_norm_g, dckvn, width=KV_RANK, col=COL_CKV // KV_RANK, tm=mt,
                                name="rms_kv_bwd")
    dproj = jnp.concatenate([du, dgb, dgc, dcq, dkr, jnp.zeros((s, COL_CKV - COL_KR - 128), bf16), dckv], axis=1)
    g_win = _matmul(h1, dproj, dims=_TN, mnk=(d, IN_PAD, s), tiles=(1024, IN_PAD // 3, kt), name="mm_gw_in",
                    out_dtype=bf16, jobs=[scatter("w_uq"), scatter("w_ukv"), share("w_up"), share("w_o")])
    _comm("pair_exchange_w_in", [[exchange("w_in", _unpad_w_in(g_win))]])
    pair_sum("w_in")
    _comm("scatter_diag_w_in", [[diag("w_in")]])
    fold("w_in")
    dh1 = _matmul(dproj, win, dims=_NT, mnk=(s, d, IN_PAD), tiles=(mt, 1024, IN_PAD // 2), name="mm_dh1",
                  jobs=[scatter("w_in"), share("w_uq"), share("w_ukv")])
    grad_x, dg_pre_mix = _input_bwd(dx2, dh1, xs, pre_mix_g, tm=tm)
    _comm("pair_share_w_in", [[share("w_in")]])

    moments = dict(w_in=(m_w_in, v_w_in), w_uq=(m_w_uq, v_w_uq), w_ukv=(m_w_ukv, v_w_ukv), w_o=(m_w_o, v_w_o),
                   w_up=(m_w_up, v_w_up), w_down=(m_w_down, v_w_down))
    gw, dw, nm, nv = {}, {}, {}, {}
    for k in names:
        delta_k, nm_k, nv_k, g = _adamw(big[k], whole[k].arr, moments[k][0][0], moments[k][1][0], name="adamw_" + k)
        gw[k], dw[k], nm[k], nv[k] = g[None], delta_k[None], nm_k[None], nv_k[None]

    small_g = _all_reduce_small(_pack_small(d, dg_pre_mix, dg_post_mix, dg_pre_mlp, dg_post_mlp, dg_conv_out,
                                            dg_attn_out, dg_q_norm, dg_kv_norm, dg_conv_w
                                            ).at[SMALL_ROWS - 1, :128].set(loss_part[0]))
    loss = small_g[SMALL_ROWS - 1, 0]
    pack_w = lambda cw, pre_mix, post_mix, pre_mlp, post_mlp, conv_out, attn_out, q_norm, kv_norm: _pack_small(
        d, pre_mix, post_mix, pre_mlp, post_mlp, conv_out, attn_out, q_norm, kv_norm, cw)
    small_w = pack_w(conv_w_full, pre_mix_g, post_mix_g, pre_mlp_g, post_mlp_g, conv_out_g, attn_out_g, q_norm_g,
                     kv_norm_g)
    small_m = pack_w(spread(m_conv_w), m_pre_mix_g, m_post_mix_g, m_pre_mlp_g, m_post_mlp_g, m_conv_out_g,
                     m_attn_out_g, m_q_norm_g, m_kv_norm_g)
    small_v = pack_w(spread(v_conv_w), v_pre_mix_g, v_post_mix_g, v_pre_mlp_g, v_post_mlp_g, v_conv_out_g,
                     v_attn_out_g, v_q_norm_g, v_kv_norm_g)
    small_d, small_nm, small_nv, small_g = _adamw(small_w, small_g, small_m, small_v, name="adamw_small")
    sg, sd, snm, snv = (_unpack_small(p, chip) for p in (small_g, small_d, small_nm, small_nv))

    for src, dst in ((sg, gw), (sd, dw), (snm, nm), (snv, nv)):
        dst.update(src)

    order = ["pre_mix_g", "w_in", "conv_w", "q_norm_g", "w_uq", "kv_norm_g", "w_ukv", "conv_out_g", "attn_out_g",
             "w_o", "post_mix_g", "pre_mlp_g", "w_up", "w_down", "post_mlp_g"]
    return (loss, grad_x.reshape(1, s, d), *[gw[k] for k in order], *[dw[k] for k in order],
            *[nm[k] for k in order], *[nv[k] for k in order])
```

```python
import functools

import jax
import jax.numpy as jnp
from jax import lax
from jax.experimental import pallas as pl
from jax.experimental.pallas import tpu as pltpu

EPS = 1e-6
NEG_INF = -1e30
CHUNK_SHIFT = 6
N_HEADS = 8
HEAD_PAD = 256
QK_NOPE = 128
QK_ROPE = 64
V_DIM = 128
CONV_WIDTH = 1024
Q_RANK = 768
KV_RANK = 512
ROPE_THETA = 10000.0
ATTN_SCALE = (QK_NOPE + QK_ROPE) ** -0.5
ADAM_LR, ADAM_B1, ADAM_B2, ADAM_EPS, ADAM_WD, ADAM_STEP = 0.001, 0.9, 0.999, 1e-08, 0.01, 10

COL_CQ = 3 * CONV_WIDTH
COL_KR = COL_CQ + Q_RANK
COL_CKV = 4096
IN_PAD = COL_CKV + KV_RANK
IN_WIDTH = 3 * CONV_WIDTH + Q_RANK + KV_RANK + QK_ROPE

VMEM_LIMIT_BYTES = 56 * 1024 * 1024
MESH = pl.DeviceIdType.MESH
ANY = pl.BlockSpec(memory_space=pl.ANY)

_NN = (((1,), (0,)), ((), ()))
_NT = (((1,), (1,)), ((), ()))
_TN = (((0,), (0,)), ((), ()))


def _params(sem):
    return pltpu.CompilerParams(dimension_semantics=sem, vmem_limit_bytes=VMEM_LIMIT_BYTES)


class _Buf:
    def __init__(self, arr):
        self.arr = arr


def _position():
    return lax.axis_index("x"), lax.axis_index("y"), lax.axis_index("c")


def _other_chips(x, y):
    return [(2 * (1 - x) + y, (1 - x, y)), (2 * x + (1 - y), (x, 1 - y)), (2 * (1 - x) + (1 - y), (1 - x, 1 - y))]


def _remote(src, dst, sems, k, to):
    send, recv, off = sems
    return pltpu.make_async_remote_copy(src_ref=src, dst_ref=dst, send_sem=send.at[off + k], recv_sem=recv.at[off + k],
                                        device_id=to, device_id_type=MESH)


class _GatherIci:
    n_sems = 2
    link = "ici"

    def __init__(self, buf, lo, n):
        self.buf, self.lo, self.n, self.buffers, self.sources = buf, lo, n, [buf], []

    def _rows(self, ref, slot, which, lo=None, n=None):
        lo, n = (self.lo, self.n) if lo is None else (lo, n)
        return ref.at[slot, pl.ds(which * (ref.shape[1] // 2) + lo, n), :]

    def start(self, refs, sems):
        ref = refs[id(self.buf)]
        x, y, c = _position()
        mine = self._rows(ref, 2 * x + y, c)
        for k, (_, xy) in enumerate(_other_chips(x, y)[:2]):
            _remote(mine, mine, sems, k, (*xy, c)).start()

    def wait(self, refs, sems):
        ref = refs[id(self.buf)]
        x, y, c = _position()
        mine = self._rows(ref, 2 * x + y, c)
        for k, (slot, xy) in enumerate(_other_chips(x, y)[:2]):
            landed = self._rows(ref, slot, c)
            _remote(landed, landed, sems, k, (*xy, c)).wait_recv()
            _remote(mine, mine, sems, k, (*xy, c)).wait_send()


class _GatherForward(_GatherIci):
    def _ways(self, x, y):
        (slot_x, xy_x), (slot_y, xy_y), (slot_d, _) = _other_chips(x, y)
        h = self.n // 2
        assert h % 16 == 0, self.n
        return [(slot_x, slot_d, self.lo, xy_y), (slot_y, slot_d, self.lo + h, xy_x)], h

    def start(self, refs, sems):
        ref = refs[id(self.buf)]
        x, y, c = _position()
        ways, h = self._ways(x, y)
        for k, (slot, _, lo, xy) in enumerate(ways):
            rows = self._rows(ref, slot, c, lo, h)
            _remote(rows, rows, sems, k, (*xy, c)).start()

    def wait(self, refs, sems):
        ref = refs[id(self.buf)]
        x, y, c = _position()
        ways, h = self._ways(x, y)
        for k, (slot, lands, lo, xy) in enumerate(ways):
            landed, sent = self._rows(ref, lands, c, lo, h), self._rows(ref, slot, c, lo, h)
            _remote(landed, landed, sems, k, (*xy, c)).wait_recv()
            _remote(sent, sent, sems, k, (*xy, c)).wait_send()


class _GatherD2d(_GatherIci):
    link = "d2d"

    def __init__(self, buf, lo, n, slots):
        super().__init__(buf, lo, n)
        self.slots, self.n_sems = slots, len(slots)

    def start(self, refs, sems):
        ref = refs[id(self.buf)]
        x, y, c = _position()
        chips = _other_chips(x, y)
        for k, which in enumerate(self.slots):
            rows = self._rows(ref, chips[which][0], c)
            _remote(rows, rows, sems, k, (x, y, 1 - c)).start()

    def wait(self, refs, sems):
        ref = refs[id(self.buf)]
        x, y, c = _position()
        chips = _other_chips(x, y)
        for k, which in enumerate(self.slots):
            sent, landed = self._rows(ref, chips[which][0], c), self._rows(ref, chips[which][0], 1 - c)
            _remote(landed, landed, sems, k, (x, y, 1 - c)).wait_recv()
            _remote(sent, sent, sems, k, (x, y, 1 - c)).wait_send()


class _ScatterDiag:
    n_sems = 2
    link = "ici"

    def __init__(self, src, via, lo, n):
        self.src, self.via, self.lo, self.n, self.buffers, self.sources = src, via, lo, n, [via], [src]

    def _copies(self, refs, sems):
        src, via = refs[id(self.src)], refs[id(self.via)]
        x, y, c = _position()
        (_, xy_x), (_, xy_y), (slot_d, _) = _other_chips(x, y)
        h2 = via.shape[0] // 2
        return [_remote(src.at[slot_d, pl.ds(first + self.lo, self.n), :], via.at[pl.ds(first + self.lo, self.n), :],
                        sems, k, (*xy, c)) for k, (first, xy) in enumerate(((0, xy_x), (h2, xy_y)))]

    def start(self, refs, sems):
        for cp in self._copies(refs, sems):
            cp.start()

    def wait(self, refs, sems):
        for cp in self._copies(refs, sems):
            cp.wait_recv()
            cp.wait_send()


class _ScatterNear:
    n_sems = 2
    link = "ici"

    def __init__(self, src, dst, lo, n):
        self.src, self.dst, self.lo, self.n, self.buffers, self.sources = src, dst, lo, n, [dst], [src]

    def _copies(self, refs, sems, landing):
        src, dst = refs[id(self.src)], refs[id(self.dst)]
        x, y, c = _position()
        rows = pl.ds(self.lo, self.n)
        return [_remote(src.at[k, rows, :], dst.at[slot if landing else 2 * x + y, rows, :], sems, k, (*xy, c))
                for k, (slot, xy) in enumerate(_other_chips(x, y)[:2])]

    def start(self, refs, sems):
        for cp in self._copies(refs, sems, False):
            cp.start()

    def wait(self, refs, sems):
        for cp in self._copies(refs, sems, True):
            cp.wait_recv()
            cp.wait_send()


class _PairExchange:
    n_sems = 1
    link = "d2d"

    def __init__(self, src, dst):
        self.src, self.dst, self.buffers, self.sources = src, dst, [dst], [src]

    def _copy(self, refs, sems):
        src, dst = refs[id(self.src)], refs[id(self.dst)]
        x, y, c = _position()
        h = src.shape[1] // 2
        return _remote(src.at[:, pl.ds((1 - c) * h, h), :], dst, sems, 0, (x, y, 1 - c))

    def start(self, refs, sems):
        self._copy(refs, sems).start()

    def wait(self, refs, sems):
        self._copy(refs, sems).wait()


class _PairShare:
    n_sems = 1
    link = "d2d"

    def __init__(self, buf):
        self.buf, self.buffers, self.sources = buf, [buf], []

    def _rows(self, ref, which):
        h = ref.shape[0] // 2
        return ref.at[pl.ds(which * h, h), :]

    def start(self, refs, sems):
        ref = refs[id(self.buf)]
        x, y, c = _position()
        _remote(self._rows(ref, c), self._rows(ref, c), sems, 0, (x, y, 1 - c)).start()

    def wait(self, refs, sems):
        ref = refs[id(self.buf)]
        x, y, c = _position()
        _remote(self._rows(ref, c), self._rows(ref, c), sems, 0, (x, y, 1 - c)).wait_send()
        _remote(self._rows(ref, 1 - c), self._rows(ref, 1 - c), sems, 0, (x, y, 1 - c)).wait_recv()


_COLLECTIVE_IDS = {("ici",): 1, ("d2d",): 2, ("d2d", "ici"): 3}


def _links(jobs):
    return tuple(sorted({j.link for j in jobs}))


def _handshake(links):
    x, y, c = _position()
    peers = ([(1 - x, y, c), (x, 1 - y, c)] if "ici" in links else []) + ([(x, y, 1 - c)] if "d2d" in links else [])
    barrier = pltpu.get_barrier_semaphore()
    for peer in peers:
        pl.semaphore_signal(barrier, inc=1, device_id=peer, device_id_type=MESH)
    pl.semaphore_wait(barrier, len(peers))


def _unique(items):
    seen, out = set(), []
    for it in items:
        if id(it) not in seen:
            seen.add(id(it))
            out.append(it)
    return out


def _job_operands(jobs):
    sources = _unique([a for j in jobs for a in j.sources])
    buffers = _unique([b for j in jobs for b in j.buffers])
    held = [b for b in buffers if not isinstance(b.arr, jax.ShapeDtypeStruct)]
    fresh = [b for b in buffers if isinstance(b.arr, jax.ShapeDtypeStruct)]
    return sources, held, fresh


def _sem_offsets(jobs):
    offs, total = [], 0
    for j in jobs:
        offs.append(total)
        total += j.n_sems
    return offs, total


def _call(body, *, name, grid, in_specs, out_specs, out_shape, args, semantics, scratch_shapes=(), jobs=(),
          prefetch=None):
    n_pre = 0 if prefetch is None else 1

    def launch(fn, in_specs, out_specs, scratch, **kw):
        if prefetch is None:
            return pl.pallas_call(fn, name=name, grid=grid, in_specs=in_specs, out_specs=out_specs,
                                  scratch_shapes=scratch, **kw)
        return pl.pallas_call(fn, name=name, grid_spec=pltpu.PrefetchScalarGridSpec(
            num_scalar_prefetch=1, grid=grid, in_specs=in_specs, out_specs=out_specs, scratch_shapes=scratch), **kw)

    pre = () if prefetch is None else (prefetch,)
    if not jobs:
        return launch(body, list(in_specs), list(out_specs), list(scratch_shapes), out_shape=list(out_shape),
                      compiler_params=_params(semantics))(*pre, *args)
    sources, held, fresh = _job_operands(jobs)
    offs, n_sem = _sem_offsets(jobs)
    links = _links(jobs)
    n_in, n_out, n_scr = len(in_specs), len(out_specs), len(scratch_shapes)
    n_src, n_held, n_fresh = len(sources), len(held), len(fresh)

    def carried(*refs):
        pre_refs, refs = refs[:n_pre], refs[n_pre:]
        ins = refs[:n_in]
        src_refs = refs[n_in:n_in + n_src]
        o0 = n_in + n_src + n_held
        outs = refs[o0:o0 + n_out]
        buf_refs = refs[o0 + n_out:o0 + n_out + n_held + n_fresh]
        s0 = o0 + n_out + n_held + n_fresh
        scratch = refs[s0:s0 + n_scr]
        send, recv = refs[s0 + n_scr], refs[s0 + n_scr + 1]
        where = {id(a): r for a, r in zip(sources, src_refs)}
        where.update({id(b): r for b, r in zip(held + fresh, buf_refs)})
        ids = [pl.program_id(a) for a in range(len(grid))]
        first = functools.reduce(jnp.logical_and, [i == 0 for i in ids])
        last = functools.reduce(jnp.logical_and, [i == g - 1 for i, g in zip(ids, grid)])

        @pl.when(first)
        def _():
            _handshake(links)
            for j, off in zip(jobs, offs):
                j.start(where, (send, recv, off))

        body(*pre_refs, *ins, *outs, *scratch)

        @pl.when(last)
        def _():
            for j, off in zip(jobs, offs):
                j.wait(where, (send, recv, off))

    shape_of = lambda b: jax.ShapeDtypeStruct(b.arr.shape, b.arr.dtype)
    res = launch(
        carried, [*in_specs, *[ANY] * (n_src + n_held)], [*out_specs, *[ANY] * (n_held + n_fresh)],
        [*scratch_shapes, pltpu.SemaphoreType.DMA((n_sem,)), pltpu.SemaphoreType.DMA((n_sem,))],
        out_shape=[*out_shape, *[shape_of(b) for b in held + fresh]],
        input_output_aliases={n_pre + n_in + n_src + i: n_out + i for i in range(n_held)},
        compiler_params=pltpu.CompilerParams(dimension_semantics=("arbitrary",) * len(grid),
                                             vmem_limit_bytes=VMEM_LIMIT_BYTES, has_side_effects=True,
                                             collective_id=_COLLECTIVE_IDS[links]),
    )(*pre, *args, *sources, *[b.arr for b in held])
    for b, new in zip(held + fresh, res[n_out:]):
        b.arr = new
    return list(res[:n_out])


def _comm(name, phases):
    jobs = [j for ph in phases for j in ph]
    sources, held, fresh = _job_operands(jobs)
    offs, n_sem = _sem_offsets(jobs)
    off_of = {id(j): o for j, o in zip(jobs, offs)}
    links = _links(jobs)
    n_src, n_held, n_fresh = len(sources), len(held), len(fresh)

    def body(*refs):
        src_refs = refs[:n_src]
        buf_refs = refs[n_src + n_held:n_src + 2 * n_held + n_fresh]
        send, recv = refs[-2], refs[-1]
        where = {id(a): r for a, r in zip(sources, src_refs)}
        where.update({id(b): r for b, r in zip(held + fresh, buf_refs)})
        _handshake(links)
        for ph in phases:
            for j in ph:
                j.start(where, (send, recv, off_of[id(j)]))
            for j in ph:
                j.wait(where, (send, recv, off_of[id(j)]))

    shape_of = lambda b: jax.ShapeDtypeStruct(b.arr.shape, b.arr.dtype)
    res = pl.pallas_call(
        body, name=name,
        in_specs=[ANY] * (n_src + n_held), out_specs=[ANY] * (n_held + n_fresh),
        out_shape=[shape_of(b) for b in held + fresh],
        input_output_aliases={n_src + i: i for i in range(n_held)},
        scratch_shapes=[pltpu.SemaphoreType.DMA((n_sem,)), pltpu.SemaphoreType.DMA((n_sem,))],
        compiler_params=pltpu.CompilerParams(has_side_effects=True, collective_id=_COLLECTIVE_IDS[links]),
    )(*sources, *[b.arr for b in held])
    for b, new in zip(held + fresh, res):
        b.arr = new


def _matmul(a, b, *, dims, mnk, tiles, name, out_dtype=jnp.float32, a_spec=None, b_spec=None,
            out_shape=None, o_spec=None, epilogue=None, extra=(), extra_specs=(), jobs=()):
    m, n, k = mnk
    tm, tn, tk = tiles
    assert m % tm == 0 and n % tn == 0 and k % tk == 0, (name, mnk, tiles)
    gm, gn, gk = m // tm, n // tn, k // tk
    if a_spec is None:
        a_spec = (pl.BlockSpec((tk, tm), lambda i, j, l: (l, i)) if dims is _TN
                  else pl.BlockSpec((tm, tk), lambda i, j, l: (i, l)))
    if b_spec is None:
        b_spec = (pl.BlockSpec((tn, tk), lambda i, j, l: (j, l)) if dims is _NT
                  else pl.BlockSpec((tk, tn), lambda i, j, l: (l, j)))
    if out_shape is None:
        out_shape = jax.ShapeDtypeStruct((m, n), out_dtype)
    if o_spec is None:
        o_spec = pl.BlockSpec((tm, tn), lambda i, j, l: (i, j))
    single = not isinstance(out_shape, (tuple, list))
    n_extra = len(extra)

    def finish(acc, extra_refs, out_refs):
        if epilogue is None:
            out_refs[0][...] = acc.astype(out_refs[0].dtype)
        else:
            epilogue(acc, extra_refs, out_refs)

    def body_whole_k(*refs):
        a_ref, b_ref = refs[0], refs[1]
        acc = lax.dot_general(a_ref[...], b_ref[...], dims, preferred_element_type=jnp.float32)
        finish(acc, refs[2:2 + n_extra], refs[2 + n_extra:])

    def body_split_k(*refs):
        a_ref, b_ref = refs[0], refs[1]
        extra_refs = refs[2:2 + n_extra]
        out_refs = refs[2 + n_extra:-1]
        acc_ref = refs[-1]
        step = pl.program_id(2)
        part = lax.dot_general(a_ref[...], b_ref[...], dims, preferred_element_type=jnp.float32)

        @pl.when(step == 0)
        def _():
            acc_ref[...] = part

        @pl.when(jnp.logical_and(step > 0, step < gk - 1))
        def _():
            acc_ref[...] += part

        @pl.when(step == gk - 1)
        def _():
            finish(acc_ref[...] + part, extra_refs, out_refs)

    res = _call(
        body_whole_k if gk == 1 else body_split_k, name=name, grid=(gm, gn, gk),
        in_specs=[a_spec, b_spec, *extra_specs],
        out_specs=[o_spec] if single else list(o_spec),
        out_shape=[out_shape] if single else list(out_shape),
        scratch_shapes=[] if gk == 1 else [pltpu.VMEM((tm, tn), jnp.float32)],
        semantics=("parallel", "parallel", "arbitrary"), args=(a, b, *extra), jobs=jobs)
    return res[0] if single else res


def _rstd(x):
    return lax.rsqrt(jnp.mean(x * x, axis=-1, keepdims=True) + EPS)


def _rms_bwd_rows(x, g, dy):
    r = _rstd(x)
    xn = x * r
    dyg = dy * g
    dx = r * (dyg - xn * jnp.mean(xn * dyg, axis=-1, keepdims=True))
    return dx, dy * xn


def _acc_rows(ref, rows, first):
    part = jnp.sum(rows, axis=0, keepdims=True)

    @pl.when(first)
    def _():
        ref[...] = part

    @pl.when(jnp.logical_not(first))
    def _():
        ref[...] += part


def _rms_fwd(x, g, *, width, col, tm, name, jobs=()):
    s = x.shape[0]

    def body(x_ref, g_ref, o_ref):
        v = x_ref[...]
        o_ref[...] = (v * _rstd(v) * g_ref[...]).astype(o_ref.dtype)

    return _call(
        body, name=name, grid=(s // tm,),
        in_specs=[pl.BlockSpec((tm, width), lambda i: (i, col)), pl.BlockSpec((1, width), lambda i: (0, 0))],
        out_specs=[pl.BlockSpec((tm, width), lambda i: (i, 0))],
        out_shape=[jax.ShapeDtypeStruct((s, width), jnp.bfloat16)],
        semantics=("parallel",), args=(x, g), jobs=jobs)[0]


def _rms_bwd(x, g, dy, *, width, col, tm, name, jobs=()):
    s = x.shape[0]

    def body(x_ref, g_ref, dy_ref, dx_ref, dg_ref):
        dx, dgr = _rms_bwd_rows(x_ref[...], g_ref[...], dy_ref[...])
        dx_ref[...] = dx.astype(dx_ref.dtype)
        _acc_rows(dg_ref, dgr, pl.program_id(0) == 0)

    return _call(
        body, name=name, grid=(s // tm,),
        in_specs=[pl.BlockSpec((tm, width), lambda i: (i, col)), pl.BlockSpec((1, width), lambda i: (0, 0)),
                  pl.BlockSpec((tm, width), lambda i: (i, 0))],
        out_specs=[pl.BlockSpec((tm, width), lambda i: (i, 0)), pl.BlockSpec((1, width), lambda i: (0, 0))],
        out_shape=[jax.ShapeDtypeStruct((s, width), jnp.bfloat16), jax.ShapeDtypeStruct((1, width), jnp.float32)],
        semantics=("arbitrary",), args=(x, g, dy), jobs=jobs)


def _row_specs(tm, d, n):
    return [pl.BlockSpec((tm, d), lambda i: (i, 0)) for _ in range(n)]


def _gain_specs(d, n):
    return [pl.BlockSpec((1, d), lambda i: (0, 0)) for _ in range(n)]


def _mix_residual_fwd(x, mix, g_post_mix, g_pre_mlp, *, tm, jobs=()):
    s, d = x.shape

    def body(x_ref, mix_ref, g1_ref, g2_ref, x2_ref, h2_ref):
        mixv = mix_ref[...]
        x2 = x_ref[...] + mixv * _rstd(mixv) * g1_ref[...]
        x2_ref[...] = x2
        h2_ref[...] = (x2 * _rstd(x2) * g2_ref[...]).astype(h2_ref.dtype)

    return _call(
        body, name="mix_residual_fwd", grid=(s // tm,),
        in_specs=_row_specs(tm, d, 2) + _gain_specs(d, 2),
        out_specs=_row_specs(tm, d, 2),
        out_shape=[jax.ShapeDtypeStruct((s, d), jnp.float32), jax.ShapeDtypeStruct((s, d), jnp.bfloat16)],
        semantics=("parallel",), args=(x, mix, g_post_mix, g_pre_mlp), jobs=jobs)


def _loss_head(x2, mlp, target, g_post_mlp, *, tm, jobs=()):
    s, d = x2.shape

    def body(x2_ref, m_ref, t_ref, g_ref, dx3_ref, dm_ref, dg_ref, loss_ref):
        first = pl.program_id(0) == 0
        mv = m_ref[...]
        g = g_ref[...]
        diff = x2_ref[...] + mv * _rstd(mv) * g - t_ref[...]
        dx3 = diff * (1.0 / d)
        dx3_ref[...] = dx3
        dm, dgr = _rms_bwd_rows(mv, g, dx3)
        dm_ref[...] = dm.astype(dm_ref.dtype)
        _acc_rows(dg_ref, dgr, first)
        part = 0.5 * jnp.sum(jnp.mean(diff * diff, axis=-1, keepdims=True), axis=0, keepdims=True)
        _acc_rows(loss_ref, jnp.broadcast_to(part, (1, 128)), first)

    return _call(
        body, name="loss_head", grid=(s // tm,),
        in_specs=_row_specs(tm, d, 3) + _gain_specs(d, 1),
        out_specs=_row_specs(tm, d, 2) + _gain_specs(d, 1) + [pl.BlockSpec((1, 128), lambda i: (0, 0))],
        out_shape=[jax.ShapeDtypeStruct((s, d), jnp.float32), jax.ShapeDtypeStruct((s, d), jnp.bfloat16),
                   jax.ShapeDtypeStruct((1, d), jnp.float32), jax.ShapeDtypeStruct((1, 128), jnp.float32)],
        semantics=("arbitrary",), args=(x2, mlp, target, g_post_mlp), jobs=jobs)


def _mix_residual_bwd(dx3, dh2, x2, mix, g_pre_mlp, g_post_mix, *, tm, jobs=()):
    s, d = x2.shape

    def body(dx3_ref, dh2_ref, x2_ref, mix_ref, g2_ref, g1_ref, dx2_ref, dmix_ref, dg2_ref, dg1_ref):
        first = pl.program_id(0) == 0
        d_in, dgr2 = _rms_bwd_rows(x2_ref[...], g2_ref[...], dh2_ref[...])
        dx2 = dx3_ref[...] + d_in
        dx2_ref[...] = dx2
        dmix, dgr1 = _rms_bwd_rows(mix_ref[...], g1_ref[...], dx2)
        dmix_ref[...] = dmix.astype(dmix_ref.dtype)
        _acc_rows(dg2_ref, dgr2, first)
        _acc_rows(dg1_ref, dgr1, first)

    return _call(
        body, name="mix_residual_bwd", grid=(s // tm,),
        in_specs=_row_specs(tm, d, 4) + _gain_specs(d, 2),
        out_specs=_row_specs(tm, d, 2) + _gain_specs(d, 2),
        out_shape=[jax.ShapeDtypeStruct((s, d), jnp.float32), jax.ShapeDtypeStruct((s, d), jnp.bfloat16),
                   jax.ShapeDtypeStruct((1, d), jnp.float32), jax.ShapeDtypeStruct((1, d), jnp.float32)],
        semantics=("arbitrary",), args=(dx3, dh2, x2, mix, g_pre_mlp, g_post_mix), jobs=jobs)


def _input_bwd(dx2, dh1, x, g_pre_mix, *, tm, jobs=()):
    s, d = x.shape

    def body(dx2_ref, dh1_ref, x_ref, g_ref, dx_ref, dg_ref):
        d_in, dgr = _rms_bwd_rows(x_ref[...], g_ref[...], dh1_ref[...])
        dx_ref[...] = dx2_ref[...] + d_in
        _acc_rows(dg_ref, dgr, pl.program_id(0) == 0)

    return _call(
        body, name="input_bwd", grid=(s // tm,),
        in_specs=_row_specs(tm, d, 3) + _gain_specs(d, 1),
        out_specs=_row_specs(tm, d, 1) + _gain_specs(d, 1),
        out_shape=[jax.ShapeDtypeStruct((s, d), jnp.float32), jax.ShapeDtypeStruct((1, d), jnp.float32)],
        semantics=("arbitrary",), args=(dx2, dh1, x, g_pre_mix), jobs=jobs)


def _shift_rows(z, by):
    s = z.shape[0]
    rows = lax.broadcasted_iota(jnp.int32, z.shape, 0)
    rolled = pltpu.roll(z, by % s, axis=0)
    keep = rows >= by if by > 0 else rows < s + by
    return jnp.where(keep, rolled, 0.0)


def _conv_fwd(proj, conv_w, conv_out_g, jobs=()):
    s = proj.shape[0]
    groups = CONV_WIDTH // 128

    def body(u_ref, gb_ref, gc_ref, w_ref, g_ref, y_ref):
        z = gc_ref[...] * u_ref[...]
        w = w_ref[...]
        conv = w[0:1, :] * _shift_rows(z, 2) + w[1:2, :] * _shift_rows(z, 1) + w[2:3, :] * z
        y = gb_ref[...] * conv
        y_ref[...] = (y * _rstd(y) * g_ref[...]).astype(y_ref.dtype)

    col = lambda base: pl.BlockSpec((s, 128), lambda j: (0, base + j))
    return _call(
        body, name="conv_fwd", grid=(groups,),
        in_specs=[col(0), col(groups), col(2 * groups), pl.BlockSpec((3, 128), lambda j: (0, j)),
                  pl.BlockSpec((1, 128), lambda j: (0, j))],
        out_specs=[pl.BlockSpec((s, 128), lambda j: (0, j))],
        out_shape=[jax.ShapeDtypeStruct((s, CONV_WIDTH), jnp.bfloat16)],
        semantics=("parallel",), args=(proj, proj, proj, conv_w, conv_out_g), jobs=jobs)[0]


def _conv_bwd(proj, conv_w, conv_out_g, dycat, jobs=()):
    s = proj.shape[0]
    groups = CONV_WIDTH // 128

    def body(u_ref, gb_ref, gc_ref, w_ref, g_ref, dy_ref, du_ref, dgb_ref, dgc_ref, dw_ref, dg_ref):
        u, gb, gc = u_ref[...], gb_ref[...], gc_ref[...]
        w = w_ref[...]
        z = gc * u
        z1, z2 = _shift_rows(z, 1), _shift_rows(z, 2)
        conv = w[0:1, :] * z2 + w[1:2, :] * z1 + w[2:3, :] * z
        dyr, dgr = _rms_bwd_rows(gb * conv, g_ref[...], dy_ref[...])
        dg_ref[...] = jnp.sum(dgr, axis=0, keepdims=True)
        dgb_ref[...] = (dyr * conv).astype(dgb_ref.dtype)
        dconv = dyr * gb
        dw_ref[0:1, :] = jnp.sum(dconv * z2, axis=0, keepdims=True)
        dw_ref[1:2, :] = jnp.sum(dconv * z1, axis=0, keepdims=True)
        dw_ref[2:3, :] = jnp.sum(dconv * z, axis=0, keepdims=True)
        dz = w[2:3, :] * dconv + w[1:2, :] * _shift_rows(dconv, -1) + w[0:1, :] * _shift_rows(dconv, -2)
        dgc_ref[...] = (dz * u).astype(dgc_ref.dtype)
        du_ref[...] = (dz * gc).astype(du_ref.dtype)

    col = lambda base: pl.BlockSpec((s, 128), lambda j: (0, base + j))
    act = jax.ShapeDtypeStruct((s, CONV_WIDTH), jnp.bfloat16)
    return _call(
        body, name="conv_bwd", grid=(groups,),
        in_specs=[col(0), col(groups), col(2 * groups), pl.BlockSpec((3, 128), lambda j: (0, j)),
                  pl.BlockSpec((1, 128), lambda j: (0, j)), col(0)],
        out_specs=[col(0), col(0), col(0), pl.BlockSpec((3, 128), lambda j: (0, j)),
                   pl.BlockSpec((1, 128), lambda j: (0, j))],
        out_shape=[act, act, act, jax.ShapeDtypeStruct((3, CONV_WIDTH), jnp.float32),
                   jax.ShapeDtypeStruct((1, CONV_WIDTH), jnp.float32)],
        semantics=("parallel",), args=(proj, proj, proj, conv_w, conv_out_g, dycat), jobs=jobs)


def _rope_tables(s):
    pos = jnp.arange(s, dtype=jnp.float32)
    inv_freq = jnp.power(ROPE_THETA, -jnp.arange(0, QK_ROPE, 2, dtype=jnp.float32) / QK_ROPE)
    ang = pos[:, None] * inv_freq[None, :]
    cos, sin = jnp.cos(ang), jnp.sin(ang)
    zeros = jnp.zeros((s, 128 - QK_ROPE), jnp.float32)
    return (jnp.concatenate([cos, cos, zeros], axis=1), jnp.concatenate([-sin, sin, zeros], axis=1))


def _swap_halves(x):
    lane = lax.broadcasted_iota(jnp.int32, x.shape, 1)
    swapped = jnp.where(lane < QK_ROPE // 2, pltpu.roll(x, 128 - QK_ROPE // 2, axis=1),
                        pltpu.roll(x, QK_ROPE // 2, axis=1))
    return jnp.where(lane < QK_ROPE, swapped, 0.0)


def _rope(x, cos, sin):
    return x * cos + _swap_halves(x) * sin


def _rope_transposed(d, cos, sin):
    return d * cos + _swap_halves(d * sin)


def _qk_rope_fwd(q_pad, proj, cos, sin, *, tm, jobs=()):
    s = q_pad.shape[0]
    wq = N_HEADS * HEAD_PAD

    def body(q_ref, kr_ref, cos_ref, sin_ref, qo_ref, kro_ref):
        c, sn = cos_ref[...], sin_ref[...]
        for h in range(N_HEADS):
            lo = h * HEAD_PAD
            qo_ref[:, lo:lo + 128] = q_ref[:, lo:lo + 128].astype(qo_ref.dtype)
            qo_ref[:, lo + 128:lo + 256] = _rope(q_ref[:, lo + 128:lo + 256], c, sn).astype(qo_ref.dtype)
        kro_ref[...] = _rope(kr_ref[...], c, sn).astype(kro_ref.dtype)

    return _call(
        body, name="qk_rope_fwd", grid=(s // tm,),
        in_specs=[pl.BlockSpec((tm, wq), lambda i: (i, 0)), pl.BlockSpec((tm, 128), lambda i: (i, COL_KR // 128)),
                  pl.BlockSpec((tm, 128), lambda i: (i, 0)), pl.BlockSpec((tm, 128), lambda i: (i, 0))],
        out_specs=[pl.BlockSpec((tm, wq), lambda i: (i, 0)), pl.BlockSpec((tm, 128), lambda i: (i, 0))],
        out_shape=[jax.ShapeDtypeStruct((s, wq), jnp.bfloat16), jax.ShapeDtypeStruct((s, 128), jnp.bfloat16)],
        semantics=("parallel",), args=(q_pad, proj, cos, sin), jobs=jobs)


def _qk_rope_bwd(dq_pad, dk_pad, dv, cos, sin, *, tm, jobs=()):
    s = dq_pad.shape[0]
    wq = N_HEADS * HEAD_PAD

    def body(dq_ref, dk_ref, dv_ref, cos_ref, sin_ref, dqo_ref, dkv_ref, dkr_ref):
        c, sn = cos_ref[...], sin_ref[...]
        dkr = jnp.zeros((tm, 128), jnp.float32)
        for h in range(N_HEADS):
            lo = h * HEAD_PAD
            dqo_ref[:, lo:lo + 128] = dq_ref[:, lo:lo + 128].astype(dqo_ref.dtype)
            dqo_ref[:, lo + 128:lo + 256] = _rope_transposed(dq_ref[:, lo + 128:lo + 256], c, sn).astype(dqo_ref.dtype)
            dkv_ref[:, lo:lo + 128] = dk_ref[:, lo:lo + 128].astype(dkv_ref.dtype)
            dkv_ref[:, lo + 128:lo + 256] = dv_ref[:, h * V_DIM:(h + 1) * V_DIM].astype(dkv_ref.dtype)
            dkr = dkr + dk_ref[:, lo + 128:lo + 256]
        dkr_ref[...] = _rope_transposed(dkr, c, sn).astype(dkr_ref.dtype)

    return _call(
        body, name="qk_rope_bwd", grid=(s // tm,),
        in_specs=[pl.BlockSpec((tm, wq), lambda i: (i, 0)), pl.BlockSpec((tm, wq), lambda i: (i, 0)),
                  pl.BlockSpec((tm, N_HEADS * V_DIM), lambda i: (i, 0)),
                  pl.BlockSpec((tm, 128), lambda i: (i, 0)), pl.BlockSpec((tm, 128), lambda i: (i, 0))],
        out_specs=[pl.BlockSpec((tm, wq), lambda i: (i, 0)), pl.BlockSpec((tm, wq), lambda i: (i, 0)),
                   pl.BlockSpec((tm, 128), lambda i: (i, 0))],
        out_shape=[jax.ShapeDtypeStruct((s, wq), jnp.bfloat16), jax.ShapeDtypeStruct((s, wq), jnp.bfloat16),
                   jax.ShapeDtypeStruct((s, 128), jnp.bfloat16)],
        semantics=("parallel",), args=(dq_pad, dk_pad, dv, cos, sin), jobs=jobs)


def _visible(q0, k0, t):
    qpos = q0 + lax.broadcasted_iota(jnp.int32, (t, t), 0)
    kpos = k0 + lax.broadcasted_iota(jnp.int32, (t, t), 1)
    return lax.shift_right_logical(kpos, CHUNK_SHIFT) <= lax.shift_right_logical(qpos, CHUNK_SHIFT)


def _attn_fwd(q, kv, kr, attn_out_g, *, t, jobs=()):
    s = q.shape[0]
    nq = s // t

    def body(q_ref, kn_ref, v_ref, kr_ref, g_ref, o_ref, lse_ref, y_ref, kcat_ref):
        i = pl.program_id(1)

        @pl.when(i == 0)
        def _():
            kcat_ref[:, 0:128] = kn_ref[...]
            kcat_ref[:, 128:256] = kr_ref[...]

        qv = q_ref[...]

        def step(j, carry, diagonal):
            m, l, acc = carry
            k = kcat_ref[pl.ds(pl.multiple_of(j * t, t), t), :]
            v = v_ref[pl.ds(pl.multiple_of(j * t, t), t), :]
            sc = lax.dot_general(qv, k, _NT, preferred_element_type=jnp.float32) * ATTN_SCALE
            if diagonal:
                sc = jnp.where(_visible(0, 0, t), sc, NEG_INF)
            m_new = jnp.maximum(m, jnp.max(sc, axis=-1, keepdims=True))
            p = jnp.exp(sc - m_new)
            alpha = jnp.exp(m - m_new)
            l = alpha * l + jnp.sum(p, axis=-1, keepdims=True)
            acc = alpha * acc + lax.dot_general(p.astype(jnp.bfloat16), v, _NN, preferred_element_type=jnp.float32)
            return m_new, l, acc

        init = (jnp.full((t, 1), NEG_INF, jnp.float32), jnp.zeros((t, 1), jnp.float32),
                jnp.zeros((t, V_DIM), jnp.float32))
        before = lax.fori_loop(0, i, functools.partial(step, diagonal=False), init)
        m, l, acc = step(i, before, True)
        o = acc / l
        o_ref[...] = o
        lse_ref[...] = jnp.broadcast_to(m + jnp.log(l), (t, 128))
        y_ref[...] = (o * _rstd(o) * g_ref[...]).astype(y_ref.dtype)

    head_rows = lambda w, f: pl.BlockSpec((s, w), lambda h, i: (0, f(h)))
    blk = pl.BlockSpec((t, 128), lambda h, i: (i, h))
    full = jax.ShapeDtypeStruct((s, N_HEADS * V_DIM), jnp.float32)
    return _call(
        body, name="attn_fwd", grid=(N_HEADS, nq),
        in_specs=[pl.BlockSpec((t, HEAD_PAD), lambda h, i: (i, h)), head_rows(128, lambda h: 2 * h),
                  head_rows(128, lambda h: 2 * h + 1), head_rows(128, lambda h: 0),
                  pl.BlockSpec((1, 128), lambda h, i: (0, h))],
        out_specs=[blk, blk, blk],
        out_shape=[full, full, jax.ShapeDtypeStruct((s, N_HEADS * V_DIM), jnp.bfloat16)],
        scratch_shapes=[pltpu.VMEM((s, HEAD_PAD), jnp.bfloat16)],
        semantics=("arbitrary", "arbitrary"), args=(q, kv, kv, kr, attn_out_g), jobs=jobs)


def _attn_norm_bwd(o, attn_out_g, dycat, jobs=()):
    s = o.shape[0]

    def body(o_ref, g_ref, dy_ref, do_ref, delta_ref, dg_ref):
        ov = o_ref[...]
        do, dgr = _rms_bwd_rows(ov, g_ref[...], dy_ref[...])
        do_ref[...] = do.astype(do_ref.dtype)
        delta_ref[...] = jnp.broadcast_to(jnp.sum(do * ov, axis=-1, keepdims=True), (s, 128))
        dg_ref[...] = jnp.sum(dgr, axis=0, keepdims=True)

    col = lambda base: pl.BlockSpec((s, 128), lambda h: (0, base + h))
    return _call(
        body, name="attn_norm_bwd", grid=(N_HEADS,),
        in_specs=[col(0), pl.BlockSpec((1, 128), lambda h: (0, h)), col(CONV_WIDTH // 128)],
        out_specs=[col(0), col(0), pl.BlockSpec((1, 128), lambda h: (0, h))],
        out_shape=[jax.ShapeDtypeStruct((s, N_HEADS * V_DIM), jnp.bfloat16),
                   jax.ShapeDtypeStruct((s, N_HEADS * V_DIM), jnp.float32),
                   jax.ShapeDtypeStruct((1, N_HEADS * V_DIM), jnp.float32)],
        semantics=("parallel",), args=(o, attn_out_g, dycat), jobs=jobs)


def _attn_bwd(q, kv, kr, do, lse, delta, *, t, jobs=()):
    s = q.shape[0]
    nq = s // t

    def body(q_ref, kn_ref, v_ref, kr_ref, do_ref, lse_ref, delta_ref, dq_ref, dk_ref, dv_ref, kcat_ref):
        kcat_ref[:, 0:128] = kn_ref[...]
        kcat_ref[:, 128:256] = kr_ref[...]
        dq_ref[...] = jnp.zeros_like(dq_ref)
        dk_ref[...] = jnp.zeros_like(dk_ref)
        dv_ref[...] = jnp.zeros_like(dv_ref)

        def kv_step(j, _):
            krows = pl.ds(pl.multiple_of(j * t, t), t)
            k = kcat_ref[krows, :]
            v = v_ref[krows, :]

            def q_step(i, _, diagonal):
                qrows = pl.ds(pl.multiple_of(i * t, t), t)
                qv = q_ref[qrows, :]
                dov = do_ref[qrows, :]
                sc = lax.dot_general(qv, k, _NT, preferred_element_type=jnp.float32) * ATTN_SCALE
                if diagonal:
                    sc = jnp.where(_visible(0, 0, t), sc, NEG_INF)
                p = jnp.exp(sc - lse_ref[qrows, :][:, 0:1])
                dp = lax.dot_general(dov, v, _NT, preferred_element_type=jnp.float32)
                ds = (p * (dp - delta_ref[qrows, :][:, 0:1]) * ATTN_SCALE).astype(jnp.bfloat16)
                dv_ref[krows, :] += lax.dot_general(p.astype(jnp.bfloat16), dov, _TN,
                                                    preferred_element_type=jnp.float32)
                dk_ref[krows, :] += lax.dot_general(ds, qv, _TN, preferred_element_type=jnp.float32)
                dq_ref[qrows, :] += lax.dot_general(ds, k, _NN, preferred_element_type=jnp.float32)
                return 0

            q_step(j, 0, True)
            lax.fori_loop(j + 1, nq, functools.partial(q_step, diagonal=False), 0)
            return 0

        lax.fori_loop(0, nq, kv_step, 0)

    col = lambda w, f: pl.BlockSpec((s, w), lambda h: (0, f(h)))
    return _call(
        body, name="attn_bwd", grid=(N_HEADS,),
        in_specs=[col(HEAD_PAD, lambda h: h), col(128, lambda h: 2 * h), col(128, lambda h: 2 * h + 1),
                  col(128, lambda h: 0), col(128, lambda h: h), col(128, lambda h: h), col(128, lambda h: h)],
        out_specs=[col(HEAD_PAD, lambda h: h), col(HEAD_PAD, lambda h: h), col(128, lambda h: h)],
        out_shape=[jax.ShapeDtypeStruct((s, N_HEADS * HEAD_PAD), jnp.float32),
                   jax.ShapeDtypeStruct((s, N_HEADS * HEAD_PAD), jnp.float32),
                   jax.ShapeDtypeStruct((s, N_HEADS * V_DIM), jnp.float32)],
        scratch_shapes=[pltpu.VMEM((s, HEAD_PAD), jnp.bfloat16)],
        semantics=("parallel",), args=(q, kv, kv, kr, do, lse, delta), jobs=jobs)


def _row_tile(rows, cap=256):
    for cand in (512, 256, 128, 64, 32, 16, 8):
        if cand <= cap and rows % cand == 0:
            return cand
    return rows


def _cast_into_slot(w, pos, *, name, jobs=()):
    r, c = w.shape
    tr = _row_tile(r)

    def body(pos_ref, w_ref, o_ref):
        o_ref[...] = w_ref[...].astype(o_ref.dtype)

    return _call(
        body, name=name, grid=(r // tr,), prefetch=pos,
        in_specs=[pl.BlockSpec((tr, c), lambda i, p: (i, 0))],
        out_specs=[pl.BlockSpec((None, tr, c), lambda i, p: (p[1], i, 0))],
        out_shape=[jax.ShapeDtypeStruct((4, r, c), jnp.bfloat16)],
        semantics=("parallel",), args=(w,), jobs=jobs)[0]


def _cast_many_into_slots(ws, pos, *, name, jobs=()):
    steps = 8
    assert all(w.shape[0] % (16 * steps) == 0 for w in ws), [w.shape for w in ws]

    def body(pos_ref, *refs):
        for w_ref, o_ref in zip(refs[:len(ws)], refs[len(ws):]):
            o_ref[...] = w_ref[...].astype(o_ref.dtype)

    return _call(
        body, name=name, grid=(steps,), prefetch=pos,
        in_specs=[pl.BlockSpec((w.shape[0] // steps, w.shape[1]), lambda i, p: (i, 0)) for w in ws],
        out_specs=[pl.BlockSpec((None, w.shape[0] // steps, w.shape[1]), lambda i, p: (p[1], i, 0)) for w in ws],
        out_shape=[jax.ShapeDtypeStruct((4, *w.shape), jnp.bfloat16) for w in ws],
        semantics=("parallel",), args=tuple(ws), jobs=jobs)


def _pair_add(g, theirs, pos, *, name, jobs=()):
    n, h, c = theirs.shape
    tr = _row_tile(h, cap=512)
    nb = h // tr

    def body(pos_ref, a_ref, b_ref, o_ref):
        o_ref[...] = (a_ref[...].astype(jnp.float32) + b_ref[...].astype(jnp.float32)).astype(o_ref.dtype)

    spec = pl.BlockSpec((None, tr, c), lambda j, i, p: (j, i, 0))
    return _call(
        body, name=name, grid=(n, nb), prefetch=pos,
        in_specs=[pl.BlockSpec((None, tr, c), lambda j, i, p: (j, i + p[0] * nb, 0)), spec],
        out_specs=[spec], out_shape=[jax.ShapeDtypeStruct(theirs.shape, jnp.bfloat16)],
        semantics=("parallel", "parallel"), args=(g, theirs), jobs=jobs)[0]


def _fold_diag(pair_sum, via, pos, *, name):
    n, h, c = pair_sum.shape
    tr = _row_tile(h // 2, cap=512)
    nb = h // tr

    def body(pos_ref, p_ref, via_ref, o_ref):
        j, i = pl.program_id(0), pl.program_id(1)
        mine = p_ref[...].astype(jnp.float32)
        add = (j == 0) == (i >= nb // 2)
        o_ref[...] = jnp.where(add, mine + via_ref[...].astype(jnp.float32), mine).astype(o_ref.dtype)

    return _call(
        body, name=name, grid=(2, nb), prefetch=pos,
        in_specs=[pl.BlockSpec((None, tr, c), lambda j, i, p: (jnp.bitwise_xor(p[1], 2 - j), i, 0)),
                  pl.BlockSpec((tr, c), lambda j, i, p: (i, 0))],
        out_specs=[pl.BlockSpec((None, tr, c), lambda j, i, p: (j, i, 0))],
        out_shape=[jax.ShapeDtypeStruct((2, h, c), jnp.bfloat16)],
        semantics=("parallel", "parallel"), args=(pair_sum, via))[0]


def _chip_sum(by_source, pair_sum, pos, *, name):
    n, h, c = by_source.shape
    tr = _row_tile(h, cap=512)
    nb = h // tr

    def body(pos_ref, own_ref, px_ref, py_ref, o_ref):
        f = lambda ref: ref[...].astype(jnp.float32)
        o_ref[...] = (f(own_ref) + f(px_ref)) + f(py_ref)

    slot = lambda flip: pl.BlockSpec((None, tr, c), lambda i, p: (jnp.bitwise_xor(p[1], flip), i, 0))
    return pl.pallas_call(
        body, name=name, out_shape=jax.ShapeDtypeStruct((2 * h, c), jnp.float32),
        grid_spec=pltpu.PrefetchScalarGridSpec(
            num_scalar_prefetch=1, grid=(nb,),
            in_specs=[slot(0), slot(2), slot(1)],
            out_specs=pl.BlockSpec((tr, c), lambda i, p: (i + p[0] * nb, 0))),
        compiler_params=_params(("parallel",)),
    )(pos, pair_sum, by_source, by_source)


def _adamw(w, g, m, v, *, name, jobs=()):
    r, c = w.shape
    tr = _row_tile(r)

    def body(w_ref, g_ref, m_ref, v_ref, d_ref, mo_ref, vo_ref, go_ref):
        gv = g_ref[...]
        go_ref[...] = gv
        mn = ADAM_B1 * m_ref[...] + (1.0 - ADAM_B1) * gv
        vn = ADAM_B2 * v_ref[...] + (1.0 - ADAM_B2) * (gv * gv)
        m_hat = mn / (1.0 - ADAM_B1 ** ADAM_STEP)
        v_hat = vn / (1.0 - ADAM_B2 ** ADAM_STEP)
        d_ref[...] = -ADAM_LR * (m_hat / (jnp.sqrt(v_hat) + ADAM_EPS) + ADAM_WD * w_ref[...])
        mo_ref[...] = mn
        vo_ref[...] = vn

    spec = pl.BlockSpec((tr, c), lambda i: (i, 0))
    out = jax.ShapeDtypeStruct((r, c), jnp.float32)
    return _call(body, name=name, grid=(r // tr,), in_specs=[spec] * 4, out_specs=[spec] * 4, out_shape=[out] * 4,
                 semantics=("parallel",), args=(w, g, m, v), jobs=jobs)


def _all_reduce_small(block):
    r, c = block.shape

    def body(src_ref, out_ref, stage_ref, send_sems, recv_sems):
        x, y, cc = _position()
        me = 4 * x + 2 * y + cc
        stage_ref[me] = src_ref[...]
        flip = lambda v, on: 1 - v if on else v
        peers = [(flip(x, dx), flip(y, dy), flip(cc, dc)) for dx in (0, 1) for dy in (0, 1) for dc in (0, 1)][1:]
        copies = [pltpu.make_async_remote_copy(
            src_ref=stage_ref.at[me], dst_ref=stage_ref.at[me],
            send_sem=send_sems.at[k], recv_sem=recv_sems.at[k], device_id=peer, device_id_type=MESH)
            for k, peer in enumerate(peers)]
        for cp in copies:
            cp.start()
        for k, (px, py, pc) in enumerate(peers):
            them = 4 * px + 2 * py + pc
            pltpu.make_async_remote_copy(
                src_ref=stage_ref.at[them], dst_ref=stage_ref.at[them],
                send_sem=send_sems.at[k], recv_sem=recv_sems.at[k], device_id=(px, py, pc),
                device_id_type=MESH).wait_recv()
        for cp in copies:
            cp.wait_send()
        total = stage_ref[0]
        for d in range(1, 8):
            total = total + stage_ref[d]
        out_ref[...] = total

    return pl.pallas_call(
        body, name="all_reduce_small",
        in_specs=[pl.BlockSpec(memory_space=pltpu.VMEM)], out_specs=pl.BlockSpec(memory_space=pltpu.VMEM),
        out_shape=jax.ShapeDtypeStruct((r, c), jnp.float32),
        scratch_shapes=[pltpu.VMEM((8, r, c), jnp.float32), pltpu.SemaphoreType.DMA((7,)),
                        pltpu.SemaphoreType.DMA((7,))],
        compiler_params=pltpu.CompilerParams(has_side_effects=True),
    )(block)


def _cols_from_shards(g):
    n, r, c = g.shape
    return jnp.transpose(g, (1, 0, 2)).reshape(r, n * c)


def _cols_to_shards(w, n=4):
    r, c = w.shape
    return jnp.transpose(w.reshape(r, n, c // n), (1, 0, 2))


def _pad_w_in(g):
    _, d, c = g.shape
    cut = COL_KR - 3 * c
    zeros = jnp.zeros((d, COL_CKV - COL_KR - QK_ROPE), g.dtype)
    return jnp.concatenate([g[0], g[1], g[2], g[3][:, :cut], g[3][:, cut + KV_RANK:], zeros,
                            g[3][:, cut:cut + KV_RANK]], axis=1)


def _unpad_w_in(padded):
    c = IN_WIDTH // 4
    last = jnp.concatenate([padded[:, 3 * c:COL_KR], padded[:, COL_CKV:COL_CKV + KV_RANK],
                            padded[:, COL_KR:COL_KR + QK_ROPE]], axis=1)
    return jnp.stack([padded[:, 0:c], padded[:, c:2 * c], padded[:, 2 * c:3 * c], last])


def _pad_w_uq(full):
    r = full.shape[0]
    per_head = full.reshape(r, N_HEADS, QK_NOPE + QK_ROPE)
    return jnp.pad(per_head, ((0, 0), (0, 0), (0, HEAD_PAD - QK_NOPE - QK_ROPE))).reshape(r, N_HEADS * HEAD_PAD)


def _unpad_w_uq(padded):
    r = padded.shape[0]
    return padded.reshape(r, N_HEADS, HEAD_PAD)[:, :, :QK_NOPE + QK_ROPE].reshape(r, N_HEADS * (QK_NOPE + QK_ROPE))


SMALL_ROWS = 16


def _pack_small(d, pre_mix, post_mix, pre_mlp, post_mlp, conv_out, attn_out, q_norm, kv_norm, conv_w):
    row = lambda *parts: jnp.pad(jnp.concatenate(parts, axis=1), ((0, 0), (0, d - sum(p.shape[1] for p in parts))))
    rows = [row(pre_mix), row(post_mix), row(pre_mlp), row(post_mlp), row(conv_out, attn_out), row(q_norm, kv_norm),
            row(conv_w[0:1]), row(conv_w[1:2]), row(conv_w[2:3])]
    return jnp.pad(jnp.concatenate(rows, axis=0), ((0, SMALL_ROWS - len(rows)), (0, 0)))


def _unpack_small(p, chip):
    cw = CONV_WIDTH // 4
    conv_w = lax.dynamic_slice(p[6:9, :CONV_WIDTH], (0, chip * cw), (3, cw))
    return dict(pre_mix_g=p[0:1], post_mix_g=p[1:2], pre_mlp_g=p[2:3], post_mlp_g=p[3:4],
                conv_out_g=p[4:5, :CONV_WIDTH], attn_out_g=p[4:5, CONV_WIDTH:2 * CONV_WIDTH],
                q_norm_g=p[5:6, :Q_RANK], kv_norm_g=p[5:6, Q_RANK:Q_RANK + KV_RANK], conv_w=conv_w[None])


def kernel(x, pre_mix_g, w_in, conv_w, q_norm_g, w_uq, kv_norm_g, w_ukv, conv_out_g, attn_out_g, w_o, post_mix_g, pre_mlp_g, w_up, w_down, post_mlp_g, loss_target, m_pre_mix_g, m_w_in, m_conv_w, m_q_norm_g, m_w_uq, m_kv_norm_g, m_w_ukv, m_conv_out_g, m_attn_out_g, m_w_o, m_post_mix_g, m_pre_mlp_g, m_w_up, m_w_down, m_post_mlp_g, v_pre_mix_g, v_w_in, v_conv_w, v_q_norm_g, v_w_uq, v_kv_norm_g, v_w_ukv, v_conv_out_g, v_attn_out_g, v_w_o, v_post_mix_g, v_pre_mlp_g, v_w_up, v_w_down, v_post_mlp_g):
    bf16 = jnp.bfloat16
    s, d = x.shape[1], x.shape[2]
    d_ff = 4 * d
    chip = 2 * lax.axis_index("x") + lax.axis_index("y")
    xs = x.reshape(s, d)
    target = loss_target.reshape(s, d)
    tm = min(256, s)
    t_attn = min(512, s)
    mt = min(1024, s)
    kt = min(2048, s)

    big = dict(w_in=w_in[0], w_uq=w_uq[0], w_ukv=w_ukv[0], w_o=w_o[0], w_up=w_up[0], w_down=w_down[0])
    names = list(big)
    pos = jnp.stack([lax.axis_index("c"), chip]).astype(jnp.int32)
    wb = {}
    half = {k: big[k].shape[0] // 2 for k in names}

    def rows(k, a, b):
        lo, n = a * half[k] // 64, (b - a) * half[k] // 64
        assert lo % 16 == 0 and n % 16 == 0 and n > 0, (k, a, b)
        return lo, n

    ici = lambda k, a=0, b=64: _GatherIci(wb[k], *rows(k, a, b))
    fwd = lambda k, a=0, b=64: _GatherForward(wb[k], *rows(k, a, b))
    near = lambda k, a=0, b=64: _GatherD2d(wb[k], *rows(k, a, b), slots=(0, 1))
    far = lambda k, a=0, b=64: _GatherD2d(wb[k], *rows(k, a, b), slots=(2,))
    tap_rows = 64
    half["conv_w"] = tap_rows // 2
    wb["conv_w"] = _Buf(lax.dynamic_update_slice(
        jnp.zeros((4, tap_rows, CONV_WIDTH // 4), jnp.float32),
        jnp.pad(conv_w[0], ((0, tap_rows - conv_w.shape[1]), (0, 0)))[None], (chip, 0, 0)))
    wb["w_in"] = _Buf(_cast_into_slot(big["w_in"], pos, name="cast_w_in"))
    rest = [k for k in names if k != "w_in"]
    for k, slot in zip(rest, _cast_many_into_slots([big[k] for k in rest], pos, name="cast_rest",
                                                   jobs=[ici("w_in"), ici("conv_w")])):
        wb[k] = _Buf(slot)
    h1 = _rms_fwd(xs, pre_mix_g, width=d, col=0, tm=tm, name="rms_pre_mix",
                  jobs=[fwd("w_in"), near("w_in"), fwd("conv_w"), near("conv_w")])
    _comm("gather_w_in", [[far("w_in"), far("conv_w")]])
    win = _pad_w_in(wb["w_in"].arr)
    conv_w_full = jnp.transpose(wb["conv_w"].arr[:, :conv_w.shape[1], :], (1, 0, 2)).reshape(-1, CONV_WIDTH)
    spread = lambda a: lax.dynamic_update_slice(jnp.zeros((3, CONV_WIDTH), jnp.float32), a[0],
                                                (0, chip * (CONV_WIDTH // 4)))
    ff4 = d_ff // 4

    proj = _matmul(h1, win, dims=_NN, mnk=(s, IN_PAD, d), tiles=(mt, IN_PAD // 3, d), name="mm_proj",
                   jobs=[ici("w_uq"), ici("w_ukv"), ici("w_o")])
    y_conv = _conv_fwd(proj, conv_w_full, conv_out_g,
                       jobs=[fwd("w_uq"), fwd("w_ukv"), fwd("w_o"), near("w_uq"), near("w_ukv"), near("w_o")])
    cqn = _rms_fwd(proj, q_norm_g, width=Q_RANK, col=COL_CQ // Q_RANK, tm=mt, name="rms_q",
                   jobs=[far("w_uq"), far("w_ukv"), far("w_o"), ici("w_up", 0, 4)])
    wuq = _pad_w_uq(_cols_from_shards(wb["w_uq"].arr))
    wukv = _cols_from_shards(wb["w_ukv"].arr)
    wo = wb["w_o"].arr.reshape(-1, d)
    ckvn = _rms_fwd(proj, kv_norm_g, width=KV_RANK, col=COL_CKV // KV_RANK, tm=mt, name="rms_kv",
                    jobs=[ici("w_up", 4, 10), fwd("w_up", 0, 4)])
    q_pad = _matmul(cqn, wuq, dims=_NN, mnk=(s, N_HEADS * HEAD_PAD, Q_RANK), tiles=(mt, 1024, Q_RANK), name="mm_q",
                    jobs=[ici("w_up", 10, 18), fwd("w_up", 4, 10), near("w_up", 0, 4)])
    kv = _matmul(ckvn, wukv, dims=_NN, mnk=(s, N_HEADS * HEAD_PAD, KV_RANK), tiles=(mt, 1024, KV_RANK),
                 name="mm_kv", out_dtype=bf16,
                 jobs=[ici("w_up", 18, 24), fwd("w_up", 10, 18), near("w_up", 4, 10), far("w_up", 0, 4)])
    cos, sin = _rope_tables(s)
    q_rot, kr_rot = _qk_rope_fwd(
        q_pad, proj, cos, sin, tm=tm,
        jobs=[ici("w_up", 24, 32), fwd("w_up", 18, 24), near("w_up", 10, 18), far("w_up", 4, 10)])
    o, lse, y_attn = _attn_fwd(
        q_rot, kv, kr_rot, attn_out_g, t=t_attn,
        jobs=[ici("w_up", 32, 64), ici("w_down", 0, 8), fwd("w_up", 24, 32), near("w_up", 18, 24), far("w_up", 10, 18)])
    ycat = jnp.concatenate([y_conv, y_attn], axis=1)
    mix = _matmul(ycat, wo, dims=_NN, mnk=(s, d, 2 * CONV_WIDTH), tiles=(mt, 1024, 2 * CONV_WIDTH), name="mm_out",
                  jobs=[fwd("w_up", 32, 64), near("w_up", 24, 32), far("w_up", 18, 24)])
    x2, h2 = _mix_residual_fwd(
        xs, mix, post_mix_g, pre_mlp_g, tm=tm,
        jobs=[ici("w_down", 8, 20), fwd("w_down", 0, 8), near("w_up", 32, 64), far("w_up", 24, 64)])
    wup = wb["w_up"].arr

    def up_epilogue(acc, extra_refs, out_refs):
        r = jnp.maximum(acc, 0.0)
        out_refs[0][...] = acc.astype(bf16)
        out_refs[1][...] = (r * r).astype(bf16)

    n_ff = ff4 // 1024
    act = jax.ShapeDtypeStruct((s, d_ff), bf16)
    up, act_sq = _matmul(
        h2, wup, dims=_NN, mnk=(s, d_ff, d), tiles=(mt, 1024, d), name="mm_up",
        b_spec=pl.BlockSpec((None, d, 1024), lambda i, j, l: (j // n_ff, l, j % n_ff)),
        out_shape=(act, act), o_spec=(pl.BlockSpec((mt, 1024), lambda i, j, l: (i, j)),) * 2, epilogue=up_epilogue,
        jobs=[ici("w_down", 20, 64), fwd("w_down", 8, 20), near("w_down", 0, 8)])
    _comm("gather_w_down_tail", [[fwd("w_down", 20, 64), near("w_down", 8, 64), far("w_down", 0, 20)],
                                 [far("w_down", 20, 64)]])
    wdown = wb["w_down"].arr.reshape(d_ff, d)
    mlp = _matmul(act_sq, wdown, dims=_NN, mnk=(s, d, d_ff), tiles=(min(512, s), 512, d_ff), name="mm_down")
    dx3, dmlp, dg_post_mlp, loss_part = _loss_head(x2, mlp, target, post_mlp_g, tm=tm)

    def dup_epilogue(acc, extra_refs, out_refs):
        out_refs[0][...] = (acc * (2.0 * jnp.maximum(extra_refs[0][...].astype(jnp.float32), 0.0))).astype(bf16)

    grads, theirs, pair_sums, via, folded, by_source, whole = {}, {}, {}, {}, {}, {}, {}

    def exchange(k, g):
        grads[k] = g
        theirs[k] = _Buf(jax.ShapeDtypeStruct((4, g.shape[1] // 2, g.shape[2]), bf16))
        return _PairExchange(g, theirs[k])

    def pair_sum(k, jobs=()):
        pair_sums[k] = _pair_add(grads[k], theirs[k].arr, pos, name="pair_add_" + k, jobs=jobs)
        via[k] = _Buf(jax.ShapeDtypeStruct(pair_sums[k].shape[1:], bf16))
        by_source[k] = _Buf(jax.ShapeDtypeStruct(pair_sums[k].shape, bf16))

    def diag(k, a=0, b=32):
        lo, n = a * half[k] // 64, (b - a) * half[k] // 64
        assert lo % 16 == 0 and n % 16 == 0 and n > 0, (k, a, b)
        return _ScatterDiag(pair_sums[k], via[k], lo, n)

    def fold(k):
        folded[k] = _fold_diag(pair_sums[k], via[k].arr, pos, name="fold_" + k)

    scatter = lambda k, a=0, b=64: _ScatterNear(folded[k], by_source[k], *rows(k, a, b))

    def share(k):
        whole[k] = _Buf(_chip_sum(by_source[k].arr, pair_sums[k], pos, name="chip_sum_" + k))
        return _PairShare(whole[k])

    g_wdown = _matmul(act_sq, dmlp, dims=_TN, mnk=(d_ff, d, s), tiles=(1024, 1024, kt), name="mm_gw_down",
                      out_dtype=bf16).reshape(4, ff4, d)
    dup = _matmul(dmlp, wdown, dims=_NT, mnk=(s, d_ff, d), tiles=(mt, 1024, d), name="mm_dact",
                  out_dtype=bf16, epilogue=dup_epilogue, extra=(up,),
                  extra_specs=(pl.BlockSpec((mt, 1024), lambda i, j, l: (i, j)),),
                  jobs=[exchange("w_down", g_wdown)])
    pair_sum("w_down")
    g_wup = _matmul(h2, dup, dims=_TN, mnk=(d, d_ff, s), tiles=(1024, 1024, kt), name="mm_gw_up",
                    out_shape=jax.ShapeDtypeStruct((4, d, ff4), bf16),
                    o_spec=pl.BlockSpec((None, 1024, 1024), lambda i, j, l: (j // n_ff, i, j % n_ff)),
                    jobs=[diag("w_down")])
    fold("w_down")
    dh2 = _matmul(dup, wup, dims=_NT, mnk=(s, d, d_ff), tiles=(mt, 1024, ff4), name="mm_dh2",
                  b_spec=pl.BlockSpec((None, 1024, ff4), lambda i, j, l: (l, j, 0)),
                  jobs=[exchange("w_up", g_wup), scatter("w_down", 0, 48)])
    pair_sum("w_up", jobs=[scatter("w_down", 48, 64)])
    dx2, dmix, dg_pre_mlp, dg_post_mix = _mix_residual_bwd(
        dx3, dh2, x2, mix, pre_mlp_g, post_mix_g, tm=tm, jobs=[diag("w_up", 0, 20)])

    dycat = _matmul(dmix, wo, dims=_NT, mnk=(s, 2 * CONV_WIDTH, d), tiles=(mt, 1024, d), name="mm_dycat",
                    jobs=[diag("w_up", 20, 32)])
    fold("w_up")
    g_wo = _matmul(ycat, dmix, dims=_TN, mnk=(2 * CONV_WIDTH, d, s), tiles=(1024, 1024, kt),
                   name="mm_gw_out", out_dtype=bf16, jobs=[scatter("w_up", 0, 12)]).reshape(4, CONV_WIDTH // 2, d)
    du, dgb, dgc, dg_conv_w, dg_conv_out = _conv_bwd(proj, conv_w_full, conv_out_g, dycat,
                                                     jobs=[exchange("w_o", g_wo), scatter("w_up", 12, 24)])
    pair_sum("w_o")
    do, delta, dg_attn_out = _attn_norm_bwd(o, attn_out_g, dycat)
    dq_pad, dk_pad, dv = _attn_bwd(q_rot, kv, kr_rot, do, lse, delta, t=t_attn,
                                   jobs=[scatter("w_up", 24, 64), diag("w_o"), share("w_down")])
    fold("w_o")
    dq_raw, dkv, dkr = _qk_rope_bwd(dq_pad, dk_pad, dv, cos, sin, tm=tm, jobs=[scatter("w_o")])
    wq_cols = N_HEADS * HEAD_PAD
    g_wuq = _matmul(cqn, dq_raw, dims=_TN, mnk=(Q_RANK, wq_cols, s), tiles=(Q_RANK, 1024, kt),
                    name="mm_gw_uq", out_dtype=bf16)
    dcqn = _matmul(dq_raw, wuq, dims=_NT, mnk=(s, Q_RANK, wq_cols), tiles=(mt, Q_RANK, wq_cols), name="mm_dcq")
    g_wukv = _matmul(ckvn, dkv, dims=_TN, mnk=(KV_RANK, wq_cols, s), tiles=(KV_RANK, 1024, kt),
                     name="mm_gw_ukv", out_dtype=bf16)
    dckvn = _matmul(dkv, wukv, dims=_NT, mnk=(s, KV_RANK, wq_cols), tiles=(mt, KV_RANK, wq_cols), name="mm_dckv",
                    jobs=[exchange("w_uq", _cols_to_shards(_unpad_w_uq(g_wuq))),
                          exchange("w_ukv", _cols_to_shards(g_wukv))])
    pair_sum("w_uq")
    pair_sum("w_ukv")
    dcq, dg_q_norm = _rms_bwd(proj, q_norm_g, dcqn, width=Q_RANK, col=COL_CQ // Q_RANK, tm=mt, name="rms_q_bwd",
                              jobs=[diag("w_uq"), diag("w_ukv")])
    fold("w_uq")
    fold("w_ukv")
    dckv, dg_kv_norm = _rms_bwd(proj, kv_norm_g, dckvn, width=KV_RANK, col=COL_CKV // KV_RANK, tm=mt,
                                name="rms_kv_bwd")
    dproj = jnp.concatenate([du, dgb, dgc, dcq, dkr, jnp.zeros((s, COL_CKV - COL_KR - 128), bf16), dckv], axis=1)
    g_win = _matmul(h1, dproj, dims=_TN, mnk=(d, IN_PAD, s), tiles=(1024, IN_PAD // 3, kt), name="mm_gw_in",
                    out_dtype=bf16, jobs=[scatter("w_uq"), scatter("w_ukv"), share("w_up"), share("w_o")])
    _comm("pair_exchange_w_in", [[exchange("w_in", _unpad_w_in(g_win))]])
    pair_sum("w_in")
    _comm("scatter_diag_w_in", [[diag("w_in")]])
    fold("w_in")
    dh1 = _matmul(dproj, win, dims=_NT, mnk=(s, d, IN_PAD), tiles=(mt, 1024, IN_PAD // 2), name="mm_dh1",
                  jobs=[scatter("w_in"), share("w_uq"), share("w_ukv")])
    grad_x, dg_pre_mix = _input_bwd(dx2, dh1, xs, pre_mix_g, tm=tm)
    _comm("pair_share_w_in", [[share("w_in")]])

    moments = dict(w_in=(m_w_in, v_w_in), w_uq=(m_w_uq, v_w_uq), w_ukv=(m_w_ukv, v_w_ukv), w_o=(m_w_o, v_w_o),
                   w_up=(m_w_up, v_w_up), w_down=(m_w_down, v_w_down))
    gw, dw, nm, nv = {}, {}, {}, {}
    for k in names:
        delta_k, nm_k, nv_k, g = _adamw(big[k], whole[k].arr, moments[k][0][0], moments[k][1][0], name="adamw_" + k)
        gw[k], dw[k], nm[k], nv[k] = g[None], delta_k[None], nm_k[None], nv_k[None]

    small_g = _all_reduce_small(_pack_small(d, dg_pre_mix, dg_post_mix, dg_pre_mlp, dg_post_mlp, dg_conv_out,
                                            dg_attn_out, dg_q_norm, dg_kv_norm, dg_conv_w
                                            ).at[SMALL_ROWS - 1, :128].set(loss_part[0]))
    loss = small_g[SMALL_ROWS - 1, 0]
    pack_w = lambda cw, pre_mix, post_mix, pre_mlp, post_mlp, conv_out, attn_out, q_norm, kv_norm: _pack_small(
        d, pre_mix, post_mix, pre_mlp, post_mlp, conv_out, attn_out, q_norm, kv_norm, cw)
    small_w = pack_w(conv_w_full, pre_mix_g, post_mix_g, pre_mlp_g, post_mlp_g, conv_out_g, attn_out_g, q_norm_g,
                     kv_norm_g)
    small_m = pack_w(spread(m_conv_w), m_pre_mix_g, m_post_mix_g, m_pre_mlp_g, m_post_mlp_g, m_conv_out_g,
                     m_attn_out_g, m_q_norm_g, m_kv_norm_g)
    small_v = pack_w(spread(v_conv_w), v_pre_mix_g, v_post_mix_g, v_pre_mlp_g, v_post_mlp_g, v_conv_out_g,
                     v_attn_out_g, v_q_norm_g, v_kv_norm_g)
    small_d, small_nm, small_nv, small_g = _adamw(small_w, small_g, small_m, small_v, name="adamw_small")
    sg, sd, snm, snv = (_unpack_small(p, chip) for p in (small_g, small_d, small_nm, small_nv))

    for src, dst in ((sg, gw), (sd, dw), (snm, nm), (snv, nv)):
        dst.update(src)

    order = ["pre_mix_g", "w_in", "conv_w", "q_norm_g", "w_uq", "kv_norm_g", "w_ukv", "conv_out_g", "attn_out_g",
             "w_o", "post_mix_g", "pre_mlp_g", "w_up", "w_down", "post_mlp_g"]
    return (loss, grad_x.reshape(1, s, d), *[gw[k] for k in order], *[dw[k] for k in order],
            *[nm[k] for k in order], *[nv[k] for k in order])
```

```python
import functools

import jax
import jax.numpy as jnp
from jax import lax
from jax.experimental import pallas as pl
from jax.experimental.pallas import tpu as pltpu

EPS = 1e-6
NEG_INF = -1e30
CHUNK_SHIFT = 6
N_HEADS = 8
HEAD_PAD = 256
QK_NOPE = 128
QK_ROPE = 64
V_DIM = 128
CONV_WIDTH = 1024
Q_RANK = 768
KV_RANK = 512
ROPE_THETA = 10000.0
ATTN_SCALE = (QK_NOPE + QK_ROPE) ** -0.5
ADAM_LR, ADAM_B1, ADAM_B2, ADAM_EPS, ADAM_WD, ADAM_STEP = 0.001, 0.9, 0.999, 1e-08, 0.01, 10

COL_CQ = 3 * CONV_WIDTH
COL_KR = COL_CQ + Q_RANK
COL_CKV = 4096
IN_PAD = COL_CKV + KV_RANK
IN_WIDTH = 3 * CONV_WIDTH + Q_RANK + KV_RANK + QK_ROPE

VMEM_LIMIT_BYTES = 56 * 1024 * 1024
MESH = pl.DeviceIdType.MESH
ANY = pl.BlockSpec(memory_space=pl.ANY)

_NN = (((1,), (0,)), ((), ()))
_NT = (((1,), (1,)), ((), ()))
_TN = (((0,), (0,)), ((), ()))


def _params(sem):
    return pltpu.CompilerParams(dimension_semantics=sem, vmem_limit_bytes=VMEM_LIMIT_BYTES)


class _Buf:
    def __init__(self, arr):
        self.arr = arr


def _position():
    return lax.axis_index("x"), lax.axis_index("y"), lax.axis_index("c")


def _other_chips(x, y):
    return [(2 * (1 - x) + y, (1 - x, y)), (2 * x + (1 - y), (x, 1 - y)), (2 * (1 - x) + (1 - y), (1 - x, 1 - y))]


def _remote(src, dst, sems, k, to):
    send, recv, off = sems
    return pltpu.make_async_remote_copy(src_ref=src, dst_ref=dst, send_sem=send.at[off + k], recv_sem=recv.at[off + k],
                                        device_id=to, device_id_type=MESH)


class _GatherIci:
    n_sems = 2
    link = "ici"

    def __init__(self, buf, lo, n):
        self.buf, self.lo, self.n, self.buffers, self.sources = buf, lo, n, [buf], []

    def _rows(self, ref, slot, which, lo=None, n=None):
        lo, n = (self.lo, self.n) if lo is None else (lo, n)
        return ref.at[slot, pl.ds(which * (ref.shape[1] // 2) + lo, n), :]

    def start(self, refs, sems):
        ref = refs[id(self.buf)]
        x, y, c = _position()
        mine = self._rows(ref, 2 * x + y, c)
        for k, (_, xy) in enumerate(_other_chips(x, y)[:2]):
            _remote(mine, mine, sems, k, (*xy, c)).start()

    def wait(self, refs, sems):
        ref = refs[id(self.buf)]
        x, y, c = _position()
        mine = self._rows(ref, 2 * x + y, c)
        for k, (slot, xy) in enumerate(_other_chips(x, y)[:2]):
            landed = self._rows(ref, slot, c)
            _remote(landed, landed, sems, k, (*xy, c)).wait_recv()
            _remote(mine, mine, sems, k, (*xy, c)).wait_send()


class _GatherForward(_GatherIci):
    def _ways(self, x, y):
        (slot_x, xy_x), (slot_y, xy_y), (slot_d, _) = _other_chips(x, y)
        h = self.n // 2
        assert h % 16 == 0, self.n
        return [(slot_x, slot_d, self.lo, xy_y), (slot_y, slot_d, self.lo + h, xy_x)], h

    def start(self, refs, sems):
        ref = refs[id(self.buf)]
        x, y, c = _position()
        ways, h = self._ways(x, y)
        for k, (slot, _, lo, xy) in enumerate(ways):
            rows = self._rows(ref, slot, c, lo, h)
            _remote(rows, rows, sems, k, (*xy, c)).start()

    def wait(self, refs, sems):
        ref = refs[id(self.buf)]
        x, y, c = _position()
        ways, h = self._ways(x, y)
        for k, (slot, lands, lo, xy) in enumerate(ways):
            landed, sent = self._rows(ref, lands, c, lo, h), self._rows(ref, slot, c, lo, h)
            _remote(landed, landed, sems, k, (*xy, c)).wait_recv()
            _remote(sent, sent, sems, k, (*xy, c)).wait_send()


class _GatherD2d(_GatherIci):
    link = "d2d"

    def __init__(self, buf, lo, n, slots):
        super().__init__(buf, lo, n)
        self.slots, self.n_sems = slots, len(slots)

    def start(self, refs, sems):
        ref = refs[id(self.buf)]
        x, y, c = _position()
        chips = _other_chips(x, y)
        for k, which in enumerate(self.slots):
            rows = self._rows(ref, chips[which][0], c)
            _remote(rows, rows, sems, k, (x, y, 1 - c)).start()

    def wait(self, refs, sems):
        ref = refs[id(self.buf)]
        x, y, c = _position()
        chips = _other_chips(x, y)
        for k, which in enumerate(self.slots):
            sent, landed = self._rows(ref, chips[which][0], c), self._rows(ref, chips[which][0], 1 - c)
            _remote(landed, landed, sems, k, (x, y, 1 - c)).wait_recv()
            _remote(sent, sent, sems, k, (x, y, 1 - c)).wait_send()


class _ScatterDiag:
    n_sems = 2
    link = "ici"

    def __init__(self, src, via, lo, n):
        self.src, self.via, self.lo, self.n, self.buffers, self.sources = src, via, lo, n, [via], [src]

    def _copies(self, refs, sems):
        src, via = refs[id(self.src)], refs[id(self.via)]
        x, y, c = _position()
        (_, xy_x), (_, xy_y), (slot_d, _) = _other_chips(x, y)
        h2 = via.shape[0] // 2
        return [_remote(src.at[slot_d, pl.ds(first + self.lo, self.n), :], via.at[pl.ds(first + self.lo, self.n), :],
                        sems, k, (*xy, c)) for k, (first, xy) in enumerate(((0, xy_x), (h2, xy_y)))]

    def start(self, refs, sems):
        for cp in self._copies(refs, sems):
            cp.start()

    def wait(self, refs, sems):
        for cp in self._copies(refs, sems):
            cp.wait_recv()
            cp.wait_send()


class _ScatterNear:
    n_sems = 2
    link = "ici"

    def __init__(self, src, dst, lo, n):
        self.src, self.dst, self.lo, self.n, self.buffers, self.sources = src, dst, lo, n, [dst], [src]

    def _copies(self, refs, sems, landing):
        src, dst = refs[id(self.src)], refs[id(self.dst)]
        x, y, c = _position()
        rows = pl.ds(self.lo, self.n)
        return [_remote(src.at[k, rows, :], dst.at[slot if landing else 2 * x + y, rows, :], sems, k, (*xy, c))
                for k, (slot, xy) in enumerate(_other_chips(x, y)[:2])]

    def start(self, refs, sems):
        for cp in self._copies(refs, sems, False):
            cp.start()

    def wait(self, refs, sems):
        for cp in self._copies(refs, sems, True):
            cp.wait_recv()
            cp.wait_send()


class _PairExchange:
    n_sems = 1
    link = "d2d"

    def __init__(self, src, dst):
        self.src, self.dst, self.buffers, self.sources = src, dst, [dst], [src]

    def _copy(self, refs, sems):
        src, dst = refs[id(self.src)], refs[id(self.dst)]
        x, y, c = _position()
        h = src.shape[1] // 2
        return _remote(src.at[:, pl.ds((1 - c) * h, h), :], dst, sems, 0, (x, y, 1 - c))

    def start(self, refs, sems):
        self._copy(refs, sems).start()

    def wait(self, refs, sems):
        self._copy(refs, sems).wait()


class _PairShare:
    n_sems = 1
    link = "d2d"

    def __init__(self, buf):
        self.buf, self.buffers, self.sources = buf, [buf], []

    def _rows(self, ref, which):
        h = ref.shape[0] // 2
        return ref.at[pl.ds(which * h, h), :]

    def start(self, refs, sems):
        ref = refs[id(self.buf)]
        x, y, c = _position()
        _remote(self._rows(ref, c), self._rows(ref, c), sems, 0, (x, y, 1 - c)).start()

    def wait(self, refs, sems):
        ref = refs[id(self.buf)]
        x, y, c = _position()
        _remote(self._rows(ref, c), self._rows(ref, c), sems, 0, (x, y, 1 - c)).wait_send()
        _remote(self._rows(ref, 1 - c), self._rows(ref, 1 - c), sems, 0, (x, y, 1 - c)).wait_recv()


_COLLECTIVE_IDS = {("ici",): 1, ("d2d",): 2, ("d2d", "ici"): 3}


def _links(jobs):
    return tuple(sorted({j.link for j in jobs}))


def _handshake(links):
    x, y, c = _position()
    peers = ([(1 - x, y, c), (x, 1 - y, c)] if "ici" in links else []) + ([(x, y, 1 - c)] if "d2d" in links else [])
    barrier = pltpu.get_barrier_semaphore()
    for peer in peers:
        pl.semaphore_signal(barrier, inc=1, device_id=peer, device_id_type=MESH)
    pl.semaphore_wait(barrier, len(peers))


def _unique(items):
    seen, out = set(), []
    for it in items:
        if id(it) not in seen:
            seen.add(id(it))
            out.append(it)
    return out


def _job_operands(jobs):
    sources = _unique([a for j in jobs for a in j.sources])
    buffers = _unique([b for j in jobs for b in j.buffers])
    held = [b for b in buffers if not isinstance(b.arr, jax.ShapeDtypeStruct)]
    fresh = [b for b in buffers if isinstance(b.arr, jax.ShapeDtypeStruct)]
    return sources, held, fresh


def _sem_offsets(jobs):
    offs, total = [], 0
    for j in jobs:
        offs.append(total)
        total += j.n_sems
    return offs, total


def _call(body, *, name, grid, in_specs, out_specs, out_shape, args, semantics, scratch_shapes=(), jobs=(),
          prefetch=None):
    n_pre = 0 if prefetch is None else 1

    def launch(fn, in_specs, out_specs, scratch, **kw):
        if prefetch is None:
            return pl.pallas_call(fn, name=name, grid=grid, in_specs=in_specs, out_specs=out_specs,
                                  scratch_shapes=scratch, **kw)
        return pl.pallas_call(fn, name=name, grid_spec=pltpu.PrefetchScalarGridSpec(
            num_scalar_prefetch=1, grid=grid, in_specs=in_specs, out_specs=out_specs, scratch_shapes=scratch), **kw)

    pre = () if prefetch is None else (prefetch,)
    if not jobs:
        return launch(body, list(in_specs), list(out_specs), list(scratch_shapes), out_shape=list(out_shape),
                      compiler_params=_params(semantics))(*pre, *args)
    sources, held, fresh = _job_operands(jobs)
    offs, n_sem = _sem_offsets(jobs)
    links = _links(jobs)
    n_in, n_out, n_scr = len(in_specs), len(out_specs), len(scratch_shapes)
    n_src, n_held, n_fresh = len(sources), len(held), len(fresh)

    def carried(*refs):
        pre_refs, refs = refs[:n_pre], refs[n_pre:]
        ins = refs[:n_in]
        src_refs = refs[n_in:n_in + n_src]
        o0 = n_in + n_src + n_held
        outs = refs[o0:o0 + n_out]
        buf_refs = refs[o0 + n_out:o0 + n_out + n_held + n_fresh]
        s0 = o0 + n_out + n_held + n_fresh
        scratch = refs[s0:s0 + n_scr]
        send, recv = refs[s0 + n_scr], refs[s0 + n_scr + 1]
        where = {id(a): r for a, r in zip(sources, src_refs)}
        where.update({id(b): r for b, r in zip(held + fresh, buf_refs)})
        ids = [pl.program_id(a) for a in range(len(grid))]
        first = functools.reduce(jnp.logical_and, [i == 0 for i in ids])
        last = functools.reduce(jnp.logical_and, [i == g - 1 for i, g in zip(ids, grid)])

        @pl.when(first)
        def _():
            _handshake(links)
            for j, off in zip(jobs, offs):
                j.start(where, (send, recv, off))

        body(*pre_refs, *ins, *outs, *scratch)

        @pl.when(last)
        def _():
            for j, off in zip(jobs, offs):
                j.wait(where, (send, recv, off))

    shape_of = lambda b: jax.ShapeDtypeStruct(b.arr.shape, b.arr.dtype)
    res = launch(
        carried, [*in_specs, *[ANY] * (n_src + n_held)], [*out_specs, *[ANY] * (n_held + n_fresh)],
        [*scratch_shapes, pltpu.SemaphoreType.DMA((n_sem,)), pltpu.SemaphoreType.DMA((n_sem,))],
        out_shape=[*out_shape, *[shape_of(b) for b in held + fresh]],
        input_output_aliases={n_pre + n_in + n_src + i: n_out + i for i in range(n_held)},
        compiler_params=pltpu.CompilerParams(dimension_semantics=("arbitrary",) * len(grid),
                                             vmem_limit_bytes=VMEM_LIMIT_BYTES, has_side_effects=True,
                                             collective_id=_COLLECTIVE_IDS[links]),
    )(*pre, *args, *sources, *[b.arr for b in held])
    for b, new in zip(held + fresh, res[n_out:]):
        b.arr = new
    return list(res[:n_out])


def _comm(name, phases):
    jobs = [j for ph in phases for j in ph]
    sources, held, fresh = _job_operands(jobs)
    offs, n_sem = _sem_offsets(jobs)
    off_of = {id(j): o for j, o in zip(jobs, offs)}
    links = _links(jobs)
    n_src, n_held, n_fresh = len(sources), len(held), len(fresh)

    def body(*refs):
        src_refs = refs[:n_src]
        buf_refs = refs[n_src + n_held:n_src + 2 * n_held + n_fresh]
        send, recv = refs[-2], refs[-1]
        where = {id(a): r for a, r in zip(sources, src_refs)}
        where.update({id(b): r for b, r in zip(held + fresh, buf_refs)})
        _handshake(links)
        for ph in phases:
            for j in ph:
                j.start(where, (send, recv, off_of[id(j)]))
            for j in ph:
                j.wait(where, (send, recv, off_of[id(j)]))

    shape_of = lambda b: jax.ShapeDtypeStruct(b.arr.shape, b.arr.dtype)
    res = pl.pallas_call(
        body, name=name,
        in_specs=[ANY] * (n_src + n_held), out_specs=[ANY] * (n_held + n_fresh),
        out_shape=[shape_of(b) for b in held + fresh],
        input_output_aliases={n_src + i: i for i in range(n_held)},
        scratch_shapes=[pltpu.SemaphoreType.DMA((n_sem,)), pltpu.SemaphoreType.DMA((n_sem,))],
        compiler_params=pltpu.CompilerParams(has_side_effects=True, collective_id=_COLLECTIVE_IDS[links]),
    )(*sources, *[b.arr for b in held])
    for b, new in zip(held + fresh, res):
        b.arr = new


def _matmul(a, b, *, dims, mnk, tiles, name, out_dtype=jnp.float32, a_spec=None, b_spec=None,
            out_shape=None, o_spec=None, epilogue=None, extra=(), extra_specs=(), jobs=()):
    m, n, k = mnk
    tm, tn, tk = tiles
    assert m % tm == 0 and n % tn == 0 and k % tk == 0, (name, mnk, tiles)
    gm, gn, gk = m // tm, n // tn, k // tk
    if a_spec is None:
        a_spec = (pl.BlockSpec((tk, tm), lambda i, j, l: (l, i)) if dims is _TN
                  else pl.BlockSpec((tm, tk), lambda i, j, l: (i, l)))
    if b_spec is None:
        b_spec = (pl.BlockSpec((tn, tk), lambda i, j, l: (j, l)) if dims is _NT
                  else pl.BlockSpec((tk, tn), lambda i, j, l: (l, j)))
    if out_shape is None:
        out_shape = jax.ShapeDtypeStruct((m, n), out_dtype)
    if o_spec is None:
        o_spec = pl.BlockSpec((tm, tn), lambda i, j, l: (i, j))
    single = not isinstance(out_shape, (tuple, list))
    n_extra = len(extra)

    def finish(acc, extra_refs, out_refs):
        if epilogue is None:
            out_refs[0][...] = acc.astype(out_refs[0].dtype)
        else:
            epilogue(acc, extra_refs, out_refs)

    def body_whole_k(*refs):
        a_ref, b_ref = refs[0], refs[1]
        acc = lax.dot_general(a_ref[...], b_ref[...], dims, preferred_element_type=jnp.float32)
        finish(acc, refs[2:2 + n_extra], refs[2 + n_extra:])

    def body_split_k(*refs):
        a_ref, b_ref = refs[0], refs[1]
        extra_refs = refs[2:2 + n_extra]
        out_refs = refs[2 + n_extra:-1]
        acc_ref = refs[-1]
        step = pl.program_id(2)
        part = lax.dot_general(a_ref[...], b_ref[...], dims, preferred_element_type=jnp.float32)

        @pl.when(step == 0)
        def _():
            acc_ref[...] = part

        @pl.when(jnp.logical_and(step > 0, step < gk - 1))
        def _():
            acc_ref[...] += part

        @pl.when(step == gk - 1)
        def _():
            finish(acc_ref[...] + part, extra_refs, out_refs)

    res = _call(
        body_whole_k if gk == 1 else body_split_k, name=name, grid=(gm, gn, gk),
        in_specs=[a_spec, b_spec, *extra_specs],
        out_specs=[o_spec] if single else list(o_spec),
        out_shape=[out_shape] if single else list(out_shape),
        scratch_shapes=[] if gk == 1 else [pltpu.VMEM((tm, tn), jnp.float32)],
        semantics=("parallel", "parallel", "arbitrary"), args=(a, b, *extra), jobs=jobs)
    return res[0] if single else res


def _rstd(x):
    return lax.rsqrt(jnp.mean(x * x, axis=-1, keepdims=True) + EPS)


def _rms_bwd_rows(x, g, dy):
    r = _rstd(x)
    xn = x * r
    dyg = dy * g
    dx = r * (dyg - xn * jnp.mean(xn * dyg, axis=-1, keepdims=True))
    return dx, dy * xn


def _acc_rows(ref, rows, first):
    part = jnp.sum(rows, axis=0, keepdims=True)

    @pl.when(first)
    def _():
        ref[...] = part

    @pl.when(jnp.logical_not(first))
    def _():
        ref[...] += part


def _rms_fwd(x, g, *, width, col, tm, name, jobs=()):
    s = x.shape[0]

    def body(x_ref, g_ref, o_ref):
        v = x_ref[...]
        o_ref[...] = (v * _rstd(v) * g_ref[...]).astype(o_ref.dtype)

    return _call(
        body, name=name, grid=(s // tm,),
        in_specs=[pl.BlockSpec((tm, width), lambda i: (i, col)), pl.BlockSpec((1, width), lambda i: (0, 0))],
        out_specs=[pl.BlockSpec((tm, width), lambda i: (i, 0))],
        out_shape=[jax.ShapeDtypeStruct((s, width), jnp.bfloat16)],
        semantics=("parallel",), args=(x, g), jobs=jobs)[0]


def _rms_bwd(x, g, dy, *, width, col, tm, name, jobs=()):
    s = x.shape[0]

    def body(x_ref, g_ref, dy_ref, dx_ref, dg_ref):
        dx, dgr = _rms_bwd_rows(x_ref[...], g_ref[...], dy_ref[...])
        dx_ref[...] = dx.astype(dx_ref.dtype)
        _acc_rows(dg_ref, dgr, pl.program_id(0) == 0)

    return _call(
        body, name=name, grid=(s // tm,),
        in_specs=[pl.BlockSpec((tm, width), lambda i: (i, col)), pl.BlockSpec((1, width), lambda i: (0, 0)),
                  pl.BlockSpec((tm, width), lambda i: (i, 0))],
        out_specs=[pl.BlockSpec((tm, width), lambda i: (i, 0)), pl.BlockSpec((1, width), lambda i: (0, 0))],
        out_shape=[jax.ShapeDtypeStruct((s, width), jnp.bfloat16), jax.ShapeDtypeStruct((1, width), jnp.float32)],
        semantics=("arbitrary",), args=(x, g, dy), jobs=jobs)


def _row_specs(tm, d, n):
    return [pl.BlockSpec((tm, d), lambda i: (i, 0)) for _ in range(n)]


def _gain_specs(d, n):
    return [pl.BlockSpec((1, d), lambda i: (0, 0)) for _ in range(n)]


def _mix_residual_fwd(x, mix, g_post_mix, g_pre_mlp, *, tm, jobs=()):
    s, d = x.shape

    def body(x_ref, mix_ref, g1_ref, g2_ref, x2_ref, h2_ref):
        mixv = mix_ref[...]
        x2 = x_ref[...] + mixv * _rstd(mixv) * g1_ref[...]
        x2_ref[...] = x2
        h2_ref[...] = (x2 * _rstd(x2) * g2_ref[...]).astype(h2_ref.dtype)

    return _call(
        body, name="mix_residual_fwd", grid=(s // tm,),
        in_specs=_row_specs(tm, d, 2) + _gain_specs(d, 2),
        out_specs=_row_specs(tm, d, 2),
        out_shape=[jax.ShapeDtypeStruct((s, d), jnp.float32), jax.ShapeDtypeStruct((s, d), jnp.bfloat16)],
        semantics=("parallel",), args=(x, mix, g_post_mix, g_pre_mlp), jobs=jobs)


def _loss_head(x2, mlp, target, g_post_mlp, *, tm, jobs=()):
    s, d = x2.shape

    def body(x2_ref, m_ref, t_ref, g_ref, dx3_ref, dm_ref, dg_ref, loss_ref):
        first = pl.program_id(0) == 0
        mv = m_ref[...]
        g = g_ref[...]
        diff = x2_ref[...] + mv * _rstd(mv) * g - t_ref[...]
        dx3 = diff * (1.0 / d)
        dx3_ref[...] = dx3
        dm, dgr = _rms_bwd_rows(mv, g, dx3)
        dm_ref[...] = dm.astype(dm_ref.dtype)
        _acc_rows(dg_ref, dgr, first)
        part = 0.5 * jnp.sum(jnp.mean(diff * diff, axis=-1, keepdims=True), axis=0, keepdims=True)
        _acc_rows(loss_ref, jnp.broadcast_to(part, (1, 128)), first)

    return _call(
        body, name="loss_head", grid=(s // tm,),
        in_specs=_row_specs(tm, d, 3) + _gain_specs(d, 1),
        out_specs=_row_specs(tm, d, 2) + _gain_specs(d, 1) + [pl.BlockSpec((1, 128), lambda i: (0, 0))],
        out_shape=[jax.ShapeDtypeStruct((s, d), jnp.float32), jax.ShapeDtypeStruct((s, d), jnp.bfloat16),
                   jax.ShapeDtypeStruct((1, d), jnp.float32), jax.ShapeDtypeStruct((1, 128), jnp.float32)],
        semantics=("arbitrary",), args=(x2, mlp, target, g_post_mlp), jobs=jobs)


def _mix_residual_bwd(dx3, dh2, x2, mix, g_pre_mlp, g_post_mix, *, tm, jobs=()):
    s, d = x2.shape

    def body(dx3_ref, dh2_ref, x2_ref, mix_ref, g2_ref, g1_ref, dx2_ref, dmix_ref, dg2_ref, dg1_ref):
        first = pl.program_id(0) == 0
        d_in, dgr2 = _rms_bwd_rows(x2_ref[...], g2_ref[...], dh2_ref[...])
        dx2 = dx3_ref[...] + d_in
        dx2_ref[...] = dx2
        dmix, dgr1 = _rms_bwd_rows(mix_ref[...], g1_ref[...], dx2)
        dmix_ref[...] = dmix.astype(dmix_ref.dtype)
        _acc_rows(dg2_ref, dgr2, first)
        _acc_rows(dg1_ref, dgr1, first)

    return _call(
        body, name="mix_residual_bwd", grid=(s // tm,),
        in_specs=_row_specs(tm, d, 4) + _gain_specs(d, 2),
        out_specs=_row_specs(tm, d, 2) + _gain_specs(d, 2),
        out_shape=[jax.ShapeDtypeStruct((s, d), jnp.float32), jax.ShapeDtypeStruct((s, d), jnp.bfloat16),
                   jax.ShapeDtypeStruct((1, d), jnp.float32), jax.ShapeDtypeStruct((1, d), jnp.float32)],
        semantics=("arbitrary",), args=(dx3, dh2, x2, mix, g_pre_mlp, g_post_mix), jobs=jobs)


def _input_bwd(dx2, dh1, x, g_pre_mix, *, tm, jobs=()):
    s, d = x.shape

    def body(dx2_ref, dh1_ref, x_ref, g_ref, dx_ref, dg_ref):
        d_in, dgr = _rms_bwd_rows(x_ref[...], g_ref[...], dh1_ref[...])
        dx_ref[...] = dx2_ref[...] + d_in
        _acc_rows(dg_ref, dgr, pl.program_id(0) == 0)

    return _call(
        body, name="input_bwd", grid=(s // tm,),
        in_specs=_row_specs(tm, d, 3) + _gain_specs(d, 1),
        out_specs=_row_specs(tm, d, 1) + _gain_specs(d, 1),
        out_shape=[jax.ShapeDtypeStruct((s, d), jnp.float32), jax.ShapeDtypeStruct((1, d), jnp.float32)],
        semantics=("arbitrary",), args=(dx2, dh1, x, g_pre_mix), jobs=jobs)


def _shift_rows(z, by):
    s = z.shape[0]
    rows = lax.broadcasted_iota(jnp.int32, z.shape, 0)
    rolled = pltpu.roll(z, by % s, axis=0)
    keep = rows >= by if by > 0 else rows < s + by
    return jnp.where(keep, rolled, 0.0)


def _conv_fwd(proj, conv_w, conv_out_g, jobs=()):
    s = proj.shape[0]
    groups = CONV_WIDTH // 128

    def body(u_ref, gb_ref, gc_ref, w_ref, g_ref, y_ref):
        z = gc_ref[...] * u_ref[...]
        w = w_ref[...]
        conv = w[0:1, :] * _shift_rows(z, 2) + w[1:2, :] * _shift_rows(z, 1) + w[2:3, :] * z
        y = gb_ref[...] * conv
        y_ref[...] = (y * _rstd(y) * g_ref[...]).astype(y_ref.dtype)

    col = lambda base: pl.BlockSpec((s, 128), lambda j: (0, base + j))
    return _call(
        body, name="conv_fwd", grid=(groups,),
        in_specs=[col(0), col(groups), col(2 * groups), pl.BlockSpec((3, 128), lambda j: (0, j)),
                  pl.BlockSpec((1, 128), lambda j: (0, j))],
        out_specs=[pl.BlockSpec((s, 128), lambda j: (0, j))],
        out_shape=[jax.ShapeDtypeStruct((s, CONV_WIDTH), jnp.bfloat16)],
        semantics=("parallel",), args=(proj, proj, proj, conv_w, conv_out_g), jobs=jobs)[0]


def _conv_bwd(proj, conv_w, conv_out_g, dycat, jobs=()):
    s = proj.shape[0]
    groups = CONV_WIDTH // 128

    def body(u_ref, gb_ref, gc_ref, w_ref, g_ref, dy_ref, du_ref, dgb_ref, dgc_ref, dw_ref, dg_ref):
        u, gb, gc = u_ref[...], gb_ref[...], gc_ref[...]
        w = w_ref[...]
        z = gc * u
        z1, z2 = _shift_rows(z, 1), _shift_rows(z, 2)
        conv = w[0:1, :] * z2 + w[1:2, :] * z1 + w[2:3, :] * z
        dyr, dgr = _rms_bwd_rows(gb * conv, g_ref[...], dy_ref[...])
        dg_ref[...] = jnp.sum(dgr, axis=0, keepdims=True)
        dgb_ref[...] = (dyr * conv).astype(dgb_ref.dtype)
        dconv = dyr * gb
        dw_ref[0:1, :] = jnp.sum(dconv * z2, axis=0, keepdims=True)
        dw_ref[1:2, :] = jnp.sum(dconv * z1, axis=0, keepdims=True)
        dw_ref[2:3, :] = jnp.sum(dconv * z, axis=0, keepdims=True)
        dz = w[2:3, :] * dconv + w[1:2, :] * _shift_rows(dconv, -1) + w[0:1, :] * _shift_rows(dconv, -2)
        dgc_ref[...] = (dz * u).astype(dgc_ref.dtype)
        du_ref[...] = (dz * gc).astype(du_ref.dtype)

    col = lambda base: pl.BlockSpec((s, 128), lambda j: (0, base + j))
    act = jax.ShapeDtypeStruct((s, CONV_WIDTH), jnp.bfloat16)
    return _call(
        body, name="conv_bwd", grid=(groups,),
        in_specs=[col(0), col(groups), col(2 * groups), pl.BlockSpec((3, 128), lambda j: (0, j)),
                  pl.BlockSpec((1, 128), lambda j: (0, j)), col(0)],
        out_specs=[col(0), col(0), col(0), pl.BlockSpec((3, 128), lambda j: (0, j)),
                   pl.BlockSpec((1, 128), lambda j: (0, j))],
        out_shape=[act, act, act, jax.ShapeDtypeStruct((3, CONV_WIDTH), jnp.float32),
                   jax.ShapeDtypeStruct((1, CONV_WIDTH), jnp.float32)],
        semantics=("parallel",), args=(proj, proj, proj, conv_w, conv_out_g, dycat), jobs=jobs)


def _rope_tables(s):
    pos = jnp.arange(s, dtype=jnp.float32)
    inv_freq = jnp.power(ROPE_THETA, -jnp.arange(0, QK_ROPE, 2, dtype=jnp.float32) / QK_ROPE)
    ang = pos[:, None] * inv_freq[None, :]
    cos, sin = jnp.cos(ang), jnp.sin(ang)
    zeros = jnp.zeros((s, 128 - QK_ROPE), jnp.float32)
    return (jnp.concatenate([cos, cos, zeros], axis=1), jnp.concatenate([-sin, sin, zeros], axis=1))


def _swap_halves(x):
    lane = lax.broadcasted_iota(jnp.int32, x.shape, 1)
    swapped = jnp.where(lane < QK_ROPE // 2, pltpu.roll(x, 128 - QK_ROPE // 2, axis=1),
                        pltpu.roll(x, QK_ROPE // 2, axis=1))
    return jnp.where(lane < QK_ROPE, swapped, 0.0)


def _rope(x, cos, sin):
    return x * cos + _swap_halves(x) * sin


def _rope_transposed(d, cos, sin):
    return d * cos + _swap_halves(d * sin)


def _qk_rope_fwd(q_pad, proj, cos, sin, *, tm, jobs=()):
    s = q_pad.shape[0]
    wq = N_HEADS * HEAD_PAD

    def body(q_ref, kr_ref, cos_ref, sin_ref, qo_ref, kro_ref):
        c, sn = cos_ref[...], sin_ref[...]
        for h in range(N_HEADS):
            lo = h * HEAD_PAD
            qo_ref[:, lo:lo + 128] = q_ref[:, lo:lo + 128].astype(qo_ref.dtype)
            qo_ref[:, lo + 128:lo + 256] = _rope(q_ref[:, lo + 128:lo + 256], c, sn).astype(qo_ref.dtype)
        kro_ref[...] = _rope(kr_ref[...], c, sn).astype(kro_ref.dtype)

    return _call(
        body, name="qk_rope_fwd", grid=(s // tm,),
        in_specs=[pl.BlockSpec((tm, wq), lambda i: (i, 0)), pl.BlockSpec((tm, 128), lambda i: (i, COL_KR // 128)),
                  pl.BlockSpec((tm, 128), lambda i: (i, 0)), pl.BlockSpec((tm, 128), lambda i: (i, 0))],
        out_specs=[pl.BlockSpec((tm, wq), lambda i: (i, 0)), pl.BlockSpec((tm, 128), lambda i: (i, 0))],
        out_shape=[jax.ShapeDtypeStruct((s, wq), jnp.bfloat16), jax.ShapeDtypeStruct((s, 128), jnp.bfloat16)],
        semantics=("parallel",), args=(q_pad, proj, cos, sin), jobs=jobs)


def _qk_rope_bwd(dq_pad, dk_pad, dv, cos, sin, *, tm, jobs=()):
    s = dq_pad.shape[0]
    wq = N_HEADS * HEAD_PAD

    def body(dq_ref, dk_ref, dv_ref, cos_ref, sin_ref, dqo_ref, dkv_ref, dkr_ref):
        c, sn = cos_ref[...], sin_ref[...]
        dkr = jnp.zeros((tm, 128), jnp.float32)
        for h in range(N_HEADS):
            lo = h * HEAD_PAD
            dqo_ref[:, lo:lo + 128] = dq_ref[:, lo:lo + 128].astype(dqo_ref.dtype)
            dqo_ref[:, lo + 128:lo + 256] = _rope_transposed(dq_ref[:, lo + 128:lo + 256], c, sn).astype(dqo_ref.dtype)
            dkv_ref[:, lo:lo + 128] = dk_ref[:, lo:lo + 128].astype(dkv_ref.dtype)
            dkv_ref[:, lo + 128:lo + 256] = dv_ref[:, h * V_DIM:(h + 1) * V_DIM].astype(dkv_ref.dtype)
            dkr = dkr + dk_ref[:, lo + 128:lo + 256]
        dkr_ref[...] = _rope_transposed(dkr, c, sn).astype(dkr_ref.dtype)

    return _call(
        body, name="qk_rope_bwd", grid=(s // tm,),
        in_specs=[pl.BlockSpec((tm, wq), lambda i: (i, 0)), pl.BlockSpec((tm, wq), lambda i: (i, 0)),
                  pl.BlockSpec((tm, N_HEADS * V_DIM), lambda i: (i, 0)),
                  pl.BlockSpec((tm, 128), lambda i: (i, 0)), pl.BlockSpec((tm, 128), lambda i: (i, 0))],
        out_specs=[pl.BlockSpec((tm, wq), lambda i: (i, 0)), pl.BlockSpec((tm, wq), lambda i: (i, 0)),
                   pl.BlockSpec((tm, 128), lambda i: (i, 0))],
        out_shape=[jax.ShapeDtypeStruct((s, wq), jnp.bfloat16), jax.ShapeDtypeStruct((s, wq), jnp.bfloat16),
                   jax.ShapeDtypeStruct((s, 128), jnp.bfloat16)],
        semantics=("parallel",), args=(dq_pad, dk_pad, dv, cos, sin), jobs=jobs)


def _visible(q0, k0, t):
    qpos = q0 + lax.broadcasted_iota(jnp.int32, (t, t), 0)
    kpos = k0 + lax.broadcasted_iota(jnp.int32, (t, t), 1)
    return lax.shift_right_logical(kpos, CHUNK_SHIFT) <= lax.shift_right_logical(qpos, CHUNK_SHIFT)


def _attn_fwd(q, kv, kr, attn_out_g, *, t, jobs=()):
    s = q.shape[0]
    nq = s // t

    def body(q_ref, kn_ref, v_ref, kr_ref, g_ref, o_ref, lse_ref, y_ref, kcat_ref):
        i = pl.program_id(1)

        @pl.when(i == 0)
        def _():
            kcat_ref[:, 0:128] = kn_ref[...]
            kcat_ref[:, 128:256] = kr_ref[...]

        qv = q_ref[...]

        def step(j, carry, diagonal):
            m, l, acc = carry
            k = kcat_ref[pl.ds(pl.multiple_of(j * t, t), t), :]
            v = v_ref[pl.ds(pl.multiple_of(j * t, t), t), :]
            sc = lax.dot_general(qv, k, _NT, preferred_element_type=jnp.float32) * ATTN_SCALE
            if diagonal:
                sc = jnp.where(_visible(0, 0, t), sc, NEG_INF)
            m_new = jnp.maximum(m, jnp.max(sc, axis=-1, keepdims=True))
            p = jnp.exp(sc - m_new)
            alpha = jnp.exp(m - m_new)
            l = alpha * l + jnp.sum(p, axis=-1, keepdims=True)
            acc = alpha * acc + lax.dot_general(p.astype(jnp.bfloat16), v, _NN, preferred_element_type=jnp.float32)
            return m_new, l, acc

        init = (jnp.full((t, 1), NEG_INF, jnp.float32), jnp.zeros((t, 1), jnp.float32),
                jnp.zeros((t, V_DIM), jnp.float32))
        before = lax.fori_loop(0, i, functools.partial(step, diagonal=False), init)
        m, l, acc = step(i, before, True)
        o = acc / l
        o_ref[...] = o
        lse_ref[...] = jnp.broadcast_to(m + jnp.log(l), (t, 128))
        y_ref[...] = (o * _rstd(o) * g_ref[...]).astype(y_ref.dtype)

    head_rows = lambda w, f: pl.BlockSpec((s, w), lambda h, i: (0, f(h)))
    blk = pl.BlockSpec((t, 128), lambda h, i: (i, h))
    full = jax.ShapeDtypeStruct((s, N_HEADS * V_DIM), jnp.float32)
    return _call(
        body, name="attn_fwd", grid=(N_HEADS, nq),
        in_specs=[pl.BlockSpec((t, HEAD_PAD), lambda h, i: (i, h)), head_rows(128, lambda h: 2 * h),
                  head_rows(128, lambda h: 2 * h + 1), head_rows(128, lambda h: 0),
                  pl.BlockSpec((1, 128), lambda h, i: (0, h))],
        out_specs=[blk, blk, blk],
        out_shape=[full, full, jax.ShapeDtypeStruct((s, N_HEADS * V_DIM), jnp.bfloat16)],
        scratch_shapes=[pltpu.VMEM((s, HEAD_PAD), jnp.bfloat16)],
        semantics=("arbitrary", "arbitrary"), args=(q, kv, kv, kr, attn_out_g), jobs=jobs)


def _attn_norm_bwd(o, attn_out_g, dycat, jobs=()):
    s = o.shape[0]

    def body(o_ref, g_ref, dy_ref, do_ref, delta_ref, dg_ref):
        ov = o_ref[...]
        do, dgr = _rms_bwd_rows(ov, g_ref[...], dy_ref[...])
        do_ref[...] = do.astype(do_ref.dtype)
        delta_ref[...] = jnp.broadcast_to(jnp.sum(do * ov, axis=-1, keepdims=True), (s, 128))
        dg_ref[...] = jnp.sum(dgr, axis=0, keepdims=True)

    col = lambda base: pl.BlockSpec((s, 128), lambda h: (0, base + h))
    return _call(
        body, name="attn_norm_bwd", grid=(N_HEADS,),
        in_specs=[col(0), pl.BlockSpec((1, 128), lambda h: (0, h)), col(CONV_WIDTH // 128)],
        out_specs=[col(0), col(0), pl.BlockSpec((1, 128), lambda h: (0, h))],
        out_shape=[jax.ShapeDtypeStruct((s, N_HEADS * V_DIM), jnp.bfloat16),
                   jax.ShapeDtypeStruct((s, N_HEADS * V_DIM), jnp.float32),
                   jax.ShapeDtypeStruct((1, N_HEADS * V_DIM), jnp.float32)],
        semantics=("parallel",), args=(o, attn_out_g, dycat), jobs=jobs)


def _attn_bwd(q, kv, kr, do, lse, delta, *, t, jobs=()):
    s = q.shape[0]
    nq = s // t

    def body(q_ref, kn_ref, v_ref, kr_ref, do_ref, lse_ref, delta_ref, dq_ref, dk_ref, dv_ref, kcat_ref):
        kcat_ref[:, 0:128] = kn_ref[...]
        kcat_ref[:, 128:256] = kr_ref[...]
        dq_ref[...] = jnp.zeros_like(dq_ref)
        dk_ref[...] = jnp.zeros_like(dk_ref)
        dv_ref[...] = jnp.zeros_like(dv_ref)

        def kv_step(j, _):
            krows = pl.ds(pl.multiple_of(j * t, t), t)
            k = kcat_ref[krows, :]
            v = v_ref[krows, :]

            def q_step(i, _, diagonal):
                qrows = pl.ds(pl.multiple_of(i * t, t), t)
                qv = q_ref[qrows, :]
                dov = do_ref[qrows, :]
                sc = lax.dot_general(qv, k, _NT, preferred_element_type=jnp.float32) * ATTN_SCALE
                if diagonal:
                    sc = jnp.where(_visible(0, 0, t), sc, NEG_INF)
                p = jnp.exp(sc - lse_ref[qrows, :][:, 0:1])
                dp = lax.dot_general(dov, v, _NT, preferred_element_type=jnp.float32)
                ds = (p * (dp - delta_ref[qrows, :][:, 0:1]) * ATTN_SCALE).astype(jnp.bfloat16)
                dv_ref[krows, :] += lax.dot_general(p.astype(jnp.bfloat16), dov, _TN,
                                                    preferred_element_type=jnp.float32)
                dk_ref[krows, :] += lax.dot_general(ds, qv, _TN, preferred_element_type=jnp.float32)
                dq_ref[qrows, :] += lax.dot_general(ds, k, _NN, preferred_element_type=jnp.float32)
                return 0

            q_step(j, 0, True)
            lax.fori_loop(j + 1, nq, functools.partial(q_step, diagonal=False), 0)
            return 0

        lax.fori_loop(0, nq, kv_step, 0)

    col = lambda w, f: pl.BlockSpec((s, w), lambda h: (0, f(h)))
    return _call(
        body, name="attn_bwd", grid=(N_HEADS,),
        in_specs=[col(HEAD_PAD, lambda h: h), col(128, lambda h: 2 * h), col(128, lambda h: 2 * h + 1),
                  col(128, lambda h: 0), col(128, lambda h: h), col(128, lambda h: h), col(128, lambda h: h)],
        out_specs=[col(HEAD_PAD, lambda h: h), col(HEAD_PAD, lambda h: h), col(128, lambda h: h)],
        out_shape=[jax.ShapeDtypeStruct((s, N_HEADS * HEAD_PAD), jnp.float32),
                   jax.ShapeDtypeStruct((s, N_HEADS * HEAD_PAD), jnp.float32),
                   jax.ShapeDtypeStruct((s, N_HEADS * V_DIM), jnp.float32)],
        scratch_shapes=[pltpu.VMEM((s, HEAD_PAD), jnp.bfloat16)],
        semantics=("parallel",), args=(q, kv, kv, kr, do, lse, delta), jobs=jobs)


def _row_tile(rows, cap=256):
    for cand in (512, 256, 128, 64, 32, 16, 8):
        if cand <= cap and rows % cand == 0:
            return cand
    return rows


def _cast_into_slot(w, pos, *, name, jobs=()):
    r, c = w.shape
    tr = _row_tile(r)

    def body(pos_ref, w_ref, o_ref):
        o_ref[...] = w_ref[...].astype(o_ref.dtype)

    return _call(
        body, name=name, grid=(r // tr,), prefetch=pos,
        in_specs=[pl.BlockSpec((tr, c), lambda i, p: (i, 0))],
        out_specs=[pl.BlockSpec((None, tr, c), lambda i, p: (p[1], i, 0))],
        out_shape=[jax.ShapeDtypeStruct((4, r, c), jnp.bfloat16)],
        semantics=("parallel",), args=(w,), jobs=jobs)[0]


def _cast_many_into_slots(ws, pos, *, name, jobs=()):
    steps = 8
    assert all(w.shape[0] % (16 * steps) == 0 for w in ws), [w.shape for w in ws]

    def body(pos_ref, *refs):
        for w_ref, o_ref in zip(refs[:len(ws)], refs[len(ws):]):
            o_ref[...] = w_ref[...].astype(o_ref.dtype)

    return _call(
        body, name=name, grid=(steps,), prefetch=pos,
        in_specs=[pl.BlockSpec((w.shape[0] // steps, w.shape[1]), lambda i, p: (i, 0)) for w in ws],
        out_specs=[pl.BlockSpec((None, w.shape[0] // steps, w.shape[1]), lambda i, p: (p[1], i, 0)) for w in ws],
        out_shape=[jax.ShapeDtypeStruct((4, *w.shape), jnp.bfloat16) for w in ws],
        semantics=("parallel",), args=tuple(ws), jobs=jobs)


def _pair_add(g, theirs, pos, *, name, jobs=()):
    n, h, c = theirs.shape
    tr = _row_tile(h, cap=512)
    nb = h // tr

    def body(pos_ref, a_ref, b_ref, o_ref):
        o_ref[...] = (a_ref[...].astype(jnp.float32) + b_ref[...].astype(jnp.float32)).astype(o_ref.dtype)

    spec = pl.BlockSpec((None, tr, c), lambda j, i, p: (j, i, 0))
    return _call(
        body, name=name, grid=(n, nb), prefetch=pos,
        in_specs=[pl.BlockSpec((None, tr, c), lambda j, i, p: (j, i + p[0] * nb, 0)), spec],
        out_specs=[spec], out_shape=[jax.ShapeDtypeStruct(theirs.shape, jnp.bfloat16)],
        semantics=("parallel", "parallel"), args=(g, theirs), jobs=jobs)[0]


def _fold_diag(pair_sum, via, pos, *, name):
    n, h, c = pair_sum.shape
    tr = _row_tile(h // 2, cap=512)
    nb = h // tr

    def body(pos_ref, p_ref, via_ref, o_ref):
        j, i = pl.program_id(0), pl.program_id(1)
        mine = p_ref[...].astype(jnp.float32)
        add = (j == 0) == (i >= nb // 2)
        o_ref[...] = jnp.where(add, mine + via_ref[...].astype(jnp.float32), mine).astype(o_ref.dtype)

    return _call(
        body, name=name, grid=(2, nb), prefetch=pos,
        in_specs=[pl.BlockSpec((None, tr, c), lambda j, i, p: (jnp.bitwise_xor(p[1], 2 - j), i, 0)),
                  pl.BlockSpec((tr, c), lambda j, i, p: (i, 0))],
        out_specs=[pl.BlockSpec((None, tr, c), lambda j, i, p: (j, i, 0))],
        out_shape=[jax.ShapeDtypeStruct((2, h, c), jnp.bfloat16)],
        semantics=("parallel", "parallel"), args=(pair_sum, via))[0]


def _chip_sum(by_source, pair_sum, pos, *, name):
    n, h, c = by_source.shape
    tr = _row_tile(h, cap=512)
    nb = h // tr

    def body(pos_ref, own_ref, px_ref, py_ref, o_ref):
        f = lambda ref: ref[...].astype(jnp.float32)
        o_ref[...] = (f(own_ref) + f(px_ref)) + f(py_ref)

    slot = lambda flip: pl.BlockSpec((None, tr, c), lambda i, p: (jnp.bitwise_xor(p[1], flip), i, 0))
    return pl.pallas_call(
        body, name=name, out_shape=jax.ShapeDtypeStruct((2 * h, c), jnp.float32),
        grid_spec=pltpu.PrefetchScalarGridSpec(
            num_scalar_prefetch=1, grid=(nb,),
            in_specs=[slot(0), slot(2), slot(1)],
            out_specs=pl.BlockSpec((tr, c), lambda i, p: (i + p[0] * nb, 0))),
        compiler_params=_params(("parallel",)),
    )(pos, pair_sum, by_source, by_source)


def _adamw(w, g, m, v, *, name, jobs=()):
    r, c = w.shape
    tr = _row_tile(r)

    def body(w_ref, g_ref, m_ref, v_ref, d_ref, mo_ref, vo_ref, go_ref):
        gv = g_ref[...]
        go_ref[...] = gv
        mn = ADAM_B1 * m_ref[...] + (1.0 - ADAM_B1) * gv
        vn = ADAM_B2 * v_ref[...] + (1.0 - ADAM_B2) * (gv * gv)
        m_hat = mn / (1.0 - ADAM_B1 ** ADAM_STEP)
        v_hat = vn / (1.0 - ADAM_B2 ** ADAM_STEP)
        d_ref[...] = -ADAM_LR * (m_hat / (jnp.sqrt(v_hat) + ADAM_EPS) + ADAM_WD * w_ref[...])
        mo_ref[...] = mn
        vo_ref[...] = vn

    spec = pl.BlockSpec((tr, c), lambda i: (i, 0))
    out = jax.ShapeDtypeStruct((r, c), jnp.float32)
    return _call(body, name=name, grid=(r // tr,), in_specs=[spec] * 4, out_specs=[spec] * 4, out_shape=[out] * 4,
                 semantics=("parallel",), args=(w, g, m, v), jobs=jobs)


def _all_reduce_small(block):
    r, c = block.shape

    def body(src_ref, out_ref, stage_ref, send_sems, recv_sems):
        x, y, cc = _position()
        me = 4 * x + 2 * y + cc
        stage_ref[me] = src_ref[...]
        flip = lambda v, on: 1 - v if on else v
        peers = [(flip(x, dx), flip(y, dy), flip(cc, dc)) for dx in (0, 1) for dy in (0, 1) for dc in (0, 1)][1:]
        copies = [pltpu.make_async_remote_copy(
            src_ref=stage_ref.at[me], dst_ref=stage_ref.at[me],
            send_sem=send_sems.at[k], recv_sem=recv_sems.at[k], device_id=peer, device_id_type=MESH)
            for k, peer in enumerate(peers)]
        for cp in copies:
            cp.start()
        for k, (px, py, pc) in enumerate(peers):
            them = 4 * px + 2 * py + pc
            pltpu.make_async_remote_copy(
                src_ref=stage_ref.at[them], dst_ref=stage_ref.at[them],
                send_sem=send_sems.at[k], recv_sem=recv_sems.at[k], device_id=(px, py, pc),
                device_id_type=MESH).wait_recv()
        for cp in copies:
            cp.wait_send()
        total = stage_ref[0]
        for d in range(1, 8):
            total = total + stage_ref[d]
        out_ref[...] = total

    return pl.pallas_call(
        body, name="all_reduce_small",
        in_specs=[pl.BlockSpec(memory_space=pltpu.VMEM)], out_specs=pl.BlockSpec(memory_space=pltpu.VMEM),
        out_shape=jax.ShapeDtypeStruct((r, c), jnp.float32),
        scratch_shapes=[pltpu.VMEM((8, r, c), jnp.float32), pltpu.SemaphoreType.DMA((7,)),
                        pltpu.SemaphoreType.DMA((7,))],
        compiler_params=pltpu.CompilerParams(has_side_effects=True),
    )(block)


def _cols_from_shards(g):
    n, r, c = g.shape
    return jnp.transpose(g, (1, 0, 2)).reshape(r, n * c)


def _cols_to_shards(w, n=4):
    r, c = w.shape
    return jnp.transpose(w.reshape(r, n, c // n), (1, 0, 2))


def _pad_w_in(g):
    _, d, c = g.shape
    cut = COL_KR - 3 * c
    zeros = jnp.zeros((d, COL_CKV - COL_KR - QK_ROPE), g.dtype)
    return jnp.concatenate([g[0], g[1], g[2], g[3][:, :cut], g[3][:, cut + KV_RANK:], zeros,
                            g[3][:, cut:cut + KV_RANK]], axis=1)


def _unpad_w_in(padded):
    c = IN_WIDTH // 4
    last = jnp.concatenate([padded[:, 3 * c:COL_KR], padded[:, COL_CKV:COL_CKV + KV_RANK],
                            padded[:, COL_KR:COL_KR + QK_ROPE]], axis=1)
    return jnp.stack([padded[:, 0:c], padded[:, c:2 * c], padded[:, 2 * c:3 * c], last])


def _pad_w_uq(full):
    r = full.shape[0]
    per_head = full.reshape(r, N_HEADS, QK_NOPE + QK_ROPE)
    return jnp.pad(per_head, ((0, 0), (0, 0), (0, HEAD_PAD - QK_NOPE - QK_ROPE))).reshape(r, N_HEADS * HEAD_PAD)


def _unpad_w_uq(padded):
    r = padded.shape[0]
    return padded.reshape(r, N_HEADS, HEAD_PAD)[:, :, :QK_NOPE + QK_ROPE].reshape(r, N_HEADS * (QK_NOPE + QK_ROPE))


SMALL_ROWS = 16


def _pack_small(d, pre_mix, post_mix, pre_mlp, post_mlp, conv_out, attn_out, q_norm, kv_norm, conv_w):
    row = lambda *parts: jnp.pad(jnp.concatenate(parts, axis=1), ((0, 0), (0, d - sum(p.shape[1] for p in parts))))
    rows = [row(pre_mix), row(post_mix), row(pre_mlp), row(post_mlp), row(conv_out, attn_out), row(q_norm, kv_norm),
            row(conv_w[0:1]), row(conv_w[1:2]), row(conv_w[2:3])]
    return jnp.pad(jnp.concatenate(rows, axis=0), ((0, SMALL_ROWS - len(rows)), (0, 0)))


def _unpack_small(p, chip):
    cw = CONV_WIDTH // 4
    conv_w = lax.dynamic_slice(p[6:9, :CONV_WIDTH], (0, chip * cw), (3, cw))
    return dict(pre_mix_g=p[0:1], post_mix_g=p[1:2], pre_mlp_g=p[2:3], post_mlp_g=p[3:4],
                conv_out_g=p[4:5, :CONV_WIDTH], attn_out_g=p[4:5, CONV_WIDTH:2 * CONV_WIDTH],
                q_norm_g=p[5:6, :Q_RANK], kv_norm_g=p[5:6, Q_RANK:Q_RANK + KV_RANK], conv_w=conv_w[None])


def kernel(x, pre_mix_g, w_in, conv_w, q_norm_g, w_uq, kv_norm_g, w_ukv, conv_out_g, attn_out_g, w_o, post_mix_g, pre_mlp_g, w_up, w_down, post_mlp_g, loss_target, m_pre_mix_g, m_w_in, m_conv_w, m_q_norm_g, m_w_uq, m_kv_norm_g, m_w_ukv, m_conv_out_g, m_attn_out_g, m_w_o, m_post_mix_g, m_pre_mlp_g, m_w_up, m_w_down, m_post_mlp_g, v_pre_mix_g, v_w_in, v_conv_w, v_q_norm_g, v_w_uq, v_kv_norm_g, v_w_ukv, v_conv_out_g, v_attn_out_g, v_w_o, v_post_mix_g, v_pre_mlp_g, v_w_up, v_w_down, v_post_mlp_g):
    bf16 = jnp.bfloat16
    s, d = x.shape[1], x.shape[2]
    d_ff = 4 * d
    chip = 2 * lax.axis_index("x") + lax.axis_index("y")
    xs = x.reshape(s, d)
    target = loss_target.reshape(s, d)
    tm = min(256, s)
    t_attn = min(1024, s)
    mt = min(1024, s)
    kt = min(2048, s)

    big = dict(w_in=w_in[0], w_uq=w_uq[0], w_ukv=w_ukv[0], w_o=w_o[0], w_up=w_up[0], w_down=w_down[0])
    names = list(big)
    pos = jnp.stack([lax.axis_index("c"), chip]).astype(jnp.int32)
    wb = {}
    half = {k: big[k].shape[0] // 2 for k in names}

    def rows(k, a, b):
        lo, n = a * half[k] // 64, (b - a) * half[k] // 64
        assert lo % 16 == 0 and n % 16 == 0 and n > 0, (k, a, b)
        return lo, n

    ici = lambda k, a=0, b=64: _GatherIci(wb[k], *rows(k, a, b))
    fwd = lambda k, a=0, b=64: _GatherForward(wb[k], *rows(k, a, b))
    near = lambda k, a=0, b=64: _GatherD2d(wb[k], *rows(k, a, b), slots=(0, 1))
    far = lambda k, a=0, b=64: _GatherD2d(wb[k], *rows(k, a, b), slots=(2,))
    tap_rows = 64
    half["conv_w"] = tap_rows // 2
    wb["conv_w"] = _Buf(lax.dynamic_update_slice(
        jnp.zeros((4, tap_rows, CONV_WIDTH // 4), jnp.float32),
        jnp.pad(conv_w[0], ((0, tap_rows - conv_w.shape[1]), (0, 0)))[None], (chip, 0, 0)))
    wb["w_in"] = _Buf(_cast_into_slot(big["w_in"], pos, name="cast_w_in"))
    rest = [k for k in names if k != "w_in"]
    for k, slot in zip(rest, _cast_many_into_slots([big[k] for k in rest], pos, name="cast_rest",
                                                   jobs=[ici("w_in"), ici("conv_w")])):
        wb[k] = _Buf(slot)
    h1 = _rms_fwd(xs, pre_mix_g, width=d, col=0, tm=tm, name="rms_pre_mix",
                  jobs=[fwd("w_in"), near("w_in"), fwd("conv_w"), near("conv_w")])
    _comm("gather_w_in", [[far("w_in"), far("conv_w")]])
    win = _pad_w_in(wb["w_in"].arr)
    conv_w_full = jnp.transpose(wb["conv_w"].arr[:, :conv_w.shape[1], :], (1, 0, 2)).reshape(-1, CONV_WIDTH)
    spread = lambda a: lax.dynamic_update_slice(jnp.zeros((3, CONV_WIDTH), jnp.float32), a[0],
                                                (0, chip * (CONV_WIDTH // 4)))
    ff4 = d_ff // 4

    proj = _matmul(h1, win, dims=_NN, mnk=(s, IN_PAD, d), tiles=(mt, IN_PAD // 3, d), name="mm_proj",
                   jobs=[ici("w_uq"), ici("w_ukv"), ici("w_o")])
    y_conv = _conv_fwd(proj, conv_w_full, conv_out_g,
                       jobs=[fwd("w_uq"), fwd("w_ukv"), fwd("w_o"), near("w_uq"), near("w_ukv"), near("w_o")])
    cqn = _rms_fwd(proj, q_norm_g, width=Q_RANK, col=COL_CQ // Q_RANK, tm=mt, name="rms_q",
                   jobs=[far("w_uq"), far("w_ukv"), far("w_o"), ici("w_up", 0, 4)])
    wuq = _pad_w_uq(_cols_from_shards(wb["w_uq"].arr))
    wukv = _cols_from_shards(wb["w_ukv"].arr)
    wo = wb["w_o"].arr.reshape(-1, d)
    ckvn = _rms_fwd(proj, kv_norm_g, width=KV_RANK, col=COL_CKV // KV_RANK, tm=mt, name="rms_kv",
                    jobs=[ici("w_up", 4, 10), fwd("w_up", 0, 4)])
    q_pad = _matmul(cqn, wuq, dims=_NN, mnk=(s, N_HEADS * HEAD_PAD, Q_RANK), tiles=(mt, 1024, Q_RANK), name="mm_q",
                    jobs=[ici("w_up", 10, 18), fwd("w_up", 4, 10), near("w_up", 0, 4)])
    kv = _matmul(ckvn, wukv, dims=_NN, mnk=(s, N_HEADS * HEAD_PAD, KV_RANK), tiles=(mt, 1024, KV_RANK),
                 name="mm_kv", out_dtype=bf16,
                 jobs=[ici("w_up", 18, 24), fwd("w_up", 10, 18), near("w_up", 4, 10), far("w_up", 0, 4)])
    cos, sin = _rope_tables(s)
    q_rot, kr_rot = _qk_rope_fwd(
        q_pad, proj, cos, sin, tm=tm,
        jobs=[ici("w_up", 24, 32), fwd("w_up", 18, 24), near("w_up", 10, 18), far("w_up", 4, 10)])
    o, lse, y_attn = _attn_fwd(
        q_rot, kv, kr_rot, attn_out_g, t=t_attn,
        jobs=[ici("w_up", 32, 64), ici("w_down", 0, 8), fwd("w_up", 24, 32), near("w_up", 18, 24), far("w_up", 10, 18)])
    ycat = jnp.concatenate([y_conv, y_attn], axis=1)
    mix = _matmul(ycat, wo, dims=_NN, mnk=(s, d, 2 * CONV_WIDTH), tiles=(mt, 1024, 2 * CONV_WIDTH), name="mm_out",
                  jobs=[fwd("w_up", 32, 64), near("w_up", 24, 32), far("w_up", 18, 24)])
    x2, h2 = _mix_residual_fwd(
        xs, mix, post_mix_g, pre_mlp_g, tm=tm,
        jobs=[ici("w_down", 8, 20), fwd("w_down", 0, 8), near("w_up", 32, 64), far("w_up", 24, 64)])
    wup = wb["w_up"].arr

    def up_epilogue(acc, extra_refs, out_refs):
        r = jnp.maximum(acc, 0.0)
        out_refs[0][...] = acc.astype(bf16)
        out_refs[1][...] = (r * r).astype(bf16)

    n_ff = ff4 // 1024
    act = jax.ShapeDtypeStruct((s, d_ff), bf16)
    up, act_sq = _matmul(
        h2, wup, dims=_NN, mnk=(s, d_ff, d), tiles=(mt, 1024, d), name="mm_up",
        b_spec=pl.BlockSpec((None, d, 1024), lambda i, j, l: (j // n_ff, l, j % n_ff)),
        out_shape=(act, act), o_spec=(pl.BlockSpec((mt, 1024), lambda i, j, l: (i, j)),) * 2, epilogue=up_epilogue,
        jobs=[ici("w_down", 20, 64), fwd("w_down", 8, 20), near("w_down", 0, 8)])
    _comm("gather_w_down_tail", [[fwd("w_down", 20, 64), near("w_down", 8, 64), far("w_down", 0, 20)],
                                 [far("w_down", 20, 64)]])
    wdown = wb["w_down"].arr.reshape(d_ff, d)
    mlp = _matmul(act_sq, wdown, dims=_NN, mnk=(s, d, d_ff), tiles=(min(512, s), 512, d_ff), name="mm_down")
    dx3, dmlp, dg_post_mlp, loss_part = _loss_head(x2, mlp, target, post_mlp_g, tm=tm)

    def dup_epilogue(acc, extra_refs, out_refs):
        out_refs[0][...] = (acc * (2.0 * jnp.maximum(extra_refs[0][...].astype(jnp.float32), 0.0))).astype(bf16)

    grads, theirs, pair_sums, via, folded, by_source, whole = {}, {}, {}, {}, {}, {}, {}

    def exchange(k, g):
        grads[k] = g
        theirs[k] = _Buf(jax.ShapeDtypeStruct((4, g.shape[1] // 2, g.shape[2]), bf16))
        return _PairExchange(g, theirs[k])

    def pair_sum(k, jobs=()):
        pair_sums[k] = _pair_add(grads[k], theirs[k].arr, pos, name="pair_add_" + k, jobs=jobs)
        via[k] = _Buf(jax.ShapeDtypeStruct(pair_sums[k].shape[1:], bf16))
        by_source[k] = _Buf(jax.ShapeDtypeStruct(pair_sums[k].shape, bf16))

    def diag(k, a=0, b=32):
        lo, n = a * half[k] // 64, (b - a) * half[k] // 64
        assert lo % 16 == 0 and n % 16 == 0 and n > 0, (k, a, b)
        return _ScatterDiag(pair_sums[k], via[k], lo, n)

    def fold(k):
        folded[k] = _fold_diag(pair_sums[k], via[k].arr, pos, name="fold_" + k)

    scatter = lambda k, a=0, b=64: _ScatterNear(folded[k], by_source[k], *rows(k, a, b))

    def share(k):
        whole[k] = _Buf(_chip_sum(by_source[k].arr, pair_sums[k], pos, name="chip_sum_" + k))
        return _PairShare(whole[k])

    g_wdown = _matmul(act_sq, dmlp, dims=_TN, mnk=(d_ff, d, s), tiles=(1024, 1024, kt), name="mm_gw_down",
                      out_dtype=bf16).reshape(4, ff4, d)
    dup = _matmul(dmlp, wdown, dims=_NT, mnk=(s, d_ff, d), tiles=(mt, 1024, d), name="mm_dact",
                  out_dtype=bf16, epilogue=dup_epilogue, extra=(up,),
                  extra_specs=(pl.BlockSpec((mt, 1024), lambda i, j, l: (i, j)),),
                  jobs=[exchange("w_down", g_wdown)])
    pair_sum("w_down")
    g_wup = _matmul(h2, dup, dims=_TN, mnk=(d, d_ff, s), tiles=(1024, 1024, kt), name="mm_gw_up",
                    out_shape=jax.ShapeDtypeStruct((4, d, ff4), bf16),
                    o_spec=pl.BlockSpec((None, 1024, 1024), lambda i, j, l: (j // n_ff, i, j % n_ff)),
                    jobs=[diag("w_down")])
    fold("w_down")
    dh2 = _matmul(dup, wup, dims=_NT, mnk=(s, d, d_ff), tiles=(mt, 1024, ff4), name="mm_dh2",
                  b_spec=pl.BlockSpec((None, 1024, ff4), lambda i, j, l: (l, j, 0)),
                  jobs=[exchange("w_up", g_wup), scatter("w_down", 0, 48)])
    pair_sum("w_up", jobs=[scatter("w_down", 48, 64)])
    dx2, dmix, dg_pre_mlp, dg_post_mix = _mix_residual_bwd(
        dx3, dh2, x2, mix, pre_mlp_g, post_mix_g, tm=tm, jobs=[diag("w_up", 0, 20)])

    dycat = _matmul(dmix, wo, dims=_NT, mnk=(s, 2 * CONV_WIDTH, d), tiles=(mt, 1024, d), name="mm_dycat",
                    jobs=[diag("w_up", 20, 32)])
    fold("w_up")
    g_wo = _matmul(ycat, dmix, dims=_TN, mnk=(2 * CONV_WIDTH, d, s), tiles=(1024, 1024, kt),
                   name="mm_gw_out", out_dtype=bf16, jobs=[scatter("w_up", 0, 12)]).reshape(4, CONV_WIDTH // 2, d)
    du, dgb, dgc, dg_conv_w, dg_conv_out = _conv_bwd(proj, conv_w_full, conv_out_g, dycat,
                                                     jobs=[exchange("w_o", g_wo), scatter("w_up", 12, 24)])
    pair_sum("w_o")
    do, delta, dg_attn_out = _attn_norm_bwd(o, attn_out_g, dycat)
    dq_pad, dk_pad, dv = _attn_bwd(q_rot, kv, kr_rot, do, lse, delta, t=min(1024, s),
                                   jobs=[scatter("w_up", 24, 64), diag("w_o"), share("w_down")])
    fold("w_o")
    dq_raw, dkv, dkr = _qk_rope_bwd(dq_pad, dk_pad, dv, cos, sin, tm=tm, jobs=[scatter("w_o")])
    wq_cols = N_HEADS * HEAD_PAD
    g_wuq = _matmul(cqn, dq_raw, dims=_TN, mnk=(Q_RANK, wq_cols, s), tiles=(Q_RANK, 1024, kt),
                    name="mm_gw_uq", out_dtype=bf16)
    dcqn = _matmul(dq_raw, wuq, dims=_NT, mnk=(s, Q_RANK, wq_cols), tiles=(mt, Q_RANK, wq_cols), name="mm_dcq")
    g_wukv = _matmul(ckvn, dkv, dims=_TN, mnk=(KV_RANK, wq_cols, s), tiles=(KV_RANK, 1024, kt),
                     name="mm_gw_ukv", out_dtype=bf16)
    dckvn = _matmul(dkv, wukv, dims=_NT, mnk=(s, KV_RANK, wq_cols), tiles=(mt, KV_RANK, wq_cols), name="mm_dckv",
                    jobs=[exchange("w_uq", _cols_to_shards(_unpad_w_uq(g_wuq))),
                          exchange("w_ukv", _cols_to_shards(g_wukv))])
    pair_sum("w_uq")
    pair_sum("w_ukv")
    dcq, dg_q_norm = _rms_bwd(proj, q_norm_g, dcqn, width=Q_RANK, col=COL_CQ // Q_RANK, tm=mt, name="rms_q_bwd",
                              jobs=[diag("w_uq"), diag("w_ukv")])
    fold("w_uq")
    fold("w_ukv")
    dckv, dg_kv_norm = _rms_bwd(proj, kv_norm_g, dckvn, width=KV_RANK, col=COL_CKV // KV_RANK, tm=mt,
                                name="rms_kv_bwd")
    dproj = jnp.concatenate([du, dgb, dgc, dcq, dkr, jnp.zeros((s, COL_CKV - COL_KR - 128), bf16), dckv], axis=1)
    g_win = _matmul(h1, dproj, dims=_TN, mnk=(d, IN_PAD, s), tiles=(1024, IN_PAD // 3, kt), name="mm_gw_in",
                    out_dtype=bf16, jobs=[scatter("w_uq"), scatter("w_ukv"), share("w_up"), share("w_o")])
    _comm("pair_exchange_w_in", [[exchange("w_in", _unpad_w_in(g_win))]])
    pair_sum("w_in")
    _comm("scatter_diag_w_in", [[diag("w_in")]])
    fold("w_in")
    dh1 = _matmul(dproj, win, dims=_NT, mnk=(s, d, IN_PAD), tiles=(mt, 1024, IN_PAD // 2), name="mm_dh1",
                  jobs=[scatter("w_in"), share("w_uq"), share("w_ukv")])
    grad_x, dg_pre_mix = _input_bwd(dx2, dh1, xs, pre_mix_g, tm=tm)
    _comm("pair_share_w_in", [[share("w_in")]])

    moments = dict(w_in=(m_w_in, v_w_in), w_uq=(m_w_uq, v_w_uq), w_ukv=(m_w_ukv, v_w_ukv), w_o=(m_w_o, v_w_o),
                   w_up=(m_w_up, v_w_up), w_down=(m_w_down, v_w_down))
    gw, dw, nm, nv = {}, {}, {}, {}
    for k in names:
        delta_k, nm_k, nv_k, g = _adamw(big[k], whole[k].arr, moments[k][0][0], moments[k][1][0], name="adamw_" + k)
        gw[k], dw[k], nm[k], nv[k] = g[None], delta_k[None], nm_k[None], nv_k[None]

    small_g = _all_reduce_small(_pack_small(d, dg_pre_mix, dg_post_mix, dg_pre_mlp, dg_post_mlp, dg_conv_out,
                                            dg_attn_out, dg_q_norm, dg_kv_norm, dg_conv_w
                                            ).at[SMALL_ROWS - 1, :128].set(loss_part[0]))
    loss = small_g[SMALL_ROWS - 1, 0]
    pack_w = lambda cw, pre_mix, post_mix, pre_mlp, post_mlp, conv_out, attn_out, q_norm, kv_norm: _pack_small(
        d, pre_mix, post_mix, pre_mlp, post_mlp, conv_out, attn_out, q_norm, kv_norm, cw)
    small_w = pack_w(conv_w_full, pre_mix_g, post_mix_g, pre_mlp_g, post_mlp_g, conv_out_g, attn_out_g, q_norm_g,
                     kv_norm_g)
    small_m = pack_w(spread(m_conv_w), m_pre_mix_g, m_post_mix_g, m_pre_mlp_g, m_post_mlp_g, m_conv_out_g,
                     m_attn_out_g, m_q_norm_g, m_kv_norm_g)
    small_v = pack_w(spread(v_conv_w), v_pre_mix_g, v_post_mix_g, v_pre_mlp_g, v_post_mlp_g, v_conv_out_g,
                     v_attn_out_g, v_q_norm_g, v_kv_norm_g)
    small_d, small_nm, small_nv, small_g = _adamw(small_w, small_g, small_m, small_v, name="adamw_small")
    sg, sd, snm, snv = (_unpack_small(p, chip) for p in (small_g, small_d, small_nm, small_nv))

    for src, dst in ((sg, gw), (sd, dw), (snm, nm), (snv, nv)):
        dst.update(src)

    order = ["pre_mix_g", "w_in", "conv_w", "q_norm_g", "w_uq", "kv_norm_g", "w_ukv", "conv_out_g", "attn_out_g",
             "w_o", "post_mix_g", "pre_mlp_g", "w_up", "w_down", "post_mlp_g"]
    return (loss, grad_x.reshape(1, s, d), *[gw[k] for k in order], *[dw[k] for k in order],
            *[nm[k] for k in order], *[nv[k] for k in order])
```

```python
import functools

import jax
import jax.numpy as jnp
from jax import lax
from jax.experimental import pallas as pl
from jax.experimental.pallas import tpu as pltpu

EPS = 1e-6
NEG_INF = -1e30
CHUNK_SHIFT = 6
N_HEADS = 8
HEAD_PAD = 256
QK_NOPE = 128
QK_ROPE = 64
V_DIM = 128
CONV_WIDTH = 1024
Q_RANK = 768
KV_RANK = 512
ROPE_THETA = 10000.0
ATTN_SCALE = (QK_NOPE + QK_ROPE) ** -0.5
ADAM_LR, ADAM_B1, ADAM_B2, ADAM_EPS, ADAM_WD, ADAM_STEP = 0.001, 0.9, 0.999, 1e-08, 0.01, 10

COL_CQ = 3 * CONV_WIDTH
COL_KR = COL_CQ + Q_RANK
COL_CKV = 4096
IN_PAD = COL_CKV + KV_RANK
IN_WIDTH = 3 * CONV_WIDTH + Q_RANK + KV_RANK + QK_ROPE

VMEM_LIMIT_BYTES = 56 * 1024 * 1024
MESH = pl.DeviceIdType.MESH
ANY = pl.BlockSpec(memory_space=pl.ANY)

_NN = (((1,), (0,)), ((), ()))
_NT = (((1,), (1,)), ((), ()))
_TN = (((0,), (0,)), ((), ()))


def _params(sem):
    return pltpu.CompilerParams(dimension_semantics=sem, vmem_limit_bytes=VMEM_LIMIT_BYTES)


class _Buf:
    def __init__(self, arr):
        self.arr = arr


def _position():
    return lax.axis_index("x"), lax.axis_index("y"), lax.axis_index("c")


def _other_chips(x, y):
    return [(2 * (1 - x) + y, (1 - x, y)), (2 * x + (1 - y), (x, 1 - y)), (2 * (1 - x) + (1 - y), (1 - x, 1 - y))]


def _remote(src, dst, sems, k, to):
    send, recv, off = sems
    return pltpu.make_async_remote_copy(src_ref=src, dst_ref=dst, send_sem=send.at[off + k], recv_sem=recv.at[off + k],
                                        device_id=to, device_id_type=MESH)


class _GatherIci:
    n_sems = 2
    link = "ici"

    def __init__(self, buf, lo, n):
        self.buf, self.lo, self.n, self.buffers, self.sources = buf, lo, n, [buf], []

    def _rows(self, ref, slot, which, lo=None, n=None):
        lo, n = (self.lo, self.n) if lo is None else (lo, n)
        return ref.at[slot, pl.ds(which * (ref.shape[1] // 2) + lo, n), :]

    def start(self, refs, sems):
        ref = refs[id(self.buf)]
        x, y, c = _position()
        mine = self._rows(ref, 2 * x + y, c)
        for k, (_, xy) in enumerate(_other_chips(x, y)[:2]):
            _remote(mine, mine, sems, k, (*xy, c)).start()

    def wait(self, refs, sems):
        ref = refs[id(self.buf)]
        x, y, c = _position()
        mine = self._rows(ref, 2 * x + y, c)
        for k, (slot, xy) in enumerate(_other_chips(x, y)[:2]):
            landed = self._rows(ref, slot, c)
            _remote(landed, landed, sems, k, (*xy, c)).wait_recv()
            _remote(mine, mine, sems, k, (*xy, c)).wait_send()


class _GatherForward(_GatherIci):
    def _ways(self, x, y):
        (slot_x, xy_x), (slot_y, xy_y), (slot_d, _) = _other_chips(x, y)
        h = self.n // 2
        assert h % 16 == 0, self.n
        return [(slot_x, slot_d, self.lo, xy_y), (slot_y, slot_d, self.lo + h, xy_x)], h

    def start(self, refs, sems):
        ref = refs[id(self.buf)]
        x, y, c = _position()
        ways, h = self._ways(x, y)
        for k, (slot, _, lo, xy) in enumerate(ways):
            rows = self._rows(ref, slot, c, lo, h)
            _remote(rows, rows, sems, k, (*xy, c)).start()

    def wait(self, refs, sems):
        ref = refs[id(self.buf)]
        x, y, c = _position()
        ways, h = self._ways(x, y)
        for k, (slot, lands, lo, xy) in enumerate(ways):
            landed, sent = self._rows(ref, lands, c, lo, h), self._rows(ref, slot, c, lo, h)
            _remote(landed, landed, sems, k, (*xy, c)).wait_recv()
            _remote(sent, sent, sems, k, (*xy, c)).wait_send()


class _GatherD2d(_GatherIci):
    link = "d2d"

    def __init__(self, buf, lo, n, slots):
        super().__init__(buf, lo, n)
        self.slots, self.n_sems = slots, len(slots)

    def start(self, refs, sems):
        ref = refs[id(self.buf)]
        x, y, c = _position()
        chips = _other_chips(x, y)
        for k, which in enumerate(self.slots):
            rows = self._rows(ref, chips[which][0], c)
            _remote(rows, rows, sems, k, (x, y, 1 - c)).start()

    def wait(self, refs, sems):
        ref = refs[id(self.buf)]
        x, y, c = _position()
        chips = _other_chips(x, y)
        for k, which in enumerate(self.slots):
            sent, landed = self._rows(ref, chips[which][0], c), self._rows(ref, chips[which][0], 1 - c)
            _remote(landed, landed, sems, k, (x, y, 1 - c)).wait_recv()
            _remote(sent, sent, sems, k, (x, y, 1 - c)).wait_send()


class _ScatterDiag:
    n_sems = 2
    link = "ici"

    def __init__(self, src, via, lo, n):
        self.src, self.via, self.lo, self.n, self.buffers, self.sources = src, via, lo, n, [via], [src]

    def _copies(self, refs, sems):
        src, via = refs[id(self.src)], refs[id(self.via)]
        x, y, c = _position()
        (_, xy_x), (_, xy_y), (slot_d, _) = _other_chips(x, y)
        h2 = via.shape[0] // 2
        return [_remote(src.at[slot_d, pl.ds(first + self.lo, self.n), :], via.at[pl.ds(first + self.lo, self.n), :],
                        sems, k, (*xy, c)) for k, (first, xy) in enumerate(((0, xy_x), (h2, xy_y)))]

    def start(self, refs, sems):
        for cp in self._copies(refs, sems):
            cp.start()

    def wait(self, refs, sems):
        for cp in self._copies(refs, sems):
            cp.wait_recv()
            cp.wait_send()


class _ScatterNear:
    n_sems = 2
    link = "ici"

    def __init__(self, src, dst, lo, n):
        self.src, self.dst, self.lo, self.n, self.buffers, self.sources = src, dst, lo, n, [dst], [src]

    def _copies(self, refs, sems, landing):
        src, dst = refs[id(self.src)], refs[id(self.dst)]
        x, y, c = _position()
        rows = pl.ds(self.lo, self.n)
        return [_remote(src.at[k, rows, :], dst.at[slot if landing else 2 * x + y, rows, :], sems, k, (*xy, c))
                for k, (slot, xy) in enumerate(_other_chips(x, y)[:2])]

    def start(self, refs, sems):
        for cp in self._copies(refs, sems, False):
            cp.start()

    def wait(self, refs, sems):
        for cp in self._copies(refs, sems, True):
            cp.wait_recv()
            cp.wait_send()


class _PairExchange:
    n_sems = 1
    link = "d2d"

    def __init__(self, src, dst):
        self.src, self.dst, self.buffers, self.sources = src, dst, [dst], [src]

    def _copy(self, refs, sems):
        src, dst = refs[id(self.src)], refs[id(self.dst)]
        x, y, c = _position()
        h = src.shape[1] // 2
        return _remote(src.at[:, pl.ds((1 - c) * h, h), :], dst, sems, 0, (x, y, 1 - c))

    def start(self, refs, sems):
        self._copy(refs, sems).start()

    def wait(self, refs, sems):
        self._copy(refs, sems).wait()


class _PairShare:
    n_sems = 1
    link = "d2d"

    def __init__(self, buf):
        self.buf, self.buffers, self.sources = buf, [buf], []

    def _rows(self, ref, which):
        h = ref.shape[0] // 2
        return ref.at[pl.ds(which * h, h), :]

    def start(self, refs, sems):
        ref = refs[id(self.buf)]
        x, y, c = _position()
        _remote(self._rows(ref, c), self._rows(ref, c), sems, 0, (x, y, 1 - c)).start()

    def wait(self, refs, sems):
        ref = refs[id(self.buf)]
        x, y, c = _position()
        _remote(self._rows(ref, c), self._rows(ref, c), sems, 0, (x, y, 1 - c)).wait_send()
        _remote(self._rows(ref, 1 - c), self._rows(ref, 1 - c), sems, 0, (x, y, 1 - c)).wait_recv()


_COLLECTIVE_IDS = {("ici",): 1, ("d2d",): 2, ("d2d", "ici"): 3}


def _links(jobs):
    return tuple(sorted({j.link for j in jobs}))


def _handshake(links):
    x, y, c = _position()
    peers = ([(1 - x, y, c), (x, 1 - y, c)] if "ici" in links else []) + ([(x, y, 1 - c)] if "d2d" in links else [])
    barrier = pltpu.get_barrier_semaphore()
    for peer in peers:
        pl.semaphore_signal(barrier, inc=1, device_id=peer, device_id_type=MESH)
    pl.semaphore_wait(barrier, len(peers))


def _unique(items):
    seen, out = set(), []
    for it in items:
        if id(it) not in seen:
            seen.add(id(it))
            out.append(it)
    return out


def _job_operands(jobs):
    sources = _unique([a for j in jobs for a in j.sources])
    buffers = _unique([b for j in jobs for b in j.buffers])
    held = [b for b in buffers if not isinstance(b.arr, jax.ShapeDtypeStruct)]
    fresh = [b for b in buffers if isinstance(b.arr, jax.ShapeDtypeStruct)]
    return sources, held, fresh


def _sem_offsets(jobs):
    offs, total = [], 0
    for j in jobs:
        offs.append(total)
        total += j.n_sems
    return offs, total


def _call(body, *, name, grid, in_specs, out_specs, out_shape, args, semantics, scratch_shapes=(), jobs=(),
          prefetch=None):
    n_pre = 0 if prefetch is None else 1

    def launch(fn, in_specs, out_specs, scratch, **kw):
        if prefetch is None:
            return pl.pallas_call(fn, name=name, grid=grid, in_specs=in_specs, out_specs=out_specs,
                                  scratch_shapes=scratch, **kw)
        return pl.pallas_call(fn, name=name, grid_spec=pltpu.PrefetchScalarGridSpec(
            num_scalar_prefetch=1, grid=grid, in_specs=in_specs, out_specs=out_specs, scratch_shapes=scratch), **kw)

    pre = () if prefetch is None else (prefetch,)
    if not jobs:
        return launch(body, list(in_specs), list(out_specs), list(scratch_shapes), out_shape=list(out_shape),
                      compiler_params=_params(semantics))(*pre, *args)
    sources, held, fresh = _job_operands(jobs)
    offs, n_sem = _sem_offsets(jobs)
    links = _links(jobs)
    n_in, n_out, n_scr = len(in_specs), len(out_specs), len(scratch_shapes)
    n_src, n_held, n_fresh = len(sources), len(held), len(fresh)

    def carried(*refs):
        pre_refs, refs = refs[:n_pre], refs[n_pre:]
        ins = refs[:n_in]
        src_refs = refs[n_in:n_in + n_src]
        o0 = n_in + n_src + n_held
        outs = refs[o0:o0 + n_out]
        buf_refs = refs[o0 + n_out:o0 + n_out + n_held + n_fresh]
        s0 = o0 + n_out + n_held + n_fresh
        scratch = refs[s0:s0 + n_scr]
        send, recv = refs[s0 + n_scr], refs[s0 + n_scr + 1]
        where = {id(a): r for a, r in zip(sources, src_refs)}
        where.update({id(b): r for b, r in zip(held + fresh, buf_refs)})
        ids = [pl.program_id(a) for a in range(len(grid))]
        first = functools.reduce(jnp.logical_and, [i == 0 for i in ids])
        last = functools.reduce(jnp.logical_and, [i == g - 1 for i, g in zip(ids, grid)])

        @pl.when(first)
        def _():
            _handshake(links)
            for j, off in zip(jobs, offs):
                j.start(where, (send, recv, off))

        body(*pre_refs, *ins, *outs, *scratch)

        @pl.when(last)
        def _():
            for j, off in zip(jobs, offs):
                j.wait(where, (send, recv, off))

    shape_of = lambda b: jax.ShapeDtypeStruct(b.arr.shape, b.arr.dtype)
    res = launch(
        carried, [*in_specs, *[ANY] * (n_src + n_held)], [*out_specs, *[ANY] * (n_held + n_fresh)],
        [*scratch_shapes, pltpu.SemaphoreType.DMA((n_sem,)), pltpu.SemaphoreType.DMA((n_sem,))],
        out_shape=[*out_shape, *[shape_of(b) for b in held + fresh]],
        input_output_aliases={n_pre + n_in + n_src + i: n_out + i for i in range(n_held)},
        compiler_params=pltpu.CompilerParams(dimension_semantics=("arbitrary",) * len(grid),
                                             vmem_limit_bytes=VMEM_LIMIT_BYTES, has_side_effects=True,
                                             collective_id=_COLLECTIVE_IDS[links]),
    )(*pre, *args, *sources, *[b.arr for b in held])
    for b, new in zip(held + fresh, res[n_out:]):
        b.arr = new
    return list(res[:n_out])


def _comm(name, phases):
    jobs = [j for ph in phases for j in ph]
    sources, held, fresh = _job_operands(jobs)
    offs, n_sem = _sem_offsets(jobs)
    off_of = {id(j): o for j, o in zip(jobs, offs)}
    links = _links(jobs)
    n_src, n_held, n_fresh = len(sources), len(held), len(fresh)

    def body(*refs):
        src_refs = refs[:n_src]
        buf_refs = refs[n_src + n_held:n_src + 2 * n_held + n_fresh]
        send, recv = refs[-2], refs[-1]
        where = {id(a): r for a, r in zip(sources, src_refs)}
        where.update({id(b): r for b, r in zip(held + fresh, buf_refs)})
        _handshake(links)
        for ph in phases:
            for j in ph:
                j.start(where, (send, recv, off_of[id(j)]))
            for j in ph:
                j.wait(where, (send, recv, off_of[id(j)]))

    shape_of = lambda b: jax.ShapeDtypeStruct(b.arr.shape, b.arr.dtype)
    res = pl.pallas_call(
        body, name=name,
        in_specs=[ANY] * (n_src + n_held), out_specs=[ANY] * (n_held + n_fresh),
        out_shape=[shape_of(b) for b in held + fresh],
        input_output_aliases={n_src + i: i for i in range(n_held)},
        scratch_shapes=[pltpu.SemaphoreType.DMA((n_sem,)), pltpu.SemaphoreType.DMA((n_sem,))],
        compiler_params=pltpu.CompilerParams(has_side_effects=True, collective_id=_COLLECTIVE_IDS[links]),
    )(*sources, *[b.arr for b in held])
    for b, new in zip(held + fresh, res):
        b.arr = new


def _matmul(a, b, *, dims, mnk, tiles, name, out_dtype=jnp.float32, a_spec=None, b_spec=None,
            out_shape=None, o_spec=None, epilogue=None, extra=(), extra_specs=(), jobs=()):
    m, n, k = mnk
    tm, tn, tk = tiles
    assert m % tm == 0 and n % tn == 0 and k % tk == 0, (name, mnk, tiles)
    gm, gn, gk = m // tm, n // tn, k // tk
    if a_spec is None:
        a_spec = (pl.BlockSpec((tk, tm), lambda i, j, l: (l, i)) if dims is _TN
                  else pl.BlockSpec((tm, tk), lambda i, j, l: (i, l)))
    if b_spec is None:
        b_spec = (pl.BlockSpec((tn, tk), lambda i, j, l: (j, l)) if dims is _NT
                  else pl.BlockSpec((tk, tn), lambda i, j, l: (l, j)))
    if out_shape is None:
        out_shape = jax.ShapeDtypeStruct((m, n), out_dtype)
    if o_spec is None:
        o_spec = pl.BlockSpec((tm, tn), lambda i, j, l: (i, j))
    single = not isinstance(out_shape, (tuple, list))
    n_extra = len(extra)

    def finish(acc, extra_refs, out_refs):
        if epilogue is None:
            out_refs[0][...] = acc.astype(out_refs[0].dtype)
        else:
            epilogue(acc, extra_refs, out_refs)

    def body_whole_k(*refs):
        a_ref, b_ref = refs[0], refs[1]
        acc = lax.dot_general(a_ref[...], b_ref[...], dims, preferred_element_type=jnp.float32)
        finish(acc, refs[2:2 + n_extra], refs[2 + n_extra:])

    def body_split_k(*refs):
        a_ref, b_ref = refs[0], refs[1]
        extra_refs = refs[2:2 + n_extra]
        out_refs = refs[2 + n_extra:-1]
        acc_ref = refs[-1]
        step = pl.program_id(2)
        part = lax.dot_general(a_ref[...], b_ref[...], dims, preferred_element_type=jnp.float32)

        @pl.when(step == 0)
        def _():
            acc_ref[...] = part

        @pl.when(jnp.logical_and(step > 0, step < gk - 1))
        def _():
            acc_ref[...] += part

        @pl.when(step == gk - 1)
        def _():
            finish(acc_ref[...] + part, extra_refs, out_refs)

    res = _call(
        body_whole_k if gk == 1 else body_split_k, name=name, grid=(gm, gn, gk),
        in_specs=[a_spec, b_spec, *extra_specs],
        out_specs=[o_spec] if single else list(o_spec),
        out_shape=[out_shape] if single else list(out_shape),
        scratch_shapes=[] if gk == 1 else [pltpu.VMEM((tm, tn), jnp.float32)],
        semantics=("parallel", "parallel", "arbitrary"), args=(a, b, *extra), jobs=jobs)
    return res[0] if single else res


def _rstd(x):
    return lax.rsqrt(jnp.mean(x * x, axis=-1, keepdims=True) + EPS)


def _rms_bwd_rows(x, g, dy):
    r = _rstd(x)
    xn = x * r
    dyg = dy * g
    dx = r * (dyg - xn * jnp.mean(xn * dyg, axis=-1, keepdims=True))
    return dx, dy * xn


def _acc_rows(ref, rows, first):
    part = jnp.sum(rows, axis=0, keepdims=True)

    @pl.when(first)
    def _():
        ref[...] = part

    @pl.when(jnp.logical_not(first))
    def _():
        ref[...] += part


def _rms_fwd(x, g, *, width, col, tm, name, jobs=()):
    s = x.shape[0]

    def body(x_ref, g_ref, o_ref):
        v = x_ref[...]
        o_ref[...] = (v * _rstd(v) * g_ref[...]).astype(o_ref.dtype)

    return _call(
        body, name=name, grid=(s // tm,),
        in_specs=[pl.BlockSpec((tm, width), lambda i: (i, col)), pl.BlockSpec((1, width), lambda i: (0, 0))],
        out_specs=[pl.BlockSpec((tm, width), lambda i: (i, 0))],
        out_shape=[jax.ShapeDtypeStruct((s, width), jnp.bfloat16)],
        semantics=("parallel",), args=(x, g), jobs=jobs)[0]


def _rms_bwd(x, g, dy, *, width, col, tm, name, jobs=()):
    s = x.shape[0]

    def body(x_ref, g_ref, dy_ref, dx_ref, dg_ref):
        dx, dgr = _rms_bwd_rows(x_ref[...], g_ref[...], dy_ref[...])
        dx_ref[...] = dx.astype(dx_ref.dtype)
        _acc_rows(dg_ref, dgr, pl.program_id(0) == 0)

    return _call(
        body, name=name, grid=(s // tm,),
        in_specs=[pl.BlockSpec((tm, width), lambda i: (i, col)), pl.BlockSpec((1, width), lambda i: (0, 0)),
                  pl.BlockSpec((tm, width), lambda i: (i, 0))],
        out_specs=[pl.BlockSpec((tm, width), lambda i: (i, 0)), pl.BlockSpec((1, width), lambda i: (0, 0))],
        out_shape=[jax.ShapeDtypeStruct((s, width), jnp.bfloat16), jax.ShapeDtypeStruct((1, width), jnp.float32)],
        semantics=("arbitrary",), args=(x, g, dy), jobs=jobs)


def _row_specs(tm, d, n):
    return [pl.BlockSpec((tm, d), lambda i: (i, 0)) for _ in range(n)]


def _gain_specs(d, n):
    return [pl.BlockSpec((1, d), lambda i: (0, 0)) for _ in range(n)]


def _mix_residual_fwd(x, mix, g_post_mix, g_pre_mlp, *, tm, jobs=()):
    s, d = x.shape

    def body(x_ref, mix_ref, g1_ref, g2_ref, x2_ref, h2_ref):
        mixv = mix_ref[...]
        x2 = x_ref[...] + mixv * _rstd(mixv) * g1_ref[...]
        x2_ref[...] = x2
        h2_ref[...] = (x2 * _rstd(x2) * g2_ref[...]).astype(h2_ref.dtype)

    return _call(
        body, name="mix_residual_fwd", grid=(s // tm,),
        in_specs=_row_specs(tm, d, 2) + _gain_specs(d, 2),
        out_specs=_row_specs(tm, d, 2),
        out_shape=[jax.ShapeDtypeStruct((s, d), jnp.float32), jax.ShapeDtypeStruct((s, d), jnp.bfloat16)],
        semantics=("parallel",), args=(x, mix, g_post_mix, g_pre_mlp), jobs=jobs)


def _loss_head(x2, mlp, target, g_post_mlp, *, tm, jobs=()):
    s, d = x2.shape

    def body(x2_ref, m_ref, t_ref, g_ref, dx3_ref, dm_ref, dg_ref, loss_ref):
        first = pl.program_id(0) == 0
        mv = m_ref[...]
        g = g_ref[...]
        diff = x2_ref[...] + mv * _rstd(mv) * g - t_ref[...]
        dx3 = diff * (1.0 / d)
        dx3_ref[...] = dx3
        dm, dgr = _rms_bwd_rows(mv, g, dx3)
        dm_ref[...] = dm.astype(dm_ref.dtype)
        _acc_rows(dg_ref, dgr, first)
        part = 0.5 * jnp.sum(jnp.mean(diff * diff, axis=-1, keepdims=True), axis=0, keepdims=True)
        _acc_rows(loss_ref, jnp.broadcast_to(part, (1, 128)), first)

    return _call(
        body, name="loss_head", grid=(s // tm,),
        in_specs=_row_specs(tm, d, 3) + _gain_specs(d, 1),
        out_specs=_row_specs(tm, d, 2) + _gain_specs(d, 1) + [pl.BlockSpec((1, 128), lambda i: (0, 0))],
        out_shape=[jax.ShapeDtypeStruct((s, d), jnp.float32), jax.ShapeDtypeStruct((s, d), jnp.bfloat16),
                   jax.ShapeDtypeStruct((1, d), jnp.float32), jax.ShapeDtypeStruct((1, 128), jnp.float32)],
        semantics=("arbitrary",), args=(x2, mlp, target, g_post_mlp), jobs=jobs)


def _mix_residual_bwd(dx3, dh2, x2, mix, g_pre_mlp, g_post_mix, *, tm, jobs=()):
    s, d = x2.shape

    def body(dx3_ref, dh2_ref, x2_ref, mix_ref, g2_ref, g1_ref, dx2_ref, dmix_ref, dg2_ref, dg1_ref):
        first = pl.program_id(0) == 0
        d_in, dgr2 = _rms_bwd_rows(x2_ref[...], g2_ref[...], dh2_ref[...])
        dx2 = dx3_ref[...] + d_in
        dx2_ref[...] = dx2
        dmix, dgr1 = _rms_bwd_rows(mix_ref[...], g1_ref[...], dx2)
        dmix_ref[...] = dmix.astype(dmix_ref.dtype)
        _acc_rows(dg2_ref, dgr2, first)
        _acc_rows(dg1_ref, dgr1, first)

    return _call(
        body, name="mix_residual_bwd", grid=(s // tm,),
        in_specs=_row_specs(tm, d, 4) + _gain_specs(d, 2),
        out_specs=_row_specs(tm, d, 2) + _gain_specs(d, 2),
        out_shape=[jax.ShapeDtypeStruct((s, d), jnp.float32), jax.ShapeDtypeStruct((s, d), jnp.bfloat16),
                   jax.ShapeDtypeStruct((1, d), jnp.float32), jax.ShapeDtypeStruct((1, d), jnp.float32)],
        semantics=("arbitrary",), args=(dx3, dh2, x2, mix, g_pre_mlp, g_post_mix), jobs=jobs)


def _input_bwd(dx2, dh1, x, g_pre_mix, *, tm, jobs=()):
    s, d = x.shape

    def body(dx2_ref, dh1_ref, x_ref, g_ref, dx_ref, dg_ref):
        d_in, dgr = _rms_bwd_rows(x_ref[...], g_ref[...], dh1_ref[...])
        dx_ref[...] = dx2_ref[...] + d_in
        _acc_rows(dg_ref, dgr, pl.program_id(0) == 0)

    return _call(
        body, name="input_bwd", grid=(s // tm,),
        in_specs=_row_specs(tm, d, 3) + _gain_specs(d, 1),
        out_specs=_row_specs(tm, d, 1) + _gain_specs(d, 1),
        out_shape=[jax.ShapeDtypeStruct((s, d), jnp.float32), jax.ShapeDtypeStruct((1, d), jnp.float32)],
        semantics=("arbitrary",), args=(dx2, dh1, x, g_pre_mix), jobs=jobs)


def _shift_rows(z, by):
    s = z.shape[0]
    rows = lax.broadcasted_iota(jnp.int32, z.shape, 0)
    rolled = pltpu.roll(z, by % s, axis=0)
    keep = rows >= by if by > 0 else rows < s + by
    return jnp.where(keep, rolled, 0.0)


def _conv_fwd(proj, conv_w, conv_out_g, jobs=()):
    s = proj.shape[0]
    groups = CONV_WIDTH // 128

    def body(u_ref, gb_ref, gc_ref, w_ref, g_ref, y_ref):
        z = gc_ref[...] * u_ref[...]
        w = w_ref[...]
        conv = w[0:1, :] * _shift_rows(z, 2) + w[1:2, :] * _shift_rows(z, 1) + w[2:3, :] * z
        y = gb_ref[...] * conv
        y_ref[...] = (y * _rstd(y) * g_ref[...]).astype(y_ref.dtype)

    col = lambda base: pl.BlockSpec((s, 128), lambda j: (0, base + j))
    return _call(
        body, name="conv_fwd", grid=(groups,),
        in_specs=[col(0), col(groups), col(2 * groups), pl.BlockSpec((3, 128), lambda j: (0, j)),
                  pl.BlockSpec((1, 128), lambda j: (0, j))],
        out_specs=[pl.BlockSpec((s, 128), lambda j: (0, j))],
        out_shape=[jax.ShapeDtypeStruct((s, CONV_WIDTH), jnp.bfloat16)],
        semantics=("parallel",), args=(proj, proj, proj, conv_w, conv_out_g), jobs=jobs)[0]


def _conv_bwd(proj, conv_w, conv_out_g, dycat, jobs=()):
    s = proj.shape[0]
    groups = CONV_WIDTH // 128

    def body(u_ref, gb_ref, gc_ref, w_ref, g_ref, dy_ref, du_ref, dgb_ref, dgc_ref, dw_ref, dg_ref):
        u, gb, gc = u_ref[...], gb_ref[...], gc_ref[...]
        w = w_ref[...]
        z = gc * u
        z1, z2 = _shift_rows(z, 1), _shift_rows(z, 2)
        conv = w[0:1, :] * z2 + w[1:2, :] * z1 + w[2:3, :] * z
        dyr, dgr = _rms_bwd_rows(gb * conv, g_ref[...], dy_ref[...])
        dg_ref[...] = jnp.sum(dgr, axis=0, keepdims=True)
        dgb_ref[...] = (dyr * conv).astype(dgb_ref.dtype)
        dconv = dyr * gb
        dw_ref[0:1, :] = jnp.sum(dconv * z2, axis=0, keepdims=True)
        dw_ref[1:2, :] = jnp.sum(dconv * z1, axis=0, keepdims=True)
        dw_ref[2:3, :] = jnp.sum(dconv * z, axis=0, keepdims=True)
        dz = w[2:3, :] * dconv + w[1:2, :] * _shift_rows(dconv, -1) + w[0:1, :] * _shift_rows(dconv, -2)
        dgc_ref[...] = (dz * u).astype(dgc_ref.dtype)
        du_ref[...] = (dz * gc).astype(du_ref.dtype)

    col = lambda base: pl.BlockSpec((s, 128), lambda j: (0, base + j))
    act = jax.ShapeDtypeStruct((s, CONV_WIDTH), jnp.bfloat16)
    return _call(
        body, name="conv_bwd", grid=(groups,),
        in_specs=[col(0), col(groups), col(2 * groups), pl.BlockSpec((3, 128), lambda j: (0, j)),
                  pl.BlockSpec((1, 128), lambda j: (0, j)), col(0)],
        out_specs=[col(0), col(0), col(0), pl.BlockSpec((3, 128), lambda j: (0, j)),
                   pl.BlockSpec((1, 128), lambda j: (0, j))],
        out_shape=[act, act, act, jax.ShapeDtypeStruct((3, CONV_WIDTH), jnp.float32),
                   jax.ShapeDtypeStruct((1, CONV_WIDTH), jnp.float32)],
        semantics=("parallel",), args=(proj, proj, proj, conv_w, conv_out_g, dycat), jobs=jobs)


def _rope_tables(s):
    pos = jnp.arange(s, dtype=jnp.float32)
    inv_freq = jnp.power(ROPE_THETA, -jnp.arange(0, QK_ROPE, 2, dtype=jnp.float32) / QK_ROPE)
    ang = pos[:, None] * inv_freq[None, :]
    cos, sin = jnp.cos(ang), jnp.sin(ang)
    zeros = jnp.zeros((s, 128 - QK_ROPE), jnp.float32)
    return (jnp.concatenate([cos, cos, zeros], axis=1), jnp.concatenate([-sin, sin, zeros], axis=1))


def _swap_halves(x):
    lane = lax.broadcasted_iota(jnp.int32, x.shape, 1)
    swapped = jnp.where(lane < QK_ROPE // 2, pltpu.roll(x, 128 - QK_ROPE // 2, axis=1),
                        pltpu.roll(x, QK_ROPE // 2, axis=1))
    return jnp.where(lane < QK_ROPE, swapped, 0.0)


def _rope(x, cos, sin):
    return x * cos + _swap_halves(x) * sin


def _rope_transposed(d, cos, sin):
    return d * cos + _swap_halves(d * sin)


def _qk_rope_fwd(q_pad, proj, cos, sin, *, tm, jobs=()):
    s = q_pad.shape[0]
    wq = N_HEADS * HEAD_PAD

    def body(q_ref, kr_ref, cos_ref, sin_ref, qo_ref, kro_ref):
        c, sn = cos_ref[...], sin_ref[...]
        for h in range(N_HEADS):
            lo = h * HEAD_PAD
            qo_ref[:, lo:lo + 128] = q_ref[:, lo:lo + 128].astype(qo_ref.dtype)
            qo_ref[:, lo + 128:lo + 256] = _rope(q_ref[:, lo + 128:lo + 256], c, sn).astype(qo_ref.dtype)
        kro_ref[...] = _rope(kr_ref[...], c, sn).astype(kro_ref.dtype)

    return _call(
        body, name="qk_rope_fwd", grid=(s // tm,),
        in_specs=[pl.BlockSpec((tm, wq), lambda i: (i, 0)), pl.BlockSpec((tm, 128), lambda i: (i, COL_KR // 128)),
                  pl.BlockSpec((tm, 128), lambda i: (i, 0)), pl.BlockSpec((tm, 128), lambda i: (i, 0))],
        out_specs=[pl.BlockSpec((tm, wq), lambda i: (i, 0)), pl.BlockSpec((tm, 128), lambda i: (i, 0))],
        out_shape=[jax.ShapeDtypeStruct((s, wq), jnp.bfloat16), jax.ShapeDtypeStruct((s, 128), jnp.bfloat16)],
        semantics=("parallel",), args=(q_pad, proj, cos, sin), jobs=jobs)


def _qk_rope_bwd(dq_pad, dk_pad, dv, cos, sin, *, tm, jobs=()):
    s = dq_pad.shape[0]
    wq = N_HEADS * HEAD_PAD

    def body(dq_ref, dk_ref, dv_ref, cos_ref, sin_ref, dqo_ref, dkv_ref, dkr_ref):
        c, sn = cos_ref[...], sin_ref[...]
        dkr = jnp.zeros((tm, 128), jnp.float32)
        for h in range(N_HEADS):
            lo = h * HEAD_PAD
            dqo_ref[:, lo:lo + 128] = dq_ref[:, lo:lo + 128].astype(dqo_ref.dtype)
            dqo_ref[:, lo + 128:lo + 256] = _rope_transposed(dq_ref[:, lo + 128:lo + 256], c, sn).astype(dqo_ref.dtype)
            dkv_ref[:, lo:lo + 128] = dk_ref[:, lo:lo + 128].astype(dkv_ref.dtype)
            dkv_ref[:, lo + 128:lo + 256] = dv_ref[:, h * V_DIM:(h + 1) * V_DIM].astype(dkv_ref.dtype)
            dkr = dkr + dk_ref[:, lo + 128:lo + 256]
        dkr_ref[...] = _rope_transposed(dkr, c, sn).astype(dkr_ref.dtype)

    return _call(
        body, name="qk_rope_bwd", grid=(s // tm,),
        in_specs=[pl.BlockSpec((tm, wq), lambda i: (i, 0)), pl.BlockSpec((tm, wq), lambda i: (i, 0)),
                  pl.BlockSpec((tm, N_HEADS * V_DIM), lambda i: (i, 0)),
                  pl.BlockSpec((tm, 128), lambda i: (i, 0)), pl.BlockSpec((tm, 128), lambda i: (i, 0))],
        out_specs=[pl.BlockSpec((tm, wq), lambda i: (i, 0)), pl.BlockSpec((tm, wq), lambda i: (i, 0)),
                   pl.BlockSpec((tm, 128), lambda i: (i, 0))],
        out_shape=[jax.ShapeDtypeStruct((s, wq), jnp.bfloat16), jax.ShapeDtypeStruct((s, wq), jnp.bfloat16),
                   jax.ShapeDtypeStruct((s, 128), jnp.bfloat16)],
        semantics=("parallel",), args=(dq_pad, dk_pad, dv, cos, sin), jobs=jobs)


def _visible(q0, k0, t):
    qpos = q0 + lax.broadcasted_iota(jnp.int32, (t, t), 0)
    kpos = k0 + lax.broadcasted_iota(jnp.int32, (t, t), 1)
    return lax.shift_right_logical(kpos, CHUNK_SHIFT) <= lax.shift_right_logical(qpos, CHUNK_SHIFT)


def _attn_fwd(q, kv, kr, attn_out_g, *, t, jobs=()):
    s = q.shape[0]
    nq = s // t

    def body(q_ref, kn_ref, v_ref, kr_ref, g_ref, o_ref, lse_ref, y_ref, kcat_ref):
        i = pl.program_id(1)

        @pl.when(i == 0)
        def _():
            kcat_ref[:, 0:128] = kn_ref[...]
            kcat_ref[:, 128:256] = kr_ref[...]

        qv = q_ref[...]

        def step(j, carry, diagonal):
            m, l, acc = carry
            k = kcat_ref[pl.ds(pl.multiple_of(j * t, t), t), :]
            v = v_ref[pl.ds(pl.multiple_of(j * t, t), t), :]
            sc = lax.dot_general(qv, k, _NT, preferred_element_type=jnp.float32) * ATTN_SCALE
            if diagonal:
                sc = jnp.where(_visible(0, 0, t), sc, NEG_INF)
            m_new = jnp.maximum(m, jnp.max(sc, axis=-1, keepdims=True))
            p = jnp.exp(sc - m_new)
            alpha = jnp.exp(m - m_new)
            l = alpha * l + jnp.sum(p, axis=-1, keepdims=True)
            acc = alpha * acc + lax.dot_general(p.astype(jnp.bfloat16), v, _NN, preferred_element_type=jnp.float32)
            return m_new, l, acc

        init = (jnp.full((t, 1), NEG_INF, jnp.float32), jnp.zeros((t, 1), jnp.float32),
                jnp.zeros((t, V_DIM), jnp.float32))
        before = lax.fori_loop(0, i, functools.partial(step, diagonal=False), init)
        m, l, acc = step(i, before, True)
        o = acc / l
        o_ref[...] = o
        lse_ref[...] = jnp.broadcast_to(m + jnp.log(l), (t, 128))
        y_ref[...] = (o * _rstd(o) * g_ref[...]).astype(y_ref.dtype)

    head_rows = lambda w, f: pl.BlockSpec((s, w), lambda h, i: (0, f(h)))
    blk = pl.BlockSpec((t, 128), lambda h, i: (i, h))
    full = jax.ShapeDtypeStruct((s, N_HEADS * V_DIM), jnp.float32)
    return _call(
        body, name="attn_fwd", grid=(N_HEADS, nq),
        in_specs=[pl.BlockSpec((t, HEAD_PAD), lambda h, i: (i, h)), head_rows(128, lambda h: 2 * h),
                  head_rows(128, lambda h: 2 * h + 1), head_rows(128, lambda h: 0),
                  pl.BlockSpec((1, 128), lambda h, i: (0, h))],
        out_specs=[blk, blk, blk],
        out_shape=[full, full, jax.ShapeDtypeStruct((s, N_HEADS * V_DIM), jnp.bfloat16)],
        scratch_shapes=[pltpu.VMEM((s, HEAD_PAD), jnp.bfloat16)],
        semantics=("arbitrary", "arbitrary"), args=(q, kv, kv, kr, attn_out_g), jobs=jobs)


def _attn_bwd(q, kv, kr, o, lse, attn_out_g, dycat, *, t, jobs=()):
    s = q.shape[0]
    nq = s // t

    def body(q_ref, kn_ref, v_ref, kr_ref, o_ref, lse_ref, g_ref, dy_ref, dq_ref, dk_ref, dv_ref, dg_ref,
             kcat_ref, do_ref, delta_ref):
        ov = o_ref[...]
        do, dgr = _rms_bwd_rows(ov, g_ref[...], dy_ref[...])
        do_ref[...] = do.astype(do_ref.dtype)
        delta_ref[...] = jnp.broadcast_to(jnp.sum(do * ov, axis=-1, keepdims=True), (s, 128))
        dg_ref[...] = jnp.sum(dgr, axis=0, keepdims=True)
        kcat_ref[:, 0:128] = kn_ref[...]
        kcat_ref[:, 128:256] = kr_ref[...]
        dq_ref[...] = jnp.zeros_like(dq_ref)
        dk_ref[...] = jnp.zeros_like(dk_ref)
        dv_ref[...] = jnp.zeros_like(dv_ref)

        def kv_step(j, _):
            krows = pl.ds(pl.multiple_of(j * t, t), t)
            k = kcat_ref[krows, :]
            v = v_ref[krows, :]

            def q_step(i, _, diagonal):
                qrows = pl.ds(pl.multiple_of(i * t, t), t)
                qv = q_ref[qrows, :]
                dov = do_ref[qrows, :]
                sc = lax.dot_general(qv, k, _NT, preferred_element_type=jnp.float32) * ATTN_SCALE
                if diagonal:
                    sc = jnp.where(_visible(0, 0, t), sc, NEG_INF)
                p = jnp.exp(sc - lse_ref[qrows, :][:, 0:1])
                dp = lax.dot_general(dov, v, _NT, preferred_element_type=jnp.float32)
                ds = (p * (dp - delta_ref[qrows, :][:, 0:1]) * ATTN_SCALE).astype(jnp.bfloat16)
                dv_ref[krows, :] += lax.dot_general(p.astype(jnp.bfloat16), dov, _TN,
                                                    preferred_element_type=jnp.float32)
                dk_ref[krows, :] += lax.dot_general(ds, qv, _TN, preferred_element_type=jnp.float32)
                dq_ref[qrows, :] += lax.dot_general(ds, k, _NN, preferred_element_type=jnp.float32)
                return 0

            q_step(j, 0, True)
            lax.fori_loop(j + 1, nq, functools.partial(q_step, diagonal=False), 0)
            return 0

        lax.fori_loop(0, nq, kv_step, 0)

    col = lambda w, f: pl.BlockSpec((s, w), lambda h: (0, f(h)))
    return _call(
        body, name="attn_bwd", grid=(N_HEADS,),
        in_specs=[col(HEAD_PAD, lambda h: h), col(128, lambda h: 2 * h), col(128, lambda h: 2 * h + 1),
                  col(128, lambda h: 0), col(128, lambda h: h), col(128, lambda h: h),
                  pl.BlockSpec((1, 128), lambda h: (0, h)), col(128, lambda h: CONV_WIDTH // 128 + h)],
        out_specs=[col(HEAD_PAD, lambda h: h), col(HEAD_PAD, lambda h: h), col(128, lambda h: h),
                   pl.BlockSpec((1, 128), lambda h: (0, h))],
        out_shape=[jax.ShapeDtypeStruct((s, N_HEADS * HEAD_PAD), jnp.float32),
                   jax.ShapeDtypeStruct((s, N_HEADS * HEAD_PAD), jnp.float32),
                   jax.ShapeDtypeStruct((s, N_HEADS * V_DIM), jnp.float32),
                   jax.ShapeDtypeStruct((1, N_HEADS * V_DIM), jnp.float32)],
        scratch_shapes=[pltpu.VMEM((s, HEAD_PAD), jnp.bfloat16), pltpu.VMEM((s, V_DIM), jnp.bfloat16),
                        pltpu.VMEM((s, 128), jnp.float32)],
        semantics=("parallel",), args=(q, kv, kv, kr, o, lse, attn_out_g, dycat), jobs=jobs)


def _row_tile(rows, cap=256):
    for cand in (512, 256, 128, 64, 32, 16, 8):
        if cand <= cap and rows % cand == 0:
            return cand
    return rows


def _cast_into_slot(w, pos, *, name, jobs=()):
    r, c = w.shape
    tr = _row_tile(r)

    def body(pos_ref, w_ref, o_ref):
        o_ref[...] = w_ref[...].astype(o_ref.dtype)

    return _call(
        body, name=name, grid=(r // tr,), prefetch=pos,
        in_specs=[pl.BlockSpec((tr, c), lambda i, p: (i, 0))],
        out_specs=[pl.BlockSpec((None, tr, c), lambda i, p: (p[1], i, 0))],
        out_shape=[jax.ShapeDtypeStruct((4, r, c), jnp.bfloat16)],
        semantics=("parallel",), args=(w,), jobs=jobs)[0]


def _cast_many_into_slots(ws, pos, *, name, jobs=()):
    steps = 8
    assert all(w.shape[0] % (16 * steps) == 0 for w in ws), [w.shape for w in ws]

    def body(pos_ref, *refs):
        for w_ref, o_ref in zip(refs[:len(ws)], refs[len(ws):]):
            o_ref[...] = w_ref[...].astype(o_ref.dtype)

    return _call(
        body, name=name, grid=(steps,), prefetch=pos,
        in_specs=[pl.BlockSpec((w.shape[0] // steps, w.shape[1]), lambda i, p: (i, 0)) for w in ws],
        out_specs=[pl.BlockSpec((None, w.shape[0] // steps, w.shape[1]), lambda i, p: (p[1], i, 0)) for w in ws],
        out_shape=[jax.ShapeDtypeStruct((4, *w.shape), jnp.bfloat16) for w in ws],
        semantics=("parallel",), args=tuple(ws), jobs=jobs)


def _pair_add(g, theirs, pos, *, name, jobs=()):
    n, h, c = theirs.shape
    tr = _row_tile(h, cap=512)
    nb = h // tr

    def body(pos_ref, a_ref, b_ref, o_ref):
        o_ref[...] = (a_ref[...].astype(jnp.float32) + b_ref[...].astype(jnp.float32)).astype(o_ref.dtype)

    spec = pl.BlockSpec((None, tr, c), lambda j, i, p: (j, i, 0))
    return _call(
        body, name=name, grid=(n, nb), prefetch=pos,
        in_specs=[pl.BlockSpec((None, tr, c), lambda j, i, p: (j, i + p[0] * nb, 0)), spec],
        out_specs=[spec], out_shape=[jax.ShapeDtypeStruct(theirs.shape, jnp.bfloat16)],
        semantics=("parallel", "parallel"), args=(g, theirs), jobs=jobs)[0]


def _fold_diag(pair_sum, via, pos, *, name):
    n, h, c = pair_sum.shape
    tr = _row_tile(h // 2, cap=512)
    nb = h // tr

    def body(pos_ref, p_ref, via_ref, o_ref):
        j, i = pl.program_id(0), pl.program_id(1)
        mine = p_ref[...].astype(jnp.float32)
        add = (j == 0) == (i >= nb // 2)
        o_ref[...] = jnp.where(add, mine + via_ref[...].astype(jnp.float32), mine).astype(o_ref.dtype)

    return _call(
        body, name=name, grid=(2, nb), prefetch=pos,
        in_specs=[pl.BlockSpec((None, tr, c), lambda j, i, p: (jnp.bitwise_xor(p[1], 2 - j), i, 0)),
                  pl.BlockSpec((tr, c), lambda j, i, p: (i, 0))],
        out_specs=[pl.BlockSpec((None, tr, c), lambda j, i, p: (j, i, 0))],
        out_shape=[jax.ShapeDtypeStruct((2, h, c), jnp.bfloat16)],
        semantics=("parallel", "parallel"), args=(pair_sum, via))[0]


def _chip_sum(by_source, pair_sum, pos, *, name):
    n, h, c = by_source.shape
    tr = _row_tile(h, cap=512)
    nb = h // tr

    def body(pos_ref, own_ref, px_ref, py_ref, o_ref):
        f = lambda ref: ref[...].astype(jnp.float32)
        o_ref[...] = (f(own_ref) + f(px_ref)) + f(py_ref)

    slot = lambda flip: pl.BlockSpec((None, tr, c), lambda i, p: (jnp.bitwise_xor(p[1], flip), i, 0))
    return pl.pallas_call(
        body, name=name, out_shape=jax.ShapeDtypeStruct((2 * h, c), jnp.float32),
        grid_spec=pltpu.PrefetchScalarGridSpec(
            num_scalar_prefetch=1, grid=(nb,),
            in_specs=[slot(0), slot(2), slot(1)],
            out_specs=pl.BlockSpec((tr, c), lambda i, p: (i + p[0] * nb, 0))),
        compiler_params=_params(("parallel",)),
    )(pos, pair_sum, by_source, by_source)


def _adamw(w, g, m, v, *, name, jobs=()):
    r, c = w.shape
    tr = _row_tile(r)

    def body(w_ref, g_ref, m_ref, v_ref, d_ref, mo_ref, vo_ref, go_ref):
        gv = g_ref[...]
        go_ref[...] = gv
        mn = ADAM_B1 * m_ref[...] + (1.0 - ADAM_B1) * gv
        vn = ADAM_B2 * v_ref[...] + (1.0 - ADAM_B2) * (gv * gv)
        m_hat = mn / (1.0 - ADAM_B1 ** ADAM_STEP)
        v_hat = vn / (1.0 - ADAM_B2 ** ADAM_STEP)
        d_ref[...] = -ADAM_LR * (m_hat / (jnp.sqrt(v_hat) + ADAM_EPS) + ADAM_WD * w_ref[...])
        mo_ref[...] = mn
        vo_ref[...] = vn

    spec = pl.BlockSpec((tr, c), lambda i: (i, 0))
    out = jax.ShapeDtypeStruct((r, c), jnp.float32)
    return _call(body, name=name, grid=(r // tr,), in_specs=[spec] * 4, out_specs=[spec] * 4, out_shape=[out] * 4,
                 semantics=("parallel",), args=(w, g, m, v), jobs=jobs)


def _all_reduce_small(block):
    r, c = block.shape

    def body(src_ref, out_ref, stage_ref, send_sems, recv_sems):
        x, y, cc = _position()
        me = 4 * x + 2 * y + cc
        stage_ref[me] = src_ref[...]
        flip = lambda v, on: 1 - v if on else v
        peers = [(flip(x, dx), flip(y, dy), flip(cc, dc)) for dx in (0, 1) for dy in (0, 1) for dc in (0, 1)][1:]
        copies = [pltpu.make_async_remote_copy(
            src_ref=stage_ref.at[me], dst_ref=stage_ref.at[me],
            send_sem=send_sems.at[k], recv_sem=recv_sems.at[k], device_id=peer, device_id_type=MESH)
            for k, peer in enumerate(peers)]
        for cp in copies:
            cp.start()
        for k, (px, py, pc) in enumerate(peers):
            them = 4 * px + 2 * py + pc
            pltpu.make_async_remote_copy(
                src_ref=stage_ref.at[them], dst_ref=stage_ref.at[them],
                send_sem=send_sems.at[k], recv_sem=recv_sems.at[k], device_id=(px, py, pc),
                device_id_type=MESH).wait_recv()
        for cp in copies:
            cp.wait_send()
        total = stage_ref[0]
        for d in range(1, 8):
            total = total + stage_ref[d]
        out_ref[...] = total

    return pl.pallas_call(
        body, name="all_reduce_small",
        in_specs=[pl.BlockSpec(memory_space=pltpu.VMEM)], out_specs=pl.BlockSpec(memory_space=pltpu.VMEM),
        out_shape=jax.ShapeDtypeStruct((r, c), jnp.float32),
        scratch_shapes=[pltpu.VMEM((8, r, c), jnp.float32), pltpu.SemaphoreType.DMA((7,)),
                        pltpu.SemaphoreType.DMA((7,))],
        compiler_params=pltpu.CompilerParams(has_side_effects=True),
    )(block)


def _cols_from_shards(g):
    n, r, c = g.shape
    return jnp.transpose(g, (1, 0, 2)).reshape(r, n * c)


def _cols_to_shards(w, n=4):
    r, c = w.shape
    return jnp.transpose(w.reshape(r, n, c // n), (1, 0, 2))


def _pad_w_in(g):
    _, d, c = g.shape
    cut = COL_KR - 3 * c
    zeros = jnp.zeros((d, COL_CKV - COL_KR - QK_ROPE), g.dtype)
    return jnp.concatenate([g[0], g[1], g[2], g[3][:, :cut], g[3][:, cut + KV_RANK:], zeros,
                            g[3][:, cut:cut + KV_RANK]], axis=1)


def _unpad_w_in(padded):
    c = IN_WIDTH // 4
    last = jnp.concatenate([padded[:, 3 * c:COL_KR], padded[:, COL_CKV:COL_CKV + KV_RANK],
                            padded[:, COL_KR:COL_KR + QK_ROPE]], axis=1)
    return jnp.stack([padded[:, 0:c], padded[:, c:2 * c], padded[:, 2 * c:3 * c], last])


def _pad_w_uq(full):
    r = full.shape[0]
    per_head = full.reshape(r, N_HEADS, QK_NOPE + QK_ROPE)
    return jnp.pad(per_head, ((0, 0), (0, 0), (0, HEAD_PAD - QK_NOPE - QK_ROPE))).reshape(r, N_HEADS * HEAD_PAD)


def _unpad_w_uq(padded):
    r = padded.shape[0]
    return padded.reshape(r, N_HEADS, HEAD_PAD)[:, :, :QK_NOPE + QK_ROPE].reshape(r, N_HEADS * (QK_NOPE + QK_ROPE))


SMALL_ROWS = 16


def _pack_small(d, pre_mix, post_mix, pre_mlp, post_mlp, conv_out, attn_out, q_norm, kv_norm, conv_w):
    row = lambda *parts: jnp.pad(jnp.concatenate(parts, axis=1), ((0, 0), (0, d - sum(p.shape[1] for p in parts))))
    rows = [row(pre_mix), row(post_mix), row(pre_mlp), row(post_mlp), row(conv_out, attn_out), row(q_norm, kv_norm),
            row(conv_w[0:1]), row(conv_w[1:2]), row(conv_w[2:3])]
    return jnp.pad(jnp.concatenate(rows, axis=0), ((0, SMALL_ROWS - len(rows)), (0, 0)))


def _unpack_small(p, chip):
    cw = CONV_WIDTH // 4
    conv_w = lax.dynamic_slice(p[6:9, :CONV_WIDTH], (0, chip * cw), (3, cw))
    return dict(pre_mix_g=p[0:1], post_mix_g=p[1:2], pre_mlp_g=p[2:3], post_mlp_g=p[3:4],
                conv_out_g=p[4:5, :CONV_WIDTH], attn_out_g=p[4:5, CONV_WIDTH:2 * CONV_WIDTH],
                q_norm_g=p[5:6, :Q_RANK], kv_norm_g=p[5:6, Q_RANK:Q_RANK + KV_RANK], conv_w=conv_w[None])


def kernel(x, pre_mix_g, w_in, conv_w, q_norm_g, w_uq, kv_norm_g, w_ukv, conv_out_g, attn_out_g, w_o, post_mix_g, pre_mlp_g, w_up, w_down, post_mlp_g, loss_target, m_pre_mix_g, m_w_in, m_conv_w, m_q_norm_g, m_w_uq, m_kv_norm_g, m_w_ukv, m_conv_out_g, m_attn_out_g, m_w_o, m_post_mix_g, m_pre_mlp_g, m_w_up, m_w_down, m_post_mlp_g, v_pre_mix_g, v_w_in, v_conv_w, v_q_norm_g, v_w_uq, v_kv_norm_g, v_w_ukv, v_conv_out_g, v_attn_out_g, v_w_o, v_post_mix_g, v_pre_mlp_g, v_w_up, v_w_down, v_post_mlp_g):
    bf16 = jnp.bfloat16
    s, d = x.shape[1], x.shape[2]
    d_ff = 4 * d
    chip = 2 * lax.axis_index("x") + lax.axis_index("y")
    xs = x.reshape(s, d)
    target = loss_target.reshape(s, d)
    tm = min(256, s)
    t_attn = min(1024, s)
    mt = min(1024, s)
    kt = min(2048, s)

    big = dict(w_in=w_in[0], w_uq=w_uq[0], w_ukv=w_ukv[0], w_o=w_o[0], w_up=w_up[0], w_down=w_down[0])
    names = list(big)
    pos = jnp.stack([lax.axis_index("c"), chip]).astype(jnp.int32)
    wb = {}
    half = {k: big[k].shape[0] // 2 for k in names}

    def rows(k, a, b):
        lo, n = a * half[k] // 64, (b - a) * half[k] // 64
        assert lo % 16 == 0 and n % 16 == 0 and n > 0, (k, a, b)
        return lo, n

    ici = lambda k, a=0, b=64: _GatherIci(wb[k], *rows(k, a, b))
    fwd = lambda k, a=0, b=64: _GatherForward(wb[k], *rows(k, a, b))
    near = lambda k, a=0, b=64: _GatherD2d(wb[k], *rows(k, a, b), slots=(0, 1))
    far = lambda k, a=0, b=64: _GatherD2d(wb[k], *rows(k, a, b), slots=(2,))
    tap_rows = 64
    half["conv_w"] = tap_rows // 2
    wb["conv_w"] = _Buf(lax.dynamic_update_slice(
        jnp.zeros((4, tap_rows, CONV_WIDTH // 4), jnp.float32),
        jnp.pad(conv_w[0], ((0, tap_rows - conv_w.shape[1]), (0, 0)))[None], (chip, 0, 0)))
    wb["w_in"] = _Buf(_cast_into_slot(big["w_in"], pos, name="cast_w_in"))
    rest = [k for k in names if k != "w_in"]
    for k, slot in zip(rest, _cast_many_into_slots([big[k] for k in rest], pos, name="cast_rest",
                                                   jobs=[ici("w_in"), ici("conv_w")])):
        wb[k] = _Buf(slot)
    h1 = _rms_fwd(xs, pre_mix_g, width=d, col=0, tm=tm, name="rms_pre_mix",
                  jobs=[fwd("w_in"), near("w_in"), fwd("conv_w"), near("conv_w")])
    _comm("gather_w_in", [[far("w_in"), far("conv_w")]])
    win = _pad_w_in(wb["w_in"].arr)
    conv_w_full = jnp.transpose(wb["conv_w"].arr[:, :conv_w.shape[1], :], (1, 0, 2)).reshape(-1, CONV_WIDTH)
    spread = lambda a: lax.dynamic_update_slice(jnp.zeros((3, CONV_WIDTH), jnp.float32), a[0],
                                                (0, chip * (CONV_WIDTH // 4)))
    ff4 = d_ff // 4

    proj = _matmul(h1, win, dims=_NN, mnk=(s, IN_PAD, d), tiles=(mt, IN_PAD // 3, d), name="mm_proj",
                   jobs=[ici("w_uq"), ici("w_ukv"), ici("w_o")])
    y_conv = _conv_fwd(proj, conv_w_full, conv_out_g,
                       jobs=[fwd("w_uq"), fwd("w_ukv"), fwd("w_o"), near("w_uq"), near("w_ukv"), near("w_o")])
    cqn = _rms_fwd(proj, q_norm_g, width=Q_RANK, col=COL_CQ // Q_RANK, tm=mt, name="rms_q",
                   jobs=[far("w_uq"), far("w_ukv"), far("w_o"), ici("w_up", 0, 4)])
    wuq = _pad_w_uq(_cols_from_shards(wb["w_uq"].arr))
    wukv = _cols_from_shards(wb["w_ukv"].arr)
    wo = wb["w_o"].arr.reshape(-1, d)
    ckvn = _rms_fwd(proj, kv_norm_g, width=KV_RANK, col=COL_CKV // KV_RANK, tm=mt, name="rms_kv",
                    jobs=[ici("w_up", 4, 10), fwd("w_up", 0, 4)])
    q_pad = _matmul(cqn, wuq, dims=_NN, mnk=(s, N_HEADS * HEAD_PAD, Q_RANK), tiles=(mt, 1024, Q_RANK), name="mm_q",
                    jobs=[ici("w_up", 10, 18), fwd("w_up", 4, 10), near("w_up", 0, 4)])
    kv = _matmul(ckvn, wukv, dims=_NN, mnk=(s, N_HEADS * HEAD_PAD, KV_RANK), tiles=(mt, 1024, KV_RANK),
                 name="mm_kv", out_dtype=bf16,
                 jobs=[ici("w_up", 18, 24), fwd("w_up", 10, 18), near("w_up", 4, 10), far("w_up", 0, 4)])
    cos, sin = _rope_tables(s)
    q_rot, kr_rot = _qk_rope_fwd(
        q_pad, proj, cos, sin, tm=tm,
        jobs=[ici("w_up", 24, 32), fwd("w_up", 18, 24), near("w_up", 10, 18), far("w_up", 4, 10)])
    o, lse, y_attn = _attn_fwd(
        q_rot, kv, kr_rot, attn_out_g, t=t_attn,
        jobs=[ici("w_up", 32, 64), ici("w_down", 0, 8), fwd("w_up", 24, 32), near("w_up", 18, 24), far("w_up", 10, 18)])
    ycat = jnp.concatenate([y_conv, y_attn], axis=1)
    mix = _matmul(ycat, wo, dims=_NN, mnk=(s, d, 2 * CONV_WIDTH), tiles=(mt, 1024, 2 * CONV_WIDTH), name="mm_out",
                  jobs=[fwd("w_up", 32, 64), near("w_up", 24, 32), far("w_up", 18, 24)])
    x2, h2 = _mix_residual_fwd(
        xs, mix, post_mix_g, pre_mlp_g, tm=tm,
        jobs=[ici("w_down", 8, 20), fwd("w_down", 0, 8), near("w_up", 32, 64), far("w_up", 24, 64)])
    wup = wb["w_up"].arr

    def up_epilogue(acc, extra_refs, out_refs):
        r = jnp.maximum(acc, 0.0)
        out_refs[0][...] = acc.astype(bf16)
        out_refs[1][...] = (r * r).astype(bf16)

    n_ff = ff4 // 1024
    act = jax.ShapeDtypeStruct((s, d_ff), bf16)
    up, act_sq = _matmul(
        h2, wup, dims=_NN, mnk=(s, d_ff, d), tiles=(mt, 1024, d), name="mm_up",
        b_spec=pl.BlockSpec((None, d, 1024), lambda i, j, l: (j // n_ff, l, j % n_ff)),
        out_shape=(act, act), o_spec=(pl.BlockSpec((mt, 1024), lambda i, j, l: (i, j)),) * 2, epilogue=up_epilogue,
        jobs=[ici("w_down", 20, 64), fwd("w_down", 8, 20), near("w_down", 0, 8)])
    _comm("gather_w_down_tail", [[fwd("w_down", 20, 64), near("w_down", 8, 64), far("w_down", 0, 20)],
                                 [far("w_down", 20, 64)]])
    wdown = wb["w_down"].arr.reshape(d_ff, d)
    mlp = _matmul(act_sq, wdown, dims=_NN, mnk=(s, d, d_ff), tiles=(min(512, s), 512, d_ff), name="mm_down")
    dx3, dmlp, dg_post_mlp, loss_part = _loss_head(x2, mlp, target, post_mlp_g, tm=tm)

    def dup_epilogue(acc, extra_refs, out_refs):
        out_refs[0][...] = (acc * (2.0 * jnp.maximum(extra_refs[0][...].astype(jnp.float32), 0.0))).astype(bf16)

    grads, theirs, pair_sums, via, folded, by_source, whole = {}, {}, {}, {}, {}, {}, {}

    def exchange(k, g):
        grads[k] = g
        theirs[k] = _Buf(jax.ShapeDtypeStruct((4, g.shape[1] // 2, g.shape[2]), bf16))
        return _PairExchange(g, theirs[k])

    def pair_sum(k, jobs=()):
        pair_sums[k] = _pair_add(grads[k], theirs[k].arr, pos, name="pair_add_" + k, jobs=jobs)
        via[k] = _Buf(jax.ShapeDtypeStruct(pair_sums[k].shape[1:], bf16))
        by_source[k] = _Buf(jax.ShapeDtypeStruct(pair_sums[k].shape, bf16))

    def diag(k, a=0, b=32):
        lo, n = a * half[k] // 64, (b - a) * half[k] // 64
        assert lo % 16 == 0 and n % 16 == 0 and n > 0, (k, a, b)
        return _ScatterDiag(pair_sums[k], via[k], lo, n)

    def fold(k):
        folded[k] = _fold_diag(pair_sums[k], via[k].arr, pos, name="fold_" + k)

    scatter = lambda k, a=0, b=64: _ScatterNear(folded[k], by_source[k], *rows(k, a, b))

    def share(k):
        whole[k] = _Buf(_chip_sum(by_source[k].arr, pair_sums[k], pos, name="chip_sum_" + k))
        return _PairShare(whole[k])

    g_wdown = _matmul(act_sq, dmlp, dims=_TN, mnk=(d_ff, d, s), tiles=(1024, 1024, kt), name="mm_gw_down",
                      out_dtype=bf16).reshape(4, ff4, d)
    dup = _matmul(dmlp, wdown, dims=_NT, mnk=(s, d_ff, d), tiles=(mt, 1024, d), name="mm_dact",
                  out_dtype=bf16, epilogue=dup_epilogue, extra=(up,),
                  extra_specs=(pl.BlockSpec((mt, 1024), lambda i, j, l: (i, j)),),
                  jobs=[exchange("w_down", g_wdown)])
    pair_sum("w_down")
    g_wup = _matmul(h2, dup, dims=_TN, mnk=(d, d_ff, s), tiles=(1024, 1024, kt), name="mm_gw_up",
                    out_shape=jax.ShapeDtypeStruct((4, d, ff4), bf16),
                    o_spec=pl.BlockSpec((None, 1024, 1024), lambda i, j, l: (j // n_ff, i, j % n_ff)),
                    jobs=[diag("w_down")])
    fold("w_down")
    dh2 = _matmul(dup, wup, dims=_NT, mnk=(s, d, d_ff), tiles=(mt, 1024, ff4), name="mm_dh2",
                  b_spec=pl.BlockSpec((None, 1024, ff4), lambda i, j, l: (l, j, 0)),
                  jobs=[exchange("w_up", g_wup), scatter("w_down", 0, 48)])
    pair_sum("w_up", jobs=[scatter("w_down", 48, 64)])
    dx2, dmix, dg_pre_mlp, dg_post_mix = _mix_residual_bwd(
        dx3, dh2, x2, mix, pre_mlp_g, post_mix_g, tm=tm, jobs=[diag("w_up", 0, 20)])

    dycat = _matmul(dmix, wo, dims=_NT, mnk=(s, 2 * CONV_WIDTH, d), tiles=(mt, 1024, d), name="mm_dycat",
                    jobs=[diag("w_up", 20, 32)])
    fold("w_up")
    g_wo = _matmul(ycat, dmix, dims=_TN, mnk=(2 * CONV_WIDTH, d, s), tiles=(1024, 1024, kt),
                   name="mm_gw_out", out_dtype=bf16, jobs=[scatter("w_up", 0, 12)]).reshape(4, CONV_WIDTH // 2, d)
    du, dgb, dgc, dg_conv_w, dg_conv_out = _conv_bwd(proj, conv_w_full, conv_out_g, dycat,
                                                     jobs=[exchange("w_o", g_wo), scatter("w_up", 12, 24)])
    pair_sum("w_o")
    dq_pad, dk_pad, dv, dg_attn_out = _attn_bwd(
        q_rot, kv, kr_rot, o, lse, attn_out_g, dycat, t=min(1024, s),
        jobs=[scatter("w_up", 24, 64), diag("w_o"), share("w_down")])
    fold("w_o")
    dq_raw, dkv, dkr = _qk_rope_bwd(dq_pad, dk_pad, dv, cos, sin, tm=tm, jobs=[scatter("w_o")])
    wq_cols = N_HEADS * HEAD_PAD
    g_wuq = _matmul(cqn, dq_raw, dims=_TN, mnk=(Q_RANK, wq_cols, s), tiles=(Q_RANK, 1024, kt),
                    name="mm_gw_uq", out_dtype=bf16)
    dcqn = _matmul(dq_raw, wuq, dims=_NT, mnk=(s, Q_RANK, wq_cols), tiles=(mt, Q_RANK, wq_cols), name="mm_dcq")
    g_wukv = _matmul(ckvn, dkv, dims=_TN, mnk=(KV_RANK, wq_cols, s), tiles=(KV_RANK, 1024, kt),
                     name="mm_gw_ukv", out_dtype=bf16)
    dckvn = _matmul(dkv, wukv, dims=_NT, mnk=(s, KV_RANK, wq_cols), tiles=(mt, KV_RANK, wq_cols), name="mm_dckv",
                    jobs=[exchange("w_uq", _cols_to_shards(_unpad_w_uq(g_wuq))),
                          exchange("w_ukv", _cols_to_shards(g_wukv))])
    pair_sum("w_uq")
    pair_sum("w_ukv")
    dcq, dg_q_norm = _rms_bwd(proj, q_norm_g, dcqn, width=Q_RANK, col=COL_CQ // Q_RANK, tm=mt, name="rms_q_bwd",
                              jobs=[diag("w_uq"), diag("w_ukv")])
    fold("w_uq")
    fold("w_ukv")
    dckv, dg_kv_norm = _rms_bwd(proj, kv_norm_g, dckvn, width=KV_RANK, col=COL_CKV // KV_RANK, tm=mt,
                                name="rms_kv_bwd")
    dproj = jnp.concatenate([du, dgb, dgc, dcq, dkr, jnp.zeros((s, COL_CKV - COL_KR - 128), bf16), dckv], axis=1)
    g_win = _matmul(h1, dproj, dims=_TN, mnk=(d, IN_PAD, s), tiles=(1024, IN_PAD // 3, kt), name="mm_gw_in",
                    out_dtype=bf16, jobs=[scatter("w_uq"), scatter("w_ukv"), share("w_up"), share("w_o")])
    _comm("pair_exchange_w_in", [[exchange("w_in", _unpad_w_in(g_win))]])
    pair_sum("w_in")
    _comm("scatter_diag_w_in", [[diag("w_in")]])
    fold("w_in")
    dh1 = _matmul(dproj, win, dims=_NT, mnk=(s, d, IN_PAD), tiles=(mt, 1024, IN_PAD // 2), name="mm_dh1",
                  jobs=[scatter("w_in"), share("w_uq"), share("w_ukv")])
    grad_x, dg_pre_mix = _input_bwd(dx2, dh1, xs, pre_mix_g, tm=tm)
    _comm("pair_share_w_in", [[share("w_in")]])

    moments = dict(w_in=(m_w_in, v_w_in), w_uq=(m_w_uq, v_w_uq), w_ukv=(m_w_ukv, v_w_ukv), w_o=(m_w_o, v_w_o),
                   w_up=(m_w_up, v_w_up), w_down=(m_w_down, v_w_down))
    gw, dw, nm, nv = {}, {}, {}, {}
    for k in names:
        delta_k, nm_k, nv_k, g = _adamw(big[k], whole[k].arr, moments[k][0][0], moments[k][1][0], name="adamw_" + k)
        gw[k], dw[k], nm[k], nv[k] = g[None], delta_k[None], nm_k[None], nv_k[None]

    small_g = _all_reduce_small(_pack_small(d, dg_pre_mix, dg_post_mix, dg_pre_mlp, dg_post_mlp, dg_conv_out,
                                            dg_attn_out, dg_q_norm, dg_kv_norm, dg_conv_w
                                            ).at[SMALL_ROWS - 1, :128].set(loss_part[0]))
    loss = small_g[SMALL_ROWS - 1, 0]
    pack_w = lambda cw, pre_mix, post_mix, pre_mlp, post_mlp, conv_out, attn_out, q_norm, kv_norm: _pack_small(
        d, pre_mix, post_mix, pre_mlp, post_mlp, conv_out, attn_out, q_norm, kv_norm, cw)
    small_w = pack_w(conv_w_full, pre_mix_g, post_mix_g, pre_mlp_g, post_mlp_g, conv_out_g, attn_out_g, q_norm_g,
                     kv_norm_g)
    small_m = pack_w(spread(m_conv_w), m_pre_mix_g, m_post_mix_g, m_pre_mlp_g, m_post_mlp_g, m_conv_out_g,
                     m_attn_out_g, m_q_norm_g, m_kv_norm_g)
    small_v = pack_w(spread(v_conv_w), v_pre_mix_g, v_post_mix_g, v_pre_mlp_g, v_post_mlp_g, v_conv_out_g,
                     v_attn_out_g, v_q_norm_g, v_kv_norm_g)
    small_d, small_nm, small_nv, small_g = _adamw(small_w, small_g, small_m, small_v, name="adamw_small")
    sg, sd, snm, snv = (_unpack_small(p, chip) for p in (small_g, small_d, small_nm, small_nv))

    for src, dst in ((sg, gw), (sd, dw), (snm, nm), (snv, nv)):
        dst.update(src)

    order = ["pre_mix_g", "w_in", "conv_w", "q_norm_g", "w_uq", "kv_norm_g", "w_ukv", "conv_out_g", "attn_out_g",
             "w_o", "post_mix_g", "pre_mlp_g", "w_up", "w_down", "post_mlp_g"]
    return (loss, grad_x.reshape(1, s, d), *[gw[k] for k in order], *[dw[k] for k in order],
            *[nm[k] for k in order], *[nv[k] for k in order])
```

```python
import functools

import jax
import jax.numpy as jnp
from jax import lax
from jax.experimental import pallas as pl
from jax.experimental.pallas import tpu as pltpu

EPS = 1e-6
NEG_INF = -1e30
CHUNK_SHIFT = 6
N_HEADS = 8
HEAD_PAD = 256
QK_NOPE = 128
QK_ROPE = 64
V_DIM = 128
CONV_WIDTH = 1024
Q_RANK = 768
KV_RANK = 512
ROPE_THETA = 10000.0
ATTN_SCALE = (QK_NOPE + QK_ROPE) ** -0.5
ADAM_LR, ADAM_B1, ADAM_B2, ADAM_EPS, ADAM_WD, ADAM_STEP = 0.001, 0.9, 0.999, 1e-08, 0.01, 10

COL_CQ = 3 * CONV_WIDTH
COL_KR = COL_CQ + Q_RANK
COL_CKV = 4096
IN_PAD = COL_CKV + KV_RANK
IN_WIDTH = 3 * CONV_WIDTH + Q_RANK + KV_RANK + QK_ROPE

VMEM_LIMIT_BYTES = 56 * 1024 * 1024
MESH = pl.DeviceIdType.MESH
ANY = pl.BlockSpec(memory_space=pl.ANY)

_NN = (((1,), (0,)), ((), ()))
_NT = (((1,), (1,)), ((), ()))
_TN = (((0,), (0,)), ((), ()))


def _params(sem):
    return pltpu.CompilerParams(dimension_semantics=sem, vmem_limit_bytes=VMEM_LIMIT_BYTES)


class _Buf:
    def __init__(self, arr):
        self.arr = arr


def _position():
    return lax.axis_index("x"), lax.axis_index("y"), lax.axis_index("c")


def _other_chips(x, y):
    return [(2 * (1 - x) + y, (1 - x, y)), (2 * x + (1 - y), (x, 1 - y)), (2 * (1 - x) + (1 - y), (1 - x, 1 - y))]


def _remote(src, dst, sems, k, to):
    send, recv, off = sems
    return pltpu.make_async_remote_copy(src_ref=src, dst_ref=dst, send_sem=send.at[off + k], recv_sem=recv.at[off + k],
                                        device_id=to, device_id_type=MESH)


class _GatherIci:
    n_sems = 2
    link = "ici"

    def __init__(self, buf, lo, n):
        self.buf, self.lo, self.n, self.buffers, self.sources = buf, lo, n, [buf], []

    def _rows(self, ref, slot, which, lo=None, n=None):
        lo, n = (self.lo, self.n) if lo is None else (lo, n)
        return ref.at[slot, pl.ds(which * (ref.shape[1] // 2) + lo, n), :]

    def start(self, refs, sems):
        ref = refs[id(self.buf)]
        x, y, c = _position()
        mine = self._rows(ref, 2 * x + y, c)
        for k, (_, xy) in enumerate(_other_chips(x, y)[:2]):
            _remote(mine, mine, sems, k, (*xy, c)).start()

    def wait(self, refs, sems):
        ref = refs[id(self.buf)]
        x, y, c = _position()
        mine = self._rows(ref, 2 * x + y, c)
        for k, (slot, xy) in enumerate(_other_chips(x, y)[:2]):
            landed = self._rows(ref, slot, c)
            _remote(landed, landed, sems, k, (*xy, c)).wait_recv()
            _remote(mine, mine, sems, k, (*xy, c)).wait_send()


class _GatherForward(_GatherIci):
    def _ways(self, x, y):
        (slot_x, xy_x), (slot_y, xy_y), (slot_d, _) = _other_chips(x, y)
        h = self.n // 2
        assert h % 16 == 0, self.n
        return [(slot_x, slot_d, self.lo, xy_y), (slot_y, slot_d, self.lo + h, xy_x)], h

    def start(self, refs, sems):
        ref = refs[id(self.buf)]
        x, y, c = _position()
        ways, h = self._ways(x, y)
        for k, (slot, _, lo, xy) in enumerate(ways):
            rows = self._rows(ref, slot, c, lo, h)
            _remote(rows, rows, sems, k, (*xy, c)).start()

    def wait(self, refs, sems):
        ref = refs[id(self.buf)]
        x, y, c = _position()
        ways, h = self._ways(x, y)
        for k, (slot, lands, lo, xy) in enumerate(ways):
            landed, sent = self._rows(ref, lands, c, lo, h), self._rows(ref, slot, c, lo, h)
            _remote(landed, landed, sems, k, (*xy, c)).wait_recv()
            _remote(sent, sent, sems, k, (*xy, c)).wait_send()


class _GatherD2d(_GatherIci):
    link = "d2d"

    def __init__(self, buf, lo, n, slots):
        super().__init__(buf, lo, n)
        self.slots, self.n_sems = slots, len(slots)

    def start(self, refs, sems):
        ref = refs[id(self.buf)]
        x, y, c = _position()
        chips = _other_chips(x, y)
        for k, which in enumerate(self.slots):
            rows = self._rows(ref, chips[which][0], c)
            _remote(rows, rows, sems, k, (x, y, 1 - c)).start()

    def wait(self, refs, sems):
        ref = refs[id(self.buf)]
        x, y, c = _position()
        chips = _other_chips(x, y)
        for k, which in enumerate(self.slots):
            sent, landed = self._rows(ref, chips[which][0], c), self._rows(ref, chips[which][0], 1 - c)
            _remote(landed, landed, sems, k, (x, y, 1 - c)).wait_recv()
            _remote(sent, sent, sems, k, (x, y, 1 - c)).wait_send()


class _ScatterDiag:
    n_sems = 2
    link = "ici"

    def __init__(self, src, via, lo, n):
        self.src, self.via, self.lo, self.n, self.buffers, self.sources = src, via, lo, n, [via], [src]

    def _copies(self, refs, sems):
        src, via = refs[id(self.src)], refs[id(self.via)]
        x, y, c = _position()
        (_, xy_x), (_, xy_y), (slot_d, _) = _other_chips(x, y)
        h2 = via.shape[0] // 2
        return [_remote(src.at[slot_d, pl.ds(first + self.lo, self.n), :], via.at[pl.ds(first + self.lo, self.n), :],
                        sems, k, (*xy, c)) for k, (first, xy) in enumerate(((0, xy_x), (h2, xy_y)))]

    def start(self, refs, sems):
        for cp in self._copies(refs, sems):
            cp.start()

    def wait(self, refs, sems):
        for cp in self._copies(refs, sems):
            cp.wait_recv()
            cp.wait_send()


class _ScatterNear:
    n_sems = 2
    link = "ici"

    def __init__(self, src, dst, lo, n):
        self.src, self.dst, self.lo, self.n, self.buffers, self.sources = src, dst, lo, n, [dst], [src]

    def _copies(self, refs, sems, landing):
        src, dst = refs[id(self.src)], refs[id(self.dst)]
        x, y, c = _position()
        rows = pl.ds(self.lo, self.n)
        return [_remote(src.at[k, rows, :], dst.at[slot if landing else 2 * x + y, rows, :], sems, k, (*xy, c))
                for k, (slot, xy) in enumerate(_other_chips(x, y)[:2])]

    def start(self, refs, sems):
        for cp in self._copies(refs, sems, False):
            cp.start()

    def wait(self, refs, sems):
        for cp in self._copies(refs, sems, True):
            cp.wait_recv()
            cp.wait_send()


class _PairExchange:
    n_sems = 1
    link = "d2d"

    def __init__(self, src, dst):
        self.src, self.dst, self.buffers, self.sources = src, dst, [dst], [src]

    def _copy(self, refs, sems):
        src, dst = refs[id(self.src)], refs[id(self.dst)]
        x, y, c = _position()
        h = src.shape[1] // 2
        return _remote(src.at[:, pl.ds((1 - c) * h, h), :], dst, sems, 0, (x, y, 1 - c))

    def start(self, refs, sems):
        self._copy(refs, sems).start()

    def wait(self, refs, sems):
        self._copy(refs, sems).wait()


class _PairShare:
    n_sems = 1
    link = "d2d"

    def __init__(self, buf):
        self.buf, self.buffers, self.sources = buf, [buf], []

    def _rows(self, ref, which):
        h = ref.shape[0] // 2
        return ref.at[pl.ds(which * h, h), :]

    def start(self, refs, sems):
        ref = refs[id(self.buf)]
        x, y, c = _position()
        _remote(self._rows(ref, c), self._rows(ref, c), sems, 0, (x, y, 1 - c)).start()

    def wait(self, refs, sems):
        ref = refs[id(self.buf)]
        x, y, c = _position()
        _remote(self._rows(ref, c), self._rows(ref, c), sems, 0, (x, y, 1 - c)).wait_send()
        _remote(self._rows(ref, 1 - c), self._rows(ref, 1 - c), sems, 0, (x, y, 1 - c)).wait_recv()


_COLLECTIVE_IDS = {("ici",): 1, ("d2d",): 2, ("d2d", "ici"): 3}


def _links(jobs):
    return tuple(sorted({j.link for j in jobs}))


def _handshake(links):
    x, y, c = _position()
    peers = ([(1 - x, y, c), (x, 1 - y, c)] if "ici" in links else []) + ([(x, y, 1 - c)] if "d2d" in links else [])
    barrier = pltpu.get_barrier_semaphore()
    for peer in peers:
        pl.semaphore_signal(barrier, inc=1, device_id=peer, device_id_type=MESH)
    pl.semaphore_wait(barrier, len(peers))


def _unique(items):
    seen, out = set(), []
    for it in items:
        if id(it) not in seen:
            seen.add(id(it))
            out.append(it)
    return out


def _job_operands(jobs):
    sources = _unique([a for j in jobs for a in j.sources])
    buffers = _unique([b for j in jobs for b in j.buffers])
    held = [b for b in buffers if not isinstance(b.arr, jax.ShapeDtypeStruct)]
    fresh = [b for b in buffers if isinstance(b.arr, jax.ShapeDtypeStruct)]
    return sources, held, fresh


def _sem_offsets(jobs):
    offs, total = [], 0
    for j in jobs:
        offs.append(total)
        total += j.n_sems
    return offs, total


def _call(body, *, name, grid, in_specs, out_specs, out_shape, args, semantics, scratch_shapes=(), jobs=(),
          prefetch=None):
    n_pre = 0 if prefetch is None else 1

    def launch(fn, in_specs, out_specs, scratch, **kw):
        if prefetch is None:
            return pl.pallas_call(fn, name=name, grid=grid, in_specs=in_specs, out_specs=out_specs,
                                  scratch_shapes=scratch, **kw)
        return pl.pallas_call(fn, name=name, grid_spec=pltpu.PrefetchScalarGridSpec(
            num_scalar_prefetch=1, grid=grid, in_specs=in_specs, out_specs=out_specs, scratch_shapes=scratch), **kw)

    pre = () if prefetch is None else (prefetch,)
    if not jobs:
        return launch(body, list(in_specs), list(out_specs), list(scratch_shapes), out_shape=list(out_shape),
                      compiler_params=_params(semantics))(*pre, *args)
    sources, held, fresh = _job_operands(jobs)
    offs, n_sem = _sem_offsets(jobs)
    links = _links(jobs)
    n_in, n_out, n_scr = len(in_specs), len(out_specs), len(scratch_shapes)
    n_src, n_held, n_fresh = len(sources), len(held), len(fresh)

    def carried(*refs):
        pre_refs, refs = refs[:n_pre], refs[n_pre:]
        ins = refs[:n_in]
        src_refs = refs[n_in:n_in + n_src]
        o0 = n_in + n_src + n_held
        outs = refs[o0:o0 + n_out]
        buf_refs = refs[o0 + n_out:o0 + n_out + n_held + n_fresh]
        s0 = o0 + n_out + n_held + n_fresh
        scratch = refs[s0:s0 + n_scr]
        send, recv = refs[s0 + n_scr], refs[s0 + n_scr + 1]
        where = {id(a): r for a, r in zip(sources, src_refs)}
        where.update({id(b): r for b, r in zip(held + fresh, buf_refs)})
        ids = [pl.program_id(a) for a in range(len(grid))]
        first = functools.reduce(jnp.logical_and, [i == 0 for i in ids])
        last = functools.reduce(jnp.logical_and, [i == g - 1 for i, g in zip(ids, grid)])

        @pl.when(first)
        def _():
            _handshake(links)
            for j, off in zip(jobs, offs):
                j.start(where, (send, recv, off))

        body(*pre_refs, *ins, *outs, *scratch)

        @pl.when(last)
        def _():
            for j, off in zip(jobs, offs):
                j.wait(where, (send, recv, off))

    shape_of = lambda b: jax.ShapeDtypeStruct(b.arr.shape, b.arr.dtype)
    res = launch(
        carried, [*in_specs, *[ANY] * (n_src + n_held)], [*out_specs, *[ANY] * (n_held + n_fresh)],
        [*scratch_shapes, pltpu.SemaphoreType.DMA((n_sem,)), pltpu.SemaphoreType.DMA((n_sem,))],
        out_shape=[*out_shape, *[shape_of(b) for b in held + fresh]],
        input_output_aliases={n_pre + n_in + n_src + i: n_out + i for i in range(n_held)},
        compiler_params=pltpu.CompilerParams(dimension_semantics=("arbitrary",) * len(grid),
                                             vmem_limit_bytes=VMEM_LIMIT_BYTES, has_side_effects=True,
                                             collective_id=_COLLECTIVE_IDS[links]),
    )(*pre, *args, *sources, *[b.arr for b in held])
    for b, new in zip(held + fresh, res[n_out:]):
        b.arr = new
    return list(res[:n_out])


def _comm(name, phases):
    jobs = [j for ph in phases for j in ph]
    sources, held, fresh = _job_operands(jobs)
    offs, n_sem = _sem_offsets(jobs)
    off_of = {id(j): o for j, o in zip(jobs, offs)}
    links = _links(jobs)
    n_src, n_held, n_fresh = len(sources), len(held), len(fresh)

    def body(*refs):
        src_refs = refs[:n_src]
        buf_refs = refs[n_src + n_held:n_src + 2 * n_held + n_fresh]
        send, recv = refs[-2], refs[-1]
        where = {id(a): r for a, r in zip(sources, src_refs)}
        where.update({id(b): r for b, r in zip(held + fresh, buf_refs)})
        _handshake(links)
        for ph in phases:
            for j in ph:
                j.start(where, (send, recv, off_of[id(j)]))
            for j in ph:
                j.wait(where, (send, recv, off_of[id(j)]))

    shape_of = lambda b: jax.ShapeDtypeStruct(b.arr.shape, b.arr.dtype)
    res = pl.pallas_call(
        body, name=name,
        in_specs=[ANY] * (n_src + n_held), out_specs=[ANY] * (n_held + n_fresh),
        out_shape=[shape_of(b) for b in held + fresh],
        input_output_aliases={n_src + i: i for i in range(n_held)},
        scratch_shapes=[pltpu.SemaphoreType.DMA((n_sem,)), pltpu.SemaphoreType.DMA((n_sem,))],
        compiler_params=pltpu.CompilerParams(has_side_effects=True, collective_id=_COLLECTIVE_IDS[links]),
    )(*sources, *[b.arr for b in held])
    for b, new in zip(held + fresh, res):
        b.arr = new


def _matmul(a, b, *, dims, mnk, tiles, name, out_dtype=jnp.float32, a_spec=None, b_spec=None,
            out_shape=None, o_spec=None, epilogue=None, extra=(), extra_specs=(), jobs=()):
    m, n, k = mnk
    tm, tn, tk = tiles
    assert m % tm == 0 and n % tn == 0 and k % tk == 0, (name, mnk, tiles)
    gm, gn, gk = m // tm, n // tn, k // tk
    if a_spec is None:
        a_spec = (pl.BlockSpec((tk, tm), lambda i, j, l: (l, i)) if dims is _TN
                  else pl.BlockSpec((tm, tk), lambda i, j, l: (i, l)))
    if b_spec is None:
        b_spec = (pl.BlockSpec((tn, tk), lambda i, j, l: (j, l)) if dims is _NT
                  else pl.BlockSpec((tk, tn), lambda i, j, l: (l, j)))
    if out_shape is None:
        out_shape = jax.ShapeDtypeStruct((m, n), out_dtype)
    if o_spec is None:
        o_spec = pl.BlockSpec((tm, tn), lambda i, j, l: (i, j))
    single = not isinstance(out_shape, (tuple, list))
    n_extra = len(extra)

    def finish(acc, extra_refs, out_refs):
        if epilogue is None:
            out_refs[0][...] = acc.astype(out_refs[0].dtype)
        else:
            epilogue(acc, extra_refs, out_refs)

    def body_whole_k(*refs):
        a_ref, b_ref = refs[0], refs[1]
        acc = lax.dot_general(a_ref[...], b_ref[...], dims, preferred_element_type=jnp.float32)
        finish(acc, refs[2:2 + n_extra], refs[2 + n_extra:])

    def body_split_k(*refs):
        a_ref, b_ref = refs[0], refs[1]
        extra_refs = refs[2:2 + n_extra]
        out_refs = refs[2 + n_extra:-1]
        acc_ref = refs[-1]
        step = pl.program_id(2)
        part = lax.dot_general(a_ref[...], b_ref[...], dims, preferred_element_type=jnp.float32)

        @pl.when(step == 0)
        def _():
            acc_ref[...] = part

        @pl.when(jnp.logical_and(step > 0, step < gk - 1))
        def _():
            acc_ref[...] += part

        @pl.when(step == gk - 1)
        def _():
            finish(acc_ref[...] + part, extra_refs, out_refs)

    res = _call(
        body_whole_k if gk == 1 else body_split_k, name=name, grid=(gm, gn, gk),
        in_specs=[a_spec, b_spec, *extra_specs],
        out_specs=[o_spec] if single else list(o_spec),
        out_shape=[out_shape] if single else list(out_shape),
        scratch_shapes=[] if gk == 1 else [pltpu.VMEM((tm, tn), jnp.float32)],
        semantics=("parallel", "parallel", "arbitrary"), args=(a, b, *extra), jobs=jobs)
    return res[0] if single else res


def _rstd(x):
    return lax.rsqrt(jnp.mean(x * x, axis=-1, keepdims=True) + EPS)


def _rms_bwd_rows(x, g, dy):
    r = _rstd(x)
    xn = x * r
    dyg = dy * g
    dx = r * (dyg - xn * jnp.mean(xn * dyg, axis=-1, keepdims=True))
    return dx, dy * xn


def _acc_rows(ref, rows, first):
    part = jnp.sum(rows, axis=0, keepdims=True)

    @pl.when(first)
    def _():
        ref[...] = part

    @pl.when(jnp.logical_not(first))
    def _():
        ref[...] += part


def _rms_fwd(x, g, *, width, col, tm, name, jobs=()):
    s = x.shape[0]

    def body(x_ref, g_ref, o_ref):
        v = x_ref[...]
        o_ref[...] = (v * _rstd(v) * g_ref[...]).astype(o_ref.dtype)

    return _call(
        body, name=name, grid=(s // tm,),
        in_specs=[pl.BlockSpec((tm, width), lambda i: (i, col)), pl.BlockSpec((1, width), lambda i: (0, 0))],
        out_specs=[pl.BlockSpec((tm, width), lambda i: (i, 0))],
        out_shape=[jax.ShapeDtypeStruct((s, width), jnp.bfloat16)],
        semantics=("parallel",), args=(x, g), jobs=jobs)[0]


def _rms_bwd(x, g, dy, *, width, col, tm, name, jobs=()):
    s = x.shape[0]

    def body(x_ref, g_ref, dy_ref, dx_ref, dg_ref):
        dx, dgr = _rms_bwd_rows(x_ref[...], g_ref[...], dy_ref[...])
        dx_ref[...] = dx.astype(dx_ref.dtype)
        _acc_rows(dg_ref, dgr, pl.program_id(0) == 0)

    return _call(
        body, name=name, grid=(s // tm,),
        in_specs=[pl.BlockSpec((tm, width), lambda i: (i, col)), pl.BlockSpec((1, width), lambda i: (0, 0)),
                  pl.BlockSpec((tm, width), lambda i: (i, 0))],
        out_specs=[pl.BlockSpec((tm, width), lambda i: (i, 0)), pl.BlockSpec((1, width), lambda i: (0, 0))],
        out_shape=[jax.ShapeDtypeStruct((s, width), jnp.bfloat16), jax.ShapeDtypeStruct((1, width), jnp.float32)],
        semantics=("arbitrary",), args=(x, g, dy), jobs=jobs)


def _latent_norms_fwd(proj, g_q, g_kv, *, tm, jobs=()):
    s = proj.shape[0]

    def body(q_ref, kv_ref, gq_ref, gkv_ref, qo_ref, kvo_ref):
        for x_ref, g_ref, o_ref in ((q_ref, gq_ref, qo_ref), (kv_ref, gkv_ref, kvo_ref)):
            v = x_ref[...]
            o_ref[...] = (v * _rstd(v) * g_ref[...]).astype(o_ref.dtype)

    rows = lambda w, c: pl.BlockSpec((tm, w), lambda i: (i, c))
    gain = lambda w: pl.BlockSpec((1, w), lambda i: (0, 0))
    return _call(
        body, name="latent_norms_fwd", grid=(s // tm,),
        in_specs=[rows(Q_RANK, COL_CQ // Q_RANK), rows(KV_RANK, COL_CKV // KV_RANK), gain(Q_RANK), gain(KV_RANK)],
        out_specs=[rows(Q_RANK, 0), rows(KV_RANK, 0)],
        out_shape=[jax.ShapeDtypeStruct((s, Q_RANK), jnp.bfloat16), jax.ShapeDtypeStruct((s, KV_RANK), jnp.bfloat16)],
        semantics=("parallel",), args=(proj, proj, g_q, g_kv), jobs=jobs)


def _latent_norms_bwd(proj, g_q, g_kv, dq, dkv, *, tm, jobs=()):
    s = proj.shape[0]

    def body(q_ref, kv_ref, gq_ref, gkv_ref, dq_ref, dkv_ref, dqo_ref, dkvo_ref, dgq_ref, dgkv_ref):
        first = pl.program_id(0) == 0
        for x_ref, g_ref, dy_ref, dx_ref, dg_ref in ((q_ref, gq_ref, dq_ref, dqo_ref, dgq_ref),
                                                     (kv_ref, gkv_ref, dkv_ref, dkvo_ref, dgkv_ref)):
            dx, dgr = _rms_bwd_rows(x_ref[...], g_ref[...], dy_ref[...])
            dx_ref[...] = dx.astype(dx_ref.dtype)
            _acc_rows(dg_ref, dgr, first)

    rows = lambda w, c: pl.BlockSpec((tm, w), lambda i: (i, c))
    gain = lambda w: pl.BlockSpec((1, w), lambda i: (0, 0))
    return _call(
        body, name="latent_norms_bwd", grid=(s // tm,),
        in_specs=[rows(Q_RANK, COL_CQ // Q_RANK), rows(KV_RANK, COL_CKV // KV_RANK), gain(Q_RANK), gain(KV_RANK),
                  rows(Q_RANK, 0), rows(KV_RANK, 0)],
        out_specs=[rows(Q_RANK, 0), rows(KV_RANK, 0), gain(Q_RANK), gain(KV_RANK)],
        out_shape=[jax.ShapeDtypeStruct((s, Q_RANK), jnp.bfloat16), jax.ShapeDtypeStruct((s, KV_RANK), jnp.bfloat16),
                   jax.ShapeDtypeStruct((1, Q_RANK), jnp.float32), jax.ShapeDtypeStruct((1, KV_RANK), jnp.float32)],
        semantics=("arbitrary",), args=(proj, proj, g_q, g_kv, dq, dkv), jobs=jobs)


def _row_specs(tm, d, n):
    return [pl.BlockSpec((tm, d), lambda i: (i, 0)) for _ in range(n)]


def _gain_specs(d, n):
    return [pl.BlockSpec((1, d), lambda i: (0, 0)) for _ in range(n)]


def _mix_residual_fwd(x, mix, g_post_mix, g_pre_mlp, *, tm, jobs=()):
    s, d = x.shape

    def body(x_ref, mix_ref, g1_ref, g2_ref, x2_ref, h2_ref):
        mixv = mix_ref[...]
        x2 = x_ref[...] + mixv * _rstd(mixv) * g1_ref[...]
        x2_ref[...] = x2
        h2_ref[...] = (x2 * _rstd(x2) * g2_ref[...]).astype(h2_ref.dtype)

    return _call(
        body, name="mix_residual_fwd", grid=(s // tm,),
        in_specs=_row_specs(tm, d, 2) + _gain_specs(d, 2),
        out_specs=_row_specs(tm, d, 2),
        out_shape=[jax.ShapeDtypeStruct((s, d), jnp.float32), jax.ShapeDtypeStruct((s, d), jnp.bfloat16)],
        semantics=("parallel",), args=(x, mix, g_post_mix, g_pre_mlp), jobs=jobs)


def _loss_head(x2, mlp, target, g_post_mlp, *, tm, jobs=()):
    s, d = x2.shape

    def body(x2_ref, m_ref, t_ref, g_ref, dx3_ref, dm_ref, dg_ref, loss_ref):
        first = pl.program_id(0) == 0
        mv = m_ref[...]
        g = g_ref[...]
        diff = x2_ref[...] + mv * _rstd(mv) * g - t_ref[...]
        dx3 = diff * (1.0 / d)
        dx3_ref[...] = dx3
        dm, dgr = _rms_bwd_rows(mv, g, dx3)
        dm_ref[...] = dm.astype(dm_ref.dtype)
        _acc_rows(dg_ref, dgr, first)
        part = 0.5 * jnp.sum(jnp.mean(diff * diff, axis=-1, keepdims=True), axis=0, keepdims=True)
        _acc_rows(loss_ref, jnp.broadcast_to(part, (1, 128)), first)

    return _call(
        body, name="loss_head", grid=(s // tm,),
        in_specs=_row_specs(tm, d, 3) + _gain_specs(d, 1),
        out_specs=_row_specs(tm, d, 2) + _gain_specs(d, 1) + [pl.BlockSpec((1, 128), lambda i: (0, 0))],
        out_shape=[jax.ShapeDtypeStruct((s, d), jnp.float32), jax.ShapeDtypeStruct((s, d), jnp.bfloat16),
                   jax.ShapeDtypeStruct((1, d), jnp.float32), jax.ShapeDtypeStruct((1, 128), jnp.float32)],
        semantics=("arbitrary",), args=(x2, mlp, target, g_post_mlp), jobs=jobs)


def _mix_residual_bwd(dx3, dh2, x2, mix, g_pre_mlp, g_post_mix, *, tm, jobs=()):
    s, d = x2.shape

    def body(dx3_ref, dh2_ref, x2_ref, mix_ref, g2_ref, g1_ref, dx2_ref, dmix_ref, dg2_ref, dg1_ref):
        first = pl.program_id(0) == 0
        d_in, dgr2 = _rms_bwd_rows(x2_ref[...], g2_ref[...], dh2_ref[...])
        dx2 = dx3_ref[...] + d_in
        dx2_ref[...] = dx2
        dmix, dgr1 = _rms_bwd_rows(mix_ref[...], g1_ref[...], dx2)
        dmix_ref[...] = dmix.astype(dmix_ref.dtype)
        _acc_rows(dg2_ref, dgr2, first)
        _acc_rows(dg1_ref, dgr1, first)

    return _call(
        body, name="mix_residual_bwd", grid=(s // tm,),
        in_specs=_row_specs(tm, d, 4) + _gain_specs(d, 2),
        out_specs=_row_specs(tm, d, 2) + _gain_specs(d, 2),
        out_shape=[jax.ShapeDtypeStruct((s, d), jnp.float32), jax.ShapeDtypeStruct((s, d), jnp.bfloat16),
                   jax.ShapeDtypeStruct((1, d), jnp.float32), jax.ShapeDtypeStruct((1, d), jnp.float32)],
        semantics=("arbitrary",), args=(dx3, dh2, x2, mix, g_pre_mlp, g_post_mix), jobs=jobs)


def _input_bwd(dx2, dh1, x, g_pre_mix, *, tm, jobs=()):
    s, d = x.shape

    def body(dx2_ref, dh1_ref, x_ref, g_ref, dx_ref, dg_ref):
        d_in, dgr = _rms_bwd_rows(x_ref[...], g_ref[...], dh1_ref[...])
        dx_ref[...] = dx2_ref[...] + d_in
        _acc_rows(dg_ref, dgr, pl.program_id(0) == 0)

    return _call(
        body, name="input_bwd", grid=(s // tm,),
        in_specs=_row_specs(tm, d, 3) + _gain_specs(d, 1),
        out_specs=_row_specs(tm, d, 1) + _gain_specs(d, 1),
        out_shape=[jax.ShapeDtypeStruct((s, d), jnp.float32), jax.ShapeDtypeStruct((1, d), jnp.float32)],
        semantics=("arbitrary",), args=(dx2, dh1, x, g_pre_mix), jobs=jobs)


def _shift_rows(z, by):
    s = z.shape[0]
    rows = lax.broadcasted_iota(jnp.int32, z.shape, 0)
    rolled = pltpu.roll(z, by % s, axis=0)
    keep = rows >= by if by > 0 else rows < s + by
    return jnp.where(keep, rolled, 0.0)


def _conv_fwd(proj, conv_w, conv_out_g, jobs=()):
    s = proj.shape[0]
    groups = CONV_WIDTH // 128

    def body(u_ref, gb_ref, gc_ref, w_ref, g_ref, y_ref):
        z = gc_ref[...] * u_ref[...]
        w = w_ref[...]
        conv = w[0:1, :] * _shift_rows(z, 2) + w[1:2, :] * _shift_rows(z, 1) + w[2:3, :] * z
        y = gb_ref[...] * conv
        y_ref[...] = (y * _rstd(y) * g_ref[...]).astype(y_ref.dtype)

    col = lambda base: pl.BlockSpec((s, 128), lambda j: (0, base + j))
    return _call(
        body, name="conv_fwd", grid=(groups,),
        in_specs=[col(0), col(groups), col(2 * groups), pl.BlockSpec((3, 128), lambda j: (0, j)),
                  pl.BlockSpec((1, 128), lambda j: (0, j))],
        out_specs=[pl.BlockSpec((s, 128), lambda j: (0, j))],
        out_shape=[jax.ShapeDtypeStruct((s, CONV_WIDTH), jnp.bfloat16)],
        semantics=("parallel",), args=(proj, proj, proj, conv_w, conv_out_g), jobs=jobs)[0]


def _conv_bwd(proj, conv_w, conv_out_g, dycat, jobs=()):
    s = proj.shape[0]
    groups = CONV_WIDTH // 128

    def body(u_ref, gb_ref, gc_ref, w_ref, g_ref, dy_ref, du_ref, dgb_ref, dgc_ref, dw_ref, dg_ref):
        u, gb, gc = u_ref[...], gb_ref[...], gc_ref[...]
        w = w_ref[...]
        z = gc * u
        z1, z2 = _shift_rows(z, 1), _shift_rows(z, 2)
        conv = w[0:1, :] * z2 + w[1:2, :] * z1 + w[2:3, :] * z
        dyr, dgr = _rms_bwd_rows(gb * conv, g_ref[...], dy_ref[...])
        dg_ref[...] = jnp.sum(dgr, axis=0, keepdims=True)
        dgb_ref[...] = (dyr * conv).astype(dgb_ref.dtype)
        dconv = dyr * gb
        dw_ref[0:1, :] = jnp.sum(dconv * z2, axis=0, keepdims=True)
        dw_ref[1:2, :] = jnp.sum(dconv * z1, axis=0, keepdims=True)
        dw_ref[2:3, :] = jnp.sum(dconv * z, axis=0, keepdims=True)
        dz = w[2:3, :] * dconv + w[1:2, :] * _shift_rows(dconv, -1) + w[0:1, :] * _shift_rows(dconv, -2)
        dgc_ref[...] = (dz * u).astype(dgc_ref.dtype)
        du_ref[...] = (dz * gc).astype(du_ref.dtype)

    col = lambda base: pl.BlockSpec((s, 128), lambda j: (0, base + j))
    act = jax.ShapeDtypeStruct((s, CONV_WIDTH), jnp.bfloat16)
    return _call(
        body, name="conv_bwd", grid=(groups,),
        in_specs=[col(0), col(groups), col(2 * groups), pl.BlockSpec((3, 128), lambda j: (0, j)),
                  pl.BlockSpec((1, 128), lambda j: (0, j)), col(0)],
        out_specs=[col(0), col(0), col(0), pl.BlockSpec((3, 128), lambda j: (0, j)),
                   pl.BlockSpec((1, 128), lambda j: (0, j))],
        out_shape=[act, act, act, jax.ShapeDtypeStruct((3, CONV_WIDTH), jnp.float32),
                   jax.ShapeDtypeStruct((1, CONV_WIDTH), jnp.float32)],
        semantics=("parallel",), args=(proj, proj, proj, conv_w, conv_out_g, dycat), jobs=jobs)


def _rope_tables(s):
    pos = jnp.arange(s, dtype=jnp.float32)
    inv_freq = jnp.power(ROPE_THETA, -jnp.arange(0, QK_ROPE, 2, dtype=jnp.float32) / QK_ROPE)
    ang = pos[:, None] * inv_freq[None, :]
    cos, sin = jnp.cos(ang), jnp.sin(ang)
    zeros = jnp.zeros((s, 128 - QK_ROPE), jnp.float32)
    return (jnp.concatenate([cos, cos, zeros], axis=1), jnp.concatenate([-sin, sin, zeros], axis=1))


def _swap_halves(x):
    lane = lax.broadcasted_iota(jnp.int32, x.shape, 1)
    swapped = jnp.where(lane < QK_ROPE // 2, pltpu.roll(x, 128 - QK_ROPE // 2, axis=1),
                        pltpu.roll(x, QK_ROPE // 2, axis=1))
    return jnp.where(lane < QK_ROPE, swapped, 0.0)


def _rope(x, cos, sin):
    return x * cos + _swap_halves(x) * sin


def _rope_transposed(d, cos, sin):
    return d * cos + _swap_halves(d * sin)


def _qk_rope_fwd(q_pad, proj, cos, sin, *, tm, jobs=()):
    s = q_pad.shape[0]
    wq = N_HEADS * HEAD_PAD

    def body(q_ref, kr_ref, cos_ref, sin_ref, qo_ref, kro_ref):
        c, sn = cos_ref[...], sin_ref[...]
        for h in range(N_HEADS):
            lo = h * HEAD_PAD
            qo_ref[:, lo:lo + 128] = q_ref[:, lo:lo + 128].astype(qo_ref.dtype)
            qo_ref[:, lo + 128:lo + 256] = _rope(q_ref[:, lo + 128:lo + 256], c, sn).astype(qo_ref.dtype)
        kro_ref[...] = _rope(kr_ref[...], c, sn).astype(kro_ref.dtype)

    return _call(
        body, name="qk_rope_fwd", grid=(s // tm,),
        in_specs=[pl.BlockSpec((tm, wq), lambda i: (i, 0)), pl.BlockSpec((tm, 128), lambda i: (i, COL_KR // 128)),
                  pl.BlockSpec((tm, 128), lambda i: (i, 0)), pl.BlockSpec((tm, 128), lambda i: (i, 0))],
        out_specs=[pl.BlockSpec((tm, wq), lambda i: (i, 0)), pl.BlockSpec((tm, 128), lambda i: (i, 0))],
        out_shape=[jax.ShapeDtypeStruct((s, wq), jnp.bfloat16), jax.ShapeDtypeStruct((s, 128), jnp.bfloat16)],
        semantics=("parallel",), args=(q_pad, proj, cos, sin), jobs=jobs)


def _qk_rope_bwd(dq_pad, dk_pad, dv, cos, sin, *, tm, jobs=()):
    s = dq_pad.shape[0]
    wq = N_HEADS * HEAD_PAD

    def body(dq_ref, dk_ref, dv_ref, cos_ref, sin_ref, dqo_ref, dkv_ref, dkr_ref):
        c, sn = cos_ref[...], sin_ref[...]
        dkr = jnp.zeros((tm, 128), jnp.float32)
        for h in range(N_HEADS):
            lo = h * HEAD_PAD
            dqo_ref[:, lo:lo + 128] = dq_ref[:, lo:lo + 128].astype(dqo_ref.dtype)
            dqo_ref[:, lo + 128:lo + 256] = _rope_transposed(dq_ref[:, lo + 128:lo + 256], c, sn).astype(dqo_ref.dtype)
            dkv_ref[:, lo:lo + 128] = dk_ref[:, lo:lo + 128].astype(dkv_ref.dtype)
            dkv_ref[:, lo + 128:lo + 256] = dv_ref[:, h * V_DIM:(h + 1) * V_DIM].astype(dkv_ref.dtype)
            dkr = dkr + dk_ref[:, lo + 128:lo + 256]
        dkr_ref[...] = _rope_transposed(dkr, c, sn).astype(dkr_ref.dtype)

    return _call(
        body, name="qk_rope_bwd", grid=(s // tm,),
        in_specs=[pl.BlockSpec((tm, wq), lambda i: (i, 0)), pl.BlockSpec((tm, wq), lambda i: (i, 0)),
                  pl.BlockSpec((tm, N_HEADS * V_DIM), lambda i: (i, 0)),
                  pl.BlockSpec((tm, 128), lambda i: (i, 0)), pl.BlockSpec((tm, 128), lambda i: (i, 0))],
        out_specs=[pl.BlockSpec((tm, wq), lambda i: (i, 0)), pl.BlockSpec((tm, wq), lambda i: (i, 0)),
                   pl.BlockSpec((tm, 128), lambda i: (i, 0))],
        out_shape=[jax.ShapeDtypeStruct((s, wq), jnp.bfloat16), jax.ShapeDtypeStruct((s, wq), jnp.bfloat16),
                   jax.ShapeDtypeStruct((s, 128), jnp.bfloat16)],
        semantics=("parallel",), args=(dq_pad, dk_pad, dv, cos, sin), jobs=jobs)


def _visible(q0, k0, t):
    qpos = q0 + lax.broadcasted_iota(jnp.int32, (t, t), 0)
    kpos = k0 + lax.broadcasted_iota(jnp.int32, (t, t), 1)
    return lax.shift_right_logical(kpos, CHUNK_SHIFT) <= lax.shift_right_logical(qpos, CHUNK_SHIFT)


def _attn_fwd(q, kv, kr, attn_out_g, *, t, jobs=()):
    s = q.shape[0]
    nq = s // t

    def body(q_ref, kn_ref, v_ref, kr_ref, g_ref, o_ref, lse_ref, y_ref, kcat_ref):
        i = pl.program_id(1)

        @pl.when(i == 0)
        def _():
            kcat_ref[:, 0:128] = kn_ref[...]
            kcat_ref[:, 128:256] = kr_ref[...]

        qv = q_ref[...]

        def step(j, carry, diagonal):
            m, l, acc = carry
            k = kcat_ref[pl.ds(pl.multiple_of(j * t, t), t), :]
            v = v_ref[pl.ds(pl.multiple_of(j * t, t), t), :]
            sc = lax.dot_general(qv, k, _NT, preferred_element_type=jnp.float32) * ATTN_SCALE
            if diagonal:
                sc = jnp.where(_visible(0, 0, t), sc, NEG_INF)
            m_new = jnp.maximum(m, jnp.max(sc, axis=-1, keepdims=True))
            p = jnp.exp(sc - m_new)
            alpha = jnp.exp(m - m_new)
            l = alpha * l + jnp.sum(p, axis=-1, keepdims=True)
            acc = alpha * acc + lax.dot_general(p.astype(jnp.bfloat16), v, _NN, preferred_element_type=jnp.float32)
            return m_new, l, acc

        init = (jnp.full((t, 1), NEG_INF, jnp.float32), jnp.zeros((t, 1), jnp.float32),
                jnp.zeros((t, V_DIM), jnp.float32))
        before = lax.fori_loop(0, i, functools.partial(step, diagonal=False), init)
        m, l, acc = step(i, before, True)
        o = acc / l
        o_ref[...] = o
        lse_ref[...] = jnp.broadcast_to(m + jnp.log(l), (t, 128))
        y_ref[...] = (o * _rstd(o) * g_ref[...]).astype(y_ref.dtype)

    head_rows = lambda w, f: pl.BlockSpec((s, w), lambda h, i: (0, f(h)))
    blk = pl.BlockSpec((t, 128), lambda h, i: (i, h))
    full = jax.ShapeDtypeStruct((s, N_HEADS * V_DIM), jnp.float32)
    return _call(
        body, name="attn_fwd", grid=(N_HEADS, nq),
        in_specs=[pl.BlockSpec((t, HEAD_PAD), lambda h, i: (i, h)), head_rows(128, lambda h: 2 * h),
                  head_rows(128, lambda h: 2 * h + 1), head_rows(128, lambda h: 0),
                  pl.BlockSpec((1, 128), lambda h, i: (0, h))],
        out_specs=[blk, blk, blk],
        out_shape=[full, full, jax.ShapeDtypeStruct((s, N_HEADS * V_DIM), jnp.bfloat16)],
        scratch_shapes=[pltpu.VMEM((s, HEAD_PAD), jnp.bfloat16)],
        semantics=("arbitrary", "arbitrary"), args=(q, kv, kv, kr, attn_out_g), jobs=jobs)


def _attn_bwd(q, kv, kr, o, lse, attn_out_g, dycat, *, t, jobs=()):
    s = q.shape[0]
    nq = s // t

    def body(q_ref, kn_ref, v_ref, kr_ref, o_ref, lse_ref, g_ref, dy_ref, dq_ref, dk_ref, dv_ref, dg_ref,
             kcat_ref, do_ref, delta_ref):
        ov = o_ref[...]
        do, dgr = _rms_bwd_rows(ov, g_ref[...], dy_ref[...])
        do_ref[...] = do.astype(do_ref.dtype)
        delta_ref[...] = jnp.broadcast_to(jnp.sum(do * ov, axis=-1, keepdims=True), (s, 128))
        dg_ref[...] = jnp.sum(dgr, axis=0, keepdims=True)
        kcat_ref[:, 0:128] = kn_ref[...]
        kcat_ref[:, 128:256] = kr_ref[...]
        dq_ref[...] = jnp.zeros_like(dq_ref)
        dk_ref[...] = jnp.zeros_like(dk_ref)
        dv_ref[...] = jnp.zeros_like(dv_ref)

        def kv_step(j, _):
            krows = pl.ds(pl.multiple_of(j * t, t), t)
            k = kcat_ref[krows, :]
            v = v_ref[krows, :]

            def q_step(i, _, diagonal):
                qrows = pl.ds(pl.multiple_of(i * t, t), t)
                qv = q_ref[qrows, :]
                dov = do_ref[qrows, :]
                sc = lax.dot_general(qv, k, _NT, preferred_element_type=jnp.float32) * ATTN_SCALE
                if diagonal:
                    sc = jnp.where(_visible(0, 0, t), sc, NEG_INF)
                p = jnp.exp(sc - lse_ref[qrows, :][:, 0:1])
                dp = lax.dot_general(dov, v, _NT, preferred_element_type=jnp.float32)
                ds = (p * (dp - delta_ref[qrows, :][:, 0:1]) * ATTN_SCALE).astype(jnp.bfloat16)
                dv_ref[krows, :] += lax.dot_general(p.astype(jnp.bfloat16), dov, _TN,
                                                    preferred_element_type=jnp.float32)
                dk_ref[krows, :] += lax.dot_general(ds, qv, _TN, preferred_element_type=jnp.float32)
                dq_ref[qrows, :] += lax.dot_general(ds, k, _NN, preferred_element_type=jnp.float32)
                return 0

            q_step(j, 0, True)
            lax.fori_loop(j + 1, nq, functools.partial(q_step, diagonal=False), 0)
            return 0

        lax.fori_loop(0, nq, kv_step, 0)

    col = lambda w, f: pl.BlockSpec((s, w), lambda h: (0, f(h)))
    return _call(
        body, name="attn_bwd", grid=(N_HEADS,),
        in_specs=[col(HEAD_PAD, lambda h: h), col(128, lambda h: 2 * h), col(128, lambda h: 2 * h + 1),
                  col(128, lambda h: 0), col(128, lambda h: h), col(128, lambda h: h),
                  pl.BlockSpec((1, 128), lambda h: (0, h)), col(128, lambda h: CONV_WIDTH // 128 + h)],
        out_specs=[col(HEAD_PAD, lambda h: h), col(HEAD_PAD, lambda h: h), col(128, lambda h: h),
                   pl.BlockSpec((1, 128), lambda h: (0, h))],
        out_shape=[jax.ShapeDtypeStruct((s, N_HEADS * HEAD_PAD), jnp.float32),
                   jax.ShapeDtypeStruct((s, N_HEADS * HEAD_PAD), jnp.float32),
                   jax.ShapeDtypeStruct((s, N_HEADS * V_DIM), jnp.float32),
                   jax.ShapeDtypeStruct((1, N_HEADS * V_DIM), jnp.float32)],
        scratch_shapes=[pltpu.VMEM((s, HEAD_PAD), jnp.bfloat16), pltpu.VMEM((s, V_DIM), jnp.bfloat16),
                        pltpu.VMEM((s, 128), jnp.float32)],
        semantics=("parallel",), args=(q, kv, kv, kr, o, lse, attn_out_g, dycat), jobs=jobs)


def _row_tile(rows, cap=256):
    for cand in (512, 256, 128, 64, 32, 16, 8):
        if cand <= cap and rows % cand == 0:
            return cand
    return rows


def _cast_into_slot(w, pos, *, name, jobs=()):
    r, c = w.shape
    tr = _row_tile(r)

    def body(pos_ref, w_ref, o_ref):
        o_ref[...] = w_ref[...].astype(o_ref.dtype)

    return _call(
        body, name=name, grid=(r // tr,), prefetch=pos,
        in_specs=[pl.BlockSpec((tr, c), lambda i, p: (i, 0))],
        out_specs=[pl.BlockSpec((None, tr, c), lambda i, p: (p[1], i, 0))],
        out_shape=[jax.ShapeDtypeStruct((4, r, c), jnp.bfloat16)],
        semantics=("parallel",), args=(w,), jobs=jobs)[0]


def _cast_many_into_slots(ws, pos, *, name, jobs=()):
    steps = 8
    assert all(w.shape[0] % (16 * steps) == 0 for w in ws), [w.shape for w in ws]

    def body(pos_ref, *refs):
        for w_ref, o_ref in zip(refs[:len(ws)], refs[len(ws):]):
            o_ref[...] = w_ref[...].astype(o_ref.dtype)

    return _call(
        body, name=name, grid=(steps,), prefetch=pos,
        in_specs=[pl.BlockSpec((w.shape[0] // steps, w.shape[1]), lambda i, p: (i, 0)) for w in ws],
        out_specs=[pl.BlockSpec((None, w.shape[0] // steps, w.shape[1]), lambda i, p: (p[1], i, 0)) for w in ws],
        out_shape=[jax.ShapeDtypeStruct((4, *w.shape), jnp.bfloat16) for w in ws],
        semantics=("parallel",), args=tuple(ws), jobs=jobs)


def _pair_add(g, theirs, pos, *, name, jobs=()):
    n, h, c = theirs.shape
    tr = _row_tile(h, cap=512)
    nb = h // tr

    def body(pos_ref, a_ref, b_ref, o_ref):
        o_ref[...] = (a_ref[...].astype(jnp.float32) + b_ref[...].astype(jnp.float32)).astype(o_ref.dtype)

    spec = pl.BlockSpec((None, tr, c), lambda j, i, p: (j, i, 0))
    return _call(
        body, name=name, grid=(n, nb), prefetch=pos,
        in_specs=[pl.BlockSpec((None, tr, c), lambda j, i, p: (j, i + p[0] * nb, 0)), spec],
        out_specs=[spec], out_shape=[jax.ShapeDtypeStruct(theirs.shape, jnp.bfloat16)],
        semantics=("parallel", "parallel"), args=(g, theirs), jobs=jobs)[0]


def _fold_diag(pair_sum, via, pos, *, name):
    n, h, c = pair_sum.shape
    tr = _row_tile(h // 2, cap=512)
    nb = h // tr

    def body(pos_ref, p_ref, via_ref, o_ref):
        j, i = pl.program_id(0), pl.program_id(1)
        mine = p_ref[...].astype(jnp.float32)
        add = (j == 0) == (i >= nb // 2)
        o_ref[...] = jnp.where(add, mine + via_ref[...].astype(jnp.float32), mine).astype(o_ref.dtype)

    return _call(
        body, name=name, grid=(2, nb), prefetch=pos,
        in_specs=[pl.BlockSpec((None, tr, c), lambda j, i, p: (jnp.bitwise_xor(p[1], 2 - j), i, 0)),
                  pl.BlockSpec((tr, c), lambda j, i, p: (i, 0))],
        out_specs=[pl.BlockSpec((None, tr, c), lambda j, i, p: (j, i, 0))],
        out_shape=[jax.ShapeDtypeStruct((2, h, c), jnp.bfloat16)],
        semantics=("parallel", "parallel"), args=(pair_sum, via))[0]


def _chip_sum(by_source, pair_sum, pos, *, name):
    n, h, c = by_source.shape
    tr = _row_tile(h, cap=512)
    nb = h // tr

    def body(pos_ref, own_ref, px_ref, py_ref, o_ref):
        f = lambda ref: ref[...].astype(jnp.float32)
        o_ref[...] = (f(own_ref) + f(px_ref)) + f(py_ref)

    slot = lambda flip: pl.BlockSpec((None, tr, c), lambda i, p: (jnp.bitwise_xor(p[1], flip), i, 0))
    return pl.pallas_call(
        body, name=name, out_shape=jax.ShapeDtypeStruct((2 * h, c), jnp.float32),
        grid_spec=pltpu.PrefetchScalarGridSpec(
            num_scalar_prefetch=1, grid=(nb,),
            in_specs=[slot(0), slot(2), slot(1)],
            out_specs=pl.BlockSpec((tr, c), lambda i, p: (i + p[0] * nb, 0))),
        compiler_params=_params(("parallel",)),
    )(pos, pair_sum, by_source, by_source)


def _adamw(w, g, m, v, *, name, jobs=()):
    r, c = w.shape
    tr = _row_tile(r)

    def body(w_ref, g_ref, m_ref, v_ref, d_ref, mo_ref, vo_ref, go_ref):
        gv = g_ref[...]
        go_ref[...] = gv
        mn = ADAM_B1 * m_ref[...] + (1.0 - ADAM_B1) * gv
        vn = ADAM_B2 * v_ref[...] + (1.0 - ADAM_B2) * (gv * gv)
        m_hat = mn / (1.0 - ADAM_B1 ** ADAM_STEP)
        v_hat = vn / (1.0 - ADAM_B2 ** ADAM_STEP)
        d_ref[...] = -ADAM_LR * (m_hat / (jnp.sqrt(v_hat) + ADAM_EPS) + ADAM_WD * w_ref[...])
        mo_ref[...] = mn
        vo_ref[...] = vn

    spec = pl.BlockSpec((tr, c), lambda i: (i, 0))
    out = jax.ShapeDtypeStruct((r, c), jnp.float32)
    return _call(body, name=name, grid=(r // tr,), in_specs=[spec] * 4, out_specs=[spec] * 4, out_shape=[out] * 4,
                 semantics=("parallel",), args=(w, g, m, v), jobs=jobs)


def _all_reduce_small(block):
    r, c = block.shape

    def body(src_ref, out_ref, stage_ref, send_sems, recv_sems):
        x, y, cc = _position()
        me = 4 * x + 2 * y + cc
        stage_ref[me] = src_ref[...]
        flip = lambda v, on: 1 - v if on else v
        peers = [(flip(x, dx), flip(y, dy), flip(cc, dc)) for dx in (0, 1) for dy in (0, 1) for dc in (0, 1)][1:]
        copies = [pltpu.make_async_remote_copy(
            src_ref=stage_ref.at[me], dst_ref=stage_ref.at[me],
            send_sem=send_sems.at[k], recv_sem=recv_sems.at[k], device_id=peer, device_id_type=MESH)
            for k, peer in enumerate(peers)]
        for cp in copies:
            cp.start()
        for k, (px, py, pc) in enumerate(peers):
            them = 4 * px + 2 * py + pc
            pltpu.make_async_remote_copy(
                src_ref=stage_ref.at[them], dst_ref=stage_ref.at[them],
                send_sem=send_sems.at[k], recv_sem=recv_sems.at[k], device_id=(px, py, pc),
                device_id_type=MESH).wait_recv()
        for cp in copies:
            cp.wait_send()
        total = stage_ref[0]
        for d in range(1, 8):
            total = total + stage_ref[d]
        out_ref[...] = total

    return pl.pallas_call(
        body, name="all_reduce_small",
        in_specs=[pl.BlockSpec(memory_space=pltpu.VMEM)], out_specs=pl.BlockSpec(memory_space=pltpu.VMEM),
        out_shape=jax.ShapeDtypeStruct((r, c), jnp.float32),
        scratch_shapes=[pltpu.VMEM((8, r, c), jnp.float32), pltpu.SemaphoreType.DMA((7,)),
                        pltpu.SemaphoreType.DMA((7,))],
        compiler_params=pltpu.CompilerParams(has_side_effects=True),
    )(block)


def _cols_from_shards(g):
    n, r, c = g.shape
    return jnp.transpose(g, (1, 0, 2)).reshape(r, n * c)


def _cols_to_shards(w, n=4):
    r, c = w.shape
    return jnp.transpose(w.reshape(r, n, c // n), (1, 0, 2))


def _pad_w_in(g):
    _, d, c = g.shape
    cut = COL_KR - 3 * c
    zeros = jnp.zeros((d, COL_CKV - COL_KR - QK_ROPE), g.dtype)
    return jnp.concatenate([g[0], g[1], g[2], g[3][:, :cut], g[3][:, cut + KV_RANK:], zeros,
                            g[3][:, cut:cut + KV_RANK]], axis=1)


def _unpad_w_in(padded):
    c = IN_WIDTH // 4
    last = jnp.concatenate([padded[:, 3 * c:COL_KR], padded[:, COL_CKV:COL_CKV + KV_RANK],
                            padded[:, COL_KR:COL_KR + QK_ROPE]], axis=1)
    return jnp.stack([padded[:, 0:c], padded[:, c:2 * c], padded[:, 2 * c:3 * c], last])


def _pad_w_uq(full):
    r = full.shape[0]
    per_head = full.reshape(r, N_HEADS, QK_NOPE + QK_ROPE)
    return jnp.pad(per_head, ((0, 0), (0, 0), (0, HEAD_PAD - QK_NOPE - QK_ROPE))).reshape(r, N_HEADS * HEAD_PAD)


def _unpad_w_uq(padded):
    r = padded.shape[0]
    return padded.reshape(r, N_HEADS, HEAD_PAD)[:, :, :QK_NOPE + QK_ROPE].reshape(r, N_HEADS * (QK_NOPE + QK_ROPE))


SMALL_ROWS = 16


def _pack_small(d, pre_mix, post_mix, pre_mlp, post_mlp, conv_out, attn_out, q_norm, kv_norm, conv_w):
    row = lambda *parts: jnp.pad(jnp.concatenate(parts, axis=1), ((0, 0), (0, d - sum(p.shape[1] for p in parts))))
    rows = [row(pre_mix), row(post_mix), row(pre_mlp), row(post_mlp), row(conv_out, attn_out), row(q_norm, kv_norm),
            row(conv_w[0:1]), row(conv_w[1:2]), row(conv_w[2:3])]
    return jnp.pad(jnp.concatenate(rows, axis=0), ((0, SMALL_ROWS - len(rows)), (0, 0)))


def _unpack_small(p, chip):
    cw = CONV_WIDTH // 4
    conv_w = lax.dynamic_slice(p[6:9, :CONV_WIDTH], (0, chip * cw), (3, cw))
    return dict(pre_mix_g=p[0:1], post_mix_g=p[1:2], pre_mlp_g=p[2:3], post_mlp_g=p[3:4],
                conv_out_g=p[4:5, :CONV_WIDTH], attn_out_g=p[4:5, CONV_WIDTH:2 * CONV_WIDTH],
                q_norm_g=p[5:6, :Q_RANK], kv_norm_g=p[5:6, Q_RANK:Q_RANK + KV_RANK], conv_w=conv_w[None])


def kernel(x, pre_mix_g, w_in, conv_w, q_norm_g, w_uq, kv_norm_g, w_ukv, conv_out_g, attn_out_g, w_o, post_mix_g, pre_mlp_g, w_up, w_down, post_mlp_g, loss_target, m_pre_mix_g, m_w_in, m_conv_w, m_q_norm_g, m_w_uq, m_kv_norm_g, m_w_ukv, m_conv_out_g, m_attn_out_g, m_w_o, m_post_mix_g, m_pre_mlp_g, m_w_up, m_w_down, m_post_mlp_g, v_pre_mix_g, v_w_in, v_conv_w, v_q_norm_g, v_w_uq, v_kv_norm_g, v_w_ukv, v_conv_out_g, v_attn_out_g, v_w_o, v_post_mix_g, v_pre_mlp_g, v_w_up, v_w_down, v_post_mlp_g):
    bf16 = jnp.bfloat16
    s, d = x.shape[1], x.shape[2]
    d_ff = 4 * d
    chip = 2 * lax.axis_index("x") + lax.axis_index("y")
    xs = x.reshape(s, d)
    target = loss_target.reshape(s, d)
    tm = min(256, s)
    t_attn = min(1024, s)
    mt = min(1024, s)
    kt = min(2048, s)

    big = dict(w_in=w_in[0], w_uq=w_uq[0], w_ukv=w_ukv[0], w_o=w_o[0], w_up=w_up[0], w_down=w_down[0])
    names = list(big)
    pos = jnp.stack([lax.axis_index("c"), chip]).astype(jnp.int32)
    wb = {}
    half = {k: big[k].shape[0] // 2 for k in names}

    def rows(k, a, b):
        lo, n = a * half[k] // 64, (b - a) * half[k] // 64
        assert lo % 16 == 0 and n % 16 == 0 and n > 0, (k, a, b)
        return lo, n

    ici = lambda k, a=0, b=64: _GatherIci(wb[k], *rows(k, a, b))
    fwd = lambda k, a=0, b=64: _GatherForward(wb[k], *rows(k, a, b))
    near = lambda k, a=0, b=64: _GatherD2d(wb[k], *rows(k, a, b), slots=(0, 1))
    far = lambda k, a=0, b=64: _GatherD2d(wb[k], *rows(k, a, b), slots=(2,))
    tap_rows = 64
    half["conv_w"] = tap_rows // 2
    wb["conv_w"] = _Buf(lax.dynamic_update_slice(
        jnp.zeros((4, tap_rows, CONV_WIDTH // 4), jnp.float32),
        jnp.pad(conv_w[0], ((0, tap_rows - conv_w.shape[1]), (0, 0)))[None], (chip, 0, 0)))
    wb["w_in"] = _Buf(_cast_into_slot(big["w_in"], pos, name="cast_w_in"))
    rest = [k for k in names if k != "w_in"]
    for k, slot in zip(rest, _cast_many_into_slots([big[k] for k in rest], pos, name="cast_rest",
                                                   jobs=[ici("w_in"), ici("conv_w")])):
        wb[k] = _Buf(slot)
    h1 = _rms_fwd(xs, pre_mix_g, width=d, col=0, tm=tm, name="rms_pre_mix",
                  jobs=[fwd("w_in"), near("w_in"), fwd("conv_w"), near("conv_w")])
    _comm("gather_w_in", [[far("w_in"), far("conv_w")]])
    win = _pad_w_in(wb["w_in"].arr)
    conv_w_full = jnp.transpose(wb["conv_w"].arr[:, :conv_w.shape[1], :], (1, 0, 2)).reshape(-1, CONV_WIDTH)
    spread = lambda a: lax.dynamic_update_slice(jnp.zeros((3, CONV_WIDTH), jnp.float32), a[0],
                                                (0, chip * (CONV_WIDTH // 4)))
    ff4 = d_ff // 4

    proj = _matmul(h1, win, dims=_NN, mnk=(s, IN_PAD, d), tiles=(mt, IN_PAD // 3, d), name="mm_proj",
                   jobs=[ici("w_uq"), ici("w_ukv"), ici("w_o")])
    y_conv = _conv_fwd(proj, conv_w_full, conv_out_g,
                       jobs=[fwd("w_uq"), fwd("w_ukv"), fwd("w_o"), near("w_uq"), near("w_ukv"), near("w_o")])
    cqn, ckvn = _latent_norms_fwd(proj, q_norm_g, kv_norm_g, tm=mt,
                                  jobs=[far("w_uq"), far("w_ukv"), far("w_o"), ici("w_up", 0, 10)])
    wuq = _pad_w_uq(_cols_from_shards(wb["w_uq"].arr))
    wukv = _cols_from_shards(wb["w_ukv"].arr)
    wo = wb["w_o"].arr.reshape(-1, d)
    q_pad = _matmul(cqn, wuq, dims=_NN, mnk=(s, N_HEADS * HEAD_PAD, Q_RANK), tiles=(mt, 1024, Q_RANK), name="mm_q",
                    jobs=[ici("w_up", 10, 18), fwd("w_up", 0, 10), near("w_up", 0, 4)])
    kv = _matmul(ckvn, wukv, dims=_NN, mnk=(s, N_HEADS * HEAD_PAD, KV_RANK), tiles=(mt, 1024, KV_RANK),
                 name="mm_kv", out_dtype=bf16,
                 jobs=[ici("w_up", 18, 24), fwd("w_up", 10, 18), near("w_up", 4, 10), far("w_up", 0, 4)])
    cos, sin = _rope_tables(s)
    q_rot, kr_rot = _qk_rope_fwd(
        q_pad, proj, cos, sin, tm=tm,
        jobs=[ici("w_up", 24, 32), fwd("w_up", 18, 24), near("w_up", 10, 18), far("w_up", 4, 10)])
    o, lse, y_attn = _attn_fwd(
        q_rot, kv, kr_rot, attn_out_g, t=t_attn,
        jobs=[ici("w_up", 32, 64), ici("w_down", 0, 8), fwd("w_up", 24, 32), near("w_up", 18, 24), far("w_up", 10, 18)])
    ycat = jnp.concatenate([y_conv, y_attn], axis=1)
    mix = _matmul(ycat, wo, dims=_NN, mnk=(s, d, 2 * CONV_WIDTH), tiles=(mt, 1024, 2 * CONV_WIDTH), name="mm_out",
                  jobs=[fwd("w_up", 32, 64), near("w_up", 24, 32), far("w_up", 18, 24)])
    x2, h2 = _mix_residual_fwd(
        xs, mix, post_mix_g, pre_mlp_g, tm=tm,
        jobs=[ici("w_down", 8, 20), fwd("w_down", 0, 8), near("w_up", 32, 64), far("w_up", 24, 64)])
    wup = wb["w_up"].arr

    def up_epilogue(acc, extra_refs, out_refs):
        r = jnp.maximum(acc, 0.0)
        out_refs[0][...] = acc.astype(bf16)
        out_refs[1][...] = (r * r).astype(bf16)

    n_ff = ff4 // 1024
    act = jax.ShapeDtypeStruct((s, d_ff), bf16)
    up, act_sq = _matmul(
        h2, wup, dims=_NN, mnk=(s, d_ff, d), tiles=(mt, 1024, d), name="mm_up",
        b_spec=pl.BlockSpec((None, d, 1024), lambda i, j, l: (j // n_ff, l, j % n_ff)),
        out_shape=(act, act), o_spec=(pl.BlockSpec((mt, 1024), lambda i, j, l: (i, j)),) * 2, epilogue=up_epilogue,
        jobs=[ici("w_down", 20, 64), fwd("w_down", 8, 20), near("w_down", 0, 8)])
    _comm("gather_w_down_tail", [[fwd("w_down", 20, 64), near("w_down", 8, 64), far("w_down", 0, 20)],
                                 [far("w_down", 20, 64)]])
    wdown = wb["w_down"].arr.reshape(d_ff, d)
    mlp = _matmul(act_sq, wdown, dims=_NN, mnk=(s, d, d_ff), tiles=(min(512, s), 512, d_ff), name="mm_down")
    dx3, dmlp, dg_post_mlp, loss_part = _loss_head(x2, mlp, target, post_mlp_g, tm=tm)

    def dup_epilogue(acc, extra_refs, out_refs):
        out_refs[0][...] = (acc * (2.0 * jnp.maximum(extra_refs[0][...].astype(jnp.float32), 0.0))).astype(bf16)

    grads, theirs, pair_sums, via, folded, by_source, whole = {}, {}, {}, {}, {}, {}, {}

    def exchange(k, g):
        grads[k] = g
        theirs[k] = _Buf(jax.ShapeDtypeStruct((4, g.shape[1] // 2, g.shape[2]), bf16))
        return _PairExchange(g, theirs[k])

    def pair_sum(k, jobs=()):
        pair_sums[k] = _pair_add(grads[k], theirs[k].arr, pos, name="pair_add_" + k, jobs=jobs)
        via[k] = _Buf(jax.ShapeDtypeStruct(pair_sums[k].shape[1:], bf16))
        by_source[k] = _Buf(jax.ShapeDtypeStruct(pair_sums[k].shape, bf16))

    def diag(k, a=0, b=32):
        lo, n = a * half[k] // 64, (b - a) * half[k] // 64
        assert lo % 16 == 0 and n % 16 == 0 and n > 0, (k, a, b)
        return _ScatterDiag(pair_sums[k], via[k], lo, n)

    def fold(k):
        folded[k] = _fold_diag(pair_sums[k], via[k].arr, pos, name="fold_" + k)

    scatter = lambda k, a=0, b=64: _ScatterNear(folded[k], by_source[k], *rows(k, a, b))

    def share(k):
        whole[k] = _Buf(_chip_sum(by_source[k].arr, pair_sums[k], pos, name="chip_sum_" + k))
        return _PairShare(whole[k])

    g_wdown = _matmul(act_sq, dmlp, dims=_TN, mnk=(d_ff, d, s), tiles=(1024, 1024, kt), name="mm_gw_down",
                      out_dtype=bf16).reshape(4, ff4, d)
    dup = _matmul(dmlp, wdown, dims=_NT, mnk=(s, d_ff, d), tiles=(mt, 1024, d), name="mm_dact",
                  out_dtype=bf16, epilogue=dup_epilogue, extra=(up,),
                  extra_specs=(pl.BlockSpec((mt, 1024), lambda i, j, l: (i, j)),),
                  jobs=[exchange("w_down", g_wdown)])
    pair_sum("w_down")
    g_wup = _matmul(h2, dup, dims=_TN, mnk=(d, d_ff, s), tiles=(1024, 1024, kt), name="mm_gw_up",
                    out_shape=jax.ShapeDtypeStruct((4, d, ff4), bf16),
                    o_spec=pl.BlockSpec((None, 1024, 1024), lambda i, j, l: (j // n_ff, i, j % n_ff)),
                    jobs=[diag("w_down")])
    fold("w_down")
    dh2 = _matmul(dup, wup, dims=_NT, mnk=(s, d, d_ff), tiles=(mt, 1024, ff4), name="mm_dh2",
                  b_spec=pl.BlockSpec((None, 1024, ff4), lambda i, j, l: (l, j, 0)),
                  jobs=[exchange("w_up", g_wup), scatter("w_down", 0, 48)])
    pair_sum("w_up", jobs=[scatter("w_down", 48, 64)])
    dx2, dmix, dg_pre_mlp, dg_post_mix = _mix_residual_bwd(
        dx3, dh2, x2, mix, pre_mlp_g, post_mix_g, tm=tm, jobs=[diag("w_up", 0, 20)])

    dycat = _matmul(dmix, wo, dims=_NT, mnk=(s, 2 * CONV_WIDTH, d), tiles=(mt, 1024, d), name="mm_dycat",
                    jobs=[diag("w_up", 20, 32)])
    fold("w_up")
    g_wo = _matmul(ycat, dmix, dims=_TN, mnk=(2 * CONV_WIDTH, d, s), tiles=(1024, 1024, kt),
                   name="mm_gw_out", out_dtype=bf16, jobs=[scatter("w_up", 0, 12)]).reshape(4, CONV_WIDTH // 2, d)
    du, dgb, dgc, dg_conv_w, dg_conv_out = _conv_bwd(proj, conv_w_full, conv_out_g, dycat,
                                                     jobs=[exchange("w_o", g_wo), scatter("w_up", 12, 24)])
    pair_sum("w_o")
    dq_pad, dk_pad, dv, dg_attn_out = _attn_bwd(
        q_rot, kv, kr_rot, o, lse, attn_out_g, dycat, t=min(1024, s),
        jobs=[scatter("w_up", 24, 64), diag("w_o"), share("w_down")])
    fold("w_o")
    dq_raw, dkv, dkr = _qk_rope_bwd(dq_pad, dk_pad, dv, cos, sin, tm=tm, jobs=[scatter("w_o")])
    wq_cols = N_HEADS * HEAD_PAD
    g_wuq = _matmul(cqn, dq_raw, dims=_TN, mnk=(Q_RANK, wq_cols, s), tiles=(Q_RANK, 1024, kt),
                    name="mm_gw_uq", out_dtype=bf16)
    dcqn = _matmul(dq_raw, wuq, dims=_NT, mnk=(s, Q_RANK, wq_cols), tiles=(mt, Q_RANK, wq_cols), name="mm_dcq")
    g_wukv = _matmul(ckvn, dkv, dims=_TN, mnk=(KV_RANK, wq_cols, s), tiles=(KV_RANK, 1024, kt),
                     name="mm_gw_ukv", out_dtype=bf16)
    dckvn = _matmul(dkv, wukv, dims=_NT, mnk=(s, KV_RANK, wq_cols), tiles=(mt, KV_RANK, wq_cols), name="mm_dckv",
                    jobs=[exchange("w_uq", _cols_to_shards(_unpad_w_uq(g_wuq))),
                          exchange("w_ukv", _cols_to_shards(g_wukv))])
    pair_sum("w_uq")
    pair_sum("w_ukv")
    dcq, dckv, dg_q_norm, dg_kv_norm = _latent_norms_bwd(proj, q_norm_g, kv_norm_g, dcqn, dckvn, tm=mt,
                                                         jobs=[diag("w_uq"), diag("w_ukv")])
    fold("w_uq")
    fold("w_ukv")
    dproj = jnp.concatenate([du, dgb, dgc, dcq, dkr, jnp.zeros((s, COL_CKV - COL_KR - 128), bf16), dckv], axis=1)
    g_win = _matmul(h1, dproj, dims=_TN, mnk=(d, IN_PAD, s), tiles=(1024, IN_PAD // 3, kt), name="mm_gw_in",
                    out_dtype=bf16, jobs=[scatter("w_uq"), scatter("w_ukv"), share("w_up"), share("w_o")])
    _comm("pair_exchange_w_in", [[exchange("w_in", _unpad_w_in(g_win))]])
    pair_sum("w_in")
    _comm("scatter_diag_w_in", [[diag("w_in")]])
    fold("w_in")
    dh1 = _matmul(dproj, win, dims=_NT, mnk=(s, d, IN_PAD), tiles=(mt, 1024, IN_PAD // 2), name="mm_dh1",
                  jobs=[scatter("w_in"), share("w_uq"), share("w_ukv")])
    grad_x, dg_pre_mix = _input_bwd(dx2, dh1, xs, pre_mix_g, tm=tm)
    _comm("pair_share_w_in", [[share("w_in")]])

    moments = dict(w_in=(m_w_in, v_w_in), w_uq=(m_w_uq, v_w_uq), w_ukv=(m_w_ukv, v_w_ukv), w_o=(m_w_o, v_w_o),
                   w_up=(m_w_up, v_w_up), w_down=(m_w_down, v_w_down))
    gw, dw, nm, nv = {}, {}, {}, {}
    for k in names:
        delta_k, nm_k, nv_k, g = _adamw(big[k], whole[k].arr, moments[k][0][0], moments[k][1][0], name="adamw_" + k)
        gw[k], dw[k], nm[k], nv[k] = g[None], delta_k[None], nm_k[None], nv_k[None]

    small_g = _all_reduce_small(_pack_small(d, dg_pre_mix, dg_post_mix, dg_pre_mlp, dg_post_mlp, dg_conv_out,
                                            dg_attn_out, dg_q_norm, dg_kv_norm, dg_conv_w
                                            ).at[SMALL_ROWS - 1, :128].set(loss_part[0]))
    loss = small_g[SMALL_ROWS - 1, 0]
    pack_w = lambda cw, pre_mix, post_mix, pre_mlp, post_mlp, conv_out, attn_out, q_norm, kv_norm: _pack_small(
        d, pre_mix, post_mix, pre_mlp, post_mlp, conv_out, attn_out, q_norm, kv_norm, cw)
    small_w = pack_w(conv_w_full, pre_mix_g, post_mix_g, pre_mlp_g, post_mlp_g, conv_out_g, attn_out_g, q_norm_g,
                     kv_norm_g)
    small_m = pack_w(spread(m_conv_w), m_pre_mix_g, m_post_mix_g, m_pre_mlp_g, m_post_mlp_g, m_conv_out_g,
                     m_attn_out_g, m_q_norm_g, m_kv_norm_g)
    small_v = pack_w(spread(v_conv_w), v_pre_mix_g, v_post_mix_g, v_pre_mlp_g, v_post_mlp_g, v_conv_out_g,
                     v_attn_out_g, v_q_norm_g, v_kv_norm_g)
    small_d, small_nm, small_nv, small_g = _adamw(small_w, small_g, small_m, small_v, name="adamw_small")
    sg, sd, snm, snv = (_unpack_small(p, chip) for p in (small_g, small_d, small_nm, small_nv))

    for src, dst in ((sg, gw), (sd, dw), (snm, nm), (snv, nv)):
        dst.update(src)

    order = ["pre_mix_g", "w_in", "conv_w", "q_norm_g", "w_uq", "kv_norm_g", "w_ukv", "conv_out_g", "attn_out_g",
             "w_o", "post_mix_g", "pre_mlp_g", "w_up", "w_down", "post_mlp_g"]
    return (loss, grad_x.reshape(1, s, d), *[gw[k] for k in order], *[dw[k] for k in order],
            *[nm[k] for k in order], *[nv[k] for k in order])
```

```python
import functools

import jax
import jax.numpy as jnp
from jax import lax
from jax.experimental import pallas as pl
from jax.experimental.pallas import tpu as pltpu

EPS = 1e-6
NEG_INF = -1e30
CHUNK_SHIFT = 6
N_HEADS = 8
HEAD_PAD = 256
QK_NOPE = 128
QK_ROPE = 64
V_DIM = 128
CONV_WIDTH = 1024
Q_RANK = 768
KV_RANK = 512
ROPE_THETA = 10000.0
ATTN_SCALE = (QK_NOPE + QK_ROPE) ** -0.5
ADAM_LR, ADAM_B1, ADAM_B2, ADAM_EPS, ADAM_WD, ADAM_STEP = 0.001, 0.9, 0.999, 1e-08, 0.01, 10

COL_CQ = 3 * CONV_WIDTH
COL_KR = COL_CQ + Q_RANK
COL_CKV = 4096
IN_PAD = COL_CKV + KV_RANK
IN_WIDTH = 3 * CONV_WIDTH + Q_RANK + KV_RANK + QK_ROPE

VMEM_LIMIT_BYTES = 56 * 1024 * 1024
MESH = pl.DeviceIdType.MESH
ANY = pl.BlockSpec(memory_space=pl.ANY)

_NN = (((1,), (0,)), ((), ()))
_NT = (((1,), (1,)), ((), ()))
_TN = (((0,), (0,)), ((), ()))


def _params(sem):
    return pltpu.CompilerParams(dimension_semantics=sem, vmem_limit_bytes=VMEM_LIMIT_BYTES)


class _Buf:
    def __init__(self, arr):
        self.arr = arr


def _position():
    return lax.axis_index("x"), lax.axis_index("y"), lax.axis_index("c")


def _other_chips(x, y):
    return [(2 * (1 - x) + y, (1 - x, y)), (2 * x + (1 - y), (x, 1 - y)), (2 * (1 - x) + (1 - y), (1 - x, 1 - y))]


def _remote(src, dst, sems, k, to):
    send, recv, off = sems
    return pltpu.make_async_remote_copy(src_ref=src, dst_ref=dst, send_sem=send.at[off + k], recv_sem=recv.at[off + k],
                                        device_id=to, device_id_type=MESH)


class _GatherIci:
    n_sems = 2
    link = "ici"

    def __init__(self, buf, lo, n):
        self.buf, self.lo, self.n, self.buffers, self.sources = buf, lo, n, [buf], []

    def _rows(self, ref, slot, which, lo=None, n=None):
        lo, n = (self.lo, self.n) if lo is None else (lo, n)
        return ref.at[slot, pl.ds(which * (ref.shape[1] // 2) + lo, n), :]

    def start(self, refs, sems):
        ref = refs[id(self.buf)]
        x, y, c = _position()
        mine = self._rows(ref, 2 * x + y, c)
        for k, (_, xy) in enumerate(_other_chips(x, y)[:2]):
            _remote(mine, mine, sems, k, (*xy, c)).start()

    def wait(self, refs, sems):
        ref = refs[id(self.buf)]
        x, y, c = _position()
        mine = self._rows(ref, 2 * x + y, c)
        for k, (slot, xy) in enumerate(_other_chips(x, y)[:2]):
            landed = self._rows(ref, slot, c)
            _remote(landed, landed, sems, k, (*xy, c)).wait_recv()
            _remote(mine, mine, sems, k, (*xy, c)).wait_send()


class _GatherForward(_GatherIci):
    def _ways(self, x, y):
        (slot_x, xy_x), (slot_y, xy_y), (slot_d, _) = _other_chips(x, y)
        h = self.n // 2
        assert h % 16 == 0, self.n
        return [(slot_x, slot_d, self.lo, xy_y), (slot_y, slot_d, self.lo + h, xy_x)], h

    def start(self, refs, sems):
        ref = refs[id(self.buf)]
        x, y, c = _position()
        ways, h = self._ways(x, y)
        for k, (slot, _, lo, xy) in enumerate(ways):
            rows = self._rows(ref, slot, c, lo, h)
            _remote(rows, rows, sems, k, (*xy, c)).start()

    def wait(self, refs, sems):
        ref = refs[id(self.buf)]
        x, y, c = _position()
        ways, h = self._ways(x, y)
        for k, (slot, lands, lo, xy) in enumerate(ways):
            landed, sent = self._rows(ref, lands, c, lo, h), self._rows(ref, slot, c, lo, h)
            _remote(landed, landed, sems, k, (*xy, c)).wait_recv()
            _remote(sent, sent, sems, k, (*xy, c)).wait_send()


class _GatherD2d(_GatherIci):
    link = "d2d"

    def __init__(self, buf, lo, n, slots):
        super().__init__(buf, lo, n)
        self.slots, self.n_sems = slots, len(slots)

    def start(self, refs, sems):
        ref = refs[id(self.buf)]
        x, y, c = _position()
        chips = _other_chips(x, y)
        for k, which in enumerate(self.slots):
            rows = self._rows(ref, chips[which][0], c)
            _remote(rows, rows, sems, k, (x, y, 1 - c)).start()

    def wait(self, refs, sems):
        ref = refs[id(self.buf)]
        x, y, c = _position()
        chips = _other_chips(x, y)
        for k, which in enumerate(self.slots):
            sent, landed = self._rows(ref, chips[which][0], c), self._rows(ref, chips[which][0], 1 - c)
            _remote(landed, landed, sems, k, (x, y, 1 - c)).wait_recv()
            _remote(sent, sent, sems, k, (x, y, 1 - c)).wait_send()


class _ScatterDiag:
    n_sems = 2
    link = "ici"

    def __init__(self, src, via, lo, n):
        self.src, self.via, self.lo, self.n, self.buffers, self.sources = src, via, lo, n, [via], [src]

    def _copies(self, refs, sems):
        src, via = refs[id(self.src)], refs[id(self.via)]
        x, y, c = _position()
        (_, xy_x), (_, xy_y), (slot_d, _) = _other_chips(x, y)
        h2 = via.shape[0] // 2
        return [_remote(src.at[slot_d, pl.ds(first + self.lo, self.n), :], via.at[pl.ds(first + self.lo, self.n), :],
                        sems, k, (*xy, c)) for k, (first, xy) in enumerate(((0, xy_x), (h2, xy_y)))]

    def start(self, refs, sems):
        for cp in self._copies(refs, sems):
            cp.start()

    def wait(self, refs, sems):
        for cp in self._copies(refs, sems):
            cp.wait_recv()
            cp.wait_send()


class _ScatterNear:
    n_sems = 2
    link = "ici"

    def __init__(self, src, dst, lo, n):
        self.src, self.dst, self.lo, self.n, self.buffers, self.sources = src, dst, lo, n, [dst], [src]

    def _copies(self, refs, sems, landing):
        src, dst = refs[id(self.src)], refs[id(self.dst)]
        x, y, c = _position()
        rows = pl.ds(self.lo, self.n)
        return [_remote(src.at[k, rows, :], dst.at[slot if landing else 2 * x + y, rows, :], sems, k, (*xy, c))
                for k, (slot, xy) in enumerate(_other_chips(x, y)[:2])]

    def start(self, refs, sems):
        for cp in self._copies(refs, sems, False):
            cp.start()

    def wait(self, refs, sems):
        for cp in self._copies(refs, sems, True):
            cp.wait_recv()
            cp.wait_send()


class _PairExchange:
    n_sems = 1
    link = "d2d"

    def __init__(self, src, dst):
        self.src, self.dst, self.buffers, self.sources = src, dst, [dst], [src]

    def _copy(self, refs, sems):
        src, dst = refs[id(self.src)], refs[id(self.dst)]
        x, y, c = _position()
        h = src.shape[1] // 2
        return _remote(src.at[:, pl.ds((1 - c) * h, h), :], dst, sems, 0, (x, y, 1 - c))

    def start(self, refs, sems):
        self._copy(refs, sems).start()

    def wait(self, refs, sems):
        self._copy(refs, sems).wait()


class _PairShare:
    n_sems = 1
    link = "d2d"

    def __init__(self, buf):
        self.buf, self.buffers, self.sources = buf, [buf], []

    def _rows(self, ref, which):
        h = ref.shape[0] // 2
        return ref.at[pl.ds(which * h, h), :]

    def start(self, refs, sems):
        ref = refs[id(self.buf)]
        x, y, c = _position()
        _remote(self._rows(ref, c), self._rows(ref, c), sems, 0, (x, y, 1 - c)).start()

    def wait(self, refs, sems):
        ref = refs[id(self.buf)]
        x, y, c = _position()
        _remote(self._rows(ref, c), self._rows(ref, c), sems, 0, (x, y, 1 - c)).wait_send()
        _remote(self._rows(ref, 1 - c), self._rows(ref, 1 - c), sems, 0, (x, y, 1 - c)).wait_recv()


_COLLECTIVE_IDS = {("ici",): 1, ("d2d",): 2, ("d2d", "ici"): 3}


def _links(jobs):
    return tuple(sorted({j.link for j in jobs}))


def _handshake(links):
    x, y, c = _position()
    peers = ([(1 - x, y, c), (x, 1 - y, c)] if "ici" in links else []) + ([(x, y, 1 - c)] if "d2d" in links else [])
    barrier = pltpu.get_barrier_semaphore()
    for peer in peers:
        pl.semaphore_signal(barrier, inc=1, device_id=peer, device_id_type=MESH)
    pl.semaphore_wait(barrier, len(peers))


def _unique(items):
    seen, out = set(), []
    for it in items:
        if id(it) not in seen:
            seen.add(id(it))
            out.append(it)
    return out


def _job_operands(jobs):
    sources = _unique([a for j in jobs for a in j.sources])
    buffers = _unique([b for j in jobs for b in j.buffers])
    held = [b for b in buffers if not isinstance(b.arr, jax.ShapeDtypeStruct)]
    fresh = [b for b in buffers if isinstance(b.arr, jax.ShapeDtypeStruct)]
    return sources, held, fresh


def _sem_offsets(jobs):
    offs, total = [], 0
    for j in jobs:
        offs.append(total)
        total += j.n_sems
    return offs, total


def _call(body, *, name, grid, in_specs, out_specs, out_shape, args, semantics, scratch_shapes=(), jobs=(),
          prefetch=None):
    n_pre = 0 if prefetch is None else 1

    def launch(fn, in_specs, out_specs, scratch, **kw):
        if prefetch is None:
            return pl.pallas_call(fn, name=name, grid=grid, in_specs=in_specs, out_specs=out_specs,
                                  scratch_shapes=scratch, **kw)
        return pl.pallas_call(fn, name=name, grid_spec=pltpu.PrefetchScalarGridSpec(
            num_scalar_prefetch=1, grid=grid, in_specs=in_specs, out_specs=out_specs, scratch_shapes=scratch), **kw)

    pre = () if prefetch is None else (prefetch,)
    if not jobs:
        return launch(body, list(in_specs), list(out_specs), list(scratch_shapes), out_shape=list(out_shape),
                      compiler_params=_params(semantics))(*pre, *args)
    sources, held, fresh = _job_operands(jobs)
    offs, n_sem = _sem_offsets(jobs)
    links = _links(jobs)
    n_in, n_out, n_scr = len(in_specs), len(out_specs), len(scratch_shapes)
    n_src, n_held, n_fresh = len(sources), len(held), len(fresh)

    def carried(*refs):
        pre_refs, refs = refs[:n_pre], refs[n_pre:]
        ins = refs[:n_in]
        src_refs = refs[n_in:n_in + n_src]
        o0 = n_in + n_src + n_held
        outs = refs[o0:o0 + n_out]
        buf_refs = refs[o0 + n_out:o0 + n_out + n_held + n_fresh]
        s0 = o0 + n_out + n_held + n_fresh
        scratch = refs[s0:s0 + n_scr]
        send, recv = refs[s0 + n_scr], refs[s0 + n_scr + 1]
        where = {id(a): r for a, r in zip(sources, src_refs)}
        where.update({id(b): r for b, r in zip(held + fresh, buf_refs)})
        ids = [pl.program_id(a) for a in range(len(grid))]
        first = functools.reduce(jnp.logical_and, [i == 0 for i in ids])
        last = functools.reduce(jnp.logical_and, [i == g - 1 for i, g in zip(ids, grid)])

        @pl.when(first)
        def _():
            _handshake(links)
            for j, off in zip(jobs, offs):
                j.start(where, (send, recv, off))

        body(*pre_refs, *ins, *outs, *scratch)

        @pl.when(last)
        def _():
            for j, off in zip(jobs, offs):
                j.wait(where, (send, recv, off))

    shape_of = lambda b: jax.ShapeDtypeStruct(b.arr.shape, b.arr.dtype)
    res = launch(
        carried, [*in_specs, *[ANY] * (n_src + n_held)], [*out_specs, *[ANY] * (n_held + n_fresh)],
        [*scratch_shapes, pltpu.SemaphoreType.DMA((n_sem,)), pltpu.SemaphoreType.DMA((n_sem,))],
        out_shape=[*out_shape, *[shape_of(b) for b in held + fresh]],
        input_output_aliases={n_pre + n_in + n_src + i: n_out + i for i in range(n_held)},
        compiler_params=pltpu.CompilerParams(dimension_semantics=("arbitrary",) * len(grid),
                                             vmem_limit_bytes=VMEM_LIMIT_BYTES, has_side_effects=True,
                                             collective_id=_COLLECTIVE_IDS[links]),
    )(*pre, *args, *sources, *[b.arr for b in held])
    for b, new in zip(held + fresh, res[n_out:]):
        b.arr = new
    return list(res[:n_out])


def _comm(name, phases):
    jobs = [j for ph in phases for j in ph]
    sources, held, fresh = _job_operands(jobs)
    offs, n_sem = _sem_offsets(jobs)
    off_of = {id(j): o for j, o in zip(jobs, offs)}
    links = _links(jobs)
    n_src, n_held, n_fresh = len(sources), len(held), len(fresh)

    def body(*refs):
        src_refs = refs[:n_src]
        buf_refs = refs[n_src + n_held:n_src + 2 * n_held + n_fresh]
        send, recv = refs[-2], refs[-1]
        where = {id(a): r for a, r in zip(sources, src_refs)}
        where.update({id(b): r for b, r in zip(held + fresh, buf_refs)})
        _handshake(links)
        for ph in phases:
            for j in ph:
                j.start(where, (send, recv, off_of[id(j)]))
            for j in ph:
                j.wait(where, (send, recv, off_of[id(j)]))

    shape_of = lambda b: jax.ShapeDtypeStruct(b.arr.shape, b.arr.dtype)
    res = pl.pallas_call(
        body, name=name,
        in_specs=[ANY] * (n_src + n_held), out_specs=[ANY] * (n_held + n_fresh),
        out_shape=[shape_of(b) for b in held + fresh],
        input_output_aliases={n_src + i: i for i in range(n_held)},
        scratch_shapes=[pltpu.SemaphoreType.DMA((n_sem,)), pltpu.SemaphoreType.DMA((n_sem,))],
        compiler_params=pltpu.CompilerParams(has_side_effects=True, collective_id=_COLLECTIVE_IDS[links]),
    )(*sources, *[b.arr for b in held])
    for b, new in zip(held + fresh, res):
        b.arr = new


def _matmul(a, b, *, dims, mnk, tiles, name, out_dtype=jnp.float32, a_spec=None, b_spec=None,
            out_shape=None, o_spec=None, epilogue=None, extra=(), extra_specs=(), jobs=()):
    m, n, k = mnk
    tm, tn, tk = tiles
    assert m % tm == 0 and n % tn == 0 and k % tk == 0, (name, mnk, tiles)
    gm, gn, gk = m // tm, n // tn, k // tk
    if a_spec is None:
        a_spec = (pl.BlockSpec((tk, tm), lambda i, j, l: (l, i)) if dims is _TN
                  else pl.BlockSpec((tm, tk), lambda i, j, l: (i, l)))
    if b_spec is None:
        b_spec = (pl.BlockSpec((tn, tk), lambda i, j, l: (j, l)) if dims is _NT
                  else pl.BlockSpec((tk, tn), lambda i, j, l: (l, j)))
    if out_shape is None:
        out_shape = jax.ShapeDtypeStruct((m, n), out_dtype)
    if o_spec is None:
        o_spec = pl.BlockSpec((tm, tn), lambda i, j, l: (i, j))
    single = not isinstance(out_shape, (tuple, list))
    n_extra = len(extra)

    def finish(acc, extra_refs, out_refs):
        if epilogue is None:
            out_refs[0][...] = acc.astype(out_refs[0].dtype)
        else:
            epilogue(acc, extra_refs, out_refs)

    def body_whole_k(*refs):
        a_ref, b_ref = refs[0], refs[1]
        acc = lax.dot_general(a_ref[...], b_ref[...], dims, preferred_element_type=jnp.float32)
        finish(acc, refs[2:2 + n_extra], refs[2 + n_extra:])

    def body_split_k(*refs):
        a_ref, b_ref = refs[0], refs[1]
        extra_refs = refs[2:2 + n_extra]
        out_refs = refs[2 + n_extra:-1]
        acc_ref = refs[-1]
        step = pl.program_id(2)
        part = lax.dot_general(a_ref[...], b_ref[...], dims, preferred_element_type=jnp.float32)

        @pl.when(step == 0)
        def _():
            acc_ref[...] = part

        @pl.when(jnp.logical_and(step > 0, step < gk - 1))
        def _():
            acc_ref[...] += part

        @pl.when(step == gk - 1)
        def _():
            finish(acc_ref[...] + part, extra_refs, out_refs)

    res = _call(
        body_whole_k if gk == 1 else body_split_k, name=name, grid=(gm, gn, gk),
        in_specs=[a_spec, b_spec, *extra_specs],
        out_specs=[o_spec] if single else list(o_spec),
        out_shape=[out_shape] if single else list(out_shape),
        scratch_shapes=[] if gk == 1 else [pltpu.VMEM((tm, tn), jnp.float32)],
        semantics=("parallel", "parallel", "arbitrary"), args=(a, b, *extra), jobs=jobs)
    return res[0] if single else res


def _rstd(x):
    return lax.rsqrt(jnp.mean(x * x, axis=-1, keepdims=True) + EPS)


def _rms_bwd_rows(x, g, dy):
    r = _rstd(x)
    xn = x * r
    dyg = dy * g
    dx = r * (dyg - xn * jnp.mean(xn * dyg, axis=-1, keepdims=True))
    return dx, dy * xn


def _acc_rows(ref, rows, first):
    part = jnp.sum(rows, axis=0, keepdims=True)

    @pl.when(first)
    def _():
        ref[...] = part

    @pl.when(jnp.logical_not(first))
    def _():
        ref[...] += part


def _rms_fwd(x, g, *, width, col, tm, name, jobs=()):
    s = x.shape[0]

    def body(x_ref, g_ref, o_ref):
        v = x_ref[...]
        o_ref[...] = (v * _rstd(v) * g_ref[...]).astype(o_ref.dtype)

    return _call(
        body, name=name, grid=(s // tm,),
        in_specs=[pl.BlockSpec((tm, width), lambda i: (i, col)), pl.BlockSpec((1, width), lambda i: (0, 0))],
        out_specs=[pl.BlockSpec((tm, width), lambda i: (i, 0))],
        out_shape=[jax.ShapeDtypeStruct((s, width), jnp.bfloat16)],
        semantics=("parallel",), args=(x, g), jobs=jobs)[0]


def _latent_norms_fwd(proj, g_q, g_kv, *, tm, jobs=()):
    s = proj.shape[0]

    def body(q_ref, kv_ref, gq_ref, gkv_ref, qo_ref, kvo_ref):
        for x_ref, g_ref, o_ref in ((q_ref, gq_ref, qo_ref), (kv_ref, gkv_ref, kvo_ref)):
            v = x_ref[...]
            o_ref[...] = (v * _rstd(v) * g_ref[...]).astype(o_ref.dtype)

    rows = lambda w, c: pl.BlockSpec((tm, w), lambda i: (i, c))
    gain = lambda w: pl.BlockSpec((1, w), lambda i: (0, 0))
    return _call(
        body, name="latent_norms_fwd", grid=(s // tm,),
        in_specs=[rows(Q_RANK, COL_CQ // Q_RANK), rows(KV_RANK, COL_CKV // KV_RANK), gain(Q_RANK), gain(KV_RANK)],
        out_specs=[rows(Q_RANK, 0), rows(KV_RANK, 0)],
        out_shape=[jax.ShapeDtypeStruct((s, Q_RANK), jnp.bfloat16), jax.ShapeDtypeStruct((s, KV_RANK), jnp.bfloat16)],
        semantics=("parallel",), args=(proj, proj, g_q, g_kv), jobs=jobs)


def _latent_norms_bwd(proj, g_q, g_kv, dq, dkv, *, tm, jobs=()):
    s = proj.shape[0]

    def body(q_ref, kv_ref, gq_ref, gkv_ref, dq_ref, dkv_ref, dqo_ref, dkvo_ref, dgq_ref, dgkv_ref):
        first = pl.program_id(0) == 0
        for x_ref, g_ref, dy_ref, dx_ref, dg_ref in ((q_ref, gq_ref, dq_ref, dqo_ref, dgq_ref),
                                                     (kv_ref, gkv_ref, dkv_ref, dkvo_ref, dgkv_ref)):
            dx, dgr = _rms_bwd_rows(x_ref[...], g_ref[...], dy_ref[...])
            dx_ref[...] = dx.astype(dx_ref.dtype)
            _acc_rows(dg_ref, dgr, first)

    rows = lambda w, c: pl.BlockSpec((tm, w), lambda i: (i, c))
    gain = lambda w: pl.BlockSpec((1, w), lambda i: (0, 0))
    return _call(
        body, name="latent_norms_bwd", grid=(s // tm,),
        in_specs=[rows(Q_RANK, COL_CQ // Q_RANK), rows(KV_RANK, COL_CKV // KV_RANK), gain(Q_RANK), gain(KV_RANK),
                  rows(Q_RANK, 0), rows(KV_RANK, 0)],
        out_specs=[rows(Q_RANK, 0), rows(KV_RANK, 0), gain(Q_RANK), gain(KV_RANK)],
        out_shape=[jax.ShapeDtypeStruct((s, Q_RANK), jnp.bfloat16), jax.ShapeDtypeStruct((s, KV_RANK), jnp.bfloat16),
                   jax.ShapeDtypeStruct((1, Q_RANK), jnp.float32), jax.ShapeDtypeStruct((1, KV_RANK), jnp.float32)],
        semantics=("arbitrary",), args=(proj, proj, g_q, g_kv, dq, dkv), jobs=jobs)


def _row_specs(tm, d, n):
    return [pl.BlockSpec((tm, d), lambda i: (i, 0)) for _ in range(n)]


def _gain_specs(d, n):
    return [pl.BlockSpec((1, d), lambda i: (0, 0)) for _ in range(n)]


def _mix_residual_fwd(x, mix, g_post_mix, g_pre_mlp, *, tm, jobs=()):
    s, d = x.shape

    def body(x_ref, mix_ref, g1_ref, g2_ref, x2_ref, h2_ref):
        mixv = mix_ref[...]
        x2 = x_ref[...] + mixv * _rstd(mixv) * g1_ref[...]
        x2_ref[...] = x2
        h2_ref[...] = (x2 * _rstd(x2) * g2_ref[...]).astype(h2_ref.dtype)

    return _call(
        body, name="mix_residual_fwd", grid=(s // tm,),
        in_specs=_row_specs(tm, d, 2) + _gain_specs(d, 2),
        out_specs=_row_specs(tm, d, 2),
        out_shape=[jax.ShapeDtypeStruct((s, d), jnp.float32), jax.ShapeDtypeStruct((s, d), jnp.bfloat16)],
        semantics=("parallel",), args=(x, mix, g_post_mix, g_pre_mlp), jobs=jobs)


def _loss_head(x2, mlp, target, g_post_mlp, *, tm, jobs=()):
    s, d = x2.shape

    def body(x2_ref, m_ref, t_ref, g_ref, dx3_ref, dm_ref, dg_ref, loss_ref):
        first = pl.program_id(0) == 0
        mv = m_ref[...]
        g = g_ref[...]
        diff = x2_ref[...] + mv * _rstd(mv) * g - t_ref[...]
        dx3 = diff * (1.0 / d)
        dx3_ref[...] = dx3
        dm, dgr = _rms_bwd_rows(mv, g, dx3)
        dm_ref[...] = dm.astype(dm_ref.dtype)
        _acc_rows(dg_ref, dgr, first)
        part = 0.5 * jnp.sum(jnp.mean(diff * diff, axis=-1, keepdims=True), axis=0, keepdims=True)
        _acc_rows(loss_ref, jnp.broadcast_to(part, (1, 128)), first)

    return _call(
        body, name="loss_head", grid=(s // tm,),
        in_specs=_row_specs(tm, d, 3) + _gain_specs(d, 1),
        out_specs=_row_specs(tm, d, 2) + _gain_specs(d, 1) + [pl.BlockSpec((1, 128), lambda i: (0, 0))],
        out_shape=[jax.ShapeDtypeStruct((s, d), jnp.float32), jax.ShapeDtypeStruct((s, d), jnp.bfloat16),
                   jax.ShapeDtypeStruct((1, d), jnp.float32), jax.ShapeDtypeStruct((1, 128), jnp.float32)],
        semantics=("arbitrary",), args=(x2, mlp, target, g_post_mlp), jobs=jobs)


def _mix_residual_bwd(dx3, dh2, x2, mix, g_pre_mlp, g_post_mix, *, tm, jobs=()):
    s, d = x2.shape

    def body(dx3_ref, dh2_ref, x2_ref, mix_ref, g2_ref, g1_ref, dx2_ref, dmix_ref, dg2_ref, dg1_ref):
        first = pl.program_id(0) == 0
        d_in, dgr2 = _rms_bwd_rows(x2_ref[...], g2_ref[...], dh2_ref[...])
        dx2 = dx3_ref[...] + d_in
        dx2_ref[...] = dx2
        dmix, dgr1 = _rms_bwd_rows(mix_ref[...], g1_ref[...], dx2)
        dmix_ref[...] = dmix.astype(dmix_ref.dtype)
        _acc_rows(dg2_ref, dgr2, first)
        _acc_rows(dg1_ref, dgr1, first)

    return _call(
        body, name="mix_residual_bwd", grid=(s // tm,),
        in_specs=_row_specs(tm, d, 4) + _gain_specs(d, 2),
        out_specs=_row_specs(tm, d, 2) + _gain_specs(d, 2),
        out_shape=[jax.ShapeDtypeStruct((s, d), jnp.float32), jax.ShapeDtypeStruct((s, d), jnp.bfloat16),
                   jax.ShapeDtypeStruct((1, d), jnp.float32), jax.ShapeDtypeStruct((1, d), jnp.float32)],
        semantics=("arbitrary",), args=(dx3, dh2, x2, mix, g_pre_mlp, g_post_mix), jobs=jobs)


def _input_bwd(dx2, dh1, x, g_pre_mix, *, tm, jobs=()):
    s, d = x.shape

    def body(dx2_ref, dh1_ref, x_ref, g_ref, dx_ref, dg_ref):
        d_in, dgr = _rms_bwd_rows(x_ref[...], g_ref[...], dh1_ref[...])
        dx_ref[...] = dx2_ref[...] + d_in
        _acc_rows(dg_ref, dgr, pl.program_id(0) == 0)

    return _call(
        body, name="input_bwd", grid=(s // tm,),
        in_specs=_row_specs(tm, d, 3) + _gain_specs(d, 1),
        out_specs=_row_specs(tm, d, 1) + _gain_specs(d, 1),
        out_shape=[jax.ShapeDtypeStruct((s, d), jnp.float32), jax.ShapeDtypeStruct((1, d), jnp.float32)],
        semantics=("arbitrary",), args=(dx2, dh1, x, g_pre_mix), jobs=jobs)


def _shift_rows(z, by):
    s = z.shape[0]
    rows = lax.broadcasted_iota(jnp.int32, z.shape, 0)
    rolled = pltpu.roll(z, by % s, axis=0)
    keep = rows >= by if by > 0 else rows < s + by
    return jnp.where(keep, rolled, 0.0)


def _conv_fwd(proj, conv_w, conv_out_g, jobs=()):
    s = proj.shape[0]
    groups = CONV_WIDTH // 128

    def body(u_ref, gb_ref, gc_ref, w_ref, g_ref, y_ref):
        z = gc_ref[...] * u_ref[...]
        w = w_ref[...]
        conv = w[0:1, :] * _shift_rows(z, 2) + w[1:2, :] * _shift_rows(z, 1) + w[2:3, :] * z
        y = gb_ref[...] * conv
        y_ref[...] = (y * _rstd(y) * g_ref[...]).astype(y_ref.dtype)

    col = lambda base: pl.BlockSpec((s, 128), lambda j: (0, base + j))
    return _call(
        body, name="conv_fwd", grid=(groups,),
        in_specs=[col(0), col(groups), col(2 * groups), pl.BlockSpec((3, 128), lambda j: (0, j)),
                  pl.BlockSpec((1, 128), lambda j: (0, j))],
        out_specs=[pl.BlockSpec((s, 128), lambda j: (0, j))],
        out_shape=[jax.ShapeDtypeStruct((s, CONV_WIDTH), jnp.bfloat16)],
        semantics=("parallel",), args=(proj, proj, proj, conv_w, conv_out_g), jobs=jobs)[0]


def _conv_bwd(proj, conv_w, conv_out_g, dycat, jobs=()):
    s = proj.shape[0]
    groups = CONV_WIDTH // 128

    def body(u_ref, gb_ref, gc_ref, w_ref, g_ref, dy_ref, du_ref, dgb_ref, dgc_ref, dw_ref, dg_ref):
        u, gb, gc = u_ref[...], gb_ref[...], gc_ref[...]
        w = w_ref[...]
        z = gc * u
        z1, z2 = _shift_rows(z, 1), _shift_rows(z, 2)
        conv = w[0:1, :] * z2 + w[1:2, :] * z1 + w[2:3, :] * z
        dyr, dgr = _rms_bwd_rows(gb * conv, g_ref[...], dy_ref[...])
        dg_ref[...] = jnp.sum(dgr, axis=0, keepdims=True)
        dgb_ref[...] = (dyr * conv).astype(dgb_ref.dtype)
        dconv = dyr * gb
        dw_ref[0:1, :] = jnp.sum(dconv * z2, axis=0, keepdims=True)
        dw_ref[1:2, :] = jnp.sum(dconv * z1, axis=0, keepdims=True)
        dw_ref[2:3, :] = jnp.sum(dconv * z, axis=0, keepdims=True)
        dz = w[2:3, :] * dconv + w[1:2, :] * _shift_rows(dconv, -1) + w[0:1, :] * _shift_rows(dconv, -2)
        dgc_ref[...] = (dz * u).astype(dgc_ref.dtype)
        du_ref[...] = (dz * gc).astype(du_ref.dtype)

    col = lambda base: pl.BlockSpec((s, 128), lambda j: (0, base + j))
    act = jax.ShapeDtypeStruct((s, CONV_WIDTH), jnp.bfloat16)
    return _call(
        body, name="conv_bwd", grid=(groups,),
        in_specs=[col(0), col(groups), col(2 * groups), pl.BlockSpec((3, 128), lambda j: (0, j)),
                  pl.BlockSpec((1, 128), lambda j: (0, j)), col(0)],
        out_specs=[col(0), col(0), col(0), pl.BlockSpec((3, 128), lambda j: (0, j)),
                   pl.BlockSpec((1, 128), lambda j: (0, j))],
        out_shape=[act, act, act, jax.ShapeDtypeStruct((3, CONV_WIDTH), jnp.float32),
                   jax.ShapeDtypeStruct((1, CONV_WIDTH), jnp.float32)],
        semantics=("parallel",), args=(proj, proj, proj, conv_w, conv_out_g, dycat), jobs=jobs)


def _rope_tables(s):
    pos = jnp.arange(s, dtype=jnp.float32)
    inv_freq = jnp.power(ROPE_THETA, -jnp.arange(0, QK_ROPE, 2, dtype=jnp.float32) / QK_ROPE)
    ang = pos[:, None] * inv_freq[None, :]
    cos, sin = jnp.cos(ang), jnp.sin(ang)
    zeros = jnp.zeros((s, 128 - QK_ROPE), jnp.float32)
    return (jnp.concatenate([cos, cos, zeros], axis=1), jnp.concatenate([-sin, sin, zeros], axis=1))


def _swap_halves(x):
    lane = lax.broadcasted_iota(jnp.int32, x.shape, 1)
    swapped = jnp.where(lane < QK_ROPE // 2, pltpu.roll(x, 128 - QK_ROPE // 2, axis=1),
                        pltpu.roll(x, QK_ROPE // 2, axis=1))
    return jnp.where(lane < QK_ROPE, swapped, 0.0)


def _rope(x, cos, sin):
    return x * cos + _swap_halves(x) * sin


def _rope_transposed(d, cos, sin):
    return d * cos + _swap_halves(d * sin)


def _qk_rope_fwd(q_pad, proj, cos, sin, *, tm, jobs=()):
    s = q_pad.shape[0]
    wq = N_HEADS * HEAD_PAD

    def body(q_ref, kr_ref, cos_ref, sin_ref, qo_ref, kro_ref):
        c, sn = cos_ref[...], sin_ref[...]
        for h in range(N_HEADS):
            lo = h * HEAD_PAD
            qo_ref[:, lo:lo + 128] = q_ref[:, lo:lo + 128].astype(qo_ref.dtype)
            qo_ref[:, lo + 128:lo + 256] = _rope(q_ref[:, lo + 128:lo + 256], c, sn).astype(qo_ref.dtype)
        kro_ref[...] = _rope(kr_ref[...], c, sn).astype(kro_ref.dtype)

    return _call(
        body, name="qk_rope_fwd", grid=(s // tm,),
        in_specs=[pl.BlockSpec((tm, wq), lambda i: (i, 0)), pl.BlockSpec((tm, 128), lambda i: (i, COL_KR // 128)),
                  pl.BlockSpec((tm, 128), lambda i: (i, 0)), pl.BlockSpec((tm, 128), lambda i: (i, 0))],
        out_specs=[pl.BlockSpec((tm, wq), lambda i: (i, 0)), pl.BlockSpec((tm, 128), lambda i: (i, 0))],
        out_shape=[jax.ShapeDtypeStruct((s, wq), jnp.bfloat16), jax.ShapeDtypeStruct((s, 128), jnp.bfloat16)],
        semantics=("parallel",), args=(q_pad, proj, cos, sin), jobs=jobs)


def _qk_rope_bwd(dq_pad, dk_pad, dv, cos, sin, *, tm, jobs=()):
    s = dq_pad.shape[0]
    wq = N_HEADS * HEAD_PAD

    def body(dq_ref, dk_ref, dv_ref, cos_ref, sin_ref, dqo_ref, dkv_ref, dkr_ref):
        c, sn = cos_ref[...], sin_ref[...]
        dkr = jnp.zeros((tm, 128), jnp.float32)
        for h in range(N_HEADS):
            lo = h * HEAD_PAD
            dqo_ref[:, lo:lo + 128] = dq_ref[:, lo:lo + 128].astype(dqo_ref.dtype)
            dqo_ref[:, lo + 128:lo + 256] = _rope_transposed(dq_ref[:, lo + 128:lo + 256], c, sn).astype(dqo_ref.dtype)
            dkv_ref[:, lo:lo + 128] = dk_ref[:, lo:lo + 128].astype(dkv_ref.dtype)
            dkv_ref[:, lo + 128:lo + 256] = dv_ref[:, h * V_DIM:(h + 1) * V_DIM].astype(dkv_ref.dtype)
            dkr = dkr + dk_ref[:, lo + 128:lo + 256]
        dkr_ref[...] = _rope_transposed(dkr, c, sn).astype(dkr_ref.dtype)

    return _call(
        body, name="qk_rope_bwd", grid=(s // tm,),
        in_specs=[pl.BlockSpec((tm, wq), lambda i: (i, 0)), pl.BlockSpec((tm, wq), lambda i: (i, 0)),
                  pl.BlockSpec((tm, N_HEADS * V_DIM), lambda i: (i, 0)),
                  pl.BlockSpec((tm, 128), lambda i: (i, 0)), pl.BlockSpec((tm, 128), lambda i: (i, 0))],
        out_specs=[pl.BlockSpec((tm, wq), lambda i: (i, 0)), pl.BlockSpec((tm, wq), lambda i: (i, 0)),
                   pl.BlockSpec((tm, 128), lambda i: (i, 0))],
        out_shape=[jax.ShapeDtypeStruct((s, wq), jnp.bfloat16), jax.ShapeDtypeStruct((s, wq), jnp.bfloat16),
                   jax.ShapeDtypeStruct((s, 128), jnp.bfloat16)],
        semantics=("parallel",), args=(dq_pad, dk_pad, dv, cos, sin), jobs=jobs)


def _visible(q0, k0, t):
    qpos = q0 + lax.broadcasted_iota(jnp.int32, (t, t), 0)
    kpos = k0 + lax.broadcasted_iota(jnp.int32, (t, t), 1)
    return lax.shift_right_logical(kpos, CHUNK_SHIFT) <= lax.shift_right_logical(qpos, CHUNK_SHIFT)


def _attn_fwd(q, kv, kr, attn_out_g, *, t, jobs=()):
    s = q.shape[0]
    nq = s // t

    def body(q_ref, kn_ref, v_ref, kr_ref, g_ref, o_ref, lse_ref, y_ref, kcat_ref):
        i = pl.program_id(1)

        @pl.when(i == 0)
        def _():
            kcat_ref[:, 0:128] = kn_ref[...]
            kcat_ref[:, 128:256] = kr_ref[...]

        qv = q_ref[...]

        def step(j, carry, diagonal):
            m, l, acc = carry
            k = kcat_ref[pl.ds(pl.multiple_of(j * t, t), t), :]
            v = v_ref[pl.ds(pl.multiple_of(j * t, t), t), :]
            sc = lax.dot_general(qv, k, _NT, preferred_element_type=jnp.float32) * ATTN_SCALE
            if diagonal:
                sc = jnp.where(_visible(0, 0, t), sc, NEG_INF)
            m_new = jnp.maximum(m, jnp.max(sc, axis=-1, keepdims=True))
            p = jnp.exp(sc - m_new)
            alpha = jnp.exp(m - m_new)
            l = alpha * l + jnp.sum(p, axis=-1, keepdims=True)
            acc = alpha * acc + lax.dot_general(p.astype(jnp.bfloat16), v, _NN, preferred_element_type=jnp.float32)
            return m_new, l, acc

        init = (jnp.full((t, 1), NEG_INF, jnp.float32), jnp.zeros((t, 1), jnp.float32),
                jnp.zeros((t, V_DIM), jnp.float32))
        before = lax.fori_loop(0, i, functools.partial(step, diagonal=False), init)
        m, l, acc = step(i, before, True)
        o = acc / l
        o_ref[...] = o
        lse_ref[...] = jnp.broadcast_to(m + jnp.log(l), (t, 128))
        y_ref[...] = (o * _rstd(o) * g_ref[...]).astype(y_ref.dtype)

    head_rows = lambda w, f: pl.BlockSpec((s, w), lambda h, i: (0, f(h)))
    blk = pl.BlockSpec((t, 128), lambda h, i: (i, h))
    full = jax.ShapeDtypeStruct((s, N_HEADS * V_DIM), jnp.float32)
    return _call(
        body, name="attn_fwd", grid=(N_HEADS, nq),
        in_specs=[pl.BlockSpec((t, HEAD_PAD), lambda h, i: (i, h)), head_rows(128, lambda h: 2 * h),
                  head_rows(128, lambda h: 2 * h + 1), head_rows(128, lambda h: 0),
                  pl.BlockSpec((1, 128), lambda h, i: (0, h))],
        out_specs=[blk, blk, blk],
        out_shape=[full, full, jax.ShapeDtypeStruct((s, N_HEADS * V_DIM), jnp.bfloat16)],
        scratch_shapes=[pltpu.VMEM((s, HEAD_PAD), jnp.bfloat16)],
        semantics=("arbitrary", "arbitrary"), args=(q, kv, kv, kr, attn_out_g), jobs=jobs)


def _attn_bwd(q, kv, kr, o, lse, attn_out_g, dycat, *, t, jobs=()):
    s = q.shape[0]
    nq = s // t

    def body(q_ref, kn_ref, v_ref, kr_ref, o_ref, lse_ref, g_ref, dy_ref, dq_ref, dk_ref, dv_ref, dg_ref,
             kcat_ref, do_ref, delta_ref):
        ov = o_ref[...]
        do, dgr = _rms_bwd_rows(ov, g_ref[...], dy_ref[...])
        do_ref[...] = do.astype(do_ref.dtype)
        delta_ref[...] = jnp.broadcast_to(jnp.sum(do * ov, axis=-1, keepdims=True), (s, 128))
        dg_ref[...] = jnp.sum(dgr, axis=0, keepdims=True)
        kcat_ref[:, 0:128] = kn_ref[...]
        kcat_ref[:, 128:256] = kr_ref[...]
        dq_ref[...] = jnp.zeros_like(dq_ref)
        dk_ref[...] = jnp.zeros_like(dk_ref)
        dv_ref[...] = jnp.zeros_like(dv_ref)

        def kv_step(j, _):
            krows = pl.ds(pl.multiple_of(j * t, t), t)
            k = kcat_ref[krows, :]
            v = v_ref[krows, :]

            def q_step(i, _, diagonal):
                qrows = pl.ds(pl.multiple_of(i * t, t), t)
                qv = q_ref[qrows, :]
                dov = do_ref[qrows, :]
                sc = lax.dot_general(qv, k, _NT, preferred_element_type=jnp.float32) * ATTN_SCALE
                if diagonal:
                    sc = jnp.where(_visible(0, 0, t), sc, NEG_INF)
                p = jnp.exp(sc - lse_ref[qrows, :][:, 0:1])
                dp = lax.dot_general(dov, v, _NT, preferred_element_type=jnp.float32)
                ds = (p * (dp - delta_ref[qrows, :][:, 0:1]) * ATTN_SCALE).astype(jnp.bfloat16)
                dv_ref[krows, :] += lax.dot_general(p.astype(jnp.bfloat16), dov, _TN,
                                                    preferred_element_type=jnp.float32)
                dk_ref[krows, :] += lax.dot_general(ds, qv, _TN, preferred_element_type=jnp.float32)
                dq_ref[qrows, :] += lax.dot_general(ds, k, _NN, preferred_element_type=jnp.float32)
                return 0

            q_step(j, 0, True)
            lax.fori_loop(j + 1, nq, functools.partial(q_step, diagonal=False), 0)
            return 0

        lax.fori_loop(0, nq, kv_step, 0)

    col = lambda w, f: pl.BlockSpec((s, w), lambda h: (0, f(h)))
    return _call(
        body, name="attn_bwd", grid=(N_HEADS,),
        in_specs=[col(HEAD_PAD, lambda h: h), col(128, lambda h: 2 * h), col(128, lambda h: 2 * h + 1),
                  col(128, lambda h: 0), col(128, lambda h: h), col(128, lambda h: h),
                  pl.BlockSpec((1, 128), lambda h: (0, h)), col(128, lambda h: CONV_WIDTH // 128 + h)],
        out_specs=[col(HEAD_PAD, lambda h: h), col(HEAD_PAD, lambda h: h), col(128, lambda h: h),
                   pl.BlockSpec((1, 128), lambda h: (0, h))],
        out_shape=[jax.ShapeDtypeStruct((s, N_HEADS * HEAD_PAD), jnp.float32),
                   jax.ShapeDtypeStruct((s, N_HEADS * HEAD_PAD), jnp.float32),
                   jax.ShapeDtypeStruct((s, N_HEADS * V_DIM), jnp.float32),
                   jax.ShapeDtypeStruct((1, N_HEADS * V_DIM), jnp.float32)],
        scratch_shapes=[pltpu.VMEM((s, HEAD_PAD), jnp.bfloat16), pltpu.VMEM((s, V_DIM), jnp.bfloat16),
                        pltpu.VMEM((s, 128), jnp.float32)],
        semantics=("parallel",), args=(q, kv, kv, kr, o, lse, attn_out_g, dycat), jobs=jobs)


def _row_tile(rows, cap=256):
    for cand in (512, 256, 128, 64, 32, 16, 8):
        if cand <= cap and rows % cand == 0:
            return cand
    return rows


def _cast_into_slot(w, pos, *, name, jobs=()):
    r, c = w.shape
    tr = _row_tile(r)

    def body(pos_ref, w_ref, o_ref):
        o_ref[...] = w_ref[...].astype(o_ref.dtype)

    return _call(
        body, name=name, grid=(r // tr,), prefetch=pos,
        in_specs=[pl.BlockSpec((tr, c), lambda i, p: (i, 0))],
        out_specs=[pl.BlockSpec((None, tr, c), lambda i, p: (p[1], i, 0))],
        out_shape=[jax.ShapeDtypeStruct((4, r, c), jnp.bfloat16)],
        semantics=("parallel",), args=(w,), jobs=jobs)[0]


def _cast_many_into_slots(ws, pos, *, name, jobs=()):
    steps = 8
    assert all(w.shape[0] % (16 * steps) == 0 for w in ws), [w.shape for w in ws]

    def body(pos_ref, *refs):
        for w_ref, o_ref in zip(refs[:len(ws)], refs[len(ws):]):
            o_ref[...] = w_ref[...].astype(o_ref.dtype)

    return _call(
        body, name=name, grid=(steps,), prefetch=pos,
        in_specs=[pl.BlockSpec((w.shape[0] // steps, w.shape[1]), lambda i, p: (i, 0)) for w in ws],
        out_specs=[pl.BlockSpec((None, w.shape[0] // steps, w.shape[1]), lambda i, p: (p[1], i, 0)) for w in ws],
        out_shape=[jax.ShapeDtypeStruct((4, *w.shape), jnp.bfloat16) for w in ws],
        semantics=("parallel",), args=tuple(ws), jobs=jobs)


def _pair_add(g, theirs, pos, *, name, jobs=()):
    n, h, c = theirs.shape
    tr = _row_tile(h, cap=512)
    nb = h // tr

    def body(pos_ref, a_ref, b_ref, o_ref):
        o_ref[...] = (a_ref[...].astype(jnp.float32) + b_ref[...].astype(jnp.float32)).astype(o_ref.dtype)

    spec = pl.BlockSpec((None, tr, c), lambda j, i, p: (j, i, 0))
    return _call(
        body, name=name, grid=(n, nb), prefetch=pos,
        in_specs=[pl.BlockSpec((None, tr, c), lambda j, i, p: (j, i + p[0] * nb, 0)), spec],
        out_specs=[spec], out_shape=[jax.ShapeDtypeStruct(theirs.shape, jnp.bfloat16)],
        semantics=("parallel", "parallel"), args=(g, theirs), jobs=jobs)[0]


def _fold_diag(pair_sum, via, pos, *, name):
    n, h, c = pair_sum.shape
    tr = _row_tile(h // 2, cap=512)
    nb = h // tr

    def body(pos_ref, p_ref, via_ref, o_ref):
        j, i = pl.program_id(0), pl.program_id(1)
        mine = p_ref[...].astype(jnp.float32)
        add = (j == 0) == (i >= nb // 2)
        o_ref[...] = jnp.where(add, mine + via_ref[...].astype(jnp.float32), mine).astype(o_ref.dtype)

    return _call(
        body, name=name, grid=(2, nb), prefetch=pos,
        in_specs=[pl.BlockSpec((None, tr, c), lambda j, i, p: (jnp.bitwise_xor(p[1], 2 - j), i, 0)),
                  pl.BlockSpec((tr, c), lambda j, i, p: (i, 0))],
        out_specs=[pl.BlockSpec((None, tr, c), lambda j, i, p: (j, i, 0))],
        out_shape=[jax.ShapeDtypeStruct((2, h, c), jnp.bfloat16)],
        semantics=("parallel", "parallel"), args=(pair_sum, via))[0]


def _chip_sum(by_source, pair_sum, pos, *, name):
    n, h, c = by_source.shape
    tr = _row_tile(h, cap=512)
    nb = h // tr

    def body(pos_ref, own_ref, px_ref, py_ref, o_ref):
        f = lambda ref: ref[...].astype(jnp.float32)
        o_ref[...] = (f(own_ref) + f(px_ref)) + f(py_ref)

    slot = lambda flip: pl.BlockSpec((None, tr, c), lambda i, p: (jnp.bitwise_xor(p[1], flip), i, 0))
    return pl.pallas_call(
        body, name=name, out_shape=jax.ShapeDtypeStruct((2 * h, c), jnp.float32),
        grid_spec=pltpu.PrefetchScalarGridSpec(
            num_scalar_prefetch=1, grid=(nb,),
            in_specs=[slot(0), slot(2), slot(1)],
            out_specs=pl.BlockSpec((tr, c), lambda i, p: (i + p[0] * nb, 0))),
        compiler_params=_params(("parallel",)),
    )(pos, pair_sum, by_source, by_source)


def _adamw(w, g, m, v, *, name, jobs=()):
    r, c = w.shape
    tr = _row_tile(r)

    def body(w_ref, g_ref, m_ref, v_ref, d_ref, mo_ref, vo_ref, go_ref):
        gv = g_ref[...]
        go_ref[...] = gv
        mn = ADAM_B1 * m_ref[...] + (1.0 - ADAM_B1) * gv
        vn = ADAM_B2 * v_ref[...] + (1.0 - ADAM_B2) * (gv * gv)
        m_hat = mn / (1.0 - ADAM_B1 ** ADAM_STEP)
        v_hat = vn / (1.0 - ADAM_B2 ** ADAM_STEP)
        d_ref[...] = -ADAM_LR * (m_hat / (jnp.sqrt(v_hat) + ADAM_EPS) + ADAM_WD * w_ref[...])
        mo_ref[...] = mn
        vo_ref[...] = vn

    spec = pl.BlockSpec((tr, c), lambda i: (i, 0))
    out = jax.ShapeDtypeStruct((r, c), jnp.float32)
    return _call(body, name=name, grid=(r // tr,), in_specs=[spec] * 4, out_specs=[spec] * 4, out_shape=[out] * 4,
                 semantics=("parallel",), args=(w, g, m, v), jobs=jobs)


def _all_reduce_small(block):
    r, c = block.shape

    def body(src_ref, out_ref, stage_ref, send_sems, recv_sems):
        x, y, cc = _position()
        me = 4 * x + 2 * y + cc
        stage_ref[me] = src_ref[...]
        flip = lambda v, on: 1 - v if on else v
        peers = [(flip(x, dx), flip(y, dy), flip(cc, dc)) for dx in (0, 1) for dy in (0, 1) for dc in (0, 1)][1:]
        copies = [pltpu.make_async_remote_copy(
            src_ref=stage_ref.at[me], dst_ref=stage_ref.at[me],
            send_sem=send_sems.at[k], recv_sem=recv_sems.at[k], device_id=peer, device_id_type=MESH)
            for k, peer in enumerate(peers)]
        for cp in copies:
            cp.start()
        for k, (px, py, pc) in enumerate(peers):
            them = 4 * px + 2 * py + pc
            pltpu.make_async_remote_copy(
                src_ref=stage_ref.at[them], dst_ref=stage_ref.at[them],
                send_sem=send_sems.at[k], recv_sem=recv_sems.at[k], device_id=(px, py, pc),
                device_id_type=MESH).wait_recv()
        for cp in copies:
            cp.wait_send()
        total = stage_ref[0]
        for d in range(1, 8):
            total = total + stage_ref[d]
        out_ref[...] = total

    return pl.pallas_call(
        body, name="all_reduce_small",
        in_specs=[pl.BlockSpec(memory_space=pltpu.VMEM)], out_specs=pl.BlockSpec(memory_space=pltpu.VMEM),
        out_shape=jax.ShapeDtypeStruct((r, c), jnp.float32),
        scratch_shapes=[pltpu.VMEM((8, r, c), jnp.float32), pltpu.SemaphoreType.DMA((7,)),
                        pltpu.SemaphoreType.DMA((7,))],
        compiler_params=pltpu.CompilerParams(has_side_effects=True),
    )(block)


def _cols_from_shards(g):
    n, r, c = g.shape
    return jnp.transpose(g, (1, 0, 2)).reshape(r, n * c)


def _cols_to_shards(w, n=4):
    r, c = w.shape
    return jnp.transpose(w.reshape(r, n, c // n), (1, 0, 2))


def _pad_w_in(g):
    _, d, c = g.shape
    cut = COL_KR - 3 * c
    zeros = jnp.zeros((d, COL_CKV - COL_KR - QK_ROPE), g.dtype)
    return jnp.concatenate([g[0], g[1], g[2], g[3][:, :cut], g[3][:, cut + KV_RANK:], zeros,
                            g[3][:, cut:cut + KV_RANK]], axis=1)


def _unpad_w_in(padded):
    c = IN_WIDTH // 4
    last = jnp.concatenate([padded[:, 3 * c:COL_KR], padded[:, COL_CKV:COL_CKV + KV_RANK],
                            padded[:, COL_KR:COL_KR + QK_ROPE]], axis=1)
    return jnp.stack([padded[:, 0:c], padded[:, c:2 * c], padded[:, 2 * c:3 * c], last])


def _pad_w_uq(full):
    r = full.shape[0]
    per_head = full.reshape(r, N_HEADS, QK_NOPE + QK_ROPE)
    return jnp.pad(per_head, ((0, 0), (0, 0), (0, HEAD_PAD - QK_NOPE - QK_ROPE))).reshape(r, N_HEADS * HEAD_PAD)


def _unpad_w_uq(padded):
    r = padded.shape[0]
    return padded.reshape(r, N_HEADS, HEAD_PAD)[:, :, :QK_NOPE + QK_ROPE].reshape(r, N_HEADS * (QK_NOPE + QK_ROPE))


SMALL_ROWS = 16


def _pack_small(d, pre_mix, post_mix, pre_mlp, post_mlp, conv_out, attn_out, q_norm, kv_norm, conv_w):
    row = lambda *parts: jnp.pad(jnp.concatenate(parts, axis=1), ((0, 0), (0, d - sum(p.shape[1] for p in parts))))
    rows = [row(pre_mix), row(post_mix), row(pre_mlp), row(post_mlp), row(conv_out, attn_out), row(q_norm, kv_norm),
            row(conv_w[0:1]), row(conv_w[1:2]), row(conv_w[2:3])]
    return jnp.pad(jnp.concatenate(rows, axis=0), ((0, SMALL_ROWS - len(rows)), (0, 0)))


def _unpack_small(p, chip):
    cw = CONV_WIDTH // 4
    conv_w = lax.dynamic_slice(p[6:9, :CONV_WIDTH], (0, chip * cw), (3, cw))
    return dict(pre_mix_g=p[0:1], post_mix_g=p[1:2], pre_mlp_g=p[2:3], post_mlp_g=p[3:4],
                conv_out_g=p[4:5, :CONV_WIDTH], attn_out_g=p[4:5, CONV_WIDTH:2 * CONV_WIDTH],
                q_norm_g=p[5:6, :Q_RANK], kv_norm_g=p[5:6, Q_RANK:Q_RANK + KV_RANK], conv_w=conv_w[None])


def kernel(x, pre_mix_g, w_in, conv_w, q_norm_g, w_uq, kv_norm_g, w_ukv, conv_out_g, attn_out_g, w_o, post_mix_g, pre_mlp_g, w_up, w_down, post_mlp_g, loss_target, m_pre_mix_g, m_w_in, m_conv_w, m_q_norm_g, m_w_uq, m_kv_norm_g, m_w_ukv, m_conv_out_g, m_attn_out_g, m_w_o, m_post_mix_g, m_pre_mlp_g, m_w_up, m_w_down, m_post_mlp_g, v_pre_mix_g, v_w_in, v_conv_w, v_q_norm_g, v_w_uq, v_kv_norm_g, v_w_ukv, v_conv_out_g, v_attn_out_g, v_w_o, v_post_mix_g, v_pre_mlp_g, v_w_up, v_w_down, v_post_mlp_g):
    bf16 = jnp.bfloat16
    s, d = x.shape[1], x.shape[2]
    d_ff = 4 * d
    chip = 2 * lax.axis_index("x") + lax.axis_index("y")
    xs = x.reshape(s, d)
    target = loss_target.reshape(s, d)
    tm = min(256, s)
    t_attn = min(1024, s)
    mt = min(1024, s)
    kt = min(2048, s)

    big = dict(w_in=w_in[0], w_uq=w_uq[0], w_ukv=w_ukv[0], w_o=w_o[0], w_up=w_up[0], w_down=w_down[0])
    names = list(big)
    pos = jnp.stack([lax.axis_index("c"), chip]).astype(jnp.int32)
    wb = {}
    half = {k: big[k].shape[0] // 2 for k in names}

    def rows(k, a, b):
        lo, n = a * half[k] // 64, (b - a) * half[k] // 64
        assert lo % 16 == 0 and n % 16 == 0 and n > 0, (k, a, b)
        return lo, n

    ici = lambda k, a=0, b=64: _GatherIci(wb[k], *rows(k, a, b))
    fwd = lambda k, a=0, b=64: _GatherForward(wb[k], *rows(k, a, b))
    near = lambda k, a=0, b=64: _GatherD2d(wb[k], *rows(k, a, b), slots=(0, 1))
    far = lambda k, a=0, b=64: _GatherD2d(wb[k], *rows(k, a, b), slots=(2,))
    tap_rows = 64
    half["conv_w"] = tap_rows // 2
    wb["conv_w"] = _Buf(lax.dynamic_update_slice(
        jnp.zeros((4, tap_rows, CONV_WIDTH // 4), jnp.float32),
        jnp.pad(conv_w[0], ((0, tap_rows - conv_w.shape[1]), (0, 0)))[None], (chip, 0, 0)))
    wb["w_in"] = _Buf(_cast_into_slot(big["w_in"], pos, name="cast_w_in"))
    rest = [k for k in names if k != "w_in"]
    for k, slot in zip(rest, _cast_many_into_slots([big[k] for k in rest], pos, name="cast_rest",
                                                   jobs=[ici("w_in"), ici("conv_w")])):
        wb[k] = _Buf(slot)
    h1 = _rms_fwd(xs, pre_mix_g, width=d, col=0, tm=tm, name="rms_pre_mix",
                  jobs=[fwd("w_in"), near("w_in"), fwd("conv_w"), near("conv_w")])
    _comm("gather_w_in", [[far("w_in"), far("conv_w")]])
    win = _pad_w_in(wb["w_in"].arr)
    conv_w_full = jnp.transpose(wb["conv_w"].arr[:, :conv_w.shape[1], :], (1, 0, 2)).reshape(-1, CONV_WIDTH)
    spread = lambda a: lax.dynamic_update_slice(jnp.zeros((3, CONV_WIDTH), jnp.float32), a[0],
                                                (0, chip * (CONV_WIDTH // 4)))
    ff4 = d_ff // 4

    proj = _matmul(h1, win, dims=_NN, mnk=(s, IN_PAD, d), tiles=(mt, IN_PAD // 3, d), name="mm_proj",
                   jobs=[ici("w_uq"), ici("w_ukv"), ici("w_o")])
    y_conv = _conv_fwd(proj, conv_w_full, conv_out_g,
                       jobs=[fwd("w_uq"), fwd("w_ukv"), fwd("w_o"), near("w_uq"), near("w_ukv"), near("w_o")])
    cqn, ckvn = _latent_norms_fwd(proj, q_norm_g, kv_norm_g, tm=mt,
                                  jobs=[far("w_uq"), far("w_ukv"), far("w_o"), ici("w_up", 0, 10)])
    wuq = _pad_w_uq(_cols_from_shards(wb["w_uq"].arr))
    wukv = _cols_from_shards(wb["w_ukv"].arr)
    wo = wb["w_o"].arr.reshape(-1, d)
    q_pad = _matmul(cqn, wuq, dims=_NN, mnk=(s, N_HEADS * HEAD_PAD, Q_RANK), tiles=(mt, 1024, Q_RANK), name="mm_q",
                    jobs=[ici("w_up", 10, 18), fwd("w_up", 0, 10), near("w_up", 0, 4)])
    kv = _matmul(ckvn, wukv, dims=_NN, mnk=(s, N_HEADS * HEAD_PAD, KV_RANK), tiles=(mt, 1024, KV_RANK),
                 name="mm_kv", out_dtype=bf16,
                 jobs=[ici("w_up", 18, 24), fwd("w_up", 10, 18), near("w_up", 4, 10), far("w_up", 0, 4)])
    cos, sin = _rope_tables(s)
    q_rot, kr_rot = _qk_rope_fwd(
        q_pad, proj, cos, sin, tm=tm,
        jobs=[ici("w_up", 24, 32), fwd("w_up", 18, 24), near("w_up", 10, 18), far("w_up", 4, 10)])
    o, lse, y_attn = _attn_fwd(
        q_rot, kv, kr_rot, attn_out_g, t=t_attn,
        jobs=[ici("w_up", 32, 64), ici("w_down", 0, 8), fwd("w_up", 24, 32), near("w_up", 18, 24), far("w_up", 10, 18)])
    ycat = jnp.concatenate([y_conv, y_attn], axis=1)
    mix = _matmul(ycat, wo, dims=_NN, mnk=(s, d, 2 * CONV_WIDTH), tiles=(mt, 1024, 2 * CONV_WIDTH), name="mm_out",
                  jobs=[fwd("w_up", 32, 64), near("w_up", 24, 32), far("w_up", 18, 24)])
    x2, h2 = _mix_residual_fwd(
        xs, mix, post_mix_g, pre_mlp_g, tm=tm,
        jobs=[ici("w_down", 8, 20), fwd("w_down", 0, 8), near("w_up", 32, 64), far("w_up", 24, 64)])
    wup = wb["w_up"].arr

    def up_epilogue(acc, extra_refs, out_refs):
        r = jnp.maximum(acc, 0.0)
        out_refs[0][...] = acc.astype(bf16)
        out_refs[1][...] = (r * r).astype(bf16)

    n_ff = ff4 // 1024
    act = jax.ShapeDtypeStruct((s, d_ff), bf16)
    up, act_sq = _matmul(
        h2, wup, dims=_NN, mnk=(s, d_ff, d), tiles=(mt, 1024, d), name="mm_up",
        b_spec=pl.BlockSpec((None, d, 1024), lambda i, j, l: (j // n_ff, l, j % n_ff)),
        out_shape=(act, act), o_spec=(pl.BlockSpec((mt, 1024), lambda i, j, l: (i, j)),) * 2, epilogue=up_epilogue,
        jobs=[ici("w_down", 20, 64), fwd("w_down", 8, 20), near("w_down", 0, 8)])
    _comm("gather_w_down_tail", [[fwd("w_down", 20, 64), near("w_down", 8, 64), far("w_down", 0, 20)],
                                 [far("w_down", 20, 64)]])
    wdown = wb["w_down"].arr.reshape(d_ff, d)
    mlp = _matmul(act_sq, wdown, dims=_NN, mnk=(s, d, d_ff), tiles=(min(512, s), 512, d_ff), name="mm_down")
    dx3, dmlp, dg_post_mlp, loss_part = _loss_head(x2, mlp, target, post_mlp_g, tm=tm)

    def dup_epilogue(acc, extra_refs, out_refs):
        out_refs[0][...] = (acc * (2.0 * jnp.maximum(extra_refs[0][...].astype(jnp.float32), 0.0))).astype(bf16)

    grads, theirs, pair_sums, via, folded, by_source, whole = {}, {}, {}, {}, {}, {}, {}

    def exchange(k, g):
        grads[k] = g
        theirs[k] = _Buf(jax.ShapeDtypeStruct((4, g.shape[1] // 2, g.shape[2]), bf16))
        return _PairExchange(g, theirs[k])

    def pair_sum(k, jobs=()):
        pair_sums[k] = _pair_add(grads[k], theirs[k].arr, pos, name="pair_add_" + k, jobs=jobs)
        via[k] = _Buf(jax.ShapeDtypeStruct(pair_sums[k].shape[1:], bf16))
        by_source[k] = _Buf(jax.ShapeDtypeStruct(pair_sums[k].shape, bf16))

    def diag(k, a=0, b=32):
        lo, n = a * half[k] // 64, (b - a) * half[k] // 64
        assert lo % 16 == 0 and n % 16 == 0 and n > 0, (k, a, b)
        return _ScatterDiag(pair_sums[k], via[k], lo, n)

    def fold(k):
        folded[k] = _fold_diag(pair_sums[k], via[k].arr, pos, name="fold_" + k)

    scatter = lambda k, a=0, b=64: _ScatterNear(folded[k], by_source[k], *rows(k, a, b))

    def share(k):
        whole[k] = _Buf(_chip_sum(by_source[k].arr, pair_sums[k], pos, name="chip_sum_" + k))
        return _PairShare(whole[k])

    g_wdown = _matmul(act_sq, dmlp, dims=_TN, mnk=(d_ff, d, s), tiles=(1024, 1024, kt), name="mm_gw_down",
                      out_dtype=bf16).reshape(4, ff4, d)
    dup = _matmul(dmlp, wdown, dims=_NT, mnk=(s, d_ff, d), tiles=(mt, 1024, d), name="mm_dact",
                  out_dtype=bf16, epilogue=dup_epilogue, extra=(up,),
                  extra_specs=(pl.BlockSpec((mt, 1024), lambda i, j, l: (i, j)),),
                  jobs=[exchange("w_down", g_wdown)])
    pair_sum("w_down")
    g_wup = _matmul(h2, dup, dims=_TN, mnk=(d, d_ff, s), tiles=(1024, 1024, kt), name="mm_gw_up",
                    out_shape=jax.ShapeDtypeStruct((4, d, ff4), bf16),
                    o_spec=pl.BlockSpec((None, 1024, 1024), lambda i, j, l: (j // n_ff, i, j % n_ff)),
                    jobs=[diag("w_down")])
    fold("w_down")
    dh2 = _matmul(dup, wup, dims=_NT, mnk=(s, d, d_ff), tiles=(mt, 1024, ff4), name="mm_dh2",
                  b_spec=pl.BlockSpec((None, 1024, ff4), lambda i, j, l: (l, j, 0)),
                  jobs=[exchange("w_up", g_wup), scatter("w_down")])
    pair_sum("w_up")
    dx2, dmix, dg_pre_mlp, dg_post_mix = _mix_residual_bwd(
        dx3, dh2, x2, mix, pre_mlp_g, post_mix_g, tm=tm, jobs=[diag("w_up", 0, 20)])

    dycat = _matmul(dmix, wo, dims=_NT, mnk=(s, 2 * CONV_WIDTH, d), tiles=(mt, 1024, d), name="mm_dycat",
                    jobs=[diag("w_up", 20, 32)])
    fold("w_up")
    g_wo = _matmul(ycat, dmix, dims=_TN, mnk=(2 * CONV_WIDTH, d, s), tiles=(1024, 1024, kt),
                   name="mm_gw_out", out_dtype=bf16, jobs=[scatter("w_up", 0, 12)]).reshape(4, CONV_WIDTH // 2, d)
    du, dgb, dgc, dg_conv_w, dg_conv_out = _conv_bwd(proj, conv_w_full, conv_out_g, dycat,
                                                     jobs=[exchange("w_o", g_wo), scatter("w_up", 12, 24)])
    pair_sum("w_o")
    dq_pad, dk_pad, dv, dg_attn_out = _attn_bwd(
        q_rot, kv, kr_rot, o, lse, attn_out_g, dycat, t=min(1024, s),
        jobs=[scatter("w_up", 24, 64), diag("w_o"), share("w_down")])
    fold("w_o")
    dq_raw, dkv, dkr = _qk_rope_bwd(dq_pad, dk_pad, dv, cos, sin, tm=tm, jobs=[scatter("w_o")])
    wq_cols = N_HEADS * HEAD_PAD
    g_wuq = _matmul(cqn, dq_raw, dims=_TN, mnk=(Q_RANK, wq_cols, s), tiles=(Q_RANK, 1024, kt),
                    name="mm_gw_uq", out_dtype=bf16)
    dcqn = _matmul(dq_raw, wuq, dims=_NT, mnk=(s, Q_RANK, wq_cols), tiles=(mt, Q_RANK, wq_cols), name="mm_dcq")
    g_wukv = _matmul(ckvn, dkv, dims=_TN, mnk=(KV_RANK, wq_cols, s), tiles=(KV_RANK, 1024, kt),
                     name="mm_gw_ukv", out_dtype=bf16)
    dckvn = _matmul(dkv, wukv, dims=_NT, mnk=(s, KV_RANK, wq_cols), tiles=(mt, KV_RANK, wq_cols), name="mm_dckv",
                    jobs=[exchange("w_uq", _cols_to_shards(_unpad_w_uq(g_wuq))),
                          exchange("w_ukv", _cols_to_shards(g_wukv))])
    pair_sum("w_uq")
    pair_sum("w_ukv")
    dcq, dckv, dg_q_norm, dg_kv_norm = _latent_norms_bwd(proj, q_norm_g, kv_norm_g, dcqn, dckvn, tm=mt,
                                                         jobs=[diag("w_uq"), diag("w_ukv")])
    fold("w_uq")
    fold("w_ukv")
    dproj = jnp.concatenate([du, dgb, dgc, dcq, dkr, jnp.zeros((s, COL_CKV - COL_KR - 128), bf16), dckv], axis=1)
    g_win = _matmul(h1, dproj, dims=_TN, mnk=(d, IN_PAD, s), tiles=(1024, IN_PAD // 3, kt), name="mm_gw_in",
                    out_dtype=bf16, jobs=[scatter("w_uq"), scatter("w_ukv"), share("w_up"), share("w_o")])
    _comm("pair_exchange_w_in", [[exchange("w_in", _unpad_w_in(g_win))]])
    pair_sum("w_in")
    _comm("scatter_diag_w_in", [[diag("w_in")]])
    fold("w_in")
    dh1 = _matmul(dproj, win, dims=_NT, mnk=(s, d, IN_PAD), tiles=(mt, 1024, IN_PAD // 2), name="mm_dh1",
                  jobs=[scatter("w_in"), share("w_uq"), share("w_ukv")])
    grad_x, dg_pre_mix = _input_bwd(dx2, dh1, xs, pre_mix_g, tm=tm)
    _comm("pair_share_w_in", [[share("w_in")]])

    moments = dict(w_in=(m_w_in, v_w_in), w_uq=(m_w_uq, v_w_uq), w_ukv=(m_w_ukv, v_w_ukv), w_o=(m_w_o, v_w_o),
                   w_up=(m_w_up, v_w_up), w_down=(m_w_down, v_w_down))
    gw, dw, nm, nv = {}, {}, {}, {}
    for k in names:
        delta_k, nm_k, nv_k, g = _adamw(big[k], whole[k].arr, moments[k][0][0], moments[k][1][0], name="adamw_" + k)
        gw[k], dw[k], nm[k], nv[k] = g[None], delta_k[None], nm_k[None], nv_k[None]

    small_g = _all_reduce_small(_pack_small(d, dg_pre_mix, dg_post_mix, dg_pre_mlp, dg_post_mlp, dg_conv_out,
                                            dg_attn_out, dg_q_norm, dg_kv_norm, dg_conv_w
                                            ).at[SMALL_ROWS - 1, :128].set(loss_part[0]))
    loss = small_g[SMALL_ROWS - 1, 0]
    pack_w = lambda cw, pre_mix, post_mix, pre_mlp, post_mlp, conv_out, attn_out, q_norm, kv_norm: _pack_small(
        d, pre_mix, post_mix, pre_mlp, post_mlp, conv_out, attn_out, q_norm, kv_norm, cw)
    small_w = pack_w(conv_w_full, pre_mix_g, post_mix_g, pre_mlp_g, post_mlp_g, conv_out_g, attn_out_g, q_norm_g,
                     kv_norm_g)
    small_m = pack_w(spread(m_conv_w), m_pre_mix_g, m_post_mix_g, m_pre_mlp_g, m_post_mlp_g, m_conv_out_g,
                     m_attn_out_g, m_q_norm_g, m_kv_norm_g)
    small_v = pack_w(spread(v_conv_w), v_pre_mix_g, v_post_mix_g, v_pre_mlp_g, v_post_mlp_g, v_conv_out_g,
                     v_attn_out_g, v_q_norm_g, v_kv_norm_g)
    small_d, small_nm, small_nv, small_g = _adamw(small_w, small_g, small_m, small_v, name="adamw_small")
    sg, sd, snm, snv = (_unpack_small(p, chip) for p in (small_g, small_d, small_nm, small_nv))

    for src, dst in ((sg, gw), (sd, dw), (snm, nm), (snv, nv)):
        dst.update(src)

    order = ["pre_mix_g", "w_in", "conv_w", "q_norm_g", "w_uq", "kv_norm_g", "w_ukv", "conv_out_g", "attn_out_g",
             "w_o", "post_mix_g", "pre_mlp_g", "w_up", "w_down", "post_mlp_g"]
    return (loss, grad_x.reshape(1, s, d), *[gw[k] for k in order], *[dw[k] for k in order],
            *[nm[k] for k in order], *[nv[k] for k in order])
```
